```python
import jax, jax.numpy as jnp
from jax import lax
import numpy as np

D_MODEL = 2048
BATCH = 8
SEQ = 8192
DEPTH = 1

CTX_LEN = 256
GRID_W = 64
HEAD_DIM = 64
N_Q_HEADS = D_MODEL // (2 * HEAD_DIM)
N_KV_HEADS = N_Q_HEADS // 4
ATTN_WIDTH = N_Q_HEADS * HEAD_DIM
KV_WIDTH = N_KV_HEADS * HEAD_DIM
POOL_WINDOWS = (2, 4, 8, 16)
POOL_GROUPS = len(POOL_WINDOWS)
POOL_WIDTH = D_MODEL - ATTN_WIDTH
POOL_GROUP_DIM = POOL_WIDTH // POOL_GROUPS
MIX_WIDTH = ATTN_WIDTH + POOL_WIDTH
IN_WIDTH = ATTN_WIDTH + 2 * KV_WIDTH + POOL_WIDTH
D_FF = 4 * D_MODEL
WINDOW = 128
BLOCK = 128
ROPE_BASE = 10000.0
N_MOD = 6
EPS = 1e-6
NEG_INF = -1e30

kernel_name = "hymba_style_window_gqa_pool_diffusion_block"


def rms_norm(x, w):
    xf = x.astype(jnp.float32)
    y = xf * lax.rsqrt(jnp.mean(xf * xf, axis=-1, keepdims=True) + EPS)
    return (y * w.astype(jnp.float32)).astype(x.dtype)


def ada_modulation(cond, w, b):
    m = jax.nn.silu(cond) @ w + b
    return jnp.split(m[..., None, :], N_MOD, axis=-1)


def modulate(h, shift, scale):
    return h * (1.0 + scale) + shift


def split_projection(p):
    B, L, _ = p.shape
    q, k, v, u = jnp.split(p, [ATTN_WIDTH, ATTN_WIDTH + KV_WIDTH, ATTN_WIDTH + 2 * KV_WIDTH], axis=-1)
    return (q.reshape(B, L, N_Q_HEADS, HEAD_DIM), k.reshape(B, L, N_KV_HEADS, HEAD_DIM),
            v.reshape(B, L, N_KV_HEADS, HEAD_DIM), u)


def axial_positions(seq_len):
    rows = seq_len // GRID_W
    row = jnp.broadcast_to(jnp.arange(rows)[:, None], (rows, GRID_W)).reshape(-1)
    col = jnp.broadcast_to(jnp.arange(GRID_W)[None, :], (rows, GRID_W)).reshape(-1)
    return row, col


def rope_2d(x, row, col):
    half = HEAD_DIM // 2
    inv_freq = ROPE_BASE ** (-jnp.arange(0, half, 2, dtype=jnp.float32) / half)

    def rot(xa, pos):
        ang = pos.astype(jnp.float32)[:, None] * inv_freq[None, :]
        cos = jnp.cos(ang)[None, :, None, :]
        sin = jnp.sin(ang)[None, :, None, :]
        x1, x2 = jnp.split(xa, 2, axis=-1)
        return jnp.concatenate([x1 * cos - x2 * sin, x1 * sin + x2 * cos], axis=-1)

    xf = x.astype(jnp.float32)
    out = jnp.concatenate([rot(xf[..., :half], row), rot(xf[..., half:], col)], axis=-1)
    return out.astype(x.dtype)


def latent_window_attention(q, k, v, k_ctx, v_ctx, sink):
    B, L, H, D = q.shape
    G = H // N_KV_HEADS
    nb = L // BLOCK
    scale = HEAD_DIM ** -0.5
    qb = q.reshape(B, nb, BLOCK, N_KV_HEADS, G, D)
    pad = ((0, 0), (BLOCK, BLOCK), (0, 0), (0, 0))
    kp = jnp.pad(k, pad).reshape(B, nb + 2, BLOCK, N_KV_HEADS, D)
    vp = jnp.pad(v, pad).reshape(B, nb + 2, BLOCK, N_KV_HEADS, D)
    kb = jnp.concatenate([kp[:, :-2], kp[:, 1:-1], kp[:, 2:]], axis=2)
    vb = jnp.concatenate([vp[:, :-2], vp[:, 1:-1], vp[:, 2:]], axis=2)
    s_win = jnp.einsum('bnqhgd,bnkhd->bhgnqk', qb, kb).astype(jnp.float32) * scale
    blk = jnp.arange(nb)[:, None, None] * BLOCK
    qpos = blk + jnp.arange(BLOCK)[None, :, None]
    kpos = blk - BLOCK + jnp.arange(3 * BLOCK)[None, None, :]
    valid = (jnp.abs(qpos - kpos) <= WINDOW) & (kpos >= 0) & (kpos < L)
    s_win = jnp.where(valid, s_win, NEG_INF)
    s_ctx = jnp.einsum('bnqhgd,bchd->bhgnqc', qb, k_ctx).astype(jnp.float32) * scale
    s_sink = jnp.broadcast_to(sink.astype(jnp.float32).reshape(N_KV_HEADS, G)[None, :, :, None, None, None],
                              s_win.shape[:-1] + (1,))
    p = jax.nn.softmax(jnp.concatenate([s_win, s_ctx, s_sink], axis=-1), axis=-1)
    n_win = 3 * BLOCK
    n_ctx = k_ctx.shape[1]
    p_win = p[..., :n_win].astype(v.dtype)
    p_ctx = p[..., n_win:n_win + n_ctx].astype(v.dtype)
    out = (jnp.einsum('bhgnqk,bnkhd->bnqhgd', p_win, vb)
           + jnp.einsum('bhgnqc,bchd->bnqhgd', p_ctx, v_ctx))
    return out.reshape(B, L, H * D)


def context_attention(q, k, v, sink):
    B, C, H, D = q.shape
    G = H // N_KV_HEADS
    qg = q.reshape(B, C, N_KV_HEADS, G, D)
    s = jnp.einsum('bqhgd,bkhd->bhgqk', qg, k).astype(jnp.float32) * (HEAD_DIM ** -0.5)
    s_sink = jnp.broadcast_to(sink.astype(jnp.float32).reshape(N_KV_HEADS, G)[None, :, :, None, None],
                              s.shape[:-1] + (1,))
    p = jax.nn.softmax(jnp.concatenate([s, s_sink], axis=-1), axis=-1)[..., :C].astype(v.dtype)
    out = jnp.einsum('bhgqk,bkhd->bqhgd', p, v)
    return out.reshape(B, C, H * D)


def multiscale_pool(u, pool_w, pool_scale):
    B, L, _ = u.shape
    uf = u.astype(jnp.float32)
    csum = jnp.pad(jnp.cumsum(uf, axis=1), ((0, 0), (1, 0), (0, 0)))
    t = jnp.arange(L)
    outs = []
    for g, w in enumerate(POOL_WINDOWS):
        lo = jnp.clip(t - w // 2, 0, L)
        hi = jnp.clip(t - w // 2 + w, 0, L)
        cs = csum[..., g * POOL_GROUP_DIM:(g + 1) * POOL_GROUP_DIM]
        mean = (cs[:, hi] - cs[:, lo]) / (hi - lo).astype(jnp.float32)[None, :, None]
        outs.append(mean - uf[..., g * POOL_GROUP_DIM:(g + 1) * POOL_GROUP_DIM])
    pooled = jnp.stack(outs, axis=2)
    mixed = jnp.einsum('blgc,gcd->blgd', pooled, pool_w.astype(jnp.float32)).reshape(B, L, POOL_WIDTH)
    return (mixed * pool_scale.astype(jnp.float32)).astype(u.dtype)


def squared_relu_mlp(h, w_up, w_down):
    return jnp.square(jax.nn.relu(h @ w_up)) @ w_down


def _fwd_setup_inputs(seed: int = 0) -> dict:
    key = jax.random.key(seed)
    ks = jax.random.split(key, 18)
    f32 = jnp.float32
    nrm = lambda k, shape, s: jax.random.normal(k, shape, f32) * s
    return {
        "x": nrm(ks[0], (BATCH, SEQ, D_MODEL), 1.0),
        "c": nrm(ks[1], (BATCH, D_MODEL), 1.0),
        "ctx": nrm(ks[2], (BATCH, CTX_LEN, D_MODEL), 1.0),
        "c_ctx": nrm(ks[3], (D_MODEL,), 1.0),
        "norm_attn_w": 1.0 + nrm(ks[4], (DEPTH, D_MODEL), 0.02),
        "norm_mlp_w": 1.0 + nrm(ks[5], (DEPTH, D_MODEL), 0.02),
        "w_ada": nrm(ks[6], (DEPTH, D_MODEL, N_MOD * D_MODEL), 0.5 * D_MODEL ** -0.5),
        "b_ada": nrm(ks[7], (DEPTH, N_MOD * D_MODEL), 0.02),
        "w_in": nrm(ks[8], (DEPTH, D_MODEL, IN_WIDTH), D_MODEL ** -0.5),
        "attn_sink": nrm(ks[9], (DEPTH, N_Q_HEADS), 1.0),
        "pool_w": nrm(ks[10], (DEPTH, POOL_GROUPS, POOL_GROUP_DIM, POOL_GROUP_DIM), POOL_GROUP_DIM ** -0.5),
        "pool_scale": 1.0 + nrm(ks[11], (DEPTH, POOL_WIDTH), 0.1),
        "w_out": nrm(ks[12], (DEPTH, MIX_WIDTH, D_MODEL), MIX_WIDTH ** -0.5),
        "w_mlp_up": nrm(ks[13], (DEPTH, D_MODEL, D_FF), D_MODEL ** -0.5),
        "w_mlp_down": nrm(ks[14], (DEPTH, D_FF, D_MODEL), D_FF ** -0.5),
        "final_norm_w": 1.0 + nrm(ks[15], (D_MODEL,), 0.02),
    }


def _fwd_reference(x, c, ctx, c_ctx, norm_attn_w, norm_mlp_w, w_ada, b_ada, w_in, attn_sink,
              pool_w, pool_scale, w_out, w_mlp_up, w_mlp_down, final_norm_w):
    seq_len = x.shape[1]
    row, col = axial_positions(seq_len)
    for layer in range(DEPTH):
        sh_a, sc_a, g_a, sh_m, sc_m, g_m = ada_modulation(c, w_ada[layer], b_ada[layer])
        csh_a, csc_a, cg_a, csh_m, csc_m, cg_m = ada_modulation(c_ctx, w_ada[layer], b_ada[layer])

        h = modulate(rms_norm(x, norm_attn_w[layer]), sh_a, sc_a)
        hc = modulate(rms_norm(ctx, norm_attn_w[layer]), csh_a, csc_a)
        q, k, v, u = split_projection(h @ w_in[layer])
        qc, kc, vc, uc = split_projection(hc @ w_in[layer])
        q = rope_2d(q, row, col)
        k = rope_2d(k, row, col)
        attn = latent_window_attention(q, k, v, kc, vc, attn_sink[layer])
        pool = multiscale_pool(u, pool_w[layer], pool_scale[layer])
        x = x + g_a * (jnp.concatenate([attn, pool], axis=-1) @ w_out[layer])

        hm = modulate(rms_norm(x, norm_mlp_w[layer]), sh_m, sc_m)
        x = x + g_m * squared_relu_mlp(hm, w_mlp_up[layer], w_mlp_down[layer])

        if layer < DEPTH - 1:
            attn_c = context_attention(qc, kc, vc, attn_sink[layer])
            pool_c = multiscale_pool(uc, pool_w[layer], pool_scale[layer])
            ctx = ctx + cg_a * (jnp.concatenate([attn_c, pool_c], axis=-1) @ w_out[layer])
            hcm = modulate(rms_norm(ctx, norm_mlp_w[layer]), csh_m, csc_m)
            ctx = ctx + cg_m * squared_relu_mlp(hcm, w_mlp_up[layer], w_mlp_down[layer])
    return rms_norm(x, final_norm_w)


import jax as _jax
import jax.numpy as _jnp

TWIN_FORMAT = 'train_step'
FWD_PARAMS = ['x', 'c', 'ctx', 'c_ctx', 'norm_attn_w', 'norm_mlp_w', 'w_ada', 'b_ada', 'w_in', 'attn_sink', 'pool_w', 'pool_scale', 'w_out', 'w_mlp_up', 'w_mlp_down', 'final_norm_w']
TWIN_WEIGHTS = ['c_ctx', 'norm_attn_w', 'norm_mlp_w', 'w_ada', 'b_ada', 'w_in', 'attn_sink', 'pool_w', 'pool_scale', 'w_out', 'w_mlp_up', 'w_mlp_down', 'final_norm_w']
TWIN_DIFF_INPUT = 'x'
TWIN_INPUTS = ['x', 'c', 'ctx', 'c_ctx', 'norm_attn_w', 'norm_mlp_w', 'w_ada', 'b_ada', 'w_in', 'attn_sink', 'pool_w', 'pool_scale', 'w_out', 'w_mlp_up', 'w_mlp_down', 'final_norm_w', 'loss_target', 'm_c_ctx', 'm_norm_attn_w', 'm_norm_mlp_w', 'm_w_ada', 'm_b_ada', 'm_w_in', 'm_attn_sink', 'm_pool_w', 'm_pool_scale', 'm_w_out', 'm_w_mlp_up', 'm_w_mlp_down', 'm_final_norm_w', 'v_c_ctx', 'v_norm_attn_w', 'v_norm_mlp_w', 'v_w_ada', 'v_b_ada', 'v_w_in', 'v_attn_sink', 'v_pool_w', 'v_pool_scale', 'v_w_out', 'v_w_mlp_up', 'v_w_mlp_down', 'v_final_norm_w']
TWIN_OUTPUTS = ['loss', 'grad_x', 'grad_c_ctx', 'grad_norm_attn_w', 'grad_norm_mlp_w', 'grad_w_ada', 'grad_b_ada', 'grad_w_in', 'grad_attn_sink', 'grad_pool_w', 'grad_pool_scale', 'grad_w_out', 'grad_w_mlp_up', 'grad_w_mlp_down', 'grad_final_norm_w', 'delta_c_ctx', 'delta_norm_attn_w', 'delta_norm_mlp_w', 'delta_w_ada', 'delta_b_ada', 'delta_w_in', 'delta_attn_sink', 'delta_pool_w', 'delta_pool_scale', 'delta_w_out', 'delta_w_mlp_up', 'delta_w_mlp_down', 'delta_final_norm_w', 'new_m_c_ctx', 'new_m_norm_attn_w', 'new_m_norm_mlp_w', 'new_m_w_ada', 'new_m_b_ada', 'new_m_w_in', 'new_m_attn_sink', 'new_m_pool_w', 'new_m_pool_scale', 'new_m_w_out', 'new_m_w_mlp_up', 'new_m_w_mlp_down', 'new_m_final_norm_w', 'new_v_c_ctx', 'new_v_norm_attn_w', 'new_v_norm_mlp_w', 'new_v_w_ada', 'new_v_b_ada', 'new_v_w_in', 'new_v_attn_sink', 'new_v_pool_w', 'new_v_pool_scale', 'new_v_w_out', 'new_v_w_mlp_up', 'new_v_w_mlp_down', 'new_v_final_norm_w']
TWIN_LEAF_KINDS = {'loss': 'loss', 'grad_x': 'grad_x', 'grad_c_ctx': 'grad_w', 'grad_norm_attn_w': 'grad_w', 'grad_norm_mlp_w': 'grad_w', 'grad_w_ada': 'grad_w', 'grad_b_ada': 'grad_w', 'grad_w_in': 'grad_w', 'grad_attn_sink': 'grad_w', 'grad_pool_w': 'grad_w', 'grad_pool_scale': 'grad_w', 'grad_w_out': 'grad_w', 'grad_w_mlp_up': 'grad_w', 'grad_w_mlp_down': 'grad_w', 'grad_final_norm_w': 'grad_w', 'delta_c_ctx': 'delta_w', 'delta_norm_attn_w': 'delta_w', 'delta_norm_mlp_w': 'delta_w', 'delta_w_ada': 'delta_w', 'delta_b_ada': 'delta_w', 'delta_w_in': 'delta_w', 'delta_attn_sink': 'delta_w', 'delta_pool_w': 'delta_w', 'delta_pool_scale': 'delta_w', 'delta_w_out': 'delta_w', 'delta_w_mlp_up': 'delta_w', 'delta_w_mlp_down': 'delta_w', 'delta_final_norm_w': 'delta_w', 'new_m_c_ctx': 'new_m', 'new_m_norm_attn_w': 'new_m', 'new_m_norm_mlp_w': 'new_m', 'new_m_w_ada': 'new_m', 'new_m_b_ada': 'new_m', 'new_m_w_in': 'new_m', 'new_m_attn_sink': 'new_m', 'new_m_pool_w': 'new_m', 'new_m_pool_scale': 'new_m', 'new_m_w_out': 'new_m', 'new_m_w_mlp_up': 'new_m', 'new_m_w_mlp_down': 'new_m', 'new_m_final_norm_w': 'new_m', 'new_v_c_ctx': 'new_v', 'new_v_norm_attn_w': 'new_v', 'new_v_norm_mlp_w': 'new_v', 'new_v_w_ada': 'new_v', 'new_v_b_ada': 'new_v', 'new_v_w_in': 'new_v', 'new_v_attn_sink': 'new_v', 'new_v_pool_w': 'new_v', 'new_v_pool_scale': 'new_v', 'new_v_w_out': 'new_v', 'new_v_w_mlp_up': 'new_v', 'new_v_w_mlp_down': 'new_v', 'new_v_final_norm_w': 'new_v'}


def _forward(args):
    return _fwd_reference(*[args[k] for k in FWD_PARAMS])


def _output_shape():
    def fwd():
        inp = _fwd_setup_inputs(0)
        return _fwd_reference(*[inp[k] for k in FWD_PARAMS])
    out = _jax.eval_shape(fwd)
    return out.shape, out.dtype

N_MICROBATCH = 1
ADAM_LR = 0.001
ADAM_B1 = 0.9
ADAM_B2 = 0.999
ADAM_EPS = 1e-08
ADAM_WD = 0.01
ADAM_STEP = 10
PER_EXAMPLE_BATCH_AXIS = {'x': 0, 'c': 0, 'ctx': 0, 'loss_target': 0}
SHARED_INPUTS = []
_WEIGHT_DTYPES = {'c_ctx': _jnp.float32, 'norm_attn_w': _jnp.float32, 'norm_mlp_w': _jnp.float32, 'w_ada': _jnp.float32, 'b_ada': _jnp.float32, 'w_in': _jnp.float32, 'attn_sink': _jnp.float32, 'pool_w': _jnp.float32, 'pool_scale': _jnp.float32, 'w_out': _jnp.float32, 'w_mlp_up': _jnp.float32, 'w_mlp_down': _jnp.float32, 'final_norm_w': _jnp.float32}
MOMENT_SCALE = {'c_ctx': 5.100706e-03, 'norm_attn_w': 2.466672e-02, 'norm_mlp_w': 5.449185e-02, 'w_ada': 5.973462e-02, 'b_ada': 1.066592e-01, 'w_in': 2.305925e-02, 'attn_sink': 2.999971e-04, 'pool_w': 3.481309e-02, 'pool_scale': 3.584966e-02, 'w_out': 2.557691e-02, 'w_mlp_up': 2.868947e-02, 'w_mlp_down': 5.327061e-02, 'final_norm_w': 3.205820e+01}


def _to_microbatches(a, axis):
    t = _jnp.moveaxis(a, axis, 0)
    t = t.reshape((N_MICROBATCH, t.shape[0] // N_MICROBATCH) + t.shape[1:])
    return _jnp.moveaxis(t, 1, axis + 1)


def setup_inputs(seed: int = 0) -> dict:
    inp = _fwd_setup_inputs(seed)
    key = _jax.random.fold_in(_jax.random.key(seed), 7919)
    shape, _ = _output_shape()
    out = dict(inp)
    out["loss_target"] = _jax.random.normal(_jax.random.fold_in(key, 0), shape, _jnp.float32)
    for i, name in enumerate(TWIN_WEIGHTS):
        w = inp[name].astype(_jnp.float32)
        if MOMENT_SCALE is None:
            s = _jnp.sqrt(_jnp.mean(_jnp.square(w)) + 1e-30)
        else:
            s = MOMENT_SCALE[name]
        km, kv = _jax.random.split(_jax.random.fold_in(key, i + 1))
        out[name] = w
        out["m_" + name] = s * _jax.random.normal(km, w.shape, _jnp.float32)
        out["v_" + name] = (s * s) * _jax.random.uniform(kv, w.shape, _jnp.float32, 0.5, 1.5)
    if N_MICROBATCH > 1:
        for name, axis in PER_EXAMPLE_BATCH_AXIS.items():
            out[name] = _to_microbatches(out[name], axis)
    return {'x': out['x'], 'c': out['c'], 'ctx': out['ctx'], 'c_ctx': out['c_ctx'], 'norm_attn_w': out['norm_attn_w'], 'norm_mlp_w': out['norm_mlp_w'], 'w_ada': out['w_ada'], 'b_ada': out['b_ada'], 'w_in': out['w_in'], 'attn_sink': out['attn_sink'], 'pool_w': out['pool_w'], 'pool_scale': out['pool_scale'], 'w_out': out['w_out'], 'w_mlp_up': out['w_mlp_up'], 'w_mlp_down': out['w_mlp_down'], 'final_norm_w': out['final_norm_w'], 'loss_target': out['loss_target'], 'm_c_ctx': out['m_c_ctx'], 'm_norm_attn_w': out['m_norm_attn_w'], 'm_norm_mlp_w': out['m_norm_mlp_w'], 'm_w_ada': out['m_w_ada'], 'm_b_ada': out['m_b_ada'], 'm_w_in': out['m_w_in'], 'm_attn_sink': out['m_attn_sink'], 'm_pool_w': out['m_pool_w'], 'm_pool_scale': out['m_pool_scale'], 'm_w_out': out['m_w_out'], 'm_w_mlp_up': out['m_w_mlp_up'], 'm_w_mlp_down': out['m_w_mlp_down'], 'm_final_norm_w': out['m_final_norm_w'], 'v_c_ctx': out['v_c_ctx'], 'v_norm_attn_w': out['v_norm_attn_w'], 'v_norm_mlp_w': out['v_norm_mlp_w'], 'v_w_ada': out['v_w_ada'], 'v_b_ada': out['v_b_ada'], 'v_w_in': out['v_w_in'], 'v_attn_sink': out['v_attn_sink'], 'v_pool_w': out['v_pool_w'], 'v_pool_scale': out['v_pool_scale'], 'v_w_out': out['v_w_out'], 'v_w_mlp_up': out['v_w_mlp_up'], 'v_w_mlp_down': out['v_w_mlp_down'], 'v_final_norm_w': out['v_final_norm_w']}


def _loss(weights, diff, rest, loss_target):
    with _jax.named_scope("forward"):
        args = {**rest, TWIN_DIFF_INPUT: diff, **{k: w.astype(_WEIGHT_DTYPES[k]) for k, w in weights.items()}}
        y = _forward(args)
    with _jax.named_scope("loss_head"):
        err = _jnp.square(y.astype(_jnp.float32) - loss_target)
        return 0.5 * _jnp.sum(_jnp.mean(err, axis=-1)) if err.ndim else 0.5 * err


def _adamw(w, g, m, v):
    m = ADAM_B1 * m + (1.0 - ADAM_B1) * g
    v = ADAM_B2 * v + (1.0 - ADAM_B2) * _jnp.square(g)
    m_hat = m / (1.0 - ADAM_B1 ** ADAM_STEP)
    v_hat = v / (1.0 - ADAM_B2 ** ADAM_STEP)
    delta = -ADAM_LR * (m_hat / (_jnp.sqrt(v_hat) + ADAM_EPS) + ADAM_WD * w)
    return delta, m, v


def reference(x, c, ctx, c_ctx, norm_attn_w, norm_mlp_w, w_ada, b_ada, w_in, attn_sink, pool_w, pool_scale, w_out, w_mlp_up, w_mlp_down, final_norm_w, loss_target, m_c_ctx, m_norm_attn_w, m_norm_mlp_w, m_w_ada, m_b_ada, m_w_in, m_attn_sink, m_pool_w, m_pool_scale, m_w_out, m_w_mlp_up, m_w_mlp_down, m_final_norm_w, v_c_ctx, v_norm_attn_w, v_norm_mlp_w, v_w_ada, v_b_ada, v_w_in, v_attn_sink, v_pool_w, v_pool_scale, v_w_out, v_w_mlp_up, v_w_mlp_down, v_final_norm_w):
    given = dict(x=x, c=c, ctx=ctx, c_ctx=c_ctx, norm_attn_w=norm_attn_w, norm_mlp_w=norm_mlp_w, w_ada=w_ada, b_ada=b_ada, w_in=w_in, attn_sink=attn_sink, pool_w=pool_w, pool_scale=pool_scale, w_out=w_out, w_mlp_up=w_mlp_up, w_mlp_down=w_mlp_down, final_norm_w=final_norm_w, loss_target=loss_target, m_c_ctx=m_c_ctx, m_norm_attn_w=m_norm_attn_w, m_norm_mlp_w=m_norm_mlp_w, m_w_ada=m_w_ada, m_b_ada=m_b_ada, m_w_in=m_w_in, m_attn_sink=m_attn_sink, m_pool_w=m_pool_w, m_pool_scale=m_pool_scale, m_w_out=m_w_out, m_w_mlp_up=m_w_mlp_up, m_w_mlp_down=m_w_mlp_down, m_final_norm_w=m_final_norm_w, v_c_ctx=v_c_ctx, v_norm_attn_w=v_norm_attn_w, v_norm_mlp_w=v_norm_mlp_w, v_w_ada=v_w_ada, v_b_ada=v_b_ada, v_w_in=v_w_in, v_attn_sink=v_attn_sink, v_pool_w=v_pool_w, v_pool_scale=v_pool_scale, v_w_out=v_w_out, v_w_mlp_up=v_w_mlp_up, v_w_mlp_down=v_w_mlp_down, v_final_norm_w=v_final_norm_w)
    weights = {n: given[n] for n in TWIN_WEIGHTS}
    shared = {n: given[n] for n in SHARED_INPUTS}
    per_example = {n: given[n] for n in ['x', 'c', 'ctx']}
    grad_fn = _jax.value_and_grad(_loss, argnums=(0, 1))

    def one_microbatch(ex, loss_target):
        ex = dict(ex)
        diff = ex.pop(TWIN_DIFF_INPUT)
        return grad_fn(weights, diff, {**shared, **ex}, loss_target)

    if N_MICROBATCH == 1:
        loss, (grad_w, grad_x) = one_microbatch(per_example, given["loss_target"])
    else:
        def body(carry, xs):
            loss_sum, grad_sum = carry
            l_k, (gw_k, gx_k) = one_microbatch(xs[0], xs[1])
            with _jax.named_scope("update"):
                return (loss_sum + l_k, _jax.tree.map(_jnp.add, grad_sum, gw_k)), gx_k

        init = (_jnp.zeros((), _jnp.float32), _jax.tree.map(_jnp.zeros_like, weights))
        (loss, grad_w), grad_x = _jax.lax.scan(body, init, (per_example, given["loss_target"]))
    with _jax.named_scope("update"):
        delta_w, new_m, new_v = {}, {}, {}
        for n in TWIN_WEIGHTS:
            delta_w[n], new_m[n], new_v[n] = _adamw(weights[n], grad_w[n], given["m_" + n], given["v_" + n])
    return (loss, grad_x, *[grad_w[n] for n in TWIN_WEIGHTS], *[delta_w[n] for n in TWIN_WEIGHTS],
            *[new_m[n] for n in TWIN_WEIGHTS], *[new_v[n] for n in TWIN_WEIGHTS])
```

```python
import functools

import jax
import jax.numpy as jnp
from jax import lax
from jax.experimental import pallas as pl
from jax.experimental.pallas import tpu as pltpu

F32 = jnp.float32
BF16 = jnp.bfloat16
I32 = jnp.int32

HEAD_DIM = 64
GQA = 4
BLOCK = 128
GRID_W = 64
ROPE_BASE = 10000.0
POOL_WINDOWS = (2, 4, 8, 16)
POOL_GROUPS = len(POOL_WINDOWS)
HALO = 8
N_MOD = 6
EPS = 1e-6
NEG_INF = -1e30
ADAM_LR = 0.001
ADAM_B1 = 0.9
ADAM_B2 = 0.999
ADAM_EPS = 1e-08
ADAM_WD = 0.01
ADAM_STEP = 10
N_DEV = 8
LANES = 128
SUBLANES_16BIT = 16
VMEM_LIMIT = 48 * 1024 * 1024
MESH = pl.DeviceIdType.MESH
HBM = pl.BlockSpec(memory_space=pltpu.HBM)


def _cparams(*sem):
    return pltpu.CompilerParams(dimension_semantics=sem, vmem_limit_bytes=VMEM_LIMIT)


def _tile(n, pref, align):
    if n <= pref:
        return n
    t = (pref // align) * align
    while t >= align:
        if n % t == 0:
            return t
        t -= align
    return n


def _dot(a, b):
    return lax.dot_general(a, b, (((1,), (0,)), ((), ())), preferred_element_type=F32)


def _dot_nt(a, b):
    return lax.dot_general(a, b, (((1,), (1,)), ((), ())), preferred_element_type=F32)


def _dot_tn(a, b):
    return lax.dot_general(a, b, (((0,), (0,)), ((), ())), preferred_element_type=F32)


_DOTS = {"nn": _dot, "nt": _dot_nt, "tn": _dot_tn}


def _mm(name, a, b, mode, out_dtypes, *, epilogue=None, extras=(), a_pre=None, b_pre=None, tm=512, tn=1024, tk=512):
    if mode == "nn":
        (M, K), (K2, N) = a.shape, b.shape
    elif mode == "nt":
        (M, K), (N, K2) = a.shape, b.shape
    else:
        (K, M), (K2, N) = a.shape, b.shape
    assert K == K2, (name, a.shape, b.shape)
    tm = _tile(M, tm, LANES if mode == "tn" else SUBLANES_16BIT)
    tn = _tile(N, tn, LANES)
    tk = _tile(K, tk, SUBLANES_16BIT if mode == "tn" else LANES)
    nk = K // tk
    n_ex, n_out = len(extras), len(out_dtypes)

    def body(a_ref, b_ref, *rest):
        ex_refs, out_refs, acc_ref = rest[:n_ex], rest[n_ex:n_ex + n_out], rest[-1]
        k = pl.program_id(2)

        @pl.when(k == 0)
        def _():
            acc_ref[...] = jnp.zeros_like(acc_ref)

        at, bt = a_ref[...], b_ref[...]
        if a_pre is not None:
            at = a_pre(at)
        if b_pre is not None:
            bt = b_pre(bt)
        acc_ref[...] += _DOTS[mode](at.astype(BF16), bt.astype(BF16))

        @pl.when(k == nk - 1)
        def _():
            acc = acc_ref[...]
            outs = (acc,) if epilogue is None else epilogue(acc, *[r[...] for r in ex_refs])
            for o_ref, o in zip(out_refs, outs):
                o_ref[...] = o.astype(o_ref.dtype)

    a_spec = pl.BlockSpec((tk, tm), lambda i, j, k: (k, i)) if mode == "tn" else pl.BlockSpec((tm, tk), lambda i, j, k: (i, k))
    b_spec = pl.BlockSpec((tn, tk), lambda i, j, k: (j, k)) if mode == "nt" else pl.BlockSpec((tk, tn), lambda i, j, k: (k, j))
    ex_specs = []
    for kind, arr in extras:
        if kind == "mn":
            ex_specs.append(pl.BlockSpec((tm, tn), lambda i, j, k: (i, j)))
        elif kind == "n":
            ex_specs.append(pl.BlockSpec((1, tn), lambda i, j, k: (0, j)))
        else:
            ex_specs.append(pl.BlockSpec((tm, arr.shape[1]), lambda i, j, k: (i, 0)))
    return pl.pallas_call(
        body,
        name=name,
        grid=(M // tm, N // tn, nk),
        in_specs=[a_spec, b_spec] + ex_specs,
        out_specs=[pl.BlockSpec((tm, tn), lambda i, j, k: (i, j)) for _ in out_dtypes],
        out_shape=[jax.ShapeDtypeStruct((M, N), d) for d in out_dtypes],
        scratch_shapes=[pltpu.VMEM((tm, tn), F32)],
        compiler_params=_cparams("parallel", "parallel", "arbitrary"),
    )(a, b, *[arr for _, arr in extras])


def _silu(v):
    return v / (1.0 + jnp.exp(-v))


def _relu2(v):
    r = jnp.maximum(v, 0.0)
    return r * r


def _rope_tables(L):
    half = HEAD_DIM // 2
    inv_freq = ROPE_BASE ** (-jnp.arange(0, half, 2, dtype=F32) / half)
    t = jnp.arange(L)
    row, col = t // GRID_W, t % GRID_W
    ang_r = row.astype(F32)[:, None] * inv_freq[None, :]
    ang_c = col.astype(F32)[:, None] * inv_freq[None, :]
    cos = jnp.concatenate([jnp.cos(ang_r), jnp.cos(ang_r), jnp.cos(ang_c), jnp.cos(ang_c)], axis=1)
    sin = jnp.concatenate([-jnp.sin(ang_r), jnp.sin(ang_r), -jnp.sin(ang_c), jnp.sin(ang_c)], axis=1)
    reps = LANES // HEAD_DIM
    return jnp.tile(cos, (1, reps)), jnp.tile(sin, (1, reps))


def _rope(xf, cos, sin):
    quarter = HEAD_DIM // 4
    lane = lax.broadcasted_iota(I32, (xf.shape[0], LANES), 1)
    first = (lane & quarter) == 0
    outs = []
    for j in range(xf.shape[1] // LANES):
        xc = xf[:, j * LANES:(j + 1) * LANES]
        partner = jnp.where(first, pltpu.roll(xc, LANES - quarter, 1), pltpu.roll(xc, quarter, 1))
        outs.append(xc * cos + partner * sin)
    return outs[0] if len(outs) == 1 else jnp.concatenate(outs, axis=1)


def _norm_fwd(name, x, w, sc, sh):
    L, D = x.shape
    T = _tile(L, 256, 8)

    def body(x_ref, w_ref, sc_ref, sh_ref, h_ref, r_ref):
        xf = x_ref[...]
        r = lax.rsqrt(jnp.mean(xf * xf, axis=-1, keepdims=True) + EPS)
        n = (xf * r) * w_ref[...]
        h_ref[...] = (n * (1.0 + sc_ref[...]) + sh_ref[...]).astype(BF16)
        r_ref[...] = r

    row = pl.BlockSpec((1, D), lambda i: (0, 0))
    return pl.pallas_call(
        body, name=name, grid=(L // T,),
        in_specs=[pl.BlockSpec((T, D), lambda i: (i, 0)), row, row, row],
        out_specs=[pl.BlockSpec((T, D), lambda i: (i, 0)), pl.BlockSpec((T, 1), lambda i: (i, 0))],
        out_shape=[jax.ShapeDtypeStruct((L, D), BF16), jax.ShapeDtypeStruct((L, 1), F32)],
        compiler_params=_cparams("parallel"),
    )(x, w, sc, sh)


def _norm_bwd(name, x, r, dh, dres, w, sc, w_init, gate=None):
    L, D = x.shape
    T = _tile(L, 256, 8)
    with_gate = gate is not None

    def body(x_ref, r_ref, dh_ref, dres_ref, w_ref, sc_ref, wi_ref, *rest):
        if with_gate:
            o_ref, g_ref, dx_ref, ssh_ref, ssc_ref, sw_ref, sg_ref, do_ref = rest
        else:
            dx_ref, ssh_ref, ssc_ref, sw_ref = rest
        i = pl.program_id(0)

        @pl.when(i == 0)
        def _():
            ssh_ref[...] = jnp.zeros_like(ssh_ref)
            ssc_ref[...] = jnp.zeros_like(ssc_ref)
            sw_ref[...] = wi_ref[...]
            if with_gate:
                sg_ref[...] = jnp.zeros_like(sg_ref)

        rr = r_ref[...]
        xh = x_ref[...] * rr
        dh = dh_ref[...]
        wv = w_ref[...]
        dn = dh * (1.0 + sc_ref[...])
        ssh_ref[...] += jnp.sum(dh, axis=0, keepdims=True)
        ssc_ref[...] += jnp.sum(dh * (xh * wv), axis=0, keepdims=True)
        sw_ref[...] += jnp.sum(dn * xh, axis=0, keepdims=True)
        dxh = dn * wv
        dx = dres_ref[...] + rr * (dxh - xh * jnp.mean(dxh * xh, axis=-1, keepdims=True))
        dx_ref[...] = dx
        if with_gate:
            sg_ref[...] += jnp.sum(dx * o_ref[...], axis=0, keepdims=True)
            do_ref[...] = (g_ref[...] * dx).astype(BF16)

    tile = pl.BlockSpec((T, D), lambda i: (i, 0))
    row = pl.BlockSpec((1, D), lambda i: (0, 0))
    in_specs = [tile, pl.BlockSpec((T, 1), lambda i: (i, 0)), tile, tile, row, row, row]
    out_specs = [tile, row, row, row]
    out_shape = [jax.ShapeDtypeStruct((L, D), F32)] + [jax.ShapeDtypeStruct((1, D), F32)] * 3
    args = [x, r, dh, dres, w, sc, w_init]
    if with_gate:
        in_specs += [tile, row]
        out_specs += [row, tile]
        out_shape += [jax.ShapeDtypeStruct((1, D), F32), jax.ShapeDtypeStruct((L, D), BF16)]
        args += list(gate)
    return pl.pallas_call(
        body, name=name, grid=(L // T,), in_specs=in_specs, out_specs=out_specs, out_shape=out_shape,
        compiler_params=_cparams("arbitrary"),
    )(*args)


def _final(x2, tgt, mlp, wf, gm):
    L, D = x2.shape
    T = _tile(L, 256, 8)

    def body(x_ref, t_ref, mlp_ref, wf_ref, gm_ref, dx_ref, dmlp_ref, dwf_ref, dgm_ref, loss_ref):
        i = pl.program_id(0)

        @pl.when(i == 0)
        def _():
            dwf_ref[...] = jnp.zeros_like(dwf_ref)
            dgm_ref[...] = jnp.zeros_like(dgm_ref)
            loss_ref[...] = jnp.zeros_like(loss_ref)

        xf = x_ref[...]
        r = lax.rsqrt(jnp.mean(xf * xf, axis=-1, keepdims=True) + EPS)
        xh = xf * r
        wv = wf_ref[...]
        err = xh * wv - t_ref[...]
        row_loss = jnp.mean(err * err, axis=-1, keepdims=True)
        loss_ref[...] += 0.5 * jnp.sum(row_loss, axis=0, keepdims=True)
        dy = err / D
        dwf_ref[...] += jnp.sum(dy * xh, axis=0, keepdims=True)
        dxh = dy * wv
        dx = r * (dxh - xh * jnp.mean(dxh * xh, axis=-1, keepdims=True))
        dx_ref[...] = dx
        dgm_ref[...] += jnp.sum(dx * mlp_ref[...], axis=0, keepdims=True)
        dmlp_ref[...] = (gm_ref[...] * dx).astype(BF16)

    tile = pl.BlockSpec((T, D), lambda i: (i, 0))
    row = pl.BlockSpec((1, D), lambda i: (0, 0))
    return pl.pallas_call(
        body, name="final_norm_loss", grid=(L // T,),
        in_specs=[tile, tile, tile, row, row],
        out_specs=[tile, tile, row, row, pl.BlockSpec((1, 1), lambda i: (0, 0))],
        out_shape=[jax.ShapeDtypeStruct((L, D), F32), jax.ShapeDtypeStruct((L, D), BF16),
                   jax.ShapeDtypeStruct((1, D), F32), jax.ShapeDtypeStruct((1, D), F32), jax.ShapeDtypeStruct((1, 1), F32)],
        compiler_params=_cparams("arbitrary"),
    )(x2, tgt, mlp, wf, gm)


def _heads(ref, first, n):
    return jnp.concatenate([ref[:, (first + g) * HEAD_DIM:(first + g + 1) * HEAD_DIM] for g in range(n)], axis=0)


def _put_col(tile, h, col):
    lane = lax.broadcasted_iota(I32, tile.shape, 1)
    return jnp.where(lane == h, col, tile)


def _get_col(tile, h):
    lane = lax.broadcasted_iota(I32, tile.shape, 1)
    return jnp.sum(jnp.where(lane == h, tile, 0.0), axis=1, keepdims=True)


def _window_mask(n, L):
    qi = lax.broadcasted_iota(I32, (GQA * BLOCK, 3 * BLOCK), 0) & (BLOCK - 1)
    kj = lax.broadcasted_iota(I32, (GQA * BLOCK, 3 * BLOCK), 1)
    kpos = n * BLOCK - BLOCK + kj
    return (kj >= qi) & (kj <= qi + 2 * BLOCK) & (kpos >= 0) & (kpos < L)


def _attn_specs(L, A, KV, C):
    nb = L // BLOCK
    kcol = A // KV
    q_spec = pl.BlockSpec((BLOCK, A), lambda n: (n, 0))
    k_specs = [pl.BlockSpec((BLOCK, KV), lambda n: (jnp.maximum(n - 1, 0), kcol)),
               pl.BlockSpec((BLOCK, KV), lambda n: (n, kcol)),
               pl.BlockSpec((BLOCK, KV), lambda n: (jnp.minimum(n + 1, nb - 1), kcol))]
    v_specs = [pl.BlockSpec((BLOCK, KV), lambda n: (jnp.maximum(n - 1, 0), 0)),
               pl.BlockSpec((BLOCK, KV), lambda n: (n, 0)),
               pl.BlockSpec((BLOCK, KV), lambda n: (jnp.minimum(n + 1, nb - 1), 0))]
    kvc_spec = pl.BlockSpec((C, 2 * KV), lambda n: (0, 0))
    return q_spec, k_specs, v_specs, kvc_spec


def _attn_fwd(qk, v, kvc, sink, A, KV):
    L = qk.shape[0]
    C = kvc.shape[0]
    nkv = KV // HEAD_DIM
    H = nkv * GQA
    scale = HEAD_DIM ** -0.5

    def body(sink_ref, q_ref, kp_ref, kc_ref, kn_ref, vp_ref, vc_ref, vn_ref, kvc_ref, o_ref, lse_ref):
        n = pl.program_id(0)
        valid = _window_mask(n, L)
        lse_t = jnp.zeros((BLOCK, H), F32)
        for hk in range(nkv):
            sl = slice(hk * HEAD_DIM, (hk + 1) * HEAD_DIM)
            kw = jnp.concatenate([kp_ref[:, sl], kc_ref[:, sl], kn_ref[:, sl]], axis=0)
            vw = jnp.concatenate([vp_ref[:, sl], vc_ref[:, sl], vn_ref[:, sl]], axis=0)
            kx = kvc_ref[:, sl]
            vx = kvc_ref[:, KV + hk * HEAD_DIM:KV + (hk + 1) * HEAD_DIM]
            qs = _heads(q_ref, hk * GQA, GQA)
            s_w = jnp.where(valid, _dot_nt(qs, kw) * scale, NEG_INF)
            s_c = _dot_nt(qs, kx) * scale
            sk = jnp.concatenate([jnp.full((BLOCK, 1), sink_ref[0, hk * GQA + g], F32) for g in range(GQA)], axis=0)
            m = jnp.maximum(jnp.maximum(jnp.max(s_w, axis=-1, keepdims=True), jnp.max(s_c, axis=-1, keepdims=True)), sk)
            p_w = jnp.exp(s_w - m)
            p_c = jnp.exp(s_c - m)
            den = jnp.sum(p_w, axis=-1, keepdims=True) + jnp.sum(p_c, axis=-1, keepdims=True) + jnp.exp(sk - m)
            inv = 1.0 / den
            o = _dot((p_w * inv).astype(BF16), vw) + _dot((p_c * inv).astype(BF16), vx)
            lse = m + jnp.log(den)
            o_ref[:, hk * GQA * HEAD_DIM:(hk + 1) * GQA * HEAD_DIM] = jnp.concatenate(
                [o[g * BLOCK:(g + 1) * BLOCK] for g in range(GQA)], axis=1).astype(BF16)
            for g in range(GQA):
                lse_t = _put_col(lse_t, hk * GQA + g, lse[g * BLOCK:(g + 1) * BLOCK])
        lse_ref[...] = lse_t

    q_spec, k_specs, v_specs, kvc_spec = _attn_specs(L, A, KV, C)
    return pl.pallas_call(
        body, name="attn_fwd", grid=(L // BLOCK,),
        in_specs=[pl.BlockSpec(memory_space=pltpu.SMEM), q_spec] + k_specs + v_specs + [kvc_spec],
        out_specs=[pl.BlockSpec((BLOCK, A), lambda n: (n, 0)), pl.BlockSpec((BLOCK, H), lambda n: (n, 0))],
        out_shape=[jax.ShapeDtypeStruct((L, A), BF16), jax.ShapeDtypeStruct((L, H), F32)],
        compiler_params=_cparams("parallel"),
    )(sink, qk, qk, qk, qk, v, v, v, kvc)


def _attn_bwd_dq(qk, v, kvc, sink, dap, lse, cos, sin, A, KV):
    L = qk.shape[0]
    C = kvc.shape[0]
    nkv = KV // HEAD_DIM
    H = nkv * GQA
    scale = HEAD_DIM ** -0.5

    def body(sink_ref, q_ref, kp_ref, kc_ref, kn_ref, vp_ref, vc_ref, vn_ref, kvc_ref, do_ref, lse_ref, cos_ref, sin_ref,
             dq_ref, rd_ref, ds_ref, dkvc_ref):
        n = pl.program_id(0)

        @pl.when(n == 0)
        def _():
            dkvc_ref[...] = jnp.zeros_like(dkvc_ref)

        valid = _window_mask(n, L)
        lse_t = lse_ref[...]
        rd_t = jnp.zeros((BLOCK, H), F32)
        ds_t = jnp.zeros((BLOCK, H), F32)
        dq_parts = []
        for hk in range(nkv):
            sl = slice(hk * HEAD_DIM, (hk + 1) * HEAD_DIM)
            kw = jnp.concatenate([kp_ref[:, sl], kc_ref[:, sl], kn_ref[:, sl]], axis=0)
            vw = jnp.concatenate([vp_ref[:, sl], vc_ref[:, sl], vn_ref[:, sl]], axis=0)
            kx = kvc_ref[:, sl]
            vx = kvc_ref[:, KV + hk * HEAD_DIM:KV + (hk + 1) * HEAD_DIM]
            qs = _heads(q_ref, hk * GQA, GQA)
            dos = _heads(do_ref, hk * GQA, GQA).astype(BF16)
            lse = jnp.concatenate([_get_col(lse_t, hk * GQA + g) for g in range(GQA)], axis=0)
            sk = jnp.concatenate([jnp.full((BLOCK, 1), sink_ref[0, hk * GQA + g], F32) for g in range(GQA)], axis=0)
            p_w = jnp.exp(jnp.where(valid, _dot_nt(qs, kw) * scale, NEG_INF) - lse)
            p_c = jnp.exp(_dot_nt(qs, kx) * scale - lse)
            p_s = jnp.exp(sk - lse)
            dp_w = _dot_nt(dos, vw)
            dp_c = _dot_nt(dos, vx)
            rd = jnp.sum(p_w * dp_w, axis=-1, keepdims=True) + jnp.sum(p_c * dp_c, axis=-1, keepdims=True)
            ds_w = (p_w * (dp_w - rd) * scale).astype(BF16)
            ds_c = (p_c * (dp_c - rd) * scale).astype(BF16)
            dq = _dot(ds_w, kw) + _dot(ds_c, kx)
            dkvc_ref[:, sl] += _dot_tn(ds_c, qs)
            dkvc_ref[:, KV + hk * HEAD_DIM:KV + (hk + 1) * HEAD_DIM] += _dot_tn(p_c.astype(BF16), dos)
            dsink = -(p_s * rd)
            for g in range(GQA):
                rows = slice(g * BLOCK, (g + 1) * BLOCK)
                rd_t = _put_col(rd_t, hk * GQA + g, rd[rows])
                ds_t = _put_col(ds_t, hk * GQA + g, dsink[rows])
                dq_parts.append(dq[rows])
        rd_ref[...] = rd_t
        ds_ref[...] = ds_t
        dq_ref[...] = _rope(jnp.concatenate(dq_parts, axis=1), cos_ref[...], -sin_ref[...]).astype(BF16)

    q_spec, k_specs, v_specs, kvc_spec = _attn_specs(L, A, KV, C)
    blk = lambda w: pl.BlockSpec((BLOCK, w), lambda n: (n, 0))
    return pl.pallas_call(
        body, name="attn_bwd_dq", grid=(L // BLOCK,),
        in_specs=[pl.BlockSpec(memory_space=pltpu.SMEM), q_spec] + k_specs + v_specs + [kvc_spec, blk(A), blk(H), blk(LANES), blk(LANES)],
        out_specs=[blk(A), blk(H), blk(H), pl.BlockSpec((C, 2 * KV), lambda n: (0, 0))],
        out_shape=[jax.ShapeDtypeStruct((L, A), BF16), jax.ShapeDtypeStruct((L, H), F32), jax.ShapeDtypeStruct((L, H), F32),
                   jax.ShapeDtypeStruct((C, 2 * KV), F32)],
        compiler_params=_cparams("arbitrary"),
    )(sink, qk, qk, qk, qk, v, v, v, kvc, dap, lse, cos, sin)


def _attn_bwd_dkv(qk, v, dap, lse, rd, cos, sin, A, KV):
    L = qk.shape[0]
    nb = L // BLOCK
    nkv = KV // HEAD_DIM
    H = nkv * GQA
    kcol = A // KV
    scale = HEAD_DIM ** -0.5
    R = 3 * GQA * BLOCK

    def body(k_ref, v_ref, qp_ref, qc_ref, qn_ref, dop_ref, doc_ref, don_ref, lsep_ref, lsec_ref, lsen_ref,
             rdp_ref, rdc_ref, rdn_ref, cos_ref, sin_ref, dk_ref, dv_ref):
        m = pl.program_id(0)
        r = lax.broadcasted_iota(I32, (R, BLOCK), 0)
        part = r // (GQA * BLOCK)
        qi = r & (BLOCK - 1)
        kj = lax.broadcasted_iota(I32, (R, BLOCK), 1)
        before = jnp.where(m >= 1, 0, -2 * BLOCK)
        after = jnp.where(m <= nb - 2, 0, 2 * BLOCK)
        valid = ((part == 0) & (kj <= qi + before)) | (part == 1) | ((part == 2) & (kj >= qi + after))
        lse_ts = [lsep_ref[...], lsec_ref[...], lsen_ref[...]]
        rd_ts = [rdp_ref[...], rdc_ref[...], rdn_ref[...]]
        dk_parts, dv_parts = [], []
        for hk in range(nkv):
            sl = slice(hk * HEAD_DIM, (hk + 1) * HEAD_DIM)
            km = k_ref[:, sl]
            vm = v_ref[:, sl]
            qs = jnp.concatenate([_heads(q, hk * GQA, GQA) for q in (qp_ref, qc_ref, qn_ref)], axis=0)
            dos = jnp.concatenate([_heads(d, hk * GQA, GQA) for d in (dop_ref, doc_ref, don_ref)], axis=0).astype(BF16)
            lse = jnp.concatenate([_get_col(t, hk * GQA + g) for t in lse_ts for g in range(GQA)], axis=0)
            rdv = jnp.concatenate([_get_col(t, hk * GQA + g) for t in rd_ts for g in range(GQA)], axis=0)
            p = jnp.exp(jnp.where(valid, _dot_nt(qs, km) * scale, NEG_INF) - lse)
            dp = _dot_nt(dos, vm)
            ds = (p * (dp - rdv) * scale).astype(BF16)
            dk_parts.append(_dot_tn(ds, qs))
            dv_parts.append(_dot_tn(p.astype(BF16), dos))
        dk = dk_parts[0] if nkv == 1 else jnp.concatenate(dk_parts, axis=1)
        dv = dv_parts[0] if nkv == 1 else jnp.concatenate(dv_parts, axis=1)
        dk_ref[...] = _rope(dk, cos_ref[...], -sin_ref[...]).astype(BF16)
        dv_ref[...] = dv.astype(BF16)

    prev = lambda m: jnp.maximum(m - 1, 0)
    nxt = lambda m: jnp.minimum(m + 1, nb - 1)
    three = lambda w: [pl.BlockSpec((BLOCK, w), lambda m: (prev(m), 0)), pl.BlockSpec((BLOCK, w), lambda m: (m, 0)),
                       pl.BlockSpec((BLOCK, w), lambda m: (nxt(m), 0))]
    blk = lambda w: pl.BlockSpec((BLOCK, w), lambda m: (m, 0))
    return pl.pallas_call(
        body, name="attn_bwd_dkv", grid=(nb,),
        in_specs=[pl.BlockSpec((BLOCK, KV), lambda m: (m, kcol)), blk(KV)] + three(A) + three(A) + three(H) + three(H)
                 + [blk(LANES), blk(LANES)],
        out_specs=[blk(KV), blk(KV)],
        out_shape=[jax.ShapeDtypeStruct((L, KV), BF16), jax.ShapeDtypeStruct((L, KV), BF16)],
        compiler_params=_cparams("parallel"),
    )(qk, v, qk, qk, qk, dap, dap, dap, lse, lse, lse, rd, rd, rd, cos, sin)


def _halo_specs(T, L, W, col):
    per = T // HALO
    return [pl.BlockSpec((HALO, W), lambda i: (jnp.maximum(i * per - 1, 0), col)),
            pl.BlockSpec((T, W), lambda i: (i, col)),
            pl.BlockSpec((HALO, W), lambda i: (jnp.minimum((i + 1) * per, L // HALO - 1), col))]


def _fill_halo_buf(buf, prev_ref, cur_ref, next_ref, i, nt, T):
    buf[0:HALO, :] = jnp.where(i > 0, prev_ref[...], 0.0)
    buf[HALO:HALO + T, :] = cur_ref[...]
    buf[HALO + T:2 * HALO + T, :] = jnp.where(i < nt - 1, next_ref[...], 0.0)


def _counts(t, w, L):
    lo = jnp.clip(t - w // 2, 0, L)
    hi = jnp.clip(t - w // 2 + w, 0, L)
    return jnp.maximum(hi - lo, 1).astype(F32)


def _pool_fwd(u, pw, scale):
    L, P = u.shape
    gd = P // POOL_GROUPS
    T = _tile(L, 256, 8)
    nt = L // T

    def body(up_ref, uc_ref, un_ref, pw_ref, sc_ref, out_ref, pooled_ref, buf):
        i = pl.program_id(0)
        _fill_halo_buf(buf, up_ref, uc_ref, un_ref, i, nt, T)
        t = i * T + lax.broadcasted_iota(I32, (T, 1), 0)
        for g, w in enumerate(POOL_WINDOWS):
            cols = slice(g * gd, (g + 1) * gd)
            acc = buf[pl.ds(HALO - w // 2, T), cols]
            for o in range(-w // 2 + 1, w // 2):
                acc = acc + buf[pl.ds(HALO + o, T), cols]
            pooled = (acc / _counts(t, w, L) - buf[pl.ds(HALO, T), cols]).astype(BF16)
            pooled_ref[:, cols] = pooled
            out_ref[:, cols] = (_dot(pooled, pw_ref[g]) * sc_ref[:, cols]).astype(BF16)

    return pl.pallas_call(
        body, name="pool_fwd", grid=(nt,),
        in_specs=_halo_specs(T, L, P, 0) + [pl.BlockSpec((POOL_GROUPS, gd, gd), lambda i: (0, 0, 0)), pl.BlockSpec((1, P), lambda i: (0, 0))],
        out_specs=[pl.BlockSpec((T, P), lambda i: (i, 0)), pl.BlockSpec((T, P), lambda i: (i, 0))],
        out_shape=[jax.ShapeDtypeStruct((L, P), BF16), jax.ShapeDtypeStruct((L, P), BF16)],
        scratch_shapes=[pltpu.VMEM((T + 2 * HALO, P), F32)],
        compiler_params=_cparams("parallel"),
    )(u, u, u, pw, scale)


def _pool_bwd_mix(dap, pooled, pw, scale, pcol):
    L, P = pooled.shape
    gd = P // POOL_GROUPS
    T = _tile(L, 256, 8)

    def body(dp_ref, pooled_ref, pw_ref, sc_ref, dpooled_ref, dpw_ref, dsc_ref):
        i = pl.program_id(0)

        @pl.when(i == 0)
        def _():
            dpw_ref[...] = jnp.zeros_like(dpw_ref)
            dsc_ref[...] = jnp.zeros_like(dsc_ref)

        for g in range(POOL_GROUPS):
            cols = slice(g * gd, (g + 1) * gd)
            pb = pooled_ref[:, cols]
            dp = dp_ref[:, cols]
            dsc_ref[:, cols] += jnp.sum(dp * _dot(pb, pw_ref[g]), axis=0, keepdims=True)
            dm = (dp * sc_ref[:, cols]).astype(BF16)
            dpw_ref[g] += _dot_tn(pb, dm)
            dpooled_ref[:, cols] = _dot_nt(dm, pw_ref[g])

    return pl.pallas_call(
        body, name="pool_bwd_mix", grid=(L // T,),
        in_specs=[pl.BlockSpec((T, P), lambda i: (i, pcol)), pl.BlockSpec((T, P), lambda i: (i, 0)),
                  pl.BlockSpec((POOL_GROUPS, gd, gd), lambda i: (0, 0, 0)), pl.BlockSpec((1, P), lambda i: (0, 0))],
        out_specs=[pl.BlockSpec((T, P), lambda i: (i, 0)), pl.BlockSpec((POOL_GROUPS, gd, gd), lambda i: (0, 0, 0)),
                   pl.BlockSpec((1, P), lambda i: (0, 0))],
        out_shape=[jax.ShapeDtypeStruct((L, P), F32), jax.ShapeDtypeStruct((POOL_GROUPS, gd, gd), F32), jax.ShapeDtypeStruct((1, P), F32)],
        compiler_params=_cparams("arbitrary"),
    )(dap, pooled, pw, scale)


def _pool_bwd_window(dpooled):
    L, P = dpooled.shape
    gd = P // POOL_GROUPS
    T = _tile(L, 256, 8)
    nt = L // T

    def body(dp_ref, dc_ref, dn_ref, du_ref, buf):
        i = pl.program_id(0)
        _fill_halo_buf(buf, dp_ref, dc_ref, dn_ref, i, nt, T)
        t = i * T - HALO + lax.broadcasted_iota(I32, (T + 2 * HALO, 1), 0)
        for g, w in enumerate(POOL_WINDOWS):
            cols = slice(g * gd, (g + 1) * gd)
            buf[:, cols] = buf[:, cols] / _counts(t, w, L)
            acc = buf[pl.ds(HALO - w // 2 + 1, T), cols]
            for o in range(-w // 2 + 2, w // 2 + 1):
                acc = acc + buf[pl.ds(HALO + o, T), cols]
            du_ref[:, cols] = (acc - dc_ref[:, cols]).astype(BF16)

    return pl.pallas_call(
        body, name="pool_bwd_window", grid=(nt,),
        in_specs=_halo_specs(T, L, P, 0),
        out_specs=pl.BlockSpec((T, P), lambda i: (i, 0)),
        out_shape=jax.ShapeDtypeStruct((L, P), BF16),
        scratch_shapes=[pltpu.VMEM((T + 2 * HALO, P), F32)],
        compiler_params=_cparams("parallel"),
    )(dpooled, dpooled, dpooled)


def _sum_rows(name, a):
    R, N = a.shape

    def body(a_ref, o_ref):
        if R <= 16:
            acc = a_ref[0:1, :]
            for r in range(1, R):
                acc = acc + a_ref[r:r + 1, :]
        else:
            acc = jnp.sum(a_ref[...], axis=0, keepdims=True)
        o_ref[...] = acc

    return pl.pallas_call(body, name=name, out_shape=jax.ShapeDtypeStruct((1, N), F32))(a)


def _silu_grad_mul(cv, g):
    def body(c_ref, g_ref, o_ref):
        cvv = c_ref[...]
        s = 1.0 / (1.0 + jnp.exp(-cvv))
        o_ref[...] = g_ref[...] * (s * (1.0 + cvv * (1.0 - s)))

    return pl.pallas_call(body, name="silu_grad_mul", out_shape=jax.ShapeDtypeStruct(cv.shape, F32))(cv, g)


def _adamw(name, w, g, m, v):
    R, C = w.shape
    parts = g.ndim == 3
    n_parts = g.shape[0] if parts else 1
    T = _tile(R, max(8, 262144 // C), 8)

    def body(w_ref, g_ref, m_ref, v_ref, go_ref, d_ref, mo_ref, vo_ref):
        if parts:
            gv = g_ref[0].astype(F32)
            for p in range(1, n_parts):
                gv = gv + g_ref[p].astype(F32)
        else:
            gv = g_ref[...]
        mn = ADAM_B1 * m_ref[...] + (1.0 - ADAM_B1) * gv
        vn = ADAM_B2 * v_ref[...] + (1.0 - ADAM_B2) * (gv * gv)
        m_hat = mn / (1.0 - ADAM_B1 ** ADAM_STEP)
        v_hat = vn / (1.0 - ADAM_B2 ** ADAM_STEP)
        go_ref[...] = gv
        d_ref[...] = -ADAM_LR * (m_hat / (jnp.sqrt(v_hat) + ADAM_EPS) + ADAM_WD * w_ref[...])
        mo_ref[...] = mn
        vo_ref[...] = vn

    tile = pl.BlockSpec((T, C), lambda i: (i, 0))
    g_spec = pl.BlockSpec((n_parts, T, C), lambda i: (0, i, 0)) if parts else tile
    return pl.pallas_call(
        body, name=name, grid=(R // T,),
        in_specs=[tile, g_spec, tile, tile], out_specs=[tile] * 4,
        out_shape=[jax.ShapeDtypeStruct((R, C), F32)] * 4,
        compiler_params=_cparams("parallel"),
    )(w, g, m, v)


def _dev_index(px, py, pc):
    return 4 * px + 2 * py + pc


def _all_gather(name, arrs):
    n = len(arrs)

    def body(*refs):
        ins, outs = refs[:n], refs[n:2 * n]
        send_sems, recv_sems, local_sems = refs[2 * n:]
        x, y, c = lax.axis_index("x"), lax.axis_index("y"), lax.axis_index("c")
        me, sibling = (x, y, c), (x, y, 1 - c)
        chips = [(1 - x, y), (x, 1 - y), (1 - x, 1 - y)]

        def copy(a, k, block, to, src=None):
            slot = outs[a].at[_dev_index(*block)]
            return pltpu.make_async_remote_copy(
                src_ref=slot if src is None else src, dst_ref=slot, send_sem=send_sems.at[a, k], recv_sem=recv_sems.at[a, k],
                device_id=to, device_id_type=MESH)

        mine = [pltpu.make_async_copy(ins[a], outs[a].at[_dev_index(*me)], local_sems.at[a]) for a in range(n)]
        for cp in mine:
            cp.start()
        first = []
        for a in range(n):
            first.append(copy(a, 0, me, sibling, src=ins[a]))
            first += [copy(a, 1 + j, me, (*chip, c), src=ins[a]) for j, chip in enumerate(chips)]
        for cp in first:
            cp.start()
        passed = []
        for j, chip in enumerate(chips):
            for a in range(n):
                copy(a, 1 + j, (*chip, c), me).wait_recv()
                fwd = copy(a, 4 + j, (*chip, c), sibling)
                fwd.start()
                passed.append(fwd)
        for a in range(n):
            copy(a, 0, sibling, me).wait_recv()
            for j, chip in enumerate(chips):
                copy(a, 4 + j, (*chip, 1 - c), me).wait_recv()
        for cp in first + passed:
            cp.wait_send()
        for cp in mine:
            cp.wait()

    return pl.pallas_call(
        body, name=name,
        in_specs=[HBM] * n, out_specs=[HBM] * n,
        out_shape=[jax.ShapeDtypeStruct((N_DEV, *a.shape), a.dtype) for a in arrs],
        scratch_shapes=[pltpu.SemaphoreType.DMA((n, N_DEV - 1)), pltpu.SemaphoreType.DMA((n, N_DEV - 1)), pltpu.SemaphoreType.DMA((n,))],
    )(*arrs)


def _all_to_all(name, arrs):
    n = len(arrs)
    flips = [(dx, dy, dc) for dx in (0, 1) for dy in (0, 1) for dc in (0, 1)][1:]

    def body(*refs):
        ins, outs = refs[:n], refs[n:2 * n]
        send_sems, recv_sems, local_sems = refs[2 * n:]
        x, y, c = lax.axis_index("x"), lax.axis_index("y"), lax.axis_index("c")
        me = _dev_index(x, y, c)
        peers = [(1 - x if dx else x, 1 - y if dy else y, 1 - c if dc else c) for dx, dy, dc in flips]

        def copy(a, k):
            peer = peers[k]
            return pltpu.make_async_remote_copy(
                src_ref=ins[a].at[_dev_index(*peer)], dst_ref=outs[a].at[me], send_sem=send_sems.at[a, k], recv_sem=recv_sems.at[a, k],
                device_id=peer, device_id_type=MESH)

        def arrival(a, k):
            slot = outs[a].at[_dev_index(*peers[k])]
            return pltpu.make_async_remote_copy(
                src_ref=slot, dst_ref=slot, send_sem=send_sems.at[a, k], recv_sem=recv_sems.at[a, k],
                device_id=peers[k], device_id_type=MESH)

        mine = [pltpu.make_async_copy(ins[a].at[me], outs[a].at[me], local_sems.at[a]) for a in range(n)]
        for cp in mine:
            cp.start()
        sends = [copy(a, k) for a in range(n) for k in range(N_DEV - 1)]
        for cp in sends:
            cp.start()
        for a in range(n):
            for k in range(N_DEV - 1):
                arrival(a, k).wait_recv()
        for cp in sends:
            cp.wait_send()
        for cp in mine:
            cp.wait()

    return pl.pallas_call(
        body, name=name,
        in_specs=[HBM] * n, out_specs=[HBM] * n,
        out_shape=[jax.ShapeDtypeStruct(a.shape, a.dtype) for a in arrs],
        scratch_shapes=[pltpu.SemaphoreType.DMA((n, N_DEV - 1)), pltpu.SemaphoreType.DMA((n, N_DEV - 1)), pltpu.SemaphoreType.DMA((n,))],
    )(*arrs)


def _shards_to_cols(g):
    return jnp.transpose(g, (1, 0, 2)).reshape(g.shape[1], N_DEV * g.shape[2])


def _cols_to_shards(a):
    R, Ctot = a.shape
    return jnp.transpose(a.reshape(R, N_DEV, Ctot // N_DEV), (1, 0, 2))


def kernel(x, c, ctx, c_ctx, norm_attn_w, norm_mlp_w, w_ada, b_ada, w_in, attn_sink, pool_w, pool_scale, w_out, w_mlp_up, w_mlp_down, final_norm_w, loss_target, m_c_ctx, m_norm_attn_w, m_norm_mlp_w, m_w_ada, m_b_ada, m_w_in, m_attn_sink, m_pool_w, m_pool_scale, m_w_out, m_w_mlp_up, m_w_mlp_down, m_final_norm_w, v_c_ctx, v_norm_attn_w, v_norm_mlp_w, v_w_ada, v_b_ada, v_w_in, v_attn_sink, v_pool_w, v_pool_scale, v_w_out, v_w_mlp_up, v_w_mlp_down, v_final_norm_w):
    _, L, D = x.shape
    C = ctx.shape[1]
    H = attn_sink.shape[1]
    A = H * HEAD_DIM
    KV = A // GQA
    P = pool_scale.shape[1]
    IN = A + 2 * KV + P
    MODW = N_MOD * D
    ws = MODW // N_DEV
    gd = P // POOL_GROUPS
    me = _dev_index(lax.axis_index("x"), lax.axis_index("y"), lax.axis_index("c"))

    x2d, ctx2d, tgt = x[0], ctx[0], loss_target[0]
    cctx_row = c_ctx.reshape(1, D)
    wf_row = final_norm_w.reshape(1, D)
    w_ada_l = w_ada[0]
    pool_w_l = pool_w[0].reshape(POOL_GROUPS * (gd // N_DEV), gd)

    c_all, win_g, wout_g, wup_g, wdown_g, pw_g = _all_gather(
        "gather_weights",
        [c, w_in[0].astype(BF16), w_out[0].astype(BF16), w_mlp_up[0].astype(BF16), w_mlp_down[0].astype(BF16), pool_w_l.astype(BF16)])
    W_in = _shards_to_cols(win_g)
    W_up = _shards_to_cols(wup_g)
    W_out = wout_g.reshape(A + P, D)
    W_down = wdown_g.reshape(-1, D)
    PW = jnp.transpose(pw_g.reshape(N_DEV, POOL_GROUPS, gd // N_DEV, gd), (1, 0, 2, 3)).reshape(POOL_GROUPS, gd, gd)
    W_qk, W_v, W_u, W_kv = W_in[:, :A + KV], W_in[:, A + KV:A + 2 * KV], W_in[:, A + 2 * KV:], W_in[:, A:A + 2 * KV]

    cond = jnp.concatenate([c_all[:, 0, :], cctx_row, jnp.zeros((2 * N_DEV - N_DEV - 1, D), F32)], axis=0)
    b_sh = lax.dynamic_slice_in_dim(b_ada, me * ws, ws, axis=1)
    (mods_sh,) = _mm("ada_mod", cond, w_ada_l, "nn", [F32], a_pre=_silu, extras=[("n", b_sh)], epilogue=lambda acc, b: (acc + b,))
    (mods_g,) = _all_gather("gather_mods", [mods_sh])
    mods = _shards_to_cols(mods_g)
    mod_b = lax.dynamic_slice_in_dim(mods, me, 1, axis=0)
    sh_a, sc_a, g_a, sh_m, sc_m, g_m = [mod_b[:, i * D:(i + 1) * D] for i in range(N_MOD)]
    csh_a, csc_a = mods[N_DEV:N_DEV + 1, :D], mods[N_DEV:N_DEV + 1, D:2 * D]

    cos, sin = _rope_tables(L)
    h, r1 = _norm_fwd("norm_attn", x2d, norm_attn_w, sc_a, sh_a)
    hc, rc = _norm_fwd("norm_attn_ctx", ctx2d, norm_attn_w, csc_a, csh_a)
    (qk,) = _mm("in_proj_qk", h, W_qk, "nn", [BF16], extras=[("m", cos), ("m", sin)], epilogue=lambda acc, cs, sn: (_rope(acc, cs, sn),), tn=640)
    (vv,) = _mm("in_proj_v", h, W_v, "nn", [BF16])
    (u,) = _mm("in_proj_u", h, W_u, "nn", [F32])
    (kvc,) = _mm("in_proj_ctx", hc, W_kv, "nn", [BF16])
    attn, lse = _attn_fwd(qk, vv, kvc, attn_sink, A, KV)
    pool_out, pooled = _pool_fwd(u, PW, pool_scale)
    ap = jnp.concatenate([attn, pool_out], axis=1)
    o, x1 = _mm("out_proj", ap, W_out, "nn", [F32, F32], extras=[("mn", x2d), ("n", g_a)], epilogue=lambda acc, xr, g: (acc, xr + g * acc))
    hm, r2 = _norm_fwd("norm_mlp", x1, norm_mlp_w, sc_m, sh_m)
    up, act = _mm("mlp_up", hm, W_up, "nn", [F32, BF16], epilogue=lambda acc: (acc, _relu2(acc)))
    mlp, x2 = _mm("mlp_down", act, W_down, "nn", [F32, F32], extras=[("mn", x1), ("n", g_m)], epilogue=lambda acc, xr, g: (acc, xr + g * acc))
    d_x2, d_mlp, d_wf, d_gm, loss_p = _final(x2, tgt, mlp, wf_row, g_m)

    (d_up,) = _mm("mlp_down_bwd_act", d_mlp, W_down, "nt", [BF16], extras=[("mn", up)], epilogue=lambda acc, uu: (acc * (2.0 * jnp.maximum(uu, 0.0)),))
    (gW_down,) = _mm("mlp_down_bwd_w", up, d_mlp, "tn", [BF16], a_pre=_relu2)
    (gW_up,) = _mm("mlp_up_bwd_w", hm, d_up, "tn", [BF16])
    (d_hm,) = _mm("mlp_up_bwd_act", d_up, W_up, "nt", [F32])
    zrow = jnp.zeros((1, D), F32)
    d_x1, s_sh_m, s_sc_m, s_w_nm, d_ga, d_o = _norm_bwd("norm_mlp_bwd", x1, r2, d_hm, d_x2, norm_mlp_w, sc_m, zrow, gate=(o, g_a))

    (d_ap,) = _mm("out_proj_bwd_act", d_o, W_out, "nt", [F32])
    (gW_out,) = _mm("out_proj_bwd_w", ap, d_o, "tn", [BF16])
    d_pooled, gPW, d_pscale = _pool_bwd_mix(d_ap, pooled, PW, pool_scale, A // P)
    d_u = _pool_bwd_window(d_pooled)
    d_q, rd, dsink_rows, d_kvc = _attn_bwd_dq(qk, vv, kvc, attn_sink, d_ap, lse, cos, sin, A, KV)
    d_k, d_v = _attn_bwd_dkv(qk, vv, d_ap, lse, rd, cos, sin, A, KV)
    d_sink = _sum_rows("sink_grad", dsink_rows)
    d_p = jnp.concatenate([d_q, d_k, d_v, d_u], axis=1)
    d_kvc_b = d_kvc.astype(BF16)
    (gW_kv_ctx,) = _mm("in_proj_ctx_bwd_w", hc, d_kvc_b, "tn", [F32])
    (d_hc,) = _mm("in_proj_ctx_bwd_act", d_kvc_b, W_kv, "nt", [F32])
    gW_in_init = jnp.pad(gW_kv_ctx, ((0, 0), (A, P)))
    (gW_in,) = _mm("in_proj_bwd_w", h, d_p, "tn", [BF16], extras=[("mn", gW_in_init)], epilogue=lambda acc, init: (acc + init,))
    (d_h,) = _mm("in_proj_bwd_act", d_p, W_in, "nt", [F32])
    grad_x, s_sh_a, s_sc_a, s_w_na = _norm_bwd("norm_attn_bwd", x2d, r1, d_h, d_x1, norm_attn_w, sc_a, zrow)
    _, s_csh, s_csc, s_w_na = _norm_bwd("norm_attn_ctx_bwd", ctx2d, rc, d_hc, jnp.zeros_like(ctx2d), norm_attn_w, csc_a, s_w_na)

    pad_l = lambda a: jnp.pad(a, ((0, 0), (0, LANES - a.shape[1])))
    d_mod_b = jnp.concatenate([s_sh_a, s_sc_a, d_ga, s_sh_m, s_sc_m, d_gm], axis=1)
    summed = jnp.concatenate([s_csh, s_csc, s_w_na, s_w_nm, d_wf, d_pscale, pad_l(d_sink), pad_l(loss_p)], axis=1)
    (small_g,) = _all_gather("gather_small", [jnp.concatenate([d_mod_b, summed], axis=1)])
    small_g = small_g[:, 0, :]
    tot = _sum_rows("small_sum", small_g[:, MODW:])
    off = [0]
    for wdt in (D, D, D, D, D, P, LANES, LANES):
        off.append(off[-1] + wdt)
    seg = lambda i: tot[:, off[i]:off[i + 1]]
    g_norm_attn, g_norm_mlp, g_final, g_pscale = seg(2), seg(3), seg(4), seg(5)
    g_sink, loss = seg(6)[:, :H], seg(7)[0, 0]
    d_mod_ctx = jnp.concatenate([seg(0), seg(1), jnp.zeros((1, MODW - 2 * D), F32)], axis=1)
    d_mod = jnp.concatenate([small_g[:, :MODW], d_mod_ctx, jnp.zeros((N_DEV - 1, MODW), F32)], axis=0)
    g_b_ada = _sum_rows("b_ada_grad", d_mod[:N_DEV + 1])
    d_mod_sh = lax.dynamic_slice_in_dim(d_mod, me * ws, ws, axis=1)
    (g_w_ada,) = _mm("ada_bwd_w", cond, d_mod_sh, "tn", [F32], a_pre=_silu)
    (d_cond_p,) = _mm("ada_bwd_cond", d_mod_sh, w_ada_l, "nt", [F32])
    (d_cctx_g,) = _all_gather("gather_cctx", [d_cond_p[N_DEV:N_DEV + 1]])
    g_c_ctx = _silu_grad_mul(cctx_row, _sum_rows("cctx_sum", d_cctx_g[:, 0, :]))

    gpw_s = jnp.transpose(gPW.astype(BF16).reshape(POOL_GROUPS, N_DEV, gd // N_DEV, gd), (1, 0, 2, 3)).reshape(N_DEV, -1, gd)
    r_in, r_out, r_up, r_down, r_pw = _all_to_all(
        "exchange_grads",
        [_cols_to_shards(gW_in), gW_out.reshape(N_DEV, (A + P) // N_DEV, D), _cols_to_shards(gW_up),
         gW_down.reshape(N_DEV, -1, D), gpw_s])

    results = {
        "c_ctx": _adamw("adam_c_ctx", cctx_row, g_c_ctx, m_c_ctx.reshape(1, D), v_c_ctx.reshape(1, D)),
        "norm_attn_w": _adamw("adam_norm_attn", norm_attn_w, g_norm_attn, m_norm_attn_w, v_norm_attn_w),
        "norm_mlp_w": _adamw("adam_norm_mlp", norm_mlp_w, g_norm_mlp, m_norm_mlp_w, v_norm_mlp_w),
        "w_ada": _adamw("adam_w_ada", w_ada_l, g_w_ada, m_w_ada[0], v_w_ada[0]),
        "b_ada": _adamw("adam_b_ada", b_ada, g_b_ada, m_b_ada, v_b_ada),
        "w_in": _adamw("adam_w_in", w_in[0], r_in, m_w_in[0], v_w_in[0]),
        "attn_sink": _adamw("adam_sink", attn_sink, g_sink, m_attn_sink, v_attn_sink),
        "pool_w": _adamw("adam_pool_w", pool_w_l, r_pw, m_pool_w[0].reshape(pool_w_l.shape), v_pool_w[0].reshape(pool_w_l.shape)),
        "pool_scale": _adamw("adam_pool_scale", pool_scale, g_pscale, m_pool_scale, v_pool_scale),
        "w_out": _adamw("adam_w_out", w_out[0], r_out, m_w_out[0], v_w_out[0]),
        "w_mlp_up": _adamw("adam_w_up", w_mlp_up[0], r_up, m_w_mlp_up[0], v_w_mlp_up[0]),
        "w_mlp_down": _adamw("adam_w_down", w_mlp_down[0], r_down, m_w_mlp_down[0], v_w_mlp_down[0]),
        "final_norm_w": _adamw("adam_final_norm", wf_row, g_final, m_final_norm_w.reshape(1, D), v_final_norm_w.reshape(1, D)),
    }
    shapes = {"c_ctx": c_ctx.shape, "norm_attn_w": norm_attn_w.shape, "norm_mlp_w": norm_mlp_w.shape, "w_ada": w_ada.shape,
              "b_ada": b_ada.shape, "w_in": w_in.shape, "attn_sink": attn_sink.shape, "pool_w": pool_w.shape,
              "pool_scale": pool_scale.shape, "w_out": w_out.shape, "w_mlp_up": w_mlp_up.shape, "w_mlp_down": w_mlp_down.shape,
              "final_norm_w": final_norm_w.shape}
    outs = [loss, grad_x.reshape(x.shape)]
    for part in range(4):
        outs += [results[name][part].reshape(shape) for name, shape in shapes.items()]
    return tuple(outs)
```

```python
import functools

import jax
import jax.numpy as jnp
from jax import lax
from jax.experimental import pallas as pl
from jax.experimental.pallas import tpu as pltpu

F32 = jnp.float32
BF16 = jnp.bfloat16
I32 = jnp.int32

HEAD_DIM = 64
GQA = 4
BLOCK = 128
GRID_W = 64
ROPE_BASE = 10000.0
POOL_WINDOWS = (2, 4, 8, 16)
POOL_GROUPS = len(POOL_WINDOWS)
HALO = 8
N_MOD = 6
EPS = 1e-6
NEG_INF = -1e30
ADAM_LR = 0.001
ADAM_B1 = 0.9
ADAM_B2 = 0.999
ADAM_EPS = 1e-08
ADAM_WD = 0.01
ADAM_STEP = 10
N_DEV = 8
COND_ROWS = 2 * N_DEV
LANES = 128
SUBLANES_16BIT = 16
VMEM_LIMIT = 48 * 1024 * 1024
SMALL_TILES = (512, 1024, 512)
MESH = pl.DeviceIdType.MESH
HBM = pl.BlockSpec(memory_space=pltpu.HBM)
SEM = pl.BlockSpec(memory_space=pltpu.SEMAPHORE)
SIDE_EFFECT = pltpu.CompilerParams(has_side_effects=pltpu.SideEffectType.DATAFLOW_SIDE_EFFECTING)


def _cparams(*sem):
    return pltpu.CompilerParams(dimension_semantics=sem, vmem_limit_bytes=VMEM_LIMIT)


def _tile(n, pref, align):
    if n <= pref:
        return n
    t = (pref // align) * align
    while t >= align:
        if n % t == 0:
            return t
        t -= align
    return n


def _dot(a, b):
    return lax.dot_general(a, b, (((1,), (0,)), ((), ())), preferred_element_type=F32)


def _dot_nt(a, b):
    return lax.dot_general(a, b, (((1,), (1,)), ((), ())), preferred_element_type=F32)


def _dot_tn(a, b):
    return lax.dot_general(a, b, (((0,), (0,)), ((), ())), preferred_element_type=F32)


_DOTS = {"nn": _dot, "nt": _dot_nt, "tn": _dot_tn}


def _mm(name, a, b, mode, out_dtypes, tiles, *, epilogue=None, extras=(), a_pre=None):
    if mode == "nn":
        (M, K), (K2, N) = a.shape, b.shape
    elif mode == "nt":
        (M, K), (N, K2) = a.shape, b.shape
    else:
        (K, M), (K2, N) = a.shape, b.shape
    assert K == K2, (name, a.shape, b.shape)
    tm = _tile(M, tiles[0], LANES if mode == "tn" else SUBLANES_16BIT)
    tn = _tile(N, tiles[1], LANES)
    tk = _tile(K, tiles[2], SUBLANES_16BIT if mode == "tn" else LANES)
    nk = K // tk
    n_ex, n_out = len(extras), len(out_dtypes)

    def product(a_ref, b_ref):
        at = a_ref[...]
        if a_pre is not None:
            at = a_pre(at)
        return _DOTS[mode](at.astype(BF16), b_ref[...].astype(BF16))

    def finish(acc, ex_refs, out_refs):
        outs = (acc,) if epilogue is None else epilogue(acc, *[r[...] for r in ex_refs])
        for o_ref, o in zip(out_refs, outs):
            o_ref[...] = o.astype(o_ref.dtype)

    def body_one(a_ref, b_ref, *rest):
        finish(product(a_ref, b_ref), rest[:n_ex], rest[n_ex:n_ex + n_out])

    def body_acc(a_ref, b_ref, *rest):
        ex_refs, out_refs, acc_ref = rest[:n_ex], rest[n_ex:n_ex + n_out], rest[-1]
        k = pl.program_id(2)

        @pl.when(k == 0)
        def _():
            acc_ref[...] = product(a_ref, b_ref)

        @pl.when(k > 0)
        def _():
            acc_ref[...] += product(a_ref, b_ref)

        @pl.when(k == nk - 1)
        def _():
            finish(acc_ref[...], ex_refs, out_refs)

    a_spec = pl.BlockSpec((tk, tm), lambda i, j, k: (k, i)) if mode == "tn" else pl.BlockSpec((tm, tk), lambda i, j, k: (i, k))
    b_spec = pl.BlockSpec((tn, tk), lambda i, j, k: (j, k)) if mode == "nt" else pl.BlockSpec((tk, tn), lambda i, j, k: (k, j))
    ex_specs = []
    for kind, arr in extras:
        if kind == "mn":
            ex_specs.append(pl.BlockSpec((tm, tn), lambda i, j, k: (i, j)))
        elif kind == "n":
            ex_specs.append(pl.BlockSpec((1, tn), lambda i, j, k: (0, j)))
        else:
            ex_specs.append(pl.BlockSpec((tm, arr.shape[1]), lambda i, j, k: (i, 0)))
    return pl.pallas_call(
        body_one if nk == 1 else body_acc,
        name=name,
        grid=(M // tm, N // tn, nk),
        in_specs=[a_spec, b_spec] + ex_specs,
        out_specs=[pl.BlockSpec((tm, tn), lambda i, j, k: (i, j)) for _ in out_dtypes],
        out_shape=[jax.ShapeDtypeStruct((M, N), d) for d in out_dtypes],
        scratch_shapes=[] if nk == 1 else [pltpu.VMEM((tm, tn), F32)],
        compiler_params=_cparams("parallel", "parallel", "arbitrary"),
    )(a, b, *[arr for _, arr in extras])


def _silu(v):
    return v / (1.0 + jnp.exp(-v))


def _relu2(v):
    r = jnp.maximum(v, 0.0)
    return r * r


def _rope_tables(L):
    half = HEAD_DIM // 2
    inv_freq = ROPE_BASE ** (-jnp.arange(0, half, 2, dtype=F32) / half)
    t = jnp.arange(L)
    row, col = t // GRID_W, t % GRID_W
    ang_r = row.astype(F32)[:, None] * inv_freq[None, :]
    ang_c = col.astype(F32)[:, None] * inv_freq[None, :]
    cos = jnp.concatenate([jnp.cos(ang_r), jnp.cos(ang_r), jnp.cos(ang_c), jnp.cos(ang_c)], axis=1)
    sin = jnp.concatenate([-jnp.sin(ang_r), jnp.sin(ang_r), -jnp.sin(ang_c), jnp.sin(ang_c)], axis=1)
    reps = LANES // HEAD_DIM
    return jnp.tile(cos, (1, reps)), jnp.tile(sin, (1, reps))


def _rope(xf, cos, sin):
    quarter = HEAD_DIM // 4
    lane = lax.broadcasted_iota(I32, (xf.shape[0], LANES), 1)
    first = (lane & quarter) == 0
    outs = []
    for j in range(xf.shape[1] // LANES):
        xc = xf[:, j * LANES:(j + 1) * LANES]
        partner = jnp.where(first, pltpu.roll(xc, LANES - quarter, 1), pltpu.roll(xc, quarter, 1))
        outs.append(xc * cos + partner * sin)
    return outs[0] if len(outs) == 1 else jnp.concatenate(outs, axis=1)


def _norm_fwd(name, x, w, sc, sh):
    L, D = x.shape
    T = _tile(L, 256, 8)

    def body(x_ref, w_ref, sc_ref, sh_ref, h_ref, r_ref):
        xf = x_ref[...]
        r = lax.rsqrt(jnp.mean(xf * xf, axis=-1, keepdims=True) + EPS)
        n = (xf * r) * w_ref[...]
        h_ref[...] = (n * (1.0 + sc_ref[...]) + sh_ref[...]).astype(BF16)
        r_ref[...] = r

    row = pl.BlockSpec((1, D), lambda i: (0, 0))
    return pl.pallas_call(
        body, name=name, grid=(L // T,),
        in_specs=[pl.BlockSpec((T, D), lambda i: (i, 0)), row, row, row],
        out_specs=[pl.BlockSpec((T, D), lambda i: (i, 0)), pl.BlockSpec((T, 1), lambda i: (i, 0))],
        out_shape=[jax.ShapeDtypeStruct((L, D), BF16), jax.ShapeDtypeStruct((L, 1), F32)],
        compiler_params=_cparams("parallel"),
    )(x, w, sc, sh)


def _norm_bwd(name, x, r, dh, dres, w, sc, w_init, gate=None):
    L, D = x.shape
    T = _tile(L, 256, 8)
    with_gate = gate is not None

    def body(x_ref, r_ref, dh_ref, dres_ref, w_ref, sc_ref, wi_ref, *rest):
        if with_gate:
            o_ref, g_ref, dx_ref, ssh_ref, ssc_ref, sw_ref, sg_ref, do_ref = rest
        else:
            dx_ref, ssh_ref, ssc_ref, sw_ref = rest
        i = pl.program_id(0)

        @pl.when(i == 0)
        def _():
            ssh_ref[...] = jnp.zeros_like(ssh_ref)
            ssc_ref[...] = jnp.zeros_like(ssc_ref)
            sw_ref[...] = wi_ref[...]
            if with_gate:
                sg_ref[...] = jnp.zeros_like(sg_ref)

        rr = r_ref[...]
        xh = x_ref[...] * rr
        dh = dh_ref[...]
        wv = w_ref[...]
        dn = dh * (1.0 + sc_ref[...])
        ssh_ref[...] += jnp.sum(dh, axis=0, keepdims=True)
        ssc_ref[...] += jnp.sum(dh * (xh * wv), axis=0, keepdims=True)
        sw_ref[...] += jnp.sum(dn * xh, axis=0, keepdims=True)
        dxh = dn * wv
        dx = dres_ref[...] + rr * (dxh - xh * jnp.mean(dxh * xh, axis=-1, keepdims=True))
        dx_ref[...] = dx
        if with_gate:
            sg_ref[...] += jnp.sum(dx * o_ref[...], axis=0, keepdims=True)
            do_ref[...] = (g_ref[...] * dx).astype(BF16)

    tile = pl.BlockSpec((T, D), lambda i: (i, 0))
    row = pl.BlockSpec((1, D), lambda i: (0, 0))
    in_specs = [tile, pl.BlockSpec((T, 1), lambda i: (i, 0)), tile, tile, row, row, row]
    out_specs = [tile, row, row, row]
    out_shape = [jax.ShapeDtypeStruct((L, D), F32)] + [jax.ShapeDtypeStruct((1, D), F32)] * 3
    args = [x, r, dh, dres, w, sc, w_init]
    if with_gate:
        in_specs += [tile, row]
        out_specs += [row, tile]
        out_shape += [jax.ShapeDtypeStruct((1, D), F32), jax.ShapeDtypeStruct((L, D), BF16)]
        args += list(gate)
    return pl.pallas_call(
        body, name=name, grid=(L // T,), in_specs=in_specs, out_specs=out_specs, out_shape=out_shape,
        compiler_params=_cparams("arbitrary"),
    )(*args)


def _final(x2, tgt, mlp, wf, gm):
    L, D = x2.shape
    T = _tile(L, 256, 8)

    def body(x_ref, t_ref, mlp_ref, wf_ref, gm_ref, dx_ref, dmlp_ref, dwf_ref, dgm_ref, loss_ref):
        i = pl.program_id(0)

        @pl.when(i == 0)
        def _():
            dwf_ref[...] = jnp.zeros_like(dwf_ref)
            dgm_ref[...] = jnp.zeros_like(dgm_ref)
            loss_ref[...] = jnp.zeros_like(loss_ref)

        xf = x_ref[...]
        r = lax.rsqrt(jnp.mean(xf * xf, axis=-1, keepdims=True) + EPS)
        xh = xf * r
        wv = wf_ref[...]
        err = xh * wv - t_ref[...]
        row_loss = jnp.mean(err * err, axis=-1, keepdims=True)
        loss_ref[...] += 0.5 * jnp.sum(row_loss, axis=0, keepdims=True)
        dy = err / D
        dwf_ref[...] += jnp.sum(dy * xh, axis=0, keepdims=True)
        dxh = dy * wv
        dx = r * (dxh - xh * jnp.mean(dxh * xh, axis=-1, keepdims=True))
        dx_ref[...] = dx
        dgm_ref[...] += jnp.sum(dx * mlp_ref[...], axis=0, keepdims=True)
        dmlp_ref[...] = (gm_ref[...] * dx).astype(BF16)

    tile = pl.BlockSpec((T, D), lambda i: (i, 0))
    row = pl.BlockSpec((1, D), lambda i: (0, 0))
    return pl.pallas_call(
        body, name="final_norm_loss", grid=(L // T,),
        in_specs=[tile, tile, tile, row, row],
        out_specs=[tile, tile, row, row, pl.BlockSpec((1, 1), lambda i: (0, 0))],
        out_shape=[jax.ShapeDtypeStruct((L, D), F32), jax.ShapeDtypeStruct((L, D), BF16),
                   jax.ShapeDtypeStruct((1, D), F32), jax.ShapeDtypeStruct((1, D), F32), jax.ShapeDtypeStruct((1, 1), F32)],
        compiler_params=_cparams("arbitrary"),
    )(x2, tgt, mlp, wf, gm)


def _heads(ref, first, n):
    return jnp.concatenate([ref[:, (first + g) * HEAD_DIM:(first + g + 1) * HEAD_DIM] for g in range(n)], axis=0)


def _put_col(tile, h, col):
    lane = lax.broadcasted_iota(I32, tile.shape, 1)
    return jnp.where(lane == h, col, tile)


def _get_col(tile, h):
    lane = lax.broadcasted_iota(I32, tile.shape, 1)
    return jnp.sum(jnp.where(lane == h, tile, 0.0), axis=1, keepdims=True)


def _attn_mask(n, L, C):
    shape = (GQA * BLOCK, 3 * BLOCK + C)
    qi = lax.broadcasted_iota(I32, shape, 0) & (BLOCK - 1)
    kj = lax.broadcasted_iota(I32, shape, 1)
    kpos = n * BLOCK - BLOCK + kj
    window = (kj >= qi) & (kj <= qi + 2 * BLOCK) & (kpos >= 0) & (kpos < L)
    return window | (kj >= 3 * BLOCK)


def _attn_specs(L, A, KV, C, vcol):
    nb = L // BLOCK
    kcol = A // KV
    prev = lambda n: jnp.maximum(n - 1, 0)
    nxt = lambda n: jnp.minimum(n + 1, nb - 1)
    q_spec = pl.BlockSpec((BLOCK, A), lambda n: (n, 0))
    k_specs = [pl.BlockSpec((BLOCK, KV), lambda n: (prev(n), kcol)), pl.BlockSpec((BLOCK, KV), lambda n: (n, kcol)),
               pl.BlockSpec((BLOCK, KV), lambda n: (nxt(n), kcol))]
    v_specs = [pl.BlockSpec((BLOCK, KV), lambda n: (prev(n), vcol)), pl.BlockSpec((BLOCK, KV), lambda n: (n, vcol)),
               pl.BlockSpec((BLOCK, KV), lambda n: (nxt(n), vcol))]
    kvc_spec = pl.BlockSpec((C, 2 * KV), lambda n: (0, 0))
    return q_spec, k_specs, v_specs, kvc_spec


def _keys_values(hk, k_refs, v_refs, kvc_ref, KV):
    sl = slice(hk * HEAD_DIM, (hk + 1) * HEAD_DIM)
    keys = jnp.concatenate([r[:, sl] for r in k_refs] + [kvc_ref[:, sl]], axis=0)
    vals = jnp.concatenate([r[:, sl].astype(BF16) for r in v_refs] + [kvc_ref[:, KV + hk * HEAD_DIM:KV + (hk + 1) * HEAD_DIM]], axis=0)
    return keys, vals


def _sink_col(sink_ref, hk):
    return jnp.concatenate([jnp.full((BLOCK, 1), sink_ref[0, hk * GQA + g], F32) for g in range(GQA)], axis=0)


def _attn_fwd(qk, uv, kvc, sink, A, KV, P):
    L = qk.shape[0]
    C = kvc.shape[0]
    nkv = KV // HEAD_DIM
    H = nkv * GQA
    scale = HEAD_DIM ** -0.5

    def body(sink_ref, q_ref, kp_ref, kc_ref, kn_ref, vp_ref, vc_ref, vn_ref, kvc_ref, o_ref, lse_ref):
        valid = _attn_mask(pl.program_id(0), L, C)
        lse_t = jnp.zeros((BLOCK, H), F32)
        for hk in range(nkv):
            keys, vals = _keys_values(hk, (kp_ref, kc_ref, kn_ref), (vp_ref, vc_ref, vn_ref), kvc_ref, KV)
            qs = _heads(q_ref, hk * GQA, GQA) * scale
            s = jnp.where(valid, _dot_nt(qs, keys), NEG_INF)
            sk = _sink_col(sink_ref, hk)
            m = jnp.maximum(jnp.max(s, axis=-1, keepdims=True), sk)
            p = jnp.exp(s - m)
            den = jnp.sum(p, axis=-1, keepdims=True) + jnp.exp(sk - m)
            o = _dot(p.astype(BF16), vals) * (1.0 / den)
            lse = m + jnp.log(den)
            o_ref[:, hk * GQA * HEAD_DIM:(hk + 1) * GQA * HEAD_DIM] = jnp.concatenate(
                [o[g * BLOCK:(g + 1) * BLOCK] for g in range(GQA)], axis=1).astype(BF16)
            for g in range(GQA):
                lse_t = _put_col(lse_t, hk * GQA + g, lse[g * BLOCK:(g + 1) * BLOCK])
        lse_ref[...] = lse_t

    q_spec, k_specs, v_specs, kvc_spec = _attn_specs(L, A, KV, C, P // KV)
    return pl.pallas_call(
        body, name="attn_fwd", grid=(L // BLOCK,),
        in_specs=[pl.BlockSpec(memory_space=pltpu.SMEM), q_spec] + k_specs + v_specs + [kvc_spec],
        out_specs=[pl.BlockSpec((BLOCK, A), lambda n: (n, 0)), pl.BlockSpec((BLOCK, H), lambda n: (n, 0))],
        out_shape=[jax.ShapeDtypeStruct((L, A), BF16), jax.ShapeDtypeStruct((L, H), F32)],
        compiler_params=_cparams("parallel"),
    )(sink, qk, qk, qk, qk, uv, uv, uv, kvc)


def _attn_bwd_dq(qk, uv, kvc, sink, dap, lse, cos, sin, A, KV, P):
    L = qk.shape[0]
    C = kvc.shape[0]
    nkv = KV // HEAD_DIM
    H = nkv * GQA
    scale = HEAD_DIM ** -0.5
    W = 3 * BLOCK

    def body(sink_ref, q_ref, kp_ref, kc_ref, kn_ref, vp_ref, vc_ref, vn_ref, kvc_ref, do_ref, lse_ref, cos_ref, sin_ref,
             dq_ref, rd_ref, ds_ref, dkvc_ref):
        n = pl.program_id(0)

        @pl.when(n == 0)
        def _():
            dkvc_ref[...] = jnp.zeros_like(dkvc_ref)

        valid = _attn_mask(n, L, C)
        lse_t = lse_ref[...]
        rd_t = jnp.zeros((BLOCK, H), F32)
        ds_t = jnp.zeros((BLOCK, H), F32)
        dq_parts = []
        for hk in range(nkv):
            sl = slice(hk * HEAD_DIM, (hk + 1) * HEAD_DIM)
            keys, vals = _keys_values(hk, (kp_ref, kc_ref, kn_ref), (vp_ref, vc_ref, vn_ref), kvc_ref, KV)
            qs = _heads(q_ref, hk * GQA, GQA) * scale
            dos = _heads(do_ref, hk * GQA, GQA).astype(BF16)
            lse = jnp.concatenate([_get_col(lse_t, hk * GQA + g) for g in range(GQA)], axis=0)
            p = jnp.exp(jnp.where(valid, _dot_nt(qs, keys), NEG_INF) - lse)
            dp = _dot_nt(dos, vals)
            rd = jnp.sum(p * dp, axis=-1, keepdims=True)
            ds = (p * (dp - rd)).astype(BF16)
            dq = _dot(ds, keys) * scale
            dkvc_ref[:, sl] += _dot_tn(ds[:, W:], qs)
            dkvc_ref[:, KV + hk * HEAD_DIM:KV + (hk + 1) * HEAD_DIM] += _dot_tn(p[:, W:].astype(BF16), dos)
            dsink = -(jnp.exp(_sink_col(sink_ref, hk) - lse) * rd)
            for g in range(GQA):
                rows = slice(g * BLOCK, (g + 1) * BLOCK)
                rd_t = _put_col(rd_t, hk * GQA + g, rd[rows])
                ds_t = _put_col(ds_t, hk * GQA + g, dsink[rows])
                dq_parts.append(dq[rows])
        rd_ref[...] = rd_t
        ds_ref[...] = ds_t
        dq_ref[...] = _rope(jnp.concatenate(dq_parts, axis=1), cos_ref[...], -sin_ref[...]).astype(BF16)

    q_spec, k_specs, v_specs, kvc_spec = _attn_specs(L, A, KV, C, P // KV)
    blk = lambda w: pl.BlockSpec((BLOCK, w), lambda n: (n, 0))
    return pl.pallas_call(
        body, name="attn_bwd_dq", grid=(L // BLOCK,),
        in_specs=[pl.BlockSpec(memory_space=pltpu.SMEM), q_spec] + k_specs + v_specs + [kvc_spec, blk(A), blk(H), blk(LANES), blk(LANES)],
        out_specs=[blk(A), blk(H), blk(H), pl.BlockSpec((C, 2 * KV), lambda n: (0, 0))],
        out_shape=[jax.ShapeDtypeStruct((L, A), BF16), jax.ShapeDtypeStruct((L, H), F32), jax.ShapeDtypeStruct((L, H), F32),
                   jax.ShapeDtypeStruct((C, 2 * KV), F32)],
        compiler_params=_cparams("arbitrary"),
    )(sink, qk, qk, qk, qk, uv, uv, uv, kvc, dap, lse, cos, sin)


def _attn_bwd_dkv(qk, uv, dap, lse, rd, cos, sin, A, KV, P):
    L = qk.shape[0]
    nb = L // BLOCK
    nkv = KV // HEAD_DIM
    H = nkv * GQA
    scale = HEAD_DIM ** -0.5
    R = 3 * GQA * BLOCK

    def body(k_ref, v_ref, qp_ref, qc_ref, qn_ref, dop_ref, doc_ref, don_ref, lsep_ref, lsec_ref, lsen_ref,
             rdp_ref, rdc_ref, rdn_ref, cos_ref, sin_ref, dk_ref, dv_ref):
        m = pl.program_id(0)
        r = lax.broadcasted_iota(I32, (R, BLOCK), 0)
        part = r // (GQA * BLOCK)
        qi = r & (BLOCK - 1)
        kj = lax.broadcasted_iota(I32, (R, BLOCK), 1)
        before = jnp.where(m >= 1, 0, -2 * BLOCK)
        after = jnp.where(m <= nb - 2, 0, 2 * BLOCK)
        valid = ((part == 0) & (kj <= qi + before)) | (part == 1) | ((part == 2) & (kj >= qi + after))
        lse_ts = [lsep_ref[...], lsec_ref[...], lsen_ref[...]]
        rd_ts = [rdp_ref[...], rdc_ref[...], rdn_ref[...]]
        dk_parts, dv_parts = [], []
        for hk in range(nkv):
            sl = slice(hk * HEAD_DIM, (hk + 1) * HEAD_DIM)
            km = k_ref[:, sl]
            vm = v_ref[:, sl].astype(BF16)
            qs = jnp.concatenate([_heads(q, hk * GQA, GQA) for q in (qp_ref, qc_ref, qn_ref)], axis=0) * scale
            dos = jnp.concatenate([_heads(d, hk * GQA, GQA) for d in (dop_ref, doc_ref, don_ref)], axis=0).astype(BF16)
            lse = jnp.concatenate([_get_col(t, hk * GQA + g) for t in lse_ts for g in range(GQA)], axis=0)
            rdv = jnp.concatenate([_get_col(t, hk * GQA + g) for t in rd_ts for g in range(GQA)], axis=0)
            p = jnp.exp(jnp.where(valid, _dot_nt(qs, km), NEG_INF) - lse)
            ds = (p * (_dot_nt(dos, vm) - rdv)).astype(BF16)
            dk_parts.append(_dot_tn(ds, qs))
            dv_parts.append(_dot_tn(p.astype(BF16), dos))
        dk = dk_parts[0] if nkv == 1 else jnp.concatenate(dk_parts, axis=1)
        dv = dv_parts[0] if nkv == 1 else jnp.concatenate(dv_parts, axis=1)
        dk_ref[...] = _rope(dk, cos_ref[...], -sin_ref[...]).astype(BF16)
        dv_ref[...] = dv.astype(BF16)

    prev = lambda m: jnp.maximum(m - 1, 0)
    nxt = lambda m: jnp.minimum(m + 1, nb - 1)
    three = lambda w: [pl.BlockSpec((BLOCK, w), lambda m: (prev(m), 0)), pl.BlockSpec((BLOCK, w), lambda m: (m, 0)),
                       pl.BlockSpec((BLOCK, w), lambda m: (nxt(m), 0))]
    blk = lambda w: pl.BlockSpec((BLOCK, w), lambda m: (m, 0))
    return pl.pallas_call(
        body, name="attn_bwd_dkv", grid=(nb,),
        in_specs=[pl.BlockSpec((BLOCK, KV), lambda m: (m, A // KV)), pl.BlockSpec((BLOCK, KV), lambda m: (m, P // KV))]
                 + three(A) + three(A) + three(H) + three(H) + [blk(LANES), blk(LANES)],
        out_specs=[blk(KV), blk(KV)],
        out_shape=[jax.ShapeDtypeStruct((L, KV), BF16), jax.ShapeDtypeStruct((L, KV), BF16)],
        compiler_params=_cparams("parallel"),
    )(qk, uv, qk, qk, qk, dap, dap, dap, lse, lse, lse, rd, rd, rd, cos, sin)


def _halo_specs(T, L, W, col):
    per = T // HALO
    return [pl.BlockSpec((HALO, W), lambda i: (jnp.maximum(i * per - 1, 0), col)),
            pl.BlockSpec((T, W), lambda i: (i, col)),
            pl.BlockSpec((HALO, W), lambda i: (jnp.minimum((i + 1) * per, L // HALO - 1), col))]


def _fill_halo_buf(buf, prev_ref, cur_ref, next_ref, i, nt, T):
    buf[0:HALO, :] = jnp.where(i > 0, prev_ref[...], 0.0)
    buf[HALO:HALO + T, :] = cur_ref[...]
    buf[HALO + T:2 * HALO + T, :] = jnp.where(i < nt - 1, next_ref[...], 0.0)


def _counts(t, w, L):
    lo = jnp.clip(t - w // 2, 0, L)
    hi = jnp.clip(t - w // 2 + w, 0, L)
    return jnp.maximum(hi - lo, 1).astype(F32)


def _pool_fwd(u, pw, scale):
    L, P = u.shape[0], scale.shape[1]
    gd = P // POOL_GROUPS
    T = _tile(L, 256, 8)
    nt = L // T

    def body(up_ref, uc_ref, un_ref, pw_ref, sc_ref, out_ref, pooled_ref, buf):
        i = pl.program_id(0)
        _fill_halo_buf(buf, up_ref, uc_ref, un_ref, i, nt, T)
        t = i * T + lax.broadcasted_iota(I32, (T, 1), 0)
        for g, w in enumerate(POOL_WINDOWS):
            cols = slice(g * gd, (g + 1) * gd)
            acc = buf[pl.ds(HALO - w // 2, T), cols]
            for o in range(-w // 2 + 1, w // 2):
                acc = acc + buf[pl.ds(HALO + o, T), cols]
            pooled = (acc / _counts(t, w, L) - buf[pl.ds(HALO, T), cols]).astype(BF16)
            pooled_ref[:, cols] = pooled
            out_ref[:, cols] = (_dot(pooled, pw_ref[g]) * sc_ref[:, cols]).astype(BF16)

    return pl.pallas_call(
        body, name="pool_fwd", grid=(nt,),
        in_specs=_halo_specs(T, L, P, 0) + [pl.BlockSpec((POOL_GROUPS, gd, gd), lambda i: (0, 0, 0)), pl.BlockSpec((1, P), lambda i: (0, 0))],
        out_specs=[pl.BlockSpec((T, P), lambda i: (i, 0)), pl.BlockSpec((T, P), lambda i: (i, 0))],
        out_shape=[jax.ShapeDtypeStruct((L, P), BF16), jax.ShapeDtypeStruct((L, P), BF16)],
        scratch_shapes=[pltpu.VMEM((T + 2 * HALO, P), F32)],
        compiler_params=_cparams("parallel"),
    )(u, u, u, pw, scale)


def _pool_bwd_mix(dap, pooled, pw, scale, pcol):
    L, P = pooled.shape
    gd = P // POOL_GROUPS
    T = _tile(L, 256, 8)

    def body(dp_ref, pooled_ref, pw_ref, sc_ref, dpooled_ref, dpw_ref, dsc_ref):
        i = pl.program_id(0)

        @pl.when(i == 0)
        def _():
            dpw_ref[...] = jnp.zeros_like(dpw_ref)
            dsc_ref[...] = jnp.zeros_like(dsc_ref)

        for g in range(POOL_GROUPS):
            cols = slice(g * gd, (g + 1) * gd)
            pb = pooled_ref[:, cols]
            dp = dp_ref[:, cols]
            dsc_ref[:, cols] += jnp.sum(dp * _dot(pb, pw_ref[g]), axis=0, keepdims=True)
            dm = (dp * sc_ref[:, cols]).astype(BF16)
            dpw_ref[g] += _dot_tn(pb, dm)
            dpooled_ref[:, cols] = _dot_nt(dm, pw_ref[g])

    return pl.pallas_call(
        body, name="pool_bwd_mix", grid=(L // T,),
        in_specs=[pl.BlockSpec((T, P), lambda i: (i, pcol)), pl.BlockSpec((T, P), lambda i: (i, 0)),
                  pl.BlockSpec((POOL_GROUPS, gd, gd), lambda i: (0, 0, 0)), pl.BlockSpec((1, P), lambda i: (0, 0))],
        out_specs=[pl.BlockSpec((T, P), lambda i: (i, 0)), pl.BlockSpec((POOL_GROUPS, gd, gd), lambda i: (0, 0, 0)),
                   pl.BlockSpec((1, P), lambda i: (0, 0))],
        out_shape=[jax.ShapeDtypeStruct((L, P), F32), jax.ShapeDtypeStruct((POOL_GROUPS, gd, gd), F32), jax.ShapeDtypeStruct((1, P), F32)],
        compiler_params=_cparams("arbitrary"),
    )(dap, pooled, pw, scale)


def _pool_bwd_window(dpooled):
    L, P = dpooled.shape
    gd = P // POOL_GROUPS
    T = _tile(L, 256, 8)
    nt = L // T

    def body(dp_ref, dc_ref, dn_ref, du_ref, buf):
        i = pl.program_id(0)
        _fill_halo_buf(buf, dp_ref, dc_ref, dn_ref, i, nt, T)
        t = i * T - HALO + lax.broadcasted_iota(I32, (T + 2 * HALO, 1), 0)
        for g, w in enumerate(POOL_WINDOWS):
            cols = slice(g * gd, (g + 1) * gd)
            buf[:, cols] = buf[:, cols] / _counts(t, w, L)
            acc = buf[pl.ds(HALO - w // 2 + 1, T), cols]
            for o in range(-w // 2 + 2, w // 2 + 1):
                acc = acc + buf[pl.ds(HALO + o, T), cols]
            du_ref[:, cols] = (acc - dc_ref[:, cols]).astype(BF16)

    return pl.pallas_call(
        body, name="pool_bwd_window", grid=(nt,),
        in_specs=_halo_specs(T, L, P, 0),
        out_specs=pl.BlockSpec((T, P), lambda i: (i, 0)),
        out_shape=jax.ShapeDtypeStruct((L, P), BF16),
        scratch_shapes=[pltpu.VMEM((T + 2 * HALO, P), F32)],
        compiler_params=_cparams("parallel"),
    )(dpooled, dpooled, dpooled)


def _sum_rows(name, a):
    R, N = a.shape

    def body(a_ref, o_ref):
        if R <= 16:
            acc = a_ref[0:1, :]
            for r in range(1, R):
                acc = acc + a_ref[r:r + 1, :]
        else:
            acc = jnp.sum(a_ref[...], axis=0, keepdims=True)
        o_ref[...] = acc

    return pl.pallas_call(body, name=name, out_shape=jax.ShapeDtypeStruct((1, N), F32))(a)


def _silu_grad_mul(cv, g):
    def body(c_ref, g_ref, o_ref):
        cvv = c_ref[...]
        s = 1.0 / (1.0 + jnp.exp(-cvv))
        o_ref[...] = g_ref[...] * (s * (1.0 + cvv * (1.0 - s)))

    return pl.pallas_call(body, name="silu_grad_mul", out_shape=jax.ShapeDtypeStruct(cv.shape, F32))(cv, g)


def _adamw(name, w, g, m, v):
    R, C = w.shape
    parts = g.ndim == 3
    n_parts = g.shape[0] if parts else 1
    T = _tile(R, max(8, 262144 // C), 8)

    def body(w_ref, g_ref, m_ref, v_ref, go_ref, d_ref, mo_ref, vo_ref):
        if parts:
            gv = g_ref[0].astype(F32)
            for p in range(1, n_parts):
                gv = gv + g_ref[p].astype(F32)
        else:
            gv = g_ref[...]
        mn = ADAM_B1 * m_ref[...] + (1.0 - ADAM_B1) * gv
        vn = ADAM_B2 * v_ref[...] + (1.0 - ADAM_B2) * (gv * gv)
        m_hat = mn / (1.0 - ADAM_B1 ** ADAM_STEP)
        v_hat = vn / (1.0 - ADAM_B2 ** ADAM_STEP)
        go_ref[...] = gv
        d_ref[...] = -ADAM_LR * (m_hat / (jnp.sqrt(v_hat) + ADAM_EPS) + ADAM_WD * w_ref[...])
        mo_ref[...] = mn
        vo_ref[...] = vn

    tile = pl.BlockSpec((T, C), lambda i: (i, 0))
    g_spec = pl.BlockSpec((n_parts, T, C), lambda i: (0, i, 0)) if parts else tile
    return pl.pallas_call(
        body, name=name, grid=(R // T,),
        in_specs=[tile, g_spec, tile, tile], out_specs=[tile] * 4,
        out_shape=[jax.ShapeDtypeStruct((R, C), F32)] * 4,
        compiler_params=_cparams("parallel"),
    )(w, g, m, v)


def _dev_index(px, py, pc):
    return 4 * px + 2 * py + pc


def _all_gather(name, arrs):
    n = len(arrs)

    def body(*refs):
        ins, outs = refs[:n], refs[n:2 * n]
        send_sems, recv_sems, local_sems = refs[2 * n:]
        x, y, c = lax.axis_index("x"), lax.axis_index("y"), lax.axis_index("c")
        me, sibling = (x, y, c), (x, y, 1 - c)
        chips = [(1 - x, y), (x, 1 - y), (1 - x, 1 - y)]

        def copy(a, k, block, to, src=None):
            slot = outs[a].at[_dev_index(*block)]
            return pltpu.make_async_remote_copy(
                src_ref=slot if src is None else src, dst_ref=slot, send_sem=send_sems.at[a, k], recv_sem=recv_sems.at[a, k],
                device_id=to, device_id_type=MESH)

        mine = [pltpu.make_async_copy(ins[a], outs[a].at[_dev_index(*me)], local_sems.at[a]) for a in range(n)]
        for cp in mine:
            cp.start()
        first = []
        for a in range(n):
            first.append(copy(a, 0, me, sibling, src=ins[a]))
            first += [copy(a, 1 + j, me, (*chip, c), src=ins[a]) for j, chip in enumerate(chips)]
        for cp in first:
            cp.start()
        passed = []
        for j, chip in enumerate(chips):
            for a in range(n):
                copy(a, 1 + j, (*chip, c), me).wait_recv()
                fwd = copy(a, 4 + j, (*chip, c), sibling)
                fwd.start()
                passed.append(fwd)
        for a in range(n):
            copy(a, 0, sibling, me).wait_recv()
            for j, chip in enumerate(chips):
                copy(a, 4 + j, (*chip, 1 - c), me).wait_recv()
        for cp in first + passed:
            cp.wait_send()
        for cp in mine:
            cp.wait()

    return pl.pallas_call(
        body, name=name,
        in_specs=[HBM] * n, out_specs=[HBM] * n,
        out_shape=[jax.ShapeDtypeStruct((N_DEV, *a.shape), a.dtype) for a in arrs],
        scratch_shapes=[pltpu.SemaphoreType.DMA((n, N_DEV - 1)), pltpu.SemaphoreType.DMA((n, N_DEV - 1)), pltpu.SemaphoreType.DMA((n,))],
    )(*arrs)


def _peers():
    x, y, c = lax.axis_index("x"), lax.axis_index("y"), lax.axis_index("c")
    flips = [(dx, dy, dc) for dx in (0, 1) for dy in (0, 1) for dc in (0, 1)][1:]
    return _dev_index(x, y, c), [(1 - x if dx else x, 1 - y if dy else y, 1 - c if dc else c) for dx, dy, dc in flips]


def _exchange_copy(kind, src_ref, land_ref, send_sems, recv_sems, k, peer, slot):
    src = src_ref if kind == "gather" else src_ref.at[_dev_index(*peer)]
    return pltpu.make_async_remote_copy(
        src_ref=src, dst_ref=land_ref.at[slot], send_sem=send_sems.at[k], recv_sem=recv_sems.at[k],
        device_id=peer, device_id_type=MESH)


def _exchange_start(name, kind, srcs):
    n = len(srcs)
    lands = [lax.empty((N_DEV, *s.shape) if kind == "gather" else s.shape, s.dtype) for s in srcs]

    def body(*refs):
        src_refs, land_refs = refs[:n], refs[n:2 * n]
        send_sems, recv_sems, token = refs[2 * n:3 * n], refs[3 * n:4 * n], refs[-1]
        me, peers = _peers()
        for a in range(n):
            for k, peer in enumerate(peers):
                _exchange_copy(kind, src_refs[a], land_refs[a], send_sems[a], recv_sems[a], k, peer, me).start()
        token[...] = jnp.zeros_like(token)

    sems = [pltpu.SemaphoreType.DMA((N_DEV - 1,))] * (2 * n)
    outs = pl.pallas_call(
        body, name=name,
        out_shape=sems + [pltpu.HBM(s.shape, s.dtype) for s in srcs] + [pltpu.HBM(l.shape, l.dtype) for l in lands]
                  + [jax.ShapeDtypeStruct((8, LANES), F32)],
        in_specs=[HBM] * (2 * n),
        out_specs=[SEM] * (2 * n) + [HBM] * (2 * n) + [pl.BlockSpec(memory_space=pltpu.VMEM)],
        input_output_aliases={i: 2 * n + i for i in range(2 * n)},
        compiler_params=SIDE_EFFECT,
    )(*[pltpu.with_memory_space_constraint(s, pltpu.HBM) for s in srcs],
      *[pltpu.with_memory_space_constraint(l, pltpu.HBM) for l in lands])
    return outs[:n], outs[n:2 * n], outs[2 * n:3 * n], outs[3 * n:4 * n], outs[4 * n]


def _exchange_wait(name, kind, send_sems, recv_sems, srcs, lands, after):
    n = len(srcs)

    def body(*refs):
        src_refs, land_refs = refs[:n], refs[n:2 * n]
        send_refs, recv_refs = refs[2 * n:3 * n], refs[3 * n:4 * n]
        _, peers = _peers()
        for a in range(n):
            for k, peer in enumerate(peers):
                cp = _exchange_copy(kind, src_refs[a], land_refs[a], send_refs[a], recv_refs[a], k, peer, _dev_index(*peer))
                cp.wait_send()
                cp.wait_recv()

    outs = pl.pallas_call(
        body, name=name,
        out_shape=[pltpu.HBM(s.shape, s.dtype) for s in srcs] + [pltpu.HBM(l.shape, l.dtype) for l in lands],
        in_specs=[HBM] * (2 * n) + [SEM] * (2 * n) + [pl.BlockSpec(memory_space=pl.ANY)],
        out_specs=[HBM] * (2 * n),
        input_output_aliases={i: i for i in range(2 * n)},
        compiler_params=SIDE_EFFECT,
    )(*srcs, *lands, *send_sems, *recv_sems, after)
    return outs[n:]


def _with_own(kind, land, src, me):
    own = src[None] if kind == "gather" else lax.dynamic_index_in_dim(src, me, 0, keepdims=True)
    return lax.dynamic_update_slice_in_dim(land, own, me, 0)


def _shards_to_cols(g):
    return jnp.transpose(g, (1, 0, 2)).reshape(g.shape[1], N_DEV * g.shape[2])


def _cols_to_shards(a):
    R, Ctot = a.shape
    return jnp.transpose(a.reshape(R, N_DEV, Ctot // N_DEV), (1, 0, 2))


def kernel(x, c, ctx, c_ctx, norm_attn_w, norm_mlp_w, w_ada, b_ada, w_in, attn_sink, pool_w, pool_scale, w_out, w_mlp_up, w_mlp_down, final_norm_w, loss_target, m_c_ctx, m_norm_attn_w, m_norm_mlp_w, m_w_ada, m_b_ada, m_w_in, m_attn_sink, m_pool_w, m_pool_scale, m_w_out, m_w_mlp_up, m_w_mlp_down, m_final_norm_w, v_c_ctx, v_norm_attn_w, v_norm_mlp_w, v_w_ada, v_b_ada, v_w_in, v_attn_sink, v_pool_w, v_pool_scale, v_w_out, v_w_mlp_up, v_w_mlp_down, v_final_norm_w):
    _, L, D = x.shape
    H = attn_sink.shape[1]
    A = H * HEAD_DIM
    KV = A // GQA
    P = pool_scale.shape[1]
    MODW = N_MOD * D
    ws = MODW // N_DEV
    gd = P // POOL_GROUPS
    me = _dev_index(lax.axis_index("x"), lax.axis_index("y"), lax.axis_index("c"))

    x2d, ctx2d, tgt = x[0], ctx[0], loss_target[0]
    cctx_row = c_ctx.reshape(1, D)
    wf_row = final_norm_w.reshape(1, D)
    w_ada_l = w_ada[0]
    pool_w_l = pool_w[0].reshape(POOL_GROUPS * (gd // N_DEV), gd)

    w_srcs = [c, w_in[0].astype(BF16), w_out[0].astype(BF16), pool_w_l.astype(BF16), w_mlp_up[0].astype(BF16), w_mlp_down[0].astype(BF16)]
    n_early = 4
    gw_send, gw_recv, gw_src, gw_land, gw_token = _exchange_start("gather_weights_start", "gather", w_srcs)
    early = _exchange_wait("gather_weights_wait_early", "gather", gw_send[:n_early], gw_recv[:n_early], gw_src[:n_early],
                           gw_land[:n_early], gw_token)
    c_all, win_g, wout_g, pw_g = [_with_own("gather", l, s, me) for l, s in zip(early, w_srcs)]
    W_in = _shards_to_cols(win_g)
    W_out = wout_g.reshape(A + P, D)
    PW = jnp.transpose(pw_g.reshape(N_DEV, POOL_GROUPS, gd // N_DEV, gd), (1, 0, 2, 3)).reshape(POOL_GROUPS, gd, gd)
    W_qk, W_kv = W_in[:, :A + KV], W_in[:, A:A + 2 * KV]
    W_uv = jnp.concatenate([W_in[:, A + 2 * KV:], W_in[:, A + KV:A + 2 * KV]], axis=1)

    cond = jnp.concatenate([c_all[:, 0, :], cctx_row, jnp.zeros((COND_ROWS - N_DEV - 1, D), F32)], axis=0)
    b_sh = lax.dynamic_slice_in_dim(b_ada, me * ws, ws, axis=1)
    (mods_sh,) = _mm("ada_mod", cond, w_ada_l, "nn", [F32], SMALL_TILES, a_pre=_silu, extras=[("n", b_sh)], epilogue=lambda acc, b: (acc + b,))
    (mods_g,) = _all_gather("gather_mods", [mods_sh])
    mods = _shards_to_cols(mods_g)
    mod_b = lax.dynamic_slice_in_dim(mods, me, 1, axis=0)
    sh_a, sc_a, g_a, sh_m, sc_m, g_m = [mod_b[:, i * D:(i + 1) * D] for i in range(N_MOD)]
    csh_a, csc_a = mods[N_DEV:N_DEV + 1, :D], mods[N_DEV:N_DEV + 1, D:2 * D]

    cos, sin = _rope_tables(L)
    h, r1 = _norm_fwd("norm_attn", x2d, norm_attn_w, sc_a, sh_a)
    hc, rc = _norm_fwd("norm_attn_ctx", ctx2d, norm_attn_w, csc_a, csh_a)
    (qk,) = _mm("in_proj_qk", h, W_qk, "nn", [BF16], (1024, A + KV, D), extras=[("m", cos), ("m", sin)],
                epilogue=lambda acc, cs, sn: (_rope(acc, cs, sn),))
    (uv,) = _mm("in_proj_uv", h, W_uv, "nn", [F32], (1024, P + KV, D))
    (kvc,) = _mm("in_proj_ctx", hc, W_kv, "nn", [BF16], SMALL_TILES)
    attn, lse = _attn_fwd(qk, uv, kvc, attn_sink, A, KV, P)
    pool_out, pooled = _pool_fwd(uv, PW, pool_scale)
    ap = jnp.concatenate([attn, pool_out], axis=1)
    o, x1 = _mm("out_proj", ap, W_out, "nn", [F32, F32], (512, 1024, 2048), extras=[("mn", x2d), ("n", g_a)], epilogue=lambda acc, xr, g: (acc, xr + g * acc))
    late = _exchange_wait("gather_weights_wait_late", "gather", gw_send[n_early:], gw_recv[n_early:], gw_src[n_early:],
                          gw_land[n_early:], x1)
    wup_g, wdown_g = [_with_own("gather", l, s, me) for l, s in zip(late, w_srcs[n_early:])]
    W_up = _shards_to_cols(wup_g)
    W_down = wdown_g.reshape(-1, D)
    hm, r2 = _norm_fwd("norm_mlp", x1, norm_mlp_w, sc_m, sh_m)
    up, act = _mm("mlp_up", hm, W_up, "nn", [F32, BF16], (1024, 1024, 2048), epilogue=lambda acc: (acc, _relu2(acc)))
    mlp, x2 = _mm("mlp_down", act, W_down, "nn", [F32, F32], (1024, 1024, 1024), extras=[("mn", x1), ("n", g_m)], epilogue=lambda acc, xr, g: (acc, xr + g * acc))
    d_x2, d_mlp, d_wf, d_gm, loss_p = _final(x2, tgt, mlp, wf_row, g_m)

    (d_up,) = _mm("mlp_down_bwd_act", d_mlp, W_down, "nt", [BF16], (1024, 1024, 2048), extras=[("mn", up)], epilogue=lambda acc, uu: (acc * (2.0 * jnp.maximum(uu, 0.0)),))
    (gW_down,) = _mm("mlp_down_bwd_w", act, d_mlp, "tn", [BF16], (1024, 2048, 1024))
    (gW_up,) = _mm("mlp_up_bwd_w", hm, d_up, "tn", [BF16], (1024, 2048, 1024))
    (d_hm,) = _mm("mlp_up_bwd_act", d_up, W_up, "nt", [F32], (1024, 1024, 2048))
    g_mlp_srcs = [_cols_to_shards(gW_up), gW_down.reshape(N_DEV, -1, D)]
    g_mlp = _exchange_start("grads_mlp_start", "all_to_all", g_mlp_srcs)
    zrow = jnp.zeros((1, D), F32) + g_mlp[4][0, 0]
    d_x1, s_sh_m, s_sc_m, s_w_nm, d_ga, d_o = _norm_bwd("norm_mlp_bwd", x1, r2, d_hm, d_x2, norm_mlp_w, sc_m, zrow, gate=(o, g_a))

    (d_ap,) = _mm("out_proj_bwd_act", d_o, W_out, "nt", [F32], (1024, 1024, 2048))
    (gW_out,) = _mm("out_proj_bwd_w", ap, d_o, "tn", [BF16], (1024, 2048, 1024))
    d_pooled, gPW, d_pscale = _pool_bwd_mix(d_ap, pooled, PW, pool_scale, A // P)
    gpw_s = jnp.transpose(gPW.astype(BF16).reshape(POOL_GROUPS, N_DEV, gd // N_DEV, gd), (1, 0, 2, 3)).reshape(N_DEV, -1, gd)
    g_mix_srcs = [gW_out.reshape(N_DEV, (A + P) // N_DEV, D), gpw_s]
    g_mix = _exchange_start("grads_mix_start", "all_to_all", g_mix_srcs)
    lse = lse + g_mix[4][0, 0]
    d_u = _pool_bwd_window(d_pooled)
    d_q, rd, dsink_rows, d_kvc = _attn_bwd_dq(qk, uv, kvc, attn_sink, d_ap, lse, cos, sin, A, KV, P)
    d_k, d_v = _attn_bwd_dkv(qk, uv, d_ap, lse, rd, cos, sin, A, KV, P)
    d_sink = _sum_rows("sink_grad", dsink_rows)
    d_p = jnp.concatenate([d_q, d_k, d_v, d_u], axis=1)
    d_kvc_b = d_kvc.astype(BF16)
    (gW_kv_ctx,) = _mm("in_proj_ctx_bwd_w", hc, d_kvc_b, "tn", [F32], SMALL_TILES)
    (d_hc,) = _mm("in_proj_ctx_bwd_act", d_kvc_b, W_kv, "nt", [F32], SMALL_TILES)
    gW_in_init = jnp.pad(gW_kv_ctx, ((0, 0), (A, P)))
    (gW_in,) = _mm("in_proj_bwd_w", h, d_p, "tn", [BF16], (1024, 1280, 1024), extras=[("mn", gW_in_init)], epilogue=lambda acc, init: (acc + init,))
    g_in_srcs = [_cols_to_shards(gW_in)]
    g_in = _exchange_start("grads_in_start", "all_to_all", g_in_srcs)
    (d_h,) = _mm("in_proj_bwd_act", d_p, W_in, "nt", [F32], (1024, 1024, 2560))
    grad_x, s_sh_a, s_sc_a, s_w_na = _norm_bwd("norm_attn_bwd", x2d, r1, d_h, d_x1, norm_attn_w, sc_a, zrow + g_in[4][0, 0])
    _, s_csh, s_csc, s_w_na = _norm_bwd("norm_attn_ctx_bwd", ctx2d, rc, d_hc, jnp.zeros_like(ctx2d), norm_attn_w, csc_a, s_w_na)

    pad_l = lambda a: jnp.pad(a, ((0, 0), (0, LANES - a.shape[1])))
    d_mod_b = jnp.concatenate([s_sh_a, s_sc_a, d_ga, s_sh_m, s_sc_m, d_gm], axis=1)
    summed = jnp.concatenate([s_csh, s_csc, s_w_na, s_w_nm, d_wf, d_pscale, pad_l(d_sink), pad_l(loss_p)], axis=1)
    (small_g,) = _all_gather("gather_small", [jnp.concatenate([d_mod_b, summed], axis=1)])
    small_g = small_g[:, 0, :]
    tot = _sum_rows("small_sum", small_g[:, MODW:])
    off = [0]
    for wdt in (D, D, D, D, D, P, LANES, LANES):
        off.append(off[-1] + wdt)
    seg = lambda i: tot[:, off[i]:off[i + 1]]
    g_norm_attn, g_norm_mlp, g_final, g_pscale = seg(2), seg(3), seg(4), seg(5)
    g_sink, loss = seg(6)[:, :H], seg(7)[0, 0]
    d_mod_ctx = jnp.concatenate([seg(0), seg(1), jnp.zeros((1, MODW - 2 * D), F32)], axis=1)
    d_mod = jnp.concatenate([small_g[:, :MODW], d_mod_ctx, jnp.zeros((COND_ROWS - N_DEV - 1, MODW), F32)], axis=0)
    g_b_ada = _sum_rows("b_ada_grad", d_mod[:N_DEV + 1])
    d_mod_sh = lax.dynamic_slice_in_dim(d_mod, me * ws, ws, axis=1)
    (g_w_ada,) = _mm("ada_bwd_w", cond, d_mod_sh, "tn", [F32], SMALL_TILES, a_pre=_silu)
    (d_cond_p,) = _mm("ada_bwd_cond", d_mod_sh, w_ada_l, "nt", [F32], SMALL_TILES)
    (d_cctx_g,) = _all_gather("gather_cctx", [d_cond_p[N_DEV:N_DEV + 1]])
    g_c_ctx = _silu_grad_mul(cctx_row, _sum_rows("cctx_sum", d_cctx_g[:, 0, :]))

    def arrived(name, started, srcs):
        lands = _exchange_wait(name, "all_to_all", started[0], started[1], started[2], started[3], g_c_ctx)
        return [_with_own("all_to_all", l, s, me) for l, s in zip(lands, srcs)]

    r_up, r_down = arrived("grads_mlp_wait", g_mlp, g_mlp_srcs)
    r_out, r_pw = arrived("grads_mix_wait", g_mix, g_mix_srcs)
    (r_in,) = arrived("grads_in_wait", g_in, g_in_srcs)

    results = {
        "c_ctx": _adamw("adam_c_ctx", cctx_row, g_c_ctx, m_c_ctx.reshape(1, D), v_c_ctx.reshape(1, D)),
        "norm_attn_w": _adamw("adam_norm_attn", norm_attn_w, g_norm_attn, m_norm_attn_w, v_norm_attn_w),
        "norm_mlp_w": _adamw("adam_norm_mlp", norm_mlp_w, g_norm_mlp, m_norm_mlp_w, v_norm_mlp_w),
        "w_ada": _adamw("adam_w_ada", w_ada_l, g_w_ada, m_w_ada[0], v_w_ada[0]),
        "b_ada": _adamw("adam_b_ada", b_ada, g_b_ada, m_b_ada, v_b_ada),
        "w_in": _adamw("adam_w_in", w_in[0], r_in, m_w_in[0], v_w_in[0]),
        "attn_sink": _adamw("adam_sink", attn_sink, g_sink, m_attn_sink, v_attn_sink),
        "pool_w": _adamw("adam_pool_w", pool_w_l, r_pw, m_pool_w[0].reshape(pool_w_l.shape), v_pool_w[0].reshape(pool_w_l.shape)),
        "pool_scale": _adamw("adam_pool_scale", pool_scale, g_pscale, m_pool_scale, v_pool_scale),
        "w_out": _adamw("adam_w_out", w_out[0], r_out, m_w_out[0], v_w_out[0]),
        "w_mlp_up": _adamw("adam_w_up", w_mlp_up[0], r_up, m_w_mlp_up[0], v_w_mlp_up[0]),
        "w_mlp_down": _adamw("adam_w_down", w_mlp_down[0], r_down, m_w_mlp_down[0], v_w_mlp_down[0]),
        "final_norm_w": _adamw("adam_final_norm", wf_row, g_final, m_final_norm_w.reshape(1, D), v_final_norm_w.reshape(1, D)),
    }
    shapes = {"c_ctx": c_ctx.shape, "norm_attn_w": norm_attn_w.shape, "norm_mlp_w": norm_mlp_w.shape, "w_ada": w_ada.shape,
              "b_ada": b_ada.shape, "w_in": w_in.shape, "attn_sink": attn_sink.shape, "pool_w": pool_w.shape,
              "pool_scale": pool_scale.shape, "w_out": w_out.shape, "w_mlp_up": w_mlp_up.shape, "w_mlp_down": w_mlp_down.shape,
              "final_norm_w": final_norm_w.shape}
    outs = [loss, grad_x.reshape(x.shape)]
    for part in range(4):
        outs += [results[name][part].reshape(shape) for name, shape in shapes.items()]
    return tuple(outs)
```

```python
import functools

import jax
import jax.numpy as jnp
from jax import lax
from jax.experimental import pallas as pl
from jax.experimental.pallas import tpu as pltpu

F32 = jnp.float32
BF16 = jnp.bfloat16
I32 = jnp.int32

HEAD_DIM = 64
GQA = 4
BLOCK = 128
GRID_W = 64
ROPE_BASE = 10000.0
POOL_WINDOWS = (2, 4, 8, 16)
POOL_GROUPS = len(POOL_WINDOWS)
HALO = 8
N_MOD = 6
EPS = 1e-6
NEG_INF = -1e30
ADAM_LR = 0.001
ADAM_B1 = 0.9
ADAM_B2 = 0.999
ADAM_EPS = 1e-08
ADAM_WD = 0.01
ADAM_STEP = 10
N_DEV = 8
COND_ROWS = 2 * N_DEV
LANES = 128
SUBLANES_16BIT = 16
VMEM_LIMIT = 48 * 1024 * 1024
SMALL_TILES = (512, 1024, 512)
MESH = pl.DeviceIdType.MESH
HBM = pl.BlockSpec(memory_space=pltpu.HBM)
SEM = pl.BlockSpec(memory_space=pltpu.SEMAPHORE)
SIDE_EFFECT = pltpu.CompilerParams(has_side_effects=pltpu.SideEffectType.DATAFLOW_SIDE_EFFECTING)


def _cparams(*sem):
    return pltpu.CompilerParams(dimension_semantics=sem, vmem_limit_bytes=VMEM_LIMIT)


def _tile(n, pref, align):
    if n <= pref:
        return n
    t = (pref // align) * align
    while t >= align:
        if n % t == 0:
            return t
        t -= align
    return n


def _dot(a, b):
    return lax.dot_general(a, b, (((1,), (0,)), ((), ())), preferred_element_type=F32)


def _dot_nt(a, b):
    return lax.dot_general(a, b, (((1,), (1,)), ((), ())), preferred_element_type=F32)


def _dot_tn(a, b):
    return lax.dot_general(a, b, (((0,), (0,)), ((), ())), preferred_element_type=F32)


_DOTS = {"nn": _dot, "nt": _dot_nt, "tn": _dot_tn}


def _mm(name, a, b, mode, out_dtypes, tiles, *, epilogue=None, extras=(), a_pre=None):
    if mode == "nn":
        (M, K), (K2, N) = a.shape, b.shape
    elif mode == "nt":
        (M, K), (N, K2) = a.shape, b.shape
    else:
        (K, M), (K2, N) = a.shape, b.shape
    assert K == K2, (name, a.shape, b.shape)
    tm = _tile(M, tiles[0], LANES if mode == "tn" else SUBLANES_16BIT)
    tn = _tile(N, tiles[1], LANES)
    tk = _tile(K, tiles[2], SUBLANES_16BIT if mode == "tn" else LANES)
    nk = K // tk
    n_ex, n_out = len(extras), len(out_dtypes)

    def product(a_ref, b_ref):
        at = a_ref[...]
        if a_pre is not None:
            at = a_pre(at)
        return _DOTS[mode](at.astype(BF16), b_ref[...].astype(BF16))

    def finish(acc, ex_refs, out_refs):
        outs = (acc,) if epilogue is None else epilogue(acc, *[r[...] for r in ex_refs])
        for o_ref, o in zip(out_refs, outs):
            o_ref[...] = o.astype(o_ref.dtype)

    def body_one(a_ref, b_ref, *rest):
        finish(product(a_ref, b_ref), rest[:n_ex], rest[n_ex:n_ex + n_out])

    def body_acc(a_ref, b_ref, *rest):
        ex_refs, out_refs, acc_ref = rest[:n_ex], rest[n_ex:n_ex + n_out], rest[-1]
        k = pl.program_id(2)

        @pl.when(k == 0)
        def _():
            acc_ref[...] = product(a_ref, b_ref)

        @pl.when(k > 0)
        def _():
            acc_ref[...] += product(a_ref, b_ref)

        @pl.when(k == nk - 1)
        def _():
            finish(acc_ref[...], ex_refs, out_refs)

    a_spec = pl.BlockSpec((tk, tm), lambda i, j, k: (k, i)) if mode == "tn" else pl.BlockSpec((tm, tk), lambda i, j, k: (i, k))
    b_spec = pl.BlockSpec((tn, tk), lambda i, j, k: (j, k)) if mode == "nt" else pl.BlockSpec((tk, tn), lambda i, j, k: (k, j))
    ex_specs = []
    for kind, arr in extras:
        if kind == "mn":
            ex_specs.append(pl.BlockSpec((tm, tn), lambda i, j, k: (i, j)))
        elif kind == "n":
            ex_specs.append(pl.BlockSpec((1, tn), lambda i, j, k: (0, j)))
        else:
            ex_specs.append(pl.BlockSpec((tm, arr.shape[1]), lambda i, j, k: (i, 0)))
    return pl.pallas_call(
        body_one if nk == 1 else body_acc,
        name=name,
        grid=(M // tm, N // tn, nk),
        in_specs=[a_spec, b_spec] + ex_specs,
        out_specs=[pl.BlockSpec((tm, tn), lambda i, j, k: (i, j)) for _ in out_dtypes],
        out_shape=[jax.ShapeDtypeStruct((M, N), d) for d in out_dtypes],
        scratch_shapes=[] if nk == 1 else [pltpu.VMEM((tm, tn), F32)],
        compiler_params=_cparams("parallel", "parallel", "arbitrary"),
    )(a, b, *[arr for _, arr in extras])


def _silu(v):
    return v / (1.0 + jnp.exp(-v))


def _relu2(v):
    r = jnp.maximum(v, 0.0)
    return r * r


def _rope_tables(L):
    half = HEAD_DIM // 2
    inv_freq = ROPE_BASE ** (-jnp.arange(0, half, 2, dtype=F32) / half)
    t = jnp.arange(L)
    row, col = t // GRID_W, t % GRID_W
    ang_r = row.astype(F32)[:, None] * inv_freq[None, :]
    ang_c = col.astype(F32)[:, None] * inv_freq[None, :]
    cos = jnp.concatenate([jnp.cos(ang_r), jnp.cos(ang_r), jnp.cos(ang_c), jnp.cos(ang_c)], axis=1)
    sin = jnp.concatenate([-jnp.sin(ang_r), jnp.sin(ang_r), -jnp.sin(ang_c), jnp.sin(ang_c)], axis=1)
    reps = LANES // HEAD_DIM
    return jnp.tile(cos, (1, reps)), jnp.tile(sin, (1, reps))


def _rope(xf, cos, sin):
    quarter = HEAD_DIM // 4
    lane = lax.broadcasted_iota(I32, (xf.shape[0], LANES), 1)
    first = (lane & quarter) == 0
    outs = []
    for j in range(xf.shape[1] // LANES):
        xc = xf[:, j * LANES:(j + 1) * LANES]
        partner = jnp.where(first, pltpu.roll(xc, LANES - quarter, 1), pltpu.roll(xc, quarter, 1))
        outs.append(xc * cos + partner * sin)
    return outs[0] if len(outs) == 1 else jnp.concatenate(outs, axis=1)


def _norm_fwd(name, x, w, sc, sh):
    L, D = x.shape
    T = _tile(L, 256, 8)

    def body(x_ref, w_ref, sc_ref, sh_ref, h_ref, r_ref):
        xf = x_ref[...]
        r = lax.rsqrt(jnp.mean(xf * xf, axis=-1, keepdims=True) + EPS)
        n = (xf * r) * w_ref[...]
        h_ref[...] = (n * (1.0 + sc_ref[...]) + sh_ref[...]).astype(BF16)
        r_ref[...] = r

    row = pl.BlockSpec((1, D), lambda i: (0, 0))
    return pl.pallas_call(
        body, name=name, grid=(L // T,),
        in_specs=[pl.BlockSpec((T, D), lambda i: (i, 0)), row, row, row],
        out_specs=[pl.BlockSpec((T, D), lambda i: (i, 0)), pl.BlockSpec((T, 1), lambda i: (i, 0))],
        out_shape=[jax.ShapeDtypeStruct((L, D), BF16), jax.ShapeDtypeStruct((L, 1), F32)],
        compiler_params=_cparams("parallel"),
    )(x, w, sc, sh)


def _norm_bwd(name, x, r, dh, dres, w, sc, w_init, gate=None):
    L, D = x.shape
    T = _tile(L, 256, 8)
    with_gate = gate is not None

    def body(x_ref, r_ref, dh_ref, dres_ref, w_ref, sc_ref, wi_ref, *rest):
        if with_gate:
            o_ref, g_ref, dx_ref, ssh_ref, ssc_ref, sw_ref, sg_ref, do_ref = rest
        else:
            dx_ref, ssh_ref, ssc_ref, sw_ref = rest
        i = pl.program_id(0)

        @pl.when(i == 0)
        def _():
            ssh_ref[...] = jnp.zeros_like(ssh_ref)
            ssc_ref[...] = jnp.zeros_like(ssc_ref)
            sw_ref[...] = wi_ref[...]
            if with_gate:
                sg_ref[...] = jnp.zeros_like(sg_ref)

        rr = r_ref[...]
        xh = x_ref[...] * rr
        dh = dh_ref[...]
        wv = w_ref[...]
        dn = dh * (1.0 + sc_ref[...])
        ssh_ref[...] += jnp.sum(dh, axis=0, keepdims=True)
        ssc_ref[...] += jnp.sum(dh * (xh * wv), axis=0, keepdims=True)
        sw_ref[...] += jnp.sum(dn * xh, axis=0, keepdims=True)
        dxh = dn * wv
        dx = dres_ref[...] + rr * (dxh - xh * jnp.mean(dxh * xh, axis=-1, keepdims=True))
        dx_ref[...] = dx
        if with_gate:
            sg_ref[...] += jnp.sum(dx * o_ref[...], axis=0, keepdims=True)
            do_ref[...] = (g_ref[...] * dx).astype(BF16)

    tile = pl.BlockSpec((T, D), lambda i: (i, 0))
    row = pl.BlockSpec((1, D), lambda i: (0, 0))
    in_specs = [tile, pl.BlockSpec((T, 1), lambda i: (i, 0)), tile, tile, row, row, row]
    out_specs = [tile, row, row, row]
    out_shape = [jax.ShapeDtypeStruct((L, D), F32)] + [jax.ShapeDtypeStruct((1, D), F32)] * 3
    args = [x, r, dh, dres, w, sc, w_init]
    if with_gate:
        in_specs += [tile, row]
        out_specs += [row, tile]
        out_shape += [jax.ShapeDtypeStruct((1, D), F32), jax.ShapeDtypeStruct((L, D), BF16)]
        args += list(gate)
    return pl.pallas_call(
        body, name=name, grid=(L // T,), in_specs=in_specs, out_specs=out_specs, out_shape=out_shape,
        compiler_params=_cparams("arbitrary"),
    )(*args)


def _final(x2, tgt, mlp, wf, gm):
    L, D = x2.shape
    T = _tile(L, 256, 8)

    def body(x_ref, t_ref, mlp_ref, wf_ref, gm_ref, dx_ref, dmlp_ref, dwf_ref, dgm_ref, loss_ref):
        i = pl.program_id(0)

        @pl.when(i == 0)
        def _():
            dwf_ref[...] = jnp.zeros_like(dwf_ref)
            dgm_ref[...] = jnp.zeros_like(dgm_ref)
            loss_ref[...] = jnp.zeros_like(loss_ref)

        xf = x_ref[...]
        r = lax.rsqrt(jnp.mean(xf * xf, axis=-1, keepdims=True) + EPS)
        xh = xf * r
        wv = wf_ref[...]
        err = xh * wv - t_ref[...]
        row_loss = jnp.mean(err * err, axis=-1, keepdims=True)
        loss_ref[...] += 0.5 * jnp.sum(row_loss, axis=0, keepdims=True)
        dy = err / D
        dwf_ref[...] += jnp.sum(dy * xh, axis=0, keepdims=True)
        dxh = dy * wv
        dx = r * (dxh - xh * jnp.mean(dxh * xh, axis=-1, keepdims=True))
        dx_ref[...] = dx
        dgm_ref[...] += jnp.sum(dx * mlp_ref[...], axis=0, keepdims=True)
        dmlp_ref[...] = (gm_ref[...] * dx).astype(BF16)

    tile = pl.BlockSpec((T, D), lambda i: (i, 0))
    row = pl.BlockSpec((1, D), lambda i: (0, 0))
    return pl.pallas_call(
        body, name="final_norm_loss", grid=(L // T,),
        in_specs=[tile, tile, tile, row, row],
        out_specs=[tile, tile, row, row, pl.BlockSpec((1, 1), lambda i: (0, 0))],
        out_shape=[jax.ShapeDtypeStruct((L, D), F32), jax.ShapeDtypeStruct((L, D), BF16),
                   jax.ShapeDtypeStruct((1, D), F32), jax.ShapeDtypeStruct((1, D), F32), jax.ShapeDtypeStruct((1, 1), F32)],
        compiler_params=_cparams("arbitrary"),
    )(x2, tgt, mlp, wf, gm)


def _heads(ref, first, n):
    return jnp.concatenate([ref[:, (first + g) * HEAD_DIM:(first + g + 1) * HEAD_DIM] for g in range(n)], axis=0)


def _put_col(tile, h, col):
    lane = lax.broadcasted_iota(I32, tile.shape, 1)
    return jnp.where(lane == h, col, tile)


def _get_col(tile, h):
    lane = lax.broadcasted_iota(I32, tile.shape, 1)
    return jnp.sum(jnp.where(lane == h, tile, 0.0), axis=1, keepdims=True)


def _attn_mask(n, L, C):
    shape = (GQA * BLOCK, 3 * BLOCK + C)
    qi = lax.broadcasted_iota(I32, shape, 0) & (BLOCK - 1)
    kj = lax.broadcasted_iota(I32, shape, 1)
    kpos = n * BLOCK - BLOCK + kj
    window = (kj >= qi) & (kj <= qi + 2 * BLOCK) & (kpos >= 0) & (kpos < L)
    return window | (kj >= 3 * BLOCK)


def _attn_specs(L, A, KV, C, vcol):
    nb = L // BLOCK
    kcol = A // KV
    prev = lambda n: jnp.maximum(n - 1, 0)
    nxt = lambda n: jnp.minimum(n + 1, nb - 1)
    q_spec = pl.BlockSpec((BLOCK, A), lambda n: (n, 0))
    k_specs = [pl.BlockSpec((BLOCK, KV), lambda n: (prev(n), kcol)), pl.BlockSpec((BLOCK, KV), lambda n: (n, kcol)),
               pl.BlockSpec((BLOCK, KV), lambda n: (nxt(n), kcol))]
    v_specs = [pl.BlockSpec((BLOCK, KV), lambda n: (prev(n), vcol)), pl.BlockSpec((BLOCK, KV), lambda n: (n, vcol)),
               pl.BlockSpec((BLOCK, KV), lambda n: (nxt(n), vcol))]
    kvc_spec = pl.BlockSpec((C, 2 * KV), lambda n: (0, 0))
    return q_spec, k_specs, v_specs, kvc_spec


def _keys_values(hk, k_refs, v_refs, kvc_ref, KV):
    sl = slice(hk * HEAD_DIM, (hk + 1) * HEAD_DIM)
    keys = jnp.concatenate([r[:, sl] for r in k_refs] + [kvc_ref[:, sl]], axis=0)
    vals = jnp.concatenate([r[:, sl].astype(BF16) for r in v_refs] + [kvc_ref[:, KV + hk * HEAD_DIM:KV + (hk + 1) * HEAD_DIM]], axis=0)
    return keys, vals


def _sink_col(sink_ref, hk):
    return jnp.concatenate([jnp.full((BLOCK, 1), sink_ref[0, hk * GQA + g], F32) for g in range(GQA)], axis=0)


def _attn_fwd(qk, uv, kvc, sink, A, KV, P):
    L = qk.shape[0]
    C = kvc.shape[0]
    nkv = KV // HEAD_DIM
    H = nkv * GQA
    scale = HEAD_DIM ** -0.5

    def body(sink_ref, q_ref, kp_ref, kc_ref, kn_ref, vp_ref, vc_ref, vn_ref, kvc_ref, o_ref, lse_ref):
        valid = _attn_mask(pl.program_id(0), L, C)
        lse_t = jnp.zeros((BLOCK, H), F32)
        for hk in range(nkv):
            keys, vals = _keys_values(hk, (kp_ref, kc_ref, kn_ref), (vp_ref, vc_ref, vn_ref), kvc_ref, KV)
            qs = _heads(q_ref, hk * GQA, GQA) * scale
            s = jnp.where(valid, _dot_nt(qs, keys), NEG_INF)
            sk = _sink_col(sink_ref, hk)
            m = jnp.maximum(jnp.max(s, axis=-1, keepdims=True), sk)
            p = jnp.exp(s - m)
            den = jnp.sum(p, axis=-1, keepdims=True) + jnp.exp(sk - m)
            o = _dot(p.astype(BF16), vals) * (1.0 / den)
            lse = m + jnp.log(den)
            o_ref[:, hk * GQA * HEAD_DIM:(hk + 1) * GQA * HEAD_DIM] = jnp.concatenate(
                [o[g * BLOCK:(g + 1) * BLOCK] for g in range(GQA)], axis=1).astype(BF16)
            for g in range(GQA):
                lse_t = _put_col(lse_t, hk * GQA + g, lse[g * BLOCK:(g + 1) * BLOCK])
        lse_ref[...] = lse_t

    q_spec, k_specs, v_specs, kvc_spec = _attn_specs(L, A, KV, C, P // KV)
    return pl.pallas_call(
        body, name="attn_fwd", grid=(L // BLOCK,),
        in_specs=[pl.BlockSpec(memory_space=pltpu.SMEM), q_spec] + k_specs + v_specs + [kvc_spec],
        out_specs=[pl.BlockSpec((BLOCK, A), lambda n: (n, 0)), pl.BlockSpec((BLOCK, H), lambda n: (n, 0))],
        out_shape=[jax.ShapeDtypeStruct((L, A), BF16), jax.ShapeDtypeStruct((L, H), F32)],
        compiler_params=_cparams("parallel"),
    )(sink, qk, qk, qk, qk, uv, uv, uv, kvc)


def _attn_bwd_dq(qk, uv, kvc, sink, dap, lse, cos, sin, A, KV, P):
    L = qk.shape[0]
    C = kvc.shape[0]
    nkv = KV // HEAD_DIM
    H = nkv * GQA
    scale = HEAD_DIM ** -0.5
    W = 3 * BLOCK

    def body(sink_ref, q_ref, kp_ref, kc_ref, kn_ref, vp_ref, vc_ref, vn_ref, kvc_ref, do_ref, lse_ref, cos_ref, sin_ref,
             dq_ref, rd_ref, ds_ref, dkvc_ref):
        n = pl.program_id(0)

        @pl.when(n == 0)
        def _():
            dkvc_ref[...] = jnp.zeros_like(dkvc_ref)

        valid = _attn_mask(n, L, C)
        lse_t = lse_ref[...]
        rd_t = jnp.zeros((BLOCK, H), F32)
        ds_t = jnp.zeros((BLOCK, H), F32)
        dq_parts = []
        for hk in range(nkv):
            sl = slice(hk * HEAD_DIM, (hk + 1) * HEAD_DIM)
            keys, vals = _keys_values(hk, (kp_ref, kc_ref, kn_ref), (vp_ref, vc_ref, vn_ref), kvc_ref, KV)
            qs = _heads(q_ref, hk * GQA, GQA) * scale
            dos = _heads(do_ref, hk * GQA, GQA).astype(BF16)
            lse = jnp.concatenate([_get_col(lse_t, hk * GQA + g) for g in range(GQA)], axis=0)
            p = jnp.exp(jnp.where(valid, _dot_nt(qs, keys), NEG_INF) - lse)
            dp = _dot_nt(dos, vals)
            rd = jnp.sum(p * dp, axis=-1, keepdims=True)
            ds = (p * (dp - rd)).astype(BF16)
            dq = _dot(ds, keys) * scale
            dkvc_ref[:, sl] += _dot_tn(ds[:, W:], qs)
            dkvc_ref[:, KV + hk * HEAD_DIM:KV + (hk + 1) * HEAD_DIM] += _dot_tn(p[:, W:].astype(BF16), dos)
            dsink = -(jnp.exp(_sink_col(sink_ref, hk) - lse) * rd)
            for g in range(GQA):
                rows = slice(g * BLOCK, (g + 1) * BLOCK)
                rd_t = _put_col(rd_t, hk * GQA + g, rd[rows])
                ds_t = _put_col(ds_t, hk * GQA + g, dsink[rows])
                dq_parts.append(dq[rows])
        rd_ref[...] = rd_t
        ds_ref[...] = ds_t
        dq_ref[...] = _rope(jnp.concatenate(dq_parts, axis=1), cos_ref[...], -sin_ref[...]).astype(BF16)

    q_spec, k_specs, v_specs, kvc_spec = _attn_specs(L, A, KV, C, P // KV)
    blk = lambda w: pl.BlockSpec((BLOCK, w), lambda n: (n, 0))
    return pl.pallas_call(
        body, name="attn_bwd_dq", grid=(L // BLOCK,),
        in_specs=[pl.BlockSpec(memory_space=pltpu.SMEM), q_spec] + k_specs + v_specs + [kvc_spec, blk(A), blk(H), blk(LANES), blk(LANES)],
        out_specs=[blk(A), blk(H), blk(H), pl.BlockSpec((C, 2 * KV), lambda n: (0, 0))],
        out_shape=[jax.ShapeDtypeStruct((L, A), BF16), jax.ShapeDtypeStruct((L, H), F32), jax.ShapeDtypeStruct((L, H), F32),
                   jax.ShapeDtypeStruct((C, 2 * KV), F32)],
        compiler_params=_cparams("arbitrary"),
    )(sink, qk, qk, qk, qk, uv, uv, uv, kvc, dap, lse, cos, sin)


def _attn_bwd_dkv(qk, uv, dap, lse_t, rd_t, cos, sin, A, KV, P):
    L = qk.shape[0]
    nb = L // BLOCK
    nkv = KV // HEAD_DIM
    H = nkv * GQA
    scale = HEAD_DIM ** -0.5
    R = 3 * GQA * BLOCK

    def body(k_ref, v_ref, qp_ref, qc_ref, qn_ref, dop_ref, doc_ref, don_ref, lsep_ref, lsec_ref, lsen_ref,
             rdp_ref, rdc_ref, rdn_ref, cos_ref, sin_ref, dk_ref, dv_ref):
        m = pl.program_id(0)
        kj = lax.broadcasted_iota(I32, (BLOCK, R), 0)
        col = lax.broadcasted_iota(I32, (BLOCK, R), 1)
        part = col // (GQA * BLOCK)
        qi = col & (BLOCK - 1)
        before = jnp.where(m >= 1, 0, -2 * BLOCK)
        after = jnp.where(m <= nb - 2, 0, 2 * BLOCK)
        valid = ((part == 0) & (kj <= qi + before)) | (part == 1) | ((part == 2) & (kj >= qi + after))
        dk_parts, dv_parts = [], []
        for hk in range(nkv):
            sl = slice(hk * HEAD_DIM, (hk + 1) * HEAD_DIM)
            km = k_ref[:, sl]
            vm = v_ref[:, sl].astype(BF16)
            qs = jnp.concatenate([_heads(q, hk * GQA, GQA) for q in (qp_ref, qc_ref, qn_ref)], axis=0) * scale
            dos = jnp.concatenate([_heads(d, hk * GQA, GQA) for d in (dop_ref, doc_ref, don_ref)], axis=0).astype(BF16)
            rows = [slice(hk * GQA + g, hk * GQA + g + 1) for g in range(GQA)]
            lse = jnp.concatenate([t[r, :] for t in (lsep_ref, lsec_ref, lsen_ref) for r in rows], axis=1)
            rdv = jnp.concatenate([t[r, :] for t in (rdp_ref, rdc_ref, rdn_ref) for r in rows], axis=1)
            p = jnp.exp(jnp.where(valid, _dot_nt(km, qs), NEG_INF) - lse)
            ds = (p * (_dot_nt(vm, dos) - rdv)).astype(BF16)
            dk_parts.append(_dot(ds, qs))
            dv_parts.append(_dot(p.astype(BF16), dos))
        dk = dk_parts[0] if nkv == 1 else jnp.concatenate(dk_parts, axis=1)
        dv = dv_parts[0] if nkv == 1 else jnp.concatenate(dv_parts, axis=1)
        dk_ref[...] = _rope(dk, cos_ref[...], -sin_ref[...]).astype(BF16)
        dv_ref[...] = dv.astype(BF16)

    prev = lambda m: jnp.maximum(m - 1, 0)
    nxt = lambda m: jnp.minimum(m + 1, nb - 1)
    three = lambda w: [pl.BlockSpec((BLOCK, w), lambda m: (prev(m), 0)), pl.BlockSpec((BLOCK, w), lambda m: (m, 0)),
                       pl.BlockSpec((BLOCK, w), lambda m: (nxt(m), 0))]
    three_t = [pl.BlockSpec((H, BLOCK), lambda m: (0, prev(m))), pl.BlockSpec((H, BLOCK), lambda m: (0, m)),
               pl.BlockSpec((H, BLOCK), lambda m: (0, nxt(m)))]
    blk = lambda w: pl.BlockSpec((BLOCK, w), lambda m: (m, 0))
    return pl.pallas_call(
        body, name="attn_bwd_dkv", grid=(nb,),
        in_specs=[pl.BlockSpec((BLOCK, KV), lambda m: (m, A // KV)), pl.BlockSpec((BLOCK, KV), lambda m: (m, P // KV))]
                 + three(A) + three(A) + three_t + three_t + [blk(LANES), blk(LANES)],
        out_specs=[blk(KV), blk(KV)],
        out_shape=[jax.ShapeDtypeStruct((L, KV), BF16), jax.ShapeDtypeStruct((L, KV), BF16)],
        compiler_params=_cparams("parallel"),
    )(qk, uv, qk, qk, qk, dap, dap, dap, lse_t, lse_t, lse_t, rd_t, rd_t, rd_t, cos, sin)


def _halo_specs(T, L, W, col):
    per = T // HALO
    return [pl.BlockSpec((HALO, W), lambda i: (jnp.maximum(i * per - 1, 0), col)),
            pl.BlockSpec((T, W), lambda i: (i, col)),
            pl.BlockSpec((HALO, W), lambda i: (jnp.minimum((i + 1) * per, L // HALO - 1), col))]


def _fill_halo_buf(buf, prev_ref, cur_ref, next_ref, i, nt, T):
    buf[0:HALO, :] = jnp.where(i > 0, prev_ref[...], 0.0)
    buf[HALO:HALO + T, :] = cur_ref[...]
    buf[HALO + T:2 * HALO + T, :] = jnp.where(i < nt - 1, next_ref[...], 0.0)


def _counts(t, w, L):
    lo = jnp.clip(t - w // 2, 0, L)
    hi = jnp.clip(t - w // 2 + w, 0, L)
    return jnp.maximum(hi - lo, 1).astype(F32)


def _pool_fwd(u, pw, scale):
    L, P = u.shape[0], scale.shape[1]
    gd = P // POOL_GROUPS
    T = _tile(L, 256, 8)
    nt = L // T

    def body(up_ref, uc_ref, un_ref, pw_ref, sc_ref, out_ref, pooled_ref, buf):
        i = pl.program_id(0)
        _fill_halo_buf(buf, up_ref, uc_ref, un_ref, i, nt, T)
        t = i * T + lax.broadcasted_iota(I32, (T, 1), 0)
        for g, w in enumerate(POOL_WINDOWS):
            cols = slice(g * gd, (g + 1) * gd)
            acc = buf[pl.ds(HALO - w // 2, T), cols]
            for o in range(-w // 2 + 1, w // 2):
                acc = acc + buf[pl.ds(HALO + o, T), cols]
            pooled = (acc / _counts(t, w, L) - buf[pl.ds(HALO, T), cols]).astype(BF16)
            pooled_ref[:, cols] = pooled
            out_ref[:, cols] = (_dot(pooled, pw_ref[g]) * sc_ref[:, cols]).astype(BF16)

    return pl.pallas_call(
        body, name="pool_fwd", grid=(nt,),
        in_specs=_halo_specs(T, L, P, 0) + [pl.BlockSpec((POOL_GROUPS, gd, gd), lambda i: (0, 0, 0)), pl.BlockSpec((1, P), lambda i: (0, 0))],
        out_specs=[pl.BlockSpec((T, P), lambda i: (i, 0)), pl.BlockSpec((T, P), lambda i: (i, 0))],
        out_shape=[jax.ShapeDtypeStruct((L, P), BF16), jax.ShapeDtypeStruct((L, P), BF16)],
        scratch_shapes=[pltpu.VMEM((T + 2 * HALO, P), F32)],
        compiler_params=_cparams("parallel"),
    )(u, u, u, pw, scale)


def _pool_bwd_mix(dap, pooled, pw, scale, pcol):
    L, P = pooled.shape
    gd = P // POOL_GROUPS
    T = _tile(L, 256, 8)

    def body(dp_ref, pooled_ref, pw_ref, sc_ref, dpooled_ref, dpw_ref, dsc_ref):
        i = pl.program_id(0)

        @pl.when(i == 0)
        def _():
            dpw_ref[...] = jnp.zeros_like(dpw_ref)
            dsc_ref[...] = jnp.zeros_like(dsc_ref)

        for g in range(POOL_GROUPS):
            cols = slice(g * gd, (g + 1) * gd)
            pb = pooled_ref[:, cols]
            dp = dp_ref[:, cols]
            dsc_ref[:, cols] += jnp.sum(dp * _dot(pb, pw_ref[g]), axis=0, keepdims=True)
            dm = (dp * sc_ref[:, cols]).astype(BF16)
            dpw_ref[g] += _dot_tn(pb, dm)
            dpooled_ref[:, cols] = _dot_nt(dm, pw_ref[g])

    return pl.pallas_call(
        body, name="pool_bwd_mix", grid=(L // T,),
        in_specs=[pl.BlockSpec((T, P), lambda i: (i, pcol)), pl.BlockSpec((T, P), lambda i: (i, 0)),
                  pl.BlockSpec((POOL_GROUPS, gd, gd), lambda i: (0, 0, 0)), pl.BlockSpec((1, P), lambda i: (0, 0))],
        out_specs=[pl.BlockSpec((T, P), lambda i: (i, 0)), pl.BlockSpec((POOL_GROUPS, gd, gd), lambda i: (0, 0, 0)),
                   pl.BlockSpec((1, P), lambda i: (0, 0))],
        out_shape=[jax.ShapeDtypeStruct((L, P), F32), jax.ShapeDtypeStruct((POOL_GROUPS, gd, gd), F32), jax.ShapeDtypeStruct((1, P), F32)],
        compiler_params=_cparams("arbitrary"),
    )(dap, pooled, pw, scale)


def _pool_bwd_window(dpooled):
    L, P = dpooled.shape
    gd = P // POOL_GROUPS
    T = _tile(L, 256, 8)
    nt = L // T

    def body(dp_ref, dc_ref, dn_ref, du_ref, buf):
        i = pl.program_id(0)
        _fill_halo_buf(buf, dp_ref, dc_ref, dn_ref, i, nt, T)
        t = i * T - HALO + lax.broadcasted_iota(I32, (T + 2 * HALO, 1), 0)
        for g, w in enumerate(POOL_WINDOWS):
            cols = slice(g * gd, (g + 1) * gd)
            buf[:, cols] = buf[:, cols] / _counts(t, w, L)
            acc = buf[pl.ds(HALO - w // 2 + 1, T), cols]
            for o in range(-w // 2 + 2, w // 2 + 1):
                acc = acc + buf[pl.ds(HALO + o, T), cols]
            du_ref[:, cols] = (acc - dc_ref[:, cols]).astype(BF16)

    return pl.pallas_call(
        body, name="pool_bwd_window", grid=(nt,),
        in_specs=_halo_specs(T, L, P, 0),
        out_specs=pl.BlockSpec((T, P), lambda i: (i, 0)),
        out_shape=jax.ShapeDtypeStruct((L, P), BF16),
        scratch_shapes=[pltpu.VMEM((T + 2 * HALO, P), F32)],
        compiler_params=_cparams("parallel"),
    )(dpooled, dpooled, dpooled)


def _sum_rows(name, a):
    R, N = a.shape

    def body(a_ref, o_ref):
        if R <= 16:
            acc = a_ref[0:1, :]
            for r in range(1, R):
                acc = acc + a_ref[r:r + 1, :]
        else:
            acc = jnp.sum(a_ref[...], axis=0, keepdims=True)
        o_ref[...] = acc

    return pl.pallas_call(body, name=name, out_shape=jax.ShapeDtypeStruct((1, N), F32))(a)


def _silu_grad_mul(cv, g):
    def body(c_ref, g_ref, o_ref):
        cvv = c_ref[...]
        s = 1.0 / (1.0 + jnp.exp(-cvv))
        o_ref[...] = g_ref[...] * (s * (1.0 + cvv * (1.0 - s)))

    return pl.pallas_call(body, name="silu_grad_mul", out_shape=jax.ShapeDtypeStruct(cv.shape, F32))(cv, g)


def _adamw(name, w, g, m, v):
    R, C = w.shape
    parts = g.ndim == 3
    n_parts = g.shape[0] if parts else 1
    T = _tile(R, max(8, 262144 // C), 8)

    def body(w_ref, g_ref, m_ref, v_ref, go_ref, d_ref, mo_ref, vo_ref):
        if parts:
            gv = g_ref[0].astype(F32)
            for p in range(1, n_parts):
                gv = gv + g_ref[p].astype(F32)
        else:
            gv = g_ref[...]
        mn = ADAM_B1 * m_ref[...] + (1.0 - ADAM_B1) * gv
        vn = ADAM_B2 * v_ref[...] + (1.0 - ADAM_B2) * (gv * gv)
        m_hat = mn / (1.0 - ADAM_B1 ** ADAM_STEP)
        v_hat = vn / (1.0 - ADAM_B2 ** ADAM_STEP)
        go_ref[...] = gv
        d_ref[...] = -ADAM_LR * (m_hat / (jnp.sqrt(v_hat) + ADAM_EPS) + ADAM_WD * w_ref[...])
        mo_ref[...] = mn
        vo_ref[...] = vn

    tile = pl.BlockSpec((T, C), lambda i: (i, 0))
    g_spec = pl.BlockSpec((n_parts, T, C), lambda i: (0, i, 0)) if parts else tile
    return pl.pallas_call(
        body, name=name, grid=(R // T,),
        in_specs=[tile, g_spec, tile, tile], out_specs=[tile] * 4,
        out_shape=[jax.ShapeDtypeStruct((R, C), F32)] * 4,
        compiler_params=_cparams("parallel"),
    )(w, g, m, v)


def _dev_index(px, py, pc):
    return 4 * px + 2 * py + pc


def _all_gather(name, arrs):
    n = len(arrs)

    def body(*refs):
        ins, outs = refs[:n], refs[n:2 * n]
        send_sems, recv_sems, local_sems = refs[2 * n:]
        x, y, c = lax.axis_index("x"), lax.axis_index("y"), lax.axis_index("c")
        me, sibling = (x, y, c), (x, y, 1 - c)
        chips = [(1 - x, y), (x, 1 - y), (1 - x, 1 - y)]

        def copy(a, k, block, to, src=None):
            slot = outs[a].at[_dev_index(*block)]
            return pltpu.make_async_remote_copy(
                src_ref=slot if src is None else src, dst_ref=slot, send_sem=send_sems.at[a, k], recv_sem=recv_sems.at[a, k],
                device_id=to, device_id_type=MESH)

        mine = [pltpu.make_async_copy(ins[a], outs[a].at[_dev_index(*me)], local_sems.at[a]) for a in range(n)]
        for cp in mine:
            cp.start()
        first = []
        for a in range(n):
            first.append(copy(a, 0, me, sibling, src=ins[a]))
            first += [copy(a, 1 + j, me, (*chip, c), src=ins[a]) for j, chip in enumerate(chips)]
        for cp in first:
            cp.start()
        passed = []
        for j, chip in enumerate(chips):
            for a in range(n):
                copy(a, 1 + j, (*chip, c), me).wait_recv()
                fwd = copy(a, 4 + j, (*chip, c), sibling)
                fwd.start()
                passed.append(fwd)
        for a in range(n):
            copy(a, 0, sibling, me).wait_recv()
            for j, chip in enumerate(chips):
                copy(a, 4 + j, (*chip, 1 - c), me).wait_recv()
        for cp in first + passed:
            cp.wait_send()
        for cp in mine:
            cp.wait()

    return pl.pallas_call(
        body, name=name,
        in_specs=[HBM] * n, out_specs=[HBM] * n,
        out_shape=[jax.ShapeDtypeStruct((N_DEV, *a.shape), a.dtype) for a in arrs],
        scratch_shapes=[pltpu.SemaphoreType.DMA((n, N_DEV - 1)), pltpu.SemaphoreType.DMA((n, N_DEV - 1)), pltpu.SemaphoreType.DMA((n,))],
    )(*arrs)


def _peers():
    x, y, c = lax.axis_index("x"), lax.axis_index("y"), lax.axis_index("c")
    flips = [(dx, dy, dc) for dx in (0, 1) for dy in (0, 1) for dc in (0, 1)][1:]
    return _dev_index(x, y, c), [(1 - x if dx else x, 1 - y if dy else y, 1 - c if dc else c) for dx, dy, dc in flips]


def _exchange_copy(kind, src_ref, land_ref, send_sems, recv_sems, k, peer, slot):
    src = src_ref if kind == "gather" else src_ref.at[_dev_index(*peer)]
    return pltpu.make_async_remote_copy(
        src_ref=src, dst_ref=land_ref.at[slot], send_sem=send_sems.at[k], recv_sem=recv_sems.at[k],
        device_id=peer, device_id_type=MESH)


def _exchange_start(name, kind, srcs):
    n = len(srcs)
    lands = [lax.empty((N_DEV, *s.shape) if kind == "gather" else s.shape, s.dtype) for s in srcs]

    def body(*refs):
        src_refs, land_refs = refs[:n], refs[n:2 * n]
        send_sems, recv_sems, token = refs[2 * n:3 * n], refs[3 * n:4 * n], refs[-1]
        me, peers = _peers()
        for a in range(n):
            for k, peer in enumerate(peers):
                _exchange_copy(kind, src_refs[a], land_refs[a], send_sems[a], recv_sems[a], k, peer, me).start()
        token[...] = jnp.zeros_like(token)

    sems = [pltpu.SemaphoreType.DMA((N_DEV - 1,))] * (2 * n)
    outs = pl.pallas_call(
        body, name=name,
        out_shape=sems + [pltpu.HBM(s.shape, s.dtype) for s in srcs] + [pltpu.HBM(l.shape, l.dtype) for l in lands]
                  + [jax.ShapeDtypeStruct((8, LANES), F32)],
        in_specs=[HBM] * (2 * n),
        out_specs=[SEM] * (2 * n) + [HBM] * (2 * n) + [pl.BlockSpec(memory_space=pltpu.VMEM)],
        input_output_aliases={i: 2 * n + i for i in range(2 * n)},
        compiler_params=SIDE_EFFECT,
    )(*[pltpu.with_memory_space_constraint(s, pltpu.HBM) for s in srcs],
      *[pltpu.with_memory_space_constraint(l, pltpu.HBM) for l in lands])
    return outs[:n], outs[n:2 * n], outs[2 * n:3 * n], outs[3 * n:4 * n], outs[4 * n]


def _exchange_wait(name, kind, send_sems, recv_sems, srcs, lands, after):
    n = len(srcs)

    def body(*refs):
        src_refs, land_refs = refs[:n], refs[n:2 * n]
        send_refs, recv_refs = refs[2 * n:3 * n], refs[3 * n:4 * n]
        _, peers = _peers()
        for a in range(n):
            for k, peer in enumerate(peers):
                cp = _exchange_copy(kind, src_refs[a], land_refs[a], send_refs[a], recv_refs[a], k, peer, _dev_index(*peer))
                cp.wait_send()
                cp.wait_recv()

    outs = pl.pallas_call(
        body, name=name,
        out_shape=[pltpu.HBM(s.shape, s.dtype) for s in srcs] + [pltpu.HBM(l.shape, l.dtype) for l in lands],
        in_specs=[HBM] * (2 * n) + [SEM] * (2 * n) + [pl.BlockSpec(memory_space=pl.ANY)],
        out_specs=[HBM] * (2 * n),
        input_output_aliases={i: i for i in range(2 * n)},
        compiler_params=SIDE_EFFECT,
    )(*srcs, *lands, *send_sems, *recv_sems, after)
    return outs[n:]


def _with_own(kind, land, src, me):
    own = src[None] if kind == "gather" else lax.dynamic_index_in_dim(src, me, 0, keepdims=True)
    return lax.dynamic_update_slice_in_dim(land, own, me, 0)


def _shards_to_cols(g):
    return jnp.transpose(g, (1, 0, 2)).reshape(g.shape[1], N_DEV * g.shape[2])


def _cols_to_shards(a):
    R, Ctot = a.shape
    return jnp.transpose(a.reshape(R, N_DEV, Ctot // N_DEV), (1, 0, 2))


def kernel(x, c, ctx, c_ctx, norm_attn_w, norm_mlp_w, w_ada, b_ada, w_in, attn_sink, pool_w, pool_scale, w_out, w_mlp_up, w_mlp_down, final_norm_w, loss_target, m_c_ctx, m_norm_attn_w, m_norm_mlp_w, m_w_ada, m_b_ada, m_w_in, m_attn_sink, m_pool_w, m_pool_scale, m_w_out, m_w_mlp_up, m_w_mlp_down, m_final_norm_w, v_c_ctx, v_norm_attn_w, v_norm_mlp_w, v_w_ada, v_b_ada, v_w_in, v_attn_sink, v_pool_w, v_pool_scale, v_w_out, v_w_mlp_up, v_w_mlp_down, v_final_norm_w):
    _, L, D = x.shape
    H = attn_sink.shape[1]
    A = H * HEAD_DIM
    KV = A // GQA
    P = pool_scale.shape[1]
    MODW = N_MOD * D
    ws = MODW // N_DEV
    gd = P // POOL_GROUPS
    me = _dev_index(lax.axis_index("x"), lax.axis_index("y"), lax.axis_index("c"))

    x2d, ctx2d, tgt = x[0], ctx[0], loss_target[0]
    cctx_row = c_ctx.reshape(1, D)
    wf_row = final_norm_w.reshape(1, D)
    w_ada_l = w_ada[0]
    pool_w_l = pool_w[0].reshape(POOL_GROUPS * (gd // N_DEV), gd)

    (c_all,) = _all_gather("gather_cond", [c])
    cond = jnp.concatenate([c_all[:, 0, :], cctx_row, jnp.zeros((COND_ROWS - N_DEV - 1, D), F32)], axis=0)
    b_sh = lax.dynamic_slice_in_dim(b_ada, me * ws, ws, axis=1)
    (mods_sh,) = _mm("ada_mod", cond, w_ada_l, "nn", [F32], SMALL_TILES, a_pre=_silu, extras=[("n", b_sh)], epilogue=lambda acc, b: (acc + b,))
    (mods_g,) = _all_gather("gather_mods", [mods_sh])

    w_srcs = [w_in[0].astype(BF16), w_out[0].astype(BF16), pool_w_l.astype(BF16), w_mlp_up[0].astype(BF16), w_mlp_down[0].astype(BF16)]
    w_srcs, mods_g = lax.optimization_barrier((w_srcs, mods_g))
    gw_send, gw_recv, gw_src, gw_land, _ = _exchange_start("gather_weights_start", "gather", w_srcs)

    def weights(name, lo, hi, after):
        lands = _exchange_wait(name, "gather", gw_send[lo:hi], gw_recv[lo:hi], gw_src[lo:hi], gw_land[lo:hi], after)
        return [_with_own("gather", l, s, me) for l, s in zip(lands, w_srcs[lo:hi])]

    mods = _shards_to_cols(mods_g)
    mod_b = lax.dynamic_slice_in_dim(mods, me, 1, axis=0)
    sh_a, sc_a, g_a, sh_m, sc_m, g_m = [mod_b[:, i * D:(i + 1) * D] for i in range(N_MOD)]
    csh_a, csc_a = mods[N_DEV:N_DEV + 1, :D], mods[N_DEV:N_DEV + 1, D:2 * D]

    cos, sin = _rope_tables(L)
    h, r1 = _norm_fwd("norm_attn", x2d, norm_attn_w, sc_a, sh_a)
    hc, rc = _norm_fwd("norm_attn_ctx", ctx2d, norm_attn_w, csc_a, csh_a)
    (win_g,) = weights("gather_w_in_wait", 0, 1, h)
    W_in = _shards_to_cols(win_g)
    W_qk, W_kv = W_in[:, :A + KV], W_in[:, A:A + 2 * KV]
    W_uv = jnp.concatenate([W_in[:, A + 2 * KV:], W_in[:, A + KV:A + 2 * KV]], axis=1)
    (qk,) = _mm("in_proj_qk", h, W_qk, "nn", [BF16], (1024, A + KV, D), extras=[("m", cos), ("m", sin)],
                epilogue=lambda acc, cs, sn: (_rope(acc, cs, sn),))
    (uv,) = _mm("in_proj_uv", h, W_uv, "nn", [F32], (1024, P + KV, D))
    (kvc,) = _mm("in_proj_ctx", hc, W_kv, "nn", [BF16], SMALL_TILES)
    attn, lse = _attn_fwd(qk, uv, kvc, attn_sink, A, KV, P)
    wout_g, pw_g = weights("gather_w_out_wait", 1, 3, attn)
    W_out = wout_g.reshape(A + P, D)
    PW = jnp.transpose(pw_g.reshape(N_DEV, POOL_GROUPS, gd // N_DEV, gd), (1, 0, 2, 3)).reshape(POOL_GROUPS, gd, gd)
    pool_out, pooled = _pool_fwd(uv, PW, pool_scale)
    ap = jnp.concatenate([attn, pool_out], axis=1)
    o, x1 = _mm("out_proj", ap, W_out, "nn", [F32, F32], (512, 1024, 2048), extras=[("mn", x2d), ("n", g_a)], epilogue=lambda acc, xr, g: (acc, xr + g * acc))
    wup_g, wdown_g = weights("gather_w_mlp_wait", 3, 5, x1)
    W_up = _shards_to_cols(wup_g)
    W_down = wdown_g.reshape(-1, D)
    hm, r2 = _norm_fwd("norm_mlp", x1, norm_mlp_w, sc_m, sh_m)
    up, act = _mm("mlp_up", hm, W_up, "nn", [F32, BF16], (1024, 1024, 2048), epilogue=lambda acc: (acc, _relu2(acc)))
    mlp, x2 = _mm("mlp_down", act, W_down, "nn", [F32, F32], (1024, 1024, 1024), extras=[("mn", x1), ("n", g_m)], epilogue=lambda acc, xr, g: (acc, xr + g * acc))
    d_x2, d_mlp, d_wf, d_gm, loss_p = _final(x2, tgt, mlp, wf_row, g_m)

    (d_up,) = _mm("mlp_down_bwd_act", d_mlp, W_down, "nt", [BF16], (1024, 1024, 2048), extras=[("mn", up)], epilogue=lambda acc, uu: (acc * (2.0 * jnp.maximum(uu, 0.0)),))
    (gW_down,) = _mm("mlp_down_bwd_w", act, d_mlp, "tn", [BF16], (1024, 2048, 1024))
    (gW_up,) = _mm("mlp_up_bwd_w", hm, d_up, "tn", [BF16], (1024, 2048, 1024))
    (d_hm,) = _mm("mlp_up_bwd_act", d_up, W_up, "nt", [F32], (1024, 1024, 2048))
    g_mlp_srcs = [_cols_to_shards(gW_up), gW_down.reshape(N_DEV, -1, D)]
    g_mlp = _exchange_start("grads_mlp_start", "all_to_all", g_mlp_srcs)
    zrow = jnp.zeros((1, D), F32) + g_mlp[4][0, 0]
    d_x1, s_sh_m, s_sc_m, s_w_nm, d_ga, d_o = _norm_bwd("norm_mlp_bwd", x1, r2, d_hm, d_x2, norm_mlp_w, sc_m, zrow, gate=(o, g_a))

    (d_ap,) = _mm("out_proj_bwd_act", d_o, W_out, "nt", [F32], (1024, 1024, 2048))
    (gW_out,) = _mm("out_proj_bwd_w", ap, d_o, "tn", [BF16], (1024, 2048, 1024))
    d_pooled, gPW, d_pscale = _pool_bwd_mix(d_ap, pooled, PW, pool_scale, A // P)
    gpw_s = jnp.transpose(gPW.astype(BF16).reshape(POOL_GROUPS, N_DEV, gd // N_DEV, gd), (1, 0, 2, 3)).reshape(N_DEV, -1, gd)
    g_mix_srcs = [gW_out.reshape(N_DEV, (A + P) // N_DEV, D), gpw_s]
    g_mix = _exchange_start("grads_mix_start", "all_to_all", g_mix_srcs)
    lse = lse + g_mix[4][0, 0]
    d_u = _pool_bwd_window(d_pooled)
    d_q, rd, dsink_rows, d_kvc = _attn_bwd_dq(qk, uv, kvc, attn_sink, d_ap, lse, cos, sin, A, KV, P)
    d_k, d_v = _attn_bwd_dkv(qk, uv, d_ap, lse.T, rd.T, cos, sin, A, KV, P)
    d_sink = _sum_rows("sink_grad", dsink_rows)
    d_p = jnp.concatenate([d_q, d_k, d_v, d_u], axis=1)
    d_kvc_b = d_kvc.astype(BF16)
    (gW_kv_ctx,) = _mm("in_proj_ctx_bwd_w", hc, d_kvc_b, "tn", [F32], SMALL_TILES)
    (d_hc,) = _mm("in_proj_ctx_bwd_act", d_kvc_b, W_kv, "nt", [F32], SMALL_TILES)
    gW_in_init = jnp.pad(gW_kv_ctx, ((0, 0), (A, P)))
    (gW_in,) = _mm("in_proj_bwd_w", h, d_p, "tn", [BF16], (1024, 1280, 1024), extras=[("mn", gW_in_init)], epilogue=lambda acc, init: (acc + init,))
    g_in_srcs = [_cols_to_shards(gW_in)]
    g_in = _exchange_start("grads_in_start", "all_to_all", g_in_srcs)
    (d_h,) = _mm("in_proj_bwd_act", d_p, W_in, "nt", [F32], (1024, 1024, 2560))
    grad_x, s_sh_a, s_sc_a, s_w_na = _norm_bwd("norm_attn_bwd", x2d, r1, d_h, d_x1, norm_attn_w, sc_a, zrow + g_in[4][0, 0])
    _, s_csh, s_csc, s_w_na = _norm_bwd("norm_attn_ctx_bwd", ctx2d, rc, d_hc, jnp.zeros_like(ctx2d), norm_attn_w, csc_a, s_w_na)

    pad_l = lambda a: jnp.pad(a, ((0, 0), (0, LANES - a.shape[1])))
    d_mod_b = jnp.concatenate([s_sh_a, s_sc_a, d_ga, s_sh_m, s_sc_m, d_gm], axis=1)
    summed = jnp.concatenate([s_csh, s_csc, s_w_na, s_w_nm, d_wf, d_pscale, pad_l(d_sink), pad_l(loss_p)], axis=1)
    (small_g,) = _all_gather("gather_small", [jnp.concatenate([d_mod_b, summed], axis=1)])
    small_g = small_g[:, 0, :]
    tot = _sum_rows("small_sum", small_g[:, MODW:])
    off = [0]
    for wdt in (D, D, D, D, D, P, LANES, LANES):
        off.append(off[-1] + wdt)
    seg = lambda i: tot[:, off[i]:off[i + 1]]
    g_norm_attn, g_norm_mlp, g_final, g_pscale = seg(2), seg(3), seg(4), seg(5)
    g_sink, loss = seg(6)[:, :H], seg(7)[0, 0]
    d_mod_ctx = jnp.concatenate([seg(0), seg(1), jnp.zeros((1, MODW - 2 * D), F32)], axis=1)
    d_mod = jnp.concatenate([small_g[:, :MODW], d_mod_ctx, jnp.zeros((COND_ROWS - N_DEV - 1, MODW), F32)], axis=0)
    g_b_ada = _sum_rows("b_ada_grad", d_mod[:N_DEV + 1])
    d_mod_sh = lax.dynamic_slice_in_dim(d_mod, me * ws, ws, axis=1)
    (g_w_ada,) = _mm("ada_bwd_w", cond, d_mod_sh, "tn", [F32], SMALL_TILES, a_pre=_silu)
    (d_cond_p,) = _mm("ada_bwd_cond", d_mod_sh, w_ada_l, "nt", [F32], SMALL_TILES)
    (d_cctx_g,) = _all_gather("gather_cctx", [d_cond_p[N_DEV:N_DEV + 1]])
    g_c_ctx = _silu_grad_mul(cctx_row, _sum_rows("cctx_sum", d_cctx_g[:, 0, :]))

    def arrived(name, started, srcs):
        lands = _exchange_wait(name, "all_to_all", started[0], started[1], started[2], started[3], g_c_ctx)
        return [_with_own("all_to_all", l, s, me) for l, s in zip(lands, srcs)]

    r_up, r_down = arrived("grads_mlp_wait", g_mlp, g_mlp_srcs)
    r_out, r_pw = arrived("grads_mix_wait", g_mix, g_mix_srcs)
    (r_in,) = arrived("grads_in_wait", g_in, g_in_srcs)

    results = {
        "c_ctx": _adamw("adam_c_ctx", cctx_row, g_c_ctx, m_c_ctx.reshape(1, D), v_c_ctx.reshape(1, D)),
        "norm_attn_w": _adamw("adam_norm_attn", norm_attn_w, g_norm_attn, m_norm_attn_w, v_norm_attn_w),
        "norm_mlp_w": _adamw("adam_norm_mlp", norm_mlp_w, g_norm_mlp, m_norm_mlp_w, v_norm_mlp_w),
        "w_ada": _adamw("adam_w_ada", w_ada_l, g_w_ada, m_w_ada[0], v_w_ada[0]),
        "b_ada": _adamw("adam_b_ada", b_ada, g_b_ada, m_b_ada, v_b_ada),
        "w_in": _adamw("adam_w_in", w_in[0], r_in, m_w_in[0], v_w_in[0]),
        "attn_sink": _adamw("adam_sink", attn_sink, g_sink, m_attn_sink, v_attn_sink),
        "pool_w": _adamw("adam_pool_w", pool_w_l, r_pw, m_pool_w[0].reshape(pool_w_l.shape), v_pool_w[0].reshape(pool_w_l.shape)),
        "pool_scale": _adamw("adam_pool_scale", pool_scale, g_pscale, m_pool_scale, v_pool_scale),
        "w_out": _adamw("adam_w_out", w_out[0], r_out, m_w_out[0], v_w_out[0]),
        "w_mlp_up": _adamw("adam_w_up", w_mlp_up[0], r_up, m_w_mlp_up[0], v_w_mlp_up[0]),
        "w_mlp_down": _adamw("adam_w_down", w_mlp_down[0], r_down, m_w_mlp_down[0], v_w_mlp_down[0]),
        "final_norm_w": _adamw("adam_final_norm", wf_row, g_final, m_final_norm_w.reshape(1, D), v_final_norm_w.reshape(1, D)),
    }
    shapes = {"c_ctx": c_ctx.shape, "norm_attn_w": norm_attn_w.shape, "norm_mlp_w": norm_mlp_w.shape, "w_ada": w_ada.shape,
              "b_ada": b_ada.shape, "w_in": w_in.shape, "attn_sink": attn_sink.shape, "pool_w": pool_w.shape,
              "pool_scale": pool_scale.shape, "w_out": w_out.shape, "w_mlp_up": w_mlp_up.shape, "w_mlp_down": w_mlp_down.shape,
              "final_norm_w": final_norm_w.shape}
    outs = [loss, grad_x.reshape(x.shape)]
    for part in range(4):
        outs += [results[name][part].reshape(shape) for name, shape in shapes.items()]
    return tuple(outs)
```

```python
import functools

import jax
import jax.numpy as jnp
from jax import lax
from jax.experimental import pallas as pl
from jax.experimental.pallas import tpu as pltpu

F32 = jnp.float32
BF16 = jnp.bfloat16
I32 = jnp.int32

HEAD_DIM = 64
GQA = 4
BLOCK = 128
GRID_W = 64
ROPE_BASE = 10000.0
POOL_WINDOWS = (2, 4, 8, 16)
POOL_GROUPS = len(POOL_WINDOWS)
HALO = 8
N_MOD = 6
EPS = 1e-6
NEG_INF = -1e30
ADAM_LR = 0.001
ADAM_B1 = 0.9
ADAM_B2 = 0.999
ADAM_EPS = 1e-08
ADAM_WD = 0.01
ADAM_STEP = 10
N_DEV = 8
COND_ROWS = 2 * N_DEV
LANES = 128
SUBLANES_16BIT = 16
VMEM_LIMIT = 48 * 1024 * 1024
FUSED_VMEM_LIMIT = 56 * 1024 * 1024
SMALL_TILES = (512, 1024, 512)
MESH = pl.DeviceIdType.MESH
HBM = pl.BlockSpec(memory_space=pltpu.HBM)
SEM = pl.BlockSpec(memory_space=pltpu.SEMAPHORE)
SIDE_EFFECT = pltpu.CompilerParams(has_side_effects=pltpu.SideEffectType.DATAFLOW_SIDE_EFFECTING)


def _cparams(*sem):
    return pltpu.CompilerParams(dimension_semantics=sem, vmem_limit_bytes=VMEM_LIMIT)


def _tile(n, pref, align):
    if n <= pref:
        return n
    t = (pref // align) * align
    while t >= align:
        if n % t == 0:
            return t
        t -= align
    return n


def _dot(a, b):
    return lax.dot_general(a, b, (((1,), (0,)), ((), ())), preferred_element_type=F32)


def _dot_nt(a, b):
    return lax.dot_general(a, b, (((1,), (1,)), ((), ())), preferred_element_type=F32)


def _dot_tn(a, b):
    return lax.dot_general(a, b, (((0,), (0,)), ((), ())), preferred_element_type=F32)


_DOTS = {"nn": _dot, "nt": _dot_nt, "tn": _dot_tn}


def _mm(name, a, b, mode, out_dtypes, tiles, *, epilogue=None, extras=(), a_pre=None, n_sums=0, chunk=None,
        b_shards=False, out_shards=False, vmem=VMEM_LIMIT):
    if mode == "nn":
        M, K = a.shape
        K2, N = (b.shape[1], N_DEV * b.shape[2]) if b_shards else b.shape
    elif mode == "nt":
        M, K = a.shape
        N, K2 = (b.shape[1], N_DEV * b.shape[2]) if b_shards else b.shape
    else:
        (K, M), (K2, N) = a.shape, b.shape
    assert K == K2 and not (b_shards and mode == "tn"), (name, a.shape, b.shape)
    n_span = N // N_DEV if out_shards or (b_shards and mode == "nn") else N
    k_span = K // N_DEV if b_shards and mode == "nt" else K
    tm = _tile(M, tiles[0], LANES if mode == "tn" else SUBLANES_16BIT)
    tn = _tile(n_span, tiles[1], LANES)
    tk = _tile(k_span, tiles[2], SUBLANES_16BIT if mode == "tn" else LANES)
    nk, nb, kb = K // tk, n_span // tn, k_span // tk
    rows = tm if chunk is None else min(chunk, tm)
    n_ex, n_out = len(extras), len(out_dtypes)
    use_acc = nk > 1 or rows < tm
    assert n_sums == 0 or N == tn, name

    def product(a_ref, b_ref):
        at = a_ref[...]
        if a_pre is not None:
            at = a_pre(at)
        return _DOTS[mode](at.astype(BF16), b_ref[...].astype(BF16))

    def apply(acc, ex, out_refs, sl):
        res = (acc,) if epilogue is None else epilogue(acc, *ex)
        for o_ref, o in zip(out_refs, res[:n_out]):
            o_ref[sl, :] = o.astype(o_ref.dtype)
        return tuple(res[n_out:])

    def finish(acc, ex_refs, out_refs, sum_refs):
        if rows == tm:
            acc = acc if not use_acc else acc[...]
            sums = apply(acc, [r[...] for r in ex_refs], out_refs, slice(None))
        else:
            def one(ci, sums):
                sl = pl.ds(pl.multiple_of(ci * rows, rows), rows)
                ex = [r[...] if kind == "n" else r[sl, :] for (kind, _), r in zip(extras, ex_refs)]
                return tuple(s + v for s, v in zip(sums, apply(acc[sl, :], ex, out_refs, sl)))
            sums = lax.fori_loop(0, tm // rows, one, tuple(jnp.zeros((1, tn), F32) for _ in range(n_sums)))
        first = pl.program_id(0) == 0
        for s_ref, sv in zip(sum_refs, sums):
            @pl.when(first)
            def _(s_ref=s_ref, sv=sv):
                s_ref[...] = sv

            @pl.when(jnp.logical_not(first))
            def _(s_ref=s_ref, sv=sv):
                s_ref[...] += sv

    def body(a_ref, b_ref, *rest):
        ex_refs, out_refs = rest[:n_ex], rest[n_ex:n_ex + n_out]
        sum_refs = rest[n_ex + n_out:n_ex + n_out + n_sums]
        if not use_acc:
            finish(product(a_ref, b_ref), ex_refs, out_refs, sum_refs)
            return
        acc_ref = rest[-1]
        k = pl.program_id(2)

        @pl.when(k == 0)
        def _():
            acc_ref[...] = product(a_ref, b_ref)

        @pl.when(k > 0)
        def _():
            acc_ref[...] += product(a_ref, b_ref)

        @pl.when(k == nk - 1)
        def _():
            finish(acc_ref, ex_refs, out_refs, sum_refs)

    a_spec = pl.BlockSpec((tk, tm), lambda i, j, k: (k, i)) if mode == "tn" else pl.BlockSpec((tm, tk), lambda i, j, k: (i, k))
    if not b_shards:
        b_spec = pl.BlockSpec((tn, tk), lambda i, j, k: (j, k)) if mode == "nt" else pl.BlockSpec((tk, tn), lambda i, j, k: (k, j))
    elif mode == "nn":
        b_spec = pl.BlockSpec((None, tk, tn), lambda i, j, k: (j // nb, k, j % nb))
    else:
        b_spec = pl.BlockSpec((None, tn, tk), lambda i, j, k: (k // kb, j, k % kb))
    ex_specs = []
    for kind, arr in extras:
        if kind == "mn":
            ex_specs.append(pl.BlockSpec((tm, tn), lambda i, j, k: (i, j)))
        elif kind == "n":
            ex_specs.append(pl.BlockSpec((1, tn), lambda i, j, k: (0, j)))
        else:
            ex_specs.append(pl.BlockSpec((tm, arr.shape[1]), lambda i, j, k: (i, 0)))
    if out_shards:
        out_specs = [pl.BlockSpec((None, tm, tn), lambda i, j, k: (j // nb, i, j % nb)) for _ in out_dtypes]
        out_shape = [jax.ShapeDtypeStruct((N_DEV, M, n_span), d) for d in out_dtypes]
    else:
        out_specs = [pl.BlockSpec((tm, tn), lambda i, j, k: (i, j)) for _ in out_dtypes]
        out_shape = [jax.ShapeDtypeStruct((M, N), d) for d in out_dtypes]
    out_specs += [pl.BlockSpec((1, tn), lambda i, j, k: (0, 0))] * n_sums
    out_shape += [jax.ShapeDtypeStruct((1, N), F32)] * n_sums
    return pl.pallas_call(
        body,
        name=name,
        grid=(M // tm, N // tn, nk),
        in_specs=[a_spec, b_spec] + ex_specs,
        out_specs=out_specs,
        out_shape=out_shape,
        scratch_shapes=[pltpu.VMEM((tm, tn), F32)] if use_acc else [],
        compiler_params=pltpu.CompilerParams(
            dimension_semantics=("arbitrary",) * 3 if n_sums else ("parallel", "parallel", "arbitrary"), vmem_limit_bytes=vmem),
    )(a, b, *[arr for _, arr in extras])


def _silu(v):
    return v / (1.0 + jnp.exp(-v))


def _relu2(v):
    r = jnp.maximum(v, 0.0)
    return r * r


def _rope_tables(L):
    half = HEAD_DIM // 2
    inv_freq = ROPE_BASE ** (-jnp.arange(0, half, 2, dtype=F32) / half)
    t = jnp.arange(L)
    row, col = t // GRID_W, t % GRID_W
    ang_r = row.astype(F32)[:, None] * inv_freq[None, :]
    ang_c = col.astype(F32)[:, None] * inv_freq[None, :]
    cos = jnp.concatenate([jnp.cos(ang_r), jnp.cos(ang_r), jnp.cos(ang_c), jnp.cos(ang_c)], axis=1)
    sin = jnp.concatenate([-jnp.sin(ang_r), jnp.sin(ang_r), -jnp.sin(ang_c), jnp.sin(ang_c)], axis=1)
    reps = LANES // HEAD_DIM
    return jnp.tile(cos, (1, reps)), jnp.tile(sin, (1, reps))


def _rope(xf, cos, sin):
    quarter = HEAD_DIM // 4
    lane = lax.broadcasted_iota(I32, (xf.shape[0], LANES), 1)
    first = (lane & quarter) == 0
    outs = []
    for j in range(xf.shape[1] // LANES):
        xc = xf[:, j * LANES:(j + 1) * LANES]
        partner = jnp.where(first, pltpu.roll(xc, LANES - quarter, 1), pltpu.roll(xc, quarter, 1))
        outs.append(xc * cos + partner * sin)
    return outs[0] if len(outs) == 1 else jnp.concatenate(outs, axis=1)


def _inv_rms(xf):
    return lax.rsqrt(jnp.mean(xf * xf, axis=-1, keepdims=True) + EPS)


def _modulated_norm(xf, w, sc, sh):
    return ((xf * _inv_rms(xf)) * w) * (1.0 + sc) + sh


def _modulated_norm_bwd(xf, dh, dres, w, sc):
    r = _inv_rms(xf)
    xh = xf * r
    dn = dh * (1.0 + sc)
    dxh = dn * w
    dx = dres + r * (dxh - xh * jnp.mean(dxh * xh, axis=-1, keepdims=True))
    col = lambda v: jnp.sum(v, axis=0, keepdims=True)
    return dx, col(dh), col(dh * (xh * w)), col(dn * xh)


def _norm_fwd(name, x, w, sc, sh):
    L, D = x.shape
    T = _tile(L, 256, 8)

    def body(x_ref, w_ref, sc_ref, sh_ref, h_ref):
        h_ref[...] = _modulated_norm(x_ref[...], w_ref[...], sc_ref[...], sh_ref[...]).astype(BF16)

    row = pl.BlockSpec((1, D), lambda i: (0, 0))
    return pl.pallas_call(
        body, name=name, grid=(L // T,),
        in_specs=[pl.BlockSpec((T, D), lambda i: (i, 0)), row, row, row],
        out_specs=pl.BlockSpec((T, D), lambda i: (i, 0)),
        out_shape=jax.ShapeDtypeStruct((L, D), BF16),
        compiler_params=_cparams("parallel"),
    )(x, w, sc, sh)


def _norm_bwd(name, x, dh, dres, w, sc, w_init, gate=None):
    L, D = x.shape
    T = _tile(L, 256, 8)
    with_gate = gate is not None

    def body(x_ref, dh_ref, dres_ref, w_ref, sc_ref, wi_ref, *rest):
        if with_gate:
            o_ref, g_ref, dx_ref, ssh_ref, ssc_ref, sw_ref, sg_ref, do_ref = rest
        else:
            dx_ref, ssh_ref, ssc_ref, sw_ref = rest
        i = pl.program_id(0)

        @pl.when(i == 0)
        def _():
            ssh_ref[...] = jnp.zeros_like(ssh_ref)
            ssc_ref[...] = jnp.zeros_like(ssc_ref)
            sw_ref[...] = wi_ref[...]
            if with_gate:
                sg_ref[...] = jnp.zeros_like(sg_ref)

        dx, s_sh, s_sc, s_w = _modulated_norm_bwd(x_ref[...], dh_ref[...], dres_ref[...], w_ref[...], sc_ref[...])
        ssh_ref[...] += s_sh
        ssc_ref[...] += s_sc
        sw_ref[...] += s_w
        dx_ref[...] = dx
        if with_gate:
            sg_ref[...] += jnp.sum(dx * o_ref[...], axis=0, keepdims=True)
            do_ref[...] = (g_ref[...] * dx).astype(BF16)

    tile = pl.BlockSpec((T, D), lambda i: (i, 0))
    row = pl.BlockSpec((1, D), lambda i: (0, 0))
    in_specs = [tile, tile, tile, row, row, row]
    out_specs = [tile, row, row, row]
    out_shape = [jax.ShapeDtypeStruct((L, D), F32)] + [jax.ShapeDtypeStruct((1, D), F32)] * 3
    args = [x, dh, dres, w, sc, w_init]
    if with_gate:
        in_specs += [tile, row]
        out_specs += [row, tile]
        out_shape += [jax.ShapeDtypeStruct((1, D), F32), jax.ShapeDtypeStruct((L, D), BF16)]
        args += list(gate)
    return pl.pallas_call(
        body, name=name, grid=(L // T,), in_specs=in_specs, out_specs=out_specs, out_shape=out_shape,
        compiler_params=_cparams("arbitrary"),
    )(*args)


def _out_proj_epilogue(acc, xr, g, w, sc, sh):
    x1 = xr + g * acc
    return acc, x1, _modulated_norm(x1, w, sc, sh)


def _mlp_down_epilogue(acc, x1, tgt, g, wf):
    D = acc.shape[1]
    x2 = x1 + g * acc
    r = _inv_rms(x2)
    xh = x2 * r
    err = xh * wf - tgt
    loss = 0.5 * jnp.sum(jnp.mean(err * err, axis=-1, keepdims=True), axis=0, keepdims=True)
    dy = err / D
    dxh = dy * wf
    dx = r * (dxh - xh * jnp.mean(dxh * xh, axis=-1, keepdims=True))
    col = lambda v: jnp.sum(v, axis=0, keepdims=True)
    return dx, g * dx, col(dy * xh), col(dx * acc), jnp.broadcast_to(loss, (1, D))


def _heads(ref, first, n):
    return jnp.concatenate([ref[:, (first + g) * HEAD_DIM:(first + g + 1) * HEAD_DIM] for g in range(n)], axis=0)


def _put_col(tile, h, col):
    lane = lax.broadcasted_iota(I32, tile.shape, 1)
    return jnp.where(lane == h, col, tile)


def _get_col(tile, h):
    lane = lax.broadcasted_iota(I32, tile.shape, 1)
    return jnp.sum(jnp.where(lane == h, tile, 0.0), axis=1, keepdims=True)


def _attn_mask(n, L, C):
    shape = (GQA * BLOCK, 3 * BLOCK + C)
    qi = lax.broadcasted_iota(I32, shape, 0) & (BLOCK - 1)
    kj = lax.broadcasted_iota(I32, shape, 1)
    kpos = n * BLOCK - BLOCK + kj
    window = (kj >= qi) & (kj <= qi + 2 * BLOCK) & (kpos >= 0) & (kpos < L)
    return window | (kj >= 3 * BLOCK)


def _attn_specs(L, A, KV, C, vcol):
    nb = L // BLOCK
    kcol = A // KV
    prev = lambda n: jnp.maximum(n - 1, 0)
    nxt = lambda n: jnp.minimum(n + 1, nb - 1)
    q_spec = pl.BlockSpec((BLOCK, A), lambda n: (n, 0))
    k_specs = [pl.BlockSpec((BLOCK, KV), lambda n: (prev(n), kcol)), pl.BlockSpec((BLOCK, KV), lambda n: (n, kcol)),
               pl.BlockSpec((BLOCK, KV), lambda n: (nxt(n), kcol))]
    v_specs = [pl.BlockSpec((BLOCK, KV), lambda n: (prev(n), vcol)), pl.BlockSpec((BLOCK, KV), lambda n: (n, vcol)),
               pl.BlockSpec((BLOCK, KV), lambda n: (nxt(n), vcol))]
    kvc_spec = pl.BlockSpec((C, 2 * KV), lambda n: (0, 0))
    return q_spec, k_specs, v_specs, kvc_spec


def _keys_values(hk, k_refs, v_refs, kvc_ref, KV):
    sl = slice(hk * HEAD_DIM, (hk + 1) * HEAD_DIM)
    keys = jnp.concatenate([r[:, sl] for r in k_refs] + [kvc_ref[:, sl]], axis=0)
    vals = jnp.concatenate([r[:, sl].astype(BF16) for r in v_refs] + [kvc_ref[:, KV + hk * HEAD_DIM:KV + (hk + 1) * HEAD_DIM]], axis=0)
    return keys, vals


def _sink_col(sink_ref, hk):
    return jnp.concatenate([jnp.full((BLOCK, 1), sink_ref[0, hk * GQA + g], F32) for g in range(GQA)], axis=0)


def _attn_fwd(qk, uv, kvc, sink, A, KV, P):
    L = qk.shape[0]
    C = kvc.shape[0]
    nkv = KV // HEAD_DIM
    H = nkv * GQA
    scale = HEAD_DIM ** -0.5

    def body(sink_ref, q_ref, kp_ref, kc_ref, kn_ref, vp_ref, vc_ref, vn_ref, kvc_ref, o_ref, lse_ref):
        valid = _attn_mask(pl.program_id(0), L, C)
        lse_t = jnp.zeros((BLOCK, H), F32)
        for hk in range(nkv):
            keys, vals = _keys_values(hk, (kp_ref, kc_ref, kn_ref), (vp_ref, vc_ref, vn_ref), kvc_ref, KV)
            qs = _heads(q_ref, hk * GQA, GQA) * scale
            s = jnp.where(valid, _dot_nt(qs, keys), NEG_INF)
            sk = _sink_col(sink_ref, hk)
            m = jnp.maximum(jnp.max(s, axis=-1, keepdims=True), sk)
            p = jnp.exp(s - m)
            den = jnp.sum(p, axis=-1, keepdims=True) + jnp.exp(sk - m)
            o = _dot(p.astype(BF16), vals) * (1.0 / den)
            lse = m + jnp.log(den)
            o_ref[:, hk * GQA * HEAD_DIM:(hk + 1) * GQA * HEAD_DIM] = jnp.concatenate(
                [o[g * BLOCK:(g + 1) * BLOCK] for g in range(GQA)], axis=1).astype(BF16)
            for g in range(GQA):
                lse_t = _put_col(lse_t, hk * GQA + g, lse[g * BLOCK:(g + 1) * BLOCK])
        lse_ref[...] = lse_t

    q_spec, k_specs, v_specs, kvc_spec = _attn_specs(L, A, KV, C, P // KV)
    return pl.pallas_call(
        body, name="attn_fwd", grid=(L // BLOCK,),
        in_specs=[pl.BlockSpec(memory_space=pltpu.SMEM), q_spec] + k_specs + v_specs + [kvc_spec],
        out_specs=[pl.BlockSpec((BLOCK, A), lambda n: (n, 0)), pl.BlockSpec((BLOCK, H), lambda n: (n, 0))],
        out_shape=[jax.ShapeDtypeStruct((L, A), BF16), jax.ShapeDtypeStruct((L, H), F32)],
        compiler_params=_cparams("parallel"),
    )(sink, qk, qk, qk, qk, uv, uv, uv, kvc)


def _attn_bwd_dq(qk, uv, kvc, sink, dap, lse, cos, sin, A, KV, P):
    L = qk.shape[0]
    C = kvc.shape[0]
    nkv = KV // HEAD_DIM
    H = nkv * GQA
    scale = HEAD_DIM ** -0.5
    W = 3 * BLOCK

    def body(sink_ref, q_ref, kp_ref, kc_ref, kn_ref, vp_ref, vc_ref, vn_ref, kvc_ref, do_ref, lse_ref, cos_ref, sin_ref,
             dq_ref, rd_ref, ds_ref, dkvc_ref):
        n = pl.program_id(0)

        @pl.when(n == 0)
        def _():
            dkvc_ref[...] = jnp.zeros_like(dkvc_ref)

        valid = _attn_mask(n, L, C)
        lse_t = lse_ref[...]
        rd_t = jnp.zeros((BLOCK, H), F32)
        ds_t = jnp.zeros((BLOCK, H), F32)
        dq_parts = []
        for hk in range(nkv):
            sl = slice(hk * HEAD_DIM, (hk + 1) * HEAD_DIM)
            keys, vals = _keys_values(hk, (kp_ref, kc_ref, kn_ref), (vp_ref, vc_ref, vn_ref), kvc_ref, KV)
            qs = _heads(q_ref, hk * GQA, GQA) * scale
            dos = _heads(do_ref, hk * GQA, GQA).astype(BF16)
            lse = jnp.concatenate([_get_col(lse_t, hk * GQA + g) for g in range(GQA)], axis=0)
            p = jnp.exp(jnp.where(valid, _dot_nt(qs, keys), NEG_INF) - lse)
            dp = _dot_nt(dos, vals)
            rd = jnp.sum(p * dp, axis=-1, keepdims=True)
            ds = (p * (dp - rd)).astype(BF16)
            dq = _dot(ds, keys) * scale
            dkvc_ref[:, sl] += _dot_tn(ds[:, W:], qs)
            dkvc_ref[:, KV + hk * HEAD_DIM:KV + (hk + 1) * HEAD_DIM] += _dot_tn(p[:, W:].astype(BF16), dos)
            dsink = -(jnp.exp(_sink_col(sink_ref, hk) - lse) * rd)
            for g in range(GQA):
                rows = slice(g * BLOCK, (g + 1) * BLOCK)
                rd_t = _put_col(rd_t, hk * GQA + g, rd[rows])
                ds_t = _put_col(ds_t, hk * GQA + g, dsink[rows])
                dq_parts.append(dq[rows])
        rd_ref[...] = rd_t
        ds_ref[...] = ds_t
        dq_ref[...] = _rope(jnp.concatenate(dq_parts, axis=1), cos_ref[...], -sin_ref[...]).astype(BF16)

    q_spec, k_specs, v_specs, kvc_spec = _attn_specs(L, A, KV, C, P // KV)
    blk = lambda w: pl.BlockSpec((BLOCK, w), lambda n: (n, 0))
    return pl.pallas_call(
        body, name="attn_bwd_dq", grid=(L // BLOCK,),
        in_specs=[pl.BlockSpec(memory_space=pltpu.SMEM), q_spec] + k_specs + v_specs + [kvc_spec, blk(A), blk(H), blk(LANES), blk(LANES)],
        out_specs=[blk(A), blk(H), blk(H), pl.BlockSpec((C, 2 * KV), lambda n: (0, 0))],
        out_shape=[jax.ShapeDtypeStruct((L, A), BF16), jax.ShapeDtypeStruct((L, H), F32), jax.ShapeDtypeStruct((L, H), F32),
                   jax.ShapeDtypeStruct((C, 2 * KV), F32)],
        compiler_params=_cparams("arbitrary"),
    )(sink, qk, qk, qk, qk, uv, uv, uv, kvc, dap, lse, cos, sin)


def _attn_bwd_dkv(qk, uv, dap, lse_t, rd_t, cos, sin, A, KV, P):
    L = qk.shape[0]
    nb = L // BLOCK
    nkv = KV // HEAD_DIM
    H = nkv * GQA
    scale = HEAD_DIM ** -0.5
    R = 3 * GQA * BLOCK

    def body(k_ref, v_ref, qp_ref, qc_ref, qn_ref, dop_ref, doc_ref, don_ref, lsep_ref, lsec_ref, lsen_ref,
             rdp_ref, rdc_ref, rdn_ref, cos_ref, sin_ref, dk_ref, dv_ref):
        m = pl.program_id(0)
        kj = lax.broadcasted_iota(I32, (BLOCK, R), 0)
        col = lax.broadcasted_iota(I32, (BLOCK, R), 1)
        part = col // (GQA * BLOCK)
        qi = col & (BLOCK - 1)
        before = jnp.where(m >= 1, 0, -2 * BLOCK)
        after = jnp.where(m <= nb - 2, 0, 2 * BLOCK)
        valid = ((part == 0) & (kj <= qi + before)) | (part == 1) | ((part == 2) & (kj >= qi + after))
        dk_parts, dv_parts = [], []
        for hk in range(nkv):
            sl = slice(hk * HEAD_DIM, (hk + 1) * HEAD_DIM)
            km = k_ref[:, sl]
            vm = v_ref[:, sl].astype(BF16)
            qs = jnp.concatenate([_heads(q, hk * GQA, GQA) for q in (qp_ref, qc_ref, qn_ref)], axis=0) * scale
            dos = jnp.concatenate([_heads(d, hk * GQA, GQA) for d in (dop_ref, doc_ref, don_ref)], axis=0).astype(BF16)
            rows = [slice(hk * GQA + g, hk * GQA + g + 1) for g in range(GQA)]
            lse = jnp.concatenate([t[r, :] for t in (lsep_ref, lsec_ref, lsen_ref) for r in rows], axis=1)
            rdv = jnp.concatenate([t[r, :] for t in (rdp_ref, rdc_ref, rdn_ref) for r in rows], axis=1)
            p = jnp.exp(jnp.where(valid, _dot_nt(km, qs), NEG_INF) - lse)
            ds = (p * (_dot_nt(vm, dos) - rdv)).astype(BF16)
            dk_parts.append(_dot(ds, qs))
            dv_parts.append(_dot(p.astype(BF16), dos))
        dk = dk_parts[0] if nkv == 1 else jnp.concatenate(dk_parts, axis=1)
        dv = dv_parts[0] if nkv == 1 else jnp.concatenate(dv_parts, axis=1)
        dk_ref[...] = _rope(dk, cos_ref[...], -sin_ref[...]).astype(BF16)
        dv_ref[...] = dv.astype(BF16)

    prev = lambda m: jnp.maximum(m - 1, 0)
    nxt = lambda m: jnp.minimum(m + 1, nb - 1)
    three = lambda w: [pl.BlockSpec((BLOCK, w), lambda m: (prev(m), 0)), pl.BlockSpec((BLOCK, w), lambda m: (m, 0)),
                       pl.BlockSpec((BLOCK, w), lambda m: (nxt(m), 0))]
    three_t = [pl.BlockSpec((H, BLOCK), lambda m: (0, prev(m))), pl.BlockSpec((H, BLOCK), lambda m: (0, m)),
               pl.BlockSpec((H, BLOCK), lambda m: (0, nxt(m)))]
    blk = lambda w: pl.BlockSpec((BLOCK, w), lambda m: (m, 0))
    return pl.pallas_call(
        body, name="attn_bwd_dkv", grid=(nb,),
        in_specs=[pl.BlockSpec((BLOCK, KV), lambda m: (m, A // KV)), pl.BlockSpec((BLOCK, KV), lambda m: (m, P // KV))]
                 + three(A) + three(A) + three_t + three_t + [blk(LANES), blk(LANES)],
        out_specs=[blk(KV), blk(KV)],
        out_shape=[jax.ShapeDtypeStruct((L, KV), BF16), jax.ShapeDtypeStruct((L, KV), BF16)],
        compiler_params=_cparams("parallel"),
    )(qk, uv, qk, qk, qk, dap, dap, dap, lse_t, lse_t, lse_t, rd_t, rd_t, rd_t, cos, sin)


def _halo_specs(T, L, W, col):
    per = T // HALO
    return [pl.BlockSpec((HALO, W), lambda i: (jnp.maximum(i * per - 1, 0), col)),
            pl.BlockSpec((T, W), lambda i: (i, col)),
            pl.BlockSpec((HALO, W), lambda i: (jnp.minimum((i + 1) * per, L // HALO - 1), col))]


def _fill_halo_buf(buf, prev_ref, cur_ref, next_ref, i, nt, T):
    buf[0:HALO, :] = jnp.where(i > 0, prev_ref[...], 0.0)
    buf[HALO:HALO + T, :] = cur_ref[...]
    buf[HALO + T:2 * HALO + T, :] = jnp.where(i < nt - 1, next_ref[...], 0.0)


def _counts(t, w, L):
    lo = jnp.clip(t - w // 2, 0, L)
    hi = jnp.clip(t - w // 2 + w, 0, L)
    return jnp.maximum(hi - lo, 1).astype(F32)


def _pool_fwd(u, pw, scale):
    L, P = u.shape[0], scale.shape[1]
    gd = P // POOL_GROUPS
    T = _tile(L, 256, 8)
    nt = L // T

    def body(up_ref, uc_ref, un_ref, pw_ref, sc_ref, out_ref, pooled_ref, buf):
        i = pl.program_id(0)
        _fill_halo_buf(buf, up_ref, uc_ref, un_ref, i, nt, T)
        t = i * T + lax.broadcasted_iota(I32, (T, 1), 0)
        for g, w in enumerate(POOL_WINDOWS):
            cols = slice(g * gd, (g + 1) * gd)
            acc = buf[pl.ds(HALO - w // 2, T), cols]
            for o in range(-w // 2 + 1, w // 2):
                acc = acc + buf[pl.ds(HALO + o, T), cols]
            pooled = (acc / _counts(t, w, L) - buf[pl.ds(HALO, T), cols]).astype(BF16)
            pooled_ref[:, cols] = pooled
            out_ref[:, cols] = (_dot(pooled, pw_ref[g]) * sc_ref[:, cols]).astype(BF16)

    return pl.pallas_call(
        body, name="pool_fwd", grid=(nt,),
        in_specs=_halo_specs(T, L, P, 0) + [pl.BlockSpec((POOL_GROUPS, gd, gd), lambda i: (0, 0, 0)), pl.BlockSpec((1, P), lambda i: (0, 0))],
        out_specs=[pl.BlockSpec((T, P), lambda i: (i, 0)), pl.BlockSpec((T, P), lambda i: (i, 0))],
        out_shape=[jax.ShapeDtypeStruct((L, P), BF16), jax.ShapeDtypeStruct((L, P), BF16)],
        scratch_shapes=[pltpu.VMEM((T + 2 * HALO, P), F32)],
        compiler_params=_cparams("parallel"),
    )(u, u, u, pw, scale)


def _pool_bwd_mix(dap, pooled, pw, scale, pcol):
    L, P = pooled.shape
    gd = P // POOL_GROUPS
    T = _tile(L, 256, 8)

    def body(dp_ref, pooled_ref, pw_ref, sc_ref, dpooled_ref, dpw_ref, dsc_ref):
        i = pl.program_id(0)

        @pl.when(i == 0)
        def _():
            dpw_ref[...] = jnp.zeros_like(dpw_ref)
            dsc_ref[...] = jnp.zeros_like(dsc_ref)

        for g in range(POOL_GROUPS):
            cols = slice(g * gd, (g + 1) * gd)
            pb = pooled_ref[:, cols]
            dp = dp_ref[:, cols]
            dsc_ref[:, cols] += jnp.sum(dp * _dot(pb, pw_ref[g]), axis=0, keepdims=True)
            dm = (dp * sc_ref[:, cols]).astype(BF16)
            dpw_ref[g] += _dot_tn(pb, dm)
            dpooled_ref[:, cols] = _dot_nt(dm, pw_ref[g])

    return pl.pallas_call(
        body, name="pool_bwd_mix", grid=(L // T,),
        in_specs=[pl.BlockSpec((T, P), lambda i: (i, pcol)), pl.BlockSpec((T, P), lambda i: (i, 0)),
                  pl.BlockSpec((POOL_GROUPS, gd, gd), lambda i: (0, 0, 0)), pl.BlockSpec((1, P), lambda i: (0, 0))],
        out_specs=[pl.BlockSpec((T, P), lambda i: (i, 0)), pl.BlockSpec((POOL_GROUPS, gd, gd), lambda i: (0, 0, 0)),
                   pl.BlockSpec((1, P), lambda i: (0, 0))],
        out_shape=[jax.ShapeDtypeStruct((L, P), F32), jax.ShapeDtypeStruct((POOL_GROUPS, gd, gd), F32), jax.ShapeDtypeStruct((1, P), F32)],
        compiler_params=_cparams("arbitrary"),
    )(dap, pooled, pw, scale)


def _pool_bwd_window(dpooled):
    L, P = dpooled.shape
    gd = P // POOL_GROUPS
    T = _tile(L, 256, 8)
    nt = L // T

    def body(dp_ref, dc_ref, dn_ref, du_ref, buf):
        i = pl.program_id(0)
        _fill_halo_buf(buf, dp_ref, dc_ref, dn_ref, i, nt, T)
        t = i * T - HALO + lax.broadcasted_iota(I32, (T + 2 * HALO, 1), 0)
        for g, w in enumerate(POOL_WINDOWS):
            cols = slice(g * gd, (g + 1) * gd)
            buf[:, cols] = buf[:, cols] / _counts(t, w, L)
            acc = buf[pl.ds(HALO - w // 2 + 1, T), cols]
            for o in range(-w // 2 + 2, w // 2 + 1):
                acc = acc + buf[pl.ds(HALO + o, T), cols]
            du_ref[:, cols] = (acc - dc_ref[:, cols]).astype(BF16)

    return pl.pallas_call(
        body, name="pool_bwd_window", grid=(nt,),
        in_specs=_halo_specs(T, L, P, 0),
        out_specs=pl.BlockSpec((T, P), lambda i: (i, 0)),
        out_shape=jax.ShapeDtypeStruct((L, P), BF16),
        scratch_shapes=[pltpu.VMEM((T + 2 * HALO, P), F32)],
        compiler_params=_cparams("parallel"),
    )(dpooled, dpooled, dpooled)


def _sum_rows(name, a):
    R, N = a.shape

    def body(a_ref, o_ref):
        if R <= 16:
            acc = a_ref[0:1, :]
            for r in range(1, R):
                acc = acc + a_ref[r:r + 1, :]
        else:
            acc = jnp.sum(a_ref[...], axis=0, keepdims=True)
        o_ref[...] = acc

    return pl.pallas_call(body, name=name, out_shape=jax.ShapeDtypeStruct((1, N), F32))(a)


def _silu_grad_mul(cv, g):
    def body(c_ref, g_ref, o_ref):
        cvv = c_ref[...]
        s = 1.0 / (1.0 + jnp.exp(-cvv))
        o_ref[...] = g_ref[...] * (s * (1.0 + cvv * (1.0 - s)))

    return pl.pallas_call(body, name="silu_grad_mul", out_shape=jax.ShapeDtypeStruct(cv.shape, F32))(cv, g)


def _adamw(name, w, g, m, v):
    R, C = w.shape
    parts = g.ndim == 3
    n_parts = g.shape[0] if parts else 1
    T = _tile(R, max(8, 262144 // C), 8)

    def body(w_ref, g_ref, m_ref, v_ref, go_ref, d_ref, mo_ref, vo_ref):
        if parts:
            gv = g_ref[0].astype(F32)
            for p in range(1, n_parts):
                gv = gv + g_ref[p].astype(F32)
        else:
            gv = g_ref[...]
        mn = ADAM_B1 * m_ref[...] + (1.0 - ADAM_B1) * gv
        vn = ADAM_B2 * v_ref[...] + (1.0 - ADAM_B2) * (gv * gv)
        m_hat = mn / (1.0 - ADAM_B1 ** ADAM_STEP)
        v_hat = vn / (1.0 - ADAM_B2 ** ADAM_STEP)
        go_ref[...] = gv
        d_ref[...] = -ADAM_LR * (m_hat / (jnp.sqrt(v_hat) + ADAM_EPS) + ADAM_WD * w_ref[...])
        mo_ref[...] = mn
        vo_ref[...] = vn

    tile = pl.BlockSpec((T, C), lambda i: (i, 0))
    g_spec = pl.BlockSpec((n_parts, T, C), lambda i: (0, i, 0)) if parts else tile
    return pl.pallas_call(
        body, name=name, grid=(R // T,),
        in_specs=[tile, g_spec, tile, tile], out_specs=[tile] * 4,
        out_shape=[jax.ShapeDtypeStruct((R, C), F32)] * 4,
        compiler_params=_cparams("parallel"),
    )(w, g, m, v)


def _dev_index(px, py, pc):
    return 4 * px + 2 * py + pc


def _all_gather(name, arrs):
    n = len(arrs)

    def body(*refs):
        ins, outs = refs[:n], refs[n:2 * n]
        send_sems, recv_sems, local_sems = refs[2 * n:]
        x, y, c = lax.axis_index("x"), lax.axis_index("y"), lax.axis_index("c")
        me, sibling = (x, y, c), (x, y, 1 - c)
        chips = [(1 - x, y), (x, 1 - y), (1 - x, 1 - y)]

        def copy(a, k, block, to, src=None):
            slot = outs[a].at[_dev_index(*block)]
            return pltpu.make_async_remote_copy(
                src_ref=slot if src is None else src, dst_ref=slot, send_sem=send_sems.at[a, k], recv_sem=recv_sems.at[a, k],
                device_id=to, device_id_type=MESH)

        mine = [pltpu.make_async_copy(ins[a], outs[a].at[_dev_index(*me)], local_sems.at[a]) for a in range(n)]
        for cp in mine:
            cp.start()
        first = []
        for a in range(n):
            first.append(copy(a, 0, me, sibling, src=ins[a]))
            first += [copy(a, 1 + j, me, (*chip, c), src=ins[a]) for j, chip in enumerate(chips)]
        for cp in first:
            cp.start()
        passed = []
        for j, chip in enumerate(chips):
            for a in range(n):
                copy(a, 1 + j, (*chip, c), me).wait_recv()
                fwd = copy(a, 4 + j, (*chip, c), sibling)
                fwd.start()
                passed.append(fwd)
        for a in range(n):
            copy(a, 0, sibling, me).wait_recv()
            for j, chip in enumerate(chips):
                copy(a, 4 + j, (*chip, 1 - c), me).wait_recv()
        for cp in first + passed:
            cp.wait_send()
        for cp in mine:
            cp.wait()

    return pl.pallas_call(
        body, name=name,
        in_specs=[HBM] * n, out_specs=[HBM] * n,
        out_shape=[jax.ShapeDtypeStruct((N_DEV, *a.shape), a.dtype) for a in arrs],
        scratch_shapes=[pltpu.SemaphoreType.DMA((n, N_DEV - 1)), pltpu.SemaphoreType.DMA((n, N_DEV - 1)), pltpu.SemaphoreType.DMA((n,))],
    )(*arrs)


N_COPIES = {"all_to_all": N_DEV - 1, "gather_chips": 4, "forward": 3}


def _exchange_copies(kind, src_ref, land_ref, send_sems, recv_sems, sending):
    x, y, c = lax.axis_index("x"), lax.axis_index("y"), lax.axis_index("c")
    me = _dev_index(x, y, c)
    others = [(1 - x, y), (x, 1 - y), (1 - x, 1 - y)]
    if kind == "all_to_all":
        flips = [(dx, dy, dc) for dx in (0, 1) for dy in (0, 1) for dc in (0, 1)][1:]
        peers = [(1 - x if dx else x, 1 - y if dy else y, 1 - c if dc else c) for dx, dy, dc in flips]
        plan = [(p, src_ref.at[_dev_index(*p)], me if sending else _dev_index(*p)) for p in peers]
    elif kind == "gather_chips":
        peers = [(x, y, 1 - c)] + [(*o, c) for o in others]
        plan = [(p, src_ref, me if sending else _dev_index(*p)) for p in peers]
    else:
        plan = [((x, y, 1 - c), land_ref.at[_dev_index(*o, c)], _dev_index(*o, c if sending else 1 - c)) for o in others]
    return [pltpu.make_async_remote_copy(src_ref=src, dst_ref=land_ref.at[slot], send_sem=send_sems.at[k], recv_sem=recv_sems.at[k],
                                         device_id=peer, device_id_type=MESH)
            for k, (peer, src, slot) in enumerate(plan)]


def _exchange_start(name, kind, srcs, lands=None):
    if lands is None:
        lands = [lax.empty((N_DEV, *s.shape) if kind == "gather_chips" else s.shape, s.dtype) for s in srcs]
    n = len(lands)
    ops = ([] if srcs is None else list(srcs)) + list(lands)
    m = len(ops)

    def body(*refs):
        src_refs = [None] * n if srcs is None else refs[:n]
        land_refs = refs[m - n:m]
        send_sems, recv_sems, token = refs[m:m + n], refs[m + n:m + 2 * n], refs[-1]
        for a in range(n):
            for cp in _exchange_copies(kind, src_refs[a], land_refs[a], send_sems[a], recv_sems[a], True):
                cp.start()
        token[...] = jnp.zeros_like(token)

    sems = [pltpu.SemaphoreType.DMA((N_COPIES[kind],))] * (2 * n)
    outs = pl.pallas_call(
        body, name=name,
        out_shape=sems + [pltpu.HBM(o.shape, o.dtype) for o in ops] + [jax.ShapeDtypeStruct((8, LANES), F32)],
        in_specs=[HBM] * m,
        out_specs=[SEM] * (2 * n) + [HBM] * m + [pl.BlockSpec(memory_space=pltpu.VMEM)],
        input_output_aliases={i: 2 * n + i for i in range(m)},
        compiler_params=SIDE_EFFECT,
    )(*[pltpu.with_memory_space_constraint(o, pltpu.HBM) for o in ops])
    thru = outs[2 * n:2 * n + m]
    return outs[:n], outs[n:2 * n], (None if srcs is None else thru[:n]), thru[m - n:], outs[-1]


def _exchange_wait(name, kind, send_sems, recv_sems, srcs, lands, after):
    n = len(lands)
    ops = ([] if srcs is None else list(srcs)) + list(lands)
    m = len(ops)

    def body(*refs):
        src_refs = [None] * n if srcs is None else refs[:n]
        land_refs = refs[m - n:m]
        send_refs, recv_refs = refs[m:m + n], refs[m + n:m + 2 * n]
        for a in range(n):
            for cp in _exchange_copies(kind, src_refs[a], land_refs[a], send_refs[a], recv_refs[a], False):
                cp.wait_send()
                cp.wait_recv()

    outs = pl.pallas_call(
        body, name=name,
        out_shape=[pltpu.HBM(o.shape, o.dtype) for o in ops],
        in_specs=[HBM] * m + [SEM] * (2 * n) + [pl.BlockSpec(memory_space=pl.ANY)],
        out_specs=[HBM] * m,
        input_output_aliases={i: i for i in range(m)},
        compiler_params=SIDE_EFFECT,
    )(*ops, *send_sems, *recv_sems, after)
    return outs[m - n:]


def _with_own(land, own, me):
    return lax.dynamic_update_slice_in_dim(land, own, me, 0)


def _shards_to_cols(g):
    return jnp.transpose(g, (1, 0, 2)).reshape(g.shape[1], N_DEV * g.shape[2])


def _cols_to_shards(a):
    R, Ctot = a.shape
    return jnp.transpose(a.reshape(R, N_DEV, Ctot // N_DEV), (1, 0, 2))


def kernel(x, c, ctx, c_ctx, norm_attn_w, norm_mlp_w, w_ada, b_ada, w_in, attn_sink, pool_w, pool_scale, w_out, w_mlp_up, w_mlp_down, final_norm_w, loss_target, m_c_ctx, m_norm_attn_w, m_norm_mlp_w, m_w_ada, m_b_ada, m_w_in, m_attn_sink, m_pool_w, m_pool_scale, m_w_out, m_w_mlp_up, m_w_mlp_down, m_final_norm_w, v_c_ctx, v_norm_attn_w, v_norm_mlp_w, v_w_ada, v_b_ada, v_w_in, v_attn_sink, v_pool_w, v_pool_scale, v_w_out, v_w_mlp_up, v_w_mlp_down, v_final_norm_w):
    _, L, D = x.shape
    H = attn_sink.shape[1]
    A = H * HEAD_DIM
    KV = A // GQA
    P = pool_scale.shape[1]
    MODW = N_MOD * D
    ws = MODW // N_DEV
    gd = P // POOL_GROUPS
    me = _dev_index(lax.axis_index("x"), lax.axis_index("y"), lax.axis_index("c"))

    x2d, ctx2d, tgt = x[0], ctx[0], loss_target[0]
    cctx_row = c_ctx.reshape(1, D)
    wf_row = final_norm_w.reshape(1, D)
    w_ada_l = w_ada[0]
    pool_w_l = pool_w[0].reshape(POOL_GROUPS * (gd // N_DEV), gd)

    (c_all,) = _all_gather("gather_cond", [c])
    cond = jnp.concatenate([c_all[:, 0, :], cctx_row, jnp.zeros((COND_ROWS - N_DEV - 1, D), F32)], axis=0)
    b_sh = lax.dynamic_slice_in_dim(b_ada, me * ws, ws, axis=1)
    (mods_sh,) = _mm("ada_mod", cond, w_ada_l, "nn", [F32], SMALL_TILES, a_pre=_silu, extras=[("n", b_sh)], epilogue=lambda acc, b: (acc + b,))
    (mods_g,) = _all_gather("gather_mods", [mods_sh])

    w_srcs = [w_in[0].astype(BF16), w_out[0].astype(BF16), pool_w_l.astype(BF16), w_mlp_up[0].astype(BF16), w_mlp_down[0].astype(BF16)]
    w_srcs, mods_g = lax.optimization_barrier((w_srcs, mods_g))
    gw_send, gw_recv, gw_src, gw_land, _ = _exchange_start("gather_weights_start", "gather_chips", w_srcs)

    def weights(tag, lo, hi, after_chips, after_forward):
        lands = _exchange_wait(f"gather_{tag}_wait", "gather_chips", gw_send[lo:hi], gw_recv[lo:hi], gw_src[lo:hi], gw_land[lo:hi],
                               after_chips)
        f_send, f_recv, _, f_land, f_token = _exchange_start(f"forward_{tag}_start", "forward", None, lands)
        lands = _exchange_wait(f"forward_{tag}_wait", "forward", f_send, f_recv, None, f_land,
                               f_token if after_forward is None else after_forward)
        return [_with_own(l, s[None], me) for l, s in zip(lands, w_srcs[lo:hi])]

    mods = _shards_to_cols(mods_g)
    mod_b = lax.dynamic_slice_in_dim(mods, me, 1, axis=0)
    sh_a, sc_a, g_a, sh_m, sc_m, g_m = [mod_b[:, i * D:(i + 1) * D] for i in range(N_MOD)]
    csh_a, csc_a = mods[N_DEV:N_DEV + 1, :D], mods[N_DEV:N_DEV + 1, D:2 * D]

    cos, sin = _rope_tables(L)
    h = _norm_fwd("norm_attn", x2d, norm_attn_w, sc_a, sh_a)
    hc = _norm_fwd("norm_attn_ctx", ctx2d, norm_attn_w, csc_a, csh_a)
    (win_g,) = weights("w_in", 0, 1, h, None)
    W_in = _shards_to_cols(win_g)
    W_qk, W_kv = W_in[:, :A + KV], W_in[:, A:A + 2 * KV]
    W_uv = jnp.concatenate([W_in[:, A + 2 * KV:], W_in[:, A + KV:A + 2 * KV]], axis=1)
    (qk,) = _mm("in_proj_qk", h, W_qk, "nn", [BF16], (1024, A + KV, D), extras=[("m", cos), ("m", sin)],
                epilogue=lambda acc, cs, sn: (_rope(acc, cs, sn),))
    (uv,) = _mm("in_proj_uv", h, W_uv, "nn", [F32], (1024, P + KV, D))
    (kvc,) = _mm("in_proj_ctx", hc, W_kv, "nn", [BF16], SMALL_TILES)
    attn, lse = _attn_fwd(qk, uv, kvc, attn_sink, A, KV, P)
    wout_g, pw_g = weights("w_out", 1, 3, qk, attn)
    W_out = wout_g.reshape(A + P, D)
    PW = jnp.transpose(pw_g.reshape(N_DEV, POOL_GROUPS, gd // N_DEV, gd), (1, 0, 2, 3)).reshape(POOL_GROUPS, gd, gd)
    pool_out, pooled = _pool_fwd(uv, PW, pool_scale)
    ap = jnp.concatenate([attn, pool_out], axis=1)
    o, x1, hm = _mm("out_proj_norm", ap, W_out, "nn", [F32, F32, BF16], (256, D, D), chunk=128, epilogue=_out_proj_epilogue,
                    extras=[("mn", x2d), ("n", g_a), ("n", norm_mlp_w), ("n", sc_m), ("n", sh_m)])
    W_up, wdown_g = weights("w_mlp", 3, 5, attn, x1)
    W_down = wdown_g.reshape(-1, D)
    up, act = _mm("mlp_up", hm, W_up, "nn", [F32, BF16], (1024, 1024, 2048), epilogue=lambda acc: (acc, _relu2(acc)), b_shards=True)
    d_x2, d_mlp, d_wf, d_gm, loss_row = _mm(
        "mlp_down_loss", act, W_down, "nn", [F32, BF16], (512, D, 1024), chunk=128, n_sums=3, vmem=FUSED_VMEM_LIMIT,
        epilogue=_mlp_down_epilogue, extras=[("mn", x1), ("mn", tgt), ("n", g_m), ("n", wf_row)])
    loss_p = loss_row[:, :1]

    (d_up,) = _mm("mlp_down_bwd_act", d_mlp, W_down, "nt", [BF16], (1024, 1024, 2048), extras=[("mn", up)], epilogue=lambda acc, uu: (acc * (2.0 * jnp.maximum(uu, 0.0)),))
    (gW_down,) = _mm("mlp_down_bwd_w", act, d_mlp, "tn", [BF16], (1024, 2048, 1024))
    (gW_up_s,) = _mm("mlp_up_bwd_w", hm, d_up, "tn", [BF16], (1024, 1024, 1024), out_shards=True)
    (d_hm,) = _mm("mlp_up_bwd_act", d_up, W_up, "nt", [F32], (1024, 1024, 1024), b_shards=True)
    g_mlp_srcs = [gW_up_s, gW_down.reshape(N_DEV, -1, D)]
    g_mlp = _exchange_start("grads_mlp_start", "all_to_all", g_mlp_srcs)
    zrow = jnp.zeros((1, D), F32) + g_mlp[4][0, 0]
    d_x1, s_sh_m, s_sc_m, s_w_nm, d_ga, d_o = _norm_bwd("norm_mlp_bwd", x1, d_hm, d_x2, norm_mlp_w, sc_m, zrow, gate=(o, g_a))

    (d_ap,) = _mm("out_proj_bwd_act", d_o, W_out, "nt", [F32], (1024, 1024, 2048))
    (gW_out,) = _mm("out_proj_bwd_w", ap, d_o, "tn", [BF16], (1024, 2048, 1024))
    d_pooled, gPW, d_pscale = _pool_bwd_mix(d_ap, pooled, PW, pool_scale, A // P)
    gpw_s = jnp.transpose(gPW.astype(BF16).reshape(POOL_GROUPS, N_DEV, gd // N_DEV, gd), (1, 0, 2, 3)).reshape(N_DEV, -1, gd)
    g_mix_srcs = [gW_out.reshape(N_DEV, (A + P) // N_DEV, D), gpw_s]
    g_mix = _exchange_start("grads_mix_start", "all_to_all", g_mix_srcs)
    lse = lse + g_mix[4][0, 0]
    d_u = _pool_bwd_window(d_pooled)
    d_q, rd, dsink_rows, d_kvc = _attn_bwd_dq(qk, uv, kvc, attn_sink, d_ap, lse, cos, sin, A, KV, P)
    d_k, d_v = _attn_bwd_dkv(qk, uv, d_ap, lse.T, rd.T, cos, sin, A, KV, P)
    d_sink = _sum_rows("sink_grad", dsink_rows)
    d_p = jnp.concatenate([d_q, d_k, d_v, d_u], axis=1)
    d_kvc_b = d_kvc.astype(BF16)
    (gW_kv_ctx,) = _mm("in_proj_ctx_bwd_w", hc, d_kvc_b, "tn", [F32], SMALL_TILES)
    (d_hc,) = _mm("in_proj_ctx_bwd_act", d_kvc_b, W_kv, "nt", [F32], SMALL_TILES)
    gW_in_init = jnp.pad(gW_kv_ctx, ((0, 0), (A, P)))
    (gW_in,) = _mm("in_proj_bwd_w", h, d_p, "tn", [BF16], (1024, 1280, 1024), extras=[("mn", gW_in_init)], epilogue=lambda acc, init: (acc + init,))
    g_in_srcs = [_cols_to_shards(gW_in)]
    g_in = _exchange_start("grads_in_start", "all_to_all", g_in_srcs)
    grad_x, s_sh_a, s_sc_a, s_w_na = _mm(
        "in_proj_bwd_norm", d_p, W_in, "nt", [F32], (256, D, A + 2 * KV + P), chunk=128, n_sums=3,
        epilogue=lambda acc, xr, dres, w, sc: _modulated_norm_bwd(xr, acc, dres, w, sc),
        extras=[("mn", x2d), ("mn", d_x1), ("n", norm_attn_w), ("n", sc_a + g_in[4][0, 0])])
    _, s_csh, s_csc, s_w_na = _norm_bwd("norm_attn_ctx_bwd", ctx2d, d_hc, jnp.zeros_like(ctx2d), norm_attn_w, csc_a, s_w_na)

    pad_l = lambda a: jnp.pad(a, ((0, 0), (0, LANES - a.shape[1])))
    d_mod_b = jnp.concatenate([s_sh_a, s_sc_a, d_ga, s_sh_m, s_sc_m, d_gm], axis=1)
    summed = jnp.concatenate([s_csh, s_csc, s_w_na, s_w_nm, d_wf, d_pscale, pad_l(d_sink), pad_l(loss_p)], axis=1)
    (small_g,) = _all_gather("gather_small", [jnp.concatenate([d_mod_b, summed], axis=1)])
    small_g = small_g[:, 0, :]
    tot = _sum_rows("small_sum", small_g[:, MODW:])
    off = [0]
    for wdt in (D, D, D, D, D, P, LANES, LANES):
        off.append(off[-1] + wdt)
    seg = lambda i: tot[:, off[i]:off[i + 1]]
    g_norm_attn, g_norm_mlp, g_final, g_pscale = seg(2), seg(3), seg(4), seg(5)
    g_sink, loss = seg(6)[:, :H], seg(7)[0, 0]
    d_mod_ctx = jnp.concatenate([seg(0), seg(1), jnp.zeros((1, MODW - 2 * D), F32)], axis=1)
    d_mod = jnp.concatenate([small_g[:, :MODW], d_mod_ctx, jnp.zeros((COND_ROWS - N_DEV - 1, MODW), F32)], axis=0)
    g_b_ada = _sum_rows("b_ada_grad", d_mod[:N_DEV + 1])
    d_mod_sh = lax.dynamic_slice_in_dim(d_mod, me * ws, ws, axis=1)
    (g_w_ada,) = _mm("ada_bwd_w", cond, d_mod_sh, "tn", [F32], SMALL_TILES, a_pre=_silu)
    (d_cond_p,) = _mm("ada_bwd_cond", d_mod_sh, w_ada_l, "nt", [F32], SMALL_TILES)
    (d_cctx_g,) = _all_gather("gather_cctx", [d_cond_p[N_DEV:N_DEV + 1]])
    g_c_ctx = _silu_grad_mul(cctx_row, _sum_rows("cctx_sum", d_cctx_g[:, 0, :]))

    def arrived(name, started, srcs):
        lands = _exchange_wait(name, "all_to_all", started[0], started[1], started[2], started[3], g_c_ctx)
        return [_with_own(l, lax.dynamic_index_in_dim(s, me, 0, keepdims=True), me) for l, s in zip(lands, srcs)]

    r_up, r_down = arrived("grads_mlp_wait", g_mlp, g_mlp_srcs)
    r_out, r_pw = arrived("grads_mix_wait", g_mix, g_mix_srcs)
    (r_in,) = arrived("grads_in_wait", g_in, g_in_srcs)

    results = {
        "c_ctx": _adamw("adam_c_ctx", cctx_row, g_c_ctx, m_c_ctx.reshape(1, D), v_c_ctx.reshape(1, D)),
        "norm_attn_w": _adamw("adam_norm_attn", norm_attn_w, g_norm_attn, m_norm_attn_w, v_norm_attn_w),
        "norm_mlp_w": _adamw("adam_norm_mlp", norm_mlp_w, g_norm_mlp, m_norm_mlp_w, v_norm_mlp_w),
        "w_ada": _adamw("adam_w_ada", w_ada_l, g_w_ada, m_w_ada[0], v_w_ada[0]),
        "b_ada": _adamw("adam_b_ada", b_ada, g_b_ada, m_b_ada, v_b_ada),
        "w_in": _adamw("adam_w_in", w_in[0], r_in, m_w_in[0], v_w_in[0]),
        "attn_sink": _adamw("adam_sink", attn_sink, g_sink, m_attn_sink, v_attn_sink),
        "pool_w": _adamw("adam_pool_w", pool_w_l, r_pw, m_pool_w[0].reshape(pool_w_l.shape), v_pool_w[0].reshape(pool_w_l.shape)),
        "pool_scale": _adamw("adam_pool_scale", pool_scale, g_pscale, m_pool_scale, v_pool_scale),
        "w_out": _adamw("adam_w_out", w_out[0], r_out, m_w_out[0], v_w_out[0]),
        "w_mlp_up": _adamw("adam_w_up", w_mlp_up[0], r_up, m_w_mlp_up[0], v_w_mlp_up[0]),
        "w_mlp_down": _adamw("adam_w_down", w_mlp_down[0], r_down, m_w_mlp_down[0], v_w_mlp_down[0]),
        "final_norm_w": _adamw("adam_final_norm", wf_row, g_final, m_final_norm_w.reshape(1, D), v_final_norm_w.reshape(1, D)),
    }
    shapes = {"c_ctx": c_ctx.shape, "norm_attn_w": norm_attn_w.shape, "norm_mlp_w": norm_mlp_w.shape, "w_ada": w_ada.shape,
              "b_ada": b_ada.shape, "w_in": w_in.shape, "attn_sink": attn_sink.shape, "pool_w": pool_w.shape,
              "pool_scale": pool_scale.shape, "w_out": w_out.shape, "w_mlp_up": w_mlp_up.shape, "w_mlp_down": w_mlp_down.shape,
              "final_norm_w": final_norm_w.shape}
    outs = [loss, grad_x.reshape(x.shape)]
    for part in range(4):
        outs += [results[name][part].reshape(shape) for name, shape in shapes.items()]
    return tuple(outs)
```

```python
import functools

import jax
import jax.numpy as jnp
import numpy as np
from jax import lax
from jax.experimental import pallas as pl
from jax.experimental.pallas import tpu as pltpu

F32 = jnp.float32
BF16 = jnp.bfloat16
I32 = jnp.int32

HEAD_DIM = 64
GQA = 4
BLOCK = 128
GRID_W = 64
ROPE_BASE = 10000.0
POOL_WINDOWS = (2, 4, 8, 16)
POOL_GROUPS = len(POOL_WINDOWS)
HALO = 8
N_MOD = 6
EPS = 1e-6
NEG_INF = -1e30
ADAM_LR = 0.001
ADAM_B1 = 0.9
ADAM_B2 = 0.999
ADAM_EPS = 1e-08
ADAM_WD = 0.01
ADAM_STEP = 10
N_DEV = 8
COND_ROWS = 2 * N_DEV
LANES = 128
SUBLANES_16BIT = 16
VMEM_LIMIT = 48 * 1024 * 1024
FUSED_VMEM_LIMIT = 56 * 1024 * 1024
SMALL_TILES = (512, 1024, 512)
MESH = pl.DeviceIdType.MESH
HBM = pl.BlockSpec(memory_space=pltpu.HBM)
SEM = pl.BlockSpec(memory_space=pltpu.SEMAPHORE)
SIDE_EFFECT = pltpu.CompilerParams(has_side_effects=pltpu.SideEffectType.DATAFLOW_SIDE_EFFECTING)


def _cparams(*sem):
    return pltpu.CompilerParams(dimension_semantics=sem, vmem_limit_bytes=VMEM_LIMIT)


def _tile(n, pref, align):
    if n <= pref:
        return n
    t = (pref // align) * align
    while t >= align:
        if n % t == 0:
            return t
        t -= align
    return n


def _dot(a, b):
    return lax.dot_general(a, b, (((1,), (0,)), ((), ())), preferred_element_type=F32)


def _dot_nt(a, b):
    return lax.dot_general(a, b, (((1,), (1,)), ((), ())), preferred_element_type=F32)


def _dot_tn(a, b):
    return lax.dot_general(a, b, (((0,), (0,)), ((), ())), preferred_element_type=F32)


_DOTS = {"nn": _dot, "nt": _dot_nt, "tn": _dot_tn}


def _mm(name, a, b, mode, out_dtypes, tiles, *, epilogue=None, extras=(), a_pre=None, n_sums=0, chunk=None,
        b_shards=False, out_shards=False, vmem=VMEM_LIMIT):
    if mode == "nn":
        M, K = a.shape
        K2, N = (b.shape[1], N_DEV * b.shape[2]) if b_shards else b.shape
    elif mode == "nt":
        M, K = a.shape
        N, K2 = (b.shape[1], N_DEV * b.shape[2]) if b_shards else b.shape
    else:
        (K, M), (K2, N) = a.shape, b.shape
    assert K == K2 and not (b_shards and mode == "tn"), (name, a.shape, b.shape)
    n_span = N // N_DEV if out_shards or (b_shards and mode == "nn") else N
    k_span = K // N_DEV if b_shards and mode == "nt" else K
    tm = _tile(M, tiles[0], LANES if mode == "tn" else SUBLANES_16BIT)
    tn = _tile(n_span, tiles[1], LANES)
    tk = _tile(k_span, tiles[2], SUBLANES_16BIT if mode == "tn" else LANES)
    nk, nb, kb = K // tk, n_span // tn, k_span // tk
    rows = tm if chunk is None else min(chunk, tm)
    n_ex, n_out = len(extras), len(out_dtypes)
    use_acc = nk > 1 or rows < tm
    assert n_sums == 0 or N == tn, name

    def product(a_ref, b_ref):
        at = a_ref[...]
        if a_pre is not None:
            at = a_pre(at)
        return _DOTS[mode](at.astype(BF16), b_ref[...].astype(BF16))

    def apply(acc, ex, out_refs, sl):
        res = (acc,) if epilogue is None else epilogue(acc, *ex)
        for o_ref, o in zip(out_refs, res[:n_out]):
            o_ref[sl, :] = o.astype(o_ref.dtype)
        return tuple(res[n_out:])

    def finish(acc, ex_refs, out_refs, sum_refs):
        if rows == tm:
            acc = acc if not use_acc else acc[...]
            sums = apply(acc, [r[...] for r in ex_refs], out_refs, slice(None))
        else:
            def one(ci, sums):
                sl = pl.ds(pl.multiple_of(ci * rows, rows), rows)
                ex = [r[...] if kind == "n" else r[sl, :] for (kind, _), r in zip(extras, ex_refs)]
                return tuple(s + v for s, v in zip(sums, apply(acc[sl, :], ex, out_refs, sl)))
            sums = lax.fori_loop(0, tm // rows, one, tuple(jnp.zeros((1, tn), F32) for _ in range(n_sums)))
        first = pl.program_id(0) == 0
        for s_ref, sv in zip(sum_refs, sums):
            @pl.when(first)
            def _(s_ref=s_ref, sv=sv):
                s_ref[...] = sv

            @pl.when(jnp.logical_not(first))
            def _(s_ref=s_ref, sv=sv):
                s_ref[...] += sv

    def body(a_ref, b_ref, *rest):
        ex_refs, out_refs = rest[:n_ex], rest[n_ex:n_ex + n_out]
        sum_refs = rest[n_ex + n_out:n_ex + n_out + n_sums]
        if not use_acc:
            finish(product(a_ref, b_ref), ex_refs, out_refs, sum_refs)
            return
        acc_ref = rest[-1]
        k = pl.program_id(2)

        @pl.when(k == 0)
        def _():
            acc_ref[...] = product(a_ref, b_ref)

        @pl.when(k > 0)
        def _():
            acc_ref[...] += product(a_ref, b_ref)

        @pl.when(k == nk - 1)
        def _():
            finish(acc_ref, ex_refs, out_refs, sum_refs)

    a_spec = pl.BlockSpec((tk, tm), lambda i, j, k: (k, i)) if mode == "tn" else pl.BlockSpec((tm, tk), lambda i, j, k: (i, k))
    if not b_shards:
        b_spec = pl.BlockSpec((tn, tk), lambda i, j, k: (j, k)) if mode == "nt" else pl.BlockSpec((tk, tn), lambda i, j, k: (k, j))
    elif mode == "nn":
        b_spec = pl.BlockSpec((None, tk, tn), lambda i, j, k: (j // nb, k, j % nb))
    else:
        b_spec = pl.BlockSpec((None, tn, tk), lambda i, j, k: (k // kb, j, k % kb))
    ex_specs = []
    for kind, arr in extras:
        if kind == "mn":
            ex_specs.append(pl.BlockSpec((tm, tn), lambda i, j, k: (i, j)))
        elif kind == "n":
            ex_specs.append(pl.BlockSpec((1, tn), lambda i, j, k: (0, j)))
        else:
            ex_specs.append(pl.BlockSpec((tm, arr.shape[1]), lambda i, j, k: (i, 0)))
    if out_shards:
        out_specs = [pl.BlockSpec((None, tm, tn), lambda i, j, k: (j // nb, i, j % nb)) for _ in out_dtypes]
        out_shape = [jax.ShapeDtypeStruct((N_DEV, M, n_span), d) for d in out_dtypes]
    else:
        out_specs = [pl.BlockSpec((tm, tn), lambda i, j, k: (i, j)) for _ in out_dtypes]
        out_shape = [jax.ShapeDtypeStruct((M, N), d) for d in out_dtypes]
    out_specs += [pl.BlockSpec((1, tn), lambda i, j, k: (0, 0))] * n_sums
    out_shape += [jax.ShapeDtypeStruct((1, N), F32)] * n_sums
    return pl.pallas_call(
        body,
        name=name,
        grid=(M // tm, N // tn, nk),
        in_specs=[a_spec, b_spec] + ex_specs,
        out_specs=out_specs,
        out_shape=out_shape,
        scratch_shapes=[pltpu.VMEM((tm, tn), F32)] if use_acc else [],
        compiler_params=pltpu.CompilerParams(
            dimension_semantics=("arbitrary",) * 3 if n_sums else ("parallel", "parallel", "arbitrary"), vmem_limit_bytes=vmem),
    )(a, b, *[arr for _, arr in extras])


def _silu(v):
    return v / (1.0 + jnp.exp(-v))


def _relu2(v):
    r = jnp.maximum(v, 0.0)
    return r * r


def _rope_tables(L):
    half = HEAD_DIM // 2
    inv_freq = np.float32(ROPE_BASE) ** (-np.arange(0, half, 2, dtype=np.float32) / np.float32(half))
    t = np.arange(L)
    row, col = t // GRID_W, t % GRID_W
    ang_r = row.astype(np.float32)[:, None] * inv_freq[None, :]
    ang_c = col.astype(np.float32)[:, None] * inv_freq[None, :]
    cos = np.concatenate([np.cos(ang_r), np.cos(ang_r), np.cos(ang_c), np.cos(ang_c)], axis=1)
    sin = np.concatenate([-np.sin(ang_r), np.sin(ang_r), -np.sin(ang_c), np.sin(ang_c)], axis=1)
    reps = LANES // HEAD_DIM
    return jnp.asarray(np.tile(cos, (1, reps)), F32), jnp.asarray(np.tile(sin, (1, reps)), F32)


def _rope(xf, cos, sin):
    quarter = HEAD_DIM // 4
    lane = lax.broadcasted_iota(I32, (xf.shape[0], LANES), 1)
    first = (lane & quarter) == 0
    outs = []
    for j in range(xf.shape[1] // LANES):
        xc = xf[:, j * LANES:(j + 1) * LANES]
        partner = jnp.where(first, pltpu.roll(xc, LANES - quarter, 1), pltpu.roll(xc, quarter, 1))
        outs.append(xc * cos + partner * sin)
    return outs[0] if len(outs) == 1 else jnp.concatenate(outs, axis=1)


def _inv_rms(xf):
    return lax.rsqrt(jnp.mean(xf * xf, axis=-1, keepdims=True) + EPS)


def _modulated_norm(xf, w, sc, sh):
    return ((xf * _inv_rms(xf)) * w) * (1.0 + sc) + sh


def _modulated_norm_bwd(xf, dh, dres, w, sc):
    r = _inv_rms(xf)
    xh = xf * r
    dn = dh * (1.0 + sc)
    dxh = dn * w
    dx = dres + r * (dxh - xh * jnp.mean(dxh * xh, axis=-1, keepdims=True))
    col = lambda v: jnp.sum(v, axis=0, keepdims=True)
    return dx, col(dh), col(dh * (xh * w)), col(dn * xh)


def _norm_fwd(name, x, w, sc, sh):
    L, D = x.shape
    T = _tile(L, 256, 8)

    def body(x_ref, w_ref, sc_ref, sh_ref, h_ref):
        h_ref[...] = _modulated_norm(x_ref[...], w_ref[...], sc_ref[...], sh_ref[...]).astype(BF16)

    row = pl.BlockSpec((1, D), lambda i: (0, 0))
    return pl.pallas_call(
        body, name=name, grid=(L // T,),
        in_specs=[pl.BlockSpec((T, D), lambda i: (i, 0)), row, row, row],
        out_specs=pl.BlockSpec((T, D), lambda i: (i, 0)),
        out_shape=jax.ShapeDtypeStruct((L, D), BF16),
        compiler_params=_cparams("parallel"),
    )(x, w, sc, sh)


def _norm_bwd(name, x, dh, dres, w, sc, w_init, gate=None):
    L, D = x.shape
    T = _tile(L, 256, 8)
    with_gate = gate is not None

    def body(x_ref, dh_ref, dres_ref, w_ref, sc_ref, wi_ref, *rest):
        if with_gate:
            o_ref, g_ref, dx_ref, ssh_ref, ssc_ref, sw_ref, sg_ref, do_ref = rest
        else:
            dx_ref, ssh_ref, ssc_ref, sw_ref = rest
        i = pl.program_id(0)

        @pl.when(i == 0)
        def _():
            ssh_ref[...] = jnp.zeros_like(ssh_ref)
            ssc_ref[...] = jnp.zeros_like(ssc_ref)
            sw_ref[...] = wi_ref[...]
            if with_gate:
                sg_ref[...] = jnp.zeros_like(sg_ref)

        dx, s_sh, s_sc, s_w = _modulated_norm_bwd(x_ref[...], dh_ref[...], dres_ref[...], w_ref[...], sc_ref[...])
        ssh_ref[...] += s_sh
        ssc_ref[...] += s_sc
        sw_ref[...] += s_w
        dx_ref[...] = dx
        if with_gate:
            sg_ref[...] += jnp.sum(dx * o_ref[...], axis=0, keepdims=True)
            do_ref[...] = (g_ref[...] * dx).astype(BF16)

    tile = pl.BlockSpec((T, D), lambda i: (i, 0))
    row = pl.BlockSpec((1, D), lambda i: (0, 0))
    in_specs = [tile, tile, tile, row, row, row]
    out_specs = [tile, row, row, row]
    out_shape = [jax.ShapeDtypeStruct((L, D), F32)] + [jax.ShapeDtypeStruct((1, D), F32)] * 3
    args = [x, dh, dres, w, sc, w_init]
    if with_gate:
        in_specs += [tile, row]
        out_specs += [row, tile]
        out_shape += [jax.ShapeDtypeStruct((1, D), F32), jax.ShapeDtypeStruct((L, D), BF16)]
        args += list(gate)
    return pl.pallas_call(
        body, name=name, grid=(L // T,), in_specs=in_specs, out_specs=out_specs, out_shape=out_shape,
        compiler_params=_cparams("arbitrary"),
    )(*args)


def _out_proj_epilogue(acc, xr, g, w, sc, sh):
    x1 = xr + g * acc
    return acc, x1, _modulated_norm(x1, w, sc, sh)


def _mlp_up_bwd_epilogue(acc, x1, dres, o, w, sc, g):
    dx, s_sh, s_sc, s_w = _modulated_norm_bwd(x1, acc, dres, w, sc)
    return dx, g * dx, s_sh, s_sc, s_w, jnp.sum(dx * o, axis=0, keepdims=True)


def _mlp_down_epilogue(acc, x1, tgt, g, wf):
    D = acc.shape[1]
    x2 = x1 + g * acc
    r = _inv_rms(x2)
    xh = x2 * r
    err = xh * wf - tgt
    loss = 0.5 * jnp.sum(jnp.mean(err * err, axis=-1, keepdims=True), axis=0, keepdims=True)
    dy = err / D
    dxh = dy * wf
    dx = r * (dxh - xh * jnp.mean(dxh * xh, axis=-1, keepdims=True))
    col = lambda v: jnp.sum(v, axis=0, keepdims=True)
    return dx, g * dx, col(dy * xh), col(dx * acc), jnp.broadcast_to(loss, (1, D))


def _heads(ref, first, n):
    return jnp.concatenate([ref[:, (first + g) * HEAD_DIM:(first + g + 1) * HEAD_DIM] for g in range(n)], axis=0)


def _put_col(tile, h, col):
    lane = lax.broadcasted_iota(I32, tile.shape, 1)
    return jnp.where(lane == h, col, tile)


def _get_col(tile, h):
    lane = lax.broadcasted_iota(I32, tile.shape, 1)
    return jnp.sum(jnp.where(lane == h, tile, 0.0), axis=1, keepdims=True)


def _attn_mask(n, L, C):
    shape = (GQA * BLOCK, 3 * BLOCK + C)
    qi = lax.broadcasted_iota(I32, shape, 0) & (BLOCK - 1)
    kj = lax.broadcasted_iota(I32, shape, 1)
    kpos = n * BLOCK - BLOCK + kj
    window = (kj >= qi) & (kj <= qi + 2 * BLOCK) & (kpos >= 0) & (kpos < L)
    return window | (kj >= 3 * BLOCK)


def _attn_specs(L, A, KV, C, vcol):
    nb = L // BLOCK
    kcol = A // KV
    prev = lambda n: jnp.maximum(n - 1, 0)
    nxt = lambda n: jnp.minimum(n + 1, nb - 1)
    q_spec = pl.BlockSpec((BLOCK, A), lambda n: (n, 0))
    k_specs = [pl.BlockSpec((BLOCK, KV), lambda n: (prev(n), kcol)), pl.BlockSpec((BLOCK, KV), lambda n: (n, kcol)),
               pl.BlockSpec((BLOCK, KV), lambda n: (nxt(n), kcol))]
    v_specs = [pl.BlockSpec((BLOCK, KV), lambda n: (prev(n), vcol)), pl.BlockSpec((BLOCK, KV), lambda n: (n, vcol)),
               pl.BlockSpec((BLOCK, KV), lambda n: (nxt(n), vcol))]
    kvc_spec = pl.BlockSpec((C, 2 * KV), lambda n: (0, 0))
    return q_spec, k_specs, v_specs, kvc_spec


def _keys_values(hk, k_refs, v_refs, kvc_ref, KV):
    sl = slice(hk * HEAD_DIM, (hk + 1) * HEAD_DIM)
    keys = jnp.concatenate([r[:, sl] for r in k_refs] + [kvc_ref[:, sl]], axis=0)
    vals = jnp.concatenate([r[:, sl].astype(BF16) for r in v_refs] + [kvc_ref[:, KV + hk * HEAD_DIM:KV + (hk + 1) * HEAD_DIM]], axis=0)
    return keys, vals


def _sink_col(sink_ref, hk):
    return jnp.concatenate([jnp.full((BLOCK, 1), sink_ref[0, hk * GQA + g], F32) for g in range(GQA)], axis=0)


def _attn_fwd(qk, uv, kvc, sink, A, KV, P):
    L = qk.shape[0]
    C = kvc.shape[0]
    nkv = KV // HEAD_DIM
    H = nkv * GQA
    scale = HEAD_DIM ** -0.5

    def body(sink_ref, q_ref, kp_ref, kc_ref, kn_ref, vp_ref, vc_ref, vn_ref, kvc_ref, o_ref, lse_ref):
        valid = _attn_mask(pl.program_id(0), L, C)
        lse_t = jnp.zeros((BLOCK, H), F32)
        for hk in range(nkv):
            keys, vals = _keys_values(hk, (kp_ref, kc_ref, kn_ref), (vp_ref, vc_ref, vn_ref), kvc_ref, KV)
            qs = _heads(q_ref, hk * GQA, GQA) * scale
            s = jnp.where(valid, _dot_nt(qs, keys), NEG_INF)
            sk = _sink_col(sink_ref, hk)
            m = jnp.maximum(jnp.max(s, axis=-1, keepdims=True), sk)
            p = jnp.exp(s - m)
            den = jnp.sum(p, axis=-1, keepdims=True) + jnp.exp(sk - m)
            o = _dot(p.astype(BF16), vals) * (1.0 / den)
            lse = m + jnp.log(den)
            o_ref[:, hk * GQA * HEAD_DIM:(hk + 1) * GQA * HEAD_DIM] = jnp.concatenate(
                [o[g * BLOCK:(g + 1) * BLOCK] for g in range(GQA)], axis=1).astype(BF16)
            for g in range(GQA):
                lse_t = _put_col(lse_t, hk * GQA + g, lse[g * BLOCK:(g + 1) * BLOCK])
        lse_ref[...] = lse_t

    q_spec, k_specs, v_specs, kvc_spec = _attn_specs(L, A, KV, C, P // KV)
    return pl.pallas_call(
        body, name="attn_fwd", grid=(L // BLOCK,),
        in_specs=[pl.BlockSpec(memory_space=pltpu.SMEM), q_spec] + k_specs + v_specs + [kvc_spec],
        out_specs=[pl.BlockSpec((BLOCK, A), lambda n: (n, 0)), pl.BlockSpec((BLOCK, H), lambda n: (n, 0))],
        out_shape=[jax.ShapeDtypeStruct((L, A + P), BF16), jax.ShapeDtypeStruct((L, H), F32)],
        compiler_params=_cparams("parallel"),
    )(sink, qk, qk, qk, qk, uv, uv, uv, kvc)


def _attn_bwd_dq(qk, uv, kvc, sink, dap, lse, cos, sin, A, KV, P):
    L = qk.shape[0]
    C = kvc.shape[0]
    nkv = KV // HEAD_DIM
    H = nkv * GQA
    scale = HEAD_DIM ** -0.5
    W = 3 * BLOCK

    def body(sink_ref, q_ref, kp_ref, kc_ref, kn_ref, vp_ref, vc_ref, vn_ref, kvc_ref, do_ref, lse_ref, cos_ref, sin_ref,
             dq_ref, rd_ref, ds_ref, dkvc_ref):
        n = pl.program_id(0)

        @pl.when(n == 0)
        def _():
            dkvc_ref[...] = jnp.zeros_like(dkvc_ref)

        valid = _attn_mask(n, L, C)
        lse_t = lse_ref[...]
        rd_t = jnp.zeros((BLOCK, H), F32)
        ds_t = jnp.zeros((BLOCK, H), F32)
        dq_parts = []
        for hk in range(nkv):
            sl = slice(hk * HEAD_DIM, (hk + 1) * HEAD_DIM)
            keys, vals = _keys_values(hk, (kp_ref, kc_ref, kn_ref), (vp_ref, vc_ref, vn_ref), kvc_ref, KV)
            qs = _heads(q_ref, hk * GQA, GQA) * scale
            dos = _heads(do_ref, hk * GQA, GQA).astype(BF16)
            lse = jnp.concatenate([_get_col(lse_t, hk * GQA + g) for g in range(GQA)], axis=0)
            p = jnp.exp(jnp.where(valid, _dot_nt(qs, keys), NEG_INF) - lse)
            dp = _dot_nt(dos, vals)
            rd = jnp.sum(p * dp, axis=-1, keepdims=True)
            ds = (p * (dp - rd)).astype(BF16)
            dq = _dot(ds, keys) * scale
            dkvc_ref[:, sl] += _dot_tn(ds[:, W:], qs)
            dkvc_ref[:, KV + hk * HEAD_DIM:KV + (hk + 1) * HEAD_DIM] += _dot_tn(p[:, W:].astype(BF16), dos)
            dsink = -(jnp.exp(_sink_col(sink_ref, hk) - lse) * rd)
            for g in range(GQA):
                rows = slice(g * BLOCK, (g + 1) * BLOCK)
                rd_t = _put_col(rd_t, hk * GQA + g, rd[rows])
                ds_t = _put_col(ds_t, hk * GQA + g, dsink[rows])
                dq_parts.append(dq[rows])
        rd_ref[...] = rd_t
        ds_ref[...] = ds_t
        dq_ref[...] = _rope(jnp.concatenate(dq_parts, axis=1), cos_ref[...], -sin_ref[...]).astype(BF16)

    q_spec, k_specs, v_specs, kvc_spec = _attn_specs(L, A, KV, C, P // KV)
    blk = lambda w: pl.BlockSpec((BLOCK, w), lambda n: (n, 0))
    return pl.pallas_call(
        body, name="attn_bwd_dq", grid=(L // BLOCK,),
        in_specs=[pl.BlockSpec(memory_space=pltpu.SMEM), q_spec] + k_specs + v_specs + [kvc_spec, blk(A), blk(H), blk(LANES), blk(LANES)],
        out_specs=[blk(A), blk(H), blk(H), pl.BlockSpec((C, 2 * KV), lambda n: (0, 0))],
        out_shape=[jax.ShapeDtypeStruct((L, A), BF16), jax.ShapeDtypeStruct((L, H), F32), jax.ShapeDtypeStruct((L, H), F32),
                   jax.ShapeDtypeStruct((C, 2 * KV), F32)],
        compiler_params=_cparams("arbitrary"),
    )(sink, qk, qk, qk, qk, uv, uv, uv, kvc, dap, lse, cos, sin)


def _attn_bwd_dkv(qk, uv, dap, lse_t, rd_t, cos, sin, A, KV, P):
    L = qk.shape[0]
    nb = L // BLOCK
    nkv = KV // HEAD_DIM
    H = nkv * GQA
    scale = HEAD_DIM ** -0.5
    R = 3 * GQA * BLOCK

    def body(k_ref, v_ref, qp_ref, qc_ref, qn_ref, dop_ref, doc_ref, don_ref, lsep_ref, lsec_ref, lsen_ref,
             rdp_ref, rdc_ref, rdn_ref, cos_ref, sin_ref, dk_ref, dv_ref):
        m = pl.program_id(0)
        kj = lax.broadcasted_iota(I32, (BLOCK, R), 0)
        col = lax.broadcasted_iota(I32, (BLOCK, R), 1)
        part = col // (GQA * BLOCK)
        qi = col & (BLOCK - 1)
        before = jnp.where(m >= 1, 0, -2 * BLOCK)
        after = jnp.where(m <= nb - 2, 0, 2 * BLOCK)
        valid = ((part == 0) & (kj <= qi + before)) | (part == 1) | ((part == 2) & (kj >= qi + after))
        dk_parts, dv_parts = [], []
        for hk in range(nkv):
            sl = slice(hk * HEAD_DIM, (hk + 1) * HEAD_DIM)
            km = k_ref[:, sl]
            vm = v_ref[:, sl].astype(BF16)
            qs = jnp.concatenate([_heads(q, hk * GQA, GQA) for q in (qp_ref, qc_ref, qn_ref)], axis=0) * scale
            dos = jnp.concatenate([_heads(d, hk * GQA, GQA) for d in (dop_ref, doc_ref, don_ref)], axis=0).astype(BF16)
            rows = [slice(hk * GQA + g, hk * GQA + g + 1) for g in range(GQA)]
            lse = jnp.concatenate([t[r, :] for t in (lsep_ref, lsec_ref, lsen_ref) for r in rows], axis=1)
            rdv = jnp.concatenate([t[r, :] for t in (rdp_ref, rdc_ref, rdn_ref) for r in rows], axis=1)
            p = jnp.exp(jnp.where(valid, _dot_nt(km, qs), NEG_INF) - lse)
            ds = (p * (_dot_nt(vm, dos) - rdv)).astype(BF16)
            dk_parts.append(_dot(ds, qs))
            dv_parts.append(_dot(p.astype(BF16), dos))
        dk = dk_parts[0] if nkv == 1 else jnp.concatenate(dk_parts, axis=1)
        dv = dv_parts[0] if nkv == 1 else jnp.concatenate(dv_parts, axis=1)
        dk_ref[...] = _rope(dk, cos_ref[...], -sin_ref[...]).astype(BF16)
        dv_ref[...] = dv.astype(BF16)

    prev = lambda m: jnp.maximum(m - 1, 0)
    nxt = lambda m: jnp.minimum(m + 1, nb - 1)
    three = lambda w: [pl.BlockSpec((BLOCK, w), lambda m: (prev(m), 0)), pl.BlockSpec((BLOCK, w), lambda m: (m, 0)),
                       pl.BlockSpec((BLOCK, w), lambda m: (nxt(m), 0))]
    three_t = [pl.BlockSpec((H, BLOCK), lambda m: (0, prev(m))), pl.BlockSpec((H, BLOCK), lambda m: (0, m)),
               pl.BlockSpec((H, BLOCK), lambda m: (0, nxt(m)))]
    blk = lambda w: pl.BlockSpec((BLOCK, w), lambda m: (m, 0))
    return pl.pallas_call(
        body, name="attn_bwd_dkv", grid=(nb,),
        in_specs=[pl.BlockSpec((BLOCK, KV), lambda m: (m, A // KV)), pl.BlockSpec((BLOCK, KV), lambda m: (m, P // KV))]
                 + three(A) + three(A) + three_t + three_t + [blk(LANES), blk(LANES)],
        out_specs=[blk(KV), blk(KV)],
        out_shape=[jax.ShapeDtypeStruct((L, KV), BF16), jax.ShapeDtypeStruct((L, KV), BF16)],
        compiler_params=_cparams("parallel"),
    )(qk, uv, qk, qk, qk, dap, dap, dap, lse_t, lse_t, lse_t, rd_t, rd_t, rd_t, cos, sin)


def _halo_specs(T, L, W, col):
    per = T // HALO
    return [pl.BlockSpec((HALO, W), lambda i: (jnp.maximum(i * per - 1, 0), col)),
            pl.BlockSpec((T, W), lambda i: (i, col)),
            pl.BlockSpec((HALO, W), lambda i: (jnp.minimum((i + 1) * per, L // HALO - 1), col))]


def _fill_halo_buf(buf, prev_ref, cur_ref, next_ref, i, nt, T):
    buf[0:HALO, :] = jnp.where(i > 0, prev_ref[...], 0.0)
    buf[HALO:HALO + T, :] = cur_ref[...]
    buf[HALO + T:2 * HALO + T, :] = jnp.where(i < nt - 1, next_ref[...], 0.0)


def _counts(t, w, L):
    lo = jnp.clip(t - w // 2, 0, L)
    hi = jnp.clip(t - w // 2 + w, 0, L)
    return jnp.maximum(hi - lo, 1).astype(F32)


def _pool_fwd(u, pw, scale, mix):
    L, P = u.shape[0], scale.shape[1]
    gd = P // POOL_GROUPS
    T = _tile(L, 256, 8)
    nt = L // T
    assert (mix.shape[1] - P) % P == 0
    mix_col = mix.shape[1] // P - 1

    def body(up_ref, uc_ref, un_ref, pw_ref, sc_ref, mix_ref, out_ref, pooled_ref, buf):
        i = pl.program_id(0)
        _fill_halo_buf(buf, up_ref, uc_ref, un_ref, i, nt, T)
        t = i * T + lax.broadcasted_iota(I32, (T, 1), 0)
        for g, w in enumerate(POOL_WINDOWS):
            cols = slice(g * gd, (g + 1) * gd)
            acc = buf[pl.ds(HALO - w // 2, T), cols]
            for o in range(-w // 2 + 1, w // 2):
                acc = acc + buf[pl.ds(HALO + o, T), cols]
            pooled = (acc / _counts(t, w, L) - buf[pl.ds(HALO, T), cols]).astype(BF16)
            pooled_ref[:, cols] = pooled
            out_ref[:, cols] = (_dot(pooled, pw_ref[g]) * sc_ref[:, cols]).astype(BF16)

    return pl.pallas_call(
        body, name="pool_fwd", grid=(nt,),
        in_specs=_halo_specs(T, L, P, 0) + [pl.BlockSpec((POOL_GROUPS, gd, gd), lambda i: (0, 0, 0)), pl.BlockSpec((1, P), lambda i: (0, 0)),
                                            pl.BlockSpec(memory_space=pl.ANY)],
        out_specs=[pl.BlockSpec((T, P), lambda i: (i, mix_col)), pl.BlockSpec((T, P), lambda i: (i, 0))],
        out_shape=[jax.ShapeDtypeStruct(mix.shape, BF16), jax.ShapeDtypeStruct((L, P), BF16)],
        scratch_shapes=[pltpu.VMEM((T + 2 * HALO, P), F32)],
        input_output_aliases={5: 0},
        compiler_params=_cparams("parallel"),
    )(u, u, u, pw, scale, mix)


def _pool_bwd_mix(dap, pooled, pw, scale, pcol):
    L, P = pooled.shape
    gd = P // POOL_GROUPS
    T = _tile(L, 256, 8)

    def body(dp_ref, pooled_ref, pw_ref, sc_ref, dpooled_ref, dpw_ref, dsc_ref):
        i = pl.program_id(0)

        @pl.when(i == 0)
        def _():
            dpw_ref[...] = jnp.zeros_like(dpw_ref)
            dsc_ref[...] = jnp.zeros_like(dsc_ref)

        for g in range(POOL_GROUPS):
            cols = slice(g * gd, (g + 1) * gd)
            pb = pooled_ref[:, cols]
            dp = dp_ref[:, cols]
            dsc_ref[:, cols] += jnp.sum(dp * _dot(pb, pw_ref[g]), axis=0, keepdims=True)
            dm = (dp * sc_ref[:, cols]).astype(BF16)
            dpw_ref[g] += _dot_tn(pb, dm)
            dpooled_ref[:, cols] = _dot_nt(dm, pw_ref[g])

    return pl.pallas_call(
        body, name="pool_bwd_mix", grid=(L // T,),
        in_specs=[pl.BlockSpec((T, P), lambda i: (i, pcol)), pl.BlockSpec((T, P), lambda i: (i, 0)),
                  pl.BlockSpec((POOL_GROUPS, gd, gd), lambda i: (0, 0, 0)), pl.BlockSpec((1, P), lambda i: (0, 0))],
        out_specs=[pl.BlockSpec((T, P), lambda i: (i, 0)), pl.BlockSpec((POOL_GROUPS, gd, gd), lambda i: (0, 0, 0)),
                   pl.BlockSpec((1, P), lambda i: (0, 0))],
        out_shape=[jax.ShapeDtypeStruct((L, P), F32), jax.ShapeDtypeStruct((POOL_GROUPS, gd, gd), F32), jax.ShapeDtypeStruct((1, P), F32)],
        compiler_params=_cparams("arbitrary"),
    )(dap, pooled, pw, scale)


def _pool_bwd_window(dpooled):
    L, P = dpooled.shape
    gd = P // POOL_GROUPS
    T = _tile(L, 256, 8)
    nt = L // T

    def body(dp_ref, dc_ref, dn_ref, du_ref, buf):
        i = pl.program_id(0)
        _fill_halo_buf(buf, dp_ref, dc_ref, dn_ref, i, nt, T)
        t = i * T - HALO + lax.broadcasted_iota(I32, (T + 2 * HALO, 1), 0)
        for g, w in enumerate(POOL_WINDOWS):
            cols = slice(g * gd, (g + 1) * gd)
            buf[:, cols] = buf[:, cols] / _counts(t, w, L)
            acc = buf[pl.ds(HALO - w // 2 + 1, T), cols]
            for o in range(-w // 2 + 2, w // 2 + 1):
                acc = acc + buf[pl.ds(HALO + o, T), cols]
            du_ref[:, cols] = (acc - dc_ref[:, cols]).astype(BF16)

    return pl.pallas_call(
        body, name="pool_bwd_window", grid=(nt,),
        in_specs=_halo_specs(T, L, P, 0),
        out_specs=pl.BlockSpec((T, P), lambda i: (i, 0)),
        out_shape=jax.ShapeDtypeStruct((L, P), BF16),
        scratch_shapes=[pltpu.VMEM((T + 2 * HALO, P), F32)],
        compiler_params=_cparams("parallel"),
    )(dpooled, dpooled, dpooled)


def _sum_rows(name, a):
    R, N = a.shape

    def body(a_ref, o_ref):
        if R <= 16:
            acc = a_ref[0:1, :]
            for r in range(1, R):
                acc = acc + a_ref[r:r + 1, :]
        else:
            acc = jnp.sum(a_ref[...], axis=0, keepdims=True)
        o_ref[...] = acc

    return pl.pallas_call(body, name=name, out_shape=jax.ShapeDtypeStruct((1, N), F32))(a)


def _silu_grad_mul(cv, g):
    def body(c_ref, g_ref, o_ref):
        cvv = c_ref[...]
        s = 1.0 / (1.0 + jnp.exp(-cvv))
        o_ref[...] = g_ref[...] * (s * (1.0 + cvv * (1.0 - s)))

    return pl.pallas_call(body, name="silu_grad_mul", out_shape=jax.ShapeDtypeStruct(cv.shape, F32))(cv, g)


def _adamw(name, w, g, m, v):
    R, C = w.shape
    parts = g.ndim == 3
    n_parts = g.shape[0] if parts else 1
    T = _tile(R, max(8, 262144 // C), 8)

    def body(w_ref, g_ref, m_ref, v_ref, go_ref, d_ref, mo_ref, vo_ref):
        if parts:
            gv = g_ref[0].astype(F32)
            for p in range(1, n_parts):
                gv = gv + g_ref[p].astype(F32)
        else:
            gv = g_ref[...]
        mn = ADAM_B1 * m_ref[...] + (1.0 - ADAM_B1) * gv
        vn = ADAM_B2 * v_ref[...] + (1.0 - ADAM_B2) * (gv * gv)
        m_hat = mn / (1.0 - ADAM_B1 ** ADAM_STEP)
        v_hat = vn / (1.0 - ADAM_B2 ** ADAM_STEP)
        go_ref[...] = gv
        d_ref[...] = -ADAM_LR * (m_hat / (jnp.sqrt(v_hat) + ADAM_EPS) + ADAM_WD * w_ref[...])
        mo_ref[...] = mn
        vo_ref[...] = vn

    tile = pl.BlockSpec((T, C), lambda i: (i, 0))
    g_spec = pl.BlockSpec((n_parts, T, C), lambda i: (0, i, 0)) if parts else tile
    return pl.pallas_call(
        body, name=name, grid=(R // T,),
        in_specs=[tile, g_spec, tile, tile], out_specs=[tile] * 4,
        out_shape=[jax.ShapeDtypeStruct((R, C), F32)] * 4,
        compiler_params=_cparams("parallel"),
    )(w, g, m, v)


def _dev_index(px, py, pc):
    return 4 * px + 2 * py + pc


def _all_gather(name, arrs):
    n = len(arrs)

    def body(*refs):
        ins, outs = refs[:n], refs[n:2 * n]
        send_sems, recv_sems, local_sems = refs[2 * n:]
        x, y, c = lax.axis_index("x"), lax.axis_index("y"), lax.axis_index("c")
        me, sibling = (x, y, c), (x, y, 1 - c)
        chips = [(1 - x, y), (x, 1 - y), (1 - x, 1 - y)]

        def copy(a, k, block, to, src=None):
            slot = outs[a].at[_dev_index(*block)]
            return pltpu.make_async_remote_copy(
                src_ref=slot if src is None else src, dst_ref=slot, send_sem=send_sems.at[a, k], recv_sem=recv_sems.at[a, k],
                device_id=to, device_id_type=MESH)

        mine = [pltpu.make_async_copy(ins[a], outs[a].at[_dev_index(*me)], local_sems.at[a]) for a in range(n)]
        for cp in mine:
            cp.start()
        first = []
        for a in range(n):
            first.append(copy(a, 0, me, sibling, src=ins[a]))
            first += [copy(a, 1 + j, me, (*chip, c), src=ins[a]) for j, chip in enumerate(chips)]
        for cp in first:
            cp.start()
        passed = []
        for j, chip in enumerate(chips):
            for a in range(n):
                copy(a, 1 + j, (*chip, c), me).wait_recv()
                fwd = copy(a, 4 + j, (*chip, c), sibling)
                fwd.start()
                passed.append(fwd)
        for a in range(n):
            copy(a, 0, sibling, me).wait_recv()
            for j, chip in enumerate(chips):
                copy(a, 4 + j, (*chip, 1 - c), me).wait_recv()
        for cp in first + passed:
            cp.wait_send()
        for cp in mine:
            cp.wait()

    return pl.pallas_call(
        body, name=name,
        in_specs=[HBM] * n, out_specs=[HBM] * n,
        out_shape=[jax.ShapeDtypeStruct((N_DEV, *a.shape), a.dtype) for a in arrs],
        scratch_shapes=[pltpu.SemaphoreType.DMA((n, N_DEV - 1)), pltpu.SemaphoreType.DMA((n, N_DEV - 1)), pltpu.SemaphoreType.DMA((n,))],
    )(*arrs)


N_COPIES = {"all_to_all": N_DEV - 1, "gather_chips": 4, "forward": 3}


def _exchange_copies(kind, src_ref, land_ref, send_sems, recv_sems, sending):
    x, y, c = lax.axis_index("x"), lax.axis_index("y"), lax.axis_index("c")
    me = _dev_index(x, y, c)
    others = [(1 - x, y), (x, 1 - y), (1 - x, 1 - y)]
    if kind == "all_to_all":
        flips = [(dx, dy, dc) for dx in (0, 1) for dy in (0, 1) for dc in (0, 1)][1:]
        peers = [(1 - x if dx else x, 1 - y if dy else y, 1 - c if dc else c) for dx, dy, dc in flips]
        plan = [(p, src_ref.at[_dev_index(*p)], me if sending else _dev_index(*p)) for p in peers]
    elif kind == "gather_chips":
        peers = [(x, y, 1 - c)] + [(*o, c) for o in others]
        plan = [(p, src_ref, me if sending else _dev_index(*p)) for p in peers]
    else:
        plan = [((x, y, 1 - c), land_ref.at[_dev_index(*o, c)], _dev_index(*o, c if sending else 1 - c)) for o in others]
    return [pltpu.make_async_remote_copy(src_ref=src, dst_ref=land_ref.at[slot], send_sem=send_sems.at[k], recv_sem=recv_sems.at[k],
                                         device_id=peer, device_id_type=MESH)
            for k, (peer, src, slot) in enumerate(plan)]


def _exchange_start(name, kind, srcs, lands=None):
    if lands is None:
        lands = [lax.empty((N_DEV, *s.shape) if kind == "gather_chips" else s.shape, s.dtype) for s in srcs]
    n = len(lands)
    ops = ([] if srcs is None else list(srcs)) + list(lands)
    m = len(ops)

    def body(*refs):
        src_refs = [None] * n if srcs is None else refs[:n]
        land_refs = refs[m - n:m]
        send_sems, recv_sems, token = refs[m:m + n], refs[m + n:m + 2 * n], refs[-1]
        for a in range(n):
            for cp in _exchange_copies(kind, src_refs[a], land_refs[a], send_sems[a], recv_sems[a], True):
                cp.start()
        token[...] = jnp.zeros_like(token)

    sems = [pltpu.SemaphoreType.DMA((N_COPIES[kind],))] * (2 * n)
    outs = pl.pallas_call(
        body, name=name,
        out_shape=sems + [pltpu.HBM(o.shape, o.dtype) for o in ops] + [jax.ShapeDtypeStruct((8, LANES), F32)],
        in_specs=[HBM] * m,
        out_specs=[SEM] * (2 * n) + [HBM] * m + [pl.BlockSpec(memory_space=pltpu.VMEM)],
        input_output_aliases={i: 2 * n + i for i in range(m)},
        compiler_params=SIDE_EFFECT,
    )(*[pltpu.with_memory_space_constraint(o, pltpu.HBM) for o in ops])
    thru = outs[2 * n:2 * n + m]
    return outs[:n], outs[n:2 * n], (None if srcs is None else thru[:n]), thru[m - n:], outs[-1]


def _exchange_wait(name, kind, send_sems, recv_sems, srcs, lands, after):
    n = len(lands)
    ops = ([] if srcs is None else list(srcs)) + list(lands)
    m = len(ops)

    def body(*refs):
        src_refs = [None] * n if srcs is None else refs[:n]
        land_refs = refs[m - n:m]
        send_refs, recv_refs = refs[m:m + n], refs[m + n:m + 2 * n]
        for a in range(n):
            for cp in _exchange_copies(kind, src_refs[a], land_refs[a], send_refs[a], recv_refs[a], False):
                cp.wait_send()
                cp.wait_recv()

    outs = pl.pallas_call(
        body, name=name,
        out_shape=[pltpu.HBM(o.shape, o.dtype) for o in ops],
        in_specs=[HBM] * m + [SEM] * (2 * n) + [pl.BlockSpec(memory_space=pl.ANY)],
        out_specs=[HBM] * m,
        input_output_aliases={i: i for i in range(m)},
        compiler_params=SIDE_EFFECT,
    )(*ops, *send_sems, *recv_sems, after)
    return (None if srcs is None else outs[:n]), outs[m - n:]


def _with_own(land, own, me):
    return lax.dynamic_update_slice_in_dim(land, own, me, 0)


def _shards_to_cols(g):
    return jnp.transpose(g, (1, 0, 2)).reshape(g.shape[1], N_DEV * g.shape[2])


def _cols_to_shards(a):
    R, Ctot = a.shape
    return jnp.transpose(a.reshape(R, N_DEV, Ctot // N_DEV), (1, 0, 2))


def kernel(x, c, ctx, c_ctx, norm_attn_w, norm_mlp_w, w_ada, b_ada, w_in, attn_sink, pool_w, pool_scale, w_out, w_mlp_up, w_mlp_down, final_norm_w, loss_target, m_c_ctx, m_norm_attn_w, m_norm_mlp_w, m_w_ada, m_b_ada, m_w_in, m_attn_sink, m_pool_w, m_pool_scale, m_w_out, m_w_mlp_up, m_w_mlp_down, m_final_norm_w, v_c_ctx, v_norm_attn_w, v_norm_mlp_w, v_w_ada, v_b_ada, v_w_in, v_attn_sink, v_pool_w, v_pool_scale, v_w_out, v_w_mlp_up, v_w_mlp_down, v_final_norm_w):
    _, L, D = x.shape
    H = attn_sink.shape[1]
    A = H * HEAD_DIM
    KV = A // GQA
    P = pool_scale.shape[1]
    MODW = N_MOD * D
    ws = MODW // N_DEV
    gd = P // POOL_GROUPS
    me = _dev_index(lax.axis_index("x"), lax.axis_index("y"), lax.axis_index("c"))

    x2d, ctx2d, tgt = x[0], ctx[0], loss_target[0]
    cctx_row = c_ctx.reshape(1, D)
    wf_row = final_norm_w.reshape(1, D)
    w_ada_l = w_ada[0]
    pool_w_l = pool_w[0].reshape(POOL_GROUPS * (gd // N_DEV), gd)

    (c_all,) = _all_gather("gather_cond", [c])
    cond = jnp.concatenate([c_all[:, 0, :], cctx_row, jnp.zeros((COND_ROWS - N_DEV - 1, D), F32)], axis=0)
    b_sh = lax.dynamic_slice_in_dim(b_ada, me * ws, ws, axis=1)
    (mods_sh,) = _mm("ada_mod", cond, w_ada_l, "nn", [F32], SMALL_TILES, a_pre=_silu, extras=[("n", b_sh)], epilogue=lambda acc, b: (acc + b,))
    (mods_g,) = _all_gather("gather_mods", [mods_sh])

    w_srcs = [w_in[0].astype(BF16), w_out[0].astype(BF16), pool_w_l.astype(BF16), w_mlp_up[0].astype(BF16), w_mlp_down[0].astype(BF16)]
    w_srcs, mods_g = lax.optimization_barrier((w_srcs, mods_g))
    gw_send, gw_recv, gw_src, gw_land, _ = _exchange_start("gather_weights_start", "gather_chips", w_srcs)

    def weights(tag, lo, hi, after_chips, after_forward):
        mine, lands = _exchange_wait(f"gather_{tag}_wait", "gather_chips", gw_send[lo:hi], gw_recv[lo:hi], gw_src[lo:hi],
                                     gw_land[lo:hi], after_chips)
        f_send, f_recv, _, f_land, f_token = _exchange_start(f"forward_{tag}_start", "forward", None, lands)
        _, lands = _exchange_wait(f"forward_{tag}_wait", "forward", f_send, f_recv, None, f_land,
                                  f_token if after_forward is None else after_forward)
        return [_with_own(l, s[None], me) for l, s in zip(lands, mine)]

    mods = _shards_to_cols(mods_g)
    mod_b = lax.dynamic_slice_in_dim(mods, me, 1, axis=0)
    sh_a, sc_a, g_a, sh_m, sc_m, g_m = [mod_b[:, i * D:(i + 1) * D] for i in range(N_MOD)]
    csh_a, csc_a = mods[N_DEV:N_DEV + 1, :D], mods[N_DEV:N_DEV + 1, D:2 * D]

    cos, sin = _rope_tables(L)
    h = _norm_fwd("norm_attn", x2d, norm_attn_w, sc_a, sh_a)
    hc = _norm_fwd("norm_attn_ctx", ctx2d, norm_attn_w, csc_a, csh_a)
    (win_g,) = weights("w_in", 0, 1, h, None)
    W_in = _shards_to_cols(win_g)
    W_qk, W_kv = W_in[:, :A + KV], W_in[:, A:A + 2 * KV]
    W_uv = jnp.concatenate([W_in[:, A + 2 * KV:], W_in[:, A + KV:A + 2 * KV]], axis=1)
    (qk,) = _mm("in_proj_qk", h, W_qk, "nn", [BF16], (1024, A + KV, D), extras=[("m", cos), ("m", sin)],
                epilogue=lambda acc, cs, sn: (_rope(acc, cs, sn),))
    (uv,) = _mm("in_proj_uv", h, W_uv, "nn", [F32], (1024, P + KV, D))
    (kvc,) = _mm("in_proj_ctx", hc, W_kv, "nn", [BF16], SMALL_TILES)
    attn, lse = _attn_fwd(qk, uv, kvc, attn_sink, A, KV, P)
    wout_g, pw_g = weights("w_out", 1, 3, qk, attn)
    W_out = wout_g.reshape(A + P, D)
    PW = jnp.transpose(pw_g.reshape(N_DEV, POOL_GROUPS, gd // N_DEV, gd), (1, 0, 2, 3)).reshape(POOL_GROUPS, gd, gd)
    ap, pooled = _pool_fwd(uv, PW, pool_scale, attn)
    o, x1, hm = _mm("out_proj_norm", ap, W_out, "nn", [F32, F32, BF16], (256, D, D), chunk=128, epilogue=_out_proj_epilogue,
                    extras=[("mn", x2d), ("n", g_a), ("n", norm_mlp_w), ("n", sc_m), ("n", sh_m)])
    W_up, wdown_g = weights("w_mlp", 3, 5, attn, x1)
    W_down = wdown_g.reshape(-1, D)
    up, act = _mm("mlp_up", hm, W_up, "nn", [F32, BF16], (1024, 1024, 2048), epilogue=lambda acc: (acc, _relu2(acc)), b_shards=True)
    d_x2, d_mlp, d_wf, d_gm, loss_row = _mm(
        "mlp_down_loss", act, W_down, "nn", [F32, BF16], (512, D, 1024), chunk=128, n_sums=3, vmem=FUSED_VMEM_LIMIT,
        epilogue=_mlp_down_epilogue, extras=[("mn", x1), ("mn", tgt), ("n", g_m), ("n", wf_row)])
    loss_p = loss_row[:, :1]

    (d_up,) = _mm("mlp_down_bwd_act", d_mlp, W_down, "nt", [BF16], (1024, 1024, 2048), extras=[("mn", up)], epilogue=lambda acc, uu: (acc * (2.0 * jnp.maximum(uu, 0.0)),))
    (gW_down,) = _mm("mlp_down_bwd_w", act, d_mlp, "tn", [BF16], (1024, 2048, 1024))
    (gW_up_s,) = _mm("mlp_up_bwd_w", hm, d_up, "tn", [BF16], (2048, 1024, 1024), out_shards=True)
    g_mlp_srcs = [gW_up_s, gW_down.reshape(N_DEV, -1, D)]
    g_mlp = _exchange_start("grads_mlp_start", "all_to_all", g_mlp_srcs)
    d_x1, d_o, s_sh_m, s_sc_m, s_w_nm, d_ga = _mm(
        "mlp_up_bwd_norm", d_up, W_up, "nt", [F32, BF16], (256, D, 1024), chunk=128, n_sums=4, b_shards=True,
        epilogue=_mlp_up_bwd_epilogue,
        extras=[("mn", x1), ("mn", d_x2), ("mn", o), ("n", norm_mlp_w), ("n", sc_m + g_mlp[4][0, 0]), ("n", g_a)])

    (d_ap,) = _mm("out_proj_bwd_act", d_o, W_out, "nt", [F32], (1024, 1024, 2048))
    (gW_out,) = _mm("out_proj_bwd_w", ap, d_o, "tn", [BF16], (1024, 2048, 1024))
    d_pooled, gPW, d_pscale = _pool_bwd_mix(d_ap, pooled, PW, pool_scale, A // P)
    gpw_s = jnp.transpose(gPW.astype(BF16).reshape(POOL_GROUPS, N_DEV, gd // N_DEV, gd), (1, 0, 2, 3)).reshape(N_DEV, -1, gd)
    g_mix_srcs = [gW_out.reshape(N_DEV, (A + P) // N_DEV, D), gpw_s]
    g_mix = _exchange_start("grads_mix_start", "all_to_all", g_mix_srcs)
    lse = lse + g_mix[4][0, 0]
    d_u = _pool_bwd_window(d_pooled)
    d_q, rd, dsink_rows, d_kvc = _attn_bwd_dq(qk, uv, kvc, attn_sink, d_ap, lse, cos, sin, A, KV, P)
    d_k, d_v = _attn_bwd_dkv(qk, uv, d_ap, lse.T, rd.T, cos, sin, A, KV, P)
    d_sink = _sum_rows("sink_grad", dsink_rows)
    d_p = jnp.concatenate([d_q, d_k, d_v, d_u], axis=1)
    d_kvc_b = d_kvc.astype(BF16)
    (gW_kv_ctx,) = _mm("in_proj_ctx_bwd_w", hc, d_kvc_b, "tn", [F32], SMALL_TILES)
    (d_hc,) = _mm("in_proj_ctx_bwd_act", d_kvc_b, W_kv, "nt", [F32], SMALL_TILES)
    gW_in_init = jnp.pad(gW_kv_ctx, ((0, 0), (A, P)))
    (gW_in,) = _mm("in_proj_bwd_w", h, d_p, "tn", [BF16], (1024, 1280, 1024), extras=[("mn", gW_in_init)], epilogue=lambda acc, init: (acc + init,))
    g_in_srcs = [_cols_to_shards(gW_in)]
    g_in = _exchange_start("grads_in_start", "all_to_all", g_in_srcs)
    grad_x, s_sh_a, s_sc_a, s_w_na = _mm(
        "in_proj_bwd_norm", d_p, W_in, "nt", [F32], (256, D, A + 2 * KV + P), chunk=128, n_sums=3,
        epilogue=lambda acc, xr, dres, w, sc: _modulated_norm_bwd(xr, acc, dres, w, sc),
        extras=[("mn", x2d), ("mn", d_x1), ("n", norm_attn_w), ("n", sc_a + g_in[4][0, 0])])
    _, s_csh, s_csc, s_w_na = _norm_bwd("norm_attn_ctx_bwd", ctx2d, d_hc, jnp.zeros_like(ctx2d), norm_attn_w, csc_a, s_w_na)

    pad_l = lambda a: jnp.pad(a, ((0, 0), (0, LANES - a.shape[1])))
    d_mod_b = jnp.concatenate([s_sh_a, s_sc_a, d_ga, s_sh_m, s_sc_m, d_gm], axis=1)
    summed = jnp.concatenate([s_csh, s_csc, s_w_na, s_w_nm, d_wf, d_pscale, pad_l(d_sink), pad_l(loss_p)], axis=1)
    (small_g,) = _all_gather("gather_small", [jnp.concatenate([d_mod_b, summed], axis=1)])
    small_g = small_g[:, 0, :]
    tot = _sum_rows("small_sum", small_g[:, MODW:])
    off = [0]
    for wdt in (D, D, D, D, D, P, LANES, LANES):
        off.append(off[-1] + wdt)
    seg = lambda i: tot[:, off[i]:off[i + 1]]
    g_norm_attn, g_norm_mlp, g_final, g_pscale = seg(2), seg(3), seg(4), seg(5)
    g_sink, loss = seg(6)[:, :H], seg(7)[0, 0]
    d_mod_ctx = jnp.concatenate([seg(0), seg(1), jnp.zeros((1, MODW - 2 * D), F32)], axis=1)
    d_mod = jnp.concatenate([small_g[:, :MODW], d_mod_ctx, jnp.zeros((COND_ROWS - N_DEV - 1, MODW), F32)], axis=0)
    g_b_ada = _sum_rows("b_ada_grad", d_mod[:N_DEV + 1])
    d_mod_sh = lax.dynamic_slice_in_dim(d_mod, me * ws, ws, axis=1)
    (g_w_ada,) = _mm("ada_bwd_w", cond, d_mod_sh, "tn", [F32], SMALL_TILES, a_pre=_silu)
    (d_cond_p,) = _mm("ada_bwd_cond", d_mod_sh, w_ada_l, "nt", [F32], SMALL_TILES)
    (d_cctx_g,) = _all_gather("gather_cctx", [d_cond_p[N_DEV:N_DEV + 1]])
    g_c_ctx = _silu_grad_mul(cctx_row, _sum_rows("cctx_sum", d_cctx_g[:, 0, :]))

    def arrived(name, started):
        srcs, lands = _exchange_wait(name, "all_to_all", started[0], started[1], started[2], started[3], g_c_ctx)
        return [_with_own(l, lax.dynamic_index_in_dim(s, me, 0, keepdims=True), me) for l, s in zip(lands, srcs)]

    r_up, r_down = arrived("grads_mlp_wait", g_mlp)
    r_out, r_pw = arrived("grads_mix_wait", g_mix)
    (r_in,) = arrived("grads_in_wait", g_in)

    results = {
        "c_ctx": _adamw("adam_c_ctx", cctx_row, g_c_ctx, m_c_ctx.reshape(1, D), v_c_ctx.reshape(1, D)),
        "norm_attn_w": _adamw("adam_norm_attn", norm_attn_w, g_norm_attn, m_norm_attn_w, v_norm_attn_w),
        "norm_mlp_w": _adamw("adam_norm_mlp", norm_mlp_w, g_norm_mlp, m_norm_mlp_w, v_norm_mlp_w),
        "w_ada": _adamw("adam_w_ada", w_ada_l, g_w_ada, m_w_ada[0], v_w_ada[0]),
        "b_ada": _adamw("adam_b_ada", b_ada, g_b_ada, m_b_ada, v_b_ada),
        "w_in": _adamw("adam_w_in", w_in[0], r_in, m_w_in[0], v_w_in[0]),
        "attn_sink": _adamw("adam_sink", attn_sink, g_sink, m_attn_sink, v_attn_sink),
        "pool_w": _adamw("adam_pool_w", pool_w_l, r_pw, m_pool_w[0].reshape(pool_w_l.shape), v_pool_w[0].reshape(pool_w_l.shape)),
        "pool_scale": _adamw("adam_pool_scale", pool_scale, g_pscale, m_pool_scale, v_pool_scale),
        "w_out": _adamw("adam_w_out", w_out[0], r_out, m_w_out[0], v_w_out[0]),
        "w_mlp_up": _adamw("adam_w_up", w_mlp_up[0], r_up, m_w_mlp_up[0], v_w_mlp_up[0]),
        "w_mlp_down": _adamw("adam_w_down", w_mlp_down[0], r_down, m_w_mlp_down[0], v_w_mlp_down[0]),
        "final_norm_w": _adamw("adam_final_norm", wf_row, g_final, m_final_norm_w.reshape(1, D), v_final_norm_w.reshape(1, D)),
    }
    shapes = {"c_ctx": c_ctx.shape, "norm_attn_w": norm_attn_w.shape, "norm_mlp_w": norm_mlp_w.shape, "w_ada": w_ada.shape,
              "b_ada": b_ada.shape, "w_in": w_in.shape, "attn_sink": attn_sink.shape, "pool_w": pool_w.shape,
              "pool_scale": pool_scale.shape, "w_out": w_out.shape, "w_mlp_up": w_mlp_up.shape, "w_mlp_down": w_mlp_down.shape,
              "final_norm_w": final_norm_w.shape}
    outs = [loss, grad_x.reshape(x.shape)]
    for part in range(4):
        outs += [results[name][part].reshape(shape) for name, shape in shapes.items()]
    return tuple(outs)
```

```python
import functools

import jax
import jax.numpy as jnp
import numpy as np
from jax import lax
from jax.experimental import pallas as pl
from jax.experimental.pallas import tpu as pltpu

F32 = jnp.float32
BF16 = jnp.bfloat16
I32 = jnp.int32

HEAD_DIM = 64
GQA = 4
BLOCK = 128
GRID_W = 64
ROPE_BASE = 10000.0
POOL_WINDOWS = (2, 4, 8, 16)
POOL_GROUPS = len(POOL_WINDOWS)
HALO = 8
N_MOD = 6
EPS = 1e-6
NEG_INF = -1e30
ADAM_LR = 0.001
ADAM_B1 = 0.9
ADAM_B2 = 0.999
ADAM_EPS = 1e-08
ADAM_WD = 0.01
ADAM_STEP = 10
N_DEV = 8
COND_ROWS = 2 * N_DEV
LANES = 128
SUBLANES_16BIT = 16
VMEM_LIMIT = 48 * 1024 * 1024
FUSED_VMEM_LIMIT = 56 * 1024 * 1024
SMALL_TILES = (512, 1024, 512)
MESH = pl.DeviceIdType.MESH
HBM = pl.BlockSpec(memory_space=pltpu.HBM)
SEM = pl.BlockSpec(memory_space=pltpu.SEMAPHORE)
SIDE_EFFECT = pltpu.CompilerParams(has_side_effects=pltpu.SideEffectType.DATAFLOW_SIDE_EFFECTING)


def _cparams(*sem):
    return pltpu.CompilerParams(dimension_semantics=sem, vmem_limit_bytes=VMEM_LIMIT)


def _tile(n, pref, align):
    if n <= pref:
        return n
    t = (pref // align) * align
    while t >= align:
        if n % t == 0:
            return t
        t -= align
    return n


def _dot(a, b):
    return lax.dot_general(a, b, (((1,), (0,)), ((), ())), preferred_element_type=F32)


def _dot_nt(a, b):
    return lax.dot_general(a, b, (((1,), (1,)), ((), ())), preferred_element_type=F32)


def _dot_tn(a, b):
    return lax.dot_general(a, b, (((0,), (0,)), ((), ())), preferred_element_type=F32)


_DOTS = {"nn": _dot, "nt": _dot_nt, "tn": _dot_tn}


def _mm(name, a, b, mode, out_dtypes, tiles, *, epilogue=None, extras=(), a_pre=None, n_sums=0, chunk=None,
        b_shards=False, out_shards=False, vmem=VMEM_LIMIT):
    if mode == "nn":
        M, K = a.shape
        K2, N = (b.shape[1], N_DEV * b.shape[2]) if b_shards else b.shape
    elif mode == "nt":
        M, K = a.shape
        N, K2 = (b.shape[1], N_DEV * b.shape[2]) if b_shards else b.shape
    else:
        (K, M), (K2, N) = a.shape, b.shape
    assert K == K2 and not (b_shards and mode == "tn"), (name, a.shape, b.shape)
    n_span = N // N_DEV if out_shards or (b_shards and mode == "nn") else N
    k_span = K // N_DEV if b_shards and mode == "nt" else K
    tm = _tile(M, tiles[0], LANES if mode == "tn" else SUBLANES_16BIT)
    tn = _tile(n_span, tiles[1], LANES)
    tk = _tile(k_span, tiles[2], SUBLANES_16BIT if mode == "tn" else LANES)
    nk, nb, kb = K // tk, n_span // tn, k_span // tk
    rows = tm if chunk is None else min(chunk, tm)
    n_ex, n_out = len(extras), len(out_dtypes)
    use_acc = nk > 1 or rows < tm
    assert n_sums == 0 or N == tn, name

    def product(a_ref, b_ref):
        at = a_ref[...]
        if a_pre is not None:
            at = a_pre(at)
        return _DOTS[mode](at.astype(BF16), b_ref[...].astype(BF16))

    def apply(acc, ex, out_refs, sl):
        res = (acc,) if epilogue is None else epilogue(acc, *ex)
        for o_ref, o in zip(out_refs, res[:n_out]):
            o_ref[sl, :] = o.astype(o_ref.dtype)
        return tuple(res[n_out:])

    def finish(acc, ex_refs, out_refs, sum_refs):
        if rows == tm:
            acc = acc if not use_acc else acc[...]
            sums = apply(acc, [r[...] for r in ex_refs], out_refs, slice(None))
        else:
            def one(ci, sums):
                sl = pl.ds(pl.multiple_of(ci * rows, rows), rows)
                ex = [r[...] if kind == "n" else r[sl, :] for (kind, _), r in zip(extras, ex_refs)]
                return tuple(s + v for s, v in zip(sums, apply(acc[sl, :], ex, out_refs, sl)))
            sums = lax.fori_loop(0, tm // rows, one, tuple(jnp.zeros((1, tn), F32) for _ in range(n_sums)))
        first = pl.program_id(0) == 0
        for s_ref, sv in zip(sum_refs, sums):
            @pl.when(first)
            def _(s_ref=s_ref, sv=sv):
                s_ref[...] = sv

            @pl.when(jnp.logical_not(first))
            def _(s_ref=s_ref, sv=sv):
                s_ref[...] += sv

    def body(a_ref, b_ref, *rest):
        ex_refs, out_refs = rest[:n_ex], rest[n_ex:n_ex + n_out]
        sum_refs = rest[n_ex + n_out:n_ex + n_out + n_sums]
        if not use_acc:
            finish(product(a_ref, b_ref), ex_refs, out_refs, sum_refs)
            return
        acc_ref = rest[-1]
        k = pl.program_id(2)

        @pl.when(k == 0)
        def _():
            acc_ref[...] = product(a_ref, b_ref)

        @pl.when(k > 0)
        def _():
            acc_ref[...] += product(a_ref, b_ref)

        @pl.when(k == nk - 1)
        def _():
            finish(acc_ref, ex_refs, out_refs, sum_refs)

    a_spec = pl.BlockSpec((tk, tm), lambda i, j, k: (k, i)) if mode == "tn" else pl.BlockSpec((tm, tk), lambda i, j, k: (i, k))
    if not b_shards:
        b_spec = pl.BlockSpec((tn, tk), lambda i, j, k: (j, k)) if mode == "nt" else pl.BlockSpec((tk, tn), lambda i, j, k: (k, j))
    elif mode == "nn":
        b_spec = pl.BlockSpec((None, tk, tn), lambda i, j, k: (j // nb, k, j % nb))
    else:
        b_spec = pl.BlockSpec((None, tn, tk), lambda i, j, k: (k // kb, j, k % kb))
    ex_specs = []
    for kind, arr in extras:
        if kind == "mn":
            ex_specs.append(pl.BlockSpec((tm, tn), lambda i, j, k: (i, j)))
        elif kind == "n":
            ex_specs.append(pl.BlockSpec((1, tn), lambda i, j, k: (0, j)))
        else:
            ex_specs.append(pl.BlockSpec((tm, arr.shape[1]), lambda i, j, k: (i, 0)))
    if out_shards:
        out_specs = [pl.BlockSpec((None, tm, tn), lambda i, j, k: (j // nb, i, j % nb)) for _ in out_dtypes]
        out_shape = [jax.ShapeDtypeStruct((N_DEV, M, n_span), d) for d in out_dtypes]
    else:
        out_specs = [pl.BlockSpec((tm, tn), lambda i, j, k: (i, j)) for _ in out_dtypes]
        out_shape = [jax.ShapeDtypeStruct((M, N), d) for d in out_dtypes]
    out_specs += [pl.BlockSpec((1, tn), lambda i, j, k: (0, 0))] * n_sums
    out_shape += [jax.ShapeDtypeStruct((1, N), F32)] * n_sums
    return pl.pallas_call(
        body,
        name=name,
        grid=(M // tm, N // tn, nk),
        in_specs=[a_spec, b_spec] + ex_specs,
        out_specs=out_specs,
        out_shape=out_shape,
        scratch_shapes=[pltpu.VMEM((tm, tn), F32)] if use_acc else [],
        compiler_params=pltpu.CompilerParams(
            dimension_semantics=("arbitrary",) * 3 if n_sums else ("parallel", "parallel", "arbitrary"), vmem_limit_bytes=vmem),
    )(a, b, *[arr for _, arr in extras])


def _silu(v):
    return v / (1.0 + jnp.exp(-v))


def _relu2(v):
    r = jnp.maximum(v, 0.0)
    return r * r


def _rope_tables(L):
    half = HEAD_DIM // 2
    inv_freq = np.float32(ROPE_BASE) ** (-np.arange(0, half, 2, dtype=np.float32) / np.float32(half))
    t = np.arange(L)
    row, col = t // GRID_W, t % GRID_W
    ang_r = row.astype(np.float32)[:, None] * inv_freq[None, :]
    ang_c = col.astype(np.float32)[:, None] * inv_freq[None, :]
    cos = np.concatenate([np.cos(ang_r), np.cos(ang_r), np.cos(ang_c), np.cos(ang_c)], axis=1)
    sin = np.concatenate([-np.sin(ang_r), np.sin(ang_r), -np.sin(ang_c), np.sin(ang_c)], axis=1)
    reps = LANES // HEAD_DIM
    return jnp.asarray(np.tile(cos, (1, reps)), F32), jnp.asarray(np.tile(sin, (1, reps)), F32)


def _rope(xf, cos, sin):
    quarter = HEAD_DIM // 4
    lane = lax.broadcasted_iota(I32, (xf.shape[0], LANES), 1)
    first = (lane & quarter) == 0
    outs = []
    for j in range(xf.shape[1] // LANES):
        xc = xf[:, j * LANES:(j + 1) * LANES]
        partner = jnp.where(first, pltpu.roll(xc, LANES - quarter, 1), pltpu.roll(xc, quarter, 1))
        outs.append(xc * cos + partner * sin)
    return outs[0] if len(outs) == 1 else jnp.concatenate(outs, axis=1)


def _inv_rms(xf):
    return lax.rsqrt(jnp.mean(xf * xf, axis=-1, keepdims=True) + EPS)


def _modulated_norm(xf, w, sc, sh):
    return ((xf * _inv_rms(xf)) * w) * (1.0 + sc) + sh


def _modulated_norm_bwd(xf, dh, dres, w, sc):
    r = _inv_rms(xf)
    xh = xf * r
    dn = dh * (1.0 + sc)
    dxh = dn * w
    dx = dres + r * (dxh - xh * jnp.mean(dxh * xh, axis=-1, keepdims=True))
    col = lambda v: jnp.sum(v, axis=0, keepdims=True)
    return dx, col(dh), col(dh * (xh * w)), col(dn * xh)


def _norm_fwd(name, x, w, sc, sh):
    L, D = x.shape
    T = _tile(L, 256, 8)

    def body(x_ref, w_ref, sc_ref, sh_ref, h_ref):
        h_ref[...] = _modulated_norm(x_ref[...], w_ref[...], sc_ref[...], sh_ref[...]).astype(BF16)

    row = pl.BlockSpec((1, D), lambda i: (0, 0))
    return pl.pallas_call(
        body, name=name, grid=(L // T,),
        in_specs=[pl.BlockSpec((T, D), lambda i: (i, 0)), row, row, row],
        out_specs=pl.BlockSpec((T, D), lambda i: (i, 0)),
        out_shape=jax.ShapeDtypeStruct((L, D), BF16),
        compiler_params=_cparams("parallel"),
    )(x, w, sc, sh)


def _norm_bwd(name, x, dh, dres, w, sc, w_init, gate=None):
    L, D = x.shape
    T = _tile(L, 256, 8)
    with_gate = gate is not None

    def body(x_ref, dh_ref, dres_ref, w_ref, sc_ref, wi_ref, *rest):
        if with_gate:
            o_ref, g_ref, dx_ref, ssh_ref, ssc_ref, sw_ref, sg_ref, do_ref = rest
        else:
            dx_ref, ssh_ref, ssc_ref, sw_ref = rest
        i = pl.program_id(0)

        @pl.when(i == 0)
        def _():
            ssh_ref[...] = jnp.zeros_like(ssh_ref)
            ssc_ref[...] = jnp.zeros_like(ssc_ref)
            sw_ref[...] = wi_ref[...]
            if with_gate:
                sg_ref[...] = jnp.zeros_like(sg_ref)

        dx, s_sh, s_sc, s_w = _modulated_norm_bwd(x_ref[...], dh_ref[...], dres_ref[...], w_ref[...], sc_ref[...])
        ssh_ref[...] += s_sh
        ssc_ref[...] += s_sc
        sw_ref[...] += s_w
        dx_ref[...] = dx
        if with_gate:
            sg_ref[...] += jnp.sum(dx * o_ref[...], axis=0, keepdims=True)
            do_ref[...] = (g_ref[...] * dx).astype(BF16)

    tile = pl.BlockSpec((T, D), lambda i: (i, 0))
    row = pl.BlockSpec((1, D), lambda i: (0, 0))
    in_specs = [tile, tile, tile, row, row, row]
    out_specs = [tile, row, row, row]
    out_shape = [jax.ShapeDtypeStruct((L, D), F32)] + [jax.ShapeDtypeStruct((1, D), F32)] * 3
    args = [x, dh, dres, w, sc, w_init]
    if with_gate:
        in_specs += [tile, row]
        out_specs += [row, tile]
        out_shape += [jax.ShapeDtypeStruct((1, D), F32), jax.ShapeDtypeStruct((L, D), BF16)]
        args += list(gate)
    return pl.pallas_call(
        body, name=name, grid=(L // T,), in_specs=in_specs, out_specs=out_specs, out_shape=out_shape,
        compiler_params=_cparams("arbitrary"),
    )(*args)


def _out_proj_epilogue(acc, xr, g, w, sc, sh):
    x1 = xr + g * acc
    return acc, x1, _modulated_norm(x1, w, sc, sh)


def _mlp_down_epilogue(acc, x1, tgt, g, wf):
    D = acc.shape[1]
    x2 = x1 + g * acc
    r = _inv_rms(x2)
    xh = x2 * r
    err = xh * wf - tgt
    loss = 0.5 * jnp.sum(jnp.mean(err * err, axis=-1, keepdims=True), axis=0, keepdims=True)
    dy = err / D
    dxh = dy * wf
    dx = r * (dxh - xh * jnp.mean(dxh * xh, axis=-1, keepdims=True))
    col = lambda v: jnp.sum(v, axis=0, keepdims=True)
    return dx, g * dx, col(dy * xh), col(dx * acc), jnp.broadcast_to(loss, (1, D))


def _heads(ref, first, n):
    return jnp.concatenate([ref[:, (first + g) * HEAD_DIM:(first + g + 1) * HEAD_DIM] for g in range(n)], axis=0)


def _attn_mask(n, L, C):
    shape = (3 * BLOCK + C, GQA * BLOCK)
    kj = lax.broadcasted_iota(I32, shape, 0)
    qi = lax.broadcasted_iota(I32, shape, 1) & (BLOCK - 1)
    kpos = n * BLOCK - BLOCK + kj
    window = (kj >= qi) & (kj <= qi + 2 * BLOCK) & (kpos >= 0) & (kpos < L)
    return window | (kj >= 3 * BLOCK)


def _head_rows(ref, hk):
    return jnp.concatenate([ref[hk * GQA + g:hk * GQA + g + 1, :] for g in range(GQA)], axis=1)


def _rows_to_heads(rows_by_kv_head):
    return jnp.concatenate([r[:, g * BLOCK:(g + 1) * BLOCK] for r in rows_by_kv_head for g in range(GQA)], axis=0)


def _queries_to_rows(t):
    return jnp.concatenate([t[:, g * BLOCK:(g + 1) * BLOCK].T for g in range(GQA)], axis=1)


def _attn_specs(L, A, KV, C, vcol):
    nb = L // BLOCK
    kcol = A // KV
    prev = lambda n: jnp.maximum(n - 1, 0)
    nxt = lambda n: jnp.minimum(n + 1, nb - 1)
    q_spec = pl.BlockSpec((BLOCK, A), lambda n: (n, 0))
    k_specs = [pl.BlockSpec((BLOCK, KV), lambda n: (prev(n), kcol)), pl.BlockSpec((BLOCK, KV), lambda n: (n, kcol)),
               pl.BlockSpec((BLOCK, KV), lambda n: (nxt(n), kcol))]
    v_specs = [pl.BlockSpec((BLOCK, KV), lambda n: (prev(n), vcol)), pl.BlockSpec((BLOCK, KV), lambda n: (n, vcol)),
               pl.BlockSpec((BLOCK, KV), lambda n: (nxt(n), vcol))]
    kvc_spec = pl.BlockSpec((C, 2 * KV), lambda n: (0, 0))
    return q_spec, k_specs, v_specs, kvc_spec


def _keys_values(hk, k_refs, v_refs, kvc_ref, KV):
    sl = slice(hk * HEAD_DIM, (hk + 1) * HEAD_DIM)
    keys = jnp.concatenate([r[:, sl] for r in k_refs] + [kvc_ref[:, sl]], axis=0)
    vals = jnp.concatenate([r[:, sl].astype(BF16) for r in v_refs] + [kvc_ref[:, KV + hk * HEAD_DIM:KV + (hk + 1) * HEAD_DIM]], axis=0)
    return keys, vals


def _sink_row(sink_ref, hk):
    return jnp.concatenate([jnp.full((1, BLOCK), sink_ref[0, hk * GQA + g], F32) for g in range(GQA)], axis=1)


def _attn_fwd(qk, uv, kvc, sink, A, KV, P):
    L = qk.shape[0]
    C = kvc.shape[0]
    nkv = KV // HEAD_DIM
    H = nkv * GQA
    scale = HEAD_DIM ** -0.5

    def body(sink_ref, q_ref, kp_ref, kc_ref, kn_ref, vp_ref, vc_ref, vn_ref, kvc_ref, o_ref, lse_ref):
        valid = _attn_mask(pl.program_id(0), L, C)
        lse_rows = []
        for hk in range(nkv):
            keys, vals = _keys_values(hk, (kp_ref, kc_ref, kn_ref), (vp_ref, vc_ref, vn_ref), kvc_ref, KV)
            qs = _heads(q_ref, hk * GQA, GQA) * scale
            s = jnp.where(valid, _dot_nt(keys, qs), NEG_INF)
            sk = _sink_row(sink_ref, hk)
            m = jnp.maximum(jnp.max(s, axis=0, keepdims=True), sk)
            p = jnp.exp(s - m)
            den = jnp.sum(p, axis=0, keepdims=True) + jnp.exp(sk - m)
            o = _dot_tn(vals, p.astype(BF16)) * (1.0 / den)
            lse_rows.append(m + jnp.log(den))
            o_ref[:, hk * GQA * HEAD_DIM:(hk + 1) * GQA * HEAD_DIM] = _queries_to_rows(o).astype(BF16)
        lse_ref[...] = _rows_to_heads(lse_rows)

    q_spec, k_specs, v_specs, kvc_spec = _attn_specs(L, A, KV, C, P // KV)
    return pl.pallas_call(
        body, name="attn_fwd", grid=(L // BLOCK,),
        in_specs=[pl.BlockSpec(memory_space=pltpu.SMEM), q_spec] + k_specs + v_specs + [kvc_spec],
        out_specs=[pl.BlockSpec((BLOCK, A), lambda n: (n, 0)), pl.BlockSpec((H, BLOCK), lambda n: (0, n))],
        out_shape=[jax.ShapeDtypeStruct((L, A + P), BF16), jax.ShapeDtypeStruct((H, L), F32)],
        compiler_params=_cparams("parallel"),
    )(sink, qk, qk, qk, qk, uv, uv, uv, kvc)


def _attn_bwd_dq(qk, uv, kvc, sink, dap, lse_t, cos, sin, A, KV, P):
    L = qk.shape[0]
    C = kvc.shape[0]
    nkv = KV // HEAD_DIM
    H = nkv * GQA
    scale = HEAD_DIM ** -0.5
    W = 3 * BLOCK

    def body(sink_ref, q_ref, kp_ref, kc_ref, kn_ref, vp_ref, vc_ref, vn_ref, kvc_ref, do_ref, lse_ref, cos_ref, sin_ref,
             dq_ref, rd_ref, ds_ref, dkvc_ref):
        n = pl.program_id(0)

        @pl.when(n == 0)
        def _():
            dkvc_ref[...] = jnp.zeros_like(dkvc_ref)

        valid = _attn_mask(n, L, C)
        rd_rows, dsink_rows, dq_parts = [], [], []
        for hk in range(nkv):
            sl = slice(hk * HEAD_DIM, (hk + 1) * HEAD_DIM)
            keys, vals = _keys_values(hk, (kp_ref, kc_ref, kn_ref), (vp_ref, vc_ref, vn_ref), kvc_ref, KV)
            qs = _heads(q_ref, hk * GQA, GQA) * scale
            dos = _heads(do_ref, hk * GQA, GQA).astype(BF16)
            lse = _head_rows(lse_ref, hk)
            p = jnp.exp(jnp.where(valid, _dot_nt(keys, qs), NEG_INF) - lse)
            dp = _dot_nt(vals, dos)
            rd = jnp.sum(p * dp, axis=0, keepdims=True)
            ds = (p * (dp - rd)).astype(BF16)
            dq_parts.append(_queries_to_rows(_dot_tn(keys, ds) * scale))
            dkvc_ref[:, sl] += _dot(ds[W:, :], qs)
            dkvc_ref[:, KV + hk * HEAD_DIM:KV + (hk + 1) * HEAD_DIM] += _dot(p[W:, :].astype(BF16), dos)
            rd_rows.append(rd)
            dsink_rows.append(-(jnp.exp(_sink_row(sink_ref, hk) - lse) * rd))
        rd_ref[...] = _rows_to_heads(rd_rows)
        ds_ref[...] = _rows_to_heads(dsink_rows)
        dq = dq_parts[0] if nkv == 1 else jnp.concatenate(dq_parts, axis=1)
        dq_ref[...] = _rope(dq, cos_ref[...], -sin_ref[...]).astype(BF16)

    q_spec, k_specs, v_specs, kvc_spec = _attn_specs(L, A, KV, C, P // KV)
    blk = lambda w: pl.BlockSpec((BLOCK, w), lambda n: (n, 0))
    per_head = pl.BlockSpec((H, BLOCK), lambda n: (0, n))
    return pl.pallas_call(
        body, name="attn_bwd_dq", grid=(L // BLOCK,),
        in_specs=[pl.BlockSpec(memory_space=pltpu.SMEM), q_spec] + k_specs + v_specs + [kvc_spec, blk(A), per_head, blk(LANES), blk(LANES)],
        out_specs=[blk(A), per_head, per_head, pl.BlockSpec((C, 2 * KV), lambda n: (0, 0))],
        out_shape=[jax.ShapeDtypeStruct((L, A), BF16), jax.ShapeDtypeStruct((H, L), F32), jax.ShapeDtypeStruct((H, L), F32),
                   jax.ShapeDtypeStruct((C, 2 * KV), F32)],
        compiler_params=_cparams("arbitrary"),
    )(sink, qk, qk, qk, qk, uv, uv, uv, kvc, dap, lse_t, cos, sin)


def _attn_bwd_dkv(qk, uv, dap, lse_t, rd_t, cos, sin, A, KV, P):
    L = qk.shape[0]
    nb = L // BLOCK
    nkv = KV // HEAD_DIM
    H = nkv * GQA
    scale = HEAD_DIM ** -0.5
    R = 3 * GQA * BLOCK

    def body(k_ref, v_ref, qp_ref, qc_ref, qn_ref, dop_ref, doc_ref, don_ref, lsep_ref, lsec_ref, lsen_ref,
             rdp_ref, rdc_ref, rdn_ref, cos_ref, sin_ref, dk_ref, dv_ref):
        m = pl.program_id(0)
        kj = lax.broadcasted_iota(I32, (BLOCK, R), 0)
        col = lax.broadcasted_iota(I32, (BLOCK, R), 1)
        part = col // (GQA * BLOCK)
        qi = col & (BLOCK - 1)
        before = jnp.where(m >= 1, 0, -2 * BLOCK)
        after = jnp.where(m <= nb - 2, 0, 2 * BLOCK)
        valid = ((part == 0) & (kj <= qi + before)) | (part == 1) | ((part == 2) & (kj >= qi + after))
        dk_parts, dv_parts = [], []
        for hk in range(nkv):
            sl = slice(hk * HEAD_DIM, (hk + 1) * HEAD_DIM)
            km = k_ref[:, sl]
            vm = v_ref[:, sl].astype(BF16)
            qs = jnp.concatenate([_heads(q, hk * GQA, GQA) for q in (qp_ref, qc_ref, qn_ref)], axis=0) * scale
            dos = jnp.concatenate([_heads(d, hk * GQA, GQA) for d in (dop_ref, doc_ref, don_ref)], axis=0).astype(BF16)
            rows = [slice(hk * GQA + g, hk * GQA + g + 1) for g in range(GQA)]
            lse = jnp.concatenate([t[r, :] for t in (lsep_ref, lsec_ref, lsen_ref) for r in rows], axis=1)
            rdv = jnp.concatenate([t[r, :] for t in (rdp_ref, rdc_ref, rdn_ref) for r in rows], axis=1)
            p = jnp.exp(jnp.where(valid, _dot_nt(km, qs), NEG_INF) - lse)
            ds = (p * (_dot_nt(vm, dos) - rdv)).astype(BF16)
            dk_parts.append(_dot(ds, qs))
            dv_parts.append(_dot(p.astype(BF16), dos))
        dk = dk_parts[0] if nkv == 1 else jnp.concatenate(dk_parts, axis=1)
        dv = dv_parts[0] if nkv == 1 else jnp.concatenate(dv_parts, axis=1)
        dk_ref[...] = _rope(dk, cos_ref[...], -sin_ref[...]).astype(BF16)
        dv_ref[...] = dv.astype(BF16)

    prev = lambda m: jnp.maximum(m - 1, 0)
    nxt = lambda m: jnp.minimum(m + 1, nb - 1)
    three = lambda w: [pl.BlockSpec((BLOCK, w), lambda m: (prev(m), 0)), pl.BlockSpec((BLOCK, w), lambda m: (m, 0)),
                       pl.BlockSpec((BLOCK, w), lambda m: (nxt(m), 0))]
    three_t = [pl.BlockSpec((H, BLOCK), lambda m: (0, prev(m))), pl.BlockSpec((H, BLOCK), lambda m: (0, m)),
               pl.BlockSpec((H, BLOCK), lambda m: (0, nxt(m)))]
    blk = lambda w: pl.BlockSpec((BLOCK, w), lambda m: (m, 0))
    return pl.pallas_call(
        body, name="attn_bwd_dkv", grid=(nb,),
        in_specs=[pl.BlockSpec((BLOCK, KV), lambda m: (m, A // KV)), pl.BlockSpec((BLOCK, KV), lambda m: (m, P // KV))]
                 + three(A) + three(A) + three_t + three_t + [blk(LANES), blk(LANES)],
        out_specs=[blk(KV), blk(KV)],
        out_shape=[jax.ShapeDtypeStruct((L, KV), BF16), jax.ShapeDtypeStruct((L, KV), BF16)],
        compiler_params=_cparams("parallel"),
    )(qk, uv, qk, qk, qk, dap, dap, dap, lse_t, lse_t, lse_t, rd_t, rd_t, rd_t, cos, sin)


def _halo_specs(T, L, W, col):
    per = T // HALO
    return [pl.BlockSpec((HALO, W), lambda i: (jnp.maximum(i * per - 1, 0), col)),
            pl.BlockSpec((T, W), lambda i: (i, col)),
            pl.BlockSpec((HALO, W), lambda i: (jnp.minimum((i + 1) * per, L // HALO - 1), col))]


def _fill_halo_buf(buf, prev_ref, cur_ref, next_ref, i, nt, T):
    buf[0:HALO, :] = jnp.where(i > 0, prev_ref[...], 0.0)
    buf[HALO:HALO + T, :] = cur_ref[...]
    buf[HALO + T:2 * HALO + T, :] = jnp.where(i < nt - 1, next_ref[...], 0.0)


def _counts(t, w, L):
    lo = jnp.clip(t - w // 2, 0, L)
    hi = jnp.clip(t - w // 2 + w, 0, L)
    return jnp.maximum(hi - lo, 1).astype(F32)


def _pool_fwd(u, pw, scale, mix):
    L, P = u.shape[0], scale.shape[1]
    gd = P // POOL_GROUPS
    T = _tile(L, 256, 8)
    nt = L // T
    assert (mix.shape[1] - P) % P == 0
    mix_col = mix.shape[1] // P - 1

    def body(up_ref, uc_ref, un_ref, pw_ref, sc_ref, mix_ref, out_ref, pooled_ref, buf):
        i = pl.program_id(0)
        _fill_halo_buf(buf, up_ref, uc_ref, un_ref, i, nt, T)
        t = i * T + lax.broadcasted_iota(I32, (T, 1), 0)
        for g, w in enumerate(POOL_WINDOWS):
            cols = slice(g * gd, (g + 1) * gd)
            acc = buf[pl.ds(HALO - w // 2, T), cols]
            for o in range(-w // 2 + 1, w // 2):
                acc = acc + buf[pl.ds(HALO + o, T), cols]
            pooled = (acc / _counts(t, w, L) - buf[pl.ds(HALO, T), cols]).astype(BF16)
            pooled_ref[:, cols] = pooled
            out_ref[:, cols] = (_dot(pooled, pw_ref[g]) * sc_ref[:, cols]).astype(BF16)

    return pl.pallas_call(
        body, name="pool_fwd", grid=(nt,),
        in_specs=_halo_specs(T, L, P, 0) + [pl.BlockSpec((POOL_GROUPS, gd, gd), lambda i: (0, 0, 0)), pl.BlockSpec((1, P), lambda i: (0, 0)),
                                            pl.BlockSpec(memory_space=pl.ANY)],
        out_specs=[pl.BlockSpec((T, P), lambda i: (i, mix_col)), pl.BlockSpec((T, P), lambda i: (i, 0))],
        out_shape=[jax.ShapeDtypeStruct(mix.shape, BF16), jax.ShapeDtypeStruct((L, P), BF16)],
        scratch_shapes=[pltpu.VMEM((T + 2 * HALO, P), F32)],
        input_output_aliases={5: 0},
        compiler_params=_cparams("parallel"),
    )(u, u, u, pw, scale, mix)


def _pool_bwd_mix(dap, pooled, pw, scale, pcol):
    L, P = pooled.shape
    gd = P // POOL_GROUPS
    T = _tile(L, 256, 8)

    def body(dp_ref, pooled_ref, pw_ref, sc_ref, dpooled_ref, dpw_ref, dsc_ref):
        i = pl.program_id(0)

        @pl.when(i == 0)
        def _():
            dpw_ref[...] = jnp.zeros_like(dpw_ref)
            dsc_ref[...] = jnp.zeros_like(dsc_ref)

        for g in range(POOL_GROUPS):
            cols = slice(g * gd, (g + 1) * gd)
            pb = pooled_ref[:, cols]
            dp = dp_ref[:, cols]
            dsc_ref[:, cols] += jnp.sum(dp * _dot(pb, pw_ref[g]), axis=0, keepdims=True)
            dm = (dp * sc_ref[:, cols]).astype(BF16)
            dpw_ref[g] += _dot_tn(pb, dm)
            dpooled_ref[:, cols] = _dot_nt(dm, pw_ref[g])

    return pl.pallas_call(
        body, name="pool_bwd_mix", grid=(L // T,),
        in_specs=[pl.BlockSpec((T, P), lambda i: (i, pcol)), pl.BlockSpec((T, P), lambda i: (i, 0)),
                  pl.BlockSpec((POOL_GROUPS, gd, gd), lambda i: (0, 0, 0)), pl.BlockSpec((1, P), lambda i: (0, 0))],
        out_specs=[pl.BlockSpec((T, P), lambda i: (i, 0)), pl.BlockSpec((POOL_GROUPS, gd, gd), lambda i: (0, 0, 0)),
                   pl.BlockSpec((1, P), lambda i: (0, 0))],
        out_shape=[jax.ShapeDtypeStruct((L, P), F32), jax.ShapeDtypeStruct((POOL_GROUPS, gd, gd), F32), jax.ShapeDtypeStruct((1, P), F32)],
        compiler_params=_cparams("arbitrary"),
    )(dap, pooled, pw, scale)


def _pool_bwd_window(dpooled):
    L, P = dpooled.shape
    gd = P // POOL_GROUPS
    T = _tile(L, 256, 8)
    nt = L // T

    def body(dp_ref, dc_ref, dn_ref, du_ref, buf):
        i = pl.program_id(0)
        _fill_halo_buf(buf, dp_ref, dc_ref, dn_ref, i, nt, T)
        t = i * T - HALO + lax.broadcasted_iota(I32, (T + 2 * HALO, 1), 0)
        for g, w in enumerate(POOL_WINDOWS):
            cols = slice(g * gd, (g + 1) * gd)
            buf[:, cols] = buf[:, cols] / _counts(t, w, L)
            acc = buf[pl.ds(HALO - w // 2 + 1, T), cols]
            for o in range(-w // 2 + 2, w // 2 + 1):
                acc = acc + buf[pl.ds(HALO + o, T), cols]
            du_ref[:, cols] = (acc - dc_ref[:, cols]).astype(BF16)

    return pl.pallas_call(
        body, name="pool_bwd_window", grid=(nt,),
        in_specs=_halo_specs(T, L, P, 0),
        out_specs=pl.BlockSpec((T, P), lambda i: (i, 0)),
        out_shape=jax.ShapeDtypeStruct((L, P), BF16),
        scratch_shapes=[pltpu.VMEM((T + 2 * HALO, P), F32)],
        compiler_params=_cparams("parallel"),
    )(dpooled, dpooled, dpooled)


def _sum_rows(name, a):
    R, N = a.shape

    def body(a_ref, o_ref):
        if R <= 16:
            acc = a_ref[0:1, :]
            for r in range(1, R):
                acc = acc + a_ref[r:r + 1, :]
        else:
            acc = jnp.sum(a_ref[...], axis=0, keepdims=True)
        o_ref[...] = acc

    return pl.pallas_call(body, name=name, out_shape=jax.ShapeDtypeStruct((1, N), F32))(a)


def _sum_lanes(name, a):
    def body(a_ref, o_ref):
        o_ref[...] = jnp.sum(a_ref[...], axis=1, keepdims=True)

    return pl.pallas_call(body, name=name, out_shape=jax.ShapeDtypeStruct((a.shape[0], 1), F32))(a)


def _silu_grad_mul(cv, g):
    def body(c_ref, g_ref, o_ref):
        cvv = c_ref[...]
        s = 1.0 / (1.0 + jnp.exp(-cvv))
        o_ref[...] = g_ref[...] * (s * (1.0 + cvv * (1.0 - s)))

    return pl.pallas_call(body, name="silu_grad_mul", out_shape=jax.ShapeDtypeStruct(cv.shape, F32))(cv, g)


def _adamw(name, w, g, m, v):
    R, C = w.shape
    parts = g.ndim == 3
    n_parts = g.shape[0] if parts else 1
    T = _tile(R, max(8, 262144 // C), 8)

    def body(w_ref, g_ref, m_ref, v_ref, go_ref, d_ref, mo_ref, vo_ref):
        if parts:
            gv = g_ref[0].astype(F32)
            for p in range(1, n_parts):
                gv = gv + g_ref[p].astype(F32)
        else:
            gv = g_ref[...]
        mn = ADAM_B1 * m_ref[...] + (1.0 - ADAM_B1) * gv
        vn = ADAM_B2 * v_ref[...] + (1.0 - ADAM_B2) * (gv * gv)
        m_hat = mn / (1.0 - ADAM_B1 ** ADAM_STEP)
        v_hat = vn / (1.0 - ADAM_B2 ** ADAM_STEP)
        go_ref[...] = gv
        d_ref[...] = -ADAM_LR * (m_hat / (jnp.sqrt(v_hat) + ADAM_EPS) + ADAM_WD * w_ref[...])
        mo_ref[...] = mn
        vo_ref[...] = vn

    tile = pl.BlockSpec((T, C), lambda i: (i, 0))
    g_spec = pl.BlockSpec((n_parts, T, C), lambda i: (0, i, 0)) if parts else tile
    return pl.pallas_call(
        body, name=name, grid=(R // T,),
        in_specs=[tile, g_spec, tile, tile], out_specs=[tile] * 4,
        out_shape=[jax.ShapeDtypeStruct((R, C), F32)] * 4,
        compiler_params=_cparams("parallel"),
    )(w, g, m, v)


def _dev_index(px, py, pc):
    return 4 * px + 2 * py + pc


def _all_gather(name, arrs):
    n = len(arrs)

    def body(*refs):
        ins, outs = refs[:n], refs[n:2 * n]
        send_sems, recv_sems, local_sems = refs[2 * n:]
        x, y, c = lax.axis_index("x"), lax.axis_index("y"), lax.axis_index("c")
        me, sibling = (x, y, c), (x, y, 1 - c)
        chips = [(1 - x, y), (x, 1 - y), (1 - x, 1 - y)]

        def copy(a, k, block, to, src=None):
            slot = outs[a].at[_dev_index(*block)]
            return pltpu.make_async_remote_copy(
                src_ref=slot if src is None else src, dst_ref=slot, send_sem=send_sems.at[a, k], recv_sem=recv_sems.at[a, k],
                device_id=to, device_id_type=MESH)

        mine = [pltpu.make_async_copy(ins[a], outs[a].at[_dev_index(*me)], local_sems.at[a]) for a in range(n)]
        for cp in mine:
            cp.start()
        first = []
        for a in range(n):
            first.append(copy(a, 0, me, sibling, src=ins[a]))
            first += [copy(a, 1 + j, me, (*chip, c), src=ins[a]) for j, chip in enumerate(chips)]
        for cp in first:
            cp.start()
        passed = []
        for j, chip in enumerate(chips):
            for a in range(n):
                copy(a, 1 + j, (*chip, c), me).wait_recv()
                fwd = copy(a, 4 + j, (*chip, c), sibling)
                fwd.start()
                passed.append(fwd)
        for a in range(n):
            copy(a, 0, sibling, me).wait_recv()
            for j, chip in enumerate(chips):
                copy(a, 4 + j, (*chip, 1 - c), me).wait_recv()
        for cp in first + passed:
            cp.wait_send()
        for cp in mine:
            cp.wait()

    return pl.pallas_call(
        body, name=name,
        in_specs=[HBM] * n, out_specs=[HBM] * n,
        out_shape=[jax.ShapeDtypeStruct((N_DEV, *a.shape), a.dtype) for a in arrs],
        scratch_shapes=[pltpu.SemaphoreType.DMA((n, N_DEV - 1)), pltpu.SemaphoreType.DMA((n, N_DEV - 1)), pltpu.SemaphoreType.DMA((n,))],
    )(*arrs)


N_COPIES = {"all_to_all": N_DEV - 1, "gather_chips": 4, "forward": 3}


def _exchange_copies(kind, src_ref, land_ref, send_sems, recv_sems, sending):
    x, y, c = lax.axis_index("x"), lax.axis_index("y"), lax.axis_index("c")
    me = _dev_index(x, y, c)
    others = [(1 - x, y), (x, 1 - y), (1 - x, 1 - y)]
    if kind == "all_to_all":
        flips = [(dx, dy, dc) for dx in (0, 1) for dy in (0, 1) for dc in (0, 1)][1:]
        peers = [(1 - x if dx else x, 1 - y if dy else y, 1 - c if dc else c) for dx, dy, dc in flips]
        plan = [(p, src_ref.at[_dev_index(*p)], me if sending else _dev_index(*p)) for p in peers]
    elif kind == "gather_chips":
        peers = [(x, y, 1 - c)] + [(*o, c) for o in others]
        plan = [(p, src_ref, me if sending else _dev_index(*p)) for p in peers]
    else:
        plan = [((x, y, 1 - c), land_ref.at[_dev_index(*o, c)], _dev_index(*o, c if sending else 1 - c)) for o in others]
    return [pltpu.make_async_remote_copy(src_ref=src, dst_ref=land_ref.at[slot], send_sem=send_sems.at[k], recv_sem=recv_sems.at[k],
                                         device_id=peer, device_id_type=MESH)
            for k, (peer, src, slot) in enumerate(plan)]


def _exchange_start(name, kind, srcs, lands=None):
    if lands is None:
        lands = [lax.empty((N_DEV, *s.shape) if kind == "gather_chips" else s.shape, s.dtype) for s in srcs]
    n = len(lands)
    ops = ([] if srcs is None else list(srcs)) + list(lands)
    m = len(ops)

    def body(*refs):
        src_refs = [None] * n if srcs is None else refs[:n]
        land_refs = refs[m - n:m]
        send_sems, recv_sems, token = refs[m:m + n], refs[m + n:m + 2 * n], refs[-1]
        for a in range(n):
            for cp in _exchange_copies(kind, src_refs[a], land_refs[a], send_sems[a], recv_sems[a], True):
                cp.start()
        token[...] = jnp.zeros_like(token)

    sems = [pltpu.SemaphoreType.DMA((N_COPIES[kind],))] * (2 * n)
    outs = pl.pallas_call(
        body, name=name,
        out_shape=sems + [pltpu.HBM(o.shape, o.dtype) for o in ops] + [jax.ShapeDtypeStruct((8, LANES), F32)],
        in_specs=[HBM] * m,
        out_specs=[SEM] * (2 * n) + [HBM] * m + [pl.BlockSpec(memory_space=pltpu.VMEM)],
        input_output_aliases={i: 2 * n + i for i in range(m)},
        compiler_params=SIDE_EFFECT,
    )(*[pltpu.with_memory_space_constraint(o, pltpu.HBM) for o in ops])
    thru = outs[2 * n:2 * n + m]
    return outs[:n], outs[n:2 * n], (None if srcs is None else thru[:n]), thru[m - n:], outs[-1]


def _exchange_wait(name, kind, send_sems, recv_sems, srcs, lands, after):
    n = len(lands)
    ops = ([] if srcs is None else list(srcs)) + list(lands)
    m = len(ops)

    def body(*refs):
        src_refs = [None] * n if srcs is None else refs[:n]
        land_refs = refs[m - n:m]
        send_refs, recv_refs = refs[m:m + n], refs[m + n:m + 2 * n]
        for a in range(n):
            for cp in _exchange_copies(kind, src_refs[a], land_refs[a], send_refs[a], recv_refs[a], False):
                cp.wait_send()
                cp.wait_recv()

    outs = pl.pallas_call(
        body, name=name,
        out_shape=[pltpu.HBM(o.shape, o.dtype) for o in ops],
        in_specs=[HBM] * m + [SEM] * (2 * n) + [pl.BlockSpec(memory_space=pl.ANY)],
        out_specs=[HBM] * m,
        input_output_aliases={i: i for i in range(m)},
        compiler_params=SIDE_EFFECT,
    )(*ops, *send_sems, *recv_sems, after)
    return (None if srcs is None else outs[:n]), outs[m - n:]


def _with_own(land, own, me):
    return lax.dynamic_update_slice_in_dim(land, own, me, 0)


def _shards_to_cols(g):
    return jnp.transpose(g, (1, 0, 2)).reshape(g.shape[1], N_DEV * g.shape[2])


def _cols_to_shards(a):
    R, Ctot = a.shape
    return jnp.transpose(a.reshape(R, N_DEV, Ctot // N_DEV), (1, 0, 2))


def kernel(x, c, ctx, c_ctx, norm_attn_w, norm_mlp_w, w_ada, b_ada, w_in, attn_sink, pool_w, pool_scale, w_out, w_mlp_up, w_mlp_down, final_norm_w, loss_target, m_c_ctx, m_norm_attn_w, m_norm_mlp_w, m_w_ada, m_b_ada, m_w_in, m_attn_sink, m_pool_w, m_pool_scale, m_w_out, m_w_mlp_up, m_w_mlp_down, m_final_norm_w, v_c_ctx, v_norm_attn_w, v_norm_mlp_w, v_w_ada, v_b_ada, v_w_in, v_attn_sink, v_pool_w, v_pool_scale, v_w_out, v_w_mlp_up, v_w_mlp_down, v_final_norm_w):
    _, L, D = x.shape
    H = attn_sink.shape[1]
    A = H * HEAD_DIM
    KV = A // GQA
    P = pool_scale.shape[1]
    MODW = N_MOD * D
    ws = MODW // N_DEV
    gd = P // POOL_GROUPS
    me = _dev_index(lax.axis_index("x"), lax.axis_index("y"), lax.axis_index("c"))

    x2d, ctx2d, tgt = x[0], ctx[0], loss_target[0]
    cctx_row = c_ctx.reshape(1, D)
    wf_row = final_norm_w.reshape(1, D)
    w_ada_l = w_ada[0]
    pool_w_l = pool_w[0].reshape(POOL_GROUPS * (gd // N_DEV), gd)

    (c_all,) = _all_gather("gather_cond", [c])
    cond = jnp.concatenate([c_all[:, 0, :], cctx_row, jnp.zeros((COND_ROWS - N_DEV - 1, D), F32)], axis=0)
    b_sh = lax.dynamic_slice_in_dim(b_ada, me * ws, ws, axis=1)
    (mods_sh,) = _mm("ada_mod", cond, w_ada_l, "nn", [F32], SMALL_TILES, a_pre=_silu, extras=[("n", b_sh)], epilogue=lambda acc, b: (acc + b,))
    (mods_g,) = _all_gather("gather_mods", [mods_sh])

    w_srcs = [w_in[0].astype(BF16), w_out[0].astype(BF16), pool_w_l.astype(BF16), w_mlp_up[0].astype(BF16), w_mlp_down[0].astype(BF16)]
    w_srcs, mods_g = lax.optimization_barrier((w_srcs, mods_g))
    gw_send, gw_recv, gw_src, gw_land, _ = _exchange_start("gather_weights_start", "gather_chips", w_srcs)

    def weights(tag, lo, hi, after_chips, after_forward):
        mine, lands = _exchange_wait(f"gather_{tag}_wait", "gather_chips", gw_send[lo:hi], gw_recv[lo:hi], gw_src[lo:hi],
                                     gw_land[lo:hi], after_chips)
        f_send, f_recv, _, f_land, f_token = _exchange_start(f"forward_{tag}_start", "forward", None, lands)
        _, lands = _exchange_wait(f"forward_{tag}_wait", "forward", f_send, f_recv, None, f_land,
                                  f_token if after_forward is None else after_forward)
        return [_with_own(l, s[None], me) for l, s in zip(lands, mine)]

    mods = _shards_to_cols(mods_g)
    mod_b = lax.dynamic_slice_in_dim(mods, me, 1, axis=0)
    sh_a, sc_a, g_a, sh_m, sc_m, g_m = [mod_b[:, i * D:(i + 1) * D] for i in range(N_MOD)]
    csh_a, csc_a = mods[N_DEV:N_DEV + 1, :D], mods[N_DEV:N_DEV + 1, D:2 * D]

    cos, sin = _rope_tables(L)
    h = _norm_fwd("norm_attn", x2d, norm_attn_w, sc_a, sh_a)
    hc = _norm_fwd("norm_attn_ctx", ctx2d, norm_attn_w, csc_a, csh_a)
    (win_g,) = weights("w_in", 0, 1, h, None)
    W_in = _shards_to_cols(win_g)
    W_qk, W_kv = W_in[:, :A + KV], W_in[:, A:A + 2 * KV]
    W_uv = jnp.concatenate([W_in[:, A + 2 * KV:], W_in[:, A + KV:A + 2 * KV]], axis=1)
    (qk,) = _mm("in_proj_qk", h, W_qk, "nn", [BF16], (1024, A + KV, D), extras=[("m", cos), ("m", sin)],
                epilogue=lambda acc, cs, sn: (_rope(acc, cs, sn),))
    (uv,) = _mm("in_proj_uv", h, W_uv, "nn", [F32], (1024, P + KV, D))
    (kvc,) = _mm("in_proj_ctx", hc, W_kv, "nn", [BF16], SMALL_TILES)
    attn, lse = _attn_fwd(qk, uv, kvc, attn_sink, A, KV, P)
    wout_g, pw_g = weights("w_out", 1, 3, qk, attn)
    W_out = wout_g.reshape(A + P, D)
    PW = jnp.transpose(pw_g.reshape(N_DEV, POOL_GROUPS, gd // N_DEV, gd), (1, 0, 2, 3)).reshape(POOL_GROUPS, gd, gd)
    ap, pooled = _pool_fwd(uv, PW, pool_scale, attn)
    o, x1, hm = _mm("out_proj_norm", ap, W_out, "nn", [F32, F32, BF16], (256, D, D), chunk=128, epilogue=_out_proj_epilogue,
                    extras=[("mn", x2d), ("n", g_a), ("n", norm_mlp_w), ("n", sc_m), ("n", sh_m)])
    W_up, wdown_g = weights("w_mlp", 3, 5, attn, x1)
    W_down = wdown_g.reshape(-1, D)
    up, act = _mm("mlp_up", hm, W_up, "nn", [F32, BF16], (1024, 1024, 2048), epilogue=lambda acc: (acc, _relu2(acc)), b_shards=True)
    d_x2, d_mlp, d_wf, d_gm, loss_row = _mm(
        "mlp_down_loss", act, W_down, "nn", [F32, BF16], (512, D, 1024), chunk=128, n_sums=3, vmem=FUSED_VMEM_LIMIT,
        epilogue=_mlp_down_epilogue, extras=[("mn", x1), ("mn", tgt), ("n", g_m), ("n", wf_row)])
    loss_p = loss_row[:, :1]

    (d_up,) = _mm("mlp_down_bwd_act", d_mlp, W_down, "nt", [BF16], (1024, 1024, 2048), extras=[("mn", up)], epilogue=lambda acc, uu: (acc * (2.0 * jnp.maximum(uu, 0.0)),))
    (gW_down,) = _mm("mlp_down_bwd_w", act, d_mlp, "tn", [BF16], (1024, 2048, 1024))
    (gW_up_s,) = _mm("mlp_up_bwd_w", hm, d_up, "tn", [BF16], (2048, 1024, 1024), out_shards=True)
    g_mlp_srcs = [gW_up_s, gW_down.reshape(N_DEV, -1, D)]
    g_mlp = _exchange_start("grads_mlp_start", "all_to_all", g_mlp_srcs)
    (d_hm,) = _mm("mlp_up_bwd_act", d_up, W_up, "nt", [F32], (1024, D, 1024), b_shards=True)
    d_x1, s_sh_m, s_sc_m, s_w_nm, d_ga, d_o = _norm_bwd("norm_mlp_bwd", x1, d_hm, d_x2, norm_mlp_w, sc_m,
                                                          jnp.zeros((1, D), F32) + g_mlp[4][0, 0], gate=(o, g_a))

    (d_ap,) = _mm("out_proj_bwd_act", d_o, W_out, "nt", [F32], (1024, 1024, 2048))
    (gW_out,) = _mm("out_proj_bwd_w", ap, d_o, "tn", [BF16], (1024, 2048, 1024))
    d_pooled, gPW, d_pscale = _pool_bwd_mix(d_ap, pooled, PW, pool_scale, A // P)
    gpw_s = jnp.transpose(gPW.astype(BF16).reshape(POOL_GROUPS, N_DEV, gd // N_DEV, gd), (1, 0, 2, 3)).reshape(N_DEV, -1, gd)
    g_mix_srcs = [gW_out.reshape(N_DEV, (A + P) // N_DEV, D), gpw_s]
    g_mix = _exchange_start("grads_mix_start", "all_to_all", g_mix_srcs)
    lse = lse + g_mix[4][0, 0]
    d_u = _pool_bwd_window(d_pooled)
    d_q, rd, dsink_q, d_kvc = _attn_bwd_dq(qk, uv, kvc, attn_sink, d_ap, lse, cos, sin, A, KV, P)
    d_k, d_v = _attn_bwd_dkv(qk, uv, d_ap, lse, rd, cos, sin, A, KV, P)
    d_sink = _sum_lanes("sink_grad", dsink_q).reshape(1, H)
    d_p = jnp.concatenate([d_q, d_k, d_v, d_u], axis=1)
    d_kvc_b = d_kvc.astype(BF16)
    (gW_kv_ctx,) = _mm("in_proj_ctx_bwd_w", hc, d_kvc_b, "tn", [F32], SMALL_TILES)
    (d_hc,) = _mm("in_proj_ctx_bwd_act", d_kvc_b, W_kv, "nt", [F32], SMALL_TILES)
    gW_in_init = jnp.pad(gW_kv_ctx, ((0, 0), (A, P)))
    (gW_in,) = _mm("in_proj_bwd_w", h, d_p, "tn", [BF16], (1024, 1280, 1024), extras=[("mn", gW_in_init)], epilogue=lambda acc, init: (acc + init,))
    g_in_srcs = [_cols_to_shards(gW_in)]
    g_in = _exchange_start("grads_in_start", "all_to_all", g_in_srcs)
    grad_x, s_sh_a, s_sc_a, s_w_na = _mm(
        "in_proj_bwd_norm", d_p, W_in, "nt", [F32], (256, D, A + 2 * KV + P), chunk=128, n_sums=3,
        epilogue=lambda acc, xr, dres, w, sc: _modulated_norm_bwd(xr, acc, dres, w, sc),
        extras=[("mn", x2d), ("mn", d_x1), ("n", norm_attn_w), ("n", sc_a + g_in[4][0, 0])])
    _, s_csh, s_csc, s_w_na = _norm_bwd("norm_attn_ctx_bwd", ctx2d, d_hc, jnp.zeros_like(ctx2d), norm_attn_w, csc_a, s_w_na)

    pad_l = lambda a: jnp.pad(a, ((0, 0), (0, LANES - a.shape[1])))
    d_mod_b = jnp.concatenate([s_sh_a, s_sc_a, d_ga, s_sh_m, s_sc_m, d_gm], axis=1)
    summed = jnp.concatenate([s_csh, s_csc, s_w_na, s_w_nm, d_wf, d_pscale, pad_l(d_sink), pad_l(loss_p)], axis=1)
    (small_g,) = _all_gather("gather_small", [jnp.concatenate([d_mod_b, summed], axis=1)])
    small_g = small_g[:, 0, :]
    tot = _sum_rows("small_sum", small_g[:, MODW:])
    off = [0]
    for wdt in (D, D, D, D, D, P, LANES, LANES):
        off.append(off[-1] + wdt)
    seg = lambda i: tot[:, off[i]:off[i + 1]]
    g_norm_attn, g_norm_mlp, g_final, g_pscale = seg(2), seg(3), seg(4), seg(5)
    g_sink, loss = seg(6)[:, :H], seg(7)[0, 0]
    d_mod_ctx = jnp.concatenate([seg(0), seg(1), jnp.zeros((1, MODW - 2 * D), F32)], axis=1)
    d_mod = jnp.concatenate([small_g[:, :MODW], d_mod_ctx, jnp.zeros((COND_ROWS - N_DEV - 1, MODW), F32)], axis=0)
    g_b_ada = _sum_rows("b_ada_grad", d_mod[:N_DEV + 1])
    d_mod_sh = lax.dynamic_slice_in_dim(d_mod, me * ws, ws, axis=1)
    (g_w_ada,) = _mm("ada_bwd_w", cond, d_mod_sh, "tn", [F32], SMALL_TILES, a_pre=_silu)
    (d_cond_p,) = _mm("ada_bwd_cond", d_mod_sh, w_ada_l, "nt", [F32], SMALL_TILES)
    (d_cctx_g,) = _all_gather("gather_cctx", [d_cond_p[N_DEV:N_DEV + 1]])
    g_c_ctx = _silu_grad_mul(cctx_row, _sum_rows("cctx_sum", d_cctx_g[:, 0, :]))

    def arrived(name, started):
        srcs, lands = _exchange_wait(name, "all_to_all", started[0], started[1], started[2], started[3], g_c_ctx)
        return [_with_own(l, lax.dynamic_index_in_dim(s, me, 0, keepdims=True), me) for l, s in zip(lands, srcs)]

    r_up, r_down = arrived("grads_mlp_wait", g_mlp)
    r_out, r_pw = arrived("grads_mix_wait", g_mix)
    (r_in,) = arrived("grads_in_wait", g_in)

    results = {
        "c_ctx": _adamw("adam_c_ctx", cctx_row, g_c_ctx, m_c_ctx.reshape(1, D), v_c_ctx.reshape(1, D)),
        "norm_attn_w": _adamw("adam_norm_attn", norm_attn_w, g_norm_attn, m_norm_attn_w, v_norm_attn_w),
        "norm_mlp_w": _adamw("adam_norm_mlp", norm_mlp_w, g_norm_mlp, m_norm_mlp_w, v_norm_mlp_w),
        "w_ada": _adamw("adam_w_ada", w_ada_l, g_w_ada, m_w_ada[0], v_w_ada[0]),
        "b_ada": _adamw("adam_b_ada", b_ada, g_b_ada, m_b_ada, v_b_ada),
        "w_in": _adamw("adam_w_in", w_in[0], r_in, m_w_in[0], v_w_in[0]),
        "attn_sink": _adamw("adam_sink", attn_sink, g_sink, m_attn_sink, v_attn_sink),
        "pool_w": _adamw("adam_pool_w", pool_w_l, r_pw, m_pool_w[0].reshape(pool_w_l.shape), v_pool_w[0].reshape(pool_w_l.shape)),
        "pool_scale": _adamw("adam_pool_scale", pool_scale, g_pscale, m_pool_scale, v_pool_scale),
        "w_out": _adamw("adam_w_out", w_out[0], r_out, m_w_out[0], v_w_out[0]),
        "w_mlp_up": _adamw("adam_w_up", w_mlp_up[0], r_up, m_w_mlp_up[0], v_w_mlp_up[0]),
        "w_mlp_down": _adamw("adam_w_down", w_mlp_down[0], r_down, m_w_mlp_down[0], v_w_mlp_down[0]),
        "final_norm_w": _adamw("adam_final_norm", wf_row, g_final, m_final_norm_w.reshape(1, D), v_final_norm_w.reshape(1, D)),
    }
    shapes = {"c_ctx": c_ctx.shape, "norm_attn_w": norm_attn_w.shape, "norm_mlp_w": norm_mlp_w.shape, "w_ada": w_ada.shape,
              "b_ada": b_ada.shape, "w_in": w_in.shape, "attn_sink": attn_sink.shape, "pool_w": pool_w.shape,
              "pool_scale": pool_scale.shape, "w_out": w_out.shape, "w_mlp_up": w_mlp_up.shape, "w_mlp_down": w_mlp_down.shape,
              "final_norm_w": final_norm_w.shape}
    outs = [loss, grad_x.reshape(x.shape)]
    for part in range(4):
        outs += [results[name][part].reshape(shape) for name, shape in shapes.items()]
    return tuple(outs)
```

```python
import functools

import jax
import jax.numpy as jnp
import numpy as np
from jax import lax
from jax.experimental import pallas as pl
from jax.experimental.pallas import tpu as pltpu

F32 = jnp.float32
BF16 = jnp.bfloat16
I32 = jnp.int32

HEAD_DIM = 64
GQA = 4
BLOCK = 128
GRID_W = 64
ROPE_BASE = 10000.0
POOL_WINDOWS = (2, 4, 8, 16)
POOL_GROUPS = len(POOL_WINDOWS)
HALO = 8
N_MOD = 6
EPS = 1e-6
NEG_INF = -1e30
ADAM_LR = 0.001
ADAM_B1 = 0.9
ADAM_B2 = 0.999
ADAM_EPS = 1e-08
ADAM_WD = 0.01
ADAM_STEP = 10
N_DEV = 8
COND_ROWS = 2 * N_DEV
LANES = 128
SUBLANES_16BIT = 16
VMEM_LIMIT = 48 * 1024 * 1024
FUSED_VMEM_LIMIT = 56 * 1024 * 1024
SMALL_TILES = (512, 1024, 512)
MESH = pl.DeviceIdType.MESH
HBM = pl.BlockSpec(memory_space=pltpu.HBM)
SEM = pl.BlockSpec(memory_space=pltpu.SEMAPHORE)
SIDE_EFFECT = pltpu.CompilerParams(has_side_effects=pltpu.SideEffectType.DATAFLOW_SIDE_EFFECTING)


def _cparams(*sem):
    return pltpu.CompilerParams(dimension_semantics=sem, vmem_limit_bytes=VMEM_LIMIT)


def _tile(n, pref, align):
    if n <= pref:
        return n
    t = (pref // align) * align
    while t >= align:
        if n % t == 0:
            return t
        t -= align
    return n


def _dot(a, b):
    return lax.dot_general(a, b, (((1,), (0,)), ((), ())), preferred_element_type=F32)


def _dot_nt(a, b):
    return lax.dot_general(a, b, (((1,), (1,)), ((), ())), preferred_element_type=F32)


def _dot_tn(a, b):
    return lax.dot_general(a, b, (((0,), (0,)), ((), ())), preferred_element_type=F32)


_DOTS = {"nn": _dot, "nt": _dot_nt, "tn": _dot_tn}


def _mm(name, a, b, mode, out_dtypes, tiles, *, epilogue=None, extras=(), a_pre=None, n_sums=0, chunk=None,
        b_shards=False, out_shards=False, vmem=VMEM_LIMIT):
    if mode == "nn":
        M, K = a.shape
        K2, N = (b.shape[1], N_DEV * b.shape[2]) if b_shards else b.shape
    elif mode == "nt":
        M, K = a.shape
        N, K2 = (b.shape[1], N_DEV * b.shape[2]) if b_shards else b.shape
    else:
        (K, M), (K2, N) = a.shape, b.shape
    assert K == K2 and not (b_shards and mode == "tn"), (name, a.shape, b.shape)
    n_span = N // N_DEV if out_shards or (b_shards and mode == "nn") else N
    k_span = K // N_DEV if b_shards and mode == "nt" else K
    tm = _tile(M, tiles[0], LANES if mode == "tn" else SUBLANES_16BIT)
    tn = _tile(n_span, tiles[1], LANES)
    tk = _tile(k_span, tiles[2], SUBLANES_16BIT if mode == "tn" else LANES)
    nk, nb, kb = K // tk, n_span // tn, k_span // tk
    rows = tm if chunk is None else min(chunk, tm)
    n_ex, n_out = len(extras), len(out_dtypes)
    use_acc = nk > 1 or rows < tm
    assert n_sums == 0 or N == tn, name

    def product(a_ref, b_ref):
        at = a_ref[...]
        if a_pre is not None:
            at = a_pre(at)
        return _DOTS[mode](at.astype(BF16), b_ref[...].astype(BF16))

    def apply(acc, ex, out_refs, sl):
        res = (acc,) if epilogue is None else epilogue(acc, *ex)
        for o_ref, o in zip(out_refs, res[:n_out]):
            o_ref[sl, :] = o.astype(o_ref.dtype)
        return tuple(res[n_out:])

    def finish(acc, ex_refs, out_refs, sum_refs):
        if rows == tm:
            acc = acc if not use_acc else acc[...]
            sums = apply(acc, [r[...] for r in ex_refs], out_refs, slice(None))
        else:
            def one(ci, sums):
                sl = pl.ds(pl.multiple_of(ci * rows, rows), rows)
                ex = [r[...] if kind == "n" else r[sl, :] for (kind, _), r in zip(extras, ex_refs)]
                return tuple(s + v for s, v in zip(sums, apply(acc[sl, :], ex, out_refs, sl)))
            sums = lax.fori_loop(0, tm // rows, one, tuple(jnp.zeros((1, tn), F32) for _ in range(n_sums)))
        first = pl.program_id(0) == 0
        for s_ref, sv in zip(sum_refs, sums):
            @pl.when(first)
            def _(s_ref=s_ref, sv=sv):
                s_ref[...] = sv

            @pl.when(jnp.logical_not(first))
            def _(s_ref=s_ref, sv=sv):
                s_ref[...] += sv

    def body(a_ref, b_ref, *rest):
        ex_refs, out_refs = rest[:n_ex], rest[n_ex:n_ex + n_out]
        sum_refs = rest[n_ex + n_out:n_ex + n_out + n_sums]
        if not use_acc:
            finish(product(a_ref, b_ref), ex_refs, out_refs, sum_refs)
            return
        acc_ref = rest[-1]
        k = pl.program_id(2)

        @pl.when(k == 0)
        def _():
            acc_ref[...] = product(a_ref, b_ref)

        @pl.when(k > 0)
        def _():
            acc_ref[...] += product(a_ref, b_ref)

        @pl.when(k == nk - 1)
        def _():
            finish(acc_ref, ex_refs, out_refs, sum_refs)

    a_spec = pl.BlockSpec((tk, tm), lambda i, j, k: (k, i)) if mode == "tn" else pl.BlockSpec((tm, tk), lambda i, j, k: (i, k))
    if not b_shards:
        b_spec = pl.BlockSpec((tn, tk), lambda i, j, k: (j, k)) if mode == "nt" else pl.BlockSpec((tk, tn), lambda i, j, k: (k, j))
    elif mode == "nn":
        b_spec = pl.BlockSpec((None, tk, tn), lambda i, j, k: (j // nb, k, j % nb))
    else:
        b_spec = pl.BlockSpec((None, tn, tk), lambda i, j, k: (k // kb, j, k % kb))
    ex_specs = []
    for kind, arr in extras:
        if kind == "mn":
            ex_specs.append(pl.BlockSpec((tm, tn), lambda i, j, k: (i, j)))
        elif kind == "n":
            ex_specs.append(pl.BlockSpec((1, tn), lambda i, j, k: (0, j)))
        else:
            ex_specs.append(pl.BlockSpec((tm, arr.shape[1]), lambda i, j, k: (i, 0)))
    if out_shards:
        out_specs = [pl.BlockSpec((None, tm, tn), lambda i, j, k: (j // nb, i, j % nb)) for _ in out_dtypes]
        out_shape = [jax.ShapeDtypeStruct((N_DEV, M, n_span), d) for d in out_dtypes]
    else:
        out_specs = [pl.BlockSpec((tm, tn), lambda i, j, k: (i, j)) for _ in out_dtypes]
        out_shape = [jax.ShapeDtypeStruct((M, N), d) for d in out_dtypes]
    out_specs += [pl.BlockSpec((1, tn), lambda i, j, k: (0, 0))] * n_sums
    out_shape += [jax.ShapeDtypeStruct((1, N), F32)] * n_sums
    return pl.pallas_call(
        body,
        name=name,
        grid=(M // tm, N // tn, nk),
        in_specs=[a_spec, b_spec] + ex_specs,
        out_specs=out_specs,
        out_shape=out_shape,
        scratch_shapes=[pltpu.VMEM((tm, tn), F32)] if use_acc else [],
        compiler_params=pltpu.CompilerParams(
            dimension_semantics=("arbitrary",) * 3 if n_sums else ("parallel", "parallel", "arbitrary"), vmem_limit_bytes=vmem),
    )(a, b, *[arr for _, arr in extras])


def _silu(v):
    return v / (1.0 + jnp.exp(-v))


def _relu2(v):
    r = jnp.maximum(v, 0.0)
    return r * r


def _rope_tables(L):
    half = HEAD_DIM // 2
    inv_freq = np.float32(ROPE_BASE) ** (-np.arange(0, half, 2, dtype=np.float32) / np.float32(half))
    t = np.arange(L)
    row, col = t // GRID_W, t % GRID_W
    ang_r = row.astype(np.float32)[:, None] * inv_freq[None, :]
    ang_c = col.astype(np.float32)[:, None] * inv_freq[None, :]
    cos = np.concatenate([np.cos(ang_r), np.cos(ang_r), np.cos(ang_c), np.cos(ang_c)], axis=1)
    sin = np.concatenate([-np.sin(ang_r), np.sin(ang_r), -np.sin(ang_c), np.sin(ang_c)], axis=1)
    reps = LANES // HEAD_DIM
    return jnp.asarray(np.tile(cos, (1, reps)), F32), jnp.asarray(np.tile(sin, (1, reps)), F32)


def _rope(xf, cos, sin):
    quarter = HEAD_DIM // 4
    lane = lax.broadcasted_iota(I32, (xf.shape[0], LANES), 1)
    first = (lane & quarter) == 0
    outs = []
    for j in range(xf.shape[1] // LANES):
        xc = xf[:, j * LANES:(j + 1) * LANES]
        partner = jnp.where(first, pltpu.roll(xc, LANES - quarter, 1), pltpu.roll(xc, quarter, 1))
        outs.append(xc * cos + partner * sin)
    return outs[0] if len(outs) == 1 else jnp.concatenate(outs, axis=1)


def _inv_rms(xf):
    return lax.rsqrt(jnp.mean(xf * xf, axis=-1, keepdims=True) + EPS)


def _modulated_norm(xf, w, sc, sh):
    return ((xf * _inv_rms(xf)) * w) * (1.0 + sc) + sh


def _modulated_norm_bwd(xf, dh, dres, w, sc):
    r = _inv_rms(xf)
    xh = xf * r
    dn = dh * (1.0 + sc)
    dxh = dn * w
    dx = dres + r * (dxh - xh * jnp.mean(dxh * xh, axis=-1, keepdims=True))
    col = lambda v: jnp.sum(v, axis=0, keepdims=True)
    return dx, col(dh), col(dh * (xh * w)), col(dn * xh)


def _norm_fwd(name, x, w, sc, sh):
    L, D = x.shape
    T = _tile(L, 256, 8)

    def body(x_ref, w_ref, sc_ref, sh_ref, h_ref):
        h_ref[...] = _modulated_norm(x_ref[...], w_ref[...], sc_ref[...], sh_ref[...]).astype(BF16)

    row = pl.BlockSpec((1, D), lambda i: (0, 0))
    return pl.pallas_call(
        body, name=name, grid=(L // T,),
        in_specs=[pl.BlockSpec((T, D), lambda i: (i, 0)), row, row, row],
        out_specs=pl.BlockSpec((T, D), lambda i: (i, 0)),
        out_shape=jax.ShapeDtypeStruct((L, D), BF16),
        compiler_params=_cparams("parallel"),
    )(x, w, sc, sh)


def _norm_bwd(name, x, dh, dres, w, sc, w_init, gate=None):
    L, D = x.shape
    T = _tile(L, 256, 8)
    with_gate = gate is not None

    def body(x_ref, dh_ref, dres_ref, w_ref, sc_ref, wi_ref, *rest):
        if with_gate:
            o_ref, g_ref, dx_ref, ssh_ref, ssc_ref, sw_ref, sg_ref, do_ref = rest
        else:
            dx_ref, ssh_ref, ssc_ref, sw_ref = rest
        i = pl.program_id(0)

        @pl.when(i == 0)
        def _():
            ssh_ref[...] = jnp.zeros_like(ssh_ref)
            ssc_ref[...] = jnp.zeros_like(ssc_ref)
            sw_ref[...] = wi_ref[...]
            if with_gate:
                sg_ref[...] = jnp.zeros_like(sg_ref)

        dx, s_sh, s_sc, s_w = _modulated_norm_bwd(x_ref[...], dh_ref[...], dres_ref[...], w_ref[...], sc_ref[...])
        ssh_ref[...] += s_sh
        ssc_ref[...] += s_sc
        sw_ref[...] += s_w
        dx_ref[...] = dx
        if with_gate:
            sg_ref[...] += jnp.sum(dx * o_ref[...], axis=0, keepdims=True)
            do_ref[...] = (g_ref[...] * dx).astype(BF16)

    tile = pl.BlockSpec((T, D), lambda i: (i, 0))
    row = pl.BlockSpec((1, D), lambda i: (0, 0))
    in_specs = [tile, tile, tile, row, row, row]
    out_specs = [tile, row, row, row]
    out_shape = [jax.ShapeDtypeStruct((L, D), F32)] + [jax.ShapeDtypeStruct((1, D), F32)] * 3
    args = [x, dh, dres, w, sc, w_init]
    if with_gate:
        in_specs += [tile, row]
        out_specs += [row, tile]
        out_shape += [jax.ShapeDtypeStruct((1, D), F32), jax.ShapeDtypeStruct((L, D), BF16)]
        args += list(gate)
    return pl.pallas_call(
        body, name=name, grid=(L // T,), in_specs=in_specs, out_specs=out_specs, out_shape=out_shape,
        compiler_params=_cparams("arbitrary"),
    )(*args)


def _out_proj_epilogue(acc, xr, g, w, sc, sh):
    x1 = xr + g * acc
    return acc, x1, _modulated_norm(x1, w, sc, sh)


def _mlp_down_epilogue(acc, x1, tgt, g, wf):
    D = acc.shape[1]
    x2 = x1 + g * acc
    r = _inv_rms(x2)
    xh = x2 * r
    err = xh * wf - tgt
    loss = 0.5 * jnp.sum(jnp.mean(err * err, axis=-1, keepdims=True), axis=0, keepdims=True)
    dy = err / D
    dxh = dy * wf
    dx = r * (dxh - xh * jnp.mean(dxh * xh, axis=-1, keepdims=True))
    col = lambda v: jnp.sum(v, axis=0, keepdims=True)
    return dx, g * dx, col(dy * xh), col(dx * acc), jnp.broadcast_to(loss, (1, D))


def _heads(ref, first, n):
    return jnp.concatenate([ref[:, (first + g) * HEAD_DIM:(first + g + 1) * HEAD_DIM] for g in range(n)], axis=0)


def _attn_mask(n, L, C):
    shape = (3 * BLOCK + C, GQA * BLOCK)
    kj = lax.broadcasted_iota(I32, shape, 0)
    qi = lax.broadcasted_iota(I32, shape, 1) & (BLOCK - 1)
    kpos = n * BLOCK - BLOCK + kj
    window = (kj >= qi) & (kj <= qi + 2 * BLOCK) & (kpos >= 0) & (kpos < L)
    return window | (kj >= 3 * BLOCK)


def _head_rows(ref, hk):
    return jnp.concatenate([ref[hk * GQA + g:hk * GQA + g + 1, :] for g in range(GQA)], axis=1)


def _rows_to_heads(rows_by_kv_head):
    return jnp.concatenate([r[:, g * BLOCK:(g + 1) * BLOCK] for r in rows_by_kv_head for g in range(GQA)], axis=0)


def _queries_to_rows(t):
    return jnp.concatenate([t[:, g * BLOCK:(g + 1) * BLOCK].T for g in range(GQA)], axis=1)


def _attn_specs(L, A, KV, C, vcol):
    nb = L // BLOCK
    kcol = A // KV
    prev = lambda n: jnp.maximum(n - 1, 0)
    nxt = lambda n: jnp.minimum(n + 1, nb - 1)
    q_spec = pl.BlockSpec((BLOCK, A), lambda n: (n, 0))
    k_specs = [pl.BlockSpec((BLOCK, KV), lambda n: (prev(n), kcol)), pl.BlockSpec((BLOCK, KV), lambda n: (n, kcol)),
               pl.BlockSpec((BLOCK, KV), lambda n: (nxt(n), kcol))]
    v_specs = [pl.BlockSpec((BLOCK, KV), lambda n: (prev(n), vcol)), pl.BlockSpec((BLOCK, KV), lambda n: (n, vcol)),
               pl.BlockSpec((BLOCK, KV), lambda n: (nxt(n), vcol))]
    kvc_spec = pl.BlockSpec((C, 2 * KV), lambda n: (0, 0))
    return q_spec, k_specs, v_specs, kvc_spec


def _keys_values(hk, k_refs, v_refs, kvc_ref, KV):
    sl = slice(hk * HEAD_DIM, (hk + 1) * HEAD_DIM)
    keys = jnp.concatenate([r[:, sl] for r in k_refs] + [kvc_ref[:, sl]], axis=0)
    vals = jnp.concatenate([r[:, sl].astype(BF16) for r in v_refs] + [kvc_ref[:, KV + hk * HEAD_DIM:KV + (hk + 1) * HEAD_DIM]], axis=0)
    return keys, vals


def _sink_row(sink_ref, hk):
    return jnp.concatenate([jnp.full((1, BLOCK), sink_ref[0, hk * GQA + g], F32) for g in range(GQA)], axis=1)


def _attn_fwd(qk, uv, kvc, sink, A, KV, P):
    L = qk.shape[0]
    C = kvc.shape[0]
    nkv = KV // HEAD_DIM
    H = nkv * GQA
    scale = HEAD_DIM ** -0.5

    def body(sink_ref, q_ref, kp_ref, kc_ref, kn_ref, vp_ref, vc_ref, vn_ref, kvc_ref, o_ref, lse_ref):
        valid = _attn_mask(pl.program_id(0), L, C)
        lse_rows = []
        for hk in range(nkv):
            keys, vals = _keys_values(hk, (kp_ref, kc_ref, kn_ref), (vp_ref, vc_ref, vn_ref), kvc_ref, KV)
            qs = _heads(q_ref, hk * GQA, GQA) * scale
            s = jnp.where(valid, _dot_nt(keys, qs), NEG_INF)
            sk = _sink_row(sink_ref, hk)
            m = jnp.maximum(jnp.max(s, axis=0, keepdims=True), sk)
            p = jnp.exp(s - m)
            den = jnp.sum(p, axis=0, keepdims=True) + jnp.exp(sk - m)
            o = _dot_tn(vals, p.astype(BF16)) * (1.0 / den)
            lse_rows.append(m + jnp.log(den))
            o_ref[:, hk * GQA * HEAD_DIM:(hk + 1) * GQA * HEAD_DIM] = _queries_to_rows(o).astype(BF16)
        lse_ref[...] = _rows_to_heads(lse_rows)

    q_spec, k_specs, v_specs, kvc_spec = _attn_specs(L, A, KV, C, P // KV)
    return pl.pallas_call(
        body, name="attn_fwd", grid=(L // BLOCK,),
        in_specs=[pl.BlockSpec(memory_space=pltpu.SMEM), q_spec] + k_specs + v_specs + [kvc_spec],
        out_specs=[pl.BlockSpec((BLOCK, A), lambda n: (n, 0)), pl.BlockSpec((H, BLOCK), lambda n: (0, n))],
        out_shape=[jax.ShapeDtypeStruct((L, A + P), BF16), jax.ShapeDtypeStruct((H, L), F32)],
        compiler_params=_cparams("parallel"),
    )(sink, qk, qk, qk, qk, uv, uv, uv, kvc)


def _attn_bwd_dq(qk, uv, kvc, sink, dap, lse_t, cos, sin, A, KV, P):
    L = qk.shape[0]
    C = kvc.shape[0]
    nkv = KV // HEAD_DIM
    H = nkv * GQA
    scale = HEAD_DIM ** -0.5
    W = 3 * BLOCK

    def body(sink_ref, q_ref, kp_ref, kc_ref, kn_ref, vp_ref, vc_ref, vn_ref, kvc_ref, do_ref, lse_ref, cos_ref, sin_ref,
             dq_ref, rd_ref, ds_ref, dkvc_ref):
        n = pl.program_id(0)

        @pl.when(n == 0)
        def _():
            dkvc_ref[...] = jnp.zeros_like(dkvc_ref)

        valid = _attn_mask(n, L, C)
        rd_rows, dsink_rows, dq_parts = [], [], []
        for hk in range(nkv):
            sl = slice(hk * HEAD_DIM, (hk + 1) * HEAD_DIM)
            keys, vals = _keys_values(hk, (kp_ref, kc_ref, kn_ref), (vp_ref, vc_ref, vn_ref), kvc_ref, KV)
            qs = _heads(q_ref, hk * GQA, GQA) * scale
            dos = _heads(do_ref, hk * GQA, GQA).astype(BF16)
            lse = _head_rows(lse_ref, hk)
            p = jnp.exp(jnp.where(valid, _dot_nt(keys, qs), NEG_INF) - lse)
            dp = _dot_nt(vals, dos)
            rd = jnp.sum(p * dp, axis=0, keepdims=True)
            ds = (p * (dp - rd)).astype(BF16)
            dq_parts.append(_queries_to_rows(_dot_tn(keys, ds) * scale))
            dkvc_ref[:, sl] += _dot(ds[W:, :], qs)
            dkvc_ref[:, KV + hk * HEAD_DIM:KV + (hk + 1) * HEAD_DIM] += _dot(p[W:, :].astype(BF16), dos)
            rd_rows.append(rd)
            dsink_rows.append(-(jnp.exp(_sink_row(sink_ref, hk) - lse) * rd))
        rd_ref[...] = _rows_to_heads(rd_rows)
        ds_ref[...] = _rows_to_heads(dsink_rows)
        dq = dq_parts[0] if nkv == 1 else jnp.concatenate(dq_parts, axis=1)
        dq_ref[...] = _rope(dq, cos_ref[...], -sin_ref[...]).astype(BF16)

    q_spec, k_specs, v_specs, kvc_spec = _attn_specs(L, A, KV, C, P // KV)
    blk = lambda w: pl.BlockSpec((BLOCK, w), lambda n: (n, 0))
    per_head = pl.BlockSpec((H, BLOCK), lambda n: (0, n))
    return pl.pallas_call(
        body, name="attn_bwd_dq", grid=(L // BLOCK,),
        in_specs=[pl.BlockSpec(memory_space=pltpu.SMEM), q_spec] + k_specs + v_specs + [kvc_spec, blk(A), per_head, blk(LANES), blk(LANES)],
        out_specs=[blk(A), per_head, per_head, pl.BlockSpec((C, 2 * KV), lambda n: (0, 0))],
        out_shape=[jax.ShapeDtypeStruct((L, A), BF16), jax.ShapeDtypeStruct((H, L), F32), jax.ShapeDtypeStruct((H, L), F32),
                   jax.ShapeDtypeStruct((C, 2 * KV), F32)],
        compiler_params=_cparams("arbitrary"),
    )(sink, qk, qk, qk, qk, uv, uv, uv, kvc, dap, lse_t, cos, sin)


def _attn_bwd_dkv(qk, uv, dap, lse_t, rd_t, cos, sin, A, KV, P):
    L = qk.shape[0]
    nb = L // BLOCK
    nkv = KV // HEAD_DIM
    H = nkv * GQA
    scale = HEAD_DIM ** -0.5
    R = 3 * GQA * BLOCK

    def body(k_ref, v_ref, qp_ref, qc_ref, qn_ref, dop_ref, doc_ref, don_ref, lsep_ref, lsec_ref, lsen_ref,
             rdp_ref, rdc_ref, rdn_ref, cos_ref, sin_ref, dk_ref, dv_ref):
        m = pl.program_id(0)
        kj = lax.broadcasted_iota(I32, (BLOCK, R), 0)
        col = lax.broadcasted_iota(I32, (BLOCK, R), 1)
        part = col // (GQA * BLOCK)
        qi = col & (BLOCK - 1)
        before = jnp.where(m >= 1, 0, -2 * BLOCK)
        after = jnp.where(m <= nb - 2, 0, 2 * BLOCK)
        valid = ((part == 0) & (kj <= qi + before)) | (part == 1) | ((part == 2) & (kj >= qi + after))
        dk_parts, dv_parts = [], []
        for hk in range(nkv):
            sl = slice(hk * HEAD_DIM, (hk + 1) * HEAD_DIM)
            km = k_ref[:, sl]
            vm = v_ref[:, sl].astype(BF16)
            qs = jnp.concatenate([_heads(q, hk * GQA, GQA) for q in (qp_ref, qc_ref, qn_ref)], axis=0) * scale
            dos = jnp.concatenate([_heads(d, hk * GQA, GQA) for d in (dop_ref, doc_ref, don_ref)], axis=0).astype(BF16)
            rows = [slice(hk * GQA + g, hk * GQA + g + 1) for g in range(GQA)]
            lse = jnp.concatenate([t[r, :] for t in (lsep_ref, lsec_ref, lsen_ref) for r in rows], axis=1)
            rdv = jnp.concatenate([t[r, :] for t in (rdp_ref, rdc_ref, rdn_ref) for r in rows], axis=1)
            p = jnp.exp(jnp.where(valid, _dot_nt(km, qs), NEG_INF) - lse)
            ds = (p * (_dot_nt(vm, dos) - rdv)).astype(BF16)
            dk_parts.append(_dot(ds, qs))
            dv_parts.append(_dot(p.astype(BF16), dos))
        dk = dk_parts[0] if nkv == 1 else jnp.concatenate(dk_parts, axis=1)
        dv = dv_parts[0] if nkv == 1 else jnp.concatenate(dv_parts, axis=1)
        dk_ref[...] = _rope(dk, cos_ref[...], -sin_ref[...]).astype(BF16)
        dv_ref[...] = dv.astype(BF16)

    prev = lambda m: jnp.maximum(m - 1, 0)
    nxt = lambda m: jnp.minimum(m + 1, nb - 1)
    three = lambda w: [pl.BlockSpec((BLOCK, w), lambda m: (prev(m), 0)), pl.BlockSpec((BLOCK, w), lambda m: (m, 0)),
                       pl.BlockSpec((BLOCK, w), lambda m: (nxt(m), 0))]
    three_t = [pl.BlockSpec((H, BLOCK), lambda m: (0, prev(m))), pl.BlockSpec((H, BLOCK), lambda m: (0, m)),
               pl.BlockSpec((H, BLOCK), lambda m: (0, nxt(m)))]
    blk = lambda w: pl.BlockSpec((BLOCK, w), lambda m: (m, 0))
    return pl.pallas_call(
        body, name="attn_bwd_dkv", grid=(nb,),
        in_specs=[pl.BlockSpec((BLOCK, KV), lambda m: (m, A // KV)), pl.BlockSpec((BLOCK, KV), lambda m: (m, P // KV))]
                 + three(A) + three(A) + three_t + three_t + [blk(LANES), blk(LANES)],
        out_specs=[blk(KV), blk(KV)],
        out_shape=[jax.ShapeDtypeStruct((L, KV), BF16), jax.ShapeDtypeStruct((L, KV), BF16)],
        compiler_params=_cparams("parallel"),
    )(qk, uv, qk, qk, qk, dap, dap, dap, lse_t, lse_t, lse_t, rd_t, rd_t, rd_t, cos, sin)


def _halo_specs(T, L, W, col):
    per = T // HALO
    return [pl.BlockSpec((HALO, W), lambda i: (jnp.maximum(i * per - 1, 0), col)),
            pl.BlockSpec((T, W), lambda i: (i, col)),
            pl.BlockSpec((HALO, W), lambda i: (jnp.minimum((i + 1) * per, L // HALO - 1), col))]


def _fill_halo_buf(buf, prev_ref, cur_ref, next_ref, i, nt, T):
    buf[0:HALO, :] = jnp.where(i > 0, prev_ref[...], 0.0)
    buf[HALO:HALO + T, :] = cur_ref[...]
    buf[HALO + T:2 * HALO + T, :] = jnp.where(i < nt - 1, next_ref[...], 0.0)


def _counts(t, w, L):
    lo = jnp.clip(t - w // 2, 0, L)
    hi = jnp.clip(t - w // 2 + w, 0, L)
    return jnp.maximum(hi - lo, 1).astype(F32)


def _pool_fwd(u, pw, scale, mix):
    L, P = u.shape[0], scale.shape[1]
    gd = P // POOL_GROUPS
    T = _tile(L, 256, 8)
    nt = L // T
    assert (mix.shape[1] - P) % P == 0
    mix_col = mix.shape[1] // P - 1

    def body(up_ref, uc_ref, un_ref, pw_ref, sc_ref, mix_ref, out_ref, pooled_ref, buf):
        i = pl.program_id(0)
        _fill_halo_buf(buf, up_ref, uc_ref, un_ref, i, nt, T)
        t = i * T + lax.broadcasted_iota(I32, (T, 1), 0)
        for g, w in enumerate(POOL_WINDOWS):
            cols = slice(g * gd, (g + 1) * gd)
            acc = buf[pl.ds(HALO - w // 2, T), cols]
            for o in range(-w // 2 + 1, w // 2):
                acc = acc + buf[pl.ds(HALO + o, T), cols]
            pooled = (acc / _counts(t, w, L) - buf[pl.ds(HALO, T), cols]).astype(BF16)
            pooled_ref[:, cols] = pooled
            out_ref[:, cols] = (_dot(pooled, pw_ref[g]) * sc_ref[:, cols]).astype(BF16)

    return pl.pallas_call(
        body, name="pool_fwd", grid=(nt,),
        in_specs=_halo_specs(T, L, P, 0) + [pl.BlockSpec((POOL_GROUPS, gd, gd), lambda i: (0, 0, 0)), pl.BlockSpec((1, P), lambda i: (0, 0)),
                                            pl.BlockSpec(memory_space=pl.ANY)],
        out_specs=[pl.BlockSpec((T, P), lambda i: (i, mix_col)), pl.BlockSpec((T, P), lambda i: (i, 0))],
        out_shape=[jax.ShapeDtypeStruct(mix.shape, BF16), jax.ShapeDtypeStruct((L, P), BF16)],
        scratch_shapes=[pltpu.VMEM((T + 2 * HALO, P), F32)],
        input_output_aliases={5: 0},
        compiler_params=_cparams("parallel"),
    )(u, u, u, pw, scale, mix)


def _pool_bwd_mix(dap, pooled, pw, scale, pcol):
    L, P = pooled.shape
    gd = P // POOL_GROUPS
    T = _tile(L, 256, 8)

    def body(dp_ref, pooled_ref, pw_ref, sc_ref, dpooled_ref, dpw_ref, dsc_ref):
        i = pl.program_id(0)

        @pl.when(i == 0)
        def _():
            dpw_ref[...] = jnp.zeros_like(dpw_ref)
            dsc_ref[...] = jnp.zeros_like(dsc_ref)

        for g in range(POOL_GROUPS):
            cols = slice(g * gd, (g + 1) * gd)
            pb = pooled_ref[:, cols]
            dp = dp_ref[:, cols]
            dsc_ref[:, cols] += jnp.sum(dp * _dot(pb, pw_ref[g]), axis=0, keepdims=True)
            dm = (dp * sc_ref[:, cols]).astype(BF16)
            dpw_ref[g] += _dot_tn(pb, dm)
            dpooled_ref[:, cols] = _dot_nt(dm, pw_ref[g])

    return pl.pallas_call(
        body, name="pool_bwd_mix", grid=(L // T,),
        in_specs=[pl.BlockSpec((T, P), lambda i: (i, pcol)), pl.BlockSpec((T, P), lambda i: (i, 0)),
                  pl.BlockSpec((POOL_GROUPS, gd, gd), lambda i: (0, 0, 0)), pl.BlockSpec((1, P), lambda i: (0, 0))],
        out_specs=[pl.BlockSpec((T, P), lambda i: (i, 0)), pl.BlockSpec((POOL_GROUPS, gd, gd), lambda i: (0, 0, 0)),
                   pl.BlockSpec((1, P), lambda i: (0, 0))],
        out_shape=[jax.ShapeDtypeStruct((L, P), F32), jax.ShapeDtypeStruct((POOL_GROUPS, gd, gd), F32), jax.ShapeDtypeStruct((1, P), F32)],
        compiler_params=_cparams("arbitrary"),
    )(dap, pooled, pw, scale)


def _pool_bwd_window(dpooled):
    L, P = dpooled.shape
    gd = P // POOL_GROUPS
    T = _tile(L, 256, 8)
    nt = L // T

    def body(dp_ref, dc_ref, dn_ref, du_ref, buf):
        i = pl.program_id(0)
        _fill_halo_buf(buf, dp_ref, dc_ref, dn_ref, i, nt, T)
        t = i * T - HALO + lax.broadcasted_iota(I32, (T + 2 * HALO, 1), 0)
        for g, w in enumerate(POOL_WINDOWS):
            cols = slice(g * gd, (g + 1) * gd)
            buf[:, cols] = buf[:, cols] / _counts(t, w, L)
            acc = buf[pl.ds(HALO - w // 2 + 1, T), cols]
            for o in range(-w // 2 + 2, w // 2 + 1):
                acc = acc + buf[pl.ds(HALO + o, T), cols]
            du_ref[:, cols] = (acc - dc_ref[:, cols]).astype(BF16)

    return pl.pallas_call(
        body, name="pool_bwd_window", grid=(nt,),
        in_specs=_halo_specs(T, L, P, 0),
        out_specs=pl.BlockSpec((T, P), lambda i: (i, 0)),
        out_shape=jax.ShapeDtypeStruct((L, P), BF16),
        scratch_shapes=[pltpu.VMEM((T + 2 * HALO, P), F32)],
        compiler_params=_cparams("parallel"),
    )(dpooled, dpooled, dpooled)


def _sum_rows(name, a):
    R, N = a.shape

    def body(a_ref, o_ref):
        if R <= 16:
            acc = a_ref[0:1, :]
            for r in range(1, R):
                acc = acc + a_ref[r:r + 1, :]
        else:
            acc = jnp.sum(a_ref[...], axis=0, keepdims=True)
        o_ref[...] = acc

    return pl.pallas_call(body, name=name, out_shape=jax.ShapeDtypeStruct((1, N), F32))(a)


def _sum_lanes(name, a):
    def body(a_ref, o_ref):
        o_ref[...] = jnp.sum(a_ref[...], axis=1, keepdims=True)

    return pl.pallas_call(body, name=name, out_shape=jax.ShapeDtypeStruct((a.shape[0], 1), F32))(a)


def _silu_grad_mul(cv, g):
    def body(c_ref, g_ref, o_ref):
        cvv = c_ref[...]
        s = 1.0 / (1.0 + jnp.exp(-cvv))
        o_ref[...] = g_ref[...] * (s * (1.0 + cvv * (1.0 - s)))

    return pl.pallas_call(body, name="silu_grad_mul", out_shape=jax.ShapeDtypeStruct(cv.shape, F32))(cv, g)


def _adamw(name, w, g, m, v):
    R, C = w.shape
    parts = g.ndim == 3
    n_parts = g.shape[0] if parts else 1
    T = _tile(R, max(8, 262144 // C), 8)

    def body(w_ref, g_ref, m_ref, v_ref, go_ref, d_ref, mo_ref, vo_ref):
        if parts:
            gv = g_ref[0].astype(F32)
            for p in range(1, n_parts):
                gv = gv + g_ref[p].astype(F32)
        else:
            gv = g_ref[...]
        mn = ADAM_B1 * m_ref[...] + (1.0 - ADAM_B1) * gv
        vn = ADAM_B2 * v_ref[...] + (1.0 - ADAM_B2) * (gv * gv)
        m_hat = mn / (1.0 - ADAM_B1 ** ADAM_STEP)
        v_hat = vn / (1.0 - ADAM_B2 ** ADAM_STEP)
        go_ref[...] = gv
        d_ref[...] = -ADAM_LR * (m_hat / (jnp.sqrt(v_hat) + ADAM_EPS) + ADAM_WD * w_ref[...])
        mo_ref[...] = mn
        vo_ref[...] = vn

    tile = pl.BlockSpec((T, C), lambda i: (i, 0))
    g_spec = pl.BlockSpec((n_parts, T, C), lambda i: (0, i, 0)) if parts else tile
    return pl.pallas_call(
        body, name=name, grid=(R // T,),
        in_specs=[tile, g_spec, tile, tile], out_specs=[tile] * 4,
        out_shape=[jax.ShapeDtypeStruct((R, C), F32)] * 4,
        compiler_params=_cparams("parallel"),
    )(w, g, m, v)


def _dev_index(px, py, pc):
    return 4 * px + 2 * py + pc


def _all_gather(name, arrs):
    n = len(arrs)

    def body(*refs):
        ins, outs = refs[:n], refs[n:2 * n]
        send_sems, recv_sems, local_sems = refs[2 * n:]
        x, y, c = lax.axis_index("x"), lax.axis_index("y"), lax.axis_index("c")
        me, sibling = (x, y, c), (x, y, 1 - c)
        chips = [(1 - x, y), (x, 1 - y), (1 - x, 1 - y)]

        def copy(a, k, block, to, src=None):
            slot = outs[a].at[_dev_index(*block)]
            return pltpu.make_async_remote_copy(
                src_ref=slot if src is None else src, dst_ref=slot, send_sem=send_sems.at[a, k], recv_sem=recv_sems.at[a, k],
                device_id=to, device_id_type=MESH)

        mine = [pltpu.make_async_copy(ins[a], outs[a].at[_dev_index(*me)], local_sems.at[a]) for a in range(n)]
        for cp in mine:
            cp.start()
        first = []
        for a in range(n):
            first.append(copy(a, 0, me, sibling, src=ins[a]))
            first += [copy(a, 1 + j, me, (*chip, c), src=ins[a]) for j, chip in enumerate(chips)]
        for cp in first:
            cp.start()
        passed = []
        for j, chip in enumerate(chips):
            for a in range(n):
                copy(a, 1 + j, (*chip, c), me).wait_recv()
                fwd = copy(a, 4 + j, (*chip, c), sibling)
                fwd.start()
                passed.append(fwd)
        for a in range(n):
            copy(a, 0, sibling, me).wait_recv()
            for j, chip in enumerate(chips):
                copy(a, 4 + j, (*chip, 1 - c), me).wait_recv()
        for cp in first + passed:
            cp.wait_send()
        for cp in mine:
            cp.wait()

    return pl.pallas_call(
        body, name=name,
        in_specs=[HBM] * n, out_specs=[HBM] * n,
        out_shape=[jax.ShapeDtypeStruct((N_DEV, *a.shape), a.dtype) for a in arrs],
        scratch_shapes=[pltpu.SemaphoreType.DMA((n, N_DEV - 1)), pltpu.SemaphoreType.DMA((n, N_DEV - 1)), pltpu.SemaphoreType.DMA((n,))],
    )(*arrs)


N_COPIES = {"all_to_all": N_DEV - 1, "gather_chips": 4, "forward": 3}


def _exchange_copies(kind, src_ref, land_ref, send_sems, recv_sems, sending):
    x, y, c = lax.axis_index("x"), lax.axis_index("y"), lax.axis_index("c")
    me = _dev_index(x, y, c)
    others = [(1 - x, y), (x, 1 - y), (1 - x, 1 - y)]
    if kind == "all_to_all":
        flips = [(dx, dy, dc) for dx in (0, 1) for dy in (0, 1) for dc in (0, 1)][1:]
        peers = [(1 - x if dx else x, 1 - y if dy else y, 1 - c if dc else c) for dx, dy, dc in flips]
        plan = [(p, src_ref.at[_dev_index(*p)], me if sending else _dev_index(*p)) for p in peers]
    elif kind == "gather_chips":
        peers = [(x, y, 1 - c)] + [(*o, c) for o in others]
        plan = [(p, src_ref, me if sending else _dev_index(*p)) for p in peers]
    else:
        plan = [((x, y, 1 - c), land_ref.at[_dev_index(*o, c)], _dev_index(*o, c if sending else 1 - c)) for o in others]
    return [pltpu.make_async_remote_copy(src_ref=src, dst_ref=land_ref.at[slot], send_sem=send_sems.at[k], recv_sem=recv_sems.at[k],
                                         device_id=peer, device_id_type=MESH)
            for k, (peer, src, slot) in enumerate(plan)]


def _exchange_start(name, kind, srcs, lands=None):
    if lands is None:
        lands = [lax.empty((N_DEV, *s.shape) if kind == "gather_chips" else s.shape, s.dtype) for s in srcs]
    n = len(lands)
    ops = ([] if srcs is None else list(srcs)) + list(lands)
    m = len(ops)

    def body(*refs):
        src_refs = [None] * n if srcs is None else refs[:n]
        land_refs = refs[m - n:m]
        send_sems, recv_sems, token = refs[m:m + n], refs[m + n:m + 2 * n], refs[-1]
        for a in range(n):
            for cp in _exchange_copies(kind, src_refs[a], land_refs[a], send_sems[a], recv_sems[a], True):
                cp.start()
        token[...] = jnp.zeros_like(token)

    sems = [pltpu.SemaphoreType.DMA((N_COPIES[kind],))] * (2 * n)
    outs = pl.pallas_call(
        body, name=name,
        out_shape=sems + [pltpu.HBM(o.shape, o.dtype) for o in ops] + [jax.ShapeDtypeStruct((8, LANES), F32)],
        in_specs=[HBM] * m,
        out_specs=[SEM] * (2 * n) + [HBM] * m + [pl.BlockSpec(memory_space=pltpu.VMEM)],
        input_output_aliases={i: 2 * n + i for i in range(m)},
        compiler_params=SIDE_EFFECT,
    )(*[pltpu.with_memory_space_constraint(o, pltpu.HBM) for o in ops])
    thru = outs[2 * n:2 * n + m]
    return outs[:n], outs[n:2 * n], (None if srcs is None else thru[:n]), thru[m - n:], outs[-1]


def _exchange_wait(name, kind, send_sems, recv_sems, srcs, lands, after):
    n = len(lands)
    ops = ([] if srcs is None else list(srcs)) + list(lands)
    m = len(ops)

    def body(*refs):
        src_refs = [None] * n if srcs is None else refs[:n]
        land_refs = refs[m - n:m]
        send_refs, recv_refs = refs[m:m + n], refs[m + n:m + 2 * n]
        for a in range(n):
            for cp in _exchange_copies(kind, src_refs[a], land_refs[a], send_refs[a], recv_refs[a], False):
                cp.wait_send()
                cp.wait_recv()

    outs = pl.pallas_call(
        body, name=name,
        out_shape=[pltpu.HBM(o.shape, o.dtype) for o in ops],
        in_specs=[HBM] * m + [SEM] * (2 * n) + [pl.BlockSpec(memory_space=pl.ANY)],
        out_specs=[HBM] * m,
        input_output_aliases={i: i for i in range(m)},
        compiler_params=SIDE_EFFECT,
    )(*ops, *send_sems, *recv_sems, after)
    return (None if srcs is None else outs[:n]), outs[m - n:]


def _with_own(land, own, me):
    return lax.dynamic_update_slice_in_dim(land, own, me, 0)


def _shards_to_cols(g):
    return jnp.transpose(g, (1, 0, 2)).reshape(g.shape[1], N_DEV * g.shape[2])


def _cols_to_shards(a):
    R, Ctot = a.shape
    return jnp.transpose(a.reshape(R, N_DEV, Ctot // N_DEV), (1, 0, 2))


def kernel(x, c, ctx, c_ctx, norm_attn_w, norm_mlp_w, w_ada, b_ada, w_in, attn_sink, pool_w, pool_scale, w_out, w_mlp_up, w_mlp_down, final_norm_w, loss_target, m_c_ctx, m_norm_attn_w, m_norm_mlp_w, m_w_ada, m_b_ada, m_w_in, m_attn_sink, m_pool_w, m_pool_scale, m_w_out, m_w_mlp_up, m_w_mlp_down, m_final_norm_w, v_c_ctx, v_norm_attn_w, v_norm_mlp_w, v_w_ada, v_b_ada, v_w_in, v_attn_sink, v_pool_w, v_pool_scale, v_w_out, v_w_mlp_up, v_w_mlp_down, v_final_norm_w):
    _, L, D = x.shape
    H = attn_sink.shape[1]
    A = H * HEAD_DIM
    KV = A // GQA
    P = pool_scale.shape[1]
    MODW = N_MOD * D
    ws = MODW // N_DEV
    gd = P // POOL_GROUPS
    me = _dev_index(lax.axis_index("x"), lax.axis_index("y"), lax.axis_index("c"))

    x2d, ctx2d, tgt = x[0], ctx[0], loss_target[0]
    cctx_row = c_ctx.reshape(1, D)
    wf_row = final_norm_w.reshape(1, D)
    w_ada_l = w_ada[0]
    pool_w_l = pool_w[0].reshape(POOL_GROUPS * (gd // N_DEV), gd)

    win_start = _exchange_start("gather_w_in_start", "gather_chips", [w_in[0].astype(BF16)])

    (c_all,) = _all_gather("gather_cond", [c + win_start[4][0, 0]])
    cond = jnp.concatenate([c_all[:, 0, :], cctx_row, jnp.zeros((COND_ROWS - N_DEV - 1, D), F32)], axis=0)
    b_sh = lax.dynamic_slice_in_dim(b_ada, me * ws, ws, axis=1)
    (mods_sh,) = _mm("ada_mod", cond, w_ada_l, "nn", [F32], SMALL_TILES, a_pre=_silu, extras=[("n", b_sh)], epilogue=lambda acc, b: (acc + b,))
    (mods_g,) = _all_gather("gather_mods", [mods_sh])

    w_srcs = [w_out[0].astype(BF16), pool_w_l.astype(BF16), w_mlp_up[0].astype(BF16), w_mlp_down[0].astype(BF16)]
    w_srcs, mods_g = lax.optimization_barrier((w_srcs, mods_g))
    rest_start = _exchange_start("gather_weights_start", "gather_chips", w_srcs)

    def weights(tag, started, lo, hi, after_chips, after_forward):
        gw_send, gw_recv, gw_src, gw_land, _ = started
        mine, lands = _exchange_wait(f"gather_{tag}_wait", "gather_chips", gw_send[lo:hi], gw_recv[lo:hi], gw_src[lo:hi],
                                     gw_land[lo:hi], after_chips)
        f_send, f_recv, _, f_land, f_token = _exchange_start(f"forward_{tag}_start", "forward", None, lands)
        _, lands = _exchange_wait(f"forward_{tag}_wait", "forward", f_send, f_recv, None, f_land,
                                  f_token if after_forward is None else after_forward)
        return [_with_own(l, s[None], me) for l, s in zip(lands, mine)]

    mods = _shards_to_cols(mods_g)
    mod_b = lax.dynamic_slice_in_dim(mods, me, 1, axis=0)
    sh_a, sc_a, g_a, sh_m, sc_m, g_m = [mod_b[:, i * D:(i + 1) * D] for i in range(N_MOD)]
    csh_a, csc_a = mods[N_DEV:N_DEV + 1, :D], mods[N_DEV:N_DEV + 1, D:2 * D]

    cos, sin = _rope_tables(L)
    h = _norm_fwd("norm_attn", x2d, norm_attn_w, sc_a, sh_a)
    hc = _norm_fwd("norm_attn_ctx", ctx2d, norm_attn_w, csc_a, csh_a)
    (win_g,) = weights("w_in", win_start, 0, 1, h, None)
    W_in = _shards_to_cols(win_g)
    W_qk, W_kv = W_in[:, :A + KV], W_in[:, A:A + 2 * KV]
    W_uv = jnp.concatenate([W_in[:, A + 2 * KV:], W_in[:, A + KV:A + 2 * KV]], axis=1)
    (qk,) = _mm("in_proj_qk", h, W_qk, "nn", [BF16], (1024, A + KV, D), extras=[("m", cos), ("m", sin)],
                epilogue=lambda acc, cs, sn: (_rope(acc, cs, sn),))
    (uv,) = _mm("in_proj_uv", h, W_uv, "nn", [F32], (1024, P + KV, D))
    (kvc,) = _mm("in_proj_ctx", hc, W_kv, "nn", [BF16], SMALL_TILES)
    attn, lse = _attn_fwd(qk, uv, kvc, attn_sink, A, KV, P)
    wout_g, pw_g = weights("w_out", rest_start, 0, 2, qk, attn)
    W_out = wout_g.reshape(A + P, D)
    PW = jnp.transpose(pw_g.reshape(N_DEV, POOL_GROUPS, gd // N_DEV, gd), (1, 0, 2, 3)).reshape(POOL_GROUPS, gd, gd)
    ap, pooled = _pool_fwd(uv, PW, pool_scale, attn)
    o, x1, hm = _mm("out_proj_norm", ap, W_out, "nn", [F32, F32, BF16], (256, D, D), chunk=128, epilogue=_out_proj_epilogue,
                    extras=[("mn", x2d), ("n", g_a), ("n", norm_mlp_w), ("n", sc_m), ("n", sh_m)])
    W_up, wdown_g = weights("w_mlp", rest_start, 2, 4, attn, x1)
    W_down = wdown_g.reshape(-1, D)
    up, act = _mm("mlp_up", hm, W_up, "nn", [F32, BF16], (1024, 1024, 2048), epilogue=lambda acc: (acc, _relu2(acc)), b_shards=True)
    d_x2, d_mlp, d_wf, d_gm, loss_row = _mm(
        "mlp_down_loss", act, W_down, "nn", [F32, BF16], (512, D, 1024), chunk=128, n_sums=3, vmem=FUSED_VMEM_LIMIT,
        epilogue=_mlp_down_epilogue, extras=[("mn", x1), ("mn", tgt), ("n", g_m), ("n", wf_row)])
    loss_p = loss_row[:, :1]

    (d_up,) = _mm("mlp_down_bwd_act", d_mlp, W_down, "nt", [BF16], (1024, 1024, 2048), extras=[("mn", up)], epilogue=lambda acc, uu: (acc * (2.0 * jnp.maximum(uu, 0.0)),))
    (gW_down,) = _mm("mlp_down_bwd_w", act, d_mlp, "tn", [BF16], (1024, 2048, 1024))
    (gW_up_s,) = _mm("mlp_up_bwd_w", hm, d_up, "tn", [BF16], (2048, 1024, 1024), out_shards=True)
    g_mlp_srcs = [gW_up_s, gW_down.reshape(N_DEV, -1, D)]
    g_mlp = _exchange_start("grads_mlp_start", "all_to_all", g_mlp_srcs)
    (d_hm,) = _mm("mlp_up_bwd_act", d_up, W_up, "nt", [F32], (1024, D, 1024), b_shards=True)
    d_x1, s_sh_m, s_sc_m, s_w_nm, d_ga, d_o = _norm_bwd("norm_mlp_bwd", x1, d_hm, d_x2, norm_mlp_w, sc_m,
                                                          jnp.zeros((1, D), F32) + g_mlp[4][0, 0], gate=(o, g_a))

    (d_ap,) = _mm("out_proj_bwd_act", d_o, W_out, "nt", [F32], (1024, 1024, 2048))
    (gW_out,) = _mm("out_proj_bwd_w", ap, d_o, "tn", [BF16], (1024, 2048, 1024))
    d_pooled, gPW, d_pscale = _pool_bwd_mix(d_ap, pooled, PW, pool_scale, A // P)
    gpw_s = jnp.transpose(gPW.astype(BF16).reshape(POOL_GROUPS, N_DEV, gd // N_DEV, gd), (1, 0, 2, 3)).reshape(N_DEV, -1, gd)
    g_mix_srcs = [gW_out.reshape(N_DEV, (A + P) // N_DEV, D), gpw_s]
    g_mix = _exchange_start("grads_mix_start", "all_to_all", g_mix_srcs)
    lse = lse + g_mix[4][0, 0]
    d_u = _pool_bwd_window(d_pooled)
    d_q, rd, dsink_q, d_kvc = _attn_bwd_dq(qk, uv, kvc, attn_sink, d_ap, lse, cos, sin, A, KV, P)
    d_k, d_v = _attn_bwd_dkv(qk, uv, d_ap, lse, rd, cos, sin, A, KV, P)
    d_sink = _sum_lanes("sink_grad", dsink_q).reshape(1, H)
    d_p = jnp.concatenate([d_q, d_k, d_v, d_u], axis=1)
    d_kvc_b = d_kvc.astype(BF16)
    (gW_kv_ctx,) = _mm("in_proj_ctx_bwd_w", hc, d_kvc_b, "tn", [F32], SMALL_TILES)
    (d_hc,) = _mm("in_proj_ctx_bwd_act", d_kvc_b, W_kv, "nt", [F32], SMALL_TILES)
    gW_in_init = jnp.pad(gW_kv_ctx, ((0, 0), (A, P)))
    (gW_in,) = _mm("in_proj_bwd_w", h, d_p, "tn", [BF16], (1024, 1280, 1024), extras=[("mn", gW_in_init)], epilogue=lambda acc, init: (acc + init,))
    g_in_srcs = [_cols_to_shards(gW_in)]
    g_in = _exchange_start("grads_in_start", "all_to_all", g_in_srcs)
    grad_x, s_sh_a, s_sc_a, s_w_na = _mm(
        "in_proj_bwd_norm", d_p, W_in, "nt", [F32], (256, D, A + 2 * KV + P), chunk=128, n_sums=3,
        epilogue=lambda acc, xr, dres, w, sc: _modulated_norm_bwd(xr, acc, dres, w, sc),
        extras=[("mn", x2d), ("mn", d_x1), ("n", norm_attn_w), ("n", sc_a + g_in[4][0, 0])])
    _, s_csh, s_csc, s_w_na = _norm_bwd("norm_attn_ctx_bwd", ctx2d, d_hc, jnp.zeros_like(ctx2d), norm_attn_w, csc_a, s_w_na)

    pad_l = lambda a: jnp.pad(a, ((0, 0), (0, LANES - a.shape[1])))
    d_mod_b = jnp.concatenate([s_sh_a, s_sc_a, d_ga, s_sh_m, s_sc_m, d_gm], axis=1)
    summed = jnp.concatenate([s_csh, s_csc, s_w_na, s_w_nm, d_wf, d_pscale, pad_l(d_sink), pad_l(loss_p)], axis=1)
    (small_g,) = _all_gather("gather_small", [jnp.concatenate([d_mod_b, summed], axis=1)])
    small_g = small_g[:, 0, :]
    tot = _sum_rows("small_sum", small_g[:, MODW:])
    off = [0]
    for wdt in (D, D, D, D, D, P, LANES, LANES):
        off.append(off[-1] + wdt)
    seg = lambda i: tot[:, off[i]:off[i + 1]]
    g_norm_attn, g_norm_mlp, g_final, g_pscale = seg(2), seg(3), seg(4), seg(5)
    g_sink, loss = seg(6)[:, :H], seg(7)[0, 0]
    d_mod_ctx = jnp.concatenate([seg(0), seg(1), jnp.zeros((1, MODW - 2 * D), F32)], axis=1)
    d_mod = jnp.concatenate([small_g[:, :MODW], d_mod_ctx, jnp.zeros((COND_ROWS - N_DEV - 1, MODW), F32)], axis=0)
    g_b_ada = _sum_rows("b_ada_grad", d_mod[:N_DEV + 1])
    d_mod_sh = lax.dynamic_slice_in_dim(d_mod, me * ws, ws, axis=1)
    (g_w_ada,) = _mm("ada_bwd_w", cond, d_mod_sh, "tn", [F32], SMALL_TILES, a_pre=_silu)
    (d_cond_p,) = _mm("ada_bwd_cond", d_mod_sh, w_ada_l, "nt", [F32], SMALL_TILES)
    (d_cctx_g,) = _all_gather("gather_cctx", [d_cond_p[N_DEV:N_DEV + 1]])
    g_c_ctx = _silu_grad_mul(cctx_row, _sum_rows("cctx_sum", d_cctx_g[:, 0, :]))

    def arrived(name, started):
        srcs, lands = _exchange_wait(name, "all_to_all", started[0], started[1], started[2], started[3], g_c_ctx)
        return [_with_own(l, lax.dynamic_index_in_dim(s, me, 0, keepdims=True), me) for l, s in zip(lands, srcs)]

    r_up, r_down = arrived("grads_mlp_wait", g_mlp)
    r_out, r_pw = arrived("grads_mix_wait", g_mix)
    (r_in,) = arrived("grads_in_wait", g_in)

    results = {
        "c_ctx": _adamw("adam_c_ctx", cctx_row, g_c_ctx, m_c_ctx.reshape(1, D), v_c_ctx.reshape(1, D)),
        "norm_attn_w": _adamw("adam_norm_attn", norm_attn_w, g_norm_attn, m_norm_attn_w, v_norm_attn_w),
        "norm_mlp_w": _adamw("adam_norm_mlp", norm_mlp_w, g_norm_mlp, m_norm_mlp_w, v_norm_mlp_w),
        "w_ada": _adamw("adam_w_ada", w_ada_l, g_w_ada, m_w_ada[0], v_w_ada[0]),
        "b_ada": _adamw("adam_b_ada", b_ada, g_b_ada, m_b_ada, v_b_ada),
        "w_in": _adamw("adam_w_in", w_in[0], r_in, m_w_in[0], v_w_in[0]),
        "attn_sink": _adamw("adam_sink", attn_sink, g_sink, m_attn_sink, v_attn_sink),
        "pool_w": _adamw("adam_pool_w", pool_w_l, r_pw, m_pool_w[0].reshape(pool_w_l.shape), v_pool_w[0].reshape(pool_w_l.shape)),
        "pool_scale": _adamw("adam_pool_scale", pool_scale, g_pscale, m_pool_scale, v_pool_scale),
        "w_out": _adamw("adam_w_out", w_out[0], r_out, m_w_out[0], v_w_out[0]),
        "w_mlp_up": _adamw("adam_w_up", w_mlp_up[0], r_up, m_w_mlp_up[0], v_w_mlp_up[0]),
        "w_mlp_down": _adamw("adam_w_down", w_mlp_down[0], r_down, m_w_mlp_down[0], v_w_mlp_down[0]),
        "final_norm_w": _adamw("adam_final_norm", wf_row, g_final, m_final_norm_w.reshape(1, D), v_final_norm_w.reshape(1, D)),
    }
    shapes = {"c_ctx": c_ctx.shape, "norm_attn_w": norm_attn_w.shape, "norm_mlp_w": norm_mlp_w.shape, "w_ada": w_ada.shape,
              "b_ada": b_ada.shape, "w_in": w_in.shape, "attn_sink": attn_sink.shape, "pool_w": pool_w.shape,
              "pool_scale": pool_scale.shape, "w_out": w_out.shape, "w_mlp_up": w_mlp_up.shape, "w_mlp_down": w_mlp_down.shape,
              "final_norm_w": final_norm_w.shape}
    outs = [loss, grad_x.reshape(x.shape)]
    for part in range(4):
        outs += [results[name][part].reshape(shape) for name, shape in shapes.items()]
    return tuple(outs)
```

```python
import functools

import jax
import jax.numpy as jnp
import numpy as np
from jax import lax
from jax.experimental import pallas as pl
from jax.experimental.pallas import tpu as pltpu

F32 = jnp.float32
BF16 = jnp.bfloat16
I32 = jnp.int32

HEAD_DIM = 64
GQA = 4
BLOCK = 128
GRID_W = 64
ROPE_BASE = 10000.0
POOL_WINDOWS = (2, 4, 8, 16)
POOL_GROUPS = len(POOL_WINDOWS)
HALO = 8
N_MOD = 6
EPS = 1e-6
NEG_INF = -1e30
ADAM_LR = 0.001
ADAM_B1 = 0.9
ADAM_B2 = 0.999
ADAM_EPS = 1e-08
ADAM_WD = 0.01
ADAM_STEP = 10
N_DEV = 8
COND_ROWS = 2 * N_DEV
LANES = 128
SUBLANES_16BIT = 16
VMEM_LIMIT = 48 * 1024 * 1024
FUSED_VMEM_LIMIT = 56 * 1024 * 1024
SMALL_TILES = (512, 1024, 512)
MESH = pl.DeviceIdType.MESH
HBM = pl.BlockSpec(memory_space=pltpu.HBM)
SEM = pl.BlockSpec(memory_space=pltpu.SEMAPHORE)
SIDE_EFFECT = pltpu.CompilerParams(has_side_effects=pltpu.SideEffectType.DATAFLOW_SIDE_EFFECTING)


def _cparams(*sem):
    return pltpu.CompilerParams(dimension_semantics=sem, vmem_limit_bytes=VMEM_LIMIT)


def _tile(n, pref, align):
    if n <= pref:
        return n
    t = (pref // align) * align
    while t >= align:
        if n % t == 0:
            return t
        t -= align
    return n


def _dot(a, b):
    return lax.dot_general(a, b, (((1,), (0,)), ((), ())), preferred_element_type=F32)


def _dot_nt(a, b):
    return lax.dot_general(a, b, (((1,), (1,)), ((), ())), preferred_element_type=F32)


def _dot_tn(a, b):
    return lax.dot_general(a, b, (((0,), (0,)), ((), ())), preferred_element_type=F32)


_DOTS = {"nn": _dot, "nt": _dot_nt, "tn": _dot_tn}


def _mm(name, a, b, mode, out_dtypes, tiles, *, epilogue=None, extras=(), a_pre=None, n_sums=0, chunk=None,
        b_shards=False, out_shards=False, vmem=VMEM_LIMIT):
    if mode == "nn":
        M, K = a.shape
        K2, N = (b.shape[1], N_DEV * b.shape[2]) if b_shards else b.shape
    elif mode == "nt":
        M, K = a.shape
        N, K2 = (b.shape[1], N_DEV * b.shape[2]) if b_shards else b.shape
    else:
        (K, M), (K2, N) = a.shape, b.shape
    assert K == K2 and not (b_shards and mode == "tn"), (name, a.shape, b.shape)
    n_span = N // N_DEV if out_shards or (b_shards and mode == "nn") else N
    k_span = K // N_DEV if b_shards and mode == "nt" else K
    tm = _tile(M, tiles[0], LANES if mode == "tn" else SUBLANES_16BIT)
    tn = _tile(n_span, tiles[1], LANES)
    tk = _tile(k_span, tiles[2], SUBLANES_16BIT if mode == "tn" else LANES)
    nk, nb, kb = K // tk, n_span // tn, k_span // tk
    rows = tm if chunk is None else min(chunk, tm)
    n_ex, n_out = len(extras), len(out_dtypes)
    use_acc = nk > 1 or rows < tm
    assert n_sums == 0 or N == tn, name

    def product(a_ref, b_ref):
        at = a_ref[...]
        if a_pre is not None:
            at = a_pre(at)
        return _DOTS[mode](at.astype(BF16), b_ref[...].astype(BF16))

    def apply(acc, ex, out_refs, sl):
        res = (acc,) if epilogue is None else epilogue(acc, *ex)
        for o_ref, o in zip(out_refs, res[:n_out]):
            o_ref[sl, :] = o.astype(o_ref.dtype)
        return tuple(res[n_out:])

    def finish(acc, ex_refs, out_refs, sum_refs):
        if rows == tm:
            acc = acc if not use_acc else acc[...]
            sums = apply(acc, [r[...] for r in ex_refs], out_refs, slice(None))
        else:
            def one(ci, sums):
                sl = pl.ds(pl.multiple_of(ci * rows, rows), rows)
                ex = [r[...] if kind == "n" else r[sl, :] for (kind, _), r in zip(extras, ex_refs)]
                return tuple(s + v for s, v in zip(sums, apply(acc[sl, :], ex, out_refs, sl)))
            sums = lax.fori_loop(0, tm // rows, one, tuple(jnp.zeros((1, tn), F32) for _ in range(n_sums)))
        first = pl.program_id(0) == 0
        for s_ref, sv in zip(sum_refs, sums):
            @pl.when(first)
            def _(s_ref=s_ref, sv=sv):
                s_ref[...] = sv

            @pl.when(jnp.logical_not(first))
            def _(s_ref=s_ref, sv=sv):
                s_ref[...] += sv

    def body(a_ref, b_ref, *rest):
        ex_refs, out_refs = rest[:n_ex], rest[n_ex:n_ex + n_out]
        sum_refs = rest[n_ex + n_out:n_ex + n_out + n_sums]
        if not use_acc:
            finish(product(a_ref, b_ref), ex_refs, out_refs, sum_refs)
            return
        acc_ref = rest[-1]
        k = pl.program_id(2)

        @pl.when(k == 0)
        def _():
            acc_ref[...] = product(a_ref, b_ref)

        @pl.when(k > 0)
        def _():
            acc_ref[...] += product(a_ref, b_ref)

        @pl.when(k == nk - 1)
        def _():
            finish(acc_ref, ex_refs, out_refs, sum_refs)

    a_spec = pl.BlockSpec((tk, tm), lambda i, j, k: (k, i)) if mode == "tn" else pl.BlockSpec((tm, tk), lambda i, j, k: (i, k))
    if not b_shards:
        b_spec = pl.BlockSpec((tn, tk), lambda i, j, k: (j, k)) if mode == "nt" else pl.BlockSpec((tk, tn), lambda i, j, k: (k, j))
    elif mode == "nn":
        b_spec = pl.BlockSpec((None, tk, tn), lambda i, j, k: (j // nb, k, j % nb))
    else:
        b_spec = pl.BlockSpec((None, tn, tk), lambda i, j, k: (k // kb, j, k % kb))
    ex_specs = []
    for kind, arr in extras:
        if kind == "mn":
            ex_specs.append(pl.BlockSpec((tm, tn), lambda i, j, k: (i, j)))
        elif kind == "n":
            ex_specs.append(pl.BlockSpec((1, tn), lambda i, j, k: (0, j)))
        else:
            ex_specs.append(pl.BlockSpec((tm, arr.shape[1]), lambda i, j, k: (i, 0)))
    if out_shards:
        out_specs = [pl.BlockSpec((None, tm, tn), lambda i, j, k: (j // nb, i, j % nb)) for _ in out_dtypes]
        out_shape = [jax.ShapeDtypeStruct((N_DEV, M, n_span), d) for d in out_dtypes]
    else:
        out_specs = [pl.BlockSpec((tm, tn), lambda i, j, k: (i, j)) for _ in out_dtypes]
        out_shape = [jax.ShapeDtypeStruct((M, N), d) for d in out_dtypes]
    out_specs += [pl.BlockSpec((1, tn), lambda i, j, k: (0, 0))] * n_sums
    out_shape += [jax.ShapeDtypeStruct((1, N), F32)] * n_sums
    return pl.pallas_call(
        body,
        name=name,
        grid=(M // tm, N // tn, nk),
        in_specs=[a_spec, b_spec] + ex_specs,
        out_specs=out_specs,
        out_shape=out_shape,
        scratch_shapes=[pltpu.VMEM((tm, tn), F32)] if use_acc else [],
        compiler_params=pltpu.CompilerParams(
            dimension_semantics=("arbitrary",) * 3 if n_sums else ("parallel", "parallel", "arbitrary"), vmem_limit_bytes=vmem),
    )(a, b, *[arr for _, arr in extras])


def _silu(v):
    return v / (1.0 + jnp.exp(-v))


def _relu2(v):
    r = jnp.maximum(v, 0.0)
    return r * r


def _rope_tables(L):
    half = HEAD_DIM // 2
    inv_freq = np.float32(ROPE_BASE) ** (-np.arange(0, half, 2, dtype=np.float32) / np.float32(half))
    t = np.arange(L)
    row, col = t // GRID_W, t % GRID_W
    ang_r = row.astype(np.float32)[:, None] * inv_freq[None, :]
    ang_c = col.astype(np.float32)[:, None] * inv_freq[None, :]
    cos = np.concatenate([np.cos(ang_r), np.cos(ang_r), np.cos(ang_c), np.cos(ang_c)], axis=1)
    sin = np.concatenate([-np.sin(ang_r), np.sin(ang_r), -np.sin(ang_c), np.sin(ang_c)], axis=1)
    reps = LANES // HEAD_DIM
    return jnp.asarray(np.tile(cos, (1, reps)), F32), jnp.asarray(np.tile(sin, (1, reps)), F32)


def _rope(xf, cos, sin):
    quarter = HEAD_DIM // 4
    lane = lax.broadcasted_iota(I32, (xf.shape[0], LANES), 1)
    first = (lane & quarter) == 0
    outs = []
    for j in range(xf.shape[1] // LANES):
        xc = xf[:, j * LANES:(j + 1) * LANES]
        partner = jnp.where(first, pltpu.roll(xc, LANES - quarter, 1), pltpu.roll(xc, quarter, 1))
        outs.append(xc * cos + partner * sin)
    return outs[0] if len(outs) == 1 else jnp.concatenate(outs, axis=1)


def _inv_rms(xf):
    return lax.rsqrt(jnp.mean(xf * xf, axis=-1, keepdims=True) + EPS)


def _modulated_norm(xf, w, sc, sh):
    return ((xf * _inv_rms(xf)) * w) * (1.0 + sc) + sh


def _modulated_norm_bwd(xf, dh, dres, w, sc):
    r = _inv_rms(xf)
    xh = xf * r
    dn = dh * (1.0 + sc)
    dxh = dn * w
    dx = dres + r * (dxh - xh * jnp.mean(dxh * xh, axis=-1, keepdims=True))
    col = lambda v: jnp.sum(v, axis=0, keepdims=True)
    return dx, col(dh), col(dh * (xh * w)), col(dn * xh)


def _norm_fwd(name, x, w, sc, sh):
    L, D = x.shape
    T = _tile(L, 256, 8)

    def body(x_ref, w_ref, sc_ref, sh_ref, h_ref):
        h_ref[...] = _modulated_norm(x_ref[...], w_ref[...], sc_ref[...], sh_ref[...]).astype(BF16)

    row = pl.BlockSpec((1, D), lambda i: (0, 0))
    return pl.pallas_call(
        body, name=name, grid=(L // T,),
        in_specs=[pl.BlockSpec((T, D), lambda i: (i, 0)), row, row, row],
        out_specs=pl.BlockSpec((T, D), lambda i: (i, 0)),
        out_shape=jax.ShapeDtypeStruct((L, D), BF16),
        compiler_params=_cparams("parallel"),
    )(x, w, sc, sh)


def _norm_bwd(name, x, dh, dres, w, sc, w_init, gate=None):
    L, D = x.shape
    T = _tile(L, 256, 8)
    with_gate = gate is not None

    def body(x_ref, dh_ref, dres_ref, w_ref, sc_ref, wi_ref, *rest):
        if with_gate:
            o_ref, g_ref, dx_ref, ssh_ref, ssc_ref, sw_ref, sg_ref, do_ref = rest
        else:
            dx_ref, ssh_ref, ssc_ref, sw_ref = rest
        i = pl.program_id(0)

        @pl.when(i == 0)
        def _():
            ssh_ref[...] = jnp.zeros_like(ssh_ref)
            ssc_ref[...] = jnp.zeros_like(ssc_ref)
            sw_ref[...] = wi_ref[...]
            if with_gate:
                sg_ref[...] = jnp.zeros_like(sg_ref)

        dx, s_sh, s_sc, s_w = _modulated_norm_bwd(x_ref[...], dh_ref[...], dres_ref[...], w_ref[...], sc_ref[...])
        ssh_ref[...] += s_sh
        ssc_ref[...] += s_sc
        sw_ref[...] += s_w
        dx_ref[...] = dx
        if with_gate:
            sg_ref[...] += jnp.sum(dx * o_ref[...], axis=0, keepdims=True)
            do_ref[...] = (g_ref[...] * dx).astype(BF16)

    tile = pl.BlockSpec((T, D), lambda i: (i, 0))
    row = pl.BlockSpec((1, D), lambda i: (0, 0))
    in_specs = [tile, tile, tile, row, row, row]
    out_specs = [tile, row, row, row]
    out_shape = [jax.ShapeDtypeStruct((L, D), F32)] + [jax.ShapeDtypeStruct((1, D), F32)] * 3
    args = [x, dh, dres, w, sc, w_init]
    if with_gate:
        in_specs += [tile, row]
        out_specs += [row, tile]
        out_shape += [jax.ShapeDtypeStruct((1, D), F32), jax.ShapeDtypeStruct((L, D), BF16)]
        args += list(gate)
    return pl.pallas_call(
        body, name=name, grid=(L // T,), in_specs=in_specs, out_specs=out_specs, out_shape=out_shape,
        compiler_params=_cparams("arbitrary"),
    )(*args)


def _out_proj_epilogue(acc, xr, g, w, sc, sh):
    x1 = xr + g * acc
    return acc, x1, _modulated_norm(x1, w, sc, sh)


def _mlp_down_epilogue(acc, x1, tgt, g, wf):
    D = acc.shape[1]
    x2 = x1 + g * acc
    r = _inv_rms(x2)
    xh = x2 * r
    err = xh * wf - tgt
    loss = 0.5 * jnp.sum(jnp.mean(err * err, axis=-1, keepdims=True), axis=0, keepdims=True)
    dy = err * (1.0 / D)
    dxh = dy * wf
    dx = r * (dxh - xh * jnp.mean(dxh * xh, axis=-1, keepdims=True))
    col = lambda v: jnp.sum(v, axis=0, keepdims=True)
    return dx, g * dx, col(dy * xh), col(dx * acc), jnp.broadcast_to(loss, (1, D))


def _heads(ref, first, n):
    return jnp.concatenate([ref[:, (first + g) * HEAD_DIM:(first + g + 1) * HEAD_DIM] for g in range(n)], axis=0)


def _attn_mask(n, L, C):
    shape = (3 * BLOCK + C, GQA * BLOCK)
    kj = lax.broadcasted_iota(I32, shape, 0)
    qi = lax.broadcasted_iota(I32, shape, 1) & (BLOCK - 1)
    kpos = n * BLOCK - BLOCK + kj
    window = (kj >= qi) & (kj <= qi + 2 * BLOCK) & (kpos >= 0) & (kpos < L)
    return window | (kj >= 3 * BLOCK)


def _head_rows(ref, hk):
    return jnp.concatenate([ref[hk * GQA + g:hk * GQA + g + 1, :] for g in range(GQA)], axis=1)


def _rows_to_heads(rows_by_kv_head):
    return jnp.concatenate([r[:, g * BLOCK:(g + 1) * BLOCK] for r in rows_by_kv_head for g in range(GQA)], axis=0)


def _queries_to_rows(t):
    return jnp.concatenate([t[:, g * BLOCK:(g + 1) * BLOCK].T for g in range(GQA)], axis=1)


def _attn_specs(L, A, KV, C, vcol):
    nb = L // BLOCK
    kcol = A // KV
    prev = lambda n: jnp.maximum(n - 1, 0)
    nxt = lambda n: jnp.minimum(n + 1, nb - 1)
    q_spec = pl.BlockSpec((BLOCK, A), lambda n: (n, 0))
    k_specs = [pl.BlockSpec((BLOCK, KV), lambda n: (prev(n), kcol)), pl.BlockSpec((BLOCK, KV), lambda n: (n, kcol)),
               pl.BlockSpec((BLOCK, KV), lambda n: (nxt(n), kcol))]
    v_specs = [pl.BlockSpec((BLOCK, KV), lambda n: (prev(n), vcol)), pl.BlockSpec((BLOCK, KV), lambda n: (n, vcol)),
               pl.BlockSpec((BLOCK, KV), lambda n: (nxt(n), vcol))]
    kvc_spec = pl.BlockSpec((C, 2 * KV), lambda n: (0, 0))
    return q_spec, k_specs, v_specs, kvc_spec


def _keys_values(hk, k_refs, v_refs, kvc_ref, KV):
    sl = slice(hk * HEAD_DIM, (hk + 1) * HEAD_DIM)
    keys = jnp.concatenate([r[:, sl] for r in k_refs] + [kvc_ref[:, sl]], axis=0)
    vals = jnp.concatenate([r[:, sl].astype(BF16) for r in v_refs] + [kvc_ref[:, KV + hk * HEAD_DIM:KV + (hk + 1) * HEAD_DIM]], axis=0)
    return keys, vals


def _sink_row(sink_ref, hk):
    return jnp.concatenate([jnp.full((1, BLOCK), sink_ref[0, hk * GQA + g], F32) for g in range(GQA)], axis=1)


def _attn_fwd(qk, uv, kvc, sink, A, KV, P):
    L = qk.shape[0]
    C = kvc.shape[0]
    nkv = KV // HEAD_DIM
    H = nkv * GQA
    scale = HEAD_DIM ** -0.5

    def body(sink_ref, q_ref, kp_ref, kc_ref, kn_ref, vp_ref, vc_ref, vn_ref, kvc_ref, o_ref, lse_ref):
        valid = _attn_mask(pl.program_id(0), L, C)
        lse_rows = []
        for hk in range(nkv):
            keys, vals = _keys_values(hk, (kp_ref, kc_ref, kn_ref), (vp_ref, vc_ref, vn_ref), kvc_ref, KV)
            qs = _heads(q_ref, hk * GQA, GQA) * scale
            s = jnp.where(valid, _dot_nt(keys, qs), NEG_INF)
            sk = _sink_row(sink_ref, hk)
            m = jnp.maximum(jnp.max(s, axis=0, keepdims=True), sk)
            p = jnp.exp(s - m)
            den = jnp.sum(p, axis=0, keepdims=True) + jnp.exp(sk - m)
            o = _dot_tn(vals, p.astype(BF16)) * (1.0 / den)
            lse_rows.append(m + jnp.log(den))
            o_ref[:, hk * GQA * HEAD_DIM:(hk + 1) * GQA * HEAD_DIM] = _queries_to_rows(o).astype(BF16)
        lse_ref[...] = _rows_to_heads(lse_rows)

    q_spec, k_specs, v_specs, kvc_spec = _attn_specs(L, A, KV, C, P // KV)
    return pl.pallas_call(
        body, name="attn_fwd", grid=(L // BLOCK,),
        in_specs=[pl.BlockSpec(memory_space=pltpu.SMEM), q_spec] + k_specs + v_specs + [kvc_spec],
        out_specs=[pl.BlockSpec((BLOCK, A), lambda n: (n, 0)), pl.BlockSpec((H, BLOCK), lambda n: (0, n))],
        out_shape=[jax.ShapeDtypeStruct((L, A + P), BF16), jax.ShapeDtypeStruct((H, L), F32)],
        compiler_params=_cparams("parallel"),
    )(sink, qk, qk, qk, qk, uv, uv, uv, kvc)


def _attn_bwd_dq(qk, uv, kvc, sink, dap, lse_t, cos, sin, A, KV, P):
    L = qk.shape[0]
    C = kvc.shape[0]
    nkv = KV // HEAD_DIM
    H = nkv * GQA
    scale = HEAD_DIM ** -0.5
    W = 3 * BLOCK

    def body(sink_ref, q_ref, kp_ref, kc_ref, kn_ref, vp_ref, vc_ref, vn_ref, kvc_ref, do_ref, lse_ref, cos_ref, sin_ref,
             dq_ref, rd_ref, ds_ref, dkvc_ref):
        n = pl.program_id(0)

        @pl.when(n == 0)
        def _():
            dkvc_ref[...] = jnp.zeros_like(dkvc_ref)

        valid = _attn_mask(n, L, C)
        rd_rows, dsink_rows, dq_parts = [], [], []
        for hk in range(nkv):
            sl = slice(hk * HEAD_DIM, (hk + 1) * HEAD_DIM)
            keys, vals = _keys_values(hk, (kp_ref, kc_ref, kn_ref), (vp_ref, vc_ref, vn_ref), kvc_ref, KV)
            qs = _heads(q_ref, hk * GQA, GQA) * scale
            dos = _heads(do_ref, hk * GQA, GQA).astype(BF16)
            lse = _head_rows(lse_ref, hk)
            p = jnp.exp(jnp.where(valid, _dot_nt(keys, qs), NEG_INF) - lse)
            dp = _dot_nt(vals, dos)
            rd = jnp.sum(p * dp, axis=0, keepdims=True)
            ds = (p * (dp - rd)).astype(BF16)
            dq_parts.append(_queries_to_rows(_dot_tn(keys, ds) * scale))
            dkvc_ref[:, sl] += _dot(ds[W:, :], qs)
            dkvc_ref[:, KV + hk * HEAD_DIM:KV + (hk + 1) * HEAD_DIM] += _dot(p[W:, :].astype(BF16), dos)
            rd_rows.append(rd)
            dsink_rows.append(-(jnp.exp(_sink_row(sink_ref, hk) - lse) * rd))
        rd_ref[...] = _rows_to_heads(rd_rows)
        ds_ref[...] = _rows_to_heads(dsink_rows)
        dq = dq_parts[0] if nkv == 1 else jnp.concatenate(dq_parts, axis=1)
        dq_ref[...] = _rope(dq, cos_ref[...], -sin_ref[...]).astype(BF16)

    q_spec, k_specs, v_specs, kvc_spec = _attn_specs(L, A, KV, C, P // KV)
    blk = lambda w: pl.BlockSpec((BLOCK, w), lambda n: (n, 0))
    per_head = pl.BlockSpec((H, BLOCK), lambda n: (0, n))
    return pl.pallas_call(
        body, name="attn_bwd_dq", grid=(L // BLOCK,),
        in_specs=[pl.BlockSpec(memory_space=pltpu.SMEM), q_spec] + k_specs + v_specs + [kvc_spec, blk(A), per_head, blk(LANES), blk(LANES)],
        out_specs=[blk(A), per_head, per_head, pl.BlockSpec((C, 2 * KV), lambda n: (0, 0))],
        out_shape=[jax.ShapeDtypeStruct((L, A), BF16), jax.ShapeDtypeStruct((H, L), F32), jax.ShapeDtypeStruct((H, L), F32),
                   jax.ShapeDtypeStruct((C, 2 * KV), F32)],
        compiler_params=_cparams("arbitrary"),
    )(sink, qk, qk, qk, qk, uv, uv, uv, kvc, dap, lse_t, cos, sin)


def _attn_bwd_dkv(qk, uv, dap, lse_t, rd_t, cos, sin, A, KV, P):
    L = qk.shape[0]
    nb = L // BLOCK
    nkv = KV // HEAD_DIM
    H = nkv * GQA
    scale = HEAD_DIM ** -0.5
    R = 3 * GQA * BLOCK

    def body(k_ref, v_ref, qp_ref, qc_ref, qn_ref, dop_ref, doc_ref, don_ref, lsep_ref, lsec_ref, lsen_ref,
             rdp_ref, rdc_ref, rdn_ref, cos_ref, sin_ref, dk_ref, dv_ref):
        m = pl.program_id(0)
        kj = lax.broadcasted_iota(I32, (BLOCK, R), 0)
        col = lax.broadcasted_iota(I32, (BLOCK, R), 1)
        part = col // (GQA * BLOCK)
        qi = col & (BLOCK - 1)
        before = jnp.where(m >= 1, 0, -2 * BLOCK)
        after = jnp.where(m <= nb - 2, 0, 2 * BLOCK)
        valid = ((part == 0) & (kj <= qi + before)) | (part == 1) | ((part == 2) & (kj >= qi + after))
        dk_parts, dv_parts = [], []
        for hk in range(nkv):
            sl = slice(hk * HEAD_DIM, (hk + 1) * HEAD_DIM)
            km = k_ref[:, sl]
            vm = v_ref[:, sl].astype(BF16)
            qs = jnp.concatenate([_heads(q, hk * GQA, GQA) for q in (qp_ref, qc_ref, qn_ref)], axis=0) * scale
            dos = jnp.concatenate([_heads(d, hk * GQA, GQA) for d in (dop_ref, doc_ref, don_ref)], axis=0).astype(BF16)
            rows = [slice(hk * GQA + g, hk * GQA + g + 1) for g in range(GQA)]
            lse = jnp.concatenate([t[r, :] for t in (lsep_ref, lsec_ref, lsen_ref) for r in rows], axis=1)
            rdv = jnp.concatenate([t[r, :] for t in (rdp_ref, rdc_ref, rdn_ref) for r in rows], axis=1)
            p = jnp.exp(jnp.where(valid, _dot_nt(km, qs), NEG_INF) - lse)
            ds = (p * (_dot_nt(vm, dos) - rdv)).astype(BF16)
            dk_parts.append(_dot(ds, qs))
            dv_parts.append(_dot(p.astype(BF16), dos))
        dk = dk_parts[0] if nkv == 1 else jnp.concatenate(dk_parts, axis=1)
        dv = dv_parts[0] if nkv == 1 else jnp.concatenate(dv_parts, axis=1)
        dk_ref[...] = _rope(dk, cos_ref[...], -sin_ref[...]).astype(BF16)
        dv_ref[...] = dv.astype(BF16)

    prev = lambda m: jnp.maximum(m - 1, 0)
    nxt = lambda m: jnp.minimum(m + 1, nb - 1)
    three = lambda w: [pl.BlockSpec((BLOCK, w), lambda m: (prev(m), 0)), pl.BlockSpec((BLOCK, w), lambda m: (m, 0)),
                       pl.BlockSpec((BLOCK, w), lambda m: (nxt(m), 0))]
    three_t = [pl.BlockSpec((H, BLOCK), lambda m: (0, prev(m))), pl.BlockSpec((H, BLOCK), lambda m: (0, m)),
               pl.BlockSpec((H, BLOCK), lambda m: (0, nxt(m)))]
    blk = lambda w: pl.BlockSpec((BLOCK, w), lambda m: (m, 0))
    return pl.pallas_call(
        body, name="attn_bwd_dkv", grid=(nb,),
        in_specs=[pl.BlockSpec((BLOCK, KV), lambda m: (m, A // KV)), pl.BlockSpec((BLOCK, KV), lambda m: (m, P // KV))]
                 + three(A) + three(A) + three_t + three_t + [blk(LANES), blk(LANES)],
        out_specs=[blk(KV), blk(KV)],
        out_shape=[jax.ShapeDtypeStruct((L, KV), BF16), jax.ShapeDtypeStruct((L, KV), BF16)],
        compiler_params=_cparams("parallel"),
    )(qk, uv, qk, qk, qk, dap, dap, dap, lse_t, lse_t, lse_t, rd_t, rd_t, rd_t, cos, sin)


def _halo_specs(T, L, W, col):
    per = T // HALO
    return [pl.BlockSpec((HALO, W), lambda i: (jnp.maximum(i * per - 1, 0), col)),
            pl.BlockSpec((T, W), lambda i: (i, col)),
            pl.BlockSpec((HALO, W), lambda i: (jnp.minimum((i + 1) * per, L // HALO - 1), col))]


def _fill_halo_buf(buf, prev_ref, cur_ref, next_ref, i, nt, T):
    buf[0:HALO, :] = jnp.where(i > 0, prev_ref[...], 0.0)
    buf[HALO:HALO + T, :] = cur_ref[...]
    buf[HALO + T:2 * HALO + T, :] = jnp.where(i < nt - 1, next_ref[...], 0.0)


def _zero_margins(lv):
    rows = lv.shape[0]
    lv[0:HALO, :] = jnp.zeros((HALO, lv.shape[1]), F32)
    lv[rows - HALO:rows, :] = jnp.zeros((HALO, lv.shape[1]), F32)


def _window_sums(lv, x, w, first):
    n = x.shape[0]
    lv[HALO:HALO + n, :] = x
    cur = x + lv[pl.ds(HALO + first, n), :]
    span = 1
    while 2 * span < w:
        lv[HALO:HALO + n, :] = cur
        cur = lv[pl.ds(HALO - span, n), :] + lv[pl.ds(HALO + span, n), :]
        span *= 2
    return cur


def _counts(t, w, L):
    lo = jnp.clip(t - w // 2, 0, L)
    hi = jnp.clip(t - w // 2 + w, 0, L)
    return jnp.maximum(hi - lo, 1).astype(F32)


def _pool_fwd(u, pw, scale, mix):
    L, P = u.shape[0], scale.shape[1]
    gd = P // POOL_GROUPS
    T = _tile(L, 256, 8)
    nt = L // T
    assert (mix.shape[1] - P) % P == 0
    mix_col = mix.shape[1] // P - 1

    def body(up_ref, uc_ref, un_ref, pw_ref, sc_ref, mix_ref, out_ref, pooled_ref, buf, lv):
        i = pl.program_id(0)
        _fill_halo_buf(buf, up_ref, uc_ref, un_ref, i, nt, T)
        _zero_margins(lv)
        t = i * T + lax.broadcasted_iota(I32, (T, 1), 0)
        for g, w in enumerate(POOL_WINDOWS):
            cols = slice(g * gd, (g + 1) * gd)
            acc = _window_sums(lv, buf[:, cols], w, -1)[HALO:HALO + T]
            pooled = (acc / _counts(t, w, L) - uc_ref[:, cols]).astype(BF16)
            pooled_ref[:, cols] = pooled
            out_ref[:, cols] = (_dot(pooled, pw_ref[g]) * sc_ref[:, cols]).astype(BF16)

    return pl.pallas_call(
        body, name="pool_fwd", grid=(nt,),
        in_specs=_halo_specs(T, L, P, 0) + [pl.BlockSpec((POOL_GROUPS, gd, gd), lambda i: (0, 0, 0)), pl.BlockSpec((1, P), lambda i: (0, 0)),
                                            pl.BlockSpec(memory_space=pl.ANY)],
        out_specs=[pl.BlockSpec((T, P), lambda i: (i, mix_col)), pl.BlockSpec((T, P), lambda i: (i, 0))],
        out_shape=[jax.ShapeDtypeStruct(mix.shape, BF16), jax.ShapeDtypeStruct((L, P), BF16)],
        scratch_shapes=[pltpu.VMEM((T + 2 * HALO, P), F32), pltpu.VMEM((T + 4 * HALO, gd), F32)],
        input_output_aliases={5: 0},
        compiler_params=_cparams("parallel"),
    )(u, u, u, pw, scale, mix)


def _pool_bwd_mix(dap, pooled, pw, scale, pcol):
    L, P = pooled.shape
    gd = P // POOL_GROUPS
    T = _tile(L, 256, 8)

    def body(dp_ref, pooled_ref, pw_ref, sc_ref, dpooled_ref, dpw_ref, dsc_ref):
        i = pl.program_id(0)

        @pl.when(i == 0)
        def _():
            dpw_ref[...] = jnp.zeros_like(dpw_ref)
            dsc_ref[...] = jnp.zeros_like(dsc_ref)

        for g in range(POOL_GROUPS):
            cols = slice(g * gd, (g + 1) * gd)
            pb = pooled_ref[:, cols]
            dp = dp_ref[:, cols]
            dsc_ref[:, cols] += jnp.sum(dp * _dot(pb, pw_ref[g]), axis=0, keepdims=True)
            dm = (dp * sc_ref[:, cols]).astype(BF16)
            dpw_ref[g] += _dot_tn(pb, dm)
            dpooled_ref[:, cols] = _dot_nt(dm, pw_ref[g])

    return pl.pallas_call(
        body, name="pool_bwd_mix", grid=(L // T,),
        in_specs=[pl.BlockSpec((T, P), lambda i: (i, pcol)), pl.BlockSpec((T, P), lambda i: (i, 0)),
                  pl.BlockSpec((POOL_GROUPS, gd, gd), lambda i: (0, 0, 0)), pl.BlockSpec((1, P), lambda i: (0, 0))],
        out_specs=[pl.BlockSpec((T, P), lambda i: (i, 0)), pl.BlockSpec((POOL_GROUPS, gd, gd), lambda i: (0, 0, 0)),
                   pl.BlockSpec((1, P), lambda i: (0, 0))],
        out_shape=[jax.ShapeDtypeStruct((L, P), F32), jax.ShapeDtypeStruct((POOL_GROUPS, gd, gd), F32), jax.ShapeDtypeStruct((1, P), F32)],
        compiler_params=_cparams("arbitrary"),
    )(dap, pooled, pw, scale)


def _pool_bwd_window(dpooled):
    L, P = dpooled.shape
    gd = P // POOL_GROUPS
    T = _tile(L, 256, 8)
    nt = L // T

    def body(dp_ref, dc_ref, dn_ref, du_ref, buf, lv):
        i = pl.program_id(0)
        _fill_halo_buf(buf, dp_ref, dc_ref, dn_ref, i, nt, T)
        _zero_margins(lv)
        t = i * T - HALO + lax.broadcasted_iota(I32, (T + 2 * HALO, 1), 0)
        for g, w in enumerate(POOL_WINDOWS):
            cols = slice(g * gd, (g + 1) * gd)
            acc = _window_sums(lv, buf[:, cols] / _counts(t, w, L), w, 1)[HALO:HALO + T]
            du_ref[:, cols] = (acc - dc_ref[:, cols]).astype(BF16)

    return pl.pallas_call(
        body, name="pool_bwd_window", grid=(nt,),
        in_specs=_halo_specs(T, L, P, 0),
        out_specs=pl.BlockSpec((T, P), lambda i: (i, 0)),
        out_shape=jax.ShapeDtypeStruct((L, P), BF16),
        scratch_shapes=[pltpu.VMEM((T + 2 * HALO, P), F32), pltpu.VMEM((T + 4 * HALO, gd), F32)],
        compiler_params=_cparams("parallel"),
    )(dpooled, dpooled, dpooled)


def _sum_rows(name, a):
    R, N = a.shape

    def body(a_ref, o_ref):
        if R <= 16:
            acc = a_ref[0:1, :]
            for r in range(1, R):
                acc = acc + a_ref[r:r + 1, :]
        else:
            acc = jnp.sum(a_ref[...], axis=0, keepdims=True)
        o_ref[...] = acc

    return pl.pallas_call(body, name=name, out_shape=jax.ShapeDtypeStruct((1, N), F32))(a)


def _sum_lanes(name, a):
    def body(a_ref, o_ref):
        o_ref[...] = jnp.sum(a_ref[...], axis=1, keepdims=True)

    return pl.pallas_call(body, name=name, out_shape=jax.ShapeDtypeStruct((a.shape[0], 1), F32))(a)


def _silu_grad_mul(cv, g):
    def body(c_ref, g_ref, o_ref):
        cvv = c_ref[...]
        s = 1.0 / (1.0 + jnp.exp(-cvv))
        o_ref[...] = g_ref[...] * (s * (1.0 + cvv * (1.0 - s)))

    return pl.pallas_call(body, name="silu_grad_mul", out_shape=jax.ShapeDtypeStruct(cv.shape, F32))(cv, g)


def _adamw(name, w, g, m, v):
    R, C = w.shape
    parts = g.ndim == 3
    n_parts = g.shape[0] if parts else 1
    T = _tile(R, max(8, 262144 // C), 8)

    def body(w_ref, g_ref, m_ref, v_ref, go_ref, d_ref, mo_ref, vo_ref):
        if parts:
            gv = g_ref[0].astype(F32)
            for p in range(1, n_parts):
                gv = gv + g_ref[p].astype(F32)
        else:
            gv = g_ref[...]
        mn = ADAM_B1 * m_ref[...] + (1.0 - ADAM_B1) * gv
        vn = ADAM_B2 * v_ref[...] + (1.0 - ADAM_B2) * (gv * gv)
        m_hat = mn / (1.0 - ADAM_B1 ** ADAM_STEP)
        v_hat = vn / (1.0 - ADAM_B2 ** ADAM_STEP)
        go_ref[...] = gv
        d_ref[...] = -ADAM_LR * (m_hat / (jnp.sqrt(v_hat) + ADAM_EPS) + ADAM_WD * w_ref[...])
        mo_ref[...] = mn
        vo_ref[...] = vn

    tile = pl.BlockSpec((T, C), lambda i: (i, 0))
    g_spec = pl.BlockSpec((n_parts, T, C), lambda i: (0, i, 0)) if parts else tile
    return pl.pallas_call(
        body, name=name, grid=(R // T,),
        in_specs=[tile, g_spec, tile, tile], out_specs=[tile] * 4,
        out_shape=[jax.ShapeDtypeStruct((R, C), F32)] * 4,
        compiler_params=_cparams("parallel"),
    )(w, g, m, v)


def _dev_index(px, py, pc):
    return 4 * px + 2 * py + pc


def _all_gather(name, arrs):
    n = len(arrs)

    def body(*refs):
        ins, outs = refs[:n], refs[n:2 * n]
        send_sems, recv_sems, local_sems = refs[2 * n:]
        x, y, c = lax.axis_index("x"), lax.axis_index("y"), lax.axis_index("c")
        me, sibling = (x, y, c), (x, y, 1 - c)
        chips = [(1 - x, y), (x, 1 - y), (1 - x, 1 - y)]

        def copy(a, k, block, to, src=None):
            slot = outs[a].at[_dev_index(*block)]
            return pltpu.make_async_remote_copy(
                src_ref=slot if src is None else src, dst_ref=slot, send_sem=send_sems.at[a, k], recv_sem=recv_sems.at[a, k],
                device_id=to, device_id_type=MESH)

        mine = [pltpu.make_async_copy(ins[a], outs[a].at[_dev_index(*me)], local_sems.at[a]) for a in range(n)]
        for cp in mine:
            cp.start()
        first = []
        for a in range(n):
            first.append(copy(a, 0, me, sibling, src=ins[a]))
            first += [copy(a, 1 + j, me, (*chip, c), src=ins[a]) for j, chip in enumerate(chips)]
        for cp in first:
            cp.start()
        passed = []
        for j, chip in enumerate(chips):
            for a in range(n):
                copy(a, 1 + j, (*chip, c), me).wait_recv()
                fwd = copy(a, 4 + j, (*chip, c), sibling)
                fwd.start()
                passed.append(fwd)
        for a in range(n):
            copy(a, 0, sibling, me).wait_recv()
            for j, chip in enumerate(chips):
                copy(a, 4 + j, (*chip, 1 - c), me).wait_recv()
        for cp in first + passed:
            cp.wait_send()
        for cp in mine:
            cp.wait()

    return pl.pallas_call(
        body, name=name,
        in_specs=[HBM] * n, out_specs=[HBM] * n,
        out_shape=[jax.ShapeDtypeStruct((N_DEV, *a.shape), a.dtype) for a in arrs],
        scratch_shapes=[pltpu.SemaphoreType.DMA((n, N_DEV - 1)), pltpu.SemaphoreType.DMA((n, N_DEV - 1)), pltpu.SemaphoreType.DMA((n,))],
    )(*arrs)


N_COPIES = {"all_to_all": N_DEV - 1, "gather_chips": 4, "forward": 3}


def _exchange_copies(kind, src_ref, land_ref, send_sems, recv_sems, sending):
    x, y, c = lax.axis_index("x"), lax.axis_index("y"), lax.axis_index("c")
    me = _dev_index(x, y, c)
    others = [(1 - x, y), (x, 1 - y), (1 - x, 1 - y)]
    if kind == "all_to_all":
        flips = [(dx, dy, dc) for dx in (0, 1) for dy in (0, 1) for dc in (0, 1)][1:]
        peers = [(1 - x if dx else x, 1 - y if dy else y, 1 - c if dc else c) for dx, dy, dc in flips]
        plan = [(p, src_ref.at[_dev_index(*p)], me if sending else _dev_index(*p)) for p in peers]
    elif kind == "gather_chips":
        peers = [(x, y, 1 - c)] + [(*o, c) for o in others]
        plan = [(p, src_ref, me if sending else _dev_index(*p)) for p in peers]
    else:
        plan = [((x, y, 1 - c), land_ref.at[_dev_index(*o, c)], _dev_index(*o, c if sending else 1 - c)) for o in others]
    return [pltpu.make_async_remote_copy(src_ref=src, dst_ref=land_ref.at[slot], send_sem=send_sems.at[k], recv_sem=recv_sems.at[k],
                                         device_id=peer, device_id_type=MESH)
            for k, (peer, src, slot) in enumerate(plan)]


def _exchange_start(name, kind, srcs, lands=None):
    if lands is None:
        lands = [lax.empty((N_DEV, *s.shape) if kind == "gather_chips" else s.shape, s.dtype) for s in srcs]
    n = len(lands)
    ops = ([] if srcs is None else list(srcs)) + list(lands)
    m = len(ops)

    def body(*refs):
        src_refs = [None] * n if srcs is None else refs[:n]
        land_refs = refs[m - n:m]
        send_sems, recv_sems, token = refs[m:m + n], refs[m + n:m + 2 * n], refs[-1]
        for a in range(n):
            for cp in _exchange_copies(kind, src_refs[a], land_refs[a], send_sems[a], recv_sems[a], True):
                cp.start()
        token[...] = jnp.zeros_like(token)

    sems = [pltpu.SemaphoreType.DMA((N_COPIES[kind],))] * (2 * n)
    outs = pl.pallas_call(
        body, name=name,
        out_shape=sems + [pltpu.HBM(o.shape, o.dtype) for o in ops] + [jax.ShapeDtypeStruct((8, LANES), F32)],
        in_specs=[HBM] * m,
        out_specs=[SEM] * (2 * n) + [HBM] * m + [pl.BlockSpec(memory_space=pltpu.VMEM)],
        input_output_aliases={i: 2 * n + i for i in range(m)},
        compiler_params=SIDE_EFFECT,
    )(*[pltpu.with_memory_space_constraint(o, pltpu.HBM) for o in ops])
    thru = outs[2 * n:2 * n + m]
    return outs[:n], outs[n:2 * n], (None if srcs is None else thru[:n]), thru[m - n:], outs[-1]


def _exchange_wait(name, kind, send_sems, recv_sems, srcs, lands, after):
    n = len(lands)
    ops = ([] if srcs is None else list(srcs)) + list(lands)
    m = len(ops)

    def body(*refs):
        src_refs = [None] * n if srcs is None else refs[:n]
        land_refs = refs[m - n:m]
        send_refs, recv_refs = refs[m:m + n], refs[m + n:m + 2 * n]
        for a in range(n):
            for cp in _exchange_copies(kind, src_refs[a], land_refs[a], send_refs[a], recv_refs[a], False):
                cp.wait_send()
                cp.wait_recv()

    outs = pl.pallas_call(
        body, name=name,
        out_shape=[pltpu.HBM(o.shape, o.dtype) for o in ops],
        in_specs=[HBM] * m + [SEM] * (2 * n) + [pl.BlockSpec(memory_space=pl.ANY)],
        out_specs=[HBM] * m,
        input_output_aliases={i: i for i in range(m)},
        compiler_params=SIDE_EFFECT,
    )(*ops, *send_sems, *recv_sems, after)
    return (None if srcs is None else outs[:n]), outs[m - n:]


def _with_own(land, own, me):
    return lax.dynamic_update_slice_in_dim(land, own, me, 0)


def _shards_to_cols(g):
    return jnp.transpose(g, (1, 0, 2)).reshape(g.shape[1], N_DEV * g.shape[2])


def _cols_to_shards(a):
    R, Ctot = a.shape
    return jnp.transpose(a.reshape(R, N_DEV, Ctot // N_DEV), (1, 0, 2))


def kernel(x, c, ctx, c_ctx, norm_attn_w, norm_mlp_w, w_ada, b_ada, w_in, attn_sink, pool_w, pool_scale, w_out, w_mlp_up, w_mlp_down, final_norm_w, loss_target, m_c_ctx, m_norm_attn_w, m_norm_mlp_w, m_w_ada, m_b_ada, m_w_in, m_attn_sink, m_pool_w, m_pool_scale, m_w_out, m_w_mlp_up, m_w_mlp_down, m_final_norm_w, v_c_ctx, v_norm_attn_w, v_norm_mlp_w, v_w_ada, v_b_ada, v_w_in, v_attn_sink, v_pool_w, v_pool_scale, v_w_out, v_w_mlp_up, v_w_mlp_down, v_final_norm_w):
    _, L, D = x.shape
    H = attn_sink.shape[1]
    A = H * HEAD_DIM
    KV = A // GQA
    P = pool_scale.shape[1]
    MODW = N_MOD * D
    ws = MODW // N_DEV
    gd = P // POOL_GROUPS
    me = _dev_index(lax.axis_index("x"), lax.axis_index("y"), lax.axis_index("c"))

    x2d, ctx2d, tgt = x[0], ctx[0], loss_target[0]
    cctx_row = c_ctx.reshape(1, D)
    wf_row = final_norm_w.reshape(1, D)
    w_ada_l = w_ada[0]
    pool_w_l = pool_w[0].reshape(POOL_GROUPS * (gd // N_DEV), gd)

    (c_all,) = _all_gather("gather_cond", [c])
    cond = jnp.concatenate([c_all[:, 0, :], cctx_row, jnp.zeros((COND_ROWS - N_DEV - 1, D), F32)], axis=0)
    b_sh = lax.dynamic_slice_in_dim(b_ada, me * ws, ws, axis=1)
    (mods_sh,) = _mm("ada_mod", cond, w_ada_l, "nn", [F32], SMALL_TILES, a_pre=_silu, extras=[("n", b_sh)], epilogue=lambda acc, b: (acc + b,))
    (mods_g,) = _all_gather("gather_mods", [mods_sh])

    w_srcs = [w_in[0].astype(BF16), w_out[0].astype(BF16), pool_w_l.astype(BF16), w_mlp_up[0].astype(BF16), w_mlp_down[0].astype(BF16)]
    w_srcs, mods_g = lax.optimization_barrier((w_srcs, mods_g))
    gather_start = _exchange_start("gather_weights_start", "gather_chips", w_srcs)

    def weights(tag, started, lo, hi, after_chips, after_forward):
        gw_send, gw_recv, gw_src, gw_land, _ = started
        mine, lands = _exchange_wait(f"gather_{tag}_wait", "gather_chips", gw_send[lo:hi], gw_recv[lo:hi], gw_src[lo:hi],
                                     gw_land[lo:hi], after_chips)
        f_send, f_recv, _, f_land, f_token = _exchange_start(f"forward_{tag}_start", "forward", None, lands)
        _, lands = _exchange_wait(f"forward_{tag}_wait", "forward", f_send, f_recv, None, f_land,
                                  f_token if after_forward is None else after_forward)
        return [_with_own(l, s[None], me) for l, s in zip(lands, mine)]

    mods = _shards_to_cols(mods_g)
    mod_b = lax.dynamic_slice_in_dim(mods, me, 1, axis=0)
    sh_a, sc_a, g_a, sh_m, sc_m, g_m = [mod_b[:, i * D:(i + 1) * D] for i in range(N_MOD)]
    csh_a, csc_a = mods[N_DEV:N_DEV + 1, :D], mods[N_DEV:N_DEV + 1, D:2 * D]

    cos, sin = _rope_tables(L)
    h = _norm_fwd("norm_attn", x2d, norm_attn_w, sc_a, sh_a)
    hc = _norm_fwd("norm_attn_ctx", ctx2d, norm_attn_w, csc_a, csh_a)
    (win_g,) = weights("w_in", gather_start, 0, 1, h, None)
    W_in = _shards_to_cols(win_g)
    W_qk, W_kv = W_in[:, :A + KV], W_in[:, A:A + 2 * KV]
    W_uv = jnp.concatenate([W_in[:, A + 2 * KV:], W_in[:, A + KV:A + 2 * KV]], axis=1)
    (qk,) = _mm("in_proj_qk", h, W_qk, "nn", [BF16], (1024, A + KV, D), extras=[("m", cos), ("m", sin)],
                epilogue=lambda acc, cs, sn: (_rope(acc, cs, sn),))
    (uv,) = _mm("in_proj_uv", h, W_uv, "nn", [F32], (1024, P + KV, D))
    (kvc,) = _mm("in_proj_ctx", hc, W_kv, "nn", [BF16], SMALL_TILES)
    attn, lse = _attn_fwd(qk, uv, kvc, attn_sink, A, KV, P)
    wout_g, pw_g = weights("w_out", gather_start, 1, 3, qk, attn)
    W_out = wout_g.reshape(A + P, D)
    PW = jnp.transpose(pw_g.reshape(N_DEV, POOL_GROUPS, gd // N_DEV, gd), (1, 0, 2, 3)).reshape(POOL_GROUPS, gd, gd)
    ap, pooled = _pool_fwd(uv, PW, pool_scale, attn)
    o, x1, hm = _mm("out_proj_norm", ap, W_out, "nn", [F32, F32, BF16], (256, D, D), chunk=128, epilogue=_out_proj_epilogue,
                    extras=[("mn", x2d), ("n", g_a), ("n", norm_mlp_w), ("n", sc_m), ("n", sh_m)])
    W_up, wdown_g = weights("w_mlp", gather_start, 3, 5, attn, x1)
    W_down = wdown_g.reshape(-1, D)
    up, act = _mm("mlp_up", hm, W_up, "nn", [F32, BF16], (1024, 1024, 2048), epilogue=lambda acc: (acc, _relu2(acc)), b_shards=True)
    d_x2, d_mlp, d_wf, d_gm, loss_row = _mm(
        "mlp_down_loss", act, W_down, "nn", [F32, BF16], (512, D, 1024), chunk=128, n_sums=3, vmem=FUSED_VMEM_LIMIT,
        epilogue=_mlp_down_epilogue, extras=[("mn", x1), ("mn", tgt), ("n", g_m), ("n", wf_row)])
    loss_p = loss_row[:, :1]

    (d_up,) = _mm("mlp_down_bwd_act", d_mlp, W_down, "nt", [BF16], (1024, 1024, 2048), extras=[("mn", up)], epilogue=lambda acc, uu: (acc * (2.0 * jnp.maximum(uu, 0.0)),))
    (gW_down,) = _mm("mlp_down_bwd_w", act, d_mlp, "tn", [BF16], (1024, 2048, 1024))
    (gW_up_s,) = _mm("mlp_up_bwd_w", hm, d_up, "tn", [BF16], (2048, 1024, 1024), out_shards=True)
    g_mlp_srcs = [gW_up_s, gW_down.reshape(N_DEV, -1, D)]
    g_mlp = _exchange_start("grads_mlp_start", "all_to_all", g_mlp_srcs)
    (d_hm,) = _mm("mlp_up_bwd_act", d_up, W_up, "nt", [F32], (1024, D, 1024), b_shards=True)
    d_x1, s_sh_m, s_sc_m, s_w_nm, d_ga, d_o = _norm_bwd("norm_mlp_bwd", x1, d_hm, d_x2, norm_mlp_w, sc_m,
                                                          jnp.zeros((1, D), F32) + g_mlp[4][0, 0], gate=(o, g_a))

    (d_ap,) = _mm("out_proj_bwd_act", d_o, W_out, "nt", [F32], (1024, 1024, 2048))
    (gW_out,) = _mm("out_proj_bwd_w", ap, d_o, "tn", [BF16], (1024, 2048, 1024))
    d_pooled, gPW, d_pscale = _pool_bwd_mix(d_ap, pooled, PW, pool_scale, A // P)
    gpw_s = jnp.transpose(gPW.astype(BF16).reshape(POOL_GROUPS, N_DEV, gd // N_DEV, gd), (1, 0, 2, 3)).reshape(N_DEV, -1, gd)
    g_mix_srcs = [gW_out.reshape(N_DEV, (A + P) // N_DEV, D), gpw_s]
    g_mix = _exchange_start("grads_mix_start", "all_to_all", g_mix_srcs)
    lse = lse + g_mix[4][0, 0]
    d_u = _pool_bwd_window(d_pooled)
    d_q, rd, dsink_q, d_kvc = _attn_bwd_dq(qk, uv, kvc, attn_sink, d_ap, lse, cos, sin, A, KV, P)
    d_k, d_v = _attn_bwd_dkv(qk, uv, d_ap, lse, rd, cos, sin, A, KV, P)
    d_sink = _sum_lanes("sink_grad", dsink_q).reshape(1, H)
    d_p = jnp.concatenate([d_q, d_k, d_v, d_u], axis=1)
    d_kvc_b = d_kvc.astype(BF16)
    (gW_kv_ctx,) = _mm("in_proj_ctx_bwd_w", hc, d_kvc_b, "tn", [F32], SMALL_TILES)
    (d_hc,) = _mm("in_proj_ctx_bwd_act", d_kvc_b, W_kv, "nt", [F32], SMALL_TILES)
    gW_in_init = jnp.pad(gW_kv_ctx, ((0, 0), (A, P)))
    (gW_in,) = _mm("in_proj_bwd_w", h, d_p, "tn", [BF16], (1024, 1280, 1024), extras=[("mn", gW_in_init)], epilogue=lambda acc, init: (acc + init,))
    g_in_srcs = [_cols_to_shards(gW_in)]
    g_in = _exchange_start("grads_in_start", "all_to_all", g_in_srcs)
    grad_x, s_sh_a, s_sc_a, s_w_na = _mm(
        "in_proj_bwd_norm", d_p, W_in, "nt", [F32], (256, D, A + 2 * KV + P), chunk=128, n_sums=3,
        epilogue=lambda acc, xr, dres, w, sc: _modulated_norm_bwd(xr, acc, dres, w, sc),
        extras=[("mn", x2d), ("mn", d_x1), ("n", norm_attn_w), ("n", sc_a + g_in[4][0, 0])])
    _, s_csh, s_csc, s_w_na = _norm_bwd("norm_attn_ctx_bwd", ctx2d, d_hc, jnp.zeros_like(ctx2d), norm_attn_w, csc_a, s_w_na)

    pad_l = lambda a: jnp.pad(a, ((0, 0), (0, LANES - a.shape[1])))
    d_mod_b = jnp.concatenate([s_sh_a, s_sc_a, d_ga, s_sh_m, s_sc_m, d_gm], axis=1)
    summed = jnp.concatenate([s_csh, s_csc, s_w_na, s_w_nm, d_wf, d_pscale, pad_l(d_sink), pad_l(loss_p)], axis=1)
    (small_g,) = _all_gather("gather_small", [jnp.concatenate([d_mod_b, summed], axis=1)])
    small_g = small_g[:, 0, :]
    tot = _sum_rows("small_sum", small_g[:, MODW:])
    off = [0]
    for wdt in (D, D, D, D, D, P, LANES, LANES):
        off.append(off[-1] + wdt)
    seg = lambda i: tot[:, off[i]:off[i + 1]]
    g_norm_attn, g_norm_mlp, g_final, g_pscale = seg(2), seg(3), seg(4), seg(5)
    g_sink, loss = seg(6)[:, :H], seg(7)[0, 0]
    d_mod_ctx = jnp.concatenate([seg(0), seg(1), jnp.zeros((1, MODW - 2 * D), F32)], axis=1)
    d_mod = jnp.concatenate([small_g[:, :MODW], d_mod_ctx, jnp.zeros((COND_ROWS - N_DEV - 1, MODW), F32)], axis=0)
    g_b_ada = _sum_rows("b_ada_grad", d_mod[:N_DEV + 1])
    d_mod_sh = lax.dynamic_slice_in_dim(d_mod, me * ws, ws, axis=1)
    (g_w_ada,) = _mm("ada_bwd_w", cond, d_mod_sh, "tn", [F32], SMALL_TILES, a_pre=_silu)
    (d_cond_p,) = _mm("ada_bwd_cond", d_mod_sh, w_ada_l, "nt", [F32], SMALL_TILES)
    (d_cctx_g,) = _all_gather("gather_cctx", [d_cond_p[N_DEV:N_DEV + 1]])
    g_c_ctx = _silu_grad_mul(cctx_row, _sum_rows("cctx_sum", d_cctx_g[:, 0, :]))

    def arrived(name, started):
        srcs, lands = _exchange_wait(name, "all_to_all", started[0], started[1], started[2], started[3], g_c_ctx)
        return [_with_own(l, lax.dynamic_index_in_dim(s, me, 0, keepdims=True), me) for l, s in zip(lands, srcs)]

    r_up, r_down = arrived("grads_mlp_wait", g_mlp)
    r_out, r_pw = arrived("grads_mix_wait", g_mix)
    (r_in,) = arrived("grads_in_wait", g_in)

    results = {
        "c_ctx": _adamw("adam_c_ctx", cctx_row, g_c_ctx, m_c_ctx.reshape(1, D), v_c_ctx.reshape(1, D)),
        "norm_attn_w": _adamw("adam_norm_attn", norm_attn_w, g_norm_attn, m_norm_attn_w, v_norm_attn_w),
        "norm_mlp_w": _adamw("adam_norm_mlp", norm_mlp_w, g_norm_mlp, m_norm_mlp_w, v_norm_mlp_w),
        "w_ada": _adamw("adam_w_ada", w_ada_l, g_w_ada, m_w_ada[0], v_w_ada[0]),
        "b_ada": _adamw("adam_b_ada", b_ada, g_b_ada, m_b_ada, v_b_ada),
        "w_in": _adamw("adam_w_in", w_in[0], r_in, m_w_in[0], v_w_in[0]),
        "attn_sink": _adamw("adam_sink", attn_sink, g_sink, m_attn_sink, v_attn_sink),
        "pool_w": _adamw("adam_pool_w", pool_w_l, r_pw, m_pool_w[0].reshape(pool_w_l.shape), v_pool_w[0].reshape(pool_w_l.shape)),
        "pool_scale": _adamw("adam_pool_scale", pool_scale, g_pscale, m_pool_scale, v_pool_scale),
        "w_out": _adamw("adam_w_out", w_out[0], r_out, m_w_out[0], v_w_out[0]),
        "w_mlp_up": _adamw("adam_w_up", w_mlp_up[0], r_up, m_w_mlp_up[0], v_w_mlp_up[0]),
        "w_mlp_down": _adamw("adam_w_down", w_mlp_down[0], r_down, m_w_mlp_down[0], v_w_mlp_down[0]),
        "final_norm_w": _adamw("adam_final_norm", wf_row, g_final, m_final_norm_w.reshape(1, D), v_final_norm_w.reshape(1, D)),
    }
    shapes = {"c_ctx": c_ctx.shape, "norm_attn_w": norm_attn_w.shape, "norm_mlp_w": norm_mlp_w.shape, "w_ada": w_ada.shape,
              "b_ada": b_ada.shape, "w_in": w_in.shape, "attn_sink": attn_sink.shape, "pool_w": pool_w.shape,
              "pool_scale": pool_scale.shape, "w_out": w_out.shape, "w_mlp_up": w_mlp_up.shape, "w_mlp_down": w_mlp_down.shape,
              "final_norm_w": final_norm_w.shape}
    outs = [loss, grad_x.reshape(x.shape)]
    for part in range(4):
        outs += [results[name][part].reshape(shape) for name, shape in shapes.items()]
    return tuple(outs)
```

```python
import jax
import jax.numpy as jnp
import numpy as np
from jax import lax
from jax.experimental import pallas as pl
from jax.experimental.pallas import tpu as pltpu

F32 = jnp.float32
BF16 = jnp.bfloat16
I32 = jnp.int32

HEAD_DIM = 64
GQA = 4
BLOCK = 128
GRID_W = 64
ROPE_BASE = 10000.0
POOL_WINDOWS = (2, 4, 8, 16)
POOL_GROUPS = len(POOL_WINDOWS)
HALO = 8
N_MOD = 6
EPS = 1e-6
NEG_INF = -1e30
ADAM_LR = 0.001
ADAM_B1 = 0.9
ADAM_B2 = 0.999
ADAM_EPS = 1e-08
ADAM_WD = 0.01
ADAM_STEP = 10
N_DEV = 8
COND_ROWS = 2 * N_DEV
LANES = 128
SUBLANES_16BIT = 16
VMEM_LIMIT = 48 * 1024 * 1024
FUSED_VMEM_LIMIT = 56 * 1024 * 1024
SMALL_TILES = (512, 1024, 512)
MESH = pl.DeviceIdType.MESH
HBM = pl.BlockSpec(memory_space=pltpu.HBM)
SEM = pl.BlockSpec(memory_space=pltpu.SEMAPHORE)
SIDE_EFFECT = pltpu.CompilerParams(has_side_effects=pltpu.SideEffectType.DATAFLOW_SIDE_EFFECTING)


def _cparams(*sem):
    return pltpu.CompilerParams(dimension_semantics=sem, vmem_limit_bytes=VMEM_LIMIT)


def _tile(n, pref, align):
    if n <= pref:
        return n
    t = (pref // align) * align
    while t >= align:
        if n % t == 0:
            return t
        t -= align
    return n


def _dot(a, b):
    return lax.dot_general(a, b, (((1,), (0,)), ((), ())), preferred_element_type=F32)


def _dot_nt(a, b):
    return lax.dot_general(a, b, (((1,), (1,)), ((), ())), preferred_element_type=F32)


def _dot_tn(a, b):
    return lax.dot_general(a, b, (((0,), (0,)), ((), ())), preferred_element_type=F32)


_DOTS = {"nn": _dot, "nt": _dot_nt, "tn": _dot_tn}


def _mm(name, a, b, mode, out_dtypes, tiles, *, epilogue=None, extras=(), a_pre=None, n_sums=0, chunk=None,
        b_shards=False, out_shards=False, vmem=VMEM_LIMIT):
    if mode == "nn":
        M, K = a.shape
        K2, N = (b.shape[1], N_DEV * b.shape[2]) if b_shards else b.shape
    elif mode == "nt":
        M, K = a.shape
        N, K2 = (b.shape[1], N_DEV * b.shape[2]) if b_shards else b.shape
    else:
        (K, M), (K2, N) = a.shape, b.shape
    assert K == K2 and not (b_shards and mode == "tn"), (name, a.shape, b.shape)
    n_span = N // N_DEV if out_shards or (b_shards and mode == "nn") else N
    k_span = K // N_DEV if b_shards and mode == "nt" else K
    tm = _tile(M, tiles[0], LANES if mode == "tn" else SUBLANES_16BIT)
    tn = _tile(n_span, tiles[1], LANES)
    tk = _tile(k_span, tiles[2], SUBLANES_16BIT if mode == "tn" else LANES)
    nk, nb, kb = K // tk, n_span // tn, k_span // tk
    rows = tm if chunk is None else min(chunk, tm)
    n_ex, n_out = len(extras), len(out_dtypes)
    use_acc = nk > 1 or rows < tm
    assert n_sums == 0 or N == tn, name

    def product(a_ref, b_ref):
        at = a_ref[...]
        if a_pre is not None:
            at = a_pre(at)
        return _DOTS[mode](at.astype(BF16), b_ref[...].astype(BF16))

    def apply(acc, ex, out_refs, sl):
        res = (acc,) if epilogue is None else epilogue(acc, *ex)
        for o_ref, o in zip(out_refs, res[:n_out]):
            o_ref[sl, :] = o.astype(o_ref.dtype)
        return tuple(res[n_out:])

    def finish(acc, ex_refs, out_refs, sum_refs):
        if rows == tm:
            acc = acc if not use_acc else acc[...]
            sums = apply(acc, [r[...] for r in ex_refs], out_refs, slice(None))
        else:
            def one(ci, sums):
                sl = pl.ds(pl.multiple_of(ci * rows, rows), rows)
                ex = [r[...] if kind == "n" else r[sl, :] for (kind, _), r in zip(extras, ex_refs)]
                return tuple(s + v for s, v in zip(sums, apply(acc[sl, :], ex, out_refs, sl)))
            sums = lax.fori_loop(0, tm // rows, one, tuple(jnp.zeros((1, tn), F32) for _ in range(n_sums)))
        first = pl.program_id(0) == 0
        for s_ref, sv in zip(sum_refs, sums):
            @pl.when(first)
            def _(s_ref=s_ref, sv=sv):
                s_ref[...] = sv

            @pl.when(jnp.logical_not(first))
            def _(s_ref=s_ref, sv=sv):
                s_ref[...] += sv

    def body(a_ref, b_ref, *rest):
        ex_refs, out_refs = rest[:n_ex], rest[n_ex:n_ex + n_out]
        sum_refs = rest[n_ex + n_out:n_ex + n_out + n_sums]
        if not use_acc:
            finish(product(a_ref, b_ref), ex_refs, out_refs, sum_refs)
            return
        acc_ref = rest[-1]
        k = pl.program_id(2)

        @pl.when(k == 0)
        def _():
            acc_ref[...] = product(a_ref, b_ref)

        @pl.when(k > 0)
        def _():
            acc_ref[...] += product(a_ref, b_ref)

        @pl.when(k == nk - 1)
        def _():
            finish(acc_ref, ex_refs, out_refs, sum_refs)

    a_spec = pl.BlockSpec((tk, tm), lambda i, j, k: (k, i)) if mode == "tn" else pl.BlockSpec((tm, tk), lambda i, j, k: (i, k))
    if not b_shards:
        b_spec = pl.BlockSpec((tn, tk), lambda i, j, k: (j, k)) if mode == "nt" else pl.BlockSpec((tk, tn), lambda i, j, k: (k, j))
    elif mode == "nn":
        b_spec = pl.BlockSpec((None, tk, tn), lambda i, j, k: (j // nb, k, j % nb))
    else:
        b_spec = pl.BlockSpec((None, tn, tk), lambda i, j, k: (k // kb, j, k % kb))
    ex_specs = []
    for kind, arr in extras:
        if kind == "mn":
            ex_specs.append(pl.BlockSpec((tm, tn), lambda i, j, k: (i, j)))
        elif kind == "n":
            ex_specs.append(pl.BlockSpec((1, tn), lambda i, j, k: (0, j)))
        else:
            ex_specs.append(pl.BlockSpec((tm, arr.shape[1]), lambda i, j, k: (i, 0)))
    if out_shards:
        out_specs = [pl.BlockSpec((None, tm, tn), lambda i, j, k: (j // nb, i, j % nb)) for _ in out_dtypes]
        out_shape = [jax.ShapeDtypeStruct((N_DEV, M, n_span), d) for d in out_dtypes]
    else:
        out_specs = [pl.BlockSpec((tm, tn), lambda i, j, k: (i, j)) for _ in out_dtypes]
        out_shape = [jax.ShapeDtypeStruct((M, N), d) for d in out_dtypes]
    out_specs += [pl.BlockSpec((1, tn), lambda i, j, k: (0, 0))] * n_sums
    out_shape += [jax.ShapeDtypeStruct((1, N), F32)] * n_sums
    return pl.pallas_call(
        body,
        name=name,
        grid=(M // tm, N // tn, nk),
        in_specs=[a_spec, b_spec] + ex_specs,
        out_specs=out_specs,
        out_shape=out_shape,
        scratch_shapes=[pltpu.VMEM((tm, tn), F32)] if use_acc else [],
        compiler_params=pltpu.CompilerParams(
            dimension_semantics=("arbitrary",) * 3 if n_sums else ("parallel", "parallel", "arbitrary"), vmem_limit_bytes=vmem),
    )(a, b, *[arr for _, arr in extras])


def _silu(v):
    return v / (1.0 + jnp.exp(-v))


def _relu2(v):
    r = jnp.maximum(v, 0.0)
    return r * r


def _rope_tables(L):
    half = HEAD_DIM // 2
    inv_freq = np.float32(ROPE_BASE) ** (-np.arange(0, half, 2, dtype=np.float32) / np.float32(half))
    t = np.arange(L)
    row, col = t // GRID_W, t % GRID_W
    ang_r = row.astype(np.float32)[:, None] * inv_freq[None, :]
    ang_c = col.astype(np.float32)[:, None] * inv_freq[None, :]
    cos = np.concatenate([np.cos(ang_r), np.cos(ang_r), np.cos(ang_c), np.cos(ang_c)], axis=1)
    sin = np.concatenate([-np.sin(ang_r), np.sin(ang_r), -np.sin(ang_c), np.sin(ang_c)], axis=1)
    reps = LANES // HEAD_DIM
    return jnp.asarray(np.tile(cos, (1, reps)), F32), jnp.asarray(np.tile(sin, (1, reps)), F32)


def _rope(xf, cos, sin):
    quarter = HEAD_DIM // 4
    lane = lax.broadcasted_iota(I32, (xf.shape[0], LANES), 1)
    first = (lane & quarter) == 0
    outs = []
    for j in range(xf.shape[1] // LANES):
        xc = xf[:, j * LANES:(j + 1) * LANES]
        partner = jnp.where(first, pltpu.roll(xc, LANES - quarter, 1), pltpu.roll(xc, quarter, 1))
        outs.append(xc * cos + partner * sin)
    return outs[0] if len(outs) == 1 else jnp.concatenate(outs, axis=1)


def _inv_rms(xf):
    return lax.rsqrt(jnp.mean(xf * xf, axis=-1, keepdims=True) + EPS)


def _modulated_norm(xf, w, sc, sh):
    return ((xf * _inv_rms(xf)) * w) * (1.0 + sc) + sh


def _modulated_norm_bwd(xf, dh, dres, w, sc):
    r = _inv_rms(xf)
    xh = xf * r
    dn = dh * (1.0 + sc)
    dxh = dn * w
    dx = dres + r * (dxh - xh * jnp.mean(dxh * xh, axis=-1, keepdims=True))
    col = lambda v: jnp.sum(v, axis=0, keepdims=True)
    return dx, col(dh), col(dh * (xh * w)), col(dn * xh)


def _norm_fwd(name, x, w, sc, sh):
    L, D = x.shape
    T = _tile(L, 512, 8)

    def body(x_ref, w_ref, sc_ref, sh_ref, h_ref):
        h_ref[...] = _modulated_norm(x_ref[...], w_ref[...], sc_ref[...], sh_ref[...]).astype(BF16)

    row = pl.BlockSpec((1, D), lambda i: (0, 0))
    return pl.pallas_call(
        body, name=name, grid=(L // T,),
        in_specs=[pl.BlockSpec((T, D), lambda i: (i, 0)), row, row, row],
        out_specs=pl.BlockSpec((T, D), lambda i: (i, 0)),
        out_shape=jax.ShapeDtypeStruct((L, D), BF16),
        compiler_params=_cparams("parallel"),
    )(x, w, sc, sh)


def _norm_bwd_sums(name, x, dh, w, sc, w_init):
    L, D = x.shape
    T = _tile(L, 256, 8)

    def body(x_ref, dh_ref, w_ref, sc_ref, wi_ref, ssh_ref, ssc_ref, sw_ref):
        @pl.when(pl.program_id(0) == 0)
        def _():
            ssh_ref[...] = jnp.zeros_like(ssh_ref)
            ssc_ref[...] = jnp.zeros_like(ssc_ref)
            sw_ref[...] = wi_ref[...]

        dh = dh_ref[...]
        _, s_sh, s_sc, s_w = _modulated_norm_bwd(x_ref[...], dh, jnp.zeros_like(dh), w_ref[...], sc_ref[...])
        ssh_ref[...] += s_sh
        ssc_ref[...] += s_sc
        sw_ref[...] += s_w

    tile = pl.BlockSpec((T, D), lambda i: (i, 0))
    row = pl.BlockSpec((1, D), lambda i: (0, 0))
    return pl.pallas_call(
        body, name=name, grid=(L // T,), in_specs=[tile, tile, row, row, row], out_specs=[row, row, row],
        out_shape=[jax.ShapeDtypeStruct((1, D), F32)] * 3, compiler_params=_cparams("arbitrary"),
    )(x, dh, w, sc, w_init)


def _norm_bwd_out_proj_bwd(x, dh, dres, o, w, sc, g, w_out_t):
    L, D = x.shape
    N = w_out_t.shape[1]
    T = _tile(L, 256, SUBLANES_16BIT)
    half = T // 2

    def body(x_ref, dh_ref, dres_ref, o_ref, w_ref, sc_ref, g_ref, wt_ref, dx_ref, do_ref, dap_ref, ssh_ref, ssc_ref, sw_ref, sg_ref):
        @pl.when(pl.program_id(0) == 0)
        def _():
            for s_ref in (ssh_ref, ssc_ref, sw_ref, sg_ref):
                s_ref[...] = jnp.zeros_like(s_ref)

        for rows in (slice(0, half), slice(half, T)):
            dx, s_sh, s_sc, s_w = _modulated_norm_bwd(x_ref[rows, :], dh_ref[rows, :], dres_ref[rows, :], w_ref[...], sc_ref[...])
            ssh_ref[...] += s_sh
            ssc_ref[...] += s_sc
            sw_ref[...] += s_w
            sg_ref[...] += jnp.sum(dx * o_ref[rows, :], axis=0, keepdims=True)
            dx_ref[rows, :] = dx
            do_ref[rows, :] = (g_ref[...] * dx).astype(BF16)
        dap_ref[...] = _dot(do_ref[...], wt_ref[...])

    tile = pl.BlockSpec((T, D), lambda i: (i, 0))
    row = pl.BlockSpec((1, D), lambda i: (0, 0))
    return pl.pallas_call(
        body, name="norm_mlp_bwd_out_proj_bwd", grid=(L // T,),
        in_specs=[tile, tile, tile, tile, row, row, row, pl.BlockSpec((D, N), lambda i: (0, 0))],
        out_specs=[tile, tile, pl.BlockSpec((T, N), lambda i: (i, 0)), row, row, row, row],
        out_shape=[jax.ShapeDtypeStruct((L, D), F32), jax.ShapeDtypeStruct((L, D), BF16), jax.ShapeDtypeStruct((L, N), F32)]
                  + [jax.ShapeDtypeStruct((1, D), F32)] * 4,
        compiler_params=pltpu.CompilerParams(dimension_semantics=("arbitrary",), vmem_limit_bytes=FUSED_VMEM_LIMIT),
    )(x, dh, dres, o, w, sc, g, w_out_t)


def _out_proj_epilogue(acc, xr, g, w, sc, sh):
    x1 = xr + g * acc
    return acc, x1, _modulated_norm(x1, w, sc, sh)


def _mlp_down_epilogue(acc, x1, tgt, g, wf):
    D = acc.shape[1]
    x2 = x1 + g * acc
    r = _inv_rms(x2)
    xh = x2 * r
    err = xh * wf - tgt
    loss = 0.5 * jnp.sum(jnp.mean(err * err, axis=-1, keepdims=True), axis=0, keepdims=True)
    dy = err * (1.0 / D)
    dxh = dy * wf
    dx = r * (dxh - xh * jnp.mean(dxh * xh, axis=-1, keepdims=True))
    col = lambda v: jnp.sum(v, axis=0, keepdims=True)
    return dx, g * dx, col(dy * xh), col(dx * acc), jnp.broadcast_to(loss, (1, D))


def _heads(ref, first, n):
    return jnp.concatenate([ref[:, (first + g) * HEAD_DIM:(first + g + 1) * HEAD_DIM] for g in range(n)], axis=0)


def _attn_mask(n, L, C):
    shape = (3 * BLOCK + C, GQA * BLOCK)
    kj = lax.broadcasted_iota(I32, shape, 0)
    qi = lax.broadcasted_iota(I32, shape, 1) & (BLOCK - 1)
    kpos = n * BLOCK - BLOCK + kj
    window = (kj >= qi) & (kj <= qi + 2 * BLOCK) & (kpos >= 0) & (kpos < L)
    return window | (kj >= 3 * BLOCK)


def _head_rows(ref, hk):
    return jnp.concatenate([ref[hk * GQA + g:hk * GQA + g + 1, :] for g in range(GQA)], axis=1)


def _rows_to_heads(rows_by_kv_head):
    return jnp.concatenate([r[:, g * BLOCK:(g + 1) * BLOCK] for r in rows_by_kv_head for g in range(GQA)], axis=0)


def _queries_to_rows(t):
    return jnp.concatenate([t[:, g * BLOCK:(g + 1) * BLOCK].T for g in range(GQA)], axis=1)


def _attn_specs(L, A, KV, C, vcol):
    nb = L // BLOCK
    kcol = A // KV
    prev = lambda n: jnp.maximum(n - 1, 0)
    nxt = lambda n: jnp.minimum(n + 1, nb - 1)
    q_spec = pl.BlockSpec((BLOCK, A), lambda n: (n, 0))
    k_specs = [pl.BlockSpec((BLOCK, KV), lambda n: (prev(n), kcol)), pl.BlockSpec((BLOCK, KV), lambda n: (n, kcol)),
               pl.BlockSpec((BLOCK, KV), lambda n: (nxt(n), kcol))]
    v_specs = [pl.BlockSpec((BLOCK, KV), lambda n: (prev(n), vcol)), pl.BlockSpec((BLOCK, KV), lambda n: (n, vcol)),
               pl.BlockSpec((BLOCK, KV), lambda n: (nxt(n), vcol))]
    kvc_spec = pl.BlockSpec((C, 2 * KV), lambda n: (0, 0))
    return q_spec, k_specs, v_specs, kvc_spec


def _keys_values(hk, k_refs, v_refs, kvc_ref, KV):
    sl = slice(hk * HEAD_DIM, (hk + 1) * HEAD_DIM)
    keys = jnp.concatenate([r[:, sl] for r in k_refs] + [kvc_ref[:, sl]], axis=0)
    vals = jnp.concatenate([r[:, sl].astype(BF16) for r in v_refs] + [kvc_ref[:, KV + hk * HEAD_DIM:KV + (hk + 1) * HEAD_DIM]], axis=0)
    return keys, vals


def _sink_row(sink_ref, hk):
    return jnp.concatenate([jnp.full((1, BLOCK), sink_ref[0, hk * GQA + g], F32) for g in range(GQA)], axis=1)


def _attn_fwd(qk, uv, kvc, sink, A, KV, P):
    L = qk.shape[0]
    C = kvc.shape[0]
    nkv = KV // HEAD_DIM
    H = nkv * GQA
    scale = HEAD_DIM ** -0.5

    def body(sink_ref, q_ref, kp_ref, kc_ref, kn_ref, vp_ref, vc_ref, vn_ref, kvc_ref, o_ref, lse_ref):
        valid = _attn_mask(pl.program_id(0), L, C)
        lse_rows = []
        for hk in range(nkv):
            keys, vals = _keys_values(hk, (kp_ref, kc_ref, kn_ref), (vp_ref, vc_ref, vn_ref), kvc_ref, KV)
            qs = _heads(q_ref, hk * GQA, GQA) * scale
            s = jnp.where(valid, _dot_nt(keys, qs), NEG_INF)
            sk = _sink_row(sink_ref, hk)
            m = jnp.maximum(jnp.max(s, axis=0, keepdims=True), sk)
            p = jnp.exp(s - m)
            den = jnp.sum(p, axis=0, keepdims=True) + jnp.exp(sk - m)
            o = _dot_tn(vals, p.astype(BF16)) * (1.0 / den)
            lse_rows.append(m + jnp.log(den))
            o_ref[:, hk * GQA * HEAD_DIM:(hk + 1) * GQA * HEAD_DIM] = _queries_to_rows(o).astype(BF16)
        lse_ref[...] = _rows_to_heads(lse_rows)

    q_spec, k_specs, v_specs, kvc_spec = _attn_specs(L, A, KV, C, P // KV)
    return pl.pallas_call(
        body, name="attn_fwd", grid=(L // BLOCK,),
        in_specs=[pl.BlockSpec(memory_space=pltpu.SMEM), q_spec] + k_specs + v_specs + [kvc_spec],
        out_specs=[pl.BlockSpec((BLOCK, A), lambda n: (n, 0)), pl.BlockSpec((H, BLOCK), lambda n: (0, n))],
        out_shape=[jax.ShapeDtypeStruct((L, A + P), BF16), jax.ShapeDtypeStruct((H, L), F32)],
        compiler_params=_cparams("parallel"),
    )(sink, qk, qk, qk, qk, uv, uv, uv, kvc)


def _attn_bwd_dq(qk, uv, kvc, sink, dap, lse_t, cos, sin, A, KV, P):
    L = qk.shape[0]
    C = kvc.shape[0]
    nkv = KV // HEAD_DIM
    H = nkv * GQA
    scale = HEAD_DIM ** -0.5
    W = 3 * BLOCK

    def body(sink_ref, q_ref, kp_ref, kc_ref, kn_ref, vp_ref, vc_ref, vn_ref, kvc_ref, do_ref, lse_ref, cos_ref, sin_ref,
             dq_ref, rd_ref, ds_ref, dkvc_ref):
        n = pl.program_id(0)

        @pl.when(n == 0)
        def _():
            dkvc_ref[...] = jnp.zeros_like(dkvc_ref)

        valid = _attn_mask(n, L, C)
        rd_rows, dsink_rows, dq_parts = [], [], []
        for hk in range(nkv):
            sl = slice(hk * HEAD_DIM, (hk + 1) * HEAD_DIM)
            keys, vals = _keys_values(hk, (kp_ref, kc_ref, kn_ref), (vp_ref, vc_ref, vn_ref), kvc_ref, KV)
            qs = _heads(q_ref, hk * GQA, GQA) * scale
            dos = _heads(do_ref, hk * GQA, GQA).astype(BF16)
            lse = _head_rows(lse_ref, hk)
            p = jnp.exp(jnp.where(valid, _dot_nt(keys, qs), NEG_INF) - lse)
            dp = _dot_nt(vals, dos)
            rd = jnp.sum(p * dp, axis=0, keepdims=True)
            ds = (p * (dp - rd)).astype(BF16)
            dq_parts.append(_queries_to_rows(_dot_tn(keys, ds) * scale))
            dkvc_ref[:, sl] += _dot(ds[W:, :], qs)
            dkvc_ref[:, KV + hk * HEAD_DIM:KV + (hk + 1) * HEAD_DIM] += _dot(p[W:, :].astype(BF16), dos)
            rd_rows.append(rd)
            dsink_rows.append(-(jnp.exp(_sink_row(sink_ref, hk) - lse) * rd))
        rd_ref[...] = _rows_to_heads(rd_rows)
        ds_ref[...] = _rows_to_heads(dsink_rows)
        dq = dq_parts[0] if nkv == 1 else jnp.concatenate(dq_parts, axis=1)
        dq_ref[...] = _rope(dq, cos_ref[...], -sin_ref[...]).astype(BF16)

    q_spec, k_specs, v_specs, kvc_spec = _attn_specs(L, A, KV, C, P // KV)
    blk = lambda w: pl.BlockSpec((BLOCK, w), lambda n: (n, 0))
    per_head = pl.BlockSpec((H, BLOCK), lambda n: (0, n))
    return pl.pallas_call(
        body, name="attn_bwd_dq", grid=(L // BLOCK,),
        in_specs=[pl.BlockSpec(memory_space=pltpu.SMEM), q_spec] + k_specs + v_specs + [kvc_spec, blk(A), per_head, blk(LANES), blk(LANES)],
        out_specs=[blk(A), per_head, per_head, pl.BlockSpec((C, 2 * KV), lambda n: (0, 0))],
        out_shape=[jax.ShapeDtypeStruct((L, A), BF16), jax.ShapeDtypeStruct((H, L), F32), jax.ShapeDtypeStruct((H, L), F32),
                   jax.ShapeDtypeStruct((C, 2 * KV), F32)],
        compiler_params=_cparams("arbitrary"),
    )(sink, qk, qk, qk, qk, uv, uv, uv, kvc, dap, lse_t, cos, sin)


def _attn_bwd_dkv(qk, uv, dap, lse_t, rd_t, cos, sin, A, KV, P):
    L = qk.shape[0]
    nb = L // BLOCK
    nkv = KV // HEAD_DIM
    H = nkv * GQA
    scale = HEAD_DIM ** -0.5
    R = 3 * GQA * BLOCK

    def body(k_ref, v_ref, qp_ref, qc_ref, qn_ref, dop_ref, doc_ref, don_ref, lsep_ref, lsec_ref, lsen_ref,
             rdp_ref, rdc_ref, rdn_ref, cos_ref, sin_ref, dk_ref, dv_ref):
        m = pl.program_id(0)
        kj = lax.broadcasted_iota(I32, (BLOCK, R), 0)
        col = lax.broadcasted_iota(I32, (BLOCK, R), 1)
        part = col // (GQA * BLOCK)
        qi = col & (BLOCK - 1)
        before = jnp.where(m >= 1, 0, -2 * BLOCK)
        after = jnp.where(m <= nb - 2, 0, 2 * BLOCK)
        valid = ((part == 0) & (kj <= qi + before)) | (part == 1) | ((part == 2) & (kj >= qi + after))
        dk_parts, dv_parts = [], []
        for hk in range(nkv):
            sl = slice(hk * HEAD_DIM, (hk + 1) * HEAD_DIM)
            km = k_ref[:, sl]
            vm = v_ref[:, sl].astype(BF16)
            qs = jnp.concatenate([_heads(q, hk * GQA, GQA) for q in (qp_ref, qc_ref, qn_ref)], axis=0) * scale
            dos = jnp.concatenate([_heads(d, hk * GQA, GQA) for d in (dop_ref, doc_ref, don_ref)], axis=0).astype(BF16)
            rows = [slice(hk * GQA + g, hk * GQA + g + 1) for g in range(GQA)]
            lse = jnp.concatenate([t[r, :] for t in (lsep_ref, lsec_ref, lsen_ref) for r in rows], axis=1)
            rdv = jnp.concatenate([t[r, :] for t in (rdp_ref, rdc_ref, rdn_ref) for r in rows], axis=1)
            p = jnp.exp(jnp.where(valid, _dot_nt(km, qs), NEG_INF) - lse)
            ds = (p * (_dot_nt(vm, dos) - rdv)).astype(BF16)
            dk_parts.append(_dot(ds, qs))
            dv_parts.append(_dot(p.astype(BF16), dos))
        dk = dk_parts[0] if nkv == 1 else jnp.concatenate(dk_parts, axis=1)
        dv = dv_parts[0] if nkv == 1 else jnp.concatenate(dv_parts, axis=1)
        dk_ref[...] = _rope(dk, cos_ref[...], -sin_ref[...]).astype(BF16)
        dv_ref[...] = dv.astype(BF16)

    prev = lambda m: jnp.maximum(m - 1, 0)
    nxt = lambda m: jnp.minimum(m + 1, nb - 1)
    three = lambda w: [pl.BlockSpec((BLOCK, w), lambda m: (prev(m), 0)), pl.BlockSpec((BLOCK, w), lambda m: (m, 0)),
                       pl.BlockSpec((BLOCK, w), lambda m: (nxt(m), 0))]
    three_t = [pl.BlockSpec((H, BLOCK), lambda m: (0, prev(m))), pl.BlockSpec((H, BLOCK), lambda m: (0, m)),
               pl.BlockSpec((H, BLOCK), lambda m: (0, nxt(m)))]
    blk = lambda w: pl.BlockSpec((BLOCK, w), lambda m: (m, 0))
    return pl.pallas_call(
        body, name="attn_bwd_dkv", grid=(nb,),
        in_specs=[pl.BlockSpec((BLOCK, KV), lambda m: (m, A // KV)), pl.BlockSpec((BLOCK, KV), lambda m: (m, P // KV))]
                 + three(A) + three(A) + three_t + three_t + [blk(LANES), blk(LANES)],
        out_specs=[blk(KV), blk(KV)],
        out_shape=[jax.ShapeDtypeStruct((L, KV), BF16), jax.ShapeDtypeStruct((L, KV), BF16)],
        compiler_params=_cparams("parallel"),
    )(qk, uv, qk, qk, qk, dap, dap, dap, lse_t, lse_t, lse_t, rd_t, rd_t, rd_t, cos, sin)


def _halo_specs(T, L, W, col):
    per = T // HALO
    return [pl.BlockSpec((HALO, W), lambda i: (jnp.maximum(i * per - 1, 0), col)),
            pl.BlockSpec((T, W), lambda i: (i, col)),
            pl.BlockSpec((HALO, W), lambda i: (jnp.minimum((i + 1) * per, L // HALO - 1), col))]


def _fill_halo_buf(buf, prev_ref, cur_ref, next_ref, i, nt, T):
    buf[0:HALO, :] = jnp.where(i > 0, prev_ref[...], 0.0)
    buf[HALO:HALO + T, :] = cur_ref[...]
    buf[HALO + T:2 * HALO + T, :] = jnp.where(i < nt - 1, next_ref[...], 0.0)


def _zero_margins(lv):
    rows = lv.shape[0]
    lv[0:HALO, :] = jnp.zeros((HALO, lv.shape[1]), F32)
    lv[rows - HALO:rows, :] = jnp.zeros((HALO, lv.shape[1]), F32)


def _window_sums(lv, x, w, first):
    n = x.shape[0]
    lv[HALO:HALO + n, :] = x
    cur = x + lv[pl.ds(HALO + first, n), :]
    span = 1
    while 2 * span < w:
        lv[HALO:HALO + n, :] = cur
        cur = lv[pl.ds(HALO - span, n), :] + lv[pl.ds(HALO + span, n), :]
        span *= 2
    return cur


def _counts(t, w, L):
    lo = jnp.clip(t - w // 2, 0, L)
    hi = jnp.clip(t - w // 2 + w, 0, L)
    return jnp.maximum(hi - lo, 1).astype(F32)


def _pool_fwd(u, pw, scale, mix):
    L, P = u.shape[0], scale.shape[1]
    gd = P // POOL_GROUPS
    T = _tile(L, 256, 8)
    nt = L // T
    assert (mix.shape[1] - P) % P == 0
    mix_col = mix.shape[1] // P - 1

    def body(up_ref, uc_ref, un_ref, pw_ref, sc_ref, mix_ref, out_ref, pooled_ref, buf, lv):
        i = pl.program_id(0)
        _fill_halo_buf(buf, up_ref, uc_ref, un_ref, i, nt, T)
        _zero_margins(lv)
        t = i * T + lax.broadcasted_iota(I32, (T, 1), 0)
        for g, w in enumerate(POOL_WINDOWS):
            cols = slice(g * gd, (g + 1) * gd)
            acc = _window_sums(lv, buf[:, cols], w, -1)[HALO:HALO + T]
            pooled = (acc / _counts(t, w, L) - uc_ref[:, cols]).astype(BF16)
            pooled_ref[:, cols] = pooled
            out_ref[:, cols] = (_dot(pooled, pw_ref[g]) * sc_ref[:, cols]).astype(BF16)

    return pl.pallas_call(
        body, name="pool_fwd", grid=(nt,),
        in_specs=_halo_specs(T, L, P, 0) + [pl.BlockSpec((POOL_GROUPS, gd, gd), lambda i: (0, 0, 0)), pl.BlockSpec((1, P), lambda i: (0, 0)),
                                            pl.BlockSpec(memory_space=pl.ANY)],
        out_specs=[pl.BlockSpec((T, P), lambda i: (i, mix_col)), pl.BlockSpec((T, P), lambda i: (i, 0))],
        out_shape=[jax.ShapeDtypeStruct(mix.shape, BF16), jax.ShapeDtypeStruct((L, P), BF16)],
        scratch_shapes=[pltpu.VMEM((T + 2 * HALO, P), F32), pltpu.VMEM((T + 4 * HALO, gd), F32)],
        input_output_aliases={5: 0},
        compiler_params=_cparams("parallel"),
    )(u, u, u, pw, scale, mix)


def _pool_bwd_mix(dap, pooled, pw, scale, pcol):
    L, P = pooled.shape
    gd = P // POOL_GROUPS
    T = _tile(L, 256, 8)

    def body(dp_ref, pooled_ref, pw_ref, sc_ref, dpooled_ref, dpw_ref, dsc_ref):
        i = pl.program_id(0)

        @pl.when(i == 0)
        def _():
            dpw_ref[...] = jnp.zeros_like(dpw_ref)
            dsc_ref[...] = jnp.zeros_like(dsc_ref)

        for g in range(POOL_GROUPS):
            cols = slice(g * gd, (g + 1) * gd)
            pb = pooled_ref[:, cols]
            dp = dp_ref[:, cols]
            dsc_ref[:, cols] += jnp.sum(dp * _dot(pb, pw_ref[g]), axis=0, keepdims=True)
            dm = (dp * sc_ref[:, cols]).astype(BF16)
            dpw_ref[g] += _dot_tn(pb, dm)
            dpooled_ref[:, cols] = _dot_nt(dm, pw_ref[g])

    return pl.pallas_call(
        body, name="pool_bwd_mix", grid=(L // T,),
        in_specs=[pl.BlockSpec((T, P), lambda i: (i, pcol)), pl.BlockSpec((T, P), lambda i: (i, 0)),
                  pl.BlockSpec((POOL_GROUPS, gd, gd), lambda i: (0, 0, 0)), pl.BlockSpec((1, P), lambda i: (0, 0))],
        out_specs=[pl.BlockSpec((T, P), lambda i: (i, 0)), pl.BlockSpec((POOL_GROUPS, gd, gd), lambda i: (0, 0, 0)),
                   pl.BlockSpec((1, P), lambda i: (0, 0))],
        out_shape=[jax.ShapeDtypeStruct((L, P), F32), jax.ShapeDtypeStruct((POOL_GROUPS, gd, gd), F32), jax.ShapeDtypeStruct((1, P), F32)],
        compiler_params=_cparams("arbitrary"),
    )(dap, pooled, pw, scale)


def _pool_bwd_window(dpooled):
    L, P = dpooled.shape
    gd = P // POOL_GROUPS
    T = _tile(L, 256, 8)
    nt = L // T

    def body(dp_ref, dc_ref, dn_ref, du_ref, buf, lv):
        i = pl.program_id(0)
        _fill_halo_buf(buf, dp_ref, dc_ref, dn_ref, i, nt, T)
        _zero_margins(lv)
        t = i * T - HALO + lax.broadcasted_iota(I32, (T + 2 * HALO, 1), 0)
        for g, w in enumerate(POOL_WINDOWS):
            cols = slice(g * gd, (g + 1) * gd)
            acc = _window_sums(lv, buf[:, cols] / _counts(t, w, L), w, 1)[HALO:HALO + T]
            du_ref[:, cols] = (acc - dc_ref[:, cols]).astype(BF16)

    return pl.pallas_call(
        body, name="pool_bwd_window", grid=(nt,),
        in_specs=_halo_specs(T, L, P, 0),
        out_specs=pl.BlockSpec((T, P), lambda i: (i, 0)),
        out_shape=jax.ShapeDtypeStruct((L, P), BF16),
        scratch_shapes=[pltpu.VMEM((T + 2 * HALO, P), F32), pltpu.VMEM((T + 4 * HALO, gd), F32)],
        compiler_params=_cparams("parallel"),
    )(dpooled, dpooled, dpooled)


def _sum_rows(name, a):
    R, N = a.shape

    def body(a_ref, o_ref):
        if R <= 16:
            acc = a_ref[0:1, :]
            for r in range(1, R):
                acc = acc + a_ref[r:r + 1, :]
        else:
            acc = jnp.sum(a_ref[...], axis=0, keepdims=True)
        o_ref[...] = acc

    return pl.pallas_call(body, name=name, out_shape=jax.ShapeDtypeStruct((1, N), F32))(a)


def _sum_lanes(name, a):
    def body(a_ref, o_ref):
        o_ref[...] = jnp.sum(a_ref[...], axis=1, keepdims=True)

    return pl.pallas_call(body, name=name, out_shape=jax.ShapeDtypeStruct((a.shape[0], 1), F32))(a)


def _silu_grad_mul(cv, g):
    def body(c_ref, g_ref, o_ref):
        cvv = c_ref[...]
        s = 1.0 / (1.0 + jnp.exp(-cvv))
        o_ref[...] = g_ref[...] * (s * (1.0 + cvv * (1.0 - s)))

    return pl.pallas_call(body, name="silu_grad_mul", out_shape=jax.ShapeDtypeStruct(cv.shape, F32))(cv, g)


def _adamw(name, w, g, m, v):
    R, C = w.shape
    parts = g.ndim == 3
    n_parts = g.shape[0] if parts else 1
    T = _tile(R, max(8, 262144 // C), 8)

    def body(w_ref, g_ref, m_ref, v_ref, go_ref, d_ref, mo_ref, vo_ref):
        if parts:
            gv = g_ref[0].astype(F32)
            for p in range(1, n_parts):
                gv = gv + g_ref[p].astype(F32)
        else:
            gv = g_ref[...]
        mn = ADAM_B1 * m_ref[...] + (1.0 - ADAM_B1) * gv
        vn = ADAM_B2 * v_ref[...] + (1.0 - ADAM_B2) * (gv * gv)
        m_hat = mn / (1.0 - ADAM_B1 ** ADAM_STEP)
        v_hat = vn / (1.0 - ADAM_B2 ** ADAM_STEP)
        go_ref[...] = gv
        d_ref[...] = -ADAM_LR * (m_hat / (jnp.sqrt(v_hat) + ADAM_EPS) + ADAM_WD * w_ref[...])
        mo_ref[...] = mn
        vo_ref[...] = vn

    tile = pl.BlockSpec((T, C), lambda i: (i, 0))
    g_spec = pl.BlockSpec((n_parts, T, C), lambda i: (0, i, 0)) if parts else tile
    return pl.pallas_call(
        body, name=name, grid=(R // T,),
        in_specs=[tile, g_spec, tile, tile], out_specs=[tile] * 4,
        out_shape=[jax.ShapeDtypeStruct((R, C), F32)] * 4,
        compiler_params=_cparams("parallel"),
    )(w, g, m, v)


def _dev_index(px, py, pc):
    return 4 * px + 2 * py + pc


def _all_gather(name, arrs):
    n = len(arrs)

    def body(*refs):
        ins, outs = refs[:n], refs[n:2 * n]
        send_sems, recv_sems, local_sems = refs[2 * n:]
        x, y, c = lax.axis_index("x"), lax.axis_index("y"), lax.axis_index("c")
        me, sibling = (x, y, c), (x, y, 1 - c)
        chips = [(1 - x, y), (x, 1 - y), (1 - x, 1 - y)]

        def copy(a, k, block, to, src=None):
            slot = outs[a].at[_dev_index(*block)]
            return pltpu.make_async_remote_copy(
                src_ref=slot if src is None else src, dst_ref=slot, send_sem=send_sems.at[a, k], recv_sem=recv_sems.at[a, k],
                device_id=to, device_id_type=MESH)

        mine = [pltpu.make_async_copy(ins[a], outs[a].at[_dev_index(*me)], local_sems.at[a]) for a in range(n)]
        for cp in mine:
            cp.start()
        first = []
        for a in range(n):
            first.append(copy(a, 0, me, sibling, src=ins[a]))
            first += [copy(a, 1 + j, me, (*chip, c), src=ins[a]) for j, chip in enumerate(chips)]
        for cp in first:
            cp.start()
        passed = []
        for j, chip in enumerate(chips):
            for a in range(n):
                copy(a, 1 + j, (*chip, c), me).wait_recv()
                fwd = copy(a, 4 + j, (*chip, c), sibling)
                fwd.start()
                passed.append(fwd)
        for a in range(n):
            copy(a, 0, sibling, me).wait_recv()
            for j, chip in enumerate(chips):
                copy(a, 4 + j, (*chip, 1 - c), me).wait_recv()
        for cp in first + passed:
            cp.wait_send()
        for cp in mine:
            cp.wait()

    return pl.pallas_call(
        body, name=name,
        in_specs=[HBM] * n, out_specs=[HBM] * n,
        out_shape=[jax.ShapeDtypeStruct((N_DEV, *a.shape), a.dtype) for a in arrs],
        scratch_shapes=[pltpu.SemaphoreType.DMA((n, N_DEV - 1)), pltpu.SemaphoreType.DMA((n, N_DEV - 1)), pltpu.SemaphoreType.DMA((n,))],
    )(*arrs)


N_COPIES = {"all_to_all": N_DEV - 1, "gather_chips": 4, "forward": 3}


def _exchange_copies(kind, src_ref, land_ref, send_sems, recv_sems, sending):
    x, y, c = lax.axis_index("x"), lax.axis_index("y"), lax.axis_index("c")
    me = _dev_index(x, y, c)
    others = [(1 - x, y), (x, 1 - y), (1 - x, 1 - y)]
    if kind == "all_to_all":
        flips = [(dx, dy, dc) for dx in (0, 1) for dy in (0, 1) for dc in (0, 1)][1:]
        peers = [(1 - x if dx else x, 1 - y if dy else y, 1 - c if dc else c) for dx, dy, dc in flips]
        plan = [(p, src_ref.at[_dev_index(*p)], me if sending else _dev_index(*p)) for p in peers]
    elif kind == "gather_chips":
        peers = [(x, y, 1 - c)] + [(*o, c) for o in others]
        plan = [(p, src_ref, me if sending else _dev_index(*p)) for p in peers]
    else:
        plan = [((x, y, 1 - c), land_ref.at[_dev_index(*o, c)], _dev_index(*o, c if sending else 1 - c)) for o in others]
    return [pltpu.make_async_remote_copy(src_ref=src, dst_ref=land_ref.at[slot], send_sem=send_sems.at[k], recv_sem=recv_sems.at[k],
                                         device_id=peer, device_id_type=MESH)
            for k, (peer, src, slot) in enumerate(plan)]


def _exchange_start(name, kind, srcs, lands=None):
    if lands is None:
        lands = [lax.empty((N_DEV, *s.shape) if kind == "gather_chips" else s.shape, s.dtype) for s in srcs]
    n = len(lands)
    ops = ([] if srcs is None else list(srcs)) + list(lands)
    m = len(ops)

    def body(*refs):
        src_refs = [None] * n if srcs is None else refs[:n]
        land_refs = refs[m - n:m]
        send_sems, recv_sems, token = refs[m:m + n], refs[m + n:m + 2 * n], refs[-1]
        for a in range(n):
            for cp in _exchange_copies(kind, src_refs[a], land_refs[a], send_sems[a], recv_sems[a], True):
                cp.start()
        token[...] = jnp.zeros_like(token)

    sems = [pltpu.SemaphoreType.DMA((N_COPIES[kind],))] * (2 * n)
    outs = pl.pallas_call(
        body, name=name,
        out_shape=sems + [pltpu.HBM(o.shape, o.dtype) for o in ops] + [jax.ShapeDtypeStruct((8, LANES), F32)],
        in_specs=[HBM] * m,
        out_specs=[SEM] * (2 * n) + [HBM] * m + [pl.BlockSpec(memory_space=pltpu.VMEM)],
        input_output_aliases={i: 2 * n + i for i in range(m)},
        compiler_params=SIDE_EFFECT,
    )(*[pltpu.with_memory_space_constraint(o, pltpu.HBM) for o in ops])
    thru = outs[2 * n:2 * n + m]
    return outs[:n], outs[n:2 * n], (None if srcs is None else thru[:n]), thru[m - n:], outs[-1]


def _exchange_wait(name, kind, send_sems, recv_sems, srcs, lands, after):
    n = len(lands)
    ops = ([] if srcs is None else list(srcs)) + list(lands)
    m = len(ops)

    def body(*refs):
        src_refs = [None] * n if srcs is None else refs[:n]
        land_refs = refs[m - n:m]
        send_refs, recv_refs = refs[m:m + n], refs[m + n:m + 2 * n]
        for a in range(n):
            for cp in _exchange_copies(kind, src_refs[a], land_refs[a], send_refs[a], recv_refs[a], False):
                cp.wait_send()
                cp.wait_recv()

    outs = pl.pallas_call(
        body, name=name,
        out_shape=[pltpu.HBM(o.shape, o.dtype) for o in ops],
        in_specs=[HBM] * m + [SEM] * (2 * n) + [pl.BlockSpec(memory_space=pl.ANY)],
        out_specs=[HBM] * m,
        input_output_aliases={i: i for i in range(m)},
        compiler_params=SIDE_EFFECT,
    )(*ops, *send_sems, *recv_sems, after)
    return (None if srcs is None else outs[:n]), outs[m - n:]


def _with_own(land, own, me):
    return lax.dynamic_update_slice_in_dim(land, own, me, 0)


def _shards_to_cols(g):
    return jnp.transpose(g, (1, 0, 2)).reshape(g.shape[1], N_DEV * g.shape[2])


def _cols_to_shards(a):
    R, Ctot = a.shape
    return jnp.transpose(a.reshape(R, N_DEV, Ctot // N_DEV), (1, 0, 2))


def kernel(x, c, ctx, c_ctx, norm_attn_w, norm_mlp_w, w_ada, b_ada, w_in, attn_sink, pool_w, pool_scale, w_out, w_mlp_up, w_mlp_down, final_norm_w, loss_target, m_c_ctx, m_norm_attn_w, m_norm_mlp_w, m_w_ada, m_b_ada, m_w_in, m_attn_sink, m_pool_w, m_pool_scale, m_w_out, m_w_mlp_up, m_w_mlp_down, m_final_norm_w, v_c_ctx, v_norm_attn_w, v_norm_mlp_w, v_w_ada, v_b_ada, v_w_in, v_attn_sink, v_pool_w, v_pool_scale, v_w_out, v_w_mlp_up, v_w_mlp_down, v_final_norm_w):
    _, L, D = x.shape
    H = attn_sink.shape[1]
    A = H * HEAD_DIM
    KV = A // GQA
    P = pool_scale.shape[1]
    MODW = N_MOD * D
    ws = MODW // N_DEV
    gd = P // POOL_GROUPS
    me = _dev_index(lax.axis_index("x"), lax.axis_index("y"), lax.axis_index("c"))

    x2d, ctx2d, tgt = x[0], ctx[0], loss_target[0]
    cctx_row = c_ctx.reshape(1, D)
    wf_row = final_norm_w.reshape(1, D)
    w_ada_l = w_ada[0]
    pool_w_l = pool_w[0].reshape(POOL_GROUPS * (gd // N_DEV), gd)

    (c_all,) = _all_gather("gather_cond", [c])
    cond = jnp.concatenate([c_all[:, 0, :], cctx_row, jnp.zeros((COND_ROWS - N_DEV - 1, D), F32)], axis=0)
    b_sh = lax.dynamic_slice_in_dim(b_ada, me * ws, ws, axis=1)
    (mods_sh,) = _mm("ada_mod", cond, w_ada_l, "nn", [F32], SMALL_TILES, a_pre=_silu, extras=[("n", b_sh)], epilogue=lambda acc, b: (acc + b,))
    (mods_g,) = _all_gather("gather_mods", [mods_sh])

    w_srcs = [w_in[0].astype(BF16), w_out[0].astype(BF16), pool_w_l.astype(BF16), w_mlp_up[0].astype(BF16), w_mlp_down[0].astype(BF16)]
    w_srcs, mods_g = lax.optimization_barrier((w_srcs, mods_g))
    gather_start = _exchange_start("gather_weights_start", "gather_chips", w_srcs)

    def weights(tag, started, lo, hi, after_chips, after_forward):
        gw_send, gw_recv, gw_src, gw_land, _ = started
        mine, lands = _exchange_wait(f"gather_{tag}_wait", "gather_chips", gw_send[lo:hi], gw_recv[lo:hi], gw_src[lo:hi],
                                     gw_land[lo:hi], after_chips)
        f_send, f_recv, _, f_land, f_token = _exchange_start(f"forward_{tag}_start", "forward", None, lands)
        _, lands = _exchange_wait(f"forward_{tag}_wait", "forward", f_send, f_recv, None, f_land,
                                  f_token if after_forward is None else after_forward)
        return [_with_own(l, s[None], me) for l, s in zip(lands, mine)]

    mods = _shards_to_cols(mods_g)
    mod_b = lax.dynamic_slice_in_dim(mods, me, 1, axis=0)
    sh_a, sc_a, g_a, sh_m, sc_m, g_m = [mod_b[:, i * D:(i + 1) * D] for i in range(N_MOD)]
    csh_a, csc_a = mods[N_DEV:N_DEV + 1, :D], mods[N_DEV:N_DEV + 1, D:2 * D]

    cos, sin = _rope_tables(L)
    h = _norm_fwd("norm_attn", x2d, norm_attn_w, sc_a, sh_a)
    hc = _norm_fwd("norm_attn_ctx", ctx2d, norm_attn_w, csc_a, csh_a)
    (win_g,) = weights("w_in", gather_start, 0, 1, h, None)
    W_in = _shards_to_cols(win_g)
    W_qk, W_kv = W_in[:, :A + KV], W_in[:, A:A + 2 * KV]
    W_uv = jnp.concatenate([W_in[:, A + 2 * KV:], W_in[:, A + KV:A + 2 * KV]], axis=1)
    (qk,) = _mm("in_proj_qk", h, W_qk, "nn", [BF16], (1024, A + KV, D), extras=[("m", cos), ("m", sin)],
                epilogue=lambda acc, cs, sn: (_rope(acc, cs, sn),))
    (uv,) = _mm("in_proj_uv", h, W_uv, "nn", [F32], (1024, P + KV, D))
    (kvc,) = _mm("in_proj_ctx", hc, W_kv, "nn", [BF16], SMALL_TILES)
    attn, lse = _attn_fwd(qk, uv, kvc, attn_sink, A, KV, P)
    wout_g, pw_g = weights("w_out", gather_start, 1, 3, qk, attn)
    W_out = wout_g.reshape(A + P, D)
    PW = jnp.transpose(pw_g.reshape(N_DEV, POOL_GROUPS, gd // N_DEV, gd), (1, 0, 2, 3)).reshape(POOL_GROUPS, gd, gd)
    ap, pooled = _pool_fwd(uv, PW, pool_scale, attn)
    o, x1, hm = _mm("out_proj_norm", ap, W_out, "nn", [F32, F32, BF16], (256, D, D), chunk=128, epilogue=_out_proj_epilogue,
                    extras=[("mn", x2d), ("n", g_a), ("n", norm_mlp_w), ("n", sc_m), ("n", sh_m)])
    W_up, wdown_g = weights("w_mlp", gather_start, 3, 5, attn, x1)
    W_down = wdown_g.reshape(-1, D)
    up, act = _mm("mlp_up", hm, W_up, "nn", [F32, BF16], (1024, 1024, 2048), epilogue=lambda acc: (acc, _relu2(acc)), b_shards=True)
    d_x2, d_mlp, d_wf, d_gm, loss_row = _mm(
        "mlp_down_loss", act, W_down, "nn", [F32, BF16], (512, D, 1024), chunk=128, n_sums=3, vmem=FUSED_VMEM_LIMIT,
        epilogue=_mlp_down_epilogue, extras=[("mn", x1), ("mn", tgt), ("n", g_m), ("n", wf_row)])
    loss_p = loss_row[:, :1]

    (d_up,) = _mm("mlp_down_bwd_act", d_mlp, W_down, "nt", [BF16], (1024, 1024, 2048), extras=[("mn", up)],
                  epilogue=lambda acc, uu: (acc * (2.0 * jnp.maximum(uu, 0.0)),))
    (gW_down,) = _mm("mlp_down_bwd_w", act, d_mlp, "tn", [BF16], (1024, 2048, 1024))
    (gW_up_s,) = _mm("mlp_up_bwd_w", hm, d_up, "tn", [BF16], (2048, 1024, 1024), out_shards=True)
    g_mlp_srcs = [gW_up_s, gW_down.reshape(N_DEV, -1, D)]
    g_mlp = _exchange_start("grads_mlp_start", "all_to_all", g_mlp_srcs)
    (d_hm,) = _mm("mlp_up_bwd_act", d_up, W_up, "nt", [F32], (1024, D, 1024), b_shards=True)

    d_x1, d_o, d_ap, s_sh_m, s_sc_m, s_w_nm, d_ga = _norm_bwd_out_proj_bwd(
        x1, d_hm, d_x2, o, norm_mlp_w, sc_m + g_mlp[4][0, 0], g_a, W_out.T)
    (gW_out,) = _mm("out_proj_bwd_w", ap, d_o, "tn", [BF16], (1024, 2048, 1024))
    d_pooled, gPW, d_pscale = _pool_bwd_mix(d_ap, pooled, PW, pool_scale, A // P)
    gpw_s = jnp.transpose(gPW.astype(BF16).reshape(POOL_GROUPS, N_DEV, gd // N_DEV, gd), (1, 0, 2, 3)).reshape(N_DEV, -1, gd)
    g_mix_srcs = [gW_out.reshape(N_DEV, (A + P) // N_DEV, D), gpw_s]
    g_mix = _exchange_start("grads_mix_start", "all_to_all", g_mix_srcs)
    lse = lse + g_mix[4][0, 0]
    d_u = _pool_bwd_window(d_pooled)
    d_q, rd, dsink_q, d_kvc = _attn_bwd_dq(qk, uv, kvc, attn_sink, d_ap, lse, cos, sin, A, KV, P)
    d_k, d_v = _attn_bwd_dkv(qk, uv, d_ap, lse, rd, cos, sin, A, KV, P)
    d_sink = _sum_lanes("sink_grad", dsink_q).reshape(1, H)
    d_p = jnp.concatenate([d_q, d_k, d_v, d_u], axis=1)
    d_kvc_b = d_kvc.astype(BF16)
    (gW_kv_ctx,) = _mm("in_proj_ctx_bwd_w", hc, d_kvc_b, "tn", [F32], SMALL_TILES)
    (d_hc,) = _mm("in_proj_ctx_bwd_act", d_kvc_b, W_kv, "nt", [F32], SMALL_TILES)
    gW_in_init = jnp.pad(gW_kv_ctx, ((0, 0), (A, P)))
    (gW_in,) = _mm("in_proj_bwd_w", h, d_p, "tn", [BF16], (1024, 1280, 1024), extras=[("mn", gW_in_init)], epilogue=lambda acc, init: (acc + init,))
    g_in_srcs = [_cols_to_shards(gW_in)]
    g_in = _exchange_start("grads_in_start", "all_to_all", g_in_srcs)
    grad_x, s_sh_a, s_sc_a, s_w_na = _mm(
        "in_proj_bwd_norm", d_p, W_in, "nt", [F32], (256, D, A + 2 * KV + P), chunk=128, n_sums=3,
        epilogue=lambda acc, xr, dres, w, sc: _modulated_norm_bwd(xr, acc, dres, w, sc),
        extras=[("mn", x2d), ("mn", d_x1), ("n", norm_attn_w), ("n", sc_a + g_in[4][0, 0])])
    s_csh, s_csc, s_w_na = _norm_bwd_sums("norm_attn_ctx_bwd", ctx2d, d_hc, norm_attn_w, csc_a, s_w_na)

    pad_l = lambda a: jnp.pad(a, ((0, 0), (0, LANES - a.shape[1])))
    d_mod_b = jnp.concatenate([s_sh_a, s_sc_a, d_ga, s_sh_m, s_sc_m, d_gm], axis=1)
    summed = jnp.concatenate([s_csh, s_csc, s_w_na, s_w_nm, d_wf, d_pscale, pad_l(d_sink), pad_l(loss_p)], axis=1)
    (small_g,) = _all_gather("gather_small", [jnp.concatenate([d_mod_b, summed], axis=1)])
    small_g = small_g[:, 0, :]
    tot = _sum_rows("small_sum", small_g[:, MODW:])
    off = [0]
    for wdt in (D, D, D, D, D, P, LANES, LANES):
        off.append(off[-1] + wdt)
    seg = lambda i: tot[:, off[i]:off[i + 1]]
    g_norm_attn, g_norm_mlp, g_final, g_pscale = seg(2), seg(3), seg(4), seg(5)
    g_sink, loss = seg(6)[:, :H], seg(7)[0, 0]
    d_mod_ctx = jnp.concatenate([seg(0), seg(1), jnp.zeros((1, MODW - 2 * D), F32)], axis=1)
    d_mod = jnp.concatenate([small_g[:, :MODW], d_mod_ctx, jnp.zeros((COND_ROWS - N_DEV - 1, MODW), F32)], axis=0)
    g_b_ada = _sum_rows("b_ada_grad", d_mod[:N_DEV + 1])
    d_mod_sh = lax.dynamic_slice_in_dim(d_mod, me * ws, ws, axis=1)
    (g_w_ada,) = _mm("ada_bwd_w", cond, d_mod_sh, "tn", [F32], SMALL_TILES, a_pre=_silu)
    (d_cond_p,) = _mm("ada_bwd_cond", d_mod_sh, w_ada_l, "nt", [F32], SMALL_TILES)
    (d_cctx_g,) = _all_gather("gather_cctx", [d_cond_p[N_DEV:N_DEV + 1]])
    g_c_ctx = _silu_grad_mul(cctx_row, _sum_rows("cctx_sum", d_cctx_g[:, 0, :]))

    def arrived(name, started):
        srcs, lands = _exchange_wait(name, "all_to_all", started[0], started[1], started[2], started[3], g_c_ctx)
        return [_with_own(l, lax.dynamic_index_in_dim(s, me, 0, keepdims=True), me) for l, s in zip(lands, srcs)]

    r_up, r_down = arrived("grads_mlp_wait", g_mlp)
    r_out, r_pw = arrived("grads_mix_wait", g_mix)
    (r_in,) = arrived("grads_in_wait", g_in)

    results = {
        "c_ctx": _adamw("adam_c_ctx", cctx_row, g_c_ctx, m_c_ctx.reshape(1, D), v_c_ctx.reshape(1, D)),
        "norm_attn_w": _adamw("adam_norm_attn", norm_attn_w, g_norm_attn, m_norm_attn_w, v_norm_attn_w),
        "norm_mlp_w": _adamw("adam_norm_mlp", norm_mlp_w, g_norm_mlp, m_norm_mlp_w, v_norm_mlp_w),
        "w_ada": _adamw("adam_w_ada", w_ada_l, g_w_ada, m_w_ada[0], v_w_ada[0]),
        "b_ada": _adamw("adam_b_ada", b_ada, g_b_ada, m_b_ada, v_b_ada),
        "w_in": _adamw("adam_w_in", w_in[0], r_in, m_w_in[0], v_w_in[0]),
        "attn_sink": _adamw("adam_sink", attn_sink, g_sink, m_attn_sink, v_attn_sink),
        "pool_w": _adamw("adam_pool_w", pool_w_l, r_pw, m_pool_w[0].reshape(pool_w_l.shape), v_pool_w[0].reshape(pool_w_l.shape)),
        "pool_scale": _adamw("adam_pool_scale", pool_scale, g_pscale, m_pool_scale, v_pool_scale),
        "w_out": _adamw("adam_w_out", w_out[0], r_out, m_w_out[0], v_w_out[0]),
        "w_mlp_up": _adamw("adam_w_up", w_mlp_up[0], r_up, m_w_mlp_up[0], v_w_mlp_up[0]),
        "w_mlp_down": _adamw("adam_w_down", w_mlp_down[0], r_down, m_w_mlp_down[0], v_w_mlp_down[0]),
        "final_norm_w": _adamw("adam_final_norm", wf_row, g_final, m_final_norm_w.reshape(1, D), v_final_norm_w.reshape(1, D)),
    }
    shapes = {"c_ctx": c_ctx.shape, "norm_attn_w": norm_attn_w.shape, "norm_mlp_w": norm_mlp_w.shape, "w_ada": w_ada.shape,
              "b_ada": b_ada.shape, "w_in": w_in.shape, "attn_sink": attn_sink.shape, "pool_w": pool_w.shape,
              "pool_scale": pool_scale.shape, "w_out": w_out.shape, "w_mlp_up": w_mlp_up.shape, "w_mlp_down": w_mlp_down.shape,
              "final_norm_w": final_norm_w.shape}
    outs = [loss, grad_x.reshape(x.shape)]
    for part in range(4):
        outs += [results[name][part].reshape(shape) for name, shape in shapes.items()]
    return tuple(outs)
```

```python
import jax
import jax.numpy as jnp
import numpy as np
from jax import lax
from jax.experimental import pallas as pl
from jax.experimental.pallas import tpu as pltpu

F32 = jnp.float32
BF16 = jnp.bfloat16
I32 = jnp.int32

HEAD_DIM = 64
GQA = 4
BLOCK = 128
GRID_W = 64
ROPE_BASE = 10000.0
POOL_WINDOWS = (2, 4, 8, 16)
POOL_GROUPS = len(POOL_WINDOWS)
HALO = 8
N_MOD = 6
EPS = 1e-6
NEG_INF = -1e30
ADAM_LR = 0.001
ADAM_B1 = 0.9
ADAM_B2 = 0.999
ADAM_EPS = 1e-08
ADAM_WD = 0.01
ADAM_STEP = 10
N_DEV = 8
COND_ROWS = 2 * N_DEV
LANES = 128
SUBLANES_16BIT = 16
VMEM_LIMIT = 48 * 1024 * 1024
FUSED_VMEM_LIMIT = 56 * 1024 * 1024
SMALL_TILES = (512, 1024, 512)
MESH = pl.DeviceIdType.MESH
HBM = pl.BlockSpec(memory_space=pltpu.HBM)
SEM = pl.BlockSpec(memory_space=pltpu.SEMAPHORE)
SIDE_EFFECT = pltpu.CompilerParams(has_side_effects=pltpu.SideEffectType.DATAFLOW_SIDE_EFFECTING)


def _cparams(*sem):
    return pltpu.CompilerParams(dimension_semantics=sem, vmem_limit_bytes=VMEM_LIMIT)


def _tile(n, pref, align):
    if n <= pref:
        return n
    t = (pref // align) * align
    while t >= align:
        if n % t == 0:
            return t
        t -= align
    return n


def _dot(a, b):
    return lax.dot_general(a, b, (((1,), (0,)), ((), ())), preferred_element_type=F32)


def _dot_nt(a, b):
    return lax.dot_general(a, b, (((1,), (1,)), ((), ())), preferred_element_type=F32)


def _dot_tn(a, b):
    return lax.dot_general(a, b, (((0,), (0,)), ((), ())), preferred_element_type=F32)


_DOTS = {"nn": _dot, "nt": _dot_nt, "tn": _dot_tn}


def _mm(name, a, b, mode, out_dtypes, tiles, *, epilogue=None, extras=(), a_pre=None, n_sums=0, chunk=None,
        b_shards=False, out_shards=False, vmem=VMEM_LIMIT):
    if mode == "nn":
        M, K = a.shape
        K2, N = (b.shape[1], N_DEV * b.shape[2]) if b_shards else b.shape
    elif mode == "nt":
        M, K = a.shape
        N, K2 = (b.shape[1], N_DEV * b.shape[2]) if b_shards else b.shape
    else:
        (K, M), (K2, N) = a.shape, b.shape
    assert K == K2 and not (b_shards and mode == "tn"), (name, a.shape, b.shape)
    n_span = N // N_DEV if out_shards or (b_shards and mode == "nn") else N
    k_span = K // N_DEV if b_shards and mode == "nt" else K
    tm = _tile(M, tiles[0], LANES if mode == "tn" else SUBLANES_16BIT)
    tn = _tile(n_span, tiles[1], LANES)
    tk = _tile(k_span, tiles[2], SUBLANES_16BIT if mode == "tn" else LANES)
    nk, nb, kb = K // tk, n_span // tn, k_span // tk
    rows = tm if chunk is None else min(chunk, tm)
    n_ex, n_out = len(extras), len(out_dtypes)
    use_acc = nk > 1 or rows < tm
    assert n_sums == 0 or N == tn, name

    def product(a_ref, b_ref):
        at = a_ref[...]
        if a_pre is not None:
            at = a_pre(at)
        return _DOTS[mode](at.astype(BF16), b_ref[...].astype(BF16))

    def apply(acc, ex, out_refs, sl):
        res = (acc,) if epilogue is None else epilogue(acc, *ex)
        for o_ref, o in zip(out_refs, res[:n_out]):
            o_ref[sl, :] = o.astype(o_ref.dtype)
        return tuple(res[n_out:])

    def finish(acc, ex_refs, out_refs, sum_refs):
        if rows == tm:
            acc = acc if not use_acc else acc[...]
            sums = apply(acc, [r[...] for r in ex_refs], out_refs, slice(None))
        else:
            def one(ci, sums):
                sl = pl.ds(pl.multiple_of(ci * rows, rows), rows)
                ex = [r[...] if kind == "n" else r[sl, :] for (kind, _), r in zip(extras, ex_refs)]
                return tuple(s + v for s, v in zip(sums, apply(acc[sl, :], ex, out_refs, sl)))
            sums = lax.fori_loop(0, tm // rows, one, tuple(jnp.zeros((1, tn), F32) for _ in range(n_sums)))
        first = pl.program_id(0) == 0
        for s_ref, sv in zip(sum_refs, sums):
            @pl.when(first)
            def _(s_ref=s_ref, sv=sv):
                s_ref[...] = sv

            @pl.when(jnp.logical_not(first))
            def _(s_ref=s_ref, sv=sv):
                s_ref[...] += sv

    def body(a_ref, b_ref, *rest):
        ex_refs, out_refs = rest[:n_ex], rest[n_ex:n_ex + n_out]
        sum_refs = rest[n_ex + n_out:n_ex + n_out + n_sums]
        if not use_acc:
            finish(product(a_ref, b_ref), ex_refs, out_refs, sum_refs)
            return
        acc_ref = rest[-1]
        k = pl.program_id(2)

        @pl.when(k == 0)
        def _():
            acc_ref[...] = product(a_ref, b_ref)

        @pl.when(k > 0)
        def _():
            acc_ref[...] += product(a_ref, b_ref)

        @pl.when(k == nk - 1)
        def _():
            finish(acc_ref, ex_refs, out_refs, sum_refs)

    a_spec = pl.BlockSpec((tk, tm), lambda i, j, k: (k, i)) if mode == "tn" else pl.BlockSpec((tm, tk), lambda i, j, k: (i, k))
    if not b_shards:
        b_spec = pl.BlockSpec((tn, tk), lambda i, j, k: (j, k)) if mode == "nt" else pl.BlockSpec((tk, tn), lambda i, j, k: (k, j))
    elif mode == "nn":
        b_spec = pl.BlockSpec((None, tk, tn), lambda i, j, k: (j // nb, k, j % nb))
    else:
        b_spec = pl.BlockSpec((None, tn, tk), lambda i, j, k: (k // kb, j, k % kb))
    ex_specs = []
    for kind, arr in extras:
        if kind == "mn":
            ex_specs.append(pl.BlockSpec((tm, tn), lambda i, j, k: (i, j)))
        elif kind == "n":
            ex_specs.append(pl.BlockSpec((1, tn), lambda i, j, k: (0, j)))
        else:
            ex_specs.append(pl.BlockSpec((tm, arr.shape[1]), lambda i, j, k: (i, 0)))
    if out_shards:
        out_specs = [pl.BlockSpec((None, tm, tn), lambda i, j, k: (j // nb, i, j % nb)) for _ in out_dtypes]
        out_shape = [jax.ShapeDtypeStruct((N_DEV, M, n_span), d) for d in out_dtypes]
    else:
        out_specs = [pl.BlockSpec((tm, tn), lambda i, j, k: (i, j)) for _ in out_dtypes]
        out_shape = [jax.ShapeDtypeStruct((M, N), d) for d in out_dtypes]
    out_specs += [pl.BlockSpec((1, tn), lambda i, j, k: (0, 0))] * n_sums
    out_shape += [jax.ShapeDtypeStruct((1, N), F32)] * n_sums
    return pl.pallas_call(
        body,
        name=name,
        grid=(M // tm, N // tn, nk),
        in_specs=[a_spec, b_spec] + ex_specs,
        out_specs=out_specs,
        out_shape=out_shape,
        scratch_shapes=[pltpu.VMEM((tm, tn), F32)] if use_acc else [],
        compiler_params=pltpu.CompilerParams(
            dimension_semantics=("arbitrary",) * 3 if n_sums else ("parallel", "parallel", "arbitrary"), vmem_limit_bytes=vmem),
    )(a, b, *[arr for _, arr in extras])


def _silu(v):
    return v / (1.0 + jnp.exp(-v))


def _relu2(v):
    r = jnp.maximum(v, 0.0)
    return r * r


def _rope_tables(L):
    half = HEAD_DIM // 2
    inv_freq = np.float32(ROPE_BASE) ** (-np.arange(0, half, 2, dtype=np.float32) / np.float32(half))
    t = np.arange(L)
    row, col = t // GRID_W, t % GRID_W
    ang_r = row.astype(np.float32)[:, None] * inv_freq[None, :]
    ang_c = col.astype(np.float32)[:, None] * inv_freq[None, :]
    cos = np.concatenate([np.cos(ang_r), np.cos(ang_r), np.cos(ang_c), np.cos(ang_c)], axis=1)
    sin = np.concatenate([-np.sin(ang_r), np.sin(ang_r), -np.sin(ang_c), np.sin(ang_c)], axis=1)
    reps = LANES // HEAD_DIM
    return jnp.asarray(np.tile(cos, (1, reps)), F32), jnp.asarray(np.tile(sin, (1, reps)), F32)


def _rope(xf, cos, sin):
    quarter = HEAD_DIM // 4
    lane = lax.broadcasted_iota(I32, (xf.shape[0], LANES), 1)
    first = (lane & quarter) == 0
    outs = []
    for j in range(xf.shape[1] // LANES):
        xc = xf[:, j * LANES:(j + 1) * LANES]
        partner = jnp.where(first, pltpu.roll(xc, LANES - quarter, 1), pltpu.roll(xc, quarter, 1))
        outs.append(xc * cos + partner * sin)
    return outs[0] if len(outs) == 1 else jnp.concatenate(outs, axis=1)


def _inv_rms(xf):
    return lax.rsqrt(jnp.mean(xf * xf, axis=-1, keepdims=True) + EPS)


def _modulated_norm(xf, w, sc, sh):
    return ((xf * _inv_rms(xf)) * w) * (1.0 + sc) + sh


def _modulated_norm_bwd(xf, dh, dres, w, sc):
    r = _inv_rms(xf)
    xh = xf * r
    dn = dh * (1.0 + sc)
    dxh = dn * w
    dx = dres + r * (dxh - xh * jnp.mean(dxh * xh, axis=-1, keepdims=True))
    col = lambda v: jnp.sum(v, axis=0, keepdims=True)
    return dx, col(dh), col(dh * (xh * w)), col(dn * xh)


def _norm_fwd(name, x, w, sc, sh):
    L, D = x.shape
    T = _tile(L, 512, 8)

    def body(x_ref, w_ref, sc_ref, sh_ref, h_ref):
        h_ref[...] = _modulated_norm(x_ref[...], w_ref[...], sc_ref[...], sh_ref[...]).astype(BF16)

    row = pl.BlockSpec((1, D), lambda i: (0, 0))
    return pl.pallas_call(
        body, name=name, grid=(L // T,),
        in_specs=[pl.BlockSpec((T, D), lambda i: (i, 0)), row, row, row],
        out_specs=pl.BlockSpec((T, D), lambda i: (i, 0)),
        out_shape=jax.ShapeDtypeStruct((L, D), BF16),
        compiler_params=_cparams("parallel"),
    )(x, w, sc, sh)


def _norm_bwd_sums(name, x, dh, w, sc, w_init):
    L, D = x.shape
    T = _tile(L, 256, 8)

    def body(x_ref, dh_ref, w_ref, sc_ref, wi_ref, ssh_ref, ssc_ref, sw_ref):
        @pl.when(pl.program_id(0) == 0)
        def _():
            ssh_ref[...] = jnp.zeros_like(ssh_ref)
            ssc_ref[...] = jnp.zeros_like(ssc_ref)
            sw_ref[...] = wi_ref[...]

        dh = dh_ref[...]
        _, s_sh, s_sc, s_w = _modulated_norm_bwd(x_ref[...], dh, jnp.zeros_like(dh), w_ref[...], sc_ref[...])
        ssh_ref[...] += s_sh
        ssc_ref[...] += s_sc
        sw_ref[...] += s_w

    tile = pl.BlockSpec((T, D), lambda i: (i, 0))
    row = pl.BlockSpec((1, D), lambda i: (0, 0))
    return pl.pallas_call(
        body, name=name, grid=(L // T,), in_specs=[tile, tile, row, row, row], out_specs=[row, row, row],
        out_shape=[jax.ShapeDtypeStruct((1, D), F32)] * 3, compiler_params=_cparams("arbitrary"),
    )(x, dh, w, sc, w_init)


def _norm_bwd_out_proj_bwd(x, dh, dres, o, w, sc, g, w_out_t, A):
    L, D = x.shape
    N = w_out_t.shape[1]
    T = _tile(L, 256, SUBLANES_16BIT)
    half = T // 2

    def body(x_ref, dh_ref, dres_ref, o_ref, w_ref, sc_ref, g_ref, wt_ref, dx_ref, do_ref, dattn_ref, dpool_ref,
             ssh_ref, ssc_ref, sw_ref, sg_ref):
        @pl.when(pl.program_id(0) == 0)
        def _():
            for s_ref in (ssh_ref, ssc_ref, sw_ref, sg_ref):
                s_ref[...] = jnp.zeros_like(s_ref)

        for rows in (slice(0, half), slice(half, T)):
            dx, s_sh, s_sc, s_w = _modulated_norm_bwd(x_ref[rows, :], dh_ref[rows, :], dres_ref[rows, :], w_ref[...], sc_ref[...])
            ssh_ref[...] += s_sh
            ssc_ref[...] += s_sc
            sw_ref[...] += s_w
            sg_ref[...] += jnp.sum(dx * o_ref[rows, :], axis=0, keepdims=True)
            dx_ref[rows, :] = dx
            do_ref[rows, :] = (g_ref[...] * dx).astype(BF16)
        dap = _dot(do_ref[...], wt_ref[...])
        dattn_ref[...] = dap[:, :A].astype(BF16)
        dpool_ref[...] = dap[:, A:]

    tile = pl.BlockSpec((T, D), lambda i: (i, 0))
    row = pl.BlockSpec((1, D), lambda i: (0, 0))
    return pl.pallas_call(
        body, name="norm_mlp_bwd_out_proj_bwd", grid=(L // T,),
        in_specs=[tile, tile, tile, tile, row, row, row, pl.BlockSpec((D, N), lambda i: (0, 0))],
        out_specs=[tile, tile, pl.BlockSpec((T, A), lambda i: (i, 0)), pl.BlockSpec((T, N - A), lambda i: (i, 0)), row, row, row, row],
        out_shape=[jax.ShapeDtypeStruct((L, D), F32), jax.ShapeDtypeStruct((L, D), BF16), jax.ShapeDtypeStruct((L, A), BF16),
                   jax.ShapeDtypeStruct((L, N - A), F32)] + [jax.ShapeDtypeStruct((1, D), F32)] * 4,
        compiler_params=pltpu.CompilerParams(dimension_semantics=("arbitrary",), vmem_limit_bytes=FUSED_VMEM_LIMIT),
    )(x, dh, dres, o, w, sc, g, w_out_t)


def _out_proj_epilogue(acc, xr, g, w, sc, sh):
    x1 = xr + g * acc
    return acc, x1, _modulated_norm(x1, w, sc, sh)


def _mlp_down_epilogue(acc, x1, tgt, g, wf):
    D = acc.shape[1]
    x2 = x1 + g * acc
    r = _inv_rms(x2)
    xh = x2 * r
    err = xh * wf - tgt
    loss = 0.5 * jnp.sum(jnp.mean(err * err, axis=-1, keepdims=True), axis=0, keepdims=True)
    dy = err * (1.0 / D)
    dxh = dy * wf
    dx = r * (dxh - xh * jnp.mean(dxh * xh, axis=-1, keepdims=True))
    col = lambda v: jnp.sum(v, axis=0, keepdims=True)
    return dx, g * dx, col(dy * xh), col(dx * acc), jnp.broadcast_to(loss, (1, D))


def _heads(ref, first, n):
    return jnp.concatenate([ref[:, (first + g) * HEAD_DIM:(first + g + 1) * HEAD_DIM] for g in range(n)], axis=0)


def _attn_mask(n, L, C):
    shape = (3 * BLOCK + C, GQA * BLOCK)
    kj = lax.broadcasted_iota(I32, shape, 0)
    qi = lax.broadcasted_iota(I32, shape, 1) & (BLOCK - 1)
    kpos = n * BLOCK - BLOCK + kj
    window = (kj >= qi) & (kj <= qi + 2 * BLOCK) & (kpos >= 0) & (kpos < L)
    return window | (kj >= 3 * BLOCK)


def _head_rows(ref, hk):
    return jnp.concatenate([ref[hk * GQA + g:hk * GQA + g + 1, :] for g in range(GQA)], axis=1)


def _rows_to_heads(rows_by_kv_head):
    return jnp.concatenate([r[:, g * BLOCK:(g + 1) * BLOCK] for r in rows_by_kv_head for g in range(GQA)], axis=0)


def _queries_to_rows(t):
    return jnp.concatenate([t[:, g * BLOCK:(g + 1) * BLOCK].T for g in range(GQA)], axis=1)


def _attn_specs(L, A, KV, C, vcol):
    nb = L // BLOCK
    kcol = A // KV
    prev = lambda n: jnp.maximum(n - 1, 0)
    nxt = lambda n: jnp.minimum(n + 1, nb - 1)
    q_spec = pl.BlockSpec((BLOCK, A), lambda n: (n, 0))
    k_specs = [pl.BlockSpec((BLOCK, KV), lambda n: (prev(n), kcol)), pl.BlockSpec((BLOCK, KV), lambda n: (n, kcol)),
               pl.BlockSpec((BLOCK, KV), lambda n: (nxt(n), kcol))]
    v_specs = [pl.BlockSpec((BLOCK, KV), lambda n: (prev(n), vcol)), pl.BlockSpec((BLOCK, KV), lambda n: (n, vcol)),
               pl.BlockSpec((BLOCK, KV), lambda n: (nxt(n), vcol))]
    kvc_spec = pl.BlockSpec((C, 2 * KV), lambda n: (0, 0))
    return q_spec, k_specs, v_specs, kvc_spec


def _keys_values(hk, k_refs, v_refs, kvc_ref, KV):
    sl = slice(hk * HEAD_DIM, (hk + 1) * HEAD_DIM)
    keys = jnp.concatenate([r[:, sl] for r in k_refs] + [kvc_ref[:, sl]], axis=0)
    vals = jnp.concatenate([r[:, sl].astype(BF16) for r in v_refs] + [kvc_ref[:, KV + hk * HEAD_DIM:KV + (hk + 1) * HEAD_DIM]], axis=0)
    return keys, vals


def _sink_row(sink_ref, hk):
    return jnp.concatenate([jnp.full((1, BLOCK), sink_ref[0, hk * GQA + g], F32) for g in range(GQA)], axis=1)


def _attn_fwd(qk, uv, kvc, sink, A, KV, P):
    L = qk.shape[0]
    C = kvc.shape[0]
    nkv = KV // HEAD_DIM
    H = nkv * GQA
    scale = HEAD_DIM ** -0.5

    def body(sink_ref, q_ref, kp_ref, kc_ref, kn_ref, vp_ref, vc_ref, vn_ref, kvc_ref, o_ref, lse_ref):
        valid = _attn_mask(pl.program_id(0), L, C)
        lse_rows = []
        for hk in range(nkv):
            keys, vals = _keys_values(hk, (kp_ref, kc_ref, kn_ref), (vp_ref, vc_ref, vn_ref), kvc_ref, KV)
            qs = _heads(q_ref, hk * GQA, GQA) * scale
            s = jnp.where(valid, _dot_nt(keys, qs), NEG_INF)
            sk = _sink_row(sink_ref, hk)
            m = jnp.maximum(jnp.max(s, axis=0, keepdims=True), sk)
            p = jnp.exp(s - m)
            den = jnp.sum(p, axis=0, keepdims=True) + jnp.exp(sk - m)
            o = _dot_tn(vals, p.astype(BF16)) * (1.0 / den)
            lse_rows.append(m + jnp.log(den))
            o_ref[:, hk * GQA * HEAD_DIM:(hk + 1) * GQA * HEAD_DIM] = _queries_to_rows(o).astype(BF16)
        lse_ref[...] = _rows_to_heads(lse_rows)

    q_spec, k_specs, v_specs, kvc_spec = _attn_specs(L, A, KV, C, P // KV)
    return pl.pallas_call(
        body, name="attn_fwd", grid=(L // BLOCK,),
        in_specs=[pl.BlockSpec(memory_space=pltpu.SMEM), q_spec] + k_specs + v_specs + [kvc_spec],
        out_specs=[pl.BlockSpec((BLOCK, A), lambda n: (n, 0)), pl.BlockSpec((H, BLOCK), lambda n: (0, n))],
        out_shape=[jax.ShapeDtypeStruct((L, A + P), BF16), jax.ShapeDtypeStruct((H, L), F32)],
        compiler_params=_cparams("parallel"),
    )(sink, qk, qk, qk, qk, uv, uv, uv, kvc)


def _attn_bwd_dq(qk, uv, kvc, sink, dap, lse_t, cos, sin, A, KV, P):
    L = qk.shape[0]
    C = kvc.shape[0]
    nkv = KV // HEAD_DIM
    H = nkv * GQA
    scale = HEAD_DIM ** -0.5
    W = 3 * BLOCK

    def body(sink_ref, q_ref, kp_ref, kc_ref, kn_ref, vp_ref, vc_ref, vn_ref, kvc_ref, do_ref, lse_ref, cos_ref, sin_ref,
             dq_ref, rd_ref, ds_ref, dkvc_ref):
        n = pl.program_id(0)

        @pl.when(n == 0)
        def _():
            dkvc_ref[...] = jnp.zeros_like(dkvc_ref)

        valid = _attn_mask(n, L, C)
        rd_rows, dsink_rows, dq_parts = [], [], []
        for hk in range(nkv):
            sl = slice(hk * HEAD_DIM, (hk + 1) * HEAD_DIM)
            keys, vals = _keys_values(hk, (kp_ref, kc_ref, kn_ref), (vp_ref, vc_ref, vn_ref), kvc_ref, KV)
            qs = _heads(q_ref, hk * GQA, GQA) * scale
            dos = _heads(do_ref, hk * GQA, GQA).astype(BF16)
            lse = _head_rows(lse_ref, hk)
            p = jnp.exp(jnp.where(valid, _dot_nt(keys, qs), NEG_INF) - lse)
            dp = _dot_nt(vals, dos)
            rd = jnp.sum(p * dp, axis=0, keepdims=True)
            ds = (p * (dp - rd)).astype(BF16)
            dq_parts.append(_queries_to_rows(_dot_tn(keys, ds) * scale))
            dkvc_ref[:, sl] += _dot(ds[W:, :], qs)
            dkvc_ref[:, KV + hk * HEAD_DIM:KV + (hk + 1) * HEAD_DIM] += _dot(p[W:, :].astype(BF16), dos)
            rd_rows.append(rd)
            dsink_rows.append(-(jnp.exp(_sink_row(sink_ref, hk) - lse) * rd))
        rd_ref[...] = _rows_to_heads(rd_rows)
        ds_ref[...] = _rows_to_heads(dsink_rows)
        dq = dq_parts[0] if nkv == 1 else jnp.concatenate(dq_parts, axis=1)
        dq_ref[...] = _rope(dq, cos_ref[...], -sin_ref[...]).astype(BF16)

    q_spec, k_specs, v_specs, kvc_spec = _attn_specs(L, A, KV, C, P // KV)
    blk = lambda w: pl.BlockSpec((BLOCK, w), lambda n: (n, 0))
    per_head = pl.BlockSpec((H, BLOCK), lambda n: (0, n))
    return pl.pallas_call(
        body, name="attn_bwd_dq", grid=(L // BLOCK,),
        in_specs=[pl.BlockSpec(memory_space=pltpu.SMEM), q_spec] + k_specs + v_specs + [kvc_spec, blk(A), per_head, blk(LANES), blk(LANES)],
        out_specs=[blk(A), per_head, per_head, pl.BlockSpec((C, 2 * KV), lambda n: (0, 0))],
        out_shape=[jax.ShapeDtypeStruct((L, A), BF16), jax.ShapeDtypeStruct((H, L), F32), jax.ShapeDtypeStruct((H, L), F32),
                   jax.ShapeDtypeStruct((C, 2 * KV), F32)],
        compiler_params=_cparams("arbitrary"),
    )(sink, qk, qk, qk, qk, uv, uv, uv, kvc, dap, lse_t, cos, sin)


def _attn_bwd_dkv(qk, uv, dap, lse_t, rd_t, cos, sin, A, KV, P):
    L = qk.shape[0]
    nb = L // BLOCK
    nkv = KV // HEAD_DIM
    H = nkv * GQA
    scale = HEAD_DIM ** -0.5
    R = 3 * GQA * BLOCK

    def body(k_ref, v_ref, qp_ref, qc_ref, qn_ref, dop_ref, doc_ref, don_ref, lsep_ref, lsec_ref, lsen_ref,
             rdp_ref, rdc_ref, rdn_ref, cos_ref, sin_ref, dk_ref, dv_ref):
        m = pl.program_id(0)
        kj = lax.broadcasted_iota(I32, (BLOCK, R), 0)
        col = lax.broadcasted_iota(I32, (BLOCK, R), 1)
        part = col // (GQA * BLOCK)
        qi = col & (BLOCK - 1)
        before = jnp.where(m >= 1, 0, -2 * BLOCK)
        after = jnp.where(m <= nb - 2, 0, 2 * BLOCK)
        valid = ((part == 0) & (kj <= qi + before)) | (part == 1) | ((part == 2) & (kj >= qi + after))
        dk_parts, dv_parts = [], []
        for hk in range(nkv):
            sl = slice(hk * HEAD_DIM, (hk + 1) * HEAD_DIM)
            km = k_ref[:, sl]
            vm = v_ref[:, sl].astype(BF16)
            qs = jnp.concatenate([_heads(q, hk * GQA, GQA) for q in (qp_ref, qc_ref, qn_ref)], axis=0) * scale
            dos = jnp.concatenate([_heads(d, hk * GQA, GQA) for d in (dop_ref, doc_ref, don_ref)], axis=0).astype(BF16)
            rows = [slice(hk * GQA + g, hk * GQA + g + 1) for g in range(GQA)]
            lse = jnp.concatenate([t[r, :] for t in (lsep_ref, lsec_ref, lsen_ref) for r in rows], axis=1)
            rdv = jnp.concatenate([t[r, :] for t in (rdp_ref, rdc_ref, rdn_ref) for r in rows], axis=1)
            p = jnp.exp(jnp.where(valid, _dot_nt(km, qs), NEG_INF) - lse)
            ds = (p * (_dot_nt(vm, dos) - rdv)).astype(BF16)
            dk_parts.append(_dot(ds, qs))
            dv_parts.append(_dot(p.astype(BF16), dos))
        dk = dk_parts[0] if nkv == 1 else jnp.concatenate(dk_parts, axis=1)
        dv = dv_parts[0] if nkv == 1 else jnp.concatenate(dv_parts, axis=1)
        dk_ref[...] = _rope(dk, cos_ref[...], -sin_ref[...]).astype(BF16)
        dv_ref[...] = dv.astype(BF16)

    prev = lambda m: jnp.maximum(m - 1, 0)
    nxt = lambda m: jnp.minimum(m + 1, nb - 1)
    three = lambda w: [pl.BlockSpec((BLOCK, w), lambda m: (prev(m), 0)), pl.BlockSpec((BLOCK, w), lambda m: (m, 0)),
                       pl.BlockSpec((BLOCK, w), lambda m: (nxt(m), 0))]
    three_t = [pl.BlockSpec((H, BLOCK), lambda m: (0, prev(m))), pl.BlockSpec((H, BLOCK), lambda m: (0, m)),
               pl.BlockSpec((H, BLOCK), lambda m: (0, nxt(m)))]
    blk = lambda w: pl.BlockSpec((BLOCK, w), lambda m: (m, 0))
    return pl.pallas_call(
        body, name="attn_bwd_dkv", grid=(nb,),
        in_specs=[pl.BlockSpec((BLOCK, KV), lambda m: (m, A // KV)), pl.BlockSpec((BLOCK, KV), lambda m: (m, P // KV))]
                 + three(A) + three(A) + three_t + three_t + [blk(LANES), blk(LANES)],
        out_specs=[blk(KV), blk(KV)],
        out_shape=[jax.ShapeDtypeStruct((L, KV), BF16), jax.ShapeDtypeStruct((L, KV), BF16)],
        compiler_params=_cparams("parallel"),
    )(qk, uv, qk, qk, qk, dap, dap, dap, lse_t, lse_t, lse_t, rd_t, rd_t, rd_t, cos, sin)


def _halo_specs(T, L, W, col):
    per = T // HALO
    return [pl.BlockSpec((HALO, W), lambda i: (jnp.maximum(i * per - 1, 0), col)),
            pl.BlockSpec((T, W), lambda i: (i, col)),
            pl.BlockSpec((HALO, W), lambda i: (jnp.minimum((i + 1) * per, L // HALO - 1), col))]


def _fill_halo_buf(buf, prev_ref, cur_ref, next_ref, i, nt, T):
    buf[0:HALO, :] = jnp.where(i > 0, prev_ref[...], 0.0)
    buf[HALO:HALO + T, :] = cur_ref[...]
    buf[HALO + T:2 * HALO + T, :] = jnp.where(i < nt - 1, next_ref[...], 0.0)


def _zero_margins(lv):
    rows = lv.shape[0]
    lv[0:HALO, :] = jnp.zeros((HALO, lv.shape[1]), F32)
    lv[rows - HALO:rows, :] = jnp.zeros((HALO, lv.shape[1]), F32)


def _window_sums(lv, x, w, first):
    n = x.shape[0]
    lv[HALO:HALO + n, :] = x
    cur = x + lv[pl.ds(HALO + first, n), :]
    span = 1
    while 2 * span < w:
        lv[HALO:HALO + n, :] = cur
        cur = lv[pl.ds(HALO - span, n), :] + lv[pl.ds(HALO + span, n), :]
        span *= 2
    return cur


def _counts(t, w, L):
    lo = jnp.clip(t - w // 2, 0, L)
    hi = jnp.clip(t - w // 2 + w, 0, L)
    return jnp.maximum(hi - lo, 1).astype(F32)


def _pool_fwd(u, pw, scale, mix):
    L, P = u.shape[0], scale.shape[1]
    gd = P // POOL_GROUPS
    T = _tile(L, 256, 8)
    nt = L // T
    assert (mix.shape[1] - P) % P == 0
    mix_col = mix.shape[1] // P - 1

    def body(up_ref, uc_ref, un_ref, pw_ref, sc_ref, mix_ref, out_ref, pooled_ref, buf, lv):
        i = pl.program_id(0)
        _fill_halo_buf(buf, up_ref, uc_ref, un_ref, i, nt, T)
        _zero_margins(lv)
        t = i * T + lax.broadcasted_iota(I32, (T, 1), 0)
        for g, w in enumerate(POOL_WINDOWS):
            cols = slice(g * gd, (g + 1) * gd)
            acc = _window_sums(lv, buf[:, cols], w, -1)[HALO:HALO + T]
            pooled = (acc / _counts(t, w, L) - uc_ref[:, cols]).astype(BF16)
            pooled_ref[:, cols] = pooled
            out_ref[:, cols] = (_dot(pooled, pw_ref[g]) * sc_ref[:, cols]).astype(BF16)

    return pl.pallas_call(
        body, name="pool_fwd", grid=(nt,),
        in_specs=_halo_specs(T, L, P, 0) + [pl.BlockSpec((POOL_GROUPS, gd, gd), lambda i: (0, 0, 0)), pl.BlockSpec((1, P), lambda i: (0, 0)),
                                            pl.BlockSpec(memory_space=pl.ANY)],
        out_specs=[pl.BlockSpec((T, P), lambda i: (i, mix_col)), pl.BlockSpec((T, P), lambda i: (i, 0))],
        out_shape=[jax.ShapeDtypeStruct(mix.shape, BF16), jax.ShapeDtypeStruct((L, P), BF16)],
        scratch_shapes=[pltpu.VMEM((T + 2 * HALO, P), F32), pltpu.VMEM((T + 4 * HALO, gd), F32)],
        input_output_aliases={5: 0},
        compiler_params=_cparams("parallel"),
    )(u, u, u, pw, scale, mix)


def _pool_bwd_mix(d_pool, pooled, pw, scale):
    L, P = pooled.shape
    gd = P // POOL_GROUPS
    T = _tile(L, 256, 8)

    def body(dp_ref, pooled_ref, pw_ref, sc_ref, dpooled_ref, dpw_ref, dsc_ref):
        i = pl.program_id(0)

        @pl.when(i == 0)
        def _():
            dpw_ref[...] = jnp.zeros_like(dpw_ref)
            dsc_ref[...] = jnp.zeros_like(dsc_ref)

        for g in range(POOL_GROUPS):
            cols = slice(g * gd, (g + 1) * gd)
            pb = pooled_ref[:, cols]
            dp = dp_ref[:, cols]
            dsc_ref[:, cols] += jnp.sum(dp * _dot(pb, pw_ref[g]), axis=0, keepdims=True)
            dm = (dp * sc_ref[:, cols]).astype(BF16)
            dpw_ref[g] += _dot_tn(pb, dm)
            dpooled_ref[:, cols] = _dot_nt(dm, pw_ref[g])

    return pl.pallas_call(
        body, name="pool_bwd_mix", grid=(L // T,),
        in_specs=[pl.BlockSpec((T, P), lambda i: (i, 0)), pl.BlockSpec((T, P), lambda i: (i, 0)),
                  pl.BlockSpec((POOL_GROUPS, gd, gd), lambda i: (0, 0, 0)), pl.BlockSpec((1, P), lambda i: (0, 0))],
        out_specs=[pl.BlockSpec((T, P), lambda i: (i, 0)), pl.BlockSpec((POOL_GROUPS, gd, gd), lambda i: (0, 0, 0)),
                   pl.BlockSpec((1, P), lambda i: (0, 0))],
        out_shape=[jax.ShapeDtypeStruct((L, P), F32), jax.ShapeDtypeStruct((POOL_GROUPS, gd, gd), F32), jax.ShapeDtypeStruct((1, P), F32)],
        compiler_params=_cparams("arbitrary"),
    )(d_pool, pooled, pw, scale)


def _pool_bwd_window(dpooled):
    L, P = dpooled.shape
    gd = P // POOL_GROUPS
    T = _tile(L, 256, 8)
    nt = L // T

    def body(dp_ref, dc_ref, dn_ref, du_ref, buf, lv):
        i = pl.program_id(0)
        _fill_halo_buf(buf, dp_ref, dc_ref, dn_ref, i, nt, T)
        _zero_margins(lv)
        t = i * T - HALO + lax.broadcasted_iota(I32, (T + 2 * HALO, 1), 0)
        for g, w in enumerate(POOL_WINDOWS):
            cols = slice(g * gd, (g + 1) * gd)
            acc = _window_sums(lv, buf[:, cols] / _counts(t, w, L), w, 1)[HALO:HALO + T]
            du_ref[:, cols] = (acc - dc_ref[:, cols]).astype(BF16)

    return pl.pallas_call(
        body, name="pool_bwd_window", grid=(nt,),
        in_specs=_halo_specs(T, L, P, 0),
        out_specs=pl.BlockSpec((T, P), lambda i: (i, 0)),
        out_shape=jax.ShapeDtypeStruct((L, P), BF16),
        scratch_shapes=[pltpu.VMEM((T + 2 * HALO, P), F32), pltpu.VMEM((T + 4 * HALO, gd), F32)],
        compiler_params=_cparams("parallel"),
    )(dpooled, dpooled, dpooled)


def _sum_rows(name, a):
    R, N = a.shape

    def body(a_ref, o_ref):
        if R <= 16:
            acc = a_ref[0:1, :]
            for r in range(1, R):
                acc = acc + a_ref[r:r + 1, :]
        else:
            acc = jnp.sum(a_ref[...], axis=0, keepdims=True)
        o_ref[...] = acc

    return pl.pallas_call(body, name=name, out_shape=jax.ShapeDtypeStruct((1, N), F32))(a)


def _sum_lanes(name, a):
    def body(a_ref, o_ref):
        o_ref[...] = jnp.sum(a_ref[...], axis=1, keepdims=True)

    return pl.pallas_call(body, name=name, out_shape=jax.ShapeDtypeStruct((a.shape[0], 1), F32))(a)


def _silu_grad_mul(cv, g):
    def body(c_ref, g_ref, o_ref):
        cvv = c_ref[...]
        s = 1.0 / (1.0 + jnp.exp(-cvv))
        o_ref[...] = g_ref[...] * (s * (1.0 + cvv * (1.0 - s)))

    return pl.pallas_call(body, name="silu_grad_mul", out_shape=jax.ShapeDtypeStruct(cv.shape, F32))(cv, g)


def _adamw(name, w, g, m, v):
    R, C = w.shape
    parts = g.ndim == 3
    n_parts = g.shape[0] if parts else 1
    T = _tile(R, max(8, 262144 // C), 8)

    def body(w_ref, g_ref, m_ref, v_ref, go_ref, d_ref, mo_ref, vo_ref):
        if parts:
            gv = g_ref[0].astype(F32)
            for p in range(1, n_parts):
                gv = gv + g_ref[p].astype(F32)
        else:
            gv = g_ref[...]
        mn = ADAM_B1 * m_ref[...] + (1.0 - ADAM_B1) * gv
        vn = ADAM_B2 * v_ref[...] + (1.0 - ADAM_B2) * (gv * gv)
        m_hat = mn / (1.0 - ADAM_B1 ** ADAM_STEP)
        v_hat = vn / (1.0 - ADAM_B2 ** ADAM_STEP)
        go_ref[...] = gv
        d_ref[...] = -ADAM_LR * (m_hat / (jnp.sqrt(v_hat) + ADAM_EPS) + ADAM_WD * w_ref[...])
        mo_ref[...] = mn
        vo_ref[...] = vn

    tile = pl.BlockSpec((T, C), lambda i: (i, 0))
    g_spec = pl.BlockSpec((n_parts, T, C), lambda i: (0, i, 0)) if parts else tile
    return pl.pallas_call(
        body, name=name, grid=(R // T,),
        in_specs=[tile, g_spec, tile, tile], out_specs=[tile] * 4,
        out_shape=[jax.ShapeDtypeStruct((R, C), F32)] * 4,
        compiler_params=_cparams("parallel"),
    )(w, g, m, v)


def _dev_index(px, py, pc):
    return 4 * px + 2 * py + pc


def _all_gather(name, arrs):
    n = len(arrs)

    def body(*refs):
        ins, outs = refs[:n], refs[n:2 * n]
        send_sems, recv_sems, local_sems = refs[2 * n:]
        x, y, c = lax.axis_index("x"), lax.axis_index("y"), lax.axis_index("c")
        me, sibling = (x, y, c), (x, y, 1 - c)
        chips = [(1 - x, y), (x, 1 - y), (1 - x, 1 - y)]

        def copy(a, k, block, to, src=None):
            slot = outs[a].at[_dev_index(*block)]
            return pltpu.make_async_remote_copy(
                src_ref=slot if src is None else src, dst_ref=slot, send_sem=send_sems.at[a, k], recv_sem=recv_sems.at[a, k],
                device_id=to, device_id_type=MESH)

        mine = [pltpu.make_async_copy(ins[a], outs[a].at[_dev_index(*me)], local_sems.at[a]) for a in range(n)]
        for cp in mine:
            cp.start()
        first = []
        for a in range(n):
            first.append(copy(a, 0, me, sibling, src=ins[a]))
            first += [copy(a, 1 + j, me, (*chip, c), src=ins[a]) for j, chip in enumerate(chips)]
        for cp in first:
            cp.start()
        passed = []
        for j, chip in enumerate(chips):
            for a in range(n):
                copy(a, 1 + j, (*chip, c), me).wait_recv()
                fwd = copy(a, 4 + j, (*chip, c), sibling)
                fwd.start()
                passed.append(fwd)
        for a in range(n):
            copy(a, 0, sibling, me).wait_recv()
            for j, chip in enumerate(chips):
                copy(a, 4 + j, (*chip, 1 - c), me).wait_recv()
        for cp in first + passed:
            cp.wait_send()
        for cp in mine:
            cp.wait()

    return pl.pallas_call(
        body, name=name,
        in_specs=[HBM] * n, out_specs=[HBM] * n,
        out_shape=[jax.ShapeDtypeStruct((N_DEV, *a.shape), a.dtype) for a in arrs],
        scratch_shapes=[pltpu.SemaphoreType.DMA((n, N_DEV - 1)), pltpu.SemaphoreType.DMA((n, N_DEV - 1)), pltpu.SemaphoreType.DMA((n,))],
    )(*arrs)


N_COPIES = {"all_to_all": N_DEV - 1, "gather_chips": 4, "forward": 3}


def _exchange_copies(kind, src_ref, land_ref, send_sems, recv_sems, sending):
    x, y, c = lax.axis_index("x"), lax.axis_index("y"), lax.axis_index("c")
    me = _dev_index(x, y, c)
    others = [(1 - x, y), (x, 1 - y), (1 - x, 1 - y)]
    if kind == "all_to_all":
        flips = [(dx, dy, dc) for dx in (0, 1) for dy in (0, 1) for dc in (0, 1)][1:]
        peers = [(1 - x if dx else x, 1 - y if dy else y, 1 - c if dc else c) for dx, dy, dc in flips]
        plan = [(p, src_ref.at[_dev_index(*p)], me if sending else _dev_index(*p)) for p in peers]
    elif kind == "gather_chips":
        peers = [(x, y, 1 - c)] + [(*o, c) for o in others]
        plan = [(p, src_ref, me if sending else _dev_index(*p)) for p in peers]
    else:
        plan = [((x, y, 1 - c), land_ref.at[_dev_index(*o, c)], _dev_index(*o, c if sending else 1 - c)) for o in others]
    return [pltpu.make_async_remote_copy(src_ref=src, dst_ref=land_ref.at[slot], send_sem=send_sems.at[k], recv_sem=recv_sems.at[k],
                                         device_id=peer, device_id_type=MESH)
            for k, (peer, src, slot) in enumerate(plan)]


def _exchange_start(name, kind, srcs, lands=None):
    if lands is None:
        lands = [lax.empty((N_DEV, *s.shape) if kind == "gather_chips" else s.shape, s.dtype) for s in srcs]
    n = len(lands)
    ops = ([] if srcs is None else list(srcs)) + list(lands)
    m = len(ops)

    def body(*refs):
        src_refs = [None] * n if srcs is None else refs[:n]
        land_refs = refs[m - n:m]
        send_sems, recv_sems, token = refs[m:m + n], refs[m + n:m + 2 * n], refs[-1]
        for a in range(n):
            for cp in _exchange_copies(kind, src_refs[a], land_refs[a], send_sems[a], recv_sems[a], True):
                cp.start()
        token[...] = jnp.zeros_like(token)

    sems = [pltpu.SemaphoreType.DMA((N_COPIES[kind],))] * (2 * n)
    outs = pl.pallas_call(
        body, name=name,
        out_shape=sems + [pltpu.HBM(o.shape, o.dtype) for o in ops] + [jax.ShapeDtypeStruct((8, LANES), F32)],
        in_specs=[HBM] * m,
        out_specs=[SEM] * (2 * n) + [HBM] * m + [pl.BlockSpec(memory_space=pltpu.VMEM)],
        input_output_aliases={i: 2 * n + i for i in range(m)},
        compiler_params=SIDE_EFFECT,
    )(*[pltpu.with_memory_space_constraint(o, pltpu.HBM) for o in ops])
    thru = outs[2 * n:2 * n + m]
    return outs[:n], outs[n:2 * n], (None if srcs is None else thru[:n]), thru[m - n:], outs[-1]


def _exchange_wait(name, kind, send_sems, recv_sems, srcs, lands, after):
    n = len(lands)
    ops = ([] if srcs is None else list(srcs)) + list(lands)
    m = len(ops)

    def body(*refs):
        src_refs = [None] * n if srcs is None else refs[:n]
        land_refs = refs[m - n:m]
        send_refs, recv_refs = refs[m:m + n], refs[m + n:m + 2 * n]
        for a in range(n):
            for cp in _exchange_copies(kind, src_refs[a], land_refs[a], send_refs[a], recv_refs[a], False):
                cp.wait_send()
                cp.wait_recv()

    outs = pl.pallas_call(
        body, name=name,
        out_shape=[pltpu.HBM(o.shape, o.dtype) for o in ops],
        in_specs=[HBM] * m + [SEM] * (2 * n) + [pl.BlockSpec(memory_space=pl.ANY)],
        out_specs=[HBM] * m,
        input_output_aliases={i: i for i in range(m)},
        compiler_params=SIDE_EFFECT,
    )(*ops, *send_sems, *recv_sems, after)
    return (None if srcs is None else outs[:n]), outs[m - n:]


def _with_own(land, own, me):
    return lax.dynamic_update_slice_in_dim(land, own, me, 0)


def _shards_to_cols(g):
    return jnp.transpose(g, (1, 0, 2)).reshape(g.shape[1], N_DEV * g.shape[2])


def _cols_to_shards(a):
    R, Ctot = a.shape
    return jnp.transpose(a.reshape(R, N_DEV, Ctot // N_DEV), (1, 0, 2))


def kernel(x, c, ctx, c_ctx, norm_attn_w, norm_mlp_w, w_ada, b_ada, w_in, attn_sink, pool_w, pool_scale, w_out, w_mlp_up, w_mlp_down, final_norm_w, loss_target, m_c_ctx, m_norm_attn_w, m_norm_mlp_w, m_w_ada, m_b_ada, m_w_in, m_attn_sink, m_pool_w, m_pool_scale, m_w_out, m_w_mlp_up, m_w_mlp_down, m_final_norm_w, v_c_ctx, v_norm_attn_w, v_norm_mlp_w, v_w_ada, v_b_ada, v_w_in, v_attn_sink, v_pool_w, v_pool_scale, v_w_out, v_w_mlp_up, v_w_mlp_down, v_final_norm_w):
    _, L, D = x.shape
    H = attn_sink.shape[1]
    A = H * HEAD_DIM
    KV = A // GQA
    P = pool_scale.shape[1]
    MODW = N_MOD * D
    ws = MODW // N_DEV
    gd = P // POOL_GROUPS
    me = _dev_index(lax.axis_index("x"), lax.axis_index("y"), lax.axis_index("c"))

    x2d, ctx2d, tgt = x[0], ctx[0], loss_target[0]
    cctx_row = c_ctx.reshape(1, D)
    wf_row = final_norm_w.reshape(1, D)
    w_ada_l = w_ada[0]
    pool_w_l = pool_w[0].reshape(POOL_GROUPS * (gd // N_DEV), gd)

    (c_all,) = _all_gather("gather_cond", [c])
    cond = jnp.concatenate([c_all[:, 0, :], cctx_row, jnp.zeros((COND_ROWS - N_DEV - 1, D), F32)], axis=0)
    b_sh = lax.dynamic_slice_in_dim(b_ada, me * ws, ws, axis=1)
    (mods_sh,) = _mm("ada_mod", cond, w_ada_l, "nn", [F32], SMALL_TILES, a_pre=_silu, extras=[("n", b_sh)], epilogue=lambda acc, b: (acc + b,))
    (mods_g,) = _all_gather("gather_mods", [mods_sh])

    w_srcs = [w_in[0].astype(BF16), w_out[0].astype(BF16), pool_w_l.astype(BF16), w_mlp_up[0].astype(BF16), w_mlp_down[0].astype(BF16)]
    w_srcs, mods_g = lax.optimization_barrier((w_srcs, mods_g))
    gather_start = _exchange_start("gather_weights_start", "gather_chips", w_srcs)

    def weights(tag, started, lo, hi, after_chips, after_forward):
        gw_send, gw_recv, gw_src, gw_land, _ = started
        mine, lands = _exchange_wait(f"gather_{tag}_wait", "gather_chips", gw_send[lo:hi], gw_recv[lo:hi], gw_src[lo:hi],
                                     gw_land[lo:hi], after_chips)
        f_send, f_recv, _, f_land, f_token = _exchange_start(f"forward_{tag}_start", "forward", None, lands)
        _, lands = _exchange_wait(f"forward_{tag}_wait", "forward", f_send, f_recv, None, f_land,
                                  f_token if after_forward is None else after_forward)
        return [_with_own(l, s[None], me) for l, s in zip(lands, mine)]

    mods = _shards_to_cols(mods_g)
    mod_b = lax.dynamic_slice_in_dim(mods, me, 1, axis=0)
    sh_a, sc_a, g_a, sh_m, sc_m, g_m = [mod_b[:, i * D:(i + 1) * D] for i in range(N_MOD)]
    csh_a, csc_a = mods[N_DEV:N_DEV + 1, :D], mods[N_DEV:N_DEV + 1, D:2 * D]

    cos, sin = _rope_tables(L)
    h = _norm_fwd("norm_attn", x2d, norm_attn_w, sc_a, sh_a)
    hc = _norm_fwd("norm_attn_ctx", ctx2d, norm_attn_w, csc_a, csh_a)
    (win_g,) = weights("w_in", gather_start, 0, 1, h, None)
    W_in = _shards_to_cols(win_g)
    W_qk, W_kv = W_in[:, :A + KV], W_in[:, A:A + 2 * KV]
    W_uv = jnp.concatenate([W_in[:, A + 2 * KV:], W_in[:, A + KV:A + 2 * KV]], axis=1)
    (qk,) = _mm("in_proj_qk", h, W_qk, "nn", [BF16], (1024, A + KV, D), extras=[("m", cos), ("m", sin)],
                epilogue=lambda acc, cs, sn: (_rope(acc, cs, sn),))
    (uv,) = _mm("in_proj_uv", h, W_uv, "nn", [F32], (1024, P + KV, D))
    (kvc,) = _mm("in_proj_ctx", hc, W_kv, "nn", [BF16], SMALL_TILES)
    attn, lse = _attn_fwd(qk, uv, kvc, attn_sink, A, KV, P)
    wout_g, pw_g = weights("w_out", gather_start, 1, 3, qk, attn)
    W_out = wout_g.reshape(A + P, D)
    PW = jnp.transpose(pw_g.reshape(N_DEV, POOL_GROUPS, gd // N_DEV, gd), (1, 0, 2, 3)).reshape(POOL_GROUPS, gd, gd)
    ap, pooled = _pool_fwd(uv, PW, pool_scale, attn)
    o, x1, hm = _mm("out_proj_norm", ap, W_out, "nn", [F32, F32, BF16], (256, D, D), chunk=128, epilogue=_out_proj_epilogue,
                    extras=[("mn", x2d), ("n", g_a), ("n", norm_mlp_w), ("n", sc_m), ("n", sh_m)])
    W_up, wdown_g = weights("w_mlp", gather_start, 3, 5, attn, x1)
    W_down = wdown_g.reshape(-1, D)
    up, act = _mm("mlp_up", hm, W_up, "nn", [F32, BF16], (1024, 1024, 2048), epilogue=lambda acc: (acc, _relu2(acc)), b_shards=True)
    d_x2, d_mlp, d_wf, d_gm, loss_row = _mm(
        "mlp_down_loss", act, W_down, "nn", [F32, BF16], (512, D, 1024), chunk=128, n_sums=3, vmem=FUSED_VMEM_LIMIT,
        epilogue=_mlp_down_epilogue, extras=[("mn", x1), ("mn", tgt), ("n", g_m), ("n", wf_row)])
    loss_p = loss_row[:, :1]

    (d_up,) = _mm("mlp_down_bwd_act", d_mlp, W_down, "nt", [BF16], (1024, 1024, 2048), extras=[("mn", up)],
                  epilogue=lambda acc, uu: (acc * (2.0 * jnp.maximum(uu, 0.0)),))
    (gW_down,) = _mm("mlp_down_bwd_w", act, d_mlp, "tn", [BF16], (1024, 2048, 1024))
    (gW_up_s,) = _mm("mlp_up_bwd_w", hm, d_up, "tn", [BF16], (2048, 1024, 1024), out_shards=True)
    g_mlp_srcs = [gW_up_s, gW_down.reshape(N_DEV, -1, D)]
    g_mlp = _exchange_start("grads_mlp_start", "all_to_all", g_mlp_srcs)
    (d_hm,) = _mm("mlp_up_bwd_act", d_up, W_up, "nt", [F32], (1024, D, 1024), b_shards=True)

    d_x1, d_o, d_attn, d_pool, s_sh_m, s_sc_m, s_w_nm, d_ga = _norm_bwd_out_proj_bwd(
        x1, d_hm, d_x2, o, norm_mlp_w, sc_m + g_mlp[4][0, 0], g_a, W_out.T, A)
    (gW_out,) = _mm("out_proj_bwd_w", ap, d_o, "tn", [BF16], (1024, 2048, 1024))
    d_pooled, gPW, d_pscale = _pool_bwd_mix(d_pool, pooled, PW, pool_scale)
    gpw_s = jnp.transpose(gPW.astype(BF16).reshape(POOL_GROUPS, N_DEV, gd // N_DEV, gd), (1, 0, 2, 3)).reshape(N_DEV, -1, gd)
    g_mix_srcs = [gW_out.reshape(N_DEV, (A + P) // N_DEV, D), gpw_s]
    g_mix = _exchange_start("grads_mix_start", "all_to_all", g_mix_srcs)
    lse = lse + g_mix[4][0, 0]
    d_u = _pool_bwd_window(d_pooled)
    d_q, rd, dsink_q, d_kvc = _attn_bwd_dq(qk, uv, kvc, attn_sink, d_attn, lse, cos, sin, A, KV, P)
    d_k, d_v = _attn_bwd_dkv(qk, uv, d_attn, lse, rd, cos, sin, A, KV, P)
    d_sink = _sum_lanes("sink_grad", dsink_q).reshape(1, H)
    d_p = jnp.concatenate([d_q, d_k, d_v, d_u], axis=1)
    d_kvc_b = d_kvc.astype(BF16)
    (gW_kv_ctx,) = _mm("in_proj_ctx_bwd_w", hc, d_kvc_b, "tn", [F32], SMALL_TILES)
    (d_hc,) = _mm("in_proj_ctx_bwd_act", d_kvc_b, W_kv, "nt", [F32], SMALL_TILES)
    gW_in_init = jnp.pad(gW_kv_ctx, ((0, 0), (A, P)))
    (gW_in,) = _mm("in_proj_bwd_w", h, d_p, "tn", [BF16], (1024, 1280, 1024), extras=[("mn", gW_in_init)], epilogue=lambda acc, init: (acc + init,))
    g_in_srcs = [_cols_to_shards(gW_in)]
    g_in = _exchange_start("grads_in_start", "all_to_all", g_in_srcs)
    grad_x, s_sh_a, s_sc_a, s_w_na = _mm(
        "in_proj_bwd_norm", d_p, W_in, "nt", [F32], (256, D, A + 2 * KV + P), chunk=128, n_sums=3,
        epilogue=lambda acc, xr, dres, w, sc: _modulated_norm_bwd(xr, acc, dres, w, sc),
        extras=[("mn", x2d), ("mn", d_x1), ("n", norm_attn_w), ("n", sc_a + g_in[4][0, 0])])
    s_csh, s_csc, s_w_na = _norm_bwd_sums("norm_attn_ctx_bwd", ctx2d, d_hc, norm_attn_w, csc_a, s_w_na)

    pad_l = lambda a: jnp.pad(a, ((0, 0), (0, LANES - a.shape[1])))
    d_mod_b = jnp.concatenate([s_sh_a, s_sc_a, d_ga, s_sh_m, s_sc_m, d_gm], axis=1)
    summed = jnp.concatenate([s_csh, s_csc, s_w_na, s_w_nm, d_wf, d_pscale, pad_l(d_sink), pad_l(loss_p)], axis=1)
    (small_g,) = _all_gather("gather_small", [jnp.concatenate([d_mod_b, summed], axis=1)])
    small_g = small_g[:, 0, :]
    tot = _sum_rows("small_sum", small_g[:, MODW:])
    off = [0]
    for wdt in (D, D, D, D, D, P, LANES, LANES):
        off.append(off[-1] + wdt)
    seg = lambda i: tot[:, off[i]:off[i + 1]]
    g_norm_attn, g_norm_mlp, g_final, g_pscale = seg(2), seg(3), seg(4), seg(5)
    g_sink, loss = seg(6)[:, :H], seg(7)[0, 0]
    d_mod_ctx = jnp.concatenate([seg(0), seg(1), jnp.zeros((1, MODW - 2 * D), F32)], axis=1)
    d_mod = jnp.concatenate([small_g[:, :MODW], d_mod_ctx, jnp.zeros((COND_ROWS - N_DEV - 1, MODW), F32)], axis=0)
    g_b_ada = _sum_rows("b_ada_grad", d_mod[:N_DEV + 1])
    d_mod_sh = lax.dynamic_slice_in_dim(d_mod, me * ws, ws, axis=1)
    (g_w_ada,) = _mm("ada_bwd_w", cond, d_mod_sh, "tn", [F32], SMALL_TILES, a_pre=_silu)
    (d_cond_p,) = _mm("ada_bwd_cond", d_mod_sh, w_ada_l, "nt", [F32], SMALL_TILES)
    (d_cctx_g,) = _all_gather("gather_cctx", [d_cond_p[N_DEV:N_DEV + 1]])
    g_c_ctx = _silu_grad_mul(cctx_row, _sum_rows("cctx_sum", d_cctx_g[:, 0, :]))

    def arrived(name, started):
        srcs, lands = _exchange_wait(name, "all_to_all", started[0], started[1], started[2], started[3], g_c_ctx)
        return [_with_own(l, lax.dynamic_index_in_dim(s, me, 0, keepdims=True), me) for l, s in zip(lands, srcs)]

    r_up, r_down = arrived("grads_mlp_wait", g_mlp)
    r_out, r_pw = arrived("grads_mix_wait", g_mix)
    (r_in,) = arrived("grads_in_wait", g_in)

    results = {
        "c_ctx": _adamw("adam_c_ctx", cctx_row, g_c_ctx, m_c_ctx.reshape(1, D), v_c_ctx.reshape(1, D)),
        "norm_attn_w": _adamw("adam_norm_attn", norm_attn_w, g_norm_attn, m_norm_attn_w, v_norm_attn_w),
        "norm_mlp_w": _adamw("adam_norm_mlp", norm_mlp_w, g_norm_mlp, m_norm_mlp_w, v_norm_mlp_w),
        "w_ada": _adamw("adam_w_ada", w_ada_l, g_w_ada, m_w_ada[0], v_w_ada[0]),
        "b_ada": _adamw("adam_b_ada", b_ada, g_b_ada, m_b_ada, v_b_ada),
        "w_in": _adamw("adam_w_in", w_in[0], r_in, m_w_in[0], v_w_in[0]),
        "attn_sink": _adamw("adam_sink", attn_sink, g_sink, m_attn_sink, v_attn_sink),
        "pool_w": _adamw("adam_pool_w", pool_w_l, r_pw, m_pool_w[0].reshape(pool_w_l.shape), v_pool_w[0].reshape(pool_w_l.shape)),
        "pool_scale": _adamw("adam_pool_scale", pool_scale, g_pscale, m_pool_scale, v_pool_scale),
        "w_out": _adamw("adam_w_out", w_out[0], r_out, m_w_out[0], v_w_out[0]),
        "w_mlp_up": _adamw("adam_w_up", w_mlp_up[0], r_up, m_w_mlp_up[0], v_w_mlp_up[0]),
        "w_mlp_down": _adamw("adam_w_down", w_mlp_down[0], r_down, m_w_mlp_down[0], v_w_mlp_down[0]),
        "final_norm_w": _adamw("adam_final_norm", wf_row, g_final, m_final_norm_w.reshape(1, D), v_final_norm_w.reshape(1, D)),
    }
    shapes = {"c_ctx": c_ctx.shape, "norm_attn_w": norm_attn_w.shape, "norm_mlp_w": norm_mlp_w.shape, "w_ada": w_ada.shape,
              "b_ada": b_ada.shape, "w_in": w_in.shape, "attn_sink": attn_sink.shape, "pool_w": pool_w.shape,
              "pool_scale": pool_scale.shape, "w_out": w_out.shape, "w_mlp_up": w_mlp_up.shape, "w_mlp_down": w_mlp_down.shape,
              "final_norm_w": final_norm_w.shape}
    outs = [loss, grad_x.reshape(x.shape)]
    for part in range(4):
        outs += [results[name][part].reshape(shape) for name, shape in shapes.items()]
    return tuple(outs)
```

```python
import jax
import jax.numpy as jnp
import numpy as np
from jax import lax
from jax.experimental import pallas as pl
from jax.experimental.pallas import tpu as pltpu

F32 = jnp.float32
BF16 = jnp.bfloat16
I32 = jnp.int32

HEAD_DIM = 64
GQA = 4
BLOCK = 128
GRID_W = 64
ROPE_BASE = 10000.0
POOL_WINDOWS = (2, 4, 8, 16)
POOL_GROUPS = len(POOL_WINDOWS)
HALO = 8
N_MOD = 6
EPS = 1e-6
NEG_INF = -1e30
ADAM_LR = 0.001
ADAM_B1 = 0.9
ADAM_B2 = 0.999
ADAM_EPS = 1e-08
ADAM_WD = 0.01
ADAM_STEP = 10
N_DEV = 8
COND_ROWS = 2 * N_DEV
LANES = 128
SUBLANES_16BIT = 16
VMEM_LIMIT = 48 * 1024 * 1024
FUSED_VMEM_LIMIT = 56 * 1024 * 1024
SMALL_TILES = (512, 1024, 512)
MESH = pl.DeviceIdType.MESH
HBM = pl.BlockSpec(memory_space=pltpu.HBM)
SEM = pl.BlockSpec(memory_space=pltpu.SEMAPHORE)
SIDE_EFFECT = pltpu.CompilerParams(has_side_effects=pltpu.SideEffectType.DATAFLOW_SIDE_EFFECTING)


def _cparams(*sem):
    return pltpu.CompilerParams(dimension_semantics=sem, vmem_limit_bytes=VMEM_LIMIT)


def _tile(n, pref, align):
    if n <= pref:
        return n
    t = (pref // align) * align
    while t >= align:
        if n % t == 0:
            return t
        t -= align
    return n


def _dot(a, b):
    return lax.dot_general(a, b, (((1,), (0,)), ((), ())), preferred_element_type=F32)


def _dot_nt(a, b):
    return lax.dot_general(a, b, (((1,), (1,)), ((), ())), preferred_element_type=F32)


def _dot_tn(a, b):
    return lax.dot_general(a, b, (((0,), (0,)), ((), ())), preferred_element_type=F32)


_DOTS = {"nn": _dot, "nt": _dot_nt, "tn": _dot_tn}


def _mm(name, a, b, mode, out_dtypes, tiles, *, epilogue=None, extras=(), a_pre=None, n_sums=0, chunk=None,
        b_shards=False, out_shards=False, vmem=VMEM_LIMIT):
    if mode == "nn":
        M, K = a.shape
        K2, N = (b.shape[1], N_DEV * b.shape[2]) if b_shards else b.shape
    elif mode == "nt":
        M, K = a.shape
        N, K2 = (b.shape[1], N_DEV * b.shape[2]) if b_shards else b.shape
    else:
        (K, M), (K2, N) = a.shape, b.shape
    assert K == K2 and not (b_shards and mode == "tn"), (name, a.shape, b.shape)
    n_span = N // N_DEV if out_shards or (b_shards and mode == "nn") else N
    k_span = K // N_DEV if b_shards and mode == "nt" else K
    tm = _tile(M, tiles[0], LANES if mode == "tn" else SUBLANES_16BIT)
    tn = _tile(n_span, tiles[1], LANES)
    tk = _tile(k_span, tiles[2], SUBLANES_16BIT if mode == "tn" else LANES)
    nk, nb, kb = K // tk, n_span // tn, k_span // tk
    rows = tm if chunk is None else min(chunk, tm)
    n_ex, n_out = len(extras), len(out_dtypes)
    use_acc = nk > 1 or rows < tm
    assert n_sums == 0 or N == tn, name

    def product(a_ref, b_ref):
        at = a_ref[...]
        if a_pre is not None:
            at = a_pre(at)
        return _DOTS[mode](at.astype(BF16), b_ref[...].astype(BF16))

    def apply(acc, ex, out_refs, sl):
        res = (acc,) if epilogue is None else epilogue(acc, *ex)
        for o_ref, o in zip(out_refs, res[:n_out]):
            o_ref[sl, :] = o.astype(o_ref.dtype)
        return tuple(res[n_out:])

    def finish(acc, ex_refs, out_refs, sum_refs):
        if rows == tm:
            acc = acc if not use_acc else acc[...]
            sums = apply(acc, [r[...] for r in ex_refs], out_refs, slice(None))
        else:
            def one(ci, sums):
                sl = pl.ds(pl.multiple_of(ci * rows, rows), rows)
                ex = [r[...] if kind == "n" else r[sl, :] for (kind, _), r in zip(extras, ex_refs)]
                return tuple(s + v for s, v in zip(sums, apply(acc[sl, :], ex, out_refs, sl)))
            sums = lax.fori_loop(0, tm // rows, one, tuple(jnp.zeros((1, tn), F32) for _ in range(n_sums)))
        first = pl.program_id(0) == 0
        for s_ref, sv in zip(sum_refs, sums):
            @pl.when(first)
            def _(s_ref=s_ref, sv=sv):
                s_ref[...] = sv

            @pl.when(jnp.logical_not(first))
            def _(s_ref=s_ref, sv=sv):
                s_ref[...] += sv

    def body(a_ref, b_ref, *rest):
        ex_refs, out_refs = rest[:n_ex], rest[n_ex:n_ex + n_out]
        sum_refs = rest[n_ex + n_out:n_ex + n_out + n_sums]
        if not use_acc:
            finish(product(a_ref, b_ref), ex_refs, out_refs, sum_refs)
            return
        acc_ref = rest[-1]
        k = pl.program_id(2)

        @pl.when(k == 0)
        def _():
            acc_ref[...] = product(a_ref, b_ref)

        @pl.when(k > 0)
        def _():
            acc_ref[...] += product(a_ref, b_ref)

        @pl.when(k == nk - 1)
        def _():
            finish(acc_ref, ex_refs, out_refs, sum_refs)

    a_spec = pl.BlockSpec((tk, tm), lambda i, j, k: (k, i)) if mode == "tn" else pl.BlockSpec((tm, tk), lambda i, j, k: (i, k))
    if not b_shards:
        b_spec = pl.BlockSpec((tn, tk), lambda i, j, k: (j, k)) if mode == "nt" else pl.BlockSpec((tk, tn), lambda i, j, k: (k, j))
    elif mode == "nn":
        b_spec = pl.BlockSpec((None, tk, tn), lambda i, j, k: (j // nb, k, j % nb))
    else:
        b_spec = pl.BlockSpec((None, tn, tk), lambda i, j, k: (k // kb, j, k % kb))
    ex_specs = []
    for kind, arr in extras:
        if kind == "mn":
            ex_specs.append(pl.BlockSpec((tm, tn), lambda i, j, k: (i, j)))
        elif kind == "n":
            ex_specs.append(pl.BlockSpec((1, tn), lambda i, j, k: (0, j)))
        else:
            ex_specs.append(pl.BlockSpec((tm, arr.shape[1]), lambda i, j, k: (i, 0)))
    if out_shards:
        out_specs = [pl.BlockSpec((None, tm, tn), lambda i, j, k: (j // nb, i, j % nb)) for _ in out_dtypes]
        out_shape = [jax.ShapeDtypeStruct((N_DEV, M, n_span), d) for d in out_dtypes]
    else:
        out_specs = [pl.BlockSpec((tm, tn), lambda i, j, k: (i, j)) for _ in out_dtypes]
        out_shape = [jax.ShapeDtypeStruct((M, N), d) for d in out_dtypes]
    out_specs += [pl.BlockSpec((1, tn), lambda i, j, k: (0, 0))] * n_sums
    out_shape += [jax.ShapeDtypeStruct((1, N), F32)] * n_sums
    return pl.pallas_call(
        body,
        name=name,
        grid=(M // tm, N // tn, nk),
        in_specs=[a_spec, b_spec] + ex_specs,
        out_specs=out_specs,
        out_shape=out_shape,
        scratch_shapes=[pltpu.VMEM((tm, tn), F32)] if use_acc else [],
        compiler_params=pltpu.CompilerParams(
            dimension_semantics=("arbitrary",) * 3 if n_sums else ("parallel", "parallel", "arbitrary"), vmem_limit_bytes=vmem),
    )(a, b, *[arr for _, arr in extras])


def _silu(v):
    return v / (1.0 + jnp.exp(-v))


def _relu2(v):
    r = jnp.maximum(v, 0.0)
    return r * r


def _rope_tables(L):
    half = HEAD_DIM // 2
    inv_freq = np.float32(ROPE_BASE) ** (-np.arange(0, half, 2, dtype=np.float32) / np.float32(half))
    t = np.arange(L)
    row, col = t // GRID_W, t % GRID_W
    ang_r = row.astype(np.float32)[:, None] * inv_freq[None, :]
    ang_c = col.astype(np.float32)[:, None] * inv_freq[None, :]
    cos = np.concatenate([np.cos(ang_r), np.cos(ang_r), np.cos(ang_c), np.cos(ang_c)], axis=1)
    sin = np.concatenate([-np.sin(ang_r), np.sin(ang_r), -np.sin(ang_c), np.sin(ang_c)], axis=1)
    reps = LANES // HEAD_DIM
    return jnp.asarray(np.tile(cos, (1, reps)), F32), jnp.asarray(np.tile(sin, (1, reps)), F32)


def _rope(xf, cos, sin):
    quarter = HEAD_DIM // 4
    lane = lax.broadcasted_iota(I32, (xf.shape[0], LANES), 1)
    first = (lane & quarter) == 0
    outs = []
    for j in range(xf.shape[1] // LANES):
        xc = xf[:, j * LANES:(j + 1) * LANES]
        partner = jnp.where(first, pltpu.roll(xc, LANES - quarter, 1), pltpu.roll(xc, quarter, 1))
        outs.append(xc * cos + partner * sin)
    return outs[0] if len(outs) == 1 else jnp.concatenate(outs, axis=1)


def _inv_rms(xf):
    return lax.rsqrt(jnp.mean(xf * xf, axis=-1, keepdims=True) + EPS)


def _modulated_norm(xf, w, sc, sh):
    return ((xf * _inv_rms(xf)) * w) * (1.0 + sc) + sh


def _modulated_norm_bwd(xf, dh, dres, w, sc):
    r = _inv_rms(xf)
    xh = xf * r
    dn = dh * (1.0 + sc)
    dxh = dn * w
    dx = dres + r * (dxh - xh * jnp.mean(dxh * xh, axis=-1, keepdims=True))
    col = lambda v: jnp.sum(v, axis=0, keepdims=True)
    return dx, col(dh), col(dh * (xh * w)), col(dn * xh)


def _norm_fwd(name, x, w, sc, sh):
    L, D = x.shape
    T = _tile(L, 512, 8)

    def body(x_ref, w_ref, sc_ref, sh_ref, h_ref):
        h_ref[...] = _modulated_norm(x_ref[...], w_ref[...], sc_ref[...], sh_ref[...]).astype(BF16)

    row = pl.BlockSpec((1, D), lambda i: (0, 0))
    return pl.pallas_call(
        body, name=name, grid=(L // T,),
        in_specs=[pl.BlockSpec((T, D), lambda i: (i, 0)), row, row, row],
        out_specs=pl.BlockSpec((T, D), lambda i: (i, 0)),
        out_shape=jax.ShapeDtypeStruct((L, D), BF16),
        compiler_params=_cparams("parallel"),
    )(x, w, sc, sh)


def _norm_bwd_sums(name, x, dh, w, sc, w_init):
    L, D = x.shape
    T = _tile(L, 256, 8)

    def body(x_ref, dh_ref, w_ref, sc_ref, wi_ref, ssh_ref, ssc_ref, sw_ref):
        @pl.when(pl.program_id(0) == 0)
        def _():
            ssh_ref[...] = jnp.zeros_like(ssh_ref)
            ssc_ref[...] = jnp.zeros_like(ssc_ref)
            sw_ref[...] = wi_ref[...]

        dh = dh_ref[...]
        _, s_sh, s_sc, s_w = _modulated_norm_bwd(x_ref[...], dh, jnp.zeros_like(dh), w_ref[...], sc_ref[...])
        ssh_ref[...] += s_sh
        ssc_ref[...] += s_sc
        sw_ref[...] += s_w

    tile = pl.BlockSpec((T, D), lambda i: (i, 0))
    row = pl.BlockSpec((1, D), lambda i: (0, 0))
    return pl.pallas_call(
        body, name=name, grid=(L // T,), in_specs=[tile, tile, row, row, row], out_specs=[row, row, row],
        out_shape=[jax.ShapeDtypeStruct((1, D), F32)] * 3, compiler_params=_cparams("arbitrary"),
    )(x, dh, w, sc, w_init)


def _norm_bwd_out_proj_bwd(x, dh, dres, o, w, sc, g, w_out_t, A):
    L, D = x.shape
    N = w_out_t.shape[1]
    T = _tile(L, 256, SUBLANES_16BIT)
    half = T // 2

    def body(x_ref, dh_ref, dres_ref, o_ref, w_ref, sc_ref, g_ref, wt_ref, dx_ref, do_ref, dattn_ref, dpool_ref,
             ssh_ref, ssc_ref, sw_ref, sg_ref):
        @pl.when(pl.program_id(0) == 0)
        def _():
            for s_ref in (ssh_ref, ssc_ref, sw_ref, sg_ref):
                s_ref[...] = jnp.zeros_like(s_ref)

        for rows in (slice(0, half), slice(half, T)):
            dx, s_sh, s_sc, s_w = _modulated_norm_bwd(x_ref[rows, :], dh_ref[rows, :], dres_ref[rows, :], w_ref[...], sc_ref[...])
            ssh_ref[...] += s_sh
            ssc_ref[...] += s_sc
            sw_ref[...] += s_w
            sg_ref[...] += jnp.sum(dx * o_ref[rows, :], axis=0, keepdims=True)
            dx_ref[rows, :] = dx
            do_ref[rows, :] = (g_ref[...] * dx).astype(BF16)
        dap = _dot(do_ref[...], wt_ref[...])
        dattn_ref[...] = dap[:, :A].astype(BF16)
        dpool_ref[...] = dap[:, A:]

    tile = pl.BlockSpec((T, D), lambda i: (i, 0))
    row = pl.BlockSpec((1, D), lambda i: (0, 0))
    return pl.pallas_call(
        body, name="norm_mlp_bwd_out_proj_bwd", grid=(L // T,),
        in_specs=[tile, tile, tile, tile, row, row, row, pl.BlockSpec((D, N), lambda i: (0, 0))],
        out_specs=[tile, tile, pl.BlockSpec((T, A), lambda i: (i, 0)), pl.BlockSpec((T, N - A), lambda i: (i, 0)), row, row, row, row],
        out_shape=[jax.ShapeDtypeStruct((L, D), F32), jax.ShapeDtypeStruct((L, D), BF16), jax.ShapeDtypeStruct((L, A), BF16),
                   jax.ShapeDtypeStruct((L, N - A), F32)] + [jax.ShapeDtypeStruct((1, D), F32)] * 4,
        compiler_params=pltpu.CompilerParams(dimension_semantics=("arbitrary",), vmem_limit_bytes=FUSED_VMEM_LIMIT),
    )(x, dh, dres, o, w, sc, g, w_out_t)


def _out_proj_epilogue(acc, xr, g, w, sc, sh):
    x1 = xr + g * acc
    return acc, x1, _modulated_norm(x1, w, sc, sh)


def _mlp_down_epilogue(acc, x1, tgt, g, wf):
    D = acc.shape[1]
    x2 = x1 + g * acc
    r = _inv_rms(x2)
    xh = x2 * r
    err = xh * wf - tgt
    loss = 0.5 * jnp.sum(jnp.mean(err * err, axis=-1, keepdims=True), axis=0, keepdims=True)
    dy = err * (1.0 / D)
    dxh = dy * wf
    dx = r * (dxh - xh * jnp.mean(dxh * xh, axis=-1, keepdims=True))
    col = lambda v: jnp.sum(v, axis=0, keepdims=True)
    return dx, g * dx, col(dy * xh), col(dx * acc), jnp.broadcast_to(loss, (1, D))


def _heads(ref, first, n):
    return jnp.concatenate([ref[:, (first + g) * HEAD_DIM:(first + g + 1) * HEAD_DIM] for g in range(n)], axis=0)


def _edge_variants(masks):
    return jnp.asarray(np.stack([np.where(masks(first, last), 0.0, NEG_INF).astype(np.float32)
                                 for last in (False, True) for first in (False, True)]))


def _attn_bias(C):
    kj = np.arange(3 * BLOCK + C)[:, None]
    qi = (np.arange(GQA * BLOCK) % BLOCK)[None, :]

    def masks(first, last):
        window = (kj >= qi) & (kj <= qi + 2 * BLOCK) & (kj >= (BLOCK if first else 0)) & (kj < (2 * BLOCK if last else 3 * BLOCK))
        return window | (kj >= 3 * BLOCK)

    return _edge_variants(masks)


def _attn_bias_keys():
    kj = np.arange(BLOCK)[:, None]
    col = np.arange(3 * GQA * BLOCK)[None, :]
    part, qi = col // (GQA * BLOCK), col % BLOCK

    def masks(first, last):
        return ((part == 0) & (kj <= qi) & (not first)) | (part == 1) | ((part == 2) & (kj >= qi) & (not last))

    return _edge_variants(masks)


def _edge_index(n, nb):
    return (n == 0).astype(I32) + 2 * (n == nb - 1).astype(I32)


def _head_rows(ref, hk):
    return jnp.concatenate([ref[hk * GQA + g:hk * GQA + g + 1, :] for g in range(GQA)], axis=1)


def _rows_to_heads(rows_by_kv_head):
    return jnp.concatenate([r[:, g * BLOCK:(g + 1) * BLOCK] for r in rows_by_kv_head for g in range(GQA)], axis=0)


def _queries_to_rows(t):
    return jnp.concatenate([t[:, g * BLOCK:(g + 1) * BLOCK].T for g in range(GQA)], axis=1)


def _attn_specs(L, A, KV, C, vcol):
    nb = L // BLOCK
    kcol = A // KV
    prev = lambda n: jnp.maximum(n - 1, 0)
    nxt = lambda n: jnp.minimum(n + 1, nb - 1)
    q_spec = pl.BlockSpec((BLOCK, A), lambda n: (n, 0))
    k_specs = [pl.BlockSpec((BLOCK, KV), lambda n: (prev(n), kcol)), pl.BlockSpec((BLOCK, KV), lambda n: (n, kcol)),
               pl.BlockSpec((BLOCK, KV), lambda n: (nxt(n), kcol))]
    v_specs = [pl.BlockSpec((BLOCK, KV), lambda n: (prev(n), vcol)), pl.BlockSpec((BLOCK, KV), lambda n: (n, vcol)),
               pl.BlockSpec((BLOCK, KV), lambda n: (nxt(n), vcol))]
    kvc_spec = pl.BlockSpec((C, 2 * KV), lambda n: (0, 0))
    return q_spec, k_specs, v_specs, kvc_spec


def _keys_values(hk, k_refs, v_refs, kvc_ref, KV):
    sl = slice(hk * HEAD_DIM, (hk + 1) * HEAD_DIM)
    keys = jnp.concatenate([r[:, sl] for r in k_refs] + [kvc_ref[:, sl]], axis=0)
    vals = jnp.concatenate([r[:, sl].astype(BF16) for r in v_refs] + [kvc_ref[:, KV + hk * HEAD_DIM:KV + (hk + 1) * HEAD_DIM]], axis=0)
    return keys, vals


def _sink_row(sink_ref, hk):
    return jnp.concatenate([jnp.full((1, BLOCK), sink_ref[0, hk * GQA + g], F32) for g in range(GQA)], axis=1)


def _attn_fwd(qk, uv, kvc, sink, A, KV, P):
    L = qk.shape[0]
    C = kvc.shape[0]
    nkv = KV // HEAD_DIM
    H = nkv * GQA
    scale = HEAD_DIM ** -0.5

    def body(sink_ref, q_ref, kp_ref, kc_ref, kn_ref, vp_ref, vc_ref, vn_ref, kvc_ref, bias_ref, o_ref, lse_ref):
        bias = bias_ref[_edge_index(pl.program_id(0), L // BLOCK)]
        lse_rows = []
        for hk in range(nkv):
            keys, vals = _keys_values(hk, (kp_ref, kc_ref, kn_ref), (vp_ref, vc_ref, vn_ref), kvc_ref, KV)
            qs = _heads(q_ref, hk * GQA, GQA) * scale
            s = _dot_nt(keys, qs) + bias
            sk = _sink_row(sink_ref, hk)
            m = jnp.maximum(jnp.max(s, axis=0, keepdims=True), sk)
            p = jnp.exp(s - m)
            den = jnp.sum(p, axis=0, keepdims=True) + jnp.exp(sk - m)
            o = _dot_tn(vals, p.astype(BF16)) * (1.0 / den)
            lse_rows.append(m + jnp.log(den))
            o_ref[:, hk * GQA * HEAD_DIM:(hk + 1) * GQA * HEAD_DIM] = _queries_to_rows(o).astype(BF16)
        lse_ref[...] = _rows_to_heads(lse_rows)

    q_spec, k_specs, v_specs, kvc_spec = _attn_specs(L, A, KV, C, P // KV)
    bias = _attn_bias(C)
    return pl.pallas_call(
        body, name="attn_fwd", grid=(L // BLOCK,),
        in_specs=[pl.BlockSpec(memory_space=pltpu.SMEM), q_spec] + k_specs + v_specs
                 + [kvc_spec, pl.BlockSpec(bias.shape, lambda n: (0, 0, 0))],
        out_specs=[pl.BlockSpec((BLOCK, A), lambda n: (n, 0)), pl.BlockSpec((H, BLOCK), lambda n: (0, n))],
        out_shape=[jax.ShapeDtypeStruct((L, A + P), BF16), jax.ShapeDtypeStruct((H, L), F32)],
        compiler_params=_cparams("parallel"),
    )(sink, qk, qk, qk, qk, uv, uv, uv, kvc, bias)


def _attn_bwd_dq(qk, uv, kvc, sink, dap, lse_t, cos, sin, A, KV, P):
    L = qk.shape[0]
    C = kvc.shape[0]
    nkv = KV // HEAD_DIM
    H = nkv * GQA
    scale = HEAD_DIM ** -0.5
    W = 3 * BLOCK

    def body(sink_ref, q_ref, kp_ref, kc_ref, kn_ref, vp_ref, vc_ref, vn_ref, kvc_ref, do_ref, lse_ref, cos_ref, sin_ref, bias_ref,
             dq_ref, rd_ref, ds_ref, dkvc_ref):
        n = pl.program_id(0)

        @pl.when(n == 0)
        def _():
            dkvc_ref[...] = jnp.zeros_like(dkvc_ref)

        bias = bias_ref[_edge_index(n, L // BLOCK)]
        rd_rows, dsink_rows, dq_parts = [], [], []
        for hk in range(nkv):
            sl = slice(hk * HEAD_DIM, (hk + 1) * HEAD_DIM)
            keys, vals = _keys_values(hk, (kp_ref, kc_ref, kn_ref), (vp_ref, vc_ref, vn_ref), kvc_ref, KV)
            qs = _heads(q_ref, hk * GQA, GQA) * scale
            dos = _heads(do_ref, hk * GQA, GQA).astype(BF16)
            lse = _head_rows(lse_ref, hk)
            p = jnp.exp(_dot_nt(keys, qs) + bias - lse)
            dp = _dot_nt(vals, dos)
            rd = jnp.sum(p * dp, axis=0, keepdims=True)
            ds = (p * (dp - rd)).astype(BF16)
            dq_parts.append(_queries_to_rows(_dot_tn(keys, ds) * scale))
            dkvc_ref[:, sl] += _dot(ds[W:, :], qs)
            dkvc_ref[:, KV + hk * HEAD_DIM:KV + (hk + 1) * HEAD_DIM] += _dot(p[W:, :].astype(BF16), dos)
            rd_rows.append(rd)
            dsink_rows.append(-(jnp.exp(_sink_row(sink_ref, hk) - lse) * rd))
        rd_ref[...] = _rows_to_heads(rd_rows)
        ds_ref[...] = _rows_to_heads(dsink_rows)
        dq = dq_parts[0] if nkv == 1 else jnp.concatenate(dq_parts, axis=1)
        dq_ref[...] = _rope(dq, cos_ref[...], -sin_ref[...]).astype(BF16)

    q_spec, k_specs, v_specs, kvc_spec = _attn_specs(L, A, KV, C, P // KV)
    blk = lambda w: pl.BlockSpec((BLOCK, w), lambda n: (n, 0))
    per_head = pl.BlockSpec((H, BLOCK), lambda n: (0, n))
    bias = _attn_bias(C)
    return pl.pallas_call(
        body, name="attn_bwd_dq", grid=(L // BLOCK,),
        in_specs=[pl.BlockSpec(memory_space=pltpu.SMEM), q_spec] + k_specs + v_specs
                 + [kvc_spec, blk(A), per_head, blk(LANES), blk(LANES), pl.BlockSpec(bias.shape, lambda n: (0, 0, 0))],
        out_specs=[blk(A), per_head, per_head, pl.BlockSpec((C, 2 * KV), lambda n: (0, 0))],
        out_shape=[jax.ShapeDtypeStruct((L, A), BF16), jax.ShapeDtypeStruct((H, L), F32), jax.ShapeDtypeStruct((H, L), F32),
                   jax.ShapeDtypeStruct((C, 2 * KV), F32)],
        compiler_params=_cparams("arbitrary"),
    )(sink, qk, qk, qk, qk, uv, uv, uv, kvc, dap, lse_t, cos, sin, bias)


def _attn_bwd_dkv(qk, uv, dap, lse_t, rd_t, cos, sin, A, KV, P):
    L = qk.shape[0]
    nb = L // BLOCK
    nkv = KV // HEAD_DIM
    H = nkv * GQA
    scale = HEAD_DIM ** -0.5

    def body(k_ref, v_ref, qp_ref, qc_ref, qn_ref, dop_ref, doc_ref, don_ref, lsep_ref, lsec_ref, lsen_ref,
             rdp_ref, rdc_ref, rdn_ref, cos_ref, sin_ref, bias_ref, dk_ref, dv_ref):
        bias = bias_ref[_edge_index(pl.program_id(0), nb)]
        dk_parts, dv_parts = [], []
        for hk in range(nkv):
            sl = slice(hk * HEAD_DIM, (hk + 1) * HEAD_DIM)
            km = k_ref[:, sl]
            vm = v_ref[:, sl].astype(BF16)
            qs = jnp.concatenate([_heads(q, hk * GQA, GQA) for q in (qp_ref, qc_ref, qn_ref)], axis=0) * scale
            dos = jnp.concatenate([_heads(d, hk * GQA, GQA) for d in (dop_ref, doc_ref, don_ref)], axis=0).astype(BF16)
            rows = [slice(hk * GQA + g, hk * GQA + g + 1) for g in range(GQA)]
            lse = jnp.concatenate([t[r, :] for t in (lsep_ref, lsec_ref, lsen_ref) for r in rows], axis=1)
            rdv = jnp.concatenate([t[r, :] for t in (rdp_ref, rdc_ref, rdn_ref) for r in rows], axis=1)
            p = jnp.exp(_dot_nt(km, qs) + bias - lse)
            ds = (p * (_dot_nt(vm, dos) - rdv)).astype(BF16)
            dk_parts.append(_dot(ds, qs))
            dv_parts.append(_dot(p.astype(BF16), dos))
        dk = dk_parts[0] if nkv == 1 else jnp.concatenate(dk_parts, axis=1)
        dv = dv_parts[0] if nkv == 1 else jnp.concatenate(dv_parts, axis=1)
        dk_ref[...] = _rope(dk, cos_ref[...], -sin_ref[...]).astype(BF16)
        dv_ref[...] = dv.astype(BF16)

    prev = lambda m: jnp.maximum(m - 1, 0)
    nxt = lambda m: jnp.minimum(m + 1, nb - 1)
    three = lambda w: [pl.BlockSpec((BLOCK, w), lambda m: (prev(m), 0)), pl.BlockSpec((BLOCK, w), lambda m: (m, 0)),
                       pl.BlockSpec((BLOCK, w), lambda m: (nxt(m), 0))]
    three_t = [pl.BlockSpec((H, BLOCK), lambda m: (0, prev(m))), pl.BlockSpec((H, BLOCK), lambda m: (0, m)),
               pl.BlockSpec((H, BLOCK), lambda m: (0, nxt(m)))]
    blk = lambda w: pl.BlockSpec((BLOCK, w), lambda m: (m, 0))
    bias = _attn_bias_keys()
    return pl.pallas_call(
        body, name="attn_bwd_dkv", grid=(nb,),
        in_specs=[pl.BlockSpec((BLOCK, KV), lambda m: (m, A // KV)), pl.BlockSpec((BLOCK, KV), lambda m: (m, P // KV))]
                 + three(A) + three(A) + three_t + three_t + [blk(LANES), blk(LANES), pl.BlockSpec(bias.shape, lambda m: (0, 0, 0))],
        out_specs=[blk(KV), blk(KV)],
        out_shape=[jax.ShapeDtypeStruct((L, KV), BF16), jax.ShapeDtypeStruct((L, KV), BF16)],
        compiler_params=_cparams("parallel"),
    )(qk, uv, qk, qk, qk, dap, dap, dap, lse_t, lse_t, lse_t, rd_t, rd_t, rd_t, cos, sin, bias)


def _halo_specs(T, L, W, col):
    per = T // HALO
    return [pl.BlockSpec((HALO, W), lambda i: (jnp.maximum(i * per - 1, 0), col)),
            pl.BlockSpec((T, W), lambda i: (i, col)),
            pl.BlockSpec((HALO, W), lambda i: (jnp.minimum((i + 1) * per, L // HALO - 1), col))]


def _fill_halo_buf(buf, prev_ref, cur_ref, next_ref, i, nt, T):
    buf[0:HALO, :] = jnp.where(i > 0, prev_ref[...], 0.0)
    buf[HALO:HALO + T, :] = cur_ref[...]
    buf[HALO + T:2 * HALO + T, :] = jnp.where(i < nt - 1, next_ref[...], 0.0)


def _zero_margins(lv):
    rows = lv.shape[0]
    lv[0:HALO, :] = jnp.zeros((HALO, lv.shape[1]), F32)
    lv[rows - HALO:rows, :] = jnp.zeros((HALO, lv.shape[1]), F32)


def _window_sums(lv, x, w, first):
    n = x.shape[0]
    lv[HALO:HALO + n, :] = x
    cur = x + lv[pl.ds(HALO + first, n), :]
    span = 1
    while 2 * span < w:
        lv[HALO:HALO + n, :] = cur
        cur = lv[pl.ds(HALO - span, n), :] + lv[pl.ds(HALO + span, n), :]
        span *= 2
    return cur


def _counts(t, w, L):
    lo = jnp.clip(t - w // 2, 0, L)
    hi = jnp.clip(t - w // 2 + w, 0, L)
    return jnp.maximum(hi - lo, 1).astype(F32)


def _pool_fwd(u, pw, scale, mix):
    L, P = u.shape[0], scale.shape[1]
    gd = P // POOL_GROUPS
    T = _tile(L, 256, 8)
    nt = L // T
    assert (mix.shape[1] - P) % P == 0
    mix_col = mix.shape[1] // P - 1

    def body(up_ref, uc_ref, un_ref, pw_ref, sc_ref, mix_ref, out_ref, pooled_ref, buf, lv):
        i = pl.program_id(0)
        _fill_halo_buf(buf, up_ref, uc_ref, un_ref, i, nt, T)
        _zero_margins(lv)
        t = i * T + lax.broadcasted_iota(I32, (T, 1), 0)
        for g, w in enumerate(POOL_WINDOWS):
            cols = slice(g * gd, (g + 1) * gd)
            acc = _window_sums(lv, buf[:, cols], w, -1)[HALO:HALO + T]
            pooled = (acc / _counts(t, w, L) - uc_ref[:, cols]).astype(BF16)
            pooled_ref[:, cols] = pooled
            out_ref[:, cols] = (_dot(pooled, pw_ref[g]) * sc_ref[:, cols]).astype(BF16)

    return pl.pallas_call(
        body, name="pool_fwd", grid=(nt,),
        in_specs=_halo_specs(T, L, P, 0) + [pl.BlockSpec((POOL_GROUPS, gd, gd), lambda i: (0, 0, 0)), pl.BlockSpec((1, P), lambda i: (0, 0)),
                                            pl.BlockSpec(memory_space=pl.ANY)],
        out_specs=[pl.BlockSpec((T, P), lambda i: (i, mix_col)), pl.BlockSpec((T, P), lambda i: (i, 0))],
        out_shape=[jax.ShapeDtypeStruct(mix.shape, BF16), jax.ShapeDtypeStruct((L, P), BF16)],
        scratch_shapes=[pltpu.VMEM((T + 2 * HALO, P), F32), pltpu.VMEM((T + 4 * HALO, gd), F32)],
        input_output_aliases={5: 0},
        compiler_params=_cparams("parallel"),
    )(u, u, u, pw, scale, mix)


def _pool_bwd_mix(d_pool, pooled, pw, scale):
    L, P = pooled.shape
    gd = P // POOL_GROUPS
    T = _tile(L, 256, 8)

    def body(dp_ref, pooled_ref, pw_ref, sc_ref, dpooled_ref, dpw_ref, dsc_ref):
        i = pl.program_id(0)

        @pl.when(i == 0)
        def _():
            dpw_ref[...] = jnp.zeros_like(dpw_ref)
            dsc_ref[...] = jnp.zeros_like(dsc_ref)

        for g in range(POOL_GROUPS):
            cols = slice(g * gd, (g + 1) * gd)
            pb = pooled_ref[:, cols]
            dp = dp_ref[:, cols]
            dsc_ref[:, cols] += jnp.sum(dp * _dot(pb, pw_ref[g]), axis=0, keepdims=True)
            dm = (dp * sc_ref[:, cols]).astype(BF16)
            dpw_ref[g] += _dot_tn(pb, dm)
            dpooled_ref[:, cols] = _dot_nt(dm, pw_ref[g])

    return pl.pallas_call(
        body, name="pool_bwd_mix", grid=(L // T,),
        in_specs=[pl.BlockSpec((T, P), lambda i: (i, 0)), pl.BlockSpec((T, P), lambda i: (i, 0)),
                  pl.BlockSpec((POOL_GROUPS, gd, gd), lambda i: (0, 0, 0)), pl.BlockSpec((1, P), lambda i: (0, 0))],
        out_specs=[pl.BlockSpec((T, P), lambda i: (i, 0)), pl.BlockSpec((POOL_GROUPS, gd, gd), lambda i: (0, 0, 0)),
                   pl.BlockSpec((1, P), lambda i: (0, 0))],
        out_shape=[jax.ShapeDtypeStruct((L, P), F32), jax.ShapeDtypeStruct((POOL_GROUPS, gd, gd), F32), jax.ShapeDtypeStruct((1, P), F32)],
        compiler_params=_cparams("arbitrary"),
    )(d_pool, pooled, pw, scale)


def _pool_bwd_window(dpooled):
    L, P = dpooled.shape
    gd = P // POOL_GROUPS
    T = _tile(L, 256, 8)
    nt = L // T

    def body(dp_ref, dc_ref, dn_ref, du_ref, buf, lv):
        i = pl.program_id(0)
        _fill_halo_buf(buf, dp_ref, dc_ref, dn_ref, i, nt, T)
        _zero_margins(lv)
        t = i * T - HALO + lax.broadcasted_iota(I32, (T + 2 * HALO, 1), 0)
        for g, w in enumerate(POOL_WINDOWS):
            cols = slice(g * gd, (g + 1) * gd)
            acc = _window_sums(lv, buf[:, cols] / _counts(t, w, L), w, 1)[HALO:HALO + T]
            du_ref[:, cols] = (acc - dc_ref[:, cols]).astype(BF16)

    return pl.pallas_call(
        body, name="pool_bwd_window", grid=(nt,),
        in_specs=_halo_specs(T, L, P, 0),
        out_specs=pl.BlockSpec((T, P), lambda i: (i, 0)),
        out_shape=jax.ShapeDtypeStruct((L, P), BF16),
        scratch_shapes=[pltpu.VMEM((T + 2 * HALO, P), F32), pltpu.VMEM((T + 4 * HALO, gd), F32)],
        compiler_params=_cparams("parallel"),
    )(dpooled, dpooled, dpooled)


def _sum_rows(name, a):
    R, N = a.shape

    def body(a_ref, o_ref):
        if R <= 16:
            acc = a_ref[0:1, :]
            for r in range(1, R):
                acc = acc + a_ref[r:r + 1, :]
        else:
            acc = jnp.sum(a_ref[...], axis=0, keepdims=True)
        o_ref[...] = acc

    return pl.pallas_call(body, name=name, out_shape=jax.ShapeDtypeStruct((1, N), F32))(a)


def _sum_lanes(name, a):
    def body(a_ref, o_ref):
        o_ref[...] = jnp.sum(a_ref[...], axis=1, keepdims=True)

    return pl.pallas_call(body, name=name, out_shape=jax.ShapeDtypeStruct((a.shape[0], 1), F32))(a)


def _silu_grad_mul(cv, g):
    def body(c_ref, g_ref, o_ref):
        cvv = c_ref[...]
        s = 1.0 / (1.0 + jnp.exp(-cvv))
        o_ref[...] = g_ref[...] * (s * (1.0 + cvv * (1.0 - s)))

    return pl.pallas_call(body, name="silu_grad_mul", out_shape=jax.ShapeDtypeStruct(cv.shape, F32))(cv, g)


def _adamw(name, w, g, m, v):
    R, C = w.shape
    parts = g.ndim == 3
    n_parts = g.shape[0] if parts else 1
    T = _tile(R, max(8, 262144 // C), 8)

    def body(w_ref, g_ref, m_ref, v_ref, go_ref, d_ref, mo_ref, vo_ref):
        if parts:
            gv = g_ref[0].astype(F32)
            for p in range(1, n_parts):
                gv = gv + g_ref[p].astype(F32)
        else:
            gv = g_ref[...]
        mn = ADAM_B1 * m_ref[...] + (1.0 - ADAM_B1) * gv
        vn = ADAM_B2 * v_ref[...] + (1.0 - ADAM_B2) * (gv * gv)
        m_hat = mn / (1.0 - ADAM_B1 ** ADAM_STEP)
        v_hat = vn / (1.0 - ADAM_B2 ** ADAM_STEP)
        go_ref[...] = gv
        d_ref[...] = -ADAM_LR * (m_hat / (jnp.sqrt(v_hat) + ADAM_EPS) + ADAM_WD * w_ref[...])
        mo_ref[...] = mn
        vo_ref[...] = vn

    tile = pl.BlockSpec((T, C), lambda i: (i, 0))
    g_spec = pl.BlockSpec((n_parts, T, C), lambda i: (0, i, 0)) if parts else tile
    return pl.pallas_call(
        body, name=name, grid=(R // T,),
        in_specs=[tile, g_spec, tile, tile], out_specs=[tile] * 4,
        out_shape=[jax.ShapeDtypeStruct((R, C), F32)] * 4,
        compiler_params=_cparams("parallel"),
    )(w, g, m, v)


def _dev_index(px, py, pc):
    return 4 * px + 2 * py + pc


def _all_gather(name, arrs):
    n = len(arrs)

    def body(*refs):
        ins, outs = refs[:n], refs[n:2 * n]
        send_sems, recv_sems, local_sems = refs[2 * n:]
        x, y, c = lax.axis_index("x"), lax.axis_index("y"), lax.axis_index("c")
        me, sibling = (x, y, c), (x, y, 1 - c)
        chips = [(1 - x, y), (x, 1 - y), (1 - x, 1 - y)]

        def copy(a, k, block, to, src=None):
            slot = outs[a].at[_dev_index(*block)]
            return pltpu.make_async_remote_copy(
                src_ref=slot if src is None else src, dst_ref=slot, send_sem=send_sems.at[a, k], recv_sem=recv_sems.at[a, k],
                device_id=to, device_id_type=MESH)

        mine = [pltpu.make_async_copy(ins[a], outs[a].at[_dev_index(*me)], local_sems.at[a]) for a in range(n)]
        for cp in mine:
            cp.start()
        first = []
        for a in range(n):
            first.append(copy(a, 0, me, sibling, src=ins[a]))
            first += [copy(a, 1 + j, me, (*chip, c), src=ins[a]) for j, chip in enumerate(chips)]
        for cp in first:
            cp.start()
        passed = []
        for j, chip in enumerate(chips):
            for a in range(n):
                copy(a, 1 + j, (*chip, c), me).wait_recv()
                fwd = copy(a, 4 + j, (*chip, c), sibling)
                fwd.start()
                passed.append(fwd)
        for a in range(n):
            copy(a, 0, sibling, me).wait_recv()
            for j, chip in enumerate(chips):
                copy(a, 4 + j, (*chip, 1 - c), me).wait_recv()
        for cp in first + passed:
            cp.wait_send()
        for cp in mine:
            cp.wait()

    return pl.pallas_call(
        body, name=name,
        in_specs=[HBM] * n, out_specs=[HBM] * n,
        out_shape=[jax.ShapeDtypeStruct((N_DEV, *a.shape), a.dtype) for a in arrs],
        scratch_shapes=[pltpu.SemaphoreType.DMA((n, N_DEV - 1)), pltpu.SemaphoreType.DMA((n, N_DEV - 1)), pltpu.SemaphoreType.DMA((n,))],
    )(*arrs)


N_COPIES = {"all_to_all": N_DEV - 1, "gather_chips": 4, "forward": 3}


def _exchange_copies(kind, src_ref, land_ref, send_sems, recv_sems, sending):
    x, y, c = lax.axis_index("x"), lax.axis_index("y"), lax.axis_index("c")
    me = _dev_index(x, y, c)
    others = [(1 - x, y), (x, 1 - y), (1 - x, 1 - y)]
    if kind == "all_to_all":
        flips = [(dx, dy, dc) for dx in (0, 1) for dy in (0, 1) for dc in (0, 1)][1:]
        peers = [(1 - x if dx else x, 1 - y if dy else y, 1 - c if dc else c) for dx, dy, dc in flips]
        plan = [(p, src_ref.at[_dev_index(*p)], me if sending else _dev_index(*p)) for p in peers]
    elif kind == "gather_chips":
        peers = [(x, y, 1 - c)] + [(*o, c) for o in others]
        plan = [(p, src_ref, me if sending else _dev_index(*p)) for p in peers]
    else:
        plan = [((x, y, 1 - c), land_ref.at[_dev_index(*o, c)], _dev_index(*o, c if sending else 1 - c)) for o in others]
    return [pltpu.make_async_remote_copy(src_ref=src, dst_ref=land_ref.at[slot], send_sem=send_sems.at[k], recv_sem=recv_sems.at[k],
                                         device_id=peer, device_id_type=MESH)
            for k, (peer, src, slot) in enumerate(plan)]


def _exchange_start(name, kind, srcs, lands=None):
    if lands is None:
        lands = [lax.empty((N_DEV, *s.shape) if kind == "gather_chips" else s.shape, s.dtype) for s in srcs]
    n = len(lands)
    ops = ([] if srcs is None else list(srcs)) + list(lands)
    m = len(ops)

    def body(*refs):
        src_refs = [None] * n if srcs is None else refs[:n]
        land_refs = refs[m - n:m]
        send_sems, recv_sems, token = refs[m:m + n], refs[m + n:m + 2 * n], refs[-1]
        for a in range(n):
            for cp in _exchange_copies(kind, src_refs[a], land_refs[a], send_sems[a], recv_sems[a], True):
                cp.start()
        token[...] = jnp.zeros_like(token)

    sems = [pltpu.SemaphoreType.DMA((N_COPIES[kind],))] * (2 * n)
    outs = pl.pallas_call(
        body, name=name,
        out_shape=sems + [pltpu.HBM(o.shape, o.dtype) for o in ops] + [jax.ShapeDtypeStruct((8, LANES), F32)],
        in_specs=[HBM] * m,
        out_specs=[SEM] * (2 * n) + [HBM] * m + [pl.BlockSpec(memory_space=pltpu.VMEM)],
        input_output_aliases={i: 2 * n + i for i in range(m)},
        compiler_params=SIDE_EFFECT,
    )(*[pltpu.with_memory_space_constraint(o, pltpu.HBM) for o in ops])
    thru = outs[2 * n:2 * n + m]
    return outs[:n], outs[n:2 * n], (None if srcs is None else thru[:n]), thru[m - n:], outs[-1]


def _exchange_wait(name, kind, send_sems, recv_sems, srcs, lands, after):
    n = len(lands)
    ops = ([] if srcs is None else list(srcs)) + list(lands)
    m = len(ops)

    def body(*refs):
        src_refs = [None] * n if srcs is None else refs[:n]
        land_refs = refs[m - n:m]
        send_refs, recv_refs = refs[m:m + n], refs[m + n:m + 2 * n]
        for a in range(n):
            for cp in _exchange_copies(kind, src_refs[a], land_refs[a], send_refs[a], recv_refs[a], False):
                cp.wait_send()
                cp.wait_recv()

    outs = pl.pallas_call(
        body, name=name,
        out_shape=[pltpu.HBM(o.shape, o.dtype) for o in ops],
        in_specs=[HBM] * m + [SEM] * (2 * n) + [pl.BlockSpec(memory_space=pl.ANY)],
        out_specs=[HBM] * m,
        input_output_aliases={i: i for i in range(m)},
        compiler_params=SIDE_EFFECT,
    )(*ops, *send_sems, *recv_sems, after)
    return (None if srcs is None else outs[:n]), outs[m - n:]


def _with_own(land, own, me):
    return lax.dynamic_update_slice_in_dim(land, own, me, 0)


def _shards_to_cols(g):
    return jnp.transpose(g, (1, 0, 2)).reshape(g.shape[1], N_DEV * g.shape[2])


def _cols_to_shards(a):
    R, Ctot = a.shape
    return jnp.transpose(a.reshape(R, N_DEV, Ctot // N_DEV), (1, 0, 2))


def kernel(x, c, ctx, c_ctx, norm_attn_w, norm_mlp_w, w_ada, b_ada, w_in, attn_sink, pool_w, pool_scale, w_out, w_mlp_up, w_mlp_down, final_norm_w, loss_target, m_c_ctx, m_norm_attn_w, m_norm_mlp_w, m_w_ada, m_b_ada, m_w_in, m_attn_sink, m_pool_w, m_pool_scale, m_w_out, m_w_mlp_up, m_w_mlp_down, m_final_norm_w, v_c_ctx, v_norm_attn_w, v_norm_mlp_w, v_w_ada, v_b_ada, v_w_in, v_attn_sink, v_pool_w, v_pool_scale, v_w_out, v_w_mlp_up, v_w_mlp_down, v_final_norm_w):
    _, L, D = x.shape
    H = attn_sink.shape[1]
    A = H * HEAD_DIM
    KV = A // GQA
    P = pool_scale.shape[1]
    MODW = N_MOD * D
    ws = MODW // N_DEV
    gd = P // POOL_GROUPS
    me = _dev_index(lax.axis_index("x"), lax.axis_index("y"), lax.axis_index("c"))

    x2d, ctx2d, tgt = x[0], ctx[0], loss_target[0]
    cctx_row = c_ctx.reshape(1, D)
    wf_row = final_norm_w.reshape(1, D)
    w_ada_l = w_ada[0]
    pool_w_l = pool_w[0].reshape(POOL_GROUPS * (gd // N_DEV), gd)

    (c_all,) = _all_gather("gather_cond", [c])
    cond = jnp.concatenate([c_all[:, 0, :], cctx_row, jnp.zeros((COND_ROWS - N_DEV - 1, D), F32)], axis=0)
    b_sh = lax.dynamic_slice_in_dim(b_ada, me * ws, ws, axis=1)
    (mods_sh,) = _mm("ada_mod", cond, w_ada_l, "nn", [F32], SMALL_TILES, a_pre=_silu, extras=[("n", b_sh)], epilogue=lambda acc, b: (acc + b,))
    (mods_g,) = _all_gather("gather_mods", [mods_sh])

    w_srcs = [w_in[0].astype(BF16), w_out[0].astype(BF16), pool_w_l.astype(BF16), w_mlp_up[0].astype(BF16), w_mlp_down[0].astype(BF16)]
    w_srcs, mods_g = lax.optimization_barrier((w_srcs, mods_g))
    gather_start = _exchange_start("gather_weights_start", "gather_chips", w_srcs)

    def weights(tag, started, lo, hi, after_chips, after_forward):
        gw_send, gw_recv, gw_src, gw_land, _ = started
        mine, lands = _exchange_wait(f"gather_{tag}_wait", "gather_chips", gw_send[lo:hi], gw_recv[lo:hi], gw_src[lo:hi],
                                     gw_land[lo:hi], after_chips)
        f_send, f_recv, _, f_land, f_token = _exchange_start(f"forward_{tag}_start", "forward", None, lands)
        _, lands = _exchange_wait(f"forward_{tag}_wait", "forward", f_send, f_recv, None, f_land,
                                  f_token if after_forward is None else after_forward)
        return [_with_own(l, s[None], me) for l, s in zip(lands, mine)]

    mods = _shards_to_cols(mods_g)
    mod_b = lax.dynamic_slice_in_dim(mods, me, 1, axis=0)
    sh_a, sc_a, g_a, sh_m, sc_m, g_m = [mod_b[:, i * D:(i + 1) * D] for i in range(N_MOD)]
    csh_a, csc_a = mods[N_DEV:N_DEV + 1, :D], mods[N_DEV:N_DEV + 1, D:2 * D]

    cos, sin = _rope_tables(L)
    h = _norm_fwd("norm_attn", x2d, norm_attn_w, sc_a, sh_a)
    hc = _norm_fwd("norm_attn_ctx", ctx2d, norm_attn_w, csc_a, csh_a)
    (win_g,) = weights("w_in", gather_start, 0, 1, h, None)
    W_in = _shards_to_cols(win_g)
    W_qk, W_kv = W_in[:, :A + KV], W_in[:, A:A + 2 * KV]
    W_uv = jnp.concatenate([W_in[:, A + 2 * KV:], W_in[:, A + KV:A + 2 * KV]], axis=1)
    (qk,) = _mm("in_proj_qk", h, W_qk, "nn", [BF16], (1024, A + KV, D), extras=[("m", cos), ("m", sin)],
                epilogue=lambda acc, cs, sn: (_rope(acc, cs, sn),))
    (uv,) = _mm("in_proj_uv", h, W_uv, "nn", [F32], (1024, P + KV, D))
    (kvc,) = _mm("in_proj_ctx", hc, W_kv, "nn", [BF16], SMALL_TILES)
    attn, lse = _attn_fwd(qk, uv, kvc, attn_sink, A, KV, P)
    wout_g, pw_g = weights("w_out", gather_start, 1, 3, qk, attn)
    W_out = wout_g.reshape(A + P, D)
    PW = jnp.transpose(pw_g.reshape(N_DEV, POOL_GROUPS, gd // N_DEV, gd), (1, 0, 2, 3)).reshape(POOL_GROUPS, gd, gd)
    ap, pooled = _pool_fwd(uv, PW, pool_scale, attn)
    o, x1, hm = _mm("out_proj_norm", ap, W_out, "nn", [F32, F32, BF16], (256, D, D), chunk=128, epilogue=_out_proj_epilogue,
                    extras=[("mn", x2d), ("n", g_a), ("n", norm_mlp_w), ("n", sc_m), ("n", sh_m)])
    W_up, wdown_g = weights("w_mlp", gather_start, 3, 5, attn, x1)
    W_down = wdown_g.reshape(-1, D)
    up, act = _mm("mlp_up", hm, W_up, "nn", [F32, BF16], (1024, 1024, 2048), epilogue=lambda acc: (acc, _relu2(acc)), b_shards=True)
    d_x2, d_mlp, d_wf, d_gm, loss_row = _mm(
        "mlp_down_loss", act, W_down, "nn", [F32, BF16], (512, D, 1024), chunk=128, n_sums=3, vmem=FUSED_VMEM_LIMIT,
        epilogue=_mlp_down_epilogue, extras=[("mn", x1), ("mn", tgt), ("n", g_m), ("n", wf_row)])
    loss_p = loss_row[:, :1]

    (d_up,) = _mm("mlp_down_bwd_act", d_mlp, W_down, "nt", [BF16], (1024, 1024, 2048), extras=[("mn", up)],
                  epilogue=lambda acc, uu: (acc * (2.0 * jnp.maximum(uu, 0.0)),))
    (gW_down,) = _mm("mlp_down_bwd_w", act, d_mlp, "tn", [BF16], (1024, 2048, 1024))
    (gW_up_s,) = _mm("mlp_up_bwd_w", hm, d_up, "tn", [BF16], (2048, 1024, 1024), out_shards=True)
    g_mlp_srcs = [gW_up_s, gW_down.reshape(N_DEV, -1, D)]
    g_mlp = _exchange_start("grads_mlp_start", "all_to_all", g_mlp_srcs)
    (d_hm,) = _mm("mlp_up_bwd_act", d_up, W_up, "nt", [F32], (1024, D, 1024), b_shards=True)

    d_x1, d_o, d_attn, d_pool, s_sh_m, s_sc_m, s_w_nm, d_ga = _norm_bwd_out_proj_bwd(
        x1, d_hm, d_x2, o, norm_mlp_w, sc_m + g_mlp[4][0, 0], g_a, W_out.T, A)
    (gW_out,) = _mm("out_proj_bwd_w", ap, d_o, "tn", [BF16], (1024, 2048, 1024))
    d_pooled, gPW, d_pscale = _pool_bwd_mix(d_pool, pooled, PW, pool_scale)
    gpw_s = jnp.transpose(gPW.astype(BF16).reshape(POOL_GROUPS, N_DEV, gd // N_DEV, gd), (1, 0, 2, 3)).reshape(N_DEV, -1, gd)
    g_mix_srcs = [gW_out.reshape(N_DEV, (A + P) // N_DEV, D), gpw_s]
    g_mix = _exchange_start("grads_mix_start", "all_to_all", g_mix_srcs)
    lse = lse + g_mix[4][0, 0]
    d_u = _pool_bwd_window(d_pooled)
    d_q, rd, dsink_q, d_kvc = _attn_bwd_dq(qk, uv, kvc, attn_sink, d_attn, lse, cos, sin, A, KV, P)
    d_k, d_v = _attn_bwd_dkv(qk, uv, d_attn, lse, rd, cos, sin, A, KV, P)
    d_sink = _sum_lanes("sink_grad", dsink_q).reshape(1, H)
    d_p = jnp.concatenate([d_q, d_k, d_v, d_u], axis=1)
    d_kvc_b = d_kvc.astype(BF16)
    (gW_kv_ctx,) = _mm("in_proj_ctx_bwd_w", hc, d_kvc_b, "tn", [F32], SMALL_TILES)
    (d_hc,) = _mm("in_proj_ctx_bwd_act", d_kvc_b, W_kv, "nt", [F32], SMALL_TILES)
    gW_in_init = jnp.pad(gW_kv_ctx, ((0, 0), (A, P)))
    (gW_in,) = _mm("in_proj_bwd_w", h, d_p, "tn", [BF16], (1024, 1280, 1024), extras=[("mn", gW_in_init)], epilogue=lambda acc, init: (acc + init,))
    g_in_srcs = [_cols_to_shards(gW_in)]
    g_in = _exchange_start("grads_in_start", "all_to_all", g_in_srcs)
    grad_x, s_sh_a, s_sc_a, s_w_na = _mm(
        "in_proj_bwd_norm", d_p, W_in, "nt", [F32], (256, D, A + 2 * KV + P), chunk=128, n_sums=3,
        epilogue=lambda acc, xr, dres, w, sc: _modulated_norm_bwd(xr, acc, dres, w, sc),
        extras=[("mn", x2d), ("mn", d_x1), ("n", norm_attn_w), ("n", sc_a + g_in[4][0, 0])])
    s_csh, s_csc, s_w_na = _norm_bwd_sums("norm_attn_ctx_bwd", ctx2d, d_hc, norm_attn_w, csc_a, s_w_na)

    pad_l = lambda a: jnp.pad(a, ((0, 0), (0, LANES - a.shape[1])))
    d_mod_b = jnp.concatenate([s_sh_a, s_sc_a, d_ga, s_sh_m, s_sc_m, d_gm], axis=1)
    summed = jnp.concatenate([s_csh, s_csc, s_w_na, s_w_nm, d_wf, d_pscale, pad_l(d_sink), pad_l(loss_p)], axis=1)
    (small_g,) = _all_gather("gather_small", [jnp.concatenate([d_mod_b, summed], axis=1)])
    small_g = small_g[:, 0, :]
    tot = _sum_rows("small_sum", small_g[:, MODW:])
    off = [0]
    for wdt in (D, D, D, D, D, P, LANES, LANES):
        off.append(off[-1] + wdt)
    seg = lambda i: tot[:, off[i]:off[i + 1]]
    g_norm_attn, g_norm_mlp, g_final, g_pscale = seg(2), seg(3), seg(4), seg(5)
    g_sink, loss = seg(6)[:, :H], seg(7)[0, 0]
    d_mod_ctx = jnp.concatenate([seg(0), seg(1), jnp.zeros((1, MODW - 2 * D), F32)], axis=1)
    d_mod = jnp.concatenate([small_g[:, :MODW], d_mod_ctx, jnp.zeros((COND_ROWS - N_DEV - 1, MODW), F32)], axis=0)
    g_b_ada = _sum_rows("b_ada_grad", d_mod[:N_DEV + 1])
    d_mod_sh = lax.dynamic_slice_in_dim(d_mod, me * ws, ws, axis=1)
    (g_w_ada,) = _mm("ada_bwd_w", cond, d_mod_sh, "tn", [F32], SMALL_TILES, a_pre=_silu)
    (d_cond_p,) = _mm("ada_bwd_cond", d_mod_sh, w_ada_l, "nt", [F32], SMALL_TILES)
    (d_cctx_g,) = _all_gather("gather_cctx", [d_cond_p[N_DEV:N_DEV + 1]])
    g_c_ctx = _silu_grad_mul(cctx_row, _sum_rows("cctx_sum", d_cctx_g[:, 0, :]))

    def arrived(name, started):
        srcs, lands = _exchange_wait(name, "all_to_all", started[0], started[1], started[2], started[3], g_c_ctx)
        return [_with_own(l, lax.dynamic_index_in_dim(s, me, 0, keepdims=True), me) for l, s in zip(lands, srcs)]

    r_up, r_down = arrived("grads_mlp_wait", g_mlp)
    r_out, r_pw = arrived("grads_mix_wait", g_mix)
    (r_in,) = arrived("grads_in_wait", g_in)

    results = {
        "c_ctx": _adamw("adam_c_ctx", cctx_row, g_c_ctx, m_c_ctx.reshape(1, D), v_c_ctx.reshape(1, D)),
        "norm_attn_w": _adamw("adam_norm_attn", norm_attn_w, g_norm_attn, m_norm_attn_w, v_norm_attn_w),
        "norm_mlp_w": _adamw("adam_norm_mlp", norm_mlp_w, g_norm_mlp, m_norm_mlp_w, v_norm_mlp_w),
        "w_ada": _adamw("adam_w_ada", w_ada_l, g_w_ada, m_w_ada[0], v_w_ada[0]),
        "b_ada": _adamw("adam_b_ada", b_ada, g_b_ada, m_b_ada, v_b_ada),
        "w_in": _adamw("adam_w_in", w_in[0], r_in, m_w_in[0], v_w_in[0]),
        "attn_sink": _adamw("adam_sink", attn_sink, g_sink, m_attn_sink, v_attn_sink),
        "pool_w": _adamw("adam_pool_w", pool_w_l, r_pw, m_pool_w[0].reshape(pool_w_l.shape), v_pool_w[0].reshape(pool_w_l.shape)),
        "pool_scale": _adamw("adam_pool_scale", pool_scale, g_pscale, m_pool_scale, v_pool_scale),
        "w_out": _adamw("adam_w_out", w_out[0], r_out, m_w_out[0], v_w_out[0]),
        "w_mlp_up": _adamw("adam_w_up", w_mlp_up[0], r_up, m_w_mlp_up[0], v_w_mlp_up[0]),
        "w_mlp_down": _adamw("adam_w_down", w_mlp_down[0], r_down, m_w_mlp_down[0], v_w_mlp_down[0]),
        "final_norm_w": _adamw("adam_final_norm", wf_row, g_final, m_final_norm_w.reshape(1, D), v_final_norm_w.reshape(1, D)),
    }
    shapes = {"c_ctx": c_ctx.shape, "norm_attn_w": norm_attn_w.shape, "norm_mlp_w": norm_mlp_w.shape, "w_ada": w_ada.shape,
              "b_ada": b_ada.shape, "w_in": w_in.shape, "attn_sink": attn_sink.shape, "pool_w": pool_w.shape,
              "pool_scale": pool_scale.shape, "w_out": w_out.shape, "w_mlp_up": w_mlp_up.shape, "w_mlp_down": w_mlp_down.shape,
              "final_norm_w": final_norm_w.shape}
    outs = [loss, grad_x.reshape(x.shape)]
    for part in range(4):
        outs += [results[name][part].reshape(shape) for name, shape in shapes.items()]
    return tuple(outs)
```

```python
import jax
import jax.numpy as jnp
import numpy as np
from jax import lax
from jax.experimental import pallas as pl
from jax.experimental.pallas import tpu as pltpu

F32 = jnp.float32
BF16 = jnp.bfloat16
I32 = jnp.int32

HEAD_DIM = 64
GQA = 4
BLOCK = 128
GRID_W = 64
ROPE_BASE = 10000.0
POOL_WINDOWS = (2, 4, 8, 16)
POOL_GROUPS = len(POOL_WINDOWS)
HALO = 8
N_MOD = 6
EPS = 1e-6
NEG_INF = -1e30
ADAM_LR = 0.001
ADAM_B1 = 0.9
ADAM_B2 = 0.999
ADAM_EPS = 1e-08
ADAM_WD = 0.01
ADAM_STEP = 10
N_DEV = 8
COND_ROWS = 2 * N_DEV
LANES = 128
SUBLANES_16BIT = 16
VMEM_LIMIT = 48 * 1024 * 1024
FUSED_VMEM_LIMIT = 56 * 1024 * 1024
SMALL_TILES = (512, 1024, 512)
MESH = pl.DeviceIdType.MESH
HBM = pl.BlockSpec(memory_space=pltpu.HBM)
SEM = pl.BlockSpec(memory_space=pltpu.SEMAPHORE)
SIDE_EFFECT = pltpu.CompilerParams(has_side_effects=pltpu.SideEffectType.DATAFLOW_SIDE_EFFECTING)


def _cparams(*sem):
    return pltpu.CompilerParams(dimension_semantics=sem, vmem_limit_bytes=VMEM_LIMIT)


def _tile(n, pref, align):
    if n <= pref:
        return n
    t = (pref // align) * align
    while t >= align:
        if n % t == 0:
            return t
        t -= align
    return n


def _dot(a, b):
    return lax.dot_general(a, b, (((1,), (0,)), ((), ())), preferred_element_type=F32)


def _dot_nt(a, b):
    return lax.dot_general(a, b, (((1,), (1,)), ((), ())), preferred_element_type=F32)


def _dot_tn(a, b):
    return lax.dot_general(a, b, (((0,), (0,)), ((), ())), preferred_element_type=F32)


_DOTS = {"nn": _dot, "nt": _dot_nt, "tn": _dot_tn}


def _mm(name, a, b, mode, out_dtypes, tiles, *, epilogue=None, extras=(), a_pre=None, n_sums=0, chunk=None,
        b_shards=False, out_shards=False, vmem=VMEM_LIMIT):
    if mode == "nn":
        M, K = a.shape
        K2, N = (b.shape[1], N_DEV * b.shape[2]) if b_shards else b.shape
    elif mode == "nt":
        M, K = a.shape
        N, K2 = (b.shape[1], N_DEV * b.shape[2]) if b_shards else b.shape
    else:
        (K, M), (K2, N) = a.shape, b.shape
    assert K == K2 and not (b_shards and mode == "tn"), (name, a.shape, b.shape)
    n_span = N // N_DEV if out_shards or (b_shards and mode == "nn") else N
    k_span = K // N_DEV if b_shards and mode == "nt" else K
    tm = _tile(M, tiles[0], LANES if mode == "tn" else SUBLANES_16BIT)
    tn = _tile(n_span, tiles[1], LANES)
    tk = _tile(k_span, tiles[2], SUBLANES_16BIT if mode == "tn" else LANES)
    nk, nb, kb = K // tk, n_span // tn, k_span // tk
    rows = tm if chunk is None else min(chunk, tm)
    n_ex, n_out = len(extras), len(out_dtypes)
    use_acc = nk > 1 or rows < tm
    assert n_sums == 0 or N == tn, name

    def product(a_ref, b_ref):
        at = a_ref[...]
        if a_pre is not None:
            at = a_pre(at)
        return _DOTS[mode](at.astype(BF16), b_ref[...].astype(BF16))

    def apply(acc, ex, out_refs, sl):
        res = (acc,) if epilogue is None else epilogue(acc, *ex)
        for o_ref, o in zip(out_refs, res[:n_out]):
            o_ref[sl, :] = o.astype(o_ref.dtype)
        return tuple(res[n_out:])

    def finish(acc, ex_refs, out_refs, sum_refs):
        if rows == tm:
            acc = acc if not use_acc else acc[...]
            sums = apply(acc, [r[...] for r in ex_refs], out_refs, slice(None))
        else:
            def one(ci, sums):
                sl = pl.ds(pl.multiple_of(ci * rows, rows), rows)
                ex = [r[...] if kind == "n" else r[sl, :] for (kind, _), r in zip(extras, ex_refs)]
                return tuple(s + v for s, v in zip(sums, apply(acc[sl, :], ex, out_refs, sl)))
            sums = lax.fori_loop(0, tm // rows, one, tuple(jnp.zeros((1, tn), F32) for _ in range(n_sums)))
        first = pl.program_id(0) == 0
        for s_ref, sv in zip(sum_refs, sums):
            @pl.when(first)
            def _(s_ref=s_ref, sv=sv):
                s_ref[...] = sv

            @pl.when(jnp.logical_not(first))
            def _(s_ref=s_ref, sv=sv):
                s_ref[...] += sv

    def body(a_ref, b_ref, *rest):
        ex_refs, out_refs = rest[:n_ex], rest[n_ex:n_ex + n_out]
        sum_refs = rest[n_ex + n_out:n_ex + n_out + n_sums]
        if not use_acc:
            finish(product(a_ref, b_ref), ex_refs, out_refs, sum_refs)
            return
        acc_ref = rest[-1]
        k = pl.program_id(2)

        @pl.when(k == 0)
        def _():
            acc_ref[...] = product(a_ref, b_ref)

        @pl.when(k > 0)
        def _():
            acc_ref[...] += product(a_ref, b_ref)

        @pl.when(k == nk - 1)
        def _():
            finish(acc_ref, ex_refs, out_refs, sum_refs)

    a_spec = pl.BlockSpec((tk, tm), lambda i, j, k: (k, i)) if mode == "tn" else pl.BlockSpec((tm, tk), lambda i, j, k: (i, k))
    if not b_shards:
        b_spec = pl.BlockSpec((tn, tk), lambda i, j, k: (j, k)) if mode == "nt" else pl.BlockSpec((tk, tn), lambda i, j, k: (k, j))
    elif mode == "nn":
        b_spec = pl.BlockSpec((None, tk, tn), lambda i, j, k: (j // nb, k, j % nb))
    else:
        b_spec = pl.BlockSpec((None, tn, tk), lambda i, j, k: (k // kb, j, k % kb))
    ex_specs = []
    for kind, arr in extras:
        if kind == "mn":
            ex_specs.append(pl.BlockSpec((tm, tn), lambda i, j, k: (i, j)))
        elif kind == "n":
            ex_specs.append(pl.BlockSpec((1, tn), lambda i, j, k: (0, j)))
        else:
            ex_specs.append(pl.BlockSpec((tm, arr.shape[1]), lambda i, j, k: (i, 0)))
    if out_shards:
        out_specs = [pl.BlockSpec((None, tm, tn), lambda i, j, k: (j // nb, i, j % nb)) for _ in out_dtypes]
        out_shape = [jax.ShapeDtypeStruct((N_DEV, M, n_span), d) for d in out_dtypes]
    else:
        out_specs = [pl.BlockSpec((tm, tn), lambda i, j, k: (i, j)) for _ in out_dtypes]
        out_shape = [jax.ShapeDtypeStruct((M, N), d) for d in out_dtypes]
    out_specs += [pl.BlockSpec((1, tn), lambda i, j, k: (0, 0))] * n_sums
    out_shape += [jax.ShapeDtypeStruct((1, N), F32)] * n_sums
    return pl.pallas_call(
        body,
        name=name,
        grid=(M // tm, N // tn, nk),
        in_specs=[a_spec, b_spec] + ex_specs,
        out_specs=out_specs,
        out_shape=out_shape,
        scratch_shapes=[pltpu.VMEM((tm, tn), F32)] if use_acc else [],
        compiler_params=pltpu.CompilerParams(
            dimension_semantics=("arbitrary",) * 3 if n_sums else ("parallel", "parallel", "arbitrary"), vmem_limit_bytes=vmem),
    )(a, b, *[arr for _, arr in extras])


def _silu(v):
    return v / (1.0 + jnp.exp(-v))


def _relu2(v):
    r = jnp.maximum(v, 0.0)
    return r * r


def _rope_tables(L):
    half = HEAD_DIM // 2
    inv_freq = np.float32(ROPE_BASE) ** (-np.arange(0, half, 2, dtype=np.float32) / np.float32(half))
    t = np.arange(L)
    row, col = t // GRID_W, t % GRID_W
    ang_r = row.astype(np.float32)[:, None] * inv_freq[None, :]
    ang_c = col.astype(np.float32)[:, None] * inv_freq[None, :]
    cos = np.concatenate([np.cos(ang_r), np.cos(ang_r), np.cos(ang_c), np.cos(ang_c)], axis=1)
    sin = np.concatenate([-np.sin(ang_r), np.sin(ang_r), -np.sin(ang_c), np.sin(ang_c)], axis=1)
    reps = LANES // HEAD_DIM
    return jnp.asarray(np.tile(cos, (1, reps)), F32), jnp.asarray(np.tile(sin, (1, reps)), F32)


def _rope(xf, cos, sin):
    quarter = HEAD_DIM // 4
    lane = lax.broadcasted_iota(I32, (xf.shape[0], LANES), 1)
    first = (lane & quarter) == 0
    outs = []
    for j in range(xf.shape[1] // LANES):
        xc = xf[:, j * LANES:(j + 1) * LANES]
        partner = jnp.where(first, pltpu.roll(xc, LANES - quarter, 1), pltpu.roll(xc, quarter, 1))
        outs.append(xc * cos + partner * sin)
    return outs[0] if len(outs) == 1 else jnp.concatenate(outs, axis=1)


def _inv_rms(xf):
    return lax.rsqrt(jnp.mean(xf * xf, axis=-1, keepdims=True) + EPS)


def _modulated_norm(xf, w, sc, sh):
    return ((xf * _inv_rms(xf)) * w) * (1.0 + sc) + sh


def _modulated_norm_bwd(xf, dh, dres, w, sc):
    r = _inv_rms(xf)
    xh = xf * r
    dn = dh * (1.0 + sc)
    dxh = dn * w
    dx = dres + r * (dxh - xh * jnp.mean(dxh * xh, axis=-1, keepdims=True))
    col = lambda v: jnp.sum(v, axis=0, keepdims=True)
    return dx, col(dh), col(dh * (xh * w)), col(dn * xh)


def _norm_fwd(name, x, w, sc, sh):
    L, D = x.shape
    T = _tile(L, 512, 8)

    def body(x_ref, w_ref, sc_ref, sh_ref, h_ref):
        h_ref[...] = _modulated_norm(x_ref[...], w_ref[...], sc_ref[...], sh_ref[...]).astype(BF16)

    row = pl.BlockSpec((1, D), lambda i: (0, 0))
    return pl.pallas_call(
        body, name=name, grid=(L // T,),
        in_specs=[pl.BlockSpec((T, D), lambda i: (i, 0)), row, row, row],
        out_specs=pl.BlockSpec((T, D), lambda i: (i, 0)),
        out_shape=jax.ShapeDtypeStruct((L, D), BF16),
        compiler_params=_cparams("parallel"),
    )(x, w, sc, sh)


def _norm_bwd_sums(name, x, dh, w, sc, w_init):
    L, D = x.shape
    T = _tile(L, 256, 8)

    def body(x_ref, dh_ref, w_ref, sc_ref, wi_ref, ssh_ref, ssc_ref, sw_ref):
        @pl.when(pl.program_id(0) == 0)
        def _():
            ssh_ref[...] = jnp.zeros_like(ssh_ref)
            ssc_ref[...] = jnp.zeros_like(ssc_ref)
            sw_ref[...] = wi_ref[...]

        dh = dh_ref[...]
        _, s_sh, s_sc, s_w = _modulated_norm_bwd(x_ref[...], dh, jnp.zeros_like(dh), w_ref[...], sc_ref[...])
        ssh_ref[...] += s_sh
        ssc_ref[...] += s_sc
        sw_ref[...] += s_w

    tile = pl.BlockSpec((T, D), lambda i: (i, 0))
    row = pl.BlockSpec((1, D), lambda i: (0, 0))
    return pl.pallas_call(
        body, name=name, grid=(L // T,), in_specs=[tile, tile, row, row, row], out_specs=[row, row, row],
        out_shape=[jax.ShapeDtypeStruct((1, D), F32)] * 3, compiler_params=_cparams("arbitrary"),
    )(x, dh, w, sc, w_init)


def _norm_bwd_out_proj_bwd(x, dh, dres, o, w, sc, g, w_out_t, A):
    L, D = x.shape
    N = w_out_t.shape[1]
    T = _tile(L, 256, SUBLANES_16BIT)
    half = T // 2

    def body(x_ref, dh_ref, dres_ref, o_ref, w_ref, sc_ref, g_ref, wt_ref, dx_ref, do_ref, dattn_ref, dpool_ref,
             ssh_ref, ssc_ref, sw_ref, sg_ref):
        @pl.when(pl.program_id(0) == 0)
        def _():
            for s_ref in (ssh_ref, ssc_ref, sw_ref, sg_ref):
                s_ref[...] = jnp.zeros_like(s_ref)

        for rows in (slice(0, half), slice(half, T)):
            dx, s_sh, s_sc, s_w = _modulated_norm_bwd(x_ref[rows, :], dh_ref[rows, :], dres_ref[rows, :], w_ref[...], sc_ref[...])
            ssh_ref[...] += s_sh
            ssc_ref[...] += s_sc
            sw_ref[...] += s_w
            sg_ref[...] += jnp.sum(dx * o_ref[rows, :], axis=0, keepdims=True)
            dx_ref[rows, :] = dx
            do_ref[rows, :] = (g_ref[...] * dx).astype(BF16)
        dap = _dot(do_ref[...], wt_ref[...])
        dattn_ref[...] = dap[:, :A].astype(BF16)
        dpool_ref[...] = dap[:, A:]

    tile = pl.BlockSpec((T, D), lambda i: (i, 0))
    row = pl.BlockSpec((1, D), lambda i: (0, 0))
    return pl.pallas_call(
        body, name="norm_mlp_bwd_out_proj_bwd", grid=(L // T,),
        in_specs=[tile, tile, tile, tile, row, row, row, pl.BlockSpec((D, N), lambda i: (0, 0))],
        out_specs=[tile, tile, pl.BlockSpec((T, A), lambda i: (i, 0)), pl.BlockSpec((T, N - A), lambda i: (i, 0)), row, row, row, row],
        out_shape=[jax.ShapeDtypeStruct((L, D), F32), jax.ShapeDtypeStruct((L, D), BF16), jax.ShapeDtypeStruct((L, A), BF16),
                   jax.ShapeDtypeStruct((L, N - A), F32)] + [jax.ShapeDtypeStruct((1, D), F32)] * 4,
        compiler_params=pltpu.CompilerParams(dimension_semantics=("arbitrary",), vmem_limit_bytes=FUSED_VMEM_LIMIT),
    )(x, dh, dres, o, w, sc, g, w_out_t)


def _out_proj_epilogue(acc, xr, g, w, sc, sh):
    x1 = xr + g * acc
    return acc, x1, _modulated_norm(x1, w, sc, sh)


def _mlp_down_epilogue(acc, x1, tgt, g, wf):
    D = acc.shape[1]
    x2 = x1 + g * acc
    r = _inv_rms(x2)
    xh = x2 * r
    err = xh * wf - tgt
    loss = 0.5 * jnp.sum(jnp.mean(err * err, axis=-1, keepdims=True), axis=0, keepdims=True)
    dy = err * (1.0 / D)
    dxh = dy * wf
    dx = r * (dxh - xh * jnp.mean(dxh * xh, axis=-1, keepdims=True))
    col = lambda v: jnp.sum(v, axis=0, keepdims=True)
    return dx, g * dx, col(dy * xh), col(dx * acc), jnp.broadcast_to(loss, (1, D))


def _heads(ref, first, n):
    return jnp.concatenate([ref[:, (first + g) * HEAD_DIM:(first + g + 1) * HEAD_DIM] for g in range(n)], axis=0)


def _edge_variants(masks):
    return jnp.asarray(np.stack([np.where(masks(first, last), 0.0, NEG_INF).astype(np.float32)
                                 for last in (False, True) for first in (False, True)]))


def _attn_bias(C):
    kj = np.arange(3 * BLOCK + C)[:, None]
    qi = (np.arange(GQA * BLOCK) % BLOCK)[None, :]

    def masks(first, last):
        window = (kj >= qi) & (kj <= qi + 2 * BLOCK) & (kj >= (BLOCK if first else 0)) & (kj < (2 * BLOCK if last else 3 * BLOCK))
        return window | (kj >= 3 * BLOCK)

    return _edge_variants(masks)


def _attn_bias_keys():
    kj = np.arange(BLOCK)[:, None]
    col = np.arange(3 * GQA * BLOCK)[None, :]
    part, qi = col // (GQA * BLOCK), col % BLOCK

    def masks(first, last):
        return ((part == 0) & (kj <= qi) & (not first)) | (part == 1) | ((part == 2) & (kj >= qi) & (not last))

    return _edge_variants(masks)


def _edge_index(n, nb):
    return (n == 0).astype(I32) + 2 * (n == nb - 1).astype(I32)


def _head_rows(ref, hk):
    return jnp.concatenate([ref[hk * GQA + g:hk * GQA + g + 1, :] for g in range(GQA)], axis=1)


def _rows_to_heads(rows_by_kv_head):
    return jnp.concatenate([r[:, g * BLOCK:(g + 1) * BLOCK] for r in rows_by_kv_head for g in range(GQA)], axis=0)


def _queries_to_rows(t):
    return jnp.concatenate([t[:, g * BLOCK:(g + 1) * BLOCK].T for g in range(GQA)], axis=1)


def _attn_specs(L, A, KV, C, vcol):
    nb = L // BLOCK
    kcol = A // KV
    prev = lambda n: jnp.maximum(n - 1, 0)
    nxt = lambda n: jnp.minimum(n + 1, nb - 1)
    q_spec = pl.BlockSpec((BLOCK, A), lambda n: (n, 0))
    k_specs = [pl.BlockSpec((BLOCK, KV), lambda n: (prev(n), kcol)), pl.BlockSpec((BLOCK, KV), lambda n: (n, kcol)),
               pl.BlockSpec((BLOCK, KV), lambda n: (nxt(n), kcol))]
    v_specs = [pl.BlockSpec((BLOCK, KV), lambda n: (prev(n), vcol)), pl.BlockSpec((BLOCK, KV), lambda n: (n, vcol)),
               pl.BlockSpec((BLOCK, KV), lambda n: (nxt(n), vcol))]
    kvc_spec = pl.BlockSpec((C, 2 * KV), lambda n: (0, 0))
    return q_spec, k_specs, v_specs, kvc_spec


def _keys_values(hk, k_refs, v_refs, kvc_ref, KV):
    sl = slice(hk * HEAD_DIM, (hk + 1) * HEAD_DIM)
    keys = jnp.concatenate([r[:, sl] for r in k_refs] + [kvc_ref[:, sl]], axis=0)
    vals = jnp.concatenate([r[:, sl].astype(BF16) for r in v_refs] + [kvc_ref[:, KV + hk * HEAD_DIM:KV + (hk + 1) * HEAD_DIM]], axis=0)
    return keys, vals


def _sink_row(sink_ref, hk):
    return jnp.concatenate([jnp.full((1, BLOCK), sink_ref[0, hk * GQA + g], F32) for g in range(GQA)], axis=1)


def _attn_fwd(qk, uv, kvc, sink, A, KV, P):
    L = qk.shape[0]
    C = kvc.shape[0]
    nkv = KV // HEAD_DIM
    H = nkv * GQA
    scale = HEAD_DIM ** -0.5

    def body(sink_ref, q_ref, kp_ref, kc_ref, kn_ref, vp_ref, vc_ref, vn_ref, kvc_ref, bias_ref, o_ref, lse_ref):
        bias = bias_ref[_edge_index(pl.program_id(0), L // BLOCK)]
        lse_rows = []
        for hk in range(nkv):
            keys, vals = _keys_values(hk, (kp_ref, kc_ref, kn_ref), (vp_ref, vc_ref, vn_ref), kvc_ref, KV)
            qs = _heads(q_ref, hk * GQA, GQA) * scale
            s = _dot_nt(keys, qs) + bias
            sk = _sink_row(sink_ref, hk)
            m = jnp.maximum(jnp.max(s, axis=0, keepdims=True), sk)
            p = jnp.exp(s - m)
            den = jnp.sum(p, axis=0, keepdims=True) + jnp.exp(sk - m)
            o = _dot_tn(vals, p.astype(BF16)) * (1.0 / den)
            lse_rows.append(m + jnp.log(den))
            o_ref[:, hk * GQA * HEAD_DIM:(hk + 1) * GQA * HEAD_DIM] = _queries_to_rows(o).astype(BF16)
        lse_ref[...] = _rows_to_heads(lse_rows)

    q_spec, k_specs, v_specs, kvc_spec = _attn_specs(L, A, KV, C, P // KV)
    bias = _attn_bias(C)
    return pl.pallas_call(
        body, name="attn_fwd", grid=(L // BLOCK,),
        in_specs=[pl.BlockSpec(memory_space=pltpu.SMEM), q_spec] + k_specs + v_specs
                 + [kvc_spec, pl.BlockSpec(bias.shape, lambda n: (0, 0, 0))],
        out_specs=[pl.BlockSpec((BLOCK, A), lambda n: (n, 0)), pl.BlockSpec((H, BLOCK), lambda n: (0, n))],
        out_shape=[jax.ShapeDtypeStruct((L, A + P), BF16), jax.ShapeDtypeStruct((H, L), F32)],
        compiler_params=_cparams("parallel"),
    )(sink, qk, qk, qk, qk, uv, uv, uv, kvc, bias)


def _attn_bwd_dq(qk, uv, kvc, sink, dap, lse_t, cos, sin, A, KV, P):
    L = qk.shape[0]
    C = kvc.shape[0]
    nkv = KV // HEAD_DIM
    H = nkv * GQA
    scale = HEAD_DIM ** -0.5
    W = 3 * BLOCK

    def body(sink_ref, q_ref, kp_ref, kc_ref, kn_ref, vp_ref, vc_ref, vn_ref, kvc_ref, do_ref, lse_ref, cos_ref, sin_ref, bias_ref,
             dq_ref, rd_ref, ds_ref, dkvc_ref):
        n = pl.program_id(0)

        @pl.when(n == 0)
        def _():
            dkvc_ref[...] = jnp.zeros_like(dkvc_ref)

        bias = bias_ref[_edge_index(n, L // BLOCK)]
        rd_rows, dsink_rows, dq_parts = [], [], []
        for hk in range(nkv):
            sl = slice(hk * HEAD_DIM, (hk + 1) * HEAD_DIM)
            keys, vals = _keys_values(hk, (kp_ref, kc_ref, kn_ref), (vp_ref, vc_ref, vn_ref), kvc_ref, KV)
            qs = _heads(q_ref, hk * GQA, GQA) * scale
            dos = _heads(do_ref, hk * GQA, GQA).astype(BF16)
            lse = _head_rows(lse_ref, hk)
            p = jnp.exp(_dot_nt(keys, qs) + bias - lse)
            dp = _dot_nt(vals, dos)
            rd = jnp.sum(p * dp, axis=0, keepdims=True)
            ds = (p * (dp - rd)).astype(BF16)
            dq_parts.append(_queries_to_rows(_dot_tn(keys, ds) * scale))
            dkvc_ref[:, sl] += _dot(ds[W:, :], qs)
            dkvc_ref[:, KV + hk * HEAD_DIM:KV + (hk + 1) * HEAD_DIM] += _dot(p[W:, :].astype(BF16), dos)
            rd_rows.append(rd)
            dsink_rows.append(-(jnp.exp(_sink_row(sink_ref, hk) - lse) * rd))
        rd_ref[...] = _rows_to_heads(rd_rows)
        ds_ref[...] = _rows_to_heads(dsink_rows)
        dq = dq_parts[0] if nkv == 1 else jnp.concatenate(dq_parts, axis=1)
        dq_ref[...] = _rope(dq, cos_ref[...], -sin_ref[...]).astype(BF16)

    q_spec, k_specs, v_specs, kvc_spec = _attn_specs(L, A, KV, C, P // KV)
    blk = lambda w: pl.BlockSpec((BLOCK, w), lambda n: (n, 0))
    per_head = pl.BlockSpec((H, BLOCK), lambda n: (0, n))
    bias = _attn_bias(C)
    return pl.pallas_call(
        body, name="attn_bwd_dq", grid=(L // BLOCK,),
        in_specs=[pl.BlockSpec(memory_space=pltpu.SMEM), q_spec] + k_specs + v_specs
                 + [kvc_spec, blk(A), per_head, blk(LANES), blk(LANES), pl.BlockSpec(bias.shape, lambda n: (0, 0, 0))],
        out_specs=[blk(A), per_head, per_head, pl.BlockSpec((C, 2 * KV), lambda n: (0, 0))],
        out_shape=[jax.ShapeDtypeStruct((L, A), BF16), jax.ShapeDtypeStruct((H, L), F32), jax.ShapeDtypeStruct((H, L), F32),
                   jax.ShapeDtypeStruct((C, 2 * KV), F32)],
        compiler_params=_cparams("arbitrary"),
    )(sink, qk, qk, qk, qk, uv, uv, uv, kvc, dap, lse_t, cos, sin, bias)


def _attn_bwd_dkv(qk, uv, dap, lse_t, rd_t, cos, sin, A, KV, P):
    L = qk.shape[0]
    nb = L // BLOCK
    nkv = KV // HEAD_DIM
    H = nkv * GQA
    scale = HEAD_DIM ** -0.5

    def body(k_ref, v_ref, qp_ref, qc_ref, qn_ref, dop_ref, doc_ref, don_ref, lsep_ref, lsec_ref, lsen_ref,
             rdp_ref, rdc_ref, rdn_ref, cos_ref, sin_ref, bias_ref, dk_ref, dv_ref):
        bias = bias_ref[_edge_index(pl.program_id(0), nb)]
        dk_parts, dv_parts = [], []
        for hk in range(nkv):
            sl = slice(hk * HEAD_DIM, (hk + 1) * HEAD_DIM)
            km = k_ref[:, sl]
            vm = v_ref[:, sl].astype(BF16)
            qs = jnp.concatenate([_heads(q, hk * GQA, GQA) for q in (qp_ref, qc_ref, qn_ref)], axis=0) * scale
            dos = jnp.concatenate([_heads(d, hk * GQA, GQA) for d in (dop_ref, doc_ref, don_ref)], axis=0).astype(BF16)
            rows = [slice(hk * GQA + g, hk * GQA + g + 1) for g in range(GQA)]
            lse = jnp.concatenate([t[r, :] for t in (lsep_ref, lsec_ref, lsen_ref) for r in rows], axis=1)
            rdv = jnp.concatenate([t[r, :] for t in (rdp_ref, rdc_ref, rdn_ref) for r in rows], axis=1)
            p = jnp.exp(_dot_nt(km, qs) + bias - lse)
            ds = (p * (_dot_nt(vm, dos) - rdv)).astype(BF16)
            dk_parts.append(_dot(ds, qs))
            dv_parts.append(_dot(p.astype(BF16), dos))
        dk = dk_parts[0] if nkv == 1 else jnp.concatenate(dk_parts, axis=1)
        dv = dv_parts[0] if nkv == 1 else jnp.concatenate(dv_parts, axis=1)
        dk_ref[...] = _rope(dk, cos_ref[...], -sin_ref[...]).astype(BF16)
        dv_ref[...] = dv.astype(BF16)

    prev = lambda m: jnp.maximum(m - 1, 0)
    nxt = lambda m: jnp.minimum(m + 1, nb - 1)
    three = lambda w: [pl.BlockSpec((BLOCK, w), lambda m: (prev(m), 0)), pl.BlockSpec((BLOCK, w), lambda m: (m, 0)),
                       pl.BlockSpec((BLOCK, w), lambda m: (nxt(m), 0))]
    three_t = [pl.BlockSpec((H, BLOCK), lambda m: (0, prev(m))), pl.BlockSpec((H, BLOCK), lambda m: (0, m)),
               pl.BlockSpec((H, BLOCK), lambda m: (0, nxt(m)))]
    blk = lambda w: pl.BlockSpec((BLOCK, w), lambda m: (m, 0))
    bias = _attn_bias_keys()
    return pl.pallas_call(
        body, name="attn_bwd_dkv", grid=(nb,),
        in_specs=[pl.BlockSpec((BLOCK, KV), lambda m: (m, A // KV)), pl.BlockSpec((BLOCK, KV), lambda m: (m, P // KV))]
                 + three(A) + three(A) + three_t + three_t + [blk(LANES), blk(LANES), pl.BlockSpec(bias.shape, lambda m: (0, 0, 0))],
        out_specs=[blk(KV), blk(KV)],
        out_shape=[jax.ShapeDtypeStruct((L, KV), BF16), jax.ShapeDtypeStruct((L, KV), BF16)],
        compiler_params=_cparams("parallel"),
    )(qk, uv, qk, qk, qk, dap, dap, dap, lse_t, lse_t, lse_t, rd_t, rd_t, rd_t, cos, sin, bias)


def _halo_specs(T, L, W, col):
    per = T // HALO
    return [pl.BlockSpec((HALO, W), lambda i: (jnp.maximum(i * per - 1, 0), col)),
            pl.BlockSpec((T, W), lambda i: (i, col)),
            pl.BlockSpec((HALO, W), lambda i: (jnp.minimum((i + 1) * per, L // HALO - 1), col))]


def _fill_halo_buf(buf, prev_ref, cur_ref, next_ref, i, nt, T):
    buf[0:HALO, :] = jnp.where(i > 0, prev_ref[...], 0.0)
    buf[HALO:HALO + T, :] = cur_ref[...]
    buf[HALO + T:2 * HALO + T, :] = jnp.where(i < nt - 1, next_ref[...], 0.0)


def _zero_margins(lv):
    rows = lv.shape[0]
    lv[0:HALO, :] = jnp.zeros((HALO, lv.shape[1]), F32)
    lv[rows - HALO:rows, :] = jnp.zeros((HALO, lv.shape[1]), F32)


def _window_sums(lv, x, w, first):
    n = x.shape[0]
    lv[HALO:HALO + n, :] = x
    cur = x + lv[pl.ds(HALO + first, n), :]
    span = 1
    while 2 * span < w:
        lv[HALO:HALO + n, :] = cur
        cur = lv[pl.ds(HALO - span, n), :] + lv[pl.ds(HALO + span, n), :]
        span *= 2
    return cur


def _counts(t, w, L):
    lo = jnp.clip(t - w // 2, 0, L)
    hi = jnp.clip(t - w // 2 + w, 0, L)
    return jnp.maximum(hi - lo, 1).astype(F32)


def _pool_fwd(u, pw, scale, mix):
    L, P = u.shape[0], scale.shape[1]
    gd = P // POOL_GROUPS
    T = _tile(L, 256, 8)
    nt = L // T
    assert (mix.shape[1] - P) % P == 0
    mix_col = mix.shape[1] // P - 1

    def body(up_ref, uc_ref, un_ref, pw_ref, sc_ref, mix_ref, out_ref, pooled_ref, buf, lv):
        i = pl.program_id(0)
        _fill_halo_buf(buf, up_ref, uc_ref, un_ref, i, nt, T)
        _zero_margins(lv)
        t = i * T + lax.broadcasted_iota(I32, (T, 1), 0)
        for g, w in enumerate(POOL_WINDOWS):
            cols = slice(g * gd, (g + 1) * gd)
            acc = _window_sums(lv, buf[:, cols], w, -1)[HALO:HALO + T]
            pooled = (acc / _counts(t, w, L) - uc_ref[:, cols]).astype(BF16)
            pooled_ref[:, cols] = pooled
            out_ref[:, cols] = (_dot(pooled, pw_ref[g]) * sc_ref[:, cols]).astype(BF16)

    return pl.pallas_call(
        body, name="pool_fwd", grid=(nt,),
        in_specs=_halo_specs(T, L, P, 0) + [pl.BlockSpec((POOL_GROUPS, gd, gd), lambda i: (0, 0, 0)), pl.BlockSpec((1, P), lambda i: (0, 0)),
                                            pl.BlockSpec(memory_space=pl.ANY)],
        out_specs=[pl.BlockSpec((T, P), lambda i: (i, mix_col)), pl.BlockSpec((T, P), lambda i: (i, 0))],
        out_shape=[jax.ShapeDtypeStruct(mix.shape, BF16), jax.ShapeDtypeStruct((L, P), BF16)],
        scratch_shapes=[pltpu.VMEM((T + 2 * HALO, P), F32), pltpu.VMEM((T + 4 * HALO, gd), F32)],
        input_output_aliases={5: 0},
        compiler_params=_cparams("parallel"),
    )(u, u, u, pw, scale, mix)


def _pool_bwd_mix(d_pool, pooled, pw, scale):
    L, P = pooled.shape
    gd = P // POOL_GROUPS
    T = _tile(L, 256, 8)

    def body(dp_ref, pooled_ref, pw_ref, sc_ref, dpooled_ref, dpw_ref, dsc_ref):
        i = pl.program_id(0)

        @pl.when(i == 0)
        def _():
            dpw_ref[...] = jnp.zeros_like(dpw_ref)
            dsc_ref[...] = jnp.zeros_like(dsc_ref)

        for g in range(POOL_GROUPS):
            cols = slice(g * gd, (g + 1) * gd)
            pb = pooled_ref[:, cols]
            dp = dp_ref[:, cols]
            dsc_ref[:, cols] += jnp.sum(dp * _dot(pb, pw_ref[g]), axis=0, keepdims=True)
            dm = (dp * sc_ref[:, cols]).astype(BF16)
            dpw_ref[g] += _dot_tn(pb, dm)
            dpooled_ref[:, cols] = _dot_nt(dm, pw_ref[g])

    return pl.pallas_call(
        body, name="pool_bwd_mix", grid=(L // T,),
        in_specs=[pl.BlockSpec((T, P), lambda i: (i, 0)), pl.BlockSpec((T, P), lambda i: (i, 0)),
                  pl.BlockSpec((POOL_GROUPS, gd, gd), lambda i: (0, 0, 0)), pl.BlockSpec((1, P), lambda i: (0, 0))],
        out_specs=[pl.BlockSpec((T, P), lambda i: (i, 0)), pl.BlockSpec((POOL_GROUPS, gd, gd), lambda i: (0, 0, 0)),
                   pl.BlockSpec((1, P), lambda i: (0, 0))],
        out_shape=[jax.ShapeDtypeStruct((L, P), F32), jax.ShapeDtypeStruct((POOL_GROUPS, gd, gd), F32), jax.ShapeDtypeStruct((1, P), F32)],
        compiler_params=_cparams("arbitrary"),
    )(d_pool, pooled, pw, scale)


def _pool_bwd_window(dpooled):
    L, P = dpooled.shape
    gd = P // POOL_GROUPS
    T = _tile(L, 256, 8)
    nt = L // T

    def body(dp_ref, dc_ref, dn_ref, du_ref, buf, lv):
        i = pl.program_id(0)
        _fill_halo_buf(buf, dp_ref, dc_ref, dn_ref, i, nt, T)
        _zero_margins(lv)
        t = i * T - HALO + lax.broadcasted_iota(I32, (T + 2 * HALO, 1), 0)
        for g, w in enumerate(POOL_WINDOWS):
            cols = slice(g * gd, (g + 1) * gd)
            acc = _window_sums(lv, buf[:, cols] / _counts(t, w, L), w, 1)[HALO:HALO + T]
            du_ref[:, cols] = (acc - dc_ref[:, cols]).astype(BF16)

    return pl.pallas_call(
        body, name="pool_bwd_window", grid=(nt,),
        in_specs=_halo_specs(T, L, P, 0),
        out_specs=pl.BlockSpec((T, P), lambda i: (i, 0)),
        out_shape=jax.ShapeDtypeStruct((L, P), BF16),
        scratch_shapes=[pltpu.VMEM((T + 2 * HALO, P), F32), pltpu.VMEM((T + 4 * HALO, gd), F32)],
        compiler_params=_cparams("parallel"),
    )(dpooled, dpooled, dpooled)


def _sum_rows(name, a):
    R, N = a.shape

    def body(a_ref, o_ref):
        if R <= 16:
            acc = a_ref[0:1, :]
            for r in range(1, R):
                acc = acc + a_ref[r:r + 1, :]
        else:
            acc = jnp.sum(a_ref[...], axis=0, keepdims=True)
        o_ref[...] = acc

    return pl.pallas_call(body, name=name, out_shape=jax.ShapeDtypeStruct((1, N), F32))(a)


def _sum_lanes(name, a):
    def body(a_ref, o_ref):
        o_ref[...] = jnp.sum(a_ref[...], axis=1, keepdims=True)

    return pl.pallas_call(body, name=name, out_shape=jax.ShapeDtypeStruct((a.shape[0], 1), F32))(a)


def _silu_grad_mul(cv, g):
    def body(c_ref, g_ref, o_ref):
        cvv = c_ref[...]
        s = 1.0 / (1.0 + jnp.exp(-cvv))
        o_ref[...] = g_ref[...] * (s * (1.0 + cvv * (1.0 - s)))

    return pl.pallas_call(body, name="silu_grad_mul", out_shape=jax.ShapeDtypeStruct(cv.shape, F32))(cv, g)


def _adamw(name, w, g, m, v):
    R, C = w.shape
    parts = g.ndim == 3
    n_parts = g.shape[0] if parts else 1
    T = _tile(R, max(8, 262144 // C), 8)

    def body(w_ref, g_ref, m_ref, v_ref, go_ref, d_ref, mo_ref, vo_ref):
        if parts:
            gv = g_ref[0].astype(F32)
            for p in range(1, n_parts):
                gv = gv + g_ref[p].astype(F32)
        else:
            gv = g_ref[...]
        mn = ADAM_B1 * m_ref[...] + (1.0 - ADAM_B1) * gv
        vn = ADAM_B2 * v_ref[...] + (1.0 - ADAM_B2) * (gv * gv)
        m_hat = mn / (1.0 - ADAM_B1 ** ADAM_STEP)
        v_hat = vn / (1.0 - ADAM_B2 ** ADAM_STEP)
        go_ref[...] = gv
        d_ref[...] = -ADAM_LR * (m_hat / (jnp.sqrt(v_hat) + ADAM_EPS) + ADAM_WD * w_ref[...])
        mo_ref[...] = mn
        vo_ref[...] = vn

    tile = pl.BlockSpec((T, C), lambda i: (i, 0))
    g_spec = pl.BlockSpec((n_parts, T, C), lambda i: (0, i, 0)) if parts else tile
    return pl.pallas_call(
        body, name=name, grid=(R // T,),
        in_specs=[tile, g_spec, tile, tile], out_specs=[tile] * 4,
        out_shape=[jax.ShapeDtypeStruct((R, C), F32)] * 4,
        compiler_params=_cparams("parallel"),
    )(w, g, m, v)


def _dev_index(px, py, pc):
    return 4 * px + 2 * py + pc


def _all_gather(name, arrs):
    n = len(arrs)

    def body(*refs):
        ins, outs = refs[:n], refs[n:2 * n]
        send_sems, recv_sems, local_sems = refs[2 * n:]
        x, y, c = lax.axis_index("x"), lax.axis_index("y"), lax.axis_index("c")
        me, sibling = (x, y, c), (x, y, 1 - c)
        chips = [(1 - x, y), (x, 1 - y), (1 - x, 1 - y)]

        def copy(a, k, block, to, src=None):
            slot = outs[a].at[_dev_index(*block)]
            return pltpu.make_async_remote_copy(
                src_ref=slot if src is None else src, dst_ref=slot, send_sem=send_sems.at[a, k], recv_sem=recv_sems.at[a, k],
                device_id=to, device_id_type=MESH)

        mine = [pltpu.make_async_copy(ins[a], outs[a].at[_dev_index(*me)], local_sems.at[a]) for a in range(n)]
        for cp in mine:
            cp.start()
        first = []
        for a in range(n):
            first.append(copy(a, 0, me, sibling, src=ins[a]))
            first += [copy(a, 1 + j, me, (*chip, c), src=ins[a]) for j, chip in enumerate(chips)]
        for cp in first:
            cp.start()
        passed = []
        for j, chip in enumerate(chips):
            for a in range(n):
                copy(a, 1 + j, (*chip, c), me).wait_recv()
                fwd = copy(a, 4 + j, (*chip, c), sibling)
                fwd.start()
                passed.append(fwd)
        for a in range(n):
            copy(a, 0, sibling, me).wait_recv()
            for j, chip in enumerate(chips):
                copy(a, 4 + j, (*chip, 1 - c), me).wait_recv()
        for cp in first + passed:
            cp.wait_send()
        for cp in mine:
            cp.wait()

    return pl.pallas_call(
        body, name=name,
        in_specs=[HBM] * n, out_specs=[HBM] * n,
        out_shape=[jax.ShapeDtypeStruct((N_DEV, *a.shape), a.dtype) for a in arrs],
        scratch_shapes=[pltpu.SemaphoreType.DMA((n, N_DEV - 1)), pltpu.SemaphoreType.DMA((n, N_DEV - 1)), pltpu.SemaphoreType.DMA((n,))],
    )(*arrs)


N_COPIES = {"all_to_all": N_DEV - 1, "gather_chips": 4, "forward": 3}


def _exchange_copies(kind, src_ref, land_ref, send_sems, recv_sems, sending):
    x, y, c = lax.axis_index("x"), lax.axis_index("y"), lax.axis_index("c")
    me = _dev_index(x, y, c)
    others = [(1 - x, y), (x, 1 - y), (1 - x, 1 - y)]
    if kind == "all_to_all":
        flips = [(dx, dy, dc) for dx in (0, 1) for dy in (0, 1) for dc in (0, 1)][1:]
        peers = [(1 - x if dx else x, 1 - y if dy else y, 1 - c if dc else c) for dx, dy, dc in flips]
        plan = [(p, src_ref.at[_dev_index(*p)], me if sending else _dev_index(*p)) for p in peers]
    elif kind == "gather_chips":
        peers = [(x, y, 1 - c)] + [(*o, c) for o in others]
        plan = [(p, src_ref, me if sending else _dev_index(*p)) for p in peers]
    else:
        plan = [((x, y, 1 - c), land_ref.at[_dev_index(*o, c)], _dev_index(*o, c if sending else 1 - c)) for o in others]
    return [pltpu.make_async_remote_copy(src_ref=src, dst_ref=land_ref.at[slot], send_sem=send_sems.at[k], recv_sem=recv_sems.at[k],
                                         device_id=peer, device_id_type=MESH)
            for k, (peer, src, slot) in enumerate(plan)]


def _exchange_start(name, kind, srcs, lands=None):
    if lands is None:
        lands = [lax.empty((N_DEV, *s.shape) if kind == "gather_chips" else s.shape, s.dtype) for s in srcs]
    n = len(lands)
    ops = ([] if srcs is None else list(srcs)) + list(lands)
    m = len(ops)

    def body(*refs):
        src_refs = [None] * n if srcs is None else refs[:n]
        land_refs = refs[m - n:m]
        send_sems, recv_sems, token = refs[m:m + n], refs[m + n:m + 2 * n], refs[-1]
        for a in range(n):
            for cp in _exchange_copies(kind, src_refs[a], land_refs[a], send_sems[a], recv_sems[a], True):
                cp.start()
        token[...] = jnp.zeros_like(token)

    sems = [pltpu.SemaphoreType.DMA((N_COPIES[kind],))] * (2 * n)
    outs = pl.pallas_call(
        body, name=name,
        out_shape=sems + [pltpu.HBM(o.shape, o.dtype) for o in ops] + [jax.ShapeDtypeStruct((8, LANES), F32)],
        in_specs=[HBM] * m,
        out_specs=[SEM] * (2 * n) + [HBM] * m + [pl.BlockSpec(memory_space=pltpu.VMEM)],
        input_output_aliases={i: 2 * n + i for i in range(m)},
        compiler_params=SIDE_EFFECT,
    )(*[pltpu.with_memory_space_constraint(o, pltpu.HBM) for o in ops])
    thru = outs[2 * n:2 * n + m]
    return outs[:n], outs[n:2 * n], (None if srcs is None else thru[:n]), thru[m - n:], outs[-1]


def _exchange_wait(name, kind, send_sems, recv_sems, srcs, lands, after):
    n = len(lands)
    ops = ([] if srcs is None else list(srcs)) + list(lands)
    m = len(ops)

    def body(*refs):
        src_refs = [None] * n if srcs is None else refs[:n]
        land_refs = refs[m - n:m]
        send_refs, recv_refs = refs[m:m + n], refs[m + n:m + 2 * n]
        for a in range(n):
            for cp in _exchange_copies(kind, src_refs[a], land_refs[a], send_refs[a], recv_refs[a], False):
                cp.wait_send()
                cp.wait_recv()

    outs = pl.pallas_call(
        body, name=name,
        out_shape=[pltpu.HBM(o.shape, o.dtype) for o in ops],
        in_specs=[HBM] * m + [SEM] * (2 * n) + [pl.BlockSpec(memory_space=pl.ANY)],
        out_specs=[HBM] * m,
        input_output_aliases={i: i for i in range(m)},
        compiler_params=SIDE_EFFECT,
    )(*ops, *send_sems, *recv_sems, after)
    return (None if srcs is None else outs[:n]), outs[m - n:]


def _with_own(land, own, me):
    return lax.dynamic_update_slice_in_dim(land, own, me, 0)


def _shards_to_cols(g):
    return jnp.transpose(g, (1, 0, 2)).reshape(g.shape[1], N_DEV * g.shape[2])


def _cols_to_shards(a):
    R, Ctot = a.shape
    return jnp.transpose(a.reshape(R, N_DEV, Ctot // N_DEV), (1, 0, 2))


def kernel(x, c, ctx, c_ctx, norm_attn_w, norm_mlp_w, w_ada, b_ada, w_in, attn_sink, pool_w, pool_scale, w_out, w_mlp_up, w_mlp_down, final_norm_w, loss_target, m_c_ctx, m_norm_attn_w, m_norm_mlp_w, m_w_ada, m_b_ada, m_w_in, m_attn_sink, m_pool_w, m_pool_scale, m_w_out, m_w_mlp_up, m_w_mlp_down, m_final_norm_w, v_c_ctx, v_norm_attn_w, v_norm_mlp_w, v_w_ada, v_b_ada, v_w_in, v_attn_sink, v_pool_w, v_pool_scale, v_w_out, v_w_mlp_up, v_w_mlp_down, v_final_norm_w):
    _, L, D = x.shape
    H = attn_sink.shape[1]
    A = H * HEAD_DIM
    KV = A // GQA
    P = pool_scale.shape[1]
    MODW = N_MOD * D
    ws = MODW // N_DEV
    gd = P // POOL_GROUPS
    me = _dev_index(lax.axis_index("x"), lax.axis_index("y"), lax.axis_index("c"))

    x2d, ctx2d, tgt = x[0], ctx[0], loss_target[0]
    cctx_row = c_ctx.reshape(1, D)
    wf_row = final_norm_w.reshape(1, D)
    w_ada_l = w_ada[0]
    pool_w_l = pool_w[0].reshape(POOL_GROUPS * (gd // N_DEV), gd)

    (c_all,) = _all_gather("gather_cond", [c])
    cond = jnp.concatenate([c_all[:, 0, :], cctx_row, jnp.zeros((COND_ROWS - N_DEV - 1, D), F32)], axis=0)
    b_sh = lax.dynamic_slice_in_dim(b_ada, me * ws, ws, axis=1)
    (mods_sh,) = _mm("ada_mod", cond, w_ada_l, "nn", [F32], SMALL_TILES, a_pre=_silu, extras=[("n", b_sh)], epilogue=lambda acc, b: (acc + b,))
    (mods_g,) = _all_gather("gather_mods", [mods_sh])

    w_srcs = [w_in[0].astype(BF16), w_out[0].astype(BF16), pool_w_l.astype(BF16), w_mlp_up[0].astype(BF16), w_mlp_down[0].astype(BF16)]
    w_srcs, mods_g = lax.optimization_barrier((w_srcs, mods_g))
    gather_start = _exchange_start("gather_weights_start", "gather_chips", w_srcs)

    def weights(tag, started, lo, hi, after_chips, after_forward):
        gw_send, gw_recv, gw_src, gw_land, _ = started
        mine, lands = _exchange_wait(f"gather_{tag}_wait", "gather_chips", gw_send[lo:hi], gw_recv[lo:hi], gw_src[lo:hi],
                                     gw_land[lo:hi], after_chips)
        f_send, f_recv, _, f_land, f_token = _exchange_start(f"forward_{tag}_start", "forward", None, lands)
        _, lands = _exchange_wait(f"forward_{tag}_wait", "forward", f_send, f_recv, None, f_land,
                                  f_token if after_forward is None else after_forward)
        return [_with_own(l, s[None], me) for l, s in zip(lands, mine)]

    mods = _shards_to_cols(mods_g)
    mod_b = lax.dynamic_slice_in_dim(mods, me, 1, axis=0)
    sh_a, sc_a, g_a, sh_m, sc_m, g_m = [mod_b[:, i * D:(i + 1) * D] for i in range(N_MOD)]
    csh_a, csc_a = mods[N_DEV:N_DEV + 1, :D], mods[N_DEV:N_DEV + 1, D:2 * D]

    cos, sin = _rope_tables(L)
    h = _norm_fwd("norm_attn", x2d, norm_attn_w, sc_a, sh_a)
    hc = _norm_fwd("norm_attn_ctx", ctx2d, norm_attn_w, csc_a, csh_a)
    (win_g,) = weights("w_in", gather_start, 0, 1, h, None)
    W_in = _shards_to_cols(win_g)
    W_qk, W_kv = W_in[:, :A + KV], W_in[:, A:A + 2 * KV]
    W_uv = jnp.concatenate([W_in[:, A + 2 * KV:], W_in[:, A + KV:A + 2 * KV]], axis=1)
    (qk,) = _mm("in_proj_qk", h, W_qk, "nn", [BF16], (1024, A + KV, D), extras=[("m", cos), ("m", sin)],
                epilogue=lambda acc, cs, sn: (_rope(acc, cs, sn),))
    (uv,) = _mm("in_proj_uv", h, W_uv, "nn", [F32], (1024, P + KV, D))
    (kvc,) = _mm("in_proj_ctx", hc, W_kv, "nn", [BF16], SMALL_TILES)
    attn, lse = _attn_fwd(qk, uv, kvc, attn_sink, A, KV, P)
    wout_g, pw_g = weights("w_out", gather_start, 1, 3, qk, attn)
    W_out = wout_g.reshape(A + P, D)
    PW = jnp.transpose(pw_g.reshape(N_DEV, POOL_GROUPS, gd // N_DEV, gd), (1, 0, 2, 3)).reshape(POOL_GROUPS, gd, gd)
    ap, pooled = _pool_fwd(uv, PW, pool_scale, attn)
    o, x1, hm = _mm("out_proj_norm", ap, W_out, "nn", [F32, F32, BF16], (256, D, D), chunk=128, epilogue=_out_proj_epilogue,
                    extras=[("mn", x2d), ("n", g_a), ("n", norm_mlp_w), ("n", sc_m), ("n", sh_m)])
    W_up, wdown_g = weights("w_mlp", gather_start, 3, 5, attn, x1)
    W_down = wdown_g.reshape(-1, D)
    up, act = _mm("mlp_up", hm, W_up, "nn", [F32, BF16], (1024, 1024, 2048), epilogue=lambda acc: (acc, _relu2(acc)), b_shards=True)
    d_x2, d_mlp, d_wf, d_gm, loss_row = _mm(
        "mlp_down_loss", act, W_down, "nn", [F32, BF16], (512, D, 1024), chunk=128, n_sums=3, vmem=FUSED_VMEM_LIMIT,
        epilogue=_mlp_down_epilogue, extras=[("mn", x1), ("mn", tgt), ("n", g_m), ("n", wf_row)])
    loss_p = loss_row[:, :1]

    (d_up,) = _mm("mlp_down_bwd_act", d_mlp, W_down, "nt", [BF16], (1024, 1024, 2048), extras=[("mn", up)],
                  epilogue=lambda acc, uu: (acc * (2.0 * jnp.maximum(uu, 0.0)),))
    (gW_down,) = _mm("mlp_down_bwd_w", act, d_mlp, "tn", [BF16], (1024, 2048, 2048), vmem=FUSED_VMEM_LIMIT)
    (gW_up_s,) = _mm("mlp_up_bwd_w", hm, d_up, "tn", [BF16], (2048, 1024, 2048), out_shards=True, vmem=FUSED_VMEM_LIMIT)
    g_mlp_srcs = [gW_up_s, gW_down.reshape(N_DEV, -1, D)]
    g_mlp = _exchange_start("grads_mlp_start", "all_to_all", g_mlp_srcs)
    (d_hm,) = _mm("mlp_up_bwd_act", d_up, W_up, "nt", [F32], (1024, D, 1024), b_shards=True)

    d_x1, d_o, d_attn, d_pool, s_sh_m, s_sc_m, s_w_nm, d_ga = _norm_bwd_out_proj_bwd(
        x1, d_hm, d_x2, o, norm_mlp_w, sc_m + g_mlp[4][0, 0], g_a, W_out.T, A)
    (gW_out,) = _mm("out_proj_bwd_w", ap, d_o, "tn", [BF16], (1024, 2048, 2048))
    d_pooled, gPW, d_pscale = _pool_bwd_mix(d_pool, pooled, PW, pool_scale)
    gpw_s = jnp.transpose(gPW.astype(BF16).reshape(POOL_GROUPS, N_DEV, gd // N_DEV, gd), (1, 0, 2, 3)).reshape(N_DEV, -1, gd)
    g_mix_srcs = [gW_out.reshape(N_DEV, (A + P) // N_DEV, D), gpw_s]
    g_mix = _exchange_start("grads_mix_start", "all_to_all", g_mix_srcs)
    lse = lse + g_mix[4][0, 0]
    d_u = _pool_bwd_window(d_pooled)
    d_q, rd, dsink_q, d_kvc = _attn_bwd_dq(qk, uv, kvc, attn_sink, d_attn, lse, cos, sin, A, KV, P)
    d_k, d_v = _attn_bwd_dkv(qk, uv, d_attn, lse, rd, cos, sin, A, KV, P)
    d_sink = _sum_lanes("sink_grad", dsink_q).reshape(1, H)
    d_p = jnp.concatenate([d_q, d_k, d_v, d_u], axis=1)
    d_kvc_b = d_kvc.astype(BF16)
    (gW_kv_ctx,) = _mm("in_proj_ctx_bwd_w", hc, d_kvc_b, "tn", [F32], SMALL_TILES)
    (d_hc,) = _mm("in_proj_ctx_bwd_act", d_kvc_b, W_kv, "nt", [F32], SMALL_TILES)
    gW_in_init = jnp.pad(gW_kv_ctx, ((0, 0), (A, P)))
    (gW_in,) = _mm("in_proj_bwd_w", h, d_p, "tn", [BF16], (1024, 1280, 2048), extras=[("mn", gW_in_init)], epilogue=lambda acc, init: (acc + init,))
    g_in_srcs = [_cols_to_shards(gW_in)]
    g_in = _exchange_start("grads_in_start", "all_to_all", g_in_srcs)
    grad_x, s_sh_a, s_sc_a, s_w_na = _mm(
        "in_proj_bwd_norm", d_p, W_in, "nt", [F32], (256, D, A + 2 * KV + P), chunk=128, n_sums=3,
        epilogue=lambda acc, xr, dres, w, sc: _modulated_norm_bwd(xr, acc, dres, w, sc),
        extras=[("mn", x2d), ("mn", d_x1), ("n", norm_attn_w), ("n", sc_a + g_in[4][0, 0])])
    s_csh, s_csc, s_w_na = _norm_bwd_sums("norm_attn_ctx_bwd", ctx2d, d_hc, norm_attn_w, csc_a, s_w_na)

    pad_l = lambda a: jnp.pad(a, ((0, 0), (0, LANES - a.shape[1])))
    d_mod_b = jnp.concatenate([s_sh_a, s_sc_a, d_ga, s_sh_m, s_sc_m, d_gm], axis=1)
    summed = jnp.concatenate([s_csh, s_csc, s_w_na, s_w_nm, d_wf, d_pscale, pad_l(d_sink), pad_l(loss_p)], axis=1)
    (small_g,) = _all_gather("gather_small", [jnp.concatenate([d_mod_b, summed], axis=1)])
    small_g = small_g[:, 0, :]
    tot = _sum_rows("small_sum", small_g[:, MODW:])
    off = [0]
    for wdt in (D, D, D, D, D, P, LANES, LANES):
        off.append(off[-1] + wdt)
    seg = lambda i: tot[:, off[i]:off[i + 1]]
    g_norm_attn, g_norm_mlp, g_final, g_pscale = seg(2), seg(3), seg(4), seg(5)
    g_sink, loss = seg(6)[:, :H], seg(7)[0, 0]
    d_mod_ctx = jnp.concatenate([seg(0), seg(1), jnp.zeros((1, MODW - 2 * D), F32)], axis=1)
    d_mod = jnp.concatenate([small_g[:, :MODW], d_mod_ctx, jnp.zeros((COND_ROWS - N_DEV - 1, MODW), F32)], axis=0)
    g_b_ada = _sum_rows("b_ada_grad", d_mod[:N_DEV + 1])
    d_mod_sh = lax.dynamic_slice_in_dim(d_mod, me * ws, ws, axis=1)
    (g_w_ada,) = _mm("ada_bwd_w", cond, d_mod_sh, "tn", [F32], SMALL_TILES, a_pre=_silu)
    (d_cond_p,) = _mm("ada_bwd_cond", d_mod_sh, w_ada_l, "nt", [F32], SMALL_TILES)
    (d_cctx_g,) = _all_gather("gather_cctx", [d_cond_p[N_DEV:N_DEV + 1]])
    g_c_ctx = _silu_grad_mul(cctx_row, _sum_rows("cctx_sum", d_cctx_g[:, 0, :]))

    def arrived(name, started):
        srcs, lands = _exchange_wait(name, "all_to_all", started[0], started[1], started[2], started[3], g_c_ctx)
        return [_with_own(l, lax.dynamic_index_in_dim(s, me, 0, keepdims=True), me) for l, s in zip(lands, srcs)]

    r_up, r_down = arrived("grads_mlp_wait", g_mlp)
    r_out, r_pw = arrived("grads_mix_wait", g_mix)
    (r_in,) = arrived("grads_in_wait", g_in)

    results = {
        "c_ctx": _adamw("adam_c_ctx", cctx_row, g_c_ctx, m_c_ctx.reshape(1, D), v_c_ctx.reshape(1, D)),
        "norm_attn_w": _adamw("adam_norm_attn", norm_attn_w, g_norm_attn, m_norm_attn_w, v_norm_attn_w),
        "norm_mlp_w": _adamw("adam_norm_mlp", norm_mlp_w, g_norm_mlp, m_norm_mlp_w, v_norm_mlp_w),
        "w_ada": _adamw("adam_w_ada", w_ada_l, g_w_ada, m_w_ada[0], v_w_ada[0]),
        "b_ada": _adamw("adam_b_ada", b_ada, g_b_ada, m_b_ada, v_b_ada),
        "w_in": _adamw("adam_w_in", w_in[0], r_in, m_w_in[0], v_w_in[0]),
        "attn_sink": _adamw("adam_sink", attn_sink, g_sink, m_attn_sink, v_attn_sink),
        "pool_w": _adamw("adam_pool_w", pool_w_l, r_pw, m_pool_w[0].reshape(pool_w_l.shape), v_pool_w[0].reshape(pool_w_l.shape)),
        "pool_scale": _adamw("adam_pool_scale", pool_scale, g_pscale, m_pool_scale, v_pool_scale),
        "w_out": _adamw("adam_w_out", w_out[0], r_out, m_w_out[0], v_w_out[0]),
        "w_mlp_up": _adamw("adam_w_up", w_mlp_up[0], r_up, m_w_mlp_up[0], v_w_mlp_up[0]),
        "w_mlp_down": _adamw("adam_w_down", w_mlp_down[0], r_down, m_w_mlp_down[0], v_w_mlp_down[0]),
        "final_norm_w": _adamw("adam_final_norm", wf_row, g_final, m_final_norm_w.reshape(1, D), v_final_norm_w.reshape(1, D)),
    }
    shapes = {"c_ctx": c_ctx.shape, "norm_attn_w": norm_attn_w.shape, "norm_mlp_w": norm_mlp_w.shape, "w_ada": w_ada.shape,
              "b_ada": b_ada.shape, "w_in": w_in.shape, "attn_sink": attn_sink.shape, "pool_w": pool_w.shape,
              "pool_scale": pool_scale.shape, "w_out": w_out.shape, "w_mlp_up": w_mlp_up.shape, "w_mlp_down": w_mlp_down.shape,
              "final_norm_w": final_norm_w.shape}
    outs = [loss, grad_x.reshape(x.shape)]
    for part in range(4):
        outs += [results[name][part].reshape(shape) for name, shape in shapes.items()]
    return tuple(outs)
```

```python
import jax
import jax.numpy as jnp
import numpy as np
from jax import lax
from jax.experimental import pallas as pl
from jax.experimental.pallas import tpu as pltpu

F32 = jnp.float32
BF16 = jnp.bfloat16
I32 = jnp.int32

HEAD_DIM = 64
GQA = 4
BLOCK = 128
GRID_W = 64
ROPE_BASE = 10000.0
POOL_WINDOWS = (2, 4, 8, 16)
POOL_GROUPS = len(POOL_WINDOWS)
HALO = 8
N_MOD = 6
EPS = 1e-6
NEG_INF = -1e30
ADAM_LR = 0.001
ADAM_B1 = 0.9
ADAM_B2 = 0.999
ADAM_EPS = 1e-08
ADAM_WD = 0.01
ADAM_STEP = 10
N_DEV = 8
COND_ROWS = 2 * N_DEV
LANES = 128
SUBLANES_16BIT = 16
VMEM_LIMIT = 48 * 1024 * 1024
FUSED_VMEM_LIMIT = 56 * 1024 * 1024
SMALL_TILES = (512, 1024, 512)
MESH = pl.DeviceIdType.MESH
HBM = pl.BlockSpec(memory_space=pltpu.HBM)
SEM = pl.BlockSpec(memory_space=pltpu.SEMAPHORE)
SIDE_EFFECT = pltpu.CompilerParams(has_side_effects=pltpu.SideEffectType.DATAFLOW_SIDE_EFFECTING)


def _cparams(*sem):
    return pltpu.CompilerParams(dimension_semantics=sem, vmem_limit_bytes=VMEM_LIMIT)


def _tile(n, pref, align):
    if n <= pref:
        return n
    t = (pref // align) * align
    while t >= align:
        if n % t == 0:
            return t
        t -= align
    return n


def _dot(a, b):
    return lax.dot_general(a, b, (((1,), (0,)), ((), ())), preferred_element_type=F32)


def _dot_nt(a, b):
    return lax.dot_general(a, b, (((1,), (1,)), ((), ())), preferred_element_type=F32)


def _dot_tn(a, b):
    return lax.dot_general(a, b, (((0,), (0,)), ((), ())), preferred_element_type=F32)


_DOTS = {"nn": _dot, "nt": _dot_nt, "tn": _dot_tn}


def _mm(name, a, b, mode, out_dtypes, tiles, *, epilogue=None, extras=(), a_pre=None, n_sums=0, chunk=None,
        b_shards=False, out_shards=False, vmem=VMEM_LIMIT):
    if mode == "nn":
        M, K = a.shape
        K2, N = (b.shape[1], N_DEV * b.shape[2]) if b_shards else b.shape
    elif mode == "nt":
        M, K = a.shape
        N, K2 = (b.shape[1], N_DEV * b.shape[2]) if b_shards else b.shape
    else:
        (K, M), (K2, N) = a.shape, b.shape
    assert K == K2 and not (b_shards and mode == "tn"), (name, a.shape, b.shape)
    n_span = N // N_DEV if out_shards or (b_shards and mode == "nn") else N
    k_span = K // N_DEV if b_shards and mode == "nt" else K
    tm = _tile(M, tiles[0], LANES if mode == "tn" else SUBLANES_16BIT)
    tn = _tile(n_span, tiles[1], LANES)
    tk = _tile(k_span, tiles[2], SUBLANES_16BIT if mode == "tn" else LANES)
    nk, nb, kb = K // tk, n_span // tn, k_span // tk
    rows = tm if chunk is None else min(chunk, tm)
    n_ex, n_out = len(extras), len(out_dtypes)
    use_acc = nk > 1 or rows < tm
    assert n_sums == 0 or N == tn, name
    streamed = [e for e, (kind, _) in enumerate(extras) if kind == "mn_stream"]
    assert not streamed or (N == tn and rows < tm), name

    def product(a_ref, b_ref):
        at = a_ref[...]
        if a_pre is not None:
            at = a_pre(at)
        return _DOTS[mode](at.astype(BF16), b_ref[...].astype(BF16))

    def apply(acc, ex, out_refs, sl):
        res = (acc,) if epilogue is None else epilogue(acc, *ex)
        for o_ref, o in zip(out_refs, res[:n_out]):
            o_ref[sl, :] = o.astype(o_ref.dtype)
        return tuple(res[n_out:])

    def finish(acc, ex_refs, out_refs, sum_refs, stream_refs=()):
        if rows == tm:
            acc = acc if not use_acc else acc[...]
            sums = apply(acc, [r[...] for r in ex_refs], out_refs, slice(None))
        else:
            n_chunks = tm // rows
            bufs, sems = stream_refs[0::2], stream_refs[1::2]

            def chunk_copies(ci, slot):
                row0 = pl.multiple_of(pl.program_id(0) * tm + ci * rows, rows)
                return [pltpu.make_async_copy(ex_refs[e].at[pl.ds(row0, rows), :], bufs[x].at[slot], sems[x].at[slot])
                        for x, e in enumerate(streamed)]

            for cp in chunk_copies(0, 0):
                cp.start()

            def one(ci, sums):
                sl = pl.ds(pl.multiple_of(ci * rows, rows), rows)
                slot = ci % 2
                for cp in chunk_copies(ci, slot):
                    cp.wait()

                @pl.when(ci + 1 < n_chunks)
                def _():
                    for cp in chunk_copies(ci + 1, 1 - slot):
                        cp.start()

                ex = [r[...] if kind == "n" else (bufs[streamed.index(e)][slot] if kind == "mn_stream" else r[sl, :])
                      for e, ((kind, _), r) in enumerate(zip(extras, ex_refs))]
                return tuple(s + v for s, v in zip(sums, apply(acc[sl, :], ex, out_refs, sl)))
            sums = lax.fori_loop(0, n_chunks, one, tuple(jnp.zeros((1, tn), F32) for _ in range(n_sums)))
        first = pl.program_id(0) == 0
        for s_ref, sv in zip(sum_refs, sums):
            @pl.when(first)
            def _(s_ref=s_ref, sv=sv):
                s_ref[...] = sv

            @pl.when(jnp.logical_not(first))
            def _(s_ref=s_ref, sv=sv):
                s_ref[...] += sv

    def body(a_ref, b_ref, *rest):
        ex_refs, out_refs = rest[:n_ex], rest[n_ex:n_ex + n_out]
        sum_refs = rest[n_ex + n_out:n_ex + n_out + n_sums]
        if not use_acc:
            finish(product(a_ref, b_ref), ex_refs, out_refs, sum_refs)
            return
        scratch = rest[n_ex + n_out + n_sums:]
        acc_ref, stream_refs = scratch[0], scratch[1:]
        k = pl.program_id(2)

        @pl.when(k == 0)
        def _():
            acc_ref[...] = product(a_ref, b_ref)

        @pl.when(k > 0)
        def _():
            acc_ref[...] += product(a_ref, b_ref)

        @pl.when(k == nk - 1)
        def _():
            finish(acc_ref, ex_refs, out_refs, sum_refs, stream_refs)

    a_spec = pl.BlockSpec((tk, tm), lambda i, j, k: (k, i)) if mode == "tn" else pl.BlockSpec((tm, tk), lambda i, j, k: (i, k))
    if not b_shards:
        b_spec = pl.BlockSpec((tn, tk), lambda i, j, k: (j, k)) if mode == "nt" else pl.BlockSpec((tk, tn), lambda i, j, k: (k, j))
    elif mode == "nn":
        b_spec = pl.BlockSpec((None, tk, tn), lambda i, j, k: (j // nb, k, j % nb))
    else:
        b_spec = pl.BlockSpec((None, tn, tk), lambda i, j, k: (k // kb, j, k % kb))
    ex_specs = []
    for kind, arr in extras:
        if kind == "mn":
            ex_specs.append(pl.BlockSpec((tm, tn), lambda i, j, k: (i, j)))
        elif kind == "mn_stream":
            ex_specs.append(pl.BlockSpec(memory_space=pl.ANY))
        elif kind == "n":
            ex_specs.append(pl.BlockSpec((1, tn), lambda i, j, k: (0, j)))
        else:
            ex_specs.append(pl.BlockSpec((tm, arr.shape[1]), lambda i, j, k: (i, 0)))
    if out_shards:
        out_specs = [pl.BlockSpec((None, tm, tn), lambda i, j, k: (j // nb, i, j % nb)) for _ in out_dtypes]
        out_shape = [jax.ShapeDtypeStruct((N_DEV, M, n_span), d) for d in out_dtypes]
    else:
        out_specs = [pl.BlockSpec((tm, tn), lambda i, j, k: (i, j)) for _ in out_dtypes]
        out_shape = [jax.ShapeDtypeStruct((M, N), d) for d in out_dtypes]
    out_specs += [pl.BlockSpec((1, tn), lambda i, j, k: (0, 0))] * n_sums
    out_shape += [jax.ShapeDtypeStruct((1, N), F32)] * n_sums
    return pl.pallas_call(
        body,
        name=name,
        grid=(M // tm, N // tn, nk),
        in_specs=[a_spec, b_spec] + ex_specs,
        out_specs=out_specs,
        out_shape=out_shape,
        scratch_shapes=([pltpu.VMEM((tm, tn), F32)] if use_acc else [])
                       + [s for e in streamed for s in (pltpu.VMEM((2, rows, tn), extras[e][1].dtype), pltpu.SemaphoreType.DMA((2,)))],
        compiler_params=pltpu.CompilerParams(
            dimension_semantics=("arbitrary",) * 3 if n_sums else ("parallel", "parallel", "arbitrary"), vmem_limit_bytes=vmem),
    )(a, b, *[arr for _, arr in extras])


def _silu(v):
    return v / (1.0 + jnp.exp(-v))


def _relu2(v):
    r = jnp.maximum(v, 0.0)
    return r * r


def _rope_tables(L):
    half = HEAD_DIM // 2
    inv_freq = np.float32(ROPE_BASE) ** (-np.arange(0, half, 2, dtype=np.float32) / np.float32(half))
    t = np.arange(L)
    row, col = t // GRID_W, t % GRID_W
    ang_r = row.astype(np.float32)[:, None] * inv_freq[None, :]
    ang_c = col.astype(np.float32)[:, None] * inv_freq[None, :]
    cos = np.concatenate([np.cos(ang_r), np.cos(ang_r), np.cos(ang_c), np.cos(ang_c)], axis=1)
    sin = np.concatenate([-np.sin(ang_r), np.sin(ang_r), -np.sin(ang_c), np.sin(ang_c)], axis=1)
    reps = LANES // HEAD_DIM
    return jnp.asarray(np.tile(cos, (1, reps)), F32), jnp.asarray(np.tile(sin, (1, reps)), F32)


def _rope(xf, cos, sin):
    quarter = HEAD_DIM // 4
    lane = lax.broadcasted_iota(I32, (xf.shape[0], LANES), 1)
    first = (lane & quarter) == 0
    outs = []
    for j in range(xf.shape[1] // LANES):
        xc = xf[:, j * LANES:(j + 1) * LANES]
        partner = jnp.where(first, pltpu.roll(xc, LANES - quarter, 1), pltpu.roll(xc, quarter, 1))
        outs.append(xc * cos + partner * sin)
    return outs[0] if len(outs) == 1 else jnp.concatenate(outs, axis=1)


def _inv_rms(xf):
    return lax.rsqrt(jnp.mean(xf * xf, axis=-1, keepdims=True) + EPS)


def _modulated_norm(xf, w, sc, sh):
    return ((xf * _inv_rms(xf)) * w) * (1.0 + sc) + sh


def _modulated_norm_bwd(xf, dh, dres, w, sc):
    r = _inv_rms(xf)
    xh = xf * r
    dn = dh * (1.0 + sc)
    dxh = dn * w
    dx = dres + r * (dxh - xh * jnp.mean(dxh * xh, axis=-1, keepdims=True))
    col = lambda v: jnp.sum(v, axis=0, keepdims=True)
    return dx, col(dh), col(dh * (xh * w)), col(dn * xh)


def _norm_fwd(name, x, w, sc, sh):
    L, D = x.shape
    T = _tile(L, 512, 8)

    def body(x_ref, w_ref, sc_ref, sh_ref, h_ref):
        h_ref[...] = _modulated_norm(x_ref[...], w_ref[...], sc_ref[...], sh_ref[...]).astype(BF16)

    row = pl.BlockSpec((1, D), lambda i: (0, 0))
    return pl.pallas_call(
        body, name=name, grid=(L // T,),
        in_specs=[pl.BlockSpec((T, D), lambda i: (i, 0)), row, row, row],
        out_specs=pl.BlockSpec((T, D), lambda i: (i, 0)),
        out_shape=jax.ShapeDtypeStruct((L, D), BF16),
        compiler_params=_cparams("parallel"),
    )(x, w, sc, sh)


def _norm_bwd_sums(name, x, dh, w, sc, w_init):
    L, D = x.shape
    T = _tile(L, 256, 8)

    def body(x_ref, dh_ref, w_ref, sc_ref, wi_ref, ssh_ref, ssc_ref, sw_ref):
        @pl.when(pl.program_id(0) == 0)
        def _():
            ssh_ref[...] = jnp.zeros_like(ssh_ref)
            ssc_ref[...] = jnp.zeros_like(ssc_ref)
            sw_ref[...] = wi_ref[...]

        dh = dh_ref[...]
        _, s_sh, s_sc, s_w = _modulated_norm_bwd(x_ref[...], dh, jnp.zeros_like(dh), w_ref[...], sc_ref[...])
        ssh_ref[...] += s_sh
        ssc_ref[...] += s_sc
        sw_ref[...] += s_w

    tile = pl.BlockSpec((T, D), lambda i: (i, 0))
    row = pl.BlockSpec((1, D), lambda i: (0, 0))
    return pl.pallas_call(
        body, name=name, grid=(L // T,), in_specs=[tile, tile, row, row, row], out_specs=[row, row, row],
        out_shape=[jax.ShapeDtypeStruct((1, D), F32)] * 3, compiler_params=_cparams("arbitrary"),
    )(x, dh, w, sc, w_init)


def _norm_bwd_out_proj_bwd(x, dh, dres, o, w, sc, g, w_out_t, A):
    L, D = x.shape
    N = w_out_t.shape[1]
    T = _tile(L, 256, SUBLANES_16BIT)
    half = T // 2

    def body(x_ref, dh_ref, dres_ref, o_ref, w_ref, sc_ref, g_ref, wt_ref, dx_ref, do_ref, dattn_ref, dpool_ref,
             ssh_ref, ssc_ref, sw_ref, sg_ref):
        @pl.when(pl.program_id(0) == 0)
        def _():
            for s_ref in (ssh_ref, ssc_ref, sw_ref, sg_ref):
                s_ref[...] = jnp.zeros_like(s_ref)

        for rows in (slice(0, half), slice(half, T)):
            dx, s_sh, s_sc, s_w = _modulated_norm_bwd(x_ref[rows, :], dh_ref[rows, :], dres_ref[rows, :], w_ref[...], sc_ref[...])
            ssh_ref[...] += s_sh
            ssc_ref[...] += s_sc
            sw_ref[...] += s_w
            sg_ref[...] += jnp.sum(dx * o_ref[rows, :], axis=0, keepdims=True)
            dx_ref[rows, :] = dx
            do_ref[rows, :] = (g_ref[...] * dx).astype(BF16)
        dap = _dot(do_ref[...], wt_ref[...])
        dattn_ref[...] = dap[:, :A].astype(BF16)
        dpool_ref[...] = dap[:, A:]

    tile = pl.BlockSpec((T, D), lambda i: (i, 0))
    row = pl.BlockSpec((1, D), lambda i: (0, 0))
    return pl.pallas_call(
        body, name="norm_mlp_bwd_out_proj_bwd", grid=(L // T,),
        in_specs=[tile, tile, tile, tile, row, row, row, pl.BlockSpec((D, N), lambda i: (0, 0))],
        out_specs=[tile, tile, pl.BlockSpec((T, A), lambda i: (i, 0)), pl.BlockSpec((T, N - A), lambda i: (i, 0)), row, row, row, row],
        out_shape=[jax.ShapeDtypeStruct((L, D), F32), jax.ShapeDtypeStruct((L, D), BF16), jax.ShapeDtypeStruct((L, A), BF16),
                   jax.ShapeDtypeStruct((L, N - A), F32)] + [jax.ShapeDtypeStruct((1, D), F32)] * 4,
        compiler_params=pltpu.CompilerParams(dimension_semantics=("arbitrary",), vmem_limit_bytes=FUSED_VMEM_LIMIT),
    )(x, dh, dres, o, w, sc, g, w_out_t)


def _out_proj_epilogue(acc, xr, g, w, sc, sh):
    x1 = xr + g * acc
    return acc, x1, _modulated_norm(x1, w, sc, sh)


def _mlp_down_epilogue(acc, x1, tgt, g, wf):
    D = acc.shape[1]
    x2 = x1 + g * acc
    r = _inv_rms(x2)
    xh = x2 * r
    err = xh * wf - tgt
    loss = 0.5 * jnp.sum(jnp.mean(err * err, axis=-1, keepdims=True), axis=0, keepdims=True)
    dy = err * (1.0 / D)
    dxh = dy * wf
    dx = r * (dxh - xh * jnp.mean(dxh * xh, axis=-1, keepdims=True))
    col = lambda v: jnp.sum(v, axis=0, keepdims=True)
    return dx, g * dx, col(dy * xh), col(dx * acc), jnp.broadcast_to(loss, (1, D))


def _heads(ref, first, n):
    return jnp.concatenate([ref[:, (first + g) * HEAD_DIM:(first + g + 1) * HEAD_DIM] for g in range(n)], axis=0)


def _edge_variants(masks):
    return jnp.asarray(np.stack([np.where(masks(first, last), 0.0, NEG_INF).astype(np.float32)
                                 for last in (False, True) for first in (False, True)]))


def _attn_bias(C):
    kj = np.arange(3 * BLOCK + C)[:, None]
    qi = (np.arange(GQA * BLOCK) % BLOCK)[None, :]

    def masks(first, last):
        window = (kj >= qi) & (kj <= qi + 2 * BLOCK) & (kj >= (BLOCK if first else 0)) & (kj < (2 * BLOCK if last else 3 * BLOCK))
        return window | (kj >= 3 * BLOCK)

    return _edge_variants(masks)


def _attn_bias_keys():
    kj = np.arange(BLOCK)[:, None]
    col = np.arange(3 * GQA * BLOCK)[None, :]
    part, qi = col // (GQA * BLOCK), col % BLOCK

    def masks(first, last):
        return ((part == 0) & (kj <= qi) & (not first)) | (part == 1) | ((part == 2) & (kj >= qi) & (not last))

    return _edge_variants(masks)


def _edge_index(n, nb):
    return (n == 0).astype(I32) + 2 * (n == nb - 1).astype(I32)


def _head_rows(ref, hk):
    return jnp.concatenate([ref[hk * GQA + g:hk * GQA + g + 1, :] for g in range(GQA)], axis=1)


def _rows_to_heads(rows_by_kv_head):
    return jnp.concatenate([r[:, g * BLOCK:(g + 1) * BLOCK] for r in rows_by_kv_head for g in range(GQA)], axis=0)


def _queries_to_rows(t):
    return jnp.concatenate([t[:, g * BLOCK:(g + 1) * BLOCK].T for g in range(GQA)], axis=1)


def _attn_specs(L, A, KV, C, vcol):
    nb = L // BLOCK
    kcol = A // KV
    prev = lambda n: jnp.maximum(n - 1, 0)
    nxt = lambda n: jnp.minimum(n + 1, nb - 1)
    q_spec = pl.BlockSpec((BLOCK, A), lambda n: (n, 0))
    k_specs = [pl.BlockSpec((BLOCK, KV), lambda n: (prev(n), kcol)), pl.BlockSpec((BLOCK, KV), lambda n: (n, kcol)),
               pl.BlockSpec((BLOCK, KV), lambda n: (nxt(n), kcol))]
    v_specs = [pl.BlockSpec((BLOCK, KV), lambda n: (prev(n), vcol)), pl.BlockSpec((BLOCK, KV), lambda n: (n, vcol)),
               pl.BlockSpec((BLOCK, KV), lambda n: (nxt(n), vcol))]
    kvc_spec = pl.BlockSpec((C, 2 * KV), lambda n: (0, 0))
    return q_spec, k_specs, v_specs, kvc_spec


def _keys_values(hk, k_refs, v_refs, kvc_ref, KV):
    sl = slice(hk * HEAD_DIM, (hk + 1) * HEAD_DIM)
    keys = jnp.concatenate([r[:, sl] for r in k_refs] + [kvc_ref[:, sl]], axis=0)
    vals = jnp.concatenate([r[:, sl].astype(BF16) for r in v_refs] + [kvc_ref[:, KV + hk * HEAD_DIM:KV + (hk + 1) * HEAD_DIM]], axis=0)
    return keys, vals


def _sink_row(sink_ref, hk):
    return jnp.concatenate([jnp.full((1, BLOCK), sink_ref[0, hk * GQA + g], F32) for g in range(GQA)], axis=1)


def _attn_fwd(qk, uv, kvc, sink, A, KV, P):
    L = qk.shape[0]
    C = kvc.shape[0]
    nkv = KV // HEAD_DIM
    H = nkv * GQA
    scale = HEAD_DIM ** -0.5

    def body(sink_ref, q_ref, kp_ref, kc_ref, kn_ref, vp_ref, vc_ref, vn_ref, kvc_ref, bias_ref, o_ref, lse_ref):
        bias = bias_ref[_edge_index(pl.program_id(0), L // BLOCK)]
        lse_rows = []
        for hk in range(nkv):
            keys, vals = _keys_values(hk, (kp_ref, kc_ref, kn_ref), (vp_ref, vc_ref, vn_ref), kvc_ref, KV)
            qs = _heads(q_ref, hk * GQA, GQA) * scale
            s = _dot_nt(keys, qs) + bias
            sk = _sink_row(sink_ref, hk)
            m = jnp.maximum(jnp.max(s, axis=0, keepdims=True), sk)
            p = jnp.exp(s - m)
            den = jnp.sum(p, axis=0, keepdims=True) + jnp.exp(sk - m)
            o = _dot_tn(vals, p.astype(BF16)) * (1.0 / den)
            lse_rows.append(m + jnp.log(den))
            o_ref[:, hk * GQA * HEAD_DIM:(hk + 1) * GQA * HEAD_DIM] = _queries_to_rows(o).astype(BF16)
        lse_ref[...] = _rows_to_heads(lse_rows)

    q_spec, k_specs, v_specs, kvc_spec = _attn_specs(L, A, KV, C, P // KV)
    bias = _attn_bias(C)
    return pl.pallas_call(
        body, name="attn_fwd", grid=(L // BLOCK,),
        in_specs=[pl.BlockSpec(memory_space=pltpu.SMEM), q_spec] + k_specs + v_specs
                 + [kvc_spec, pl.BlockSpec(bias.shape, lambda n: (0, 0, 0))],
        out_specs=[pl.BlockSpec((BLOCK, A), lambda n: (n, 0)), pl.BlockSpec((H, BLOCK), lambda n: (0, n))],
        out_shape=[jax.ShapeDtypeStruct((L, A + P), BF16), jax.ShapeDtypeStruct((H, L), F32)],
        compiler_params=_cparams("parallel"),
    )(sink, qk, qk, qk, qk, uv, uv, uv, kvc, bias)


def _attn_bwd_dq(qk, uv, kvc, sink, dap, lse_t, cos, sin, A, KV, P):
    L = qk.shape[0]
    C = kvc.shape[0]
    nkv = KV // HEAD_DIM
    H = nkv * GQA
    scale = HEAD_DIM ** -0.5
    W = 3 * BLOCK

    def body(sink_ref, q_ref, kp_ref, kc_ref, kn_ref, vp_ref, vc_ref, vn_ref, kvc_ref, do_ref, lse_ref, cos_ref, sin_ref, bias_ref,
             dq_ref, rd_ref, ds_ref, dkvc_ref):
        n = pl.program_id(0)

        @pl.when(n == 0)
        def _():
            dkvc_ref[...] = jnp.zeros_like(dkvc_ref)

        bias = bias_ref[_edge_index(n, L // BLOCK)]
        rd_rows, dsink_rows, dq_parts = [], [], []
        for hk in range(nkv):
            sl = slice(hk * HEAD_DIM, (hk + 1) * HEAD_DIM)
            keys, vals = _keys_values(hk, (kp_ref, kc_ref, kn_ref), (vp_ref, vc_ref, vn_ref), kvc_ref, KV)
            qs = _heads(q_ref, hk * GQA, GQA) * scale
            dos = _heads(do_ref, hk * GQA, GQA).astype(BF16)
            lse = _head_rows(lse_ref, hk)
            p = jnp.exp(_dot_nt(keys, qs) + bias - lse)
            dp = _dot_nt(vals, dos)
            rd = jnp.sum(p * dp, axis=0, keepdims=True)
            ds = (p * (dp - rd)).astype(BF16)
            dq_parts.append(_queries_to_rows(_dot_tn(keys, ds) * scale))
            dkvc_ref[:, sl] += _dot(ds[W:, :], qs)
            dkvc_ref[:, KV + hk * HEAD_DIM:KV + (hk + 1) * HEAD_DIM] += _dot(p[W:, :].astype(BF16), dos)
            rd_rows.append(rd)
            dsink_rows.append(-(jnp.exp(_sink_row(sink_ref, hk) - lse) * rd))
        rd_ref[...] = _rows_to_heads(rd_rows)
        ds_ref[...] = _rows_to_heads(dsink_rows)
        dq = dq_parts[0] if nkv == 1 else jnp.concatenate(dq_parts, axis=1)
        dq_ref[...] = _rope(dq, cos_ref[...], -sin_ref[...]).astype(BF16)

    q_spec, k_specs, v_specs, kvc_spec = _attn_specs(L, A, KV, C, P // KV)
    blk = lambda w: pl.BlockSpec((BLOCK, w), lambda n: (n, 0))
    per_head = pl.BlockSpec((H, BLOCK), lambda n: (0, n))
    bias = _attn_bias(C)
    return pl.pallas_call(
        body, name="attn_bwd_dq", grid=(L // BLOCK,),
        in_specs=[pl.BlockSpec(memory_space=pltpu.SMEM), q_spec] + k_specs + v_specs
                 + [kvc_spec, blk(A), per_head, blk(LANES), blk(LANES), pl.BlockSpec(bias.shape, lambda n: (0, 0, 0))],
        out_specs=[blk(A), per_head, per_head, pl.BlockSpec((C, 2 * KV), lambda n: (0, 0))],
        out_shape=[jax.ShapeDtypeStruct((L, A), BF16), jax.ShapeDtypeStruct((H, L), F32), jax.ShapeDtypeStruct((H, L), F32),
                   jax.ShapeDtypeStruct((C, 2 * KV), F32)],
        compiler_params=_cparams("arbitrary"),
    )(sink, qk, qk, qk, qk, uv, uv, uv, kvc, dap, lse_t, cos, sin, bias)


def _attn_bwd_dkv(qk, uv, dap, lse_t, rd_t, cos, sin, A, KV, P):
    L = qk.shape[0]
    nb = L // BLOCK
    nkv = KV // HEAD_DIM
    H = nkv * GQA
    scale = HEAD_DIM ** -0.5

    def body(k_ref, v_ref, qp_ref, qc_ref, qn_ref, dop_ref, doc_ref, don_ref, lsep_ref, lsec_ref, lsen_ref,
             rdp_ref, rdc_ref, rdn_ref, cos_ref, sin_ref, bias_ref, dk_ref, dv_ref):
        bias = bias_ref[_edge_index(pl.program_id(0), nb)]
        dk_parts, dv_parts = [], []
        for hk in range(nkv):
            sl = slice(hk * HEAD_DIM, (hk + 1) * HEAD_DIM)
            km = k_ref[:, sl]
            vm = v_ref[:, sl].astype(BF16)
            qs = jnp.concatenate([_heads(q, hk * GQA, GQA) for q in (qp_ref, qc_ref, qn_ref)], axis=0) * scale
            dos = jnp.concatenate([_heads(d, hk * GQA, GQA) for d in (dop_ref, doc_ref, don_ref)], axis=0).astype(BF16)
            rows = [slice(hk * GQA + g, hk * GQA + g + 1) for g in range(GQA)]
            lse = jnp.concatenate([t[r, :] for t in (lsep_ref, lsec_ref, lsen_ref) for r in rows], axis=1)
            rdv = jnp.concatenate([t[r, :] for t in (rdp_ref, rdc_ref, rdn_ref) for r in rows], axis=1)
            p = jnp.exp(_dot_nt(km, qs) + bias - lse)
            ds = (p * (_dot_nt(vm, dos) - rdv)).astype(BF16)
            dk_parts.append(_dot(ds, qs))
            dv_parts.append(_dot(p.astype(BF16), dos))
        dk = dk_parts[0] if nkv == 1 else jnp.concatenate(dk_parts, axis=1)
        dv = dv_parts[0] if nkv == 1 else jnp.concatenate(dv_parts, axis=1)
        dk_ref[...] = _rope(dk, cos_ref[...], -sin_ref[...]).astype(BF16)
        dv_ref[...] = dv.astype(BF16)

    prev = lambda m: jnp.maximum(m - 1, 0)
    nxt = lambda m: jnp.minimum(m + 1, nb - 1)
    three = lambda w: [pl.BlockSpec((BLOCK, w), lambda m: (prev(m), 0)), pl.BlockSpec((BLOCK, w), lambda m: (m, 0)),
                       pl.BlockSpec((BLOCK, w), lambda m: (nxt(m), 0))]
    three_t = [pl.BlockSpec((H, BLOCK), lambda m: (0, prev(m))), pl.BlockSpec((H, BLOCK), lambda m: (0, m)),
               pl.BlockSpec((H, BLOCK), lambda m: (0, nxt(m)))]
    blk = lambda w: pl.BlockSpec((BLOCK, w), lambda m: (m, 0))
    bias = _attn_bias_keys()
    return pl.pallas_call(
        body, name="attn_bwd_dkv", grid=(nb,),
        in_specs=[pl.BlockSpec((BLOCK, KV), lambda m: (m, A // KV)), pl.BlockSpec((BLOCK, KV), lambda m: (m, P // KV))]
                 + three(A) + three(A) + three_t + three_t + [blk(LANES), blk(LANES), pl.BlockSpec(bias.shape, lambda m: (0, 0, 0))],
        out_specs=[blk(KV), blk(KV)],
        out_shape=[jax.ShapeDtypeStruct((L, KV), BF16), jax.ShapeDtypeStruct((L, KV), BF16)],
        compiler_params=_cparams("parallel"),
    )(qk, uv, qk, qk, qk, dap, dap, dap, lse_t, lse_t, lse_t, rd_t, rd_t, rd_t, cos, sin, bias)


def _halo_specs(T, L, W, col):
    per = T // HALO
    return [pl.BlockSpec((HALO, W), lambda i: (jnp.maximum(i * per - 1, 0), col)),
            pl.BlockSpec((T, W), lambda i: (i, col)),
            pl.BlockSpec((HALO, W), lambda i: (jnp.minimum((i + 1) * per, L // HALO - 1), col))]


def _fill_halo_buf(buf, prev_ref, cur_ref, next_ref, i, nt, T):
    buf[0:HALO, :] = jnp.where(i > 0, prev_ref[...], 0.0)
    buf[HALO:HALO + T, :] = cur_ref[...]
    buf[HALO + T:2 * HALO + T, :] = jnp.where(i < nt - 1, next_ref[...], 0.0)


def _zero_margins(lv):
    rows = lv.shape[0]
    lv[0:HALO, :] = jnp.zeros((HALO, lv.shape[1]), F32)
    lv[rows - HALO:rows, :] = jnp.zeros((HALO, lv.shape[1]), F32)


def _window_sums(lv, x, w, first):
    n = x.shape[0]
    lv[HALO:HALO + n, :] = x
    cur = x + lv[pl.ds(HALO + first, n), :]
    span = 1
    while 2 * span < w:
        lv[HALO:HALO + n, :] = cur
        cur = lv[pl.ds(HALO - span, n), :] + lv[pl.ds(HALO + span, n), :]
        span *= 2
    return cur


def _counts(t, w, L):
    lo = jnp.clip(t - w // 2, 0, L)
    hi = jnp.clip(t - w // 2 + w, 0, L)
    return jnp.maximum(hi - lo, 1).astype(F32)


def _pool_fwd(u, pw, scale, mix):
    L, P = u.shape[0], scale.shape[1]
    gd = P // POOL_GROUPS
    T = _tile(L, 256, 8)
    nt = L // T
    assert (mix.shape[1] - P) % P == 0
    mix_col = mix.shape[1] // P - 1

    def body(up_ref, uc_ref, un_ref, pw_ref, sc_ref, mix_ref, out_ref, pooled_ref, buf, lv):
        i = pl.program_id(0)
        _fill_halo_buf(buf, up_ref, uc_ref, un_ref, i, nt, T)
        _zero_margins(lv)
        t = i * T + lax.broadcasted_iota(I32, (T, 1), 0)
        for g, w in enumerate(POOL_WINDOWS):
            cols = slice(g * gd, (g + 1) * gd)
            acc = _window_sums(lv, buf[:, cols], w, -1)[HALO:HALO + T]
            pooled = (acc / _counts(t, w, L) - uc_ref[:, cols]).astype(BF16)
            pooled_ref[:, cols] = pooled
            out_ref[:, cols] = (_dot(pooled, pw_ref[g]) * sc_ref[:, cols]).astype(BF16)

    return pl.pallas_call(
        body, name="pool_fwd", grid=(nt,),
        in_specs=_halo_specs(T, L, P, 0) + [pl.BlockSpec((POOL_GROUPS, gd, gd), lambda i: (0, 0, 0)), pl.BlockSpec((1, P), lambda i: (0, 0)),
                                            pl.BlockSpec(memory_space=pl.ANY)],
        out_specs=[pl.BlockSpec((T, P), lambda i: (i, mix_col)), pl.BlockSpec((T, P), lambda i: (i, 0))],
        out_shape=[jax.ShapeDtypeStruct(mix.shape, BF16), jax.ShapeDtypeStruct((L, P), BF16)],
        scratch_shapes=[pltpu.VMEM((T + 2 * HALO, P), F32), pltpu.VMEM((T + 4 * HALO, gd), F32)],
        input_output_aliases={5: 0},
        compiler_params=_cparams("parallel"),
    )(u, u, u, pw, scale, mix)


def _pool_bwd_mix(d_pool, pooled, pw, scale):
    L, P = pooled.shape
    gd = P // POOL_GROUPS
    T = _tile(L, 256, 8)

    def body(dp_ref, pooled_ref, pw_ref, sc_ref, dpooled_ref, dpw_ref, dsc_ref):
        i = pl.program_id(0)

        @pl.when(i == 0)
        def _():
            dpw_ref[...] = jnp.zeros_like(dpw_ref)
            dsc_ref[...] = jnp.zeros_like(dsc_ref)

        for g in range(POOL_GROUPS):
            cols = slice(g * gd, (g + 1) * gd)
            pb = pooled_ref[:, cols]
            dp = dp_ref[:, cols]
            dsc_ref[:, cols] += jnp.sum(dp * _dot(pb, pw_ref[g]), axis=0, keepdims=True)
            dm = (dp * sc_ref[:, cols]).astype(BF16)
            dpw_ref[g] += _dot_tn(pb, dm)
            dpooled_ref[:, cols] = _dot_nt(dm, pw_ref[g])

    return pl.pallas_call(
        body, name="pool_bwd_mix", grid=(L // T,),
        in_specs=[pl.BlockSpec((T, P), lambda i: (i, 0)), pl.BlockSpec((T, P), lambda i: (i, 0)),
                  pl.BlockSpec((POOL_GROUPS, gd, gd), lambda i: (0, 0, 0)), pl.BlockSpec((1, P), lambda i: (0, 0))],
        out_specs=[pl.BlockSpec((T, P), lambda i: (i, 0)), pl.BlockSpec((POOL_GROUPS, gd, gd), lambda i: (0, 0, 0)),
                   pl.BlockSpec((1, P), lambda i: (0, 0))],
        out_shape=[jax.ShapeDtypeStruct((L, P), F32), jax.ShapeDtypeStruct((POOL_GROUPS, gd, gd), F32), jax.ShapeDtypeStruct((1, P), F32)],
        compiler_params=_cparams("arbitrary"),
    )(d_pool, pooled, pw, scale)


def _pool_bwd_window(dpooled):
    L, P = dpooled.shape
    gd = P // POOL_GROUPS
    T = _tile(L, 256, 8)
    nt = L // T

    def body(dp_ref, dc_ref, dn_ref, du_ref, buf, lv):
        i = pl.program_id(0)
        _fill_halo_buf(buf, dp_ref, dc_ref, dn_ref, i, nt, T)
        _zero_margins(lv)
        t = i * T - HALO + lax.broadcasted_iota(I32, (T + 2 * HALO, 1), 0)
        for g, w in enumerate(POOL_WINDOWS):
            cols = slice(g * gd, (g + 1) * gd)
            acc = _window_sums(lv, buf[:, cols] / _counts(t, w, L), w, 1)[HALO:HALO + T]
            du_ref[:, cols] = (acc - dc_ref[:, cols]).astype(BF16)

    return pl.pallas_call(
        body, name="pool_bwd_window", grid=(nt,),
        in_specs=_halo_specs(T, L, P, 0),
        out_specs=pl.BlockSpec((T, P), lambda i: (i, 0)),
        out_shape=jax.ShapeDtypeStruct((L, P), BF16),
        scratch_shapes=[pltpu.VMEM((T + 2 * HALO, P), F32), pltpu.VMEM((T + 4 * HALO, gd), F32)],
        compiler_params=_cparams("parallel"),
    )(dpooled, dpooled, dpooled)


def _sum_rows(name, a):
    R, N = a.shape

    def body(a_ref, o_ref):
        if R <= 16:
            acc = a_ref[0:1, :]
            for r in range(1, R):
                acc = acc + a_ref[r:r + 1, :]
        else:
            acc = jnp.sum(a_ref[...], axis=0, keepdims=True)
        o_ref[...] = acc

    return pl.pallas_call(body, name=name, out_shape=jax.ShapeDtypeStruct((1, N), F32))(a)


def _sum_lanes(name, a):
    def body(a_ref, o_ref):
        o_ref[...] = jnp.sum(a_ref[...], axis=1, keepdims=True)

    return pl.pallas_call(body, name=name, out_shape=jax.ShapeDtypeStruct((a.shape[0], 1), F32))(a)


def _silu_grad_mul(cv, g):
    def body(c_ref, g_ref, o_ref):
        cvv = c_ref[...]
        s = 1.0 / (1.0 + jnp.exp(-cvv))
        o_ref[...] = g_ref[...] * (s * (1.0 + cvv * (1.0 - s)))

    return pl.pallas_call(body, name="silu_grad_mul", out_shape=jax.ShapeDtypeStruct(cv.shape, F32))(cv, g)


def _adamw(name, w, g, m, v):
    R, C = w.shape
    parts = g.ndim == 3
    n_parts = g.shape[0] if parts else 1
    T = _tile(R, max(8, 262144 // C), 8)

    def body(w_ref, g_ref, m_ref, v_ref, go_ref, d_ref, mo_ref, vo_ref):
        if parts:
            gv = g_ref[0].astype(F32)
            for p in range(1, n_parts):
                gv = gv + g_ref[p].astype(F32)
        else:
            gv = g_ref[...]
        mn = ADAM_B1 * m_ref[...] + (1.0 - ADAM_B1) * gv
        vn = ADAM_B2 * v_ref[...] + (1.0 - ADAM_B2) * (gv * gv)
        m_hat = mn / (1.0 - ADAM_B1 ** ADAM_STEP)
        v_hat = vn / (1.0 - ADAM_B2 ** ADAM_STEP)
        go_ref[...] = gv
        d_ref[...] = -ADAM_LR * (m_hat / (jnp.sqrt(v_hat) + ADAM_EPS) + ADAM_WD * w_ref[...])
        mo_ref[...] = mn
        vo_ref[...] = vn

    tile = pl.BlockSpec((T, C), lambda i: (i, 0))
    g_spec = pl.BlockSpec((n_parts, T, C), lambda i: (0, i, 0)) if parts else tile
    return pl.pallas_call(
        body, name=name, grid=(R // T,),
        in_specs=[tile, g_spec, tile, tile], out_specs=[tile] * 4,
        out_shape=[jax.ShapeDtypeStruct((R, C), F32)] * 4,
        compiler_params=_cparams("parallel"),
    )(w, g, m, v)


def _dev_index(px, py, pc):
    return 4 * px + 2 * py + pc


def _all_gather(name, arrs):
    n = len(arrs)

    def body(*refs):
        ins, outs = refs[:n], refs[n:2 * n]
        send_sems, recv_sems, local_sems = refs[2 * n:]
        x, y, c = lax.axis_index("x"), lax.axis_index("y"), lax.axis_index("c")
        me, sibling = (x, y, c), (x, y, 1 - c)
        chips = [(1 - x, y), (x, 1 - y), (1 - x, 1 - y)]

        def copy(a, k, block, to, src=None):
            slot = outs[a].at[_dev_index(*block)]
            return pltpu.make_async_remote_copy(
                src_ref=slot if src is None else src, dst_ref=slot, send_sem=send_sems.at[a, k], recv_sem=recv_sems.at[a, k],
                device_id=to, device_id_type=MESH)

        mine = [pltpu.make_async_copy(ins[a], outs[a].at[_dev_index(*me)], local_sems.at[a]) for a in range(n)]
        for cp in mine:
            cp.start()
        first = []
        for a in range(n):
            first.append(copy(a, 0, me, sibling, src=ins[a]))
            first += [copy(a, 1 + j, me, (*chip, c), src=ins[a]) for j, chip in enumerate(chips)]
        for cp in first:
            cp.start()
        passed = []
        for j, chip in enumerate(chips):
            for a in range(n):
                copy(a, 1 + j, (*chip, c), me).wait_recv()
                fwd = copy(a, 4 + j, (*chip, c), sibling)
                fwd.start()
                passed.append(fwd)
        for a in range(n):
            copy(a, 0, sibling, me).wait_recv()
            for j, chip in enumerate(chips):
                copy(a, 4 + j, (*chip, 1 - c), me).wait_recv()
        for cp in first + passed:
            cp.wait_send()
        for cp in mine:
            cp.wait()

    return pl.pallas_call(
        body, name=name,
        in_specs=[HBM] * n, out_specs=[HBM] * n,
        out_shape=[jax.ShapeDtypeStruct((N_DEV, *a.shape), a.dtype) for a in arrs],
        scratch_shapes=[pltpu.SemaphoreType.DMA((n, N_DEV - 1)), pltpu.SemaphoreType.DMA((n, N_DEV - 1)), pltpu.SemaphoreType.DMA((n,))],
    )(*arrs)


N_COPIES = {"all_to_all": N_DEV - 1, "gather_chips": 4, "forward": 3}


def _exchange_copies(kind, src_ref, land_ref, send_sems, recv_sems, sending):
    x, y, c = lax.axis_index("x"), lax.axis_index("y"), lax.axis_index("c")
    me = _dev_index(x, y, c)
    others = [(1 - x, y), (x, 1 - y), (1 - x, 1 - y)]
    if kind == "all_to_all":
        flips = [(dx, dy, dc) for dx in (0, 1) for dy in (0, 1) for dc in (0, 1)][1:]
        peers = [(1 - x if dx else x, 1 - y if dy else y, 1 - c if dc else c) for dx, dy, dc in flips]
        plan = [(p, src_ref.at[_dev_index(*p)], me if sending else _dev_index(*p)) for p in peers]
    elif kind == "gather_chips":
        peers = [(x, y, 1 - c)] + [(*o, c) for o in others]
        plan = [(p, src_ref, me if sending else _dev_index(*p)) for p in peers]
    else:
        plan = [((x, y, 1 - c), land_ref.at[_dev_index(*o, c)], _dev_index(*o, c if sending else 1 - c)) for o in others]
    return [pltpu.make_async_remote_copy(src_ref=src, dst_ref=land_ref.at[slot], send_sem=send_sems.at[k], recv_sem=recv_sems.at[k],
                                         device_id=peer, device_id_type=MESH)
            for k, (peer, src, slot) in enumerate(plan)]


def _exchange_start(name, kind, srcs, lands=None):
    if lands is None:
        lands = [lax.empty((N_DEV, *s.shape) if kind == "gather_chips" else s.shape, s.dtype) for s in srcs]
    n = len(lands)
    ops = ([] if srcs is None else list(srcs)) + list(lands)
    m = len(ops)

    def body(*refs):
        src_refs = [None] * n if srcs is None else refs[:n]
        land_refs = refs[m - n:m]
        send_sems, recv_sems, token = refs[m:m + n], refs[m + n:m + 2 * n], refs[-1]
        for a in range(n):
            for cp in _exchange_copies(kind, src_refs[a], land_refs[a], send_sems[a], recv_sems[a], True):
                cp.start()
        token[...] = jnp.zeros_like(token)

    sems = [pltpu.SemaphoreType.DMA((N_COPIES[kind],))] * (2 * n)
    outs = pl.pallas_call(
        body, name=name,
        out_shape=sems + [pltpu.HBM(o.shape, o.dtype) for o in ops] + [jax.ShapeDtypeStruct((8, LANES), F32)],
        in_specs=[HBM] * m,
        out_specs=[SEM] * (2 * n) + [HBM] * m + [pl.BlockSpec(memory_space=pltpu.VMEM)],
        input_output_aliases={i: 2 * n + i for i in range(m)},
        compiler_params=SIDE_EFFECT,
    )(*[pltpu.with_memory_space_constraint(o, pltpu.HBM) for o in ops])
    thru = outs[2 * n:2 * n + m]
    return outs[:n], outs[n:2 * n], (None if srcs is None else thru[:n]), thru[m - n:], outs[-1]


def _exchange_wait(name, kind, send_sems, recv_sems, srcs, lands, after):
    n = len(lands)
    ops = ([] if srcs is None else list(srcs)) + list(lands)
    m = len(ops)

    def body(*refs):
        src_refs = [None] * n if srcs is None else refs[:n]
        land_refs = refs[m - n:m]
        send_refs, recv_refs = refs[m:m + n], refs[m + n:m + 2 * n]
        for a in range(n):
            for cp in _exchange_copies(kind, src_refs[a], land_refs[a], send_refs[a], recv_refs[a], False):
                cp.wait_send()
                cp.wait_recv()

    outs = pl.pallas_call(
        body, name=name,
        out_shape=[pltpu.HBM(o.shape, o.dtype) for o in ops],
        in_specs=[HBM] * m + [SEM] * (2 * n) + [pl.BlockSpec(memory_space=pl.ANY)],
        out_specs=[HBM] * m,
        input_output_aliases={i: i for i in range(m)},
        compiler_params=SIDE_EFFECT,
    )(*ops, *send_sems, *recv_sems, after)
    return (None if srcs is None else outs[:n]), outs[m - n:]


def _with_own(land, own, me):
    return lax.dynamic_update_slice_in_dim(land, own, me, 0)


def _shards_to_cols(g):
    return jnp.transpose(g, (1, 0, 2)).reshape(g.shape[1], N_DEV * g.shape[2])


def _cols_to_shards(a):
    R, Ctot = a.shape
    return jnp.transpose(a.reshape(R, N_DEV, Ctot // N_DEV), (1, 0, 2))


def kernel(x, c, ctx, c_ctx, norm_attn_w, norm_mlp_w, w_ada, b_ada, w_in, attn_sink, pool_w, pool_scale, w_out, w_mlp_up, w_mlp_down, final_norm_w, loss_target, m_c_ctx, m_norm_attn_w, m_norm_mlp_w, m_w_ada, m_b_ada, m_w_in, m_attn_sink, m_pool_w, m_pool_scale, m_w_out, m_w_mlp_up, m_w_mlp_down, m_final_norm_w, v_c_ctx, v_norm_attn_w, v_norm_mlp_w, v_w_ada, v_b_ada, v_w_in, v_attn_sink, v_pool_w, v_pool_scale, v_w_out, v_w_mlp_up, v_w_mlp_down, v_final_norm_w):
    _, L, D = x.shape
    H = attn_sink.shape[1]
    A = H * HEAD_DIM
    KV = A // GQA
    P = pool_scale.shape[1]
    MODW = N_MOD * D
    ws = MODW // N_DEV
    gd = P // POOL_GROUPS
    me = _dev_index(lax.axis_index("x"), lax.axis_index("y"), lax.axis_index("c"))

    x2d, ctx2d, tgt = x[0], ctx[0], loss_target[0]
    cctx_row = c_ctx.reshape(1, D)
    wf_row = final_norm_w.reshape(1, D)
    w_ada_l = w_ada[0]
    pool_w_l = pool_w[0].reshape(POOL_GROUPS * (gd // N_DEV), gd)

    (c_all,) = _all_gather("gather_cond", [c])
    cond = jnp.concatenate([c_all[:, 0, :], cctx_row, jnp.zeros((COND_ROWS - N_DEV - 1, D), F32)], axis=0)
    b_sh = lax.dynamic_slice_in_dim(b_ada, me * ws, ws, axis=1)
    (mods_sh,) = _mm("ada_mod", cond, w_ada_l, "nn", [F32], SMALL_TILES, a_pre=_silu, extras=[("n", b_sh)], epilogue=lambda acc, b: (acc + b,))
    (mods_g,) = _all_gather("gather_mods", [mods_sh])

    w_srcs = [w_in[0].astype(BF16), w_out[0].astype(BF16), pool_w_l.astype(BF16), w_mlp_up[0].astype(BF16), w_mlp_down[0].astype(BF16)]
    w_srcs, mods_g = lax.optimization_barrier((w_srcs, mods_g))
    gather_start = _exchange_start("gather_weights_start", "gather_chips", w_srcs)

    def weights(tag, started, lo, hi, after_chips, after_forward):
        gw_send, gw_recv, gw_src, gw_land, _ = started
        mine, lands = _exchange_wait(f"gather_{tag}_wait", "gather_chips", gw_send[lo:hi], gw_recv[lo:hi], gw_src[lo:hi],
                                     gw_land[lo:hi], after_chips)
        f_send, f_recv, _, f_land, f_token = _exchange_start(f"forward_{tag}_start", "forward", None, lands)
        _, lands = _exchange_wait(f"forward_{tag}_wait", "forward", f_send, f_recv, None, f_land,
                                  f_token if after_forward is None else after_forward)
        return [_with_own(l, s[None], me) for l, s in zip(lands, mine)]

    mods = _shards_to_cols(mods_g)
    mod_b = lax.dynamic_slice_in_dim(mods, me, 1, axis=0)
    sh_a, sc_a, g_a, sh_m, sc_m, g_m = [mod_b[:, i * D:(i + 1) * D] for i in range(N_MOD)]
    csh_a, csc_a = mods[N_DEV:N_DEV + 1, :D], mods[N_DEV:N_DEV + 1, D:2 * D]

    cos, sin = _rope_tables(L)
    h = _norm_fwd("norm_attn", x2d, norm_attn_w, sc_a, sh_a)
    hc = _norm_fwd("norm_attn_ctx", ctx2d, norm_attn_w, csc_a, csh_a)
    (win_g,) = weights("w_in", gather_start, 0, 1, h, None)
    W_in = _shards_to_cols(win_g)
    W_qk, W_kv = W_in[:, :A + KV], W_in[:, A:A + 2 * KV]
    W_uv = jnp.concatenate([W_in[:, A + 2 * KV:], W_in[:, A + KV:A + 2 * KV]], axis=1)
    (qk,) = _mm("in_proj_qk", h, W_qk, "nn", [BF16], (1024, A + KV, D), extras=[("m", cos), ("m", sin)],
                epilogue=lambda acc, cs, sn: (_rope(acc, cs, sn),))
    (uv,) = _mm("in_proj_uv", h, W_uv, "nn", [F32], (1024, P + KV, D))
    (kvc,) = _mm("in_proj_ctx", hc, W_kv, "nn", [BF16], SMALL_TILES)
    attn, lse = _attn_fwd(qk, uv, kvc, attn_sink, A, KV, P)
    wout_g, pw_g = weights("w_out", gather_start, 1, 3, qk, attn)
    W_out = wout_g.reshape(A + P, D)
    PW = jnp.transpose(pw_g.reshape(N_DEV, POOL_GROUPS, gd // N_DEV, gd), (1, 0, 2, 3)).reshape(POOL_GROUPS, gd, gd)
    ap, pooled = _pool_fwd(uv, PW, pool_scale, attn)
    o, x1, hm = _mm("out_proj_norm", ap, W_out, "nn", [F32, F32, BF16], (256, D, D), chunk=128, epilogue=_out_proj_epilogue,
                    extras=[("mn", x2d), ("n", g_a), ("n", norm_mlp_w), ("n", sc_m), ("n", sh_m)])
    W_up, wdown_g = weights("w_mlp", gather_start, 3, 5, attn, x1)
    W_down = wdown_g.reshape(-1, D)
    up, act = _mm("mlp_up", hm, W_up, "nn", [F32, BF16], (1024, 1024, 2048), epilogue=lambda acc: (acc, _relu2(acc)), b_shards=True)
    d_x2, d_mlp, d_wf, d_gm, loss_row = _mm(
        "mlp_down_loss", act, W_down, "nn", [F32, BF16], (512, D, 2048), chunk=128, n_sums=3, vmem=FUSED_VMEM_LIMIT,
        epilogue=_mlp_down_epilogue, extras=[("mn_stream", x1), ("mn_stream", tgt), ("n", g_m), ("n", wf_row)])
    loss_p = loss_row[:, :1]

    (d_up,) = _mm("mlp_down_bwd_act", d_mlp, W_down, "nt", [BF16], (1024, 1024, 2048), extras=[("mn", up)],
                  epilogue=lambda acc, uu: (acc * (2.0 * jnp.maximum(uu, 0.0)),))
    (gW_down,) = _mm("mlp_down_bwd_w", act, d_mlp, "tn", [BF16], (1024, 2048, 2048), vmem=FUSED_VMEM_LIMIT)
    (gW_up_s,) = _mm("mlp_up_bwd_w", hm, d_up, "tn", [BF16], (2048, 1024, 2048), out_shards=True, vmem=FUSED_VMEM_LIMIT)
    g_mlp_srcs = [gW_up_s, gW_down.reshape(N_DEV, -1, D)]
    g_mlp = _exchange_start("grads_mlp_start", "all_to_all", g_mlp_srcs)
    (d_hm,) = _mm("mlp_up_bwd_act", d_up, W_up, "nt", [F32], (1024, D, 1024), b_shards=True)

    d_x1, d_o, d_attn, d_pool, s_sh_m, s_sc_m, s_w_nm, d_ga = _norm_bwd_out_proj_bwd(
        x1, d_hm, d_x2, o, norm_mlp_w, sc_m + g_mlp[4][0, 0], g_a, W_out.T, A)
    (gW_out,) = _mm("out_proj_bwd_w", ap, d_o, "tn", [BF16], (1024, 2048, 2048))
    d_pooled, gPW, d_pscale = _pool_bwd_mix(d_pool, pooled, PW, pool_scale)
    gpw_s = jnp.transpose(gPW.astype(BF16).reshape(POOL_GROUPS, N_DEV, gd // N_DEV, gd), (1, 0, 2, 3)).reshape(N_DEV, -1, gd)
    g_mix_srcs = [gW_out.reshape(N_DEV, (A + P) // N_DEV, D), gpw_s]
    g_mix = _exchange_start("grads_mix_start", "all_to_all", g_mix_srcs)
    lse = lse + g_mix[4][0, 0]
    d_u = _pool_bwd_window(d_pooled)
    d_q, rd, dsink_q, d_kvc = _attn_bwd_dq(qk, uv, kvc, attn_sink, d_attn, lse, cos, sin, A, KV, P)
    d_k, d_v = _attn_bwd_dkv(qk, uv, d_attn, lse, rd, cos, sin, A, KV, P)
    d_sink = _sum_lanes("sink_grad", dsink_q).reshape(1, H)
    d_p = jnp.concatenate([d_q, d_k, d_v, d_u], axis=1)
    d_kvc_b = d_kvc.astype(BF16)
    (gW_kv_ctx,) = _mm("in_proj_ctx_bwd_w", hc, d_kvc_b, "tn", [F32], SMALL_TILES)
    (d_hc,) = _mm("in_proj_ctx_bwd_act", d_kvc_b, W_kv, "nt", [F32], SMALL_TILES)
    gW_in_init = jnp.pad(gW_kv_ctx, ((0, 0), (A, P)))
    (gW_in,) = _mm("in_proj_bwd_w", h, d_p, "tn", [BF16], (1024, 1280, 2048), extras=[("mn", gW_in_init)], epilogue=lambda acc, init: (acc + init,))
    g_in_srcs = [_cols_to_shards(gW_in)]
    g_in = _exchange_start("grads_in_start", "all_to_all", g_in_srcs)
    grad_x, s_sh_a, s_sc_a, s_w_na = _mm(
        "in_proj_bwd_norm", d_p, W_in, "nt", [F32], (256, D, A + 2 * KV + P), chunk=128, n_sums=3,
        epilogue=lambda acc, xr, dres, w, sc: _modulated_norm_bwd(xr, acc, dres, w, sc),
        extras=[("mn", x2d), ("mn", d_x1), ("n", norm_attn_w), ("n", sc_a + g_in[4][0, 0])])
    s_csh, s_csc, s_w_na = _norm_bwd_sums("norm_attn_ctx_bwd", ctx2d, d_hc, norm_attn_w, csc_a, s_w_na)

    pad_l = lambda a: jnp.pad(a, ((0, 0), (0, LANES - a.shape[1])))
    d_mod_b = jnp.concatenate([s_sh_a, s_sc_a, d_ga, s_sh_m, s_sc_m, d_gm], axis=1)
    summed = jnp.concatenate([s_csh, s_csc, s_w_na, s_w_nm, d_wf, d_pscale, pad_l(d_sink), pad_l(loss_p)], axis=1)
    (small_g,) = _all_gather("gather_small", [jnp.concatenate([d_mod_b, summed], axis=1)])
    small_g = small_g[:, 0, :]
    tot = _sum_rows("small_sum", small_g[:, MODW:])
    off = [0]
    for wdt in (D, D, D, D, D, P, LANES, LANES):
        off.append(off[-1] + wdt)
    seg = lambda i: tot[:, off[i]:off[i + 1]]
    g_norm_attn, g_norm_mlp, g_final, g_pscale = seg(2), seg(3), seg(4), seg(5)
    g_sink, loss = seg(6)[:, :H], seg(7)[0, 0]
    d_mod_ctx = jnp.concatenate([seg(0), seg(1), jnp.zeros((1, MODW - 2 * D), F32)], axis=1)
    d_mod = jnp.concatenate([small_g[:, :MODW], d_mod_ctx, jnp.zeros((COND_ROWS - N_DEV - 1, MODW), F32)], axis=0)
    g_b_ada = _sum_rows("b_ada_grad", d_mod[:N_DEV + 1])
    d_mod_sh = lax.dynamic_slice_in_dim(d_mod, me * ws, ws, axis=1)
    (g_w_ada,) = _mm("ada_bwd_w", cond, d_mod_sh, "tn", [F32], SMALL_TILES, a_pre=_silu)
    (d_cond_p,) = _mm("ada_bwd_cond", d_mod_sh, w_ada_l, "nt", [F32], SMALL_TILES)
    (d_cctx_g,) = _all_gather("gather_cctx", [d_cond_p[N_DEV:N_DEV + 1]])
    g_c_ctx = _silu_grad_mul(cctx_row, _sum_rows("cctx_sum", d_cctx_g[:, 0, :]))

    def arrived(name, started):
        srcs, lands = _exchange_wait(name, "all_to_all", started[0], started[1], started[2], started[3], g_c_ctx)
        return [_with_own(l, lax.dynamic_index_in_dim(s, me, 0, keepdims=True), me) for l, s in zip(lands, srcs)]

    r_up, r_down = arrived("grads_mlp_wait", g_mlp)
    r_out, r_pw = arrived("grads_mix_wait", g_mix)
    (r_in,) = arrived("grads_in_wait", g_in)

    results = {
        "c_ctx": _adamw("adam_c_ctx", cctx_row, g_c_ctx, m_c_ctx.reshape(1, D), v_c_ctx.reshape(1, D)),
        "norm_attn_w": _adamw("adam_norm_attn", norm_attn_w, g_norm_attn, m_norm_attn_w, v_norm_attn_w),
        "norm_mlp_w": _adamw("adam_norm_mlp", norm_mlp_w, g_norm_mlp, m_norm_mlp_w, v_norm_mlp_w),
        "w_ada": _adamw("adam_w_ada", w_ada_l, g_w_ada, m_w_ada[0], v_w_ada[0]),
        "b_ada": _adamw("adam_b_ada", b_ada, g_b_ada, m_b_ada, v_b_ada),
        "w_in": _adamw("adam_w_in", w_in[0], r_in, m_w_in[0], v_w_in[0]),
        "attn_sink": _adamw("adam_sink", attn_sink, g_sink, m_attn_sink, v_attn_sink),
        "pool_w": _adamw("adam_pool_w", pool_w_l, r_pw, m_pool_w[0].reshape(pool_w_l.shape), v_pool_w[0].reshape(pool_w_l.shape)),
        "pool_scale": _adamw("adam_pool_scale", pool_scale, g_pscale, m_pool_scale, v_pool_scale),
        "w_out": _adamw("adam_w_out", w_out[0], r_out, m_w_out[0], v_w_out[0]),
        "w_mlp_up": _adamw("adam_w_up", w_mlp_up[0], r_up, m_w_mlp_up[0], v_w_mlp_up[0]),
        "w_mlp_down": _adamw("adam_w_down", w_mlp_down[0], r_down, m_w_mlp_down[0], v_w_mlp_down[0]),
        "final_norm_w": _adamw("adam_final_norm", wf_row, g_final, m_final_norm_w.reshape(1, D), v_final_norm_w.reshape(1, D)),
    }
    shapes = {"c_ctx": c_ctx.shape, "norm_attn_w": norm_attn_w.shape, "norm_mlp_w": norm_mlp_w.shape, "w_ada": w_ada.shape,
              "b_ada": b_ada.shape, "w_in": w_in.shape, "attn_sink": attn_sink.shape, "pool_w": pool_w.shape,
              "pool_scale": pool_scale.shape, "w_out": w_out.shape, "w_mlp_up": w_mlp_up.shape, "w_mlp_down": w_mlp_down.shape,
              "final_norm_w": final_norm_w.shape}
    outs = [loss, grad_x.reshape(x.shape)]
    for part in range(4):
        outs += [results[name][part].reshape(shape) for name, shape in shapes.items()]
    return tuple(outs)
```

```python
import jax
import jax.numpy as jnp
import numpy as np
from jax import lax
from jax.experimental import pallas as pl
from jax.experimental.pallas import tpu as pltpu

F32 = jnp.float32
BF16 = jnp.bfloat16
I32 = jnp.int32

HEAD_DIM = 64
GQA = 4
BLOCK = 128
GRID_W = 64
ROPE_BASE = 10000.0
POOL_WINDOWS = (2, 4, 8, 16)
POOL_GROUPS = len(POOL_WINDOWS)
HALO = 8
N_MOD = 6
EPS = 1e-6
NEG_INF = -1e30
ADAM_LR = 0.001
ADAM_B1 = 0.9
ADAM_B2 = 0.999
ADAM_EPS = 1e-08
ADAM_WD = 0.01
ADAM_STEP = 10
N_DEV = 8
COND_ROWS = 2 * N_DEV
LANES = 128
SUBLANES_16BIT = 16
VMEM_LIMIT = 48 * 1024 * 1024
FUSED_VMEM_LIMIT = 56 * 1024 * 1024
SMALL_TILES = (512, 1024, 512)
MESH = pl.DeviceIdType.MESH
HBM = pl.BlockSpec(memory_space=pltpu.HBM)
SEM = pl.BlockSpec(memory_space=pltpu.SEMAPHORE)
SIDE_EFFECT = pltpu.CompilerParams(has_side_effects=pltpu.SideEffectType.DATAFLOW_SIDE_EFFECTING)


def _cparams(*sem):
    return pltpu.CompilerParams(dimension_semantics=sem, vmem_limit_bytes=VMEM_LIMIT)


def _tile(n, pref, align):
    if n <= pref:
        return n
    t = (pref // align) * align
    while t >= align:
        if n % t == 0:
            return t
        t -= align
    return n


def _dot(a, b):
    return lax.dot_general(a, b, (((1,), (0,)), ((), ())), preferred_element_type=F32)


def _dot_nt(a, b):
    return lax.dot_general(a, b, (((1,), (1,)), ((), ())), preferred_element_type=F32)


def _dot_tn(a, b):
    return lax.dot_general(a, b, (((0,), (0,)), ((), ())), preferred_element_type=F32)


_DOTS = {"nn": _dot, "nt": _dot_nt, "tn": _dot_tn}


def _mm(name, a, b, mode, out_dtypes, tiles, *, epilogue=None, extras=(), a_pre=None, n_sums=0, chunk=None,
        b_shards=False, out_shards=False, vmem=VMEM_LIMIT):
    if mode == "nn":
        M, K = a.shape
        K2, N = (b.shape[1], N_DEV * b.shape[2]) if b_shards else b.shape
    elif mode == "nt":
        M, K = a.shape
        N, K2 = (b.shape[1], N_DEV * b.shape[2]) if b_shards else b.shape
    else:
        (K, M), (K2, N) = a.shape, b.shape
    assert K == K2 and not (b_shards and mode == "tn"), (name, a.shape, b.shape)
    n_span = N // N_DEV if out_shards or (b_shards and mode == "nn") else N
    k_span = K // N_DEV if b_shards and mode == "nt" else K
    tm = _tile(M, tiles[0], LANES if mode == "tn" else SUBLANES_16BIT)
    tn = _tile(n_span, tiles[1], LANES)
    tk = _tile(k_span, tiles[2], SUBLANES_16BIT if mode == "tn" else LANES)
    nk, nb, kb = K // tk, n_span // tn, k_span // tk
    rows = tm if chunk is None else min(chunk, tm)
    n_ex, n_out = len(extras), len(out_dtypes)
    use_acc = nk > 1 or rows < tm
    assert n_sums == 0 or N == tn, name

    def product(a_ref, b_ref):
        at = a_ref[...]
        if a_pre is not None:
            at = a_pre(at)
        return _DOTS[mode](at.astype(BF16), b_ref[...].astype(BF16))

    def apply(acc, ex, out_refs, sl):
        res = (acc,) if epilogue is None else epilogue(acc, *ex)
        for o_ref, o in zip(out_refs, res[:n_out]):
            o_ref[sl, :] = o.astype(o_ref.dtype)
        return tuple(res[n_out:])

    def finish(acc, ex_refs, out_refs, sum_refs):
        if rows == tm:
            acc = acc if not use_acc else acc[...]
            sums = apply(acc, [r[...] for r in ex_refs], out_refs, slice(None))
        else:
            def one(ci, sums):
                sl = pl.ds(pl.multiple_of(ci * rows, rows), rows)
                ex = [r[...] if kind == "n" else r[sl, :] for (kind, _), r in zip(extras, ex_refs)]
                return tuple(s + v for s, v in zip(sums, apply(acc[sl, :], ex, out_refs, sl)))
            sums = lax.fori_loop(0, tm // rows, one, tuple(jnp.zeros((1, tn), F32) for _ in range(n_sums)))
        first = pl.program_id(0) == 0
        for s_ref, sv in zip(sum_refs, sums):
            @pl.when(first)
            def _(s_ref=s_ref, sv=sv):
                s_ref[...] = sv

            @pl.when(jnp.logical_not(first))
            def _(s_ref=s_ref, sv=sv):
                s_ref[...] += sv

    def body(a_ref, b_ref, *rest):
        ex_refs, out_refs = rest[:n_ex], rest[n_ex:n_ex + n_out]
        sum_refs = rest[n_ex + n_out:n_ex + n_out + n_sums]
        if not use_acc:
            finish(product(a_ref, b_ref), ex_refs, out_refs, sum_refs)
            return
        acc_ref = rest[-1]
        k = pl.program_id(2)

        @pl.when(k == 0)
        def _():
            acc_ref[...] = product(a_ref, b_ref)

        @pl.when(k > 0)
        def _():
            acc_ref[...] += product(a_ref, b_ref)

        @pl.when(k == nk - 1)
        def _():
            finish(acc_ref, ex_refs, out_refs, sum_refs)

    a_spec = pl.BlockSpec((tk, tm), lambda i, j, k: (k, i)) if mode == "tn" else pl.BlockSpec((tm, tk), lambda i, j, k: (i, k))
    if not b_shards:
        b_spec = pl.BlockSpec((tn, tk), lambda i, j, k: (j, k)) if mode == "nt" else pl.BlockSpec((tk, tn), lambda i, j, k: (k, j))
    elif mode == "nn":
        b_spec = pl.BlockSpec((None, tk, tn), lambda i, j, k: (j // nb, k, j % nb))
    else:
        b_spec = pl.BlockSpec((None, tn, tk), lambda i, j, k: (k // kb, j, k % kb))
    ex_specs = []
    for kind, arr in extras:
        if kind == "mn":
            ex_specs.append(pl.BlockSpec((tm, tn), lambda i, j, k: (i, j)))
        elif kind == "n":
            ex_specs.append(pl.BlockSpec((1, tn), lambda i, j, k: (0, j)))
        else:
            ex_specs.append(pl.BlockSpec((tm, arr.shape[1]), lambda i, j, k: (i, 0)))
    if out_shards:
        out_specs = [pl.BlockSpec((None, tm, tn), lambda i, j, k: (j // nb, i, j % nb)) for _ in out_dtypes]
        out_shape = [jax.ShapeDtypeStruct((N_DEV, M, n_span), d) for d in out_dtypes]
    else:
        out_specs = [pl.BlockSpec((tm, tn), lambda i, j, k: (i, j)) for _ in out_dtypes]
        out_shape = [jax.ShapeDtypeStruct((M, N), d) for d in out_dtypes]
    out_specs += [pl.BlockSpec((1, tn), lambda i, j, k: (0, 0))] * n_sums
    out_shape += [jax.ShapeDtypeStruct((1, N), F32)] * n_sums
    return pl.pallas_call(
        body,
        name=name,
        grid=(M // tm, N // tn, nk),
        in_specs=[a_spec, b_spec] + ex_specs,
        out_specs=out_specs,
        out_shape=out_shape,
        scratch_shapes=[pltpu.VMEM((tm, tn), F32)] if use_acc else [],
        compiler_params=pltpu.CompilerParams(
            dimension_semantics=("arbitrary",) * 3 if n_sums else ("parallel", "parallel", "arbitrary"), vmem_limit_bytes=vmem),
    )(a, b, *[arr for _, arr in extras])


def _silu(v):
    return v / (1.0 + jnp.exp(-v))


def _relu2(v):
    r = jnp.maximum(v, 0.0)
    return r * r


def _rope_tables(L):
    half = HEAD_DIM // 2
    inv_freq = np.float32(ROPE_BASE) ** (-np.arange(0, half, 2, dtype=np.float32) / np.float32(half))
    t = np.arange(L)
    row, col = t // GRID_W, t % GRID_W
    ang_r = row.astype(np.float32)[:, None] * inv_freq[None, :]
    ang_c = col.astype(np.float32)[:, None] * inv_freq[None, :]
    cos = np.concatenate([np.cos(ang_r), np.cos(ang_r), np.cos(ang_c), np.cos(ang_c)], axis=1)
    sin = np.concatenate([-np.sin(ang_r), np.sin(ang_r), -np.sin(ang_c), np.sin(ang_c)], axis=1)
    reps = LANES // HEAD_DIM
    return jnp.asarray(np.tile(cos, (1, reps)), F32), jnp.asarray(np.tile(sin, (1, reps)), F32)


def _rope(xf, cos, sin):
    quarter = HEAD_DIM // 4
    lane = lax.broadcasted_iota(I32, (xf.shape[0], LANES), 1)
    first = (lane & quarter) == 0
    outs = []
    for j in range(xf.shape[1] // LANES):
        xc = xf[:, j * LANES:(j + 1) * LANES]
        partner = jnp.where(first, pltpu.roll(xc, LANES - quarter, 1), pltpu.roll(xc, quarter, 1))
        outs.append(xc * cos + partner * sin)
    return outs[0] if len(outs) == 1 else jnp.concatenate(outs, axis=1)


def _inv_rms(xf):
    return lax.rsqrt(jnp.mean(xf * xf, axis=-1, keepdims=True) + EPS)


def _modulated_norm(xf, w, sc, sh):
    return ((xf * _inv_rms(xf)) * w) * (1.0 + sc) + sh


def _modulated_norm_bwd(xf, dh, dres, w, sc):
    r = _inv_rms(xf)
    xh = xf * r
    dn = dh * (1.0 + sc)
    dxh = dn * w
    dx = dres + r * (dxh - xh * jnp.mean(dxh * xh, axis=-1, keepdims=True))
    col = lambda v: jnp.sum(v, axis=0, keepdims=True)
    return dx, col(dh), col(dh * (xh * w)), col(dn * xh)


def _in_proj(h, w_qk, w_uv, cos, sin):
    L, D = h.shape
    T = _tile(L, 512, SUBLANES_16BIT)

    def body(h_ref, wqk_ref, wuv_ref, cos_ref, sin_ref, qk_ref, uv_ref):
        hv = h_ref[...]
        qk_ref[...] = _rope(_dot(hv, wqk_ref[...]), cos_ref[...], sin_ref[...]).astype(BF16)
        uv_ref[...] = _dot(hv, wuv_ref[...])

    tile = lambda wd: pl.BlockSpec((T, wd), lambda i: (i, 0))
    whole = lambda a: pl.BlockSpec(a.shape, lambda i: (0, 0))
    return pl.pallas_call(
        body, name="in_proj", grid=(L // T,),
        in_specs=[tile(D), whole(w_qk), whole(w_uv), tile(LANES), tile(LANES)],
        out_specs=[tile(w_qk.shape[1]), tile(w_uv.shape[1])],
        out_shape=[jax.ShapeDtypeStruct((L, w_qk.shape[1]), BF16), jax.ShapeDtypeStruct((L, w_uv.shape[1]), F32)],
        compiler_params=_cparams("parallel"),
    )(h, w_qk, w_uv, cos, sin)


def _norm_fwd(name, x, w, sc, sh):
    L, D = x.shape
    T = _tile(L, 512, 8)

    def body(x_ref, w_ref, sc_ref, sh_ref, h_ref):
        h_ref[...] = _modulated_norm(x_ref[...], w_ref[...], sc_ref[...], sh_ref[...]).astype(BF16)

    row = pl.BlockSpec((1, D), lambda i: (0, 0))
    return pl.pallas_call(
        body, name=name, grid=(L // T,),
        in_specs=[pl.BlockSpec((T, D), lambda i: (i, 0)), row, row, row],
        out_specs=pl.BlockSpec((T, D), lambda i: (i, 0)),
        out_shape=jax.ShapeDtypeStruct((L, D), BF16),
        compiler_params=_cparams("parallel"),
    )(x, w, sc, sh)


def _norm_bwd_sums(name, x, dh, w, sc, w_init):
    L, D = x.shape
    T = _tile(L, 256, 8)

    def body(x_ref, dh_ref, w_ref, sc_ref, wi_ref, ssh_ref, ssc_ref, sw_ref):
        @pl.when(pl.program_id(0) == 0)
        def _():
            ssh_ref[...] = jnp.zeros_like(ssh_ref)
            ssc_ref[...] = jnp.zeros_like(ssc_ref)
            sw_ref[...] = wi_ref[...]

        dh = dh_ref[...]
        _, s_sh, s_sc, s_w = _modulated_norm_bwd(x_ref[...], dh, jnp.zeros_like(dh), w_ref[...], sc_ref[...])
        ssh_ref[...] += s_sh
        ssc_ref[...] += s_sc
        sw_ref[...] += s_w

    tile = pl.BlockSpec((T, D), lambda i: (i, 0))
    row = pl.BlockSpec((1, D), lambda i: (0, 0))
    return pl.pallas_call(
        body, name=name, grid=(L // T,), in_specs=[tile, tile, row, row, row], out_specs=[row, row, row],
        out_shape=[jax.ShapeDtypeStruct((1, D), F32)] * 3, compiler_params=_cparams("arbitrary"),
    )(x, dh, w, sc, w_init)


def _norm_bwd_out_proj_bwd(x, dh, dres, o, w, sc, g, w_out_t, A):
    L, D = x.shape
    N = w_out_t.shape[1]
    T = _tile(L, 256, SUBLANES_16BIT)
    half = T // 2

    def body(x_ref, dh_ref, dres_ref, o_ref, w_ref, sc_ref, g_ref, wt_ref, dx_ref, do_ref, dattn_ref, dpool_ref,
             ssh_ref, ssc_ref, sw_ref, sg_ref):
        @pl.when(pl.program_id(0) == 0)
        def _():
            for s_ref in (ssh_ref, ssc_ref, sw_ref, sg_ref):
                s_ref[...] = jnp.zeros_like(s_ref)

        for rows in (slice(0, half), slice(half, T)):
            dx, s_sh, s_sc, s_w = _modulated_norm_bwd(x_ref[rows, :], dh_ref[rows, :], dres_ref[rows, :], w_ref[...], sc_ref[...])
            ssh_ref[...] += s_sh
            ssc_ref[...] += s_sc
            sw_ref[...] += s_w
            sg_ref[...] += jnp.sum(dx * o_ref[rows, :], axis=0, keepdims=True)
            dx_ref[rows, :] = dx
            do_ref[rows, :] = (g_ref[...] * dx).astype(BF16)
        dap = _dot(do_ref[...], wt_ref[...])
        dattn_ref[...] = dap[:, :A].astype(BF16)
        dpool_ref[...] = dap[:, A:]

    tile = pl.BlockSpec((T, D), lambda i: (i, 0))
    row = pl.BlockSpec((1, D), lambda i: (0, 0))
    return pl.pallas_call(
        body, name="norm_mlp_bwd_out_proj_bwd", grid=(L // T,),
        in_specs=[tile, tile, tile, tile, row, row, row, pl.BlockSpec((D, N), lambda i: (0, 0))],
        out_specs=[tile, tile, pl.BlockSpec((T, A), lambda i: (i, 0)), pl.BlockSpec((T, N - A), lambda i: (i, 0)), row, row, row, row],
        out_shape=[jax.ShapeDtypeStruct((L, D), F32), jax.ShapeDtypeStruct((L, D), BF16), jax.ShapeDtypeStruct((L, A), BF16),
                   jax.ShapeDtypeStruct((L, N - A), F32)] + [jax.ShapeDtypeStruct((1, D), F32)] * 4,
        compiler_params=pltpu.CompilerParams(dimension_semantics=("arbitrary",), vmem_limit_bytes=FUSED_VMEM_LIMIT),
    )(x, dh, dres, o, w, sc, g, w_out_t)


def _out_proj_epilogue(acc, xr, g, w, sc, sh):
    x1 = xr + g * acc
    return acc, x1, _modulated_norm(x1, w, sc, sh)


def _mlp_down_epilogue(acc, x1, tgt, g, wf):
    D = acc.shape[1]
    x2 = x1 + g * acc
    r = _inv_rms(x2)
    xh = x2 * r
    err = xh * wf - tgt
    loss = 0.5 * jnp.sum(jnp.mean(err * err, axis=-1, keepdims=True), axis=0, keepdims=True)
    dy = err * (1.0 / D)
    dxh = dy * wf
    dx = r * (dxh - xh * jnp.mean(dxh * xh, axis=-1, keepdims=True))
    col = lambda v: jnp.sum(v, axis=0, keepdims=True)
    return dx, g * dx, col(dy * xh), col(dx * acc), jnp.broadcast_to(loss, (1, D))


def _heads(ref, first, n):
    return jnp.concatenate([ref[:, (first + g) * HEAD_DIM:(first + g + 1) * HEAD_DIM] for g in range(n)], axis=0)


def _edge_variants(masks):
    return jnp.asarray(np.stack([np.where(masks(first, last), 0.0, NEG_INF).astype(np.float32)
                                 for last in (False, True) for first in (False, True)]))


def _attn_bias(C):
    kj = np.arange(3 * BLOCK + C)[:, None]
    qi = (np.arange(GQA * BLOCK) % BLOCK)[None, :]

    def masks(first, last):
        window = (kj >= qi) & (kj <= qi + 2 * BLOCK) & (kj >= (BLOCK if first else 0)) & (kj < (2 * BLOCK if last else 3 * BLOCK))
        return window | (kj >= 3 * BLOCK)

    return _edge_variants(masks)


def _attn_bias_keys():
    kj = np.arange(BLOCK)[:, None]
    col = np.arange(3 * GQA * BLOCK)[None, :]
    part, qi = col // (GQA * BLOCK), col % BLOCK

    def masks(first, last):
        return ((part == 0) & (kj <= qi) & (not first)) | (part == 1) | ((part == 2) & (kj >= qi) & (not last))

    return _edge_variants(masks)


def _edge_index(n, nb):
    return (n == 0).astype(I32) + 2 * (n == nb - 1).astype(I32)


def _head_rows(ref, hk):
    return jnp.concatenate([ref[hk * GQA + g:hk * GQA + g + 1, :] for g in range(GQA)], axis=1)


def _rows_to_heads(rows_by_kv_head):
    return jnp.concatenate([r[:, g * BLOCK:(g + 1) * BLOCK] for r in rows_by_kv_head for g in range(GQA)], axis=0)


def _queries_to_rows(t):
    return jnp.concatenate([t[:, g * BLOCK:(g + 1) * BLOCK].T for g in range(GQA)], axis=1)


def _attn_specs(L, A, KV, C, vcol):
    nb = L // BLOCK
    kcol = A // KV
    prev = lambda n: jnp.maximum(n - 1, 0)
    nxt = lambda n: jnp.minimum(n + 1, nb - 1)
    q_spec = pl.BlockSpec((BLOCK, A), lambda n: (n, 0))
    k_specs = [pl.BlockSpec((BLOCK, KV), lambda n: (prev(n), kcol)), pl.BlockSpec((BLOCK, KV), lambda n: (n, kcol)),
               pl.BlockSpec((BLOCK, KV), lambda n: (nxt(n), kcol))]
    v_specs = [pl.BlockSpec((BLOCK, KV), lambda n: (prev(n), vcol)), pl.BlockSpec((BLOCK, KV), lambda n: (n, vcol)),
               pl.BlockSpec((BLOCK, KV), lambda n: (nxt(n), vcol))]
    kvc_spec = pl.BlockSpec((C, 2 * KV), lambda n: (0, 0))
    return q_spec, k_specs, v_specs, kvc_spec


def _keys_values(hk, k_refs, v_refs, kvc_ref, KV):
    sl = slice(hk * HEAD_DIM, (hk + 1) * HEAD_DIM)
    keys = jnp.concatenate([r[:, sl] for r in k_refs] + [kvc_ref[:, sl]], axis=0)
    vals = jnp.concatenate([r[:, sl].astype(BF16) for r in v_refs] + [kvc_ref[:, KV + hk * HEAD_DIM:KV + (hk + 1) * HEAD_DIM]], axis=0)
    return keys, vals


def _sink_row(sink_ref, hk):
    return jnp.concatenate([jnp.full((1, BLOCK), sink_ref[0, hk * GQA + g], F32) for g in range(GQA)], axis=1)


def _attn_fwd(qk, uv, kvc, sink, A, KV, P):
    L = qk.shape[0]
    C = kvc.shape[0]
    nkv = KV // HEAD_DIM
    H = nkv * GQA
    scale = HEAD_DIM ** -0.5

    def body(sink_ref, q_ref, kp_ref, kc_ref, kn_ref, vp_ref, vc_ref, vn_ref, kvc_ref, bias_ref, o_ref, lse_ref):
        bias = bias_ref[_edge_index(pl.program_id(0), L // BLOCK)]
        lse_rows = []
        for hk in range(nkv):
            keys, vals = _keys_values(hk, (kp_ref, kc_ref, kn_ref), (vp_ref, vc_ref, vn_ref), kvc_ref, KV)
            qs = _heads(q_ref, hk * GQA, GQA) * scale
            s = _dot_nt(keys, qs) + bias
            sk = _sink_row(sink_ref, hk)
            m = jnp.maximum(jnp.max(s, axis=0, keepdims=True), sk)
            p = jnp.exp(s - m)
            den = jnp.sum(p, axis=0, keepdims=True) + jnp.exp(sk - m)
            o = _dot_tn(vals, p.astype(BF16)) * (1.0 / den)
            lse_rows.append(m + jnp.log(den))
            o_ref[:, hk * GQA * HEAD_DIM:(hk + 1) * GQA * HEAD_DIM] = _queries_to_rows(o).astype(BF16)
        lse_ref[...] = _rows_to_heads(lse_rows)

    q_spec, k_specs, v_specs, kvc_spec = _attn_specs(L, A, KV, C, P // KV)
    bias = _attn_bias(C)
    return pl.pallas_call(
        body, name="attn_fwd", grid=(L // BLOCK,),
        in_specs=[pl.BlockSpec(memory_space=pltpu.SMEM), q_spec] + k_specs + v_specs
                 + [kvc_spec, pl.BlockSpec(bias.shape, lambda n: (0, 0, 0))],
        out_specs=[pl.BlockSpec((BLOCK, A), lambda n: (n, 0)), pl.BlockSpec((H, BLOCK), lambda n: (0, n))],
        out_shape=[jax.ShapeDtypeStruct((L, A + P), BF16), jax.ShapeDtypeStruct((H, L), F32)],
        compiler_params=_cparams("parallel"),
    )(sink, qk, qk, qk, qk, uv, uv, uv, kvc, bias)


def _attn_bwd_dq(qk, uv, kvc, sink, dap, lse_t, cos, sin, A, KV, P):
    L = qk.shape[0]
    C = kvc.shape[0]
    nkv = KV // HEAD_DIM
    H = nkv * GQA
    scale = HEAD_DIM ** -0.5
    W = 3 * BLOCK

    def body(sink_ref, q_ref, kp_ref, kc_ref, kn_ref, vp_ref, vc_ref, vn_ref, kvc_ref, do_ref, lse_ref, cos_ref, sin_ref, bias_ref,
             dq_ref, rd_ref, ds_ref, dkvc_ref):
        n = pl.program_id(0)

        @pl.when(n == 0)
        def _():
            dkvc_ref[...] = jnp.zeros_like(dkvc_ref)

        bias = bias_ref[_edge_index(n, L // BLOCK)]
        rd_rows, dsink_rows, dq_parts = [], [], []
        for hk in range(nkv):
            sl = slice(hk * HEAD_DIM, (hk + 1) * HEAD_DIM)
            keys, vals = _keys_values(hk, (kp_ref, kc_ref, kn_ref), (vp_ref, vc_ref, vn_ref), kvc_ref, KV)
            qs = _heads(q_ref, hk * GQA, GQA) * scale
            dos = _heads(do_ref, hk * GQA, GQA).astype(BF16)
            lse = _head_rows(lse_ref, hk)
            p = jnp.exp(_dot_nt(keys, qs) + bias - lse)
            dp = _dot_nt(vals, dos)
            rd = jnp.sum(p * dp, axis=0, keepdims=True)
            ds = (p * (dp - rd)).astype(BF16)
            dq_parts.append(_queries_to_rows(_dot_tn(keys, ds) * scale))
            dkvc_ref[:, sl] += _dot(ds[W:, :], qs)
            dkvc_ref[:, KV + hk * HEAD_DIM:KV + (hk + 1) * HEAD_DIM] += _dot(p[W:, :].astype(BF16), dos)
            rd_rows.append(rd)
            dsink_rows.append(-(jnp.exp(_sink_row(sink_ref, hk) - lse) * rd))
        rd_ref[...] = _rows_to_heads(rd_rows)
        ds_ref[...] = _rows_to_heads(dsink_rows)
        dq = dq_parts[0] if nkv == 1 else jnp.concatenate(dq_parts, axis=1)
        dq_ref[...] = _rope(dq, cos_ref[...], -sin_ref[...]).astype(BF16)

    q_spec, k_specs, v_specs, kvc_spec = _attn_specs(L, A, KV, C, P // KV)
    blk = lambda w: pl.BlockSpec((BLOCK, w), lambda n: (n, 0))
    per_head = pl.BlockSpec((H, BLOCK), lambda n: (0, n))
    bias = _attn_bias(C)
    return pl.pallas_call(
        body, name="attn_bwd_dq", grid=(L // BLOCK,),
        in_specs=[pl.BlockSpec(memory_space=pltpu.SMEM), q_spec] + k_specs + v_specs
                 + [kvc_spec, blk(A), per_head, blk(LANES), blk(LANES), pl.BlockSpec(bias.shape, lambda n: (0, 0, 0))],
        out_specs=[blk(A), per_head, per_head, pl.BlockSpec((C, 2 * KV), lambda n: (0, 0))],
        out_shape=[jax.ShapeDtypeStruct((L, A), BF16), jax.ShapeDtypeStruct((H, L), F32), jax.ShapeDtypeStruct((H, L), F32),
                   jax.ShapeDtypeStruct((C, 2 * KV), F32)],
        compiler_params=_cparams("arbitrary"),
    )(sink, qk, qk, qk, qk, uv, uv, uv, kvc, dap, lse_t, cos, sin, bias)


def _attn_bwd_dkv(qk, uv, dap, lse_t, rd_t, cos, sin, A, KV, P):
    L = qk.shape[0]
    nb = L // BLOCK
    nkv = KV // HEAD_DIM
    H = nkv * GQA
    scale = HEAD_DIM ** -0.5

    def body(k_ref, v_ref, qp_ref, qc_ref, qn_ref, dop_ref, doc_ref, don_ref, lsep_ref, lsec_ref, lsen_ref,
             rdp_ref, rdc_ref, rdn_ref, cos_ref, sin_ref, bias_ref, dk_ref, dv_ref):
        bias = bias_ref[_edge_index(pl.program_id(0), nb)]
        dk_parts, dv_parts = [], []
        for hk in range(nkv):
            sl = slice(hk * HEAD_DIM, (hk + 1) * HEAD_DIM)
            km = k_ref[:, sl]
            vm = v_ref[:, sl].astype(BF16)
            qs = jnp.concatenate([_heads(q, hk * GQA, GQA) for q in (qp_ref, qc_ref, qn_ref)], axis=0) * scale
            dos = jnp.concatenate([_heads(d, hk * GQA, GQA) for d in (dop_ref, doc_ref, don_ref)], axis=0).astype(BF16)
            rows = [slice(hk * GQA + g, hk * GQA + g + 1) for g in range(GQA)]
            lse = jnp.concatenate([t[r, :] for t in (lsep_ref, lsec_ref, lsen_ref) for r in rows], axis=1)
            rdv = jnp.concatenate([t[r, :] for t in (rdp_ref, rdc_ref, rdn_ref) for r in rows], axis=1)
            p = jnp.exp(_dot_nt(km, qs) + bias - lse)
            ds = (p * (_dot_nt(vm, dos) - rdv)).astype(BF16)
            dk_parts.append(_dot(ds, qs))
            dv_parts.append(_dot(p.astype(BF16), dos))
        dk = dk_parts[0] if nkv == 1 else jnp.concatenate(dk_parts, axis=1)
        dv = dv_parts[0] if nkv == 1 else jnp.concatenate(dv_parts, axis=1)
        dk_ref[...] = _rope(dk, cos_ref[...], -sin_ref[...]).astype(BF16)
        dv_ref[...] = dv.astype(BF16)

    prev = lambda m: jnp.maximum(m - 1, 0)
    nxt = lambda m: jnp.minimum(m + 1, nb - 1)
    three = lambda w: [pl.BlockSpec((BLOCK, w), lambda m: (prev(m), 0)), pl.BlockSpec((BLOCK, w), lambda m: (m, 0)),
                       pl.BlockSpec((BLOCK, w), lambda m: (nxt(m), 0))]
    three_t = [pl.BlockSpec((H, BLOCK), lambda m: (0, prev(m))), pl.BlockSpec((H, BLOCK), lambda m: (0, m)),
               pl.BlockSpec((H, BLOCK), lambda m: (0, nxt(m)))]
    blk = lambda w: pl.BlockSpec((BLOCK, w), lambda m: (m, 0))
    bias = _attn_bias_keys()
    return pl.pallas_call(
        body, name="attn_bwd_dkv", grid=(nb,),
        in_specs=[pl.BlockSpec((BLOCK, KV), lambda m: (m, A // KV)), pl.BlockSpec((BLOCK, KV), lambda m: (m, P // KV))]
                 + three(A) + three(A) + three_t + three_t + [blk(LANES), blk(LANES), pl.BlockSpec(bias.shape, lambda m: (0, 0, 0))],
        out_specs=[blk(KV), blk(KV)],
        out_shape=[jax.ShapeDtypeStruct((L, KV), BF16), jax.ShapeDtypeStruct((L, KV), BF16)],
        compiler_params=_cparams("parallel"),
    )(qk, uv, qk, qk, qk, dap, dap, dap, lse_t, lse_t, lse_t, rd_t, rd_t, rd_t, cos, sin, bias)


def _halo_specs(T, L, W, col):
    per = T // HALO
    return [pl.BlockSpec((HALO, W), lambda i: (jnp.maximum(i * per - 1, 0), col)),
            pl.BlockSpec((T, W), lambda i: (i, col)),
            pl.BlockSpec((HALO, W), lambda i: (jnp.minimum((i + 1) * per, L // HALO - 1), col))]


def _fill_halo_buf(buf, prev_ref, cur_ref, next_ref, i, nt, T):
    buf[0:HALO, :] = jnp.where(i > 0, prev_ref[...], 0.0)
    buf[HALO:HALO + T, :] = cur_ref[...]
    buf[HALO + T:2 * HALO + T, :] = jnp.where(i < nt - 1, next_ref[...], 0.0)


def _zero_margins(lv):
    rows = lv.shape[0]
    lv[0:HALO, :] = jnp.zeros((HALO, lv.shape[1]), F32)
    lv[rows - HALO:rows, :] = jnp.zeros((HALO, lv.shape[1]), F32)


def _window_sums(lv, x, w, first):
    n = x.shape[0]
    lv[HALO:HALO + n, :] = x
    cur = x + lv[pl.ds(HALO + first, n), :]
    span = 1
    while 2 * span < w:
        lv[HALO:HALO + n, :] = cur
        cur = lv[pl.ds(HALO - span, n), :] + lv[pl.ds(HALO + span, n), :]
        span *= 2
    return cur


def _counts(t, w, L):
    lo = jnp.clip(t - w // 2, 0, L)
    hi = jnp.clip(t - w // 2 + w, 0, L)
    return jnp.maximum(hi - lo, 1).astype(F32)


def _pool_fwd(u, pw, scale, mix):
    L, P = u.shape[0], scale.shape[1]
    gd = P // POOL_GROUPS
    T = _tile(L, 256, 8)
    nt = L // T
    assert (mix.shape[1] - P) % P == 0
    mix_col = mix.shape[1] // P - 1

    def body(up_ref, uc_ref, un_ref, pw_ref, sc_ref, mix_ref, out_ref, pooled_ref, buf, lv):
        i = pl.program_id(0)
        _fill_halo_buf(buf, up_ref, uc_ref, un_ref, i, nt, T)
        _zero_margins(lv)
        t = i * T + lax.broadcasted_iota(I32, (T, 1), 0)
        for g, w in enumerate(POOL_WINDOWS):
            cols = slice(g * gd, (g + 1) * gd)
            acc = _window_sums(lv, buf[:, cols], w, -1)[HALO:HALO + T]
            pooled = (acc / _counts(t, w, L) - uc_ref[:, cols]).astype(BF16)
            pooled_ref[:, cols] = pooled
            out_ref[:, cols] = (_dot(pooled, pw_ref[g]) * sc_ref[:, cols]).astype(BF16)

    return pl.pallas_call(
        body, name="pool_fwd", grid=(nt,),
        in_specs=_halo_specs(T, L, P, 0) + [pl.BlockSpec((POOL_GROUPS, gd, gd), lambda i: (0, 0, 0)), pl.BlockSpec((1, P), lambda i: (0, 0)),
                                            pl.BlockSpec(memory_space=pl.ANY)],
        out_specs=[pl.BlockSpec((T, P), lambda i: (i, mix_col)), pl.BlockSpec((T, P), lambda i: (i, 0))],
        out_shape=[jax.ShapeDtypeStruct(mix.shape, BF16), jax.ShapeDtypeStruct((L, P), BF16)],
        scratch_shapes=[pltpu.VMEM((T + 2 * HALO, P), F32), pltpu.VMEM((T + 4 * HALO, gd), F32)],
        input_output_aliases={5: 0},
        compiler_params=_cparams("parallel"),
    )(u, u, u, pw, scale, mix)


def _pool_bwd_mix(d_pool, pooled, pw, scale):
    L, P = pooled.shape
    gd = P // POOL_GROUPS
    T = _tile(L, 256, 8)

    def body(dp_ref, pooled_ref, pw_ref, sc_ref, dpooled_ref, dpw_ref, dsc_ref):
        i = pl.program_id(0)

        @pl.when(i == 0)
        def _():
            dpw_ref[...] = jnp.zeros_like(dpw_ref)
            dsc_ref[...] = jnp.zeros_like(dsc_ref)

        for g in range(POOL_GROUPS):
            cols = slice(g * gd, (g + 1) * gd)
            pb = pooled_ref[:, cols]
            dp = dp_ref[:, cols]
            dsc_ref[:, cols] += jnp.sum(dp * _dot(pb, pw_ref[g]), axis=0, keepdims=True)
            dm = (dp * sc_ref[:, cols]).astype(BF16)
            dpw_ref[g] += _dot_tn(pb, dm)
            dpooled_ref[:, cols] = _dot_nt(dm, pw_ref[g])

    return pl.pallas_call(
        body, name="pool_bwd_mix", grid=(L // T,),
        in_specs=[pl.BlockSpec((T, P), lambda i: (i, 0)), pl.BlockSpec((T, P), lambda i: (i, 0)),
                  pl.BlockSpec((POOL_GROUPS, gd, gd), lambda i: (0, 0, 0)), pl.BlockSpec((1, P), lambda i: (0, 0))],
        out_specs=[pl.BlockSpec((T, P), lambda i: (i, 0)), pl.BlockSpec((POOL_GROUPS, gd, gd), lambda i: (0, 0, 0)),
                   pl.BlockSpec((1, P), lambda i: (0, 0))],
        out_shape=[jax.ShapeDtypeStruct((L, P), F32), jax.ShapeDtypeStruct((POOL_GROUPS, gd, gd), F32), jax.ShapeDtypeStruct((1, P), F32)],
        compiler_params=_cparams("arbitrary"),
    )(d_pool, pooled, pw, scale)


def _pool_bwd_window(dpooled):
    L, P = dpooled.shape
    gd = P // POOL_GROUPS
    T = _tile(L, 256, 8)
    nt = L // T

    def body(dp_ref, dc_ref, dn_ref, du_ref, buf, lv):
        i = pl.program_id(0)
        _fill_halo_buf(buf, dp_ref, dc_ref, dn_ref, i, nt, T)
        _zero_margins(lv)
        t = i * T - HALO + lax.broadcasted_iota(I32, (T + 2 * HALO, 1), 0)
        for g, w in enumerate(POOL_WINDOWS):
            cols = slice(g * gd, (g + 1) * gd)
            acc = _window_sums(lv, buf[:, cols] / _counts(t, w, L), w, 1)[HALO:HALO + T]
            du_ref[:, cols] = (acc - dc_ref[:, cols]).astype(BF16)

    return pl.pallas_call(
        body, name="pool_bwd_window", grid=(nt,),
        in_specs=_halo_specs(T, L, P, 0),
        out_specs=pl.BlockSpec((T, P), lambda i: (i, 0)),
        out_shape=jax.ShapeDtypeStruct((L, P), BF16),
        scratch_shapes=[pltpu.VMEM((T + 2 * HALO, P), F32), pltpu.VMEM((T + 4 * HALO, gd), F32)],
        compiler_params=_cparams("parallel"),
    )(dpooled, dpooled, dpooled)


def _sum_rows(name, a):
    R, N = a.shape

    def body(a_ref, o_ref):
        if R <= 16:
            acc = a_ref[0:1, :]
            for r in range(1, R):
                acc = acc + a_ref[r:r + 1, :]
        else:
            acc = jnp.sum(a_ref[...], axis=0, keepdims=True)
        o_ref[...] = acc

    return pl.pallas_call(body, name=name, out_shape=jax.ShapeDtypeStruct((1, N), F32))(a)


def _sum_lanes(name, a):
    def body(a_ref, o_ref):
        o_ref[...] = jnp.sum(a_ref[...], axis=1, keepdims=True)

    return pl.pallas_call(body, name=name, out_shape=jax.ShapeDtypeStruct((a.shape[0], 1), F32))(a)


def _silu_grad_mul(cv, g):
    def body(c_ref, g_ref, o_ref):
        cvv = c_ref[...]
        s = 1.0 / (1.0 + jnp.exp(-cvv))
        o_ref[...] = g_ref[...] * (s * (1.0 + cvv * (1.0 - s)))

    return pl.pallas_call(body, name="silu_grad_mul", out_shape=jax.ShapeDtypeStruct(cv.shape, F32))(cv, g)


def _adamw(name, w, g, m, v):
    R, C = w.shape
    parts = g.ndim == 3
    n_parts = g.shape[0] if parts else 1
    T = _tile(R, max(8, 262144 // C), 8)

    def body(w_ref, g_ref, m_ref, v_ref, go_ref, d_ref, mo_ref, vo_ref):
        if parts:
            gv = g_ref[0].astype(F32)
            for p in range(1, n_parts):
                gv = gv + g_ref[p].astype(F32)
        else:
            gv = g_ref[...]
        mn = ADAM_B1 * m_ref[...] + (1.0 - ADAM_B1) * gv
        vn = ADAM_B2 * v_ref[...] + (1.0 - ADAM_B2) * (gv * gv)
        m_hat = mn / (1.0 - ADAM_B1 ** ADAM_STEP)
        v_hat = vn / (1.0 - ADAM_B2 ** ADAM_STEP)
        go_ref[...] = gv
        d_ref[...] = -ADAM_LR * (m_hat / (jnp.sqrt(v_hat) + ADAM_EPS) + ADAM_WD * w_ref[...])
        mo_ref[...] = mn
        vo_ref[...] = vn

    tile = pl.BlockSpec((T, C), lambda i: (i, 0))
    g_spec = pl.BlockSpec((n_parts, T, C), lambda i: (0, i, 0)) if parts else tile
    return pl.pallas_call(
        body, name=name, grid=(R // T,),
        in_specs=[tile, g_spec, tile, tile], out_specs=[tile] * 4,
        out_shape=[jax.ShapeDtypeStruct((R, C), F32)] * 4,
        compiler_params=_cparams("parallel"),
    )(w, g, m, v)


def _dev_index(px, py, pc):
    return 4 * px + 2 * py + pc


def _all_gather(name, arrs):
    n = len(arrs)

    def body(*refs):
        ins, outs = refs[:n], refs[n:2 * n]
        send_sems, recv_sems, local_sems = refs[2 * n:]
        x, y, c = lax.axis_index("x"), lax.axis_index("y"), lax.axis_index("c")
        me, sibling = (x, y, c), (x, y, 1 - c)
        chips = [(1 - x, y), (x, 1 - y), (1 - x, 1 - y)]

        def copy(a, k, block, to, src=None):
            slot = outs[a].at[_dev_index(*block)]
            return pltpu.make_async_remote_copy(
                src_ref=slot if src is None else src, dst_ref=slot, send_sem=send_sems.at[a, k], recv_sem=recv_sems.at[a, k],
                device_id=to, device_id_type=MESH)

        mine = [pltpu.make_async_copy(ins[a], outs[a].at[_dev_index(*me)], local_sems.at[a]) for a in range(n)]
        for cp in mine:
            cp.start()
        first = []
        for a in range(n):
            first.append(copy(a, 0, me, sibling, src=ins[a]))
            first += [copy(a, 1 + j, me, (*chip, c), src=ins[a]) for j, chip in enumerate(chips)]
        for cp in first:
            cp.start()
        passed = []
        for j, chip in enumerate(chips):
            for a in range(n):
                copy(a, 1 + j, (*chip, c), me).wait_recv()
                fwd = copy(a, 4 + j, (*chip, c), sibling)
                fwd.start()
                passed.append(fwd)
        for a in range(n):
            copy(a, 0, sibling, me).wait_recv()
            for j, chip in enumerate(chips):
                copy(a, 4 + j, (*chip, 1 - c), me).wait_recv()
        for cp in first + passed:
            cp.wait_send()
        for cp in mine:
            cp.wait()

    return pl.pallas_call(
        body, name=name,
        in_specs=[HBM] * n, out_specs=[HBM] * n,
        out_shape=[jax.ShapeDtypeStruct((N_DEV, *a.shape), a.dtype) for a in arrs],
        scratch_shapes=[pltpu.SemaphoreType.DMA((n, N_DEV - 1)), pltpu.SemaphoreType.DMA((n, N_DEV - 1)), pltpu.SemaphoreType.DMA((n,))],
    )(*arrs)


N_COPIES = {"all_to_all": N_DEV - 1, "gather_chips": 4, "forward": 3}


def _exchange_copies(kind, src_ref, land_ref, send_sems, recv_sems, sending):
    x, y, c = lax.axis_index("x"), lax.axis_index("y"), lax.axis_index("c")
    me = _dev_index(x, y, c)
    others = [(1 - x, y), (x, 1 - y), (1 - x, 1 - y)]
    if kind == "all_to_all":
        flips = [(dx, dy, dc) for dx in (0, 1) for dy in (0, 1) for dc in (0, 1)][1:]
        peers = [(1 - x if dx else x, 1 - y if dy else y, 1 - c if dc else c) for dx, dy, dc in flips]
        plan = [(p, src_ref.at[_dev_index(*p)], me if sending else _dev_index(*p)) for p in peers]
    elif kind == "gather_chips":
        peers = [(x, y, 1 - c)] + [(*o, c) for o in others]
        plan = [(p, src_ref, me if sending else _dev_index(*p)) for p in peers]
    else:
        plan = [((x, y, 1 - c), land_ref.at[_dev_index(*o, c)], _dev_index(*o, c if sending else 1 - c)) for o in others]
    return [pltpu.make_async_remote_copy(src_ref=src, dst_ref=land_ref.at[slot], send_sem=send_sems.at[k], recv_sem=recv_sems.at[k],
                                         device_id=peer, device_id_type=MESH)
            for k, (peer, src, slot) in enumerate(plan)]


def _exchange_start(name, kind, srcs, lands=None):
    if lands is None:
        lands = [lax.empty((N_DEV, *s.shape) if kind == "gather_chips" else s.shape, s.dtype) for s in srcs]
    n = len(lands)
    ops = ([] if srcs is None else list(srcs)) + list(lands)
    m = len(ops)

    def body(*refs):
        src_refs = [None] * n if srcs is None else refs[:n]
        land_refs = refs[m - n:m]
        send_sems, recv_sems, token = refs[m:m + n], refs[m + n:m + 2 * n], refs[-1]
        for a in range(n):
            for cp in _exchange_copies(kind, src_refs[a], land_refs[a], send_sems[a], recv_sems[a], True):
                cp.start()
        token[...] = jnp.zeros_like(token)

    sems = [pltpu.SemaphoreType.DMA((N_COPIES[kind],))] * (2 * n)
    outs = pl.pallas_call(
        body, name=name,
        out_shape=sems + [pltpu.HBM(o.shape, o.dtype) for o in ops] + [jax.ShapeDtypeStruct((8, LANES), F32)],
        in_specs=[HBM] * m,
        out_specs=[SEM] * (2 * n) + [HBM] * m + [pl.BlockSpec(memory_space=pltpu.VMEM)],
        input_output_aliases={i: 2 * n + i for i in range(m)},
        compiler_params=SIDE_EFFECT,
    )(*[pltpu.with_memory_space_constraint(o, pltpu.HBM) for o in ops])
    thru = outs[2 * n:2 * n + m]
    return outs[:n], outs[n:2 * n], (None if srcs is None else thru[:n]), thru[m - n:], outs[-1]


def _exchange_wait(name, kind, send_sems, recv_sems, srcs, lands, after):
    n = len(lands)
    ops = ([] if srcs is None else list(srcs)) + list(lands)
    m = len(ops)

    def body(*refs):
        src_refs = [None] * n if srcs is None else refs[:n]
        land_refs = refs[m - n:m]
        send_refs, recv_refs = refs[m:m + n], refs[m + n:m + 2 * n]
        for a in range(n):
            for cp in _exchange_copies(kind, src_refs[a], land_refs[a], send_refs[a], recv_refs[a], False):
                cp.wait_send()
                cp.wait_recv()

    outs = pl.pallas_call(
        body, name=name,
        out_shape=[pltpu.HBM(o.shape, o.dtype) for o in ops],
        in_specs=[HBM] * m + [SEM] * (2 * n) + [pl.BlockSpec(memory_space=pl.ANY)],
        out_specs=[HBM] * m,
        input_output_aliases={i: i for i in range(m)},
        compiler_params=SIDE_EFFECT,
    )(*ops, *send_sems, *recv_sems, after)
    return (None if srcs is None else outs[:n]), outs[m - n:]


def _with_own(land, own, me):
    return lax.dynamic_update_slice_in_dim(land, own, me, 0)


def _shards_to_cols(g):
    return jnp.transpose(g, (1, 0, 2)).reshape(g.shape[1], N_DEV * g.shape[2])


def _cols_to_shards(a):
    R, Ctot = a.shape
    return jnp.transpose(a.reshape(R, N_DEV, Ctot // N_DEV), (1, 0, 2))


def kernel(x, c, ctx, c_ctx, norm_attn_w, norm_mlp_w, w_ada, b_ada, w_in, attn_sink, pool_w, pool_scale, w_out, w_mlp_up, w_mlp_down, final_norm_w, loss_target, m_c_ctx, m_norm_attn_w, m_norm_mlp_w, m_w_ada, m_b_ada, m_w_in, m_attn_sink, m_pool_w, m_pool_scale, m_w_out, m_w_mlp_up, m_w_mlp_down, m_final_norm_w, v_c_ctx, v_norm_attn_w, v_norm_mlp_w, v_w_ada, v_b_ada, v_w_in, v_attn_sink, v_pool_w, v_pool_scale, v_w_out, v_w_mlp_up, v_w_mlp_down, v_final_norm_w):
    _, L, D = x.shape
    H = attn_sink.shape[1]
    A = H * HEAD_DIM
    KV = A // GQA
    P = pool_scale.shape[1]
    MODW = N_MOD * D
    ws = MODW // N_DEV
    gd = P // POOL_GROUPS
    me = _dev_index(lax.axis_index("x"), lax.axis_index("y"), lax.axis_index("c"))

    x2d, ctx2d, tgt = x[0], ctx[0], loss_target[0]
    cctx_row = c_ctx.reshape(1, D)
    wf_row = final_norm_w.reshape(1, D)
    w_ada_l = w_ada[0]
    pool_w_l = pool_w[0].reshape(POOL_GROUPS * (gd // N_DEV), gd)

    (c_all,) = _all_gather("gather_cond", [c])
    cond = jnp.concatenate([c_all[:, 0, :], cctx_row, jnp.zeros((COND_ROWS - N_DEV - 1, D), F32)], axis=0)
    b_sh = lax.dynamic_slice_in_dim(b_ada, me * ws, ws, axis=1)
    (mods_sh,) = _mm("ada_mod", cond, w_ada_l, "nn", [F32], SMALL_TILES, a_pre=_silu, extras=[("n", b_sh)], epilogue=lambda acc, b: (acc + b,))
    (mods_g,) = _all_gather("gather_mods", [mods_sh])

    w_srcs = [w_in[0].astype(BF16), w_out[0].astype(BF16), pool_w_l.astype(BF16), w_mlp_up[0].astype(BF16), w_mlp_down[0].astype(BF16)]
    w_srcs, mods_g = lax.optimization_barrier((w_srcs, mods_g))
    gather_start = _exchange_start("gather_weights_start", "gather_chips", w_srcs)

    def weights(tag, started, lo, hi, after_chips, after_forward):
        gw_send, gw_recv, gw_src, gw_land, _ = started
        mine, lands = _exchange_wait(f"gather_{tag}_wait", "gather_chips", gw_send[lo:hi], gw_recv[lo:hi], gw_src[lo:hi],
                                     gw_land[lo:hi], after_chips)
        f_send, f_recv, _, f_land, f_token = _exchange_start(f"forward_{tag}_start", "forward", None, lands)
        _, lands = _exchange_wait(f"forward_{tag}_wait", "forward", f_send, f_recv, None, f_land,
                                  f_token if after_forward is None else after_forward)
        return [_with_own(l, s[None], me) for l, s in zip(lands, mine)]

    mods = _shards_to_cols(mods_g)
    mod_b = lax.dynamic_slice_in_dim(mods, me, 1, axis=0)
    sh_a, sc_a, g_a, sh_m, sc_m, g_m = [mod_b[:, i * D:(i + 1) * D] for i in range(N_MOD)]
    csh_a, csc_a = mods[N_DEV:N_DEV + 1, :D], mods[N_DEV:N_DEV + 1, D:2 * D]

    cos, sin = _rope_tables(L)
    h = _norm_fwd("norm_attn", x2d, norm_attn_w, sc_a, sh_a)
    hc = _norm_fwd("norm_attn_ctx", ctx2d, norm_attn_w, csc_a, csh_a)
    (win_g,) = weights("w_in", gather_start, 0, 1, h, None)
    W_in = _shards_to_cols(win_g)
    W_qk, W_kv = W_in[:, :A + KV], W_in[:, A:A + 2 * KV]
    W_uv = jnp.concatenate([W_in[:, A + 2 * KV:], W_in[:, A + KV:A + 2 * KV]], axis=1)
    qk, uv = _in_proj(h, W_qk, W_uv, cos, sin)
    (kvc,) = _mm("in_proj_ctx", hc, W_kv, "nn", [BF16], SMALL_TILES)
    attn, lse = _attn_fwd(qk, uv, kvc, attn_sink, A, KV, P)
    wout_g, pw_g = weights("w_out", gather_start, 1, 3, qk, attn)
    W_out = wout_g.reshape(A + P, D)
    PW = jnp.transpose(pw_g.reshape(N_DEV, POOL_GROUPS, gd // N_DEV, gd), (1, 0, 2, 3)).reshape(POOL_GROUPS, gd, gd)
    ap, pooled = _pool_fwd(uv, PW, pool_scale, attn)
    o, x1, hm = _mm("out_proj_norm", ap, W_out, "nn", [F32, F32, BF16], (256, D, D), chunk=128, epilogue=_out_proj_epilogue,
                    extras=[("mn", x2d), ("n", g_a), ("n", norm_mlp_w), ("n", sc_m), ("n", sh_m)])
    W_up, wdown_g = weights("w_mlp", gather_start, 3, 5, attn, x1)
    W_down = wdown_g.reshape(-1, D)
    up, act = _mm("mlp_up", hm, W_up, "nn", [F32, BF16], (1024, 1024, 2048), epilogue=lambda acc: (acc, _relu2(acc)), b_shards=True)
    d_x2, d_mlp, d_wf, d_gm, loss_row = _mm(
        "mlp_down_loss", act, W_down, "nn", [F32, BF16], (512, D, 1024), chunk=128, n_sums=3, vmem=FUSED_VMEM_LIMIT,
        epilogue=_mlp_down_epilogue, extras=[("mn", x1), ("mn", tgt), ("n", g_m), ("n", wf_row)])
    loss_p = loss_row[:, :1]

    (d_up,) = _mm("mlp_down_bwd_act", d_mlp, W_down, "nt", [BF16], (1024, 1024, 2048), extras=[("mn", up)],
                  epilogue=lambda acc, uu: (acc * (2.0 * jnp.maximum(uu, 0.0)),))
    (gW_down,) = _mm("mlp_down_bwd_w", act, d_mlp, "tn", [BF16], (1024, 2048, 2048), vmem=FUSED_VMEM_LIMIT)
    (gW_up_s,) = _mm("mlp_up_bwd_w", hm, d_up, "tn", [BF16], (2048, 1024, 2048), out_shards=True, vmem=FUSED_VMEM_LIMIT)
    g_mlp_srcs = [gW_up_s, gW_down.reshape(N_DEV, -1, D)]
    g_mlp = _exchange_start("grads_mlp_start", "all_to_all", g_mlp_srcs)
    (d_hm,) = _mm("mlp_up_bwd_act", d_up, W_up, "nt", [F32], (1024, D, 1024), b_shards=True)

    d_x1, d_o, d_attn, d_pool, s_sh_m, s_sc_m, s_w_nm, d_ga = _norm_bwd_out_proj_bwd(
        x1, d_hm, d_x2, o, norm_mlp_w, sc_m + g_mlp[4][0, 0], g_a, W_out.T, A)
    (gW_out,) = _mm("out_proj_bwd_w", ap, d_o, "tn", [BF16], (1024, 2048, 2048))
    d_pooled, gPW, d_pscale = _pool_bwd_mix(d_pool, pooled, PW, pool_scale)
    gpw_s = jnp.transpose(gPW.astype(BF16).reshape(POOL_GROUPS, N_DEV, gd // N_DEV, gd), (1, 0, 2, 3)).reshape(N_DEV, -1, gd)
    g_mix_srcs = [gW_out.reshape(N_DEV, (A + P) // N_DEV, D), gpw_s]
    g_mix = _exchange_start("grads_mix_start", "all_to_all", g_mix_srcs)
    lse = lse + g_mix[4][0, 0]
    d_u = _pool_bwd_window(d_pooled)
    d_q, rd, dsink_q, d_kvc = _attn_bwd_dq(qk, uv, kvc, attn_sink, d_attn, lse, cos, sin, A, KV, P)
    d_k, d_v = _attn_bwd_dkv(qk, uv, d_attn, lse, rd, cos, sin, A, KV, P)
    d_sink = _sum_lanes("sink_grad", dsink_q).reshape(1, H)
    d_p = jnp.concatenate([d_q, d_k, d_v, d_u], axis=1)
    d_kvc_b = d_kvc.astype(BF16)
    (gW_kv_ctx,) = _mm("in_proj_ctx_bwd_w", hc, d_kvc_b, "tn", [F32], SMALL_TILES)
    (d_hc,) = _mm("in_proj_ctx_bwd_act", d_kvc_b, W_kv, "nt", [F32], SMALL_TILES)
    gW_in_init = jnp.pad(gW_kv_ctx, ((0, 0), (A, P)))
    (gW_in,) = _mm("in_proj_bwd_w", h, d_p, "tn", [BF16], (1024, 1280, 2048), extras=[("mn", gW_in_init)], epilogue=lambda acc, init: (acc + init,))
    g_in_srcs = [_cols_to_shards(gW_in)]
    g_in = _exchange_start("grads_in_start", "all_to_all", g_in_srcs)
    grad_x, s_sh_a, s_sc_a, s_w_na = _mm(
        "in_proj_bwd_norm", d_p, W_in, "nt", [F32], (256, D, A + 2 * KV + P), chunk=128, n_sums=3,
        epilogue=lambda acc, xr, dres, w, sc: _modulated_norm_bwd(xr, acc, dres, w, sc),
        extras=[("mn", x2d), ("mn", d_x1), ("n", norm_attn_w), ("n", sc_a + g_in[4][0, 0])])
    s_csh, s_csc, s_w_na = _norm_bwd_sums("norm_attn_ctx_bwd", ctx2d, d_hc, norm_attn_w, csc_a, s_w_na)

    pad_l = lambda a: jnp.pad(a, ((0, 0), (0, LANES - a.shape[1])))
    d_mod_b = jnp.concatenate([s_sh_a, s_sc_a, d_ga, s_sh_m, s_sc_m, d_gm], axis=1)
    summed = jnp.concatenate([s_csh, s_csc, s_w_na, s_w_nm, d_wf, d_pscale, pad_l(d_sink), pad_l(loss_p)], axis=1)
    (small_g,) = _all_gather("gather_small", [jnp.concatenate([d_mod_b, summed], axis=1)])
    small_g = small_g[:, 0, :]
    tot = _sum_rows("small_sum", small_g[:, MODW:])
    off = [0]
    for wdt in (D, D, D, D, D, P, LANES, LANES):
        off.append(off[-1] + wdt)
    seg = lambda i: tot[:, off[i]:off[i + 1]]
    g_norm_attn, g_norm_mlp, g_final, g_pscale = seg(2), seg(3), seg(4), seg(5)
    g_sink, loss = seg(6)[:, :H], seg(7)[0, 0]
    d_mod_ctx = jnp.concatenate([seg(0), seg(1), jnp.zeros((1, MODW - 2 * D), F32)], axis=1)
    d_mod = jnp.concatenate([small_g[:, :MODW], d_mod_ctx, jnp.zeros((COND_ROWS - N_DEV - 1, MODW), F32)], axis=0)
    g_b_ada = _sum_rows("b_ada_grad", d_mod[:N_DEV + 1])
    d_mod_sh = lax.dynamic_slice_in_dim(d_mod, me * ws, ws, axis=1)
    (g_w_ada,) = _mm("ada_bwd_w", cond, d_mod_sh, "tn", [F32], SMALL_TILES, a_pre=_silu)
    (d_cond_p,) = _mm("ada_bwd_cond", d_mod_sh, w_ada_l, "nt", [F32], SMALL_TILES)
    (d_cctx_g,) = _all_gather("gather_cctx", [d_cond_p[N_DEV:N_DEV + 1]])
    g_c_ctx = _silu_grad_mul(cctx_row, _sum_rows("cctx_sum", d_cctx_g[:, 0, :]))

    def arrived(name, started):
        srcs, lands = _exchange_wait(name, "all_to_all", started[0], started[1], started[2], started[3], g_c_ctx)
        return [_with_own(l, lax.dynamic_index_in_dim(s, me, 0, keepdims=True), me) for l, s in zip(lands, srcs)]

    r_up, r_down = arrived("grads_mlp_wait", g_mlp)
    r_out, r_pw = arrived("grads_mix_wait", g_mix)
    (r_in,) = arrived("grads_in_wait", g_in)

    results = {
        "c_ctx": _adamw("adam_c_ctx", cctx_row, g_c_ctx, m_c_ctx.reshape(1, D), v_c_ctx.reshape(1, D)),
        "norm_attn_w": _adamw("adam_norm_attn", norm_attn_w, g_norm_attn, m_norm_attn_w, v_norm_attn_w),
        "norm_mlp_w": _adamw("adam_norm_mlp", norm_mlp_w, g_norm_mlp, m_norm_mlp_w, v_norm_mlp_w),
        "w_ada": _adamw("adam_w_ada", w_ada_l, g_w_ada, m_w_ada[0], v_w_ada[0]),
        "b_ada": _adamw("adam_b_ada", b_ada, g_b_ada, m_b_ada, v_b_ada),
        "w_in": _adamw("adam_w_in", w_in[0], r_in, m_w_in[0], v_w_in[0]),
        "attn_sink": _adamw("adam_sink", attn_sink, g_sink, m_attn_sink, v_attn_sink),
        "pool_w": _adamw("adam_pool_w", pool_w_l, r_pw, m_pool_w[0].reshape(pool_w_l.shape), v_pool_w[0].reshape(pool_w_l.shape)),
        "pool_scale": _adamw("adam_pool_scale", pool_scale, g_pscale, m_pool_scale, v_pool_scale),
        "w_out": _adamw("adam_w_out", w_out[0], r_out, m_w_out[0], v_w_out[0]),
        "w_mlp_up": _adamw("adam_w_up", w_mlp_up[0], r_up, m_w_mlp_up[0], v_w_mlp_up[0]),
        "w_mlp_down": _adamw("adam_w_down", w_mlp_down[0], r_down, m_w_mlp_down[0], v_w_mlp_down[0]),
        "final_norm_w": _adamw("adam_final_norm", wf_row, g_final, m_final_norm_w.reshape(1, D), v_final_norm_w.reshape(1, D)),
    }
    shapes = {"c_ctx": c_ctx.shape, "norm_attn_w": norm_attn_w.shape, "norm_mlp_w": norm_mlp_w.shape, "w_ada": w_ada.shape,
              "b_ada": b_ada.shape, "w_in": w_in.shape, "attn_sink": attn_sink.shape, "pool_w": pool_w.shape,
              "pool_scale": pool_scale.shape, "w_out": w_out.shape, "w_mlp_up": w_mlp_up.shape, "w_mlp_down": w_mlp_down.shape,
              "final_norm_w": final_norm_w.shape}
    outs = [loss, grad_x.reshape(x.shape)]
    for part in range(4):
        outs += [results[name][part].reshape(shape) for name, shape in shapes.items()]
    return tuple(outs)
```

```python
import jax
import jax.numpy as jnp
import numpy as np
from jax import lax
from jax.experimental import pallas as pl
from jax.experimental.pallas import tpu as pltpu

F32 = jnp.float32
BF16 = jnp.bfloat16
I32 = jnp.int32

HEAD_DIM = 64
GQA = 4
BLOCK = 128
GRID_W = 64
ROPE_BASE = 10000.0
POOL_WINDOWS = (2, 4, 8, 16)
POOL_GROUPS = len(POOL_WINDOWS)
HALO = 8
N_MOD = 6
EPS = 1e-6
NEG_INF = -1e30
ADAM_LR = 0.001
ADAM_B1 = 0.9
ADAM_B2 = 0.999
ADAM_EPS = 1e-08
ADAM_WD = 0.01
ADAM_STEP = 10
N_DEV = 8
COND_ROWS = 2 * N_DEV
LANES = 128
SUBLANES_16BIT = 16
VMEM_LIMIT = 48 * 1024 * 1024
FUSED_VMEM_LIMIT = 56 * 1024 * 1024
SMALL_TILES = (512, 1024, 512)
MESH = pl.DeviceIdType.MESH
HBM = pl.BlockSpec(memory_space=pltpu.HBM)
SEM = pl.BlockSpec(memory_space=pltpu.SEMAPHORE)
SIDE_EFFECT = pltpu.CompilerParams(has_side_effects=pltpu.SideEffectType.DATAFLOW_SIDE_EFFECTING)


def _cparams(*sem):
    return pltpu.CompilerParams(dimension_semantics=sem, vmem_limit_bytes=VMEM_LIMIT)


def _tile(n, pref, align):
    if n <= pref:
        return n
    t = (pref // align) * align
    while t >= align:
        if n % t == 0:
            return t
        t -= align
    return n


def _dot(a, b):
    return lax.dot_general(a, b, (((1,), (0,)), ((), ())), preferred_element_type=F32)


def _dot_nt(a, b):
    return lax.dot_general(a, b, (((1,), (1,)), ((), ())), preferred_element_type=F32)


def _dot_tn(a, b):
    return lax.dot_general(a, b, (((0,), (0,)), ((), ())), preferred_element_type=F32)


_DOTS = {"nn": _dot, "nt": _dot_nt, "tn": _dot_tn}


def _mm(name, a, b, mode, out_dtypes, tiles, *, epilogue=None, extras=(), a_pre=None, n_sums=0, chunk=None,
        b_shards=False, out_shards=False, vmem=VMEM_LIMIT):
    if mode == "nn":
        M, K = a.shape
        K2, N = (b.shape[1], N_DEV * b.shape[2]) if b_shards else b.shape
    elif mode == "nt":
        M, K = a.shape
        N, K2 = (b.shape[1], N_DEV * b.shape[2]) if b_shards else b.shape
    else:
        (K, M), (K2, N) = a.shape, b.shape
    assert K == K2 and not (b_shards and mode == "tn"), (name, a.shape, b.shape)
    n_span = N // N_DEV if out_shards or (b_shards and mode == "nn") else N
    k_span = K // N_DEV if b_shards and mode == "nt" else K
    tm = _tile(M, tiles[0], LANES if mode == "tn" else SUBLANES_16BIT)
    tn = _tile(n_span, tiles[1], LANES)
    tk = _tile(k_span, tiles[2], SUBLANES_16BIT if mode == "tn" else LANES)
    nk, nb, kb = K // tk, n_span // tn, k_span // tk
    rows = tm if chunk is None else min(chunk, tm)
    n_ex, n_out = len(extras), len(out_dtypes)
    use_acc = nk > 1 or rows < tm
    assert n_sums == 0 or N == tn, name

    def product(a_ref, b_ref):
        at = a_ref[...]
        if a_pre is not None:
            at = a_pre(at)
        return _DOTS[mode](at.astype(BF16), b_ref[...].astype(BF16))

    def apply(acc, ex, out_refs, sl):
        res = (acc,) if epilogue is None else epilogue(acc, *ex)
        for o_ref, o in zip(out_refs, res[:n_out]):
            o_ref[sl, :] = o.astype(o_ref.dtype)
        return tuple(res[n_out:])

    def finish(acc, ex_refs, out_refs, sum_refs):
        if rows == tm:
            acc = acc if not use_acc else acc[...]
            sums = apply(acc, [r[...] for r in ex_refs], out_refs, slice(None))
        else:
            def one(ci, sums):
                sl = pl.ds(pl.multiple_of(ci * rows, rows), rows)
                ex = [r[...] if kind == "n" else r[sl, :] for (kind, _), r in zip(extras, ex_refs)]
                return tuple(s + v for s, v in zip(sums, apply(acc[sl, :], ex, out_refs, sl)))
            sums = lax.fori_loop(0, tm // rows, one, tuple(jnp.zeros((1, tn), F32) for _ in range(n_sums)))
        first = pl.program_id(0) == 0
        for s_ref, sv in zip(sum_refs, sums):
            @pl.when(first)
            def _(s_ref=s_ref, sv=sv):
                s_ref[...] = sv

            @pl.when(jnp.logical_not(first))
            def _(s_ref=s_ref, sv=sv):
                s_ref[...] += sv

    def body(a_ref, b_ref, *rest):
        ex_refs, out_refs = rest[:n_ex], rest[n_ex:n_ex + n_out]
        sum_refs = rest[n_ex + n_out:n_ex + n_out + n_sums]
        if not use_acc:
            finish(product(a_ref, b_ref), ex_refs, out_refs, sum_refs)
            return
        acc_ref = rest[-1]
        k = pl.program_id(2)

        @pl.when(k == 0)
        def _():
            acc_ref[...] = product(a_ref, b_ref)

        @pl.when(k > 0)
        def _():
            acc_ref[...] += product(a_ref, b_ref)

        @pl.when(k == nk - 1)
        def _():
            finish(acc_ref, ex_refs, out_refs, sum_refs)

    a_spec = pl.BlockSpec((tk, tm), lambda i, j, k: (k, i)) if mode == "tn" else pl.BlockSpec((tm, tk), lambda i, j, k: (i, k))
    if not b_shards:
        b_spec = pl.BlockSpec((tn, tk), lambda i, j, k: (j, k)) if mode == "nt" else pl.BlockSpec((tk, tn), lambda i, j, k: (k, j))
    elif mode == "nn":
        b_spec = pl.BlockSpec((None, tk, tn), lambda i, j, k: (j // nb, k, j % nb))
    else:
        b_spec = pl.BlockSpec((None, tn, tk), lambda i, j, k: (k // kb, j, k % kb))
    ex_specs = []
    for kind, arr in extras:
        if kind == "mn":
            ex_specs.append(pl.BlockSpec((tm, tn), lambda i, j, k: (i, j)))
        elif kind == "n":
            ex_specs.append(pl.BlockSpec((1, tn), lambda i, j, k: (0, j)))
        else:
            ex_specs.append(pl.BlockSpec((tm, arr.shape[1]), lambda i, j, k: (i, 0)))
    if out_shards:
        out_specs = [pl.BlockSpec((None, tm, tn), lambda i, j, k: (j // nb, i, j % nb)) for _ in out_dtypes]
        out_shape = [jax.ShapeDtypeStruct((N_DEV, M, n_span), d) for d in out_dtypes]
    else:
        out_specs = [pl.BlockSpec((tm, tn), lambda i, j, k: (i, j)) for _ in out_dtypes]
        out_shape = [jax.ShapeDtypeStruct((M, N), d) for d in out_dtypes]
    out_specs += [pl.BlockSpec((1, tn), lambda i, j, k: (0, 0))] * n_sums
    out_shape += [jax.ShapeDtypeStruct((1, N), F32)] * n_sums
    return pl.pallas_call(
        body,
        name=name,
        grid=(M // tm, N // tn, nk),
        in_specs=[a_spec, b_spec] + ex_specs,
        out_specs=out_specs,
        out_shape=out_shape,
        scratch_shapes=[pltpu.VMEM((tm, tn), F32)] if use_acc else [],
        compiler_params=pltpu.CompilerParams(
            dimension_semantics=("arbitrary",) * 3 if n_sums else ("parallel", "parallel", "arbitrary"), vmem_limit_bytes=vmem),
    )(a, b, *[arr for _, arr in extras])


def _silu(v):
    return v / (1.0 + jnp.exp(-v))


def _relu2(v):
    r = jnp.maximum(v, 0.0)
    return r * r


def _rope_tables(L):
    half = HEAD_DIM // 2
    inv_freq = np.float32(ROPE_BASE) ** (-np.arange(0, half, 2, dtype=np.float32) / np.float32(half))
    t = np.arange(L)
    row, col = t // GRID_W, t % GRID_W
    ang_r = row.astype(np.float32)[:, None] * inv_freq[None, :]
    ang_c = col.astype(np.float32)[:, None] * inv_freq[None, :]
    cos = np.concatenate([np.cos(ang_r), np.cos(ang_r), np.cos(ang_c), np.cos(ang_c)], axis=1)
    sin = np.concatenate([-np.sin(ang_r), np.sin(ang_r), -np.sin(ang_c), np.sin(ang_c)], axis=1)
    reps = LANES // HEAD_DIM
    return jnp.asarray(np.tile(cos, (1, reps)), F32), jnp.asarray(np.tile(sin, (1, reps)), F32)


def _rope(xf, cos, sin):
    quarter = HEAD_DIM // 4
    lane = lax.broadcasted_iota(I32, (xf.shape[0], LANES), 1)
    first = (lane & quarter) == 0
    outs = []
    for j in range(xf.shape[1] // LANES):
        xc = xf[:, j * LANES:(j + 1) * LANES]
        partner = jnp.where(first, pltpu.roll(xc, LANES - quarter, 1), pltpu.roll(xc, quarter, 1))
        outs.append(xc * cos + partner * sin)
    return outs[0] if len(outs) == 1 else jnp.concatenate(outs, axis=1)


def _inv_rms(xf):
    return lax.rsqrt(jnp.mean(xf * xf, axis=-1, keepdims=True) + EPS)


def _modulated_norm(xf, w, sc, sh):
    return ((xf * _inv_rms(xf)) * w) * (1.0 + sc) + sh


def _modulated_norm_bwd(xf, dh, dres, w, sc):
    r = _inv_rms(xf)
    xh = xf * r
    dn = dh * (1.0 + sc)
    dxh = dn * w
    dx = dres + r * (dxh - xh * jnp.mean(dxh * xh, axis=-1, keepdims=True))
    col = lambda v: jnp.sum(v, axis=0, keepdims=True)
    return dx, col(dh), col(dh * (xh * w)), col(dn * xh)


def _in_proj(h, w_qk, w_uv, cos, sin):
    L, D = h.shape
    T = _tile(L, 512, SUBLANES_16BIT)

    def body(h_ref, wqk_ref, wuv_ref, cos_ref, sin_ref, qk_ref, uv_ref):
        hv = h_ref[...]
        qk_ref[...] = _rope(_dot(hv, wqk_ref[...]), cos_ref[...], sin_ref[...]).astype(BF16)
        uv_ref[...] = _dot(hv, wuv_ref[...])

    tile = lambda wd: pl.BlockSpec((T, wd), lambda i: (i, 0))
    whole = lambda a: pl.BlockSpec(a.shape, lambda i: (0, 0))
    return pl.pallas_call(
        body, name="in_proj", grid=(L // T,),
        in_specs=[tile(D), whole(w_qk), whole(w_uv), tile(LANES), tile(LANES)],
        out_specs=[tile(w_qk.shape[1]), tile(w_uv.shape[1])],
        out_shape=[jax.ShapeDtypeStruct((L, w_qk.shape[1]), BF16), jax.ShapeDtypeStruct((L, w_uv.shape[1]), F32)],
        compiler_params=_cparams("parallel"),
    )(h, w_qk, w_uv, cos, sin)


def _norm_fwd(name, x, w, sc, sh):
    L, D = x.shape
    T = _tile(L, 512, 8)

    def body(x_ref, w_ref, sc_ref, sh_ref, h_ref):
        h_ref[...] = _modulated_norm(x_ref[...], w_ref[...], sc_ref[...], sh_ref[...]).astype(BF16)

    row = pl.BlockSpec((1, D), lambda i: (0, 0))
    return pl.pallas_call(
        body, name=name, grid=(L // T,),
        in_specs=[pl.BlockSpec((T, D), lambda i: (i, 0)), row, row, row],
        out_specs=pl.BlockSpec((T, D), lambda i: (i, 0)),
        out_shape=jax.ShapeDtypeStruct((L, D), BF16),
        compiler_params=_cparams("parallel"),
    )(x, w, sc, sh)


def _norm_bwd_sums(name, x, dh, w, sc, w_init):
    L, D = x.shape
    T = _tile(L, 256, 8)

    def body(x_ref, dh_ref, w_ref, sc_ref, wi_ref, ssh_ref, ssc_ref, sw_ref):
        @pl.when(pl.program_id(0) == 0)
        def _():
            ssh_ref[...] = jnp.zeros_like(ssh_ref)
            ssc_ref[...] = jnp.zeros_like(ssc_ref)
            sw_ref[...] = wi_ref[...]

        dh = dh_ref[...]
        _, s_sh, s_sc, s_w = _modulated_norm_bwd(x_ref[...], dh, jnp.zeros_like(dh), w_ref[...], sc_ref[...])
        ssh_ref[...] += s_sh
        ssc_ref[...] += s_sc
        sw_ref[...] += s_w

    tile = pl.BlockSpec((T, D), lambda i: (i, 0))
    row = pl.BlockSpec((1, D), lambda i: (0, 0))
    return pl.pallas_call(
        body, name=name, grid=(L // T,), in_specs=[tile, tile, row, row, row], out_specs=[row, row, row],
        out_shape=[jax.ShapeDtypeStruct((1, D), F32)] * 3, compiler_params=_cparams("arbitrary"),
    )(x, dh, w, sc, w_init)


def _norm_bwd_out_proj_bwd(x, dh, dres, o, w, sc, g, w_out_t, A):
    L, D = x.shape
    N = w_out_t.shape[1]
    T = _tile(L, 256, SUBLANES_16BIT)
    half = T // 2

    def body(x_ref, dh_ref, dres_ref, o_ref, w_ref, sc_ref, g_ref, wt_ref, dx_ref, do_ref, dattn_ref, dpool_ref,
             ssh_ref, ssc_ref, sw_ref, sg_ref):
        @pl.when(pl.program_id(0) == 0)
        def _():
            for s_ref in (ssh_ref, ssc_ref, sw_ref, sg_ref):
                s_ref[...] = jnp.zeros_like(s_ref)

        for rows in (slice(0, half), slice(half, T)):
            dx, s_sh, s_sc, s_w = _modulated_norm_bwd(x_ref[rows, :], dh_ref[rows, :], dres_ref[rows, :], w_ref[...], sc_ref[...])
            ssh_ref[...] += s_sh
            ssc_ref[...] += s_sc
            sw_ref[...] += s_w
            sg_ref[...] += jnp.sum(dx * o_ref[rows, :], axis=0, keepdims=True)
            dx_ref[rows, :] = dx
            do_ref[rows, :] = (g_ref[...] * dx).astype(BF16)
        dap = _dot(do_ref[...], wt_ref[...])
        dattn_ref[...] = dap[:, :A].astype(BF16)
        dpool_ref[...] = dap[:, A:]

    tile = pl.BlockSpec((T, D), lambda i: (i, 0))
    row = pl.BlockSpec((1, D), lambda i: (0, 0))
    return pl.pallas_call(
        body, name="norm_mlp_bwd_out_proj_bwd", grid=(L // T,),
        in_specs=[tile, tile, tile, tile, row, row, row, pl.BlockSpec((D, N), lambda i: (0, 0))],
        out_specs=[tile, tile, pl.BlockSpec((T, A), lambda i: (i, 0)), pl.BlockSpec((T, N - A), lambda i: (i, 0)), row, row, row, row],
        out_shape=[jax.ShapeDtypeStruct((L, D), F32), jax.ShapeDtypeStruct((L, D), BF16), jax.ShapeDtypeStruct((L, A), BF16),
                   jax.ShapeDtypeStruct((L, N - A), F32)] + [jax.ShapeDtypeStruct((1, D), F32)] * 4,
        compiler_params=pltpu.CompilerParams(dimension_semantics=("arbitrary",), vmem_limit_bytes=FUSED_VMEM_LIMIT),
    )(x, dh, dres, o, w, sc, g, w_out_t)


def _out_proj_epilogue(acc, xr, g, w, sc, sh):
    x1 = xr + g * acc
    return acc, x1, _modulated_norm(x1, w, sc, sh)


def _mlp_down_epilogue(acc, x1, tgt, g, wf):
    D = acc.shape[1]
    x2 = x1 + g * acc
    r = _inv_rms(x2)
    xh = x2 * r
    err = xh * wf - tgt
    loss = 0.5 * jnp.sum(jnp.mean(err * err, axis=-1, keepdims=True), axis=0, keepdims=True)
    dy = err * (1.0 / D)
    dxh = dy * wf
    dx = r * (dxh - xh * jnp.mean(dxh * xh, axis=-1, keepdims=True))
    col = lambda v: jnp.sum(v, axis=0, keepdims=True)
    return dx, g * dx, col(dy * xh), col(dx * acc), jnp.broadcast_to(loss, (1, D))


def _heads(ref, first, n):
    return jnp.concatenate([ref[:, (first + g) * HEAD_DIM:(first + g + 1) * HEAD_DIM] for g in range(n)], axis=0)


def _edge_variants(masks):
    return jnp.asarray(np.stack([np.where(masks(first, last), 0.0, NEG_INF).astype(np.float32)
                                 for last in (False, True) for first in (False, True)]))


def _attn_bias(C):
    kj = np.arange(3 * BLOCK + C)[:, None]
    qi = (np.arange(GQA * BLOCK) % BLOCK)[None, :]

    def masks(first, last):
        window = (kj >= qi) & (kj <= qi + 2 * BLOCK) & (kj >= (BLOCK if first else 0)) & (kj < (2 * BLOCK if last else 3 * BLOCK))
        return window | (kj >= 3 * BLOCK)

    return _edge_variants(masks)


def _attn_bias_keys():
    kj = np.arange(BLOCK)[:, None]
    col = np.arange(3 * GQA * BLOCK)[None, :]
    part, qi = col // (GQA * BLOCK), col % BLOCK

    def masks(first, last):
        return ((part == 0) & (kj <= qi) & (not first)) | (part == 1) | ((part == 2) & (kj >= qi) & (not last))

    return _edge_variants(masks)


def _edge_index(n, nb):
    return (n == 0).astype(I32) + 2 * (n == nb - 1).astype(I32)


def _head_rows(ref, hk):
    return jnp.concatenate([ref[hk * GQA + g:hk * GQA + g + 1, :] for g in range(GQA)], axis=1)


def _rows_to_heads(rows_by_kv_head):
    return jnp.concatenate([r[:, g * BLOCK:(g + 1) * BLOCK] for r in rows_by_kv_head for g in range(GQA)], axis=0)


def _queries_to_rows(t):
    return jnp.concatenate([t[:, g * BLOCK:(g + 1) * BLOCK].T for g in range(GQA)], axis=1)


def _attn_specs(L, A, KV, C, vcol):
    nb = L // BLOCK
    kcol = A // KV
    prev = lambda n: jnp.maximum(n - 1, 0)
    nxt = lambda n: jnp.minimum(n + 1, nb - 1)
    q_spec = pl.BlockSpec((BLOCK, A), lambda n: (n, 0))
    k_specs = [pl.BlockSpec((BLOCK, KV), lambda n: (prev(n), kcol)), pl.BlockSpec((BLOCK, KV), lambda n: (n, kcol)),
               pl.BlockSpec((BLOCK, KV), lambda n: (nxt(n), kcol))]
    v_specs = [pl.BlockSpec((BLOCK, KV), lambda n: (prev(n), vcol)), pl.BlockSpec((BLOCK, KV), lambda n: (n, vcol)),
               pl.BlockSpec((BLOCK, KV), lambda n: (nxt(n), vcol))]
    kvc_spec = pl.BlockSpec((C, 2 * KV), lambda n: (0, 0))
    return q_spec, k_specs, v_specs, kvc_spec


def _keys_values(hk, k_refs, v_refs, kvc_ref, KV):
    sl = slice(hk * HEAD_DIM, (hk + 1) * HEAD_DIM)
    keys = jnp.concatenate([r[:, sl] for r in k_refs] + [kvc_ref[:, sl]], axis=0)
    vals = jnp.concatenate([r[:, sl].astype(BF16) for r in v_refs] + [kvc_ref[:, KV + hk * HEAD_DIM:KV + (hk + 1) * HEAD_DIM]], axis=0)
    return keys, vals


def _sink_row(sink_ref, hk):
    return jnp.concatenate([jnp.full((1, BLOCK), sink_ref[0, hk * GQA + g], F32) for g in range(GQA)], axis=1)


def _attn_fwd(qk, uv, kvc, sink, A, KV, P):
    L = qk.shape[0]
    C = kvc.shape[0]
    nkv = KV // HEAD_DIM
    H = nkv * GQA
    scale = HEAD_DIM ** -0.5

    def body(sink_ref, q_ref, kp_ref, kc_ref, kn_ref, vp_ref, vc_ref, vn_ref, kvc_ref, bias_ref, o_ref, lse_ref):
        bias = bias_ref[_edge_index(pl.program_id(0), L // BLOCK)]
        lse_rows = []
        for hk in range(nkv):
            keys, vals = _keys_values(hk, (kp_ref, kc_ref, kn_ref), (vp_ref, vc_ref, vn_ref), kvc_ref, KV)
            qs = _heads(q_ref, hk * GQA, GQA) * scale
            s = _dot_nt(keys, qs) + bias
            sk = _sink_row(sink_ref, hk)
            m = jnp.maximum(jnp.max(s, axis=0, keepdims=True), sk)
            p = jnp.exp(s - m)
            den = jnp.sum(p, axis=0, keepdims=True) + jnp.exp(sk - m)
            o = _dot_tn(vals, p.astype(BF16)) * (1.0 / den)
            lse_rows.append(m + jnp.log(den))
            o_ref[:, hk * GQA * HEAD_DIM:(hk + 1) * GQA * HEAD_DIM] = _queries_to_rows(o).astype(BF16)
        lse_ref[...] = _rows_to_heads(lse_rows)

    q_spec, k_specs, v_specs, kvc_spec = _attn_specs(L, A, KV, C, P // KV)
    bias = _attn_bias(C)
    return pl.pallas_call(
        body, name="attn_fwd", grid=(L // BLOCK,),
        in_specs=[pl.BlockSpec(memory_space=pltpu.SMEM), q_spec] + k_specs + v_specs
                 + [kvc_spec, pl.BlockSpec(bias.shape, lambda n: (0, 0, 0))],
        out_specs=[pl.BlockSpec((BLOCK, A), lambda n: (n, 0)), pl.BlockSpec((H, BLOCK), lambda n: (0, n))],
        out_shape=[jax.ShapeDtypeStruct((L, A + P), BF16), jax.ShapeDtypeStruct((H, L), F32)],
        compiler_params=_cparams("parallel"),
    )(sink, qk, qk, qk, qk, uv, uv, uv, kvc, bias)


def _attn_bwd_dq(qk, uv, kvc, sink, dap, lse_t, cos, sin, A, KV, P):
    L = qk.shape[0]
    C = kvc.shape[0]
    nkv = KV // HEAD_DIM
    H = nkv * GQA
    scale = HEAD_DIM ** -0.5
    W = 3 * BLOCK

    def body(sink_ref, q_ref, kp_ref, kc_ref, kn_ref, vp_ref, vc_ref, vn_ref, kvc_ref, do_ref, lse_ref, cos_ref, sin_ref, bias_ref,
             dq_ref, rd_ref, ds_ref, dkvc_ref):
        n = pl.program_id(0)

        @pl.when(n == 0)
        def _():
            dkvc_ref[...] = jnp.zeros_like(dkvc_ref)

        bias = bias_ref[_edge_index(n, L // BLOCK)]
        rd_rows, dsink_rows, dq_parts = [], [], []
        for hk in range(nkv):
            sl = slice(hk * HEAD_DIM, (hk + 1) * HEAD_DIM)
            keys, vals = _keys_values(hk, (kp_ref, kc_ref, kn_ref), (vp_ref, vc_ref, vn_ref), kvc_ref, KV)
            qs = _heads(q_ref, hk * GQA, GQA) * scale
            dos = _heads(do_ref, hk * GQA, GQA).astype(BF16)
            lse = _head_rows(lse_ref, hk)
            p = jnp.exp(_dot_nt(keys, qs) + bias - lse)
            dp = _dot_nt(vals, dos)
            rd = jnp.sum(p * dp, axis=0, keepdims=True)
            ds = (p * (dp - rd)).astype(BF16)
            dq_parts.append(_queries_to_rows(_dot_tn(keys, ds) * scale))
            dkvc_ref[:, sl] += _dot(ds[W:, :], qs)
            dkvc_ref[:, KV + hk * HEAD_DIM:KV + (hk + 1) * HEAD_DIM] += _dot(p[W:, :].astype(BF16), dos)
            rd_rows.append(rd)
            dsink_rows.append(-(jnp.exp(_sink_row(sink_ref, hk) - lse) * rd))
        rd_ref[...] = _rows_to_heads(rd_rows)
        ds_ref[...] = _rows_to_heads(dsink_rows)
        dq = dq_parts[0] if nkv == 1 else jnp.concatenate(dq_parts, axis=1)
        dq_ref[...] = _rope(dq, cos_ref[...], -sin_ref[...]).astype(BF16)

    q_spec, k_specs, v_specs, kvc_spec = _attn_specs(L, A, KV, C, P // KV)
    blk = lambda w: pl.BlockSpec((BLOCK, w), lambda n: (n, 0))
    per_head = pl.BlockSpec((H, BLOCK), lambda n: (0, n))
    bias = _attn_bias(C)
    return pl.pallas_call(
        body, name="attn_bwd_dq", grid=(L // BLOCK,),
        in_specs=[pl.BlockSpec(memory_space=pltpu.SMEM), q_spec] + k_specs + v_specs
                 + [kvc_spec, blk(A), per_head, blk(LANES), blk(LANES), pl.BlockSpec(bias.shape, lambda n: (0, 0, 0))],
        out_specs=[blk(A), per_head, per_head, pl.BlockSpec((C, 2 * KV), lambda n: (0, 0))],
        out_shape=[jax.ShapeDtypeStruct((L, A), BF16), jax.ShapeDtypeStruct((H, L), F32), jax.ShapeDtypeStruct((H, L), F32),
                   jax.ShapeDtypeStruct((C, 2 * KV), F32)],
        compiler_params=_cparams("arbitrary"),
    )(sink, qk, qk, qk, qk, uv, uv, uv, kvc, dap, lse_t, cos, sin, bias)


def _attn_bwd_dkv(qk, uv, dap, lse_t, rd_t, cos, sin, A, KV, P):
    L = qk.shape[0]
    nb = L // BLOCK
    nkv = KV // HEAD_DIM
    H = nkv * GQA
    scale = HEAD_DIM ** -0.5

    def body(k_ref, v_ref, qp_ref, qc_ref, qn_ref, dop_ref, doc_ref, don_ref, lsep_ref, lsec_ref, lsen_ref,
             rdp_ref, rdc_ref, rdn_ref, cos_ref, sin_ref, bias_ref, dk_ref, dv_ref):
        bias = bias_ref[_edge_index(pl.program_id(0), nb)]
        dk_parts, dv_parts = [], []
        for hk in range(nkv):
            sl = slice(hk * HEAD_DIM, (hk + 1) * HEAD_DIM)
            km = k_ref[:, sl]
            vm = v_ref[:, sl].astype(BF16)
            qs = jnp.concatenate([_heads(q, hk * GQA, GQA) for q in (qp_ref, qc_ref, qn_ref)], axis=0) * scale
            dos = jnp.concatenate([_heads(d, hk * GQA, GQA) for d in (dop_ref, doc_ref, don_ref)], axis=0).astype(BF16)
            rows = [slice(hk * GQA + g, hk * GQA + g + 1) for g in range(GQA)]
            lse = jnp.concatenate([t[r, :] for t in (lsep_ref, lsec_ref, lsen_ref) for r in rows], axis=1)
            rdv = jnp.concatenate([t[r, :] for t in (rdp_ref, rdc_ref, rdn_ref) for r in rows], axis=1)
            p = jnp.exp(_dot_nt(km, qs) + bias - lse)
            ds = (p * (_dot_nt(vm, dos) - rdv)).astype(BF16)
            dk_parts.append(_dot(ds, qs))
            dv_parts.append(_dot(p.astype(BF16), dos))
        dk = dk_parts[0] if nkv == 1 else jnp.concatenate(dk_parts, axis=1)
        dv = dv_parts[0] if nkv == 1 else jnp.concatenate(dv_parts, axis=1)
        dk_ref[...] = _rope(dk, cos_ref[...], -sin_ref[...]).astype(BF16)
        dv_ref[...] = dv.astype(BF16)

    prev = lambda m: jnp.maximum(m - 1, 0)
    nxt = lambda m: jnp.minimum(m + 1, nb - 1)
    three = lambda w: [pl.BlockSpec((BLOCK, w), lambda m: (prev(m), 0)), pl.BlockSpec((BLOCK, w), lambda m: (m, 0)),
                       pl.BlockSpec((BLOCK, w), lambda m: (nxt(m), 0))]
    three_t = [pl.BlockSpec((H, BLOCK), lambda m: (0, prev(m))), pl.BlockSpec((H, BLOCK), lambda m: (0, m)),
               pl.BlockSpec((H, BLOCK), lambda m: (0, nxt(m)))]
    blk = lambda w: pl.BlockSpec((BLOCK, w), lambda m: (m, 0))
    bias = _attn_bias_keys()
    return pl.pallas_call(
        body, name="attn_bwd_dkv", grid=(nb,),
        in_specs=[pl.BlockSpec((BLOCK, KV), lambda m: (m, A // KV)), pl.BlockSpec((BLOCK, KV), lambda m: (m, P // KV))]
                 + three(A) + three(A) + three_t + three_t + [blk(LANES), blk(LANES), pl.BlockSpec(bias.shape, lambda m: (0, 0, 0))],
        out_specs=[blk(KV), blk(KV)],
        out_shape=[jax.ShapeDtypeStruct((L, KV), BF16), jax.ShapeDtypeStruct((L, KV), BF16)],
        compiler_params=_cparams("parallel"),
    )(qk, uv, qk, qk, qk, dap, dap, dap, lse_t, lse_t, lse_t, rd_t, rd_t, rd_t, cos, sin, bias)


def _halo_specs(T, L, W, col):
    per = T // HALO
    return [pl.BlockSpec((HALO, W), lambda i: (jnp.maximum(i * per - 1, 0), col)),
            pl.BlockSpec((T, W), lambda i: (i, col)),
            pl.BlockSpec((HALO, W), lambda i: (jnp.minimum((i + 1) * per, L // HALO - 1), col))]


def _fill_halo_buf(buf, prev_ref, cur_ref, next_ref, i, nt, T):
    buf[0:HALO, :] = jnp.where(i > 0, prev_ref[...], 0.0)
    buf[HALO:HALO + T, :] = cur_ref[...]
    buf[HALO + T:2 * HALO + T, :] = jnp.where(i < nt - 1, next_ref[...], 0.0)


def _zero_margins(lv):
    rows = lv.shape[0]
    lv[0:HALO, :] = jnp.zeros((HALO, lv.shape[1]), F32)
    lv[rows - HALO:rows, :] = jnp.zeros((HALO, lv.shape[1]), F32)


def _window_sums(lv, x, w, first):
    n = x.shape[0]
    lv[HALO:HALO + n, :] = x
    cur = x + lv[pl.ds(HALO + first, n), :]
    span = 1
    while 2 * span < w:
        lv[HALO:HALO + n, :] = cur
        cur = lv[pl.ds(HALO - span, n), :] + lv[pl.ds(HALO + span, n), :]
        span *= 2
    return cur


def _counts(t, w, L):
    lo = jnp.clip(t - w // 2, 0, L)
    hi = jnp.clip(t - w // 2 + w, 0, L)
    return jnp.maximum(hi - lo, 1).astype(F32)


def _pool_fwd(u, pw, scale, mix):
    L, P = u.shape[0], scale.shape[1]
    gd = P // POOL_GROUPS
    T = _tile(L, 256, 8)
    nt = L // T
    assert (mix.shape[1] - P) % P == 0
    mix_col = mix.shape[1] // P - 1

    def body(up_ref, uc_ref, un_ref, pw_ref, sc_ref, mix_ref, out_ref, pooled_ref, buf, lv):
        i = pl.program_id(0)
        _fill_halo_buf(buf, up_ref, uc_ref, un_ref, i, nt, T)
        _zero_margins(lv)
        t = i * T + lax.broadcasted_iota(I32, (T, 1), 0)
        for g, w in enumerate(POOL_WINDOWS):
            cols = slice(g * gd, (g + 1) * gd)
            acc = _window_sums(lv, buf[:, cols], w, -1)[HALO:HALO + T]
            pooled = (acc / _counts(t, w, L) - uc_ref[:, cols]).astype(BF16)
            pooled_ref[:, cols] = pooled
            out_ref[:, cols] = (_dot(pooled, pw_ref[g]) * sc_ref[:, cols]).astype(BF16)

    return pl.pallas_call(
        body, name="pool_fwd", grid=(nt,),
        in_specs=_halo_specs(T, L, P, 0) + [pl.BlockSpec((POOL_GROUPS, gd, gd), lambda i: (0, 0, 0)), pl.BlockSpec((1, P), lambda i: (0, 0)),
                                            pl.BlockSpec(memory_space=pl.ANY)],
        out_specs=[pl.BlockSpec((T, P), lambda i: (i, mix_col)), pl.BlockSpec((T, P), lambda i: (i, 0))],
        out_shape=[jax.ShapeDtypeStruct(mix.shape, BF16), jax.ShapeDtypeStruct((L, P), BF16)],
        scratch_shapes=[pltpu.VMEM((T + 2 * HALO, P), F32), pltpu.VMEM((T + 4 * HALO, gd), F32)],
        input_output_aliases={5: 0},
        compiler_params=_cparams("parallel"),
    )(u, u, u, pw, scale, mix)


def _pool_bwd_mix(d_pool, pooled, pw, scale):
    L, P = pooled.shape
    gd = P // POOL_GROUPS
    T = _tile(L, 256, 8)

    def body(dp_ref, pooled_ref, pw_ref, sc_ref, dpooled_ref, dpw_ref, dsc_ref):
        i = pl.program_id(0)

        @pl.when(i == 0)
        def _():
            dpw_ref[...] = jnp.zeros_like(dpw_ref)
            dsc_ref[...] = jnp.zeros_like(dsc_ref)

        for g in range(POOL_GROUPS):
            cols = slice(g * gd, (g + 1) * gd)
            pb = pooled_ref[:, cols]
            dp = dp_ref[:, cols]
            dsc_ref[:, cols] += jnp.sum(dp * _dot(pb, pw_ref[g]), axis=0, keepdims=True)
            dm = (dp * sc_ref[:, cols]).astype(BF16)
            dpw_ref[g] += _dot_tn(pb, dm)
            dpooled_ref[:, cols] = _dot_nt(dm, pw_ref[g])

    return pl.pallas_call(
        body, name="pool_bwd_mix", grid=(L // T,),
        in_specs=[pl.BlockSpec((T, P), lambda i: (i, 0)), pl.BlockSpec((T, P), lambda i: (i, 0)),
                  pl.BlockSpec((POOL_GROUPS, gd, gd), lambda i: (0, 0, 0)), pl.BlockSpec((1, P), lambda i: (0, 0))],
        out_specs=[pl.BlockSpec((T, P), lambda i: (i, 0)), pl.BlockSpec((POOL_GROUPS, gd, gd), lambda i: (0, 0, 0)),
                   pl.BlockSpec((1, P), lambda i: (0, 0))],
        out_shape=[jax.ShapeDtypeStruct((L, P), F32), jax.ShapeDtypeStruct((POOL_GROUPS, gd, gd), F32), jax.ShapeDtypeStruct((1, P), F32)],
        compiler_params=_cparams("arbitrary"),
    )(d_pool, pooled, pw, scale)


def _pool_bwd_window(dpooled):
    L, P = dpooled.shape
    gd = P // POOL_GROUPS
    T = _tile(L, 256, 8)
    nt = L // T

    def body(dp_ref, dc_ref, dn_ref, du_ref, buf, lv):
        i = pl.program_id(0)
        _fill_halo_buf(buf, dp_ref, dc_ref, dn_ref, i, nt, T)
        _zero_margins(lv)
        t = i * T - HALO + lax.broadcasted_iota(I32, (T + 2 * HALO, 1), 0)
        for g, w in enumerate(POOL_WINDOWS):
            cols = slice(g * gd, (g + 1) * gd)
            acc = _window_sums(lv, buf[:, cols] / _counts(t, w, L), w, 1)[HALO:HALO + T]
            du_ref[:, cols] = (acc - dc_ref[:, cols]).astype(BF16)

    return pl.pallas_call(
        body, name="pool_bwd_window", grid=(nt,),
        in_specs=_halo_specs(T, L, P, 0),
        out_specs=pl.BlockSpec((T, P), lambda i: (i, 0)),
        out_shape=jax.ShapeDtypeStruct((L, P), BF16),
        scratch_shapes=[pltpu.VMEM((T + 2 * HALO, P), F32), pltpu.VMEM((T + 4 * HALO, gd), F32)],
        compiler_params=_cparams("parallel"),
    )(dpooled, dpooled, dpooled)


def _sum_rows(name, a):
    R, N = a.shape

    def body(a_ref, o_ref):
        if R <= 16:
            acc = a_ref[0:1, :]
            for r in range(1, R):
                acc = acc + a_ref[r:r + 1, :]
        else:
            acc = jnp.sum(a_ref[...], axis=0, keepdims=True)
        o_ref[...] = acc

    return pl.pallas_call(body, name=name, out_shape=jax.ShapeDtypeStruct((1, N), F32))(a)


def _sum_lanes(name, a):
    def body(a_ref, o_ref):
        o_ref[...] = jnp.sum(a_ref[...], axis=1, keepdims=True)

    return pl.pallas_call(body, name=name, out_shape=jax.ShapeDtypeStruct((a.shape[0], 1), F32))(a)


def _silu_grad_mul(cv, g):
    def body(c_ref, g_ref, o_ref):
        cvv = c_ref[...]
        s = 1.0 / (1.0 + jnp.exp(-cvv))
        o_ref[...] = g_ref[...] * (s * (1.0 + cvv * (1.0 - s)))

    return pl.pallas_call(body, name="silu_grad_mul", out_shape=jax.ShapeDtypeStruct(cv.shape, F32))(cv, g)


def _adamw(name, w, g, m, v):
    R, C = w.shape
    parts = g.ndim == 3
    n_parts = g.shape[0] if parts else 1
    T = _tile(R, max(8, 262144 // C), 8)

    def body(w_ref, g_ref, m_ref, v_ref, go_ref, d_ref, mo_ref, vo_ref):
        if parts:
            gv = g_ref[0].astype(F32)
            for p in range(1, n_parts):
                gv = gv + g_ref[p].astype(F32)
        else:
            gv = g_ref[...]
        mn = ADAM_B1 * m_ref[...] + (1.0 - ADAM_B1) * gv
        vn = ADAM_B2 * v_ref[...] + (1.0 - ADAM_B2) * (gv * gv)
        m_hat = mn / (1.0 - ADAM_B1 ** ADAM_STEP)
        v_hat = vn / (1.0 - ADAM_B2 ** ADAM_STEP)
        go_ref[...] = gv
        d_ref[...] = -ADAM_LR * (m_hat / (jnp.sqrt(v_hat) + ADAM_EPS) + ADAM_WD * w_ref[...])
        mo_ref[...] = mn
        vo_ref[...] = vn

    tile = pl.BlockSpec((T, C), lambda i: (i, 0))
    g_spec = pl.BlockSpec((n_parts, T, C), lambda i: (0, i, 0)) if parts else tile
    return pl.pallas_call(
        body, name=name, grid=(R // T,),
        in_specs=[tile, g_spec, tile, tile], out_specs=[tile] * 4,
        out_shape=[jax.ShapeDtypeStruct((R, C), F32)] * 4,
        compiler_params=_cparams("parallel"),
    )(w, g, m, v)


def _dev_index(px, py, pc):
    return 4 * px + 2 * py + pc


def _all_gather(name, arrs):
    n = len(arrs)

    def body(*refs):
        ins, outs = refs[:n], refs[n:2 * n]
        send_sems, recv_sems, local_sems = refs[2 * n:]
        x, y, c = lax.axis_index("x"), lax.axis_index("y"), lax.axis_index("c")
        me, sibling = (x, y, c), (x, y, 1 - c)
        chips = [(1 - x, y), (x, 1 - y), (1 - x, 1 - y)]

        def copy(a, k, block, to, src=None):
            slot = outs[a].at[_dev_index(*block)]
            return pltpu.make_async_remote_copy(
                src_ref=slot if src is None else src, dst_ref=slot, send_sem=send_sems.at[a, k], recv_sem=recv_sems.at[a, k],
                device_id=to, device_id_type=MESH)

        mine = [pltpu.make_async_copy(ins[a], outs[a].at[_dev_index(*me)], local_sems.at[a]) for a in range(n)]
        for cp in mine:
            cp.start()
        first = []
        for a in range(n):
            first.append(copy(a, 0, me, sibling, src=ins[a]))
            first += [copy(a, 1 + j, me, (*chip, c), src=ins[a]) for j, chip in enumerate(chips)]
        for cp in first:
            cp.start()
        passed = []
        for j, chip in enumerate(chips):
            for a in range(n):
                copy(a, 1 + j, (*chip, c), me).wait_recv()
                fwd = copy(a, 4 + j, (*chip, c), sibling)
                fwd.start()
                passed.append(fwd)
        for a in range(n):
            copy(a, 0, sibling, me).wait_recv()
            for j, chip in enumerate(chips):
                copy(a, 4 + j, (*chip, 1 - c), me).wait_recv()
        for cp in first + passed:
            cp.wait_send()
        for cp in mine:
            cp.wait()

    return pl.pallas_call(
        body, name=name,
        in_specs=[HBM] * n, out_specs=[HBM] * n,
        out_shape=[jax.ShapeDtypeStruct((N_DEV, *a.shape), a.dtype) for a in arrs],
        scratch_shapes=[pltpu.SemaphoreType.DMA((n, N_DEV - 1)), pltpu.SemaphoreType.DMA((n, N_DEV - 1)), pltpu.SemaphoreType.DMA((n,))],
    )(*arrs)


N_COPIES = {"all_to_all": N_DEV - 1, "gather_chips": 4, "forward": 3}


def _exchange_copies(kind, src_ref, land_ref, send_sems, recv_sems, sending):
    x, y, c = lax.axis_index("x"), lax.axis_index("y"), lax.axis_index("c")
    me = _dev_index(x, y, c)
    others = [(1 - x, y), (x, 1 - y), (1 - x, 1 - y)]
    if kind == "all_to_all":
        flips = [(dx, dy, dc) for dx in (0, 1) for dy in (0, 1) for dc in (0, 1)][1:]
        peers = [(1 - x if dx else x, 1 - y if dy else y, 1 - c if dc else c) for dx, dy, dc in flips]
        plan = [(p, src_ref.at[_dev_index(*p)], me if sending else _dev_index(*p)) for p in peers]
    elif kind == "gather_chips":
        peers = [(x, y, 1 - c)] + [(*o, c) for o in others]
        plan = [(p, src_ref, me if sending else _dev_index(*p)) for p in peers]
    else:
        plan = [((x, y, 1 - c), land_ref.at[_dev_index(*o, c)], _dev_index(*o, c if sending else 1 - c)) for o in others]
    return [pltpu.make_async_remote_copy(src_ref=src, dst_ref=land_ref.at[slot], send_sem=send_sems.at[k], recv_sem=recv_sems.at[k],
                                         device_id=peer, device_id_type=MESH)
            for k, (peer, src, slot) in enumerate(plan)]


def _exchange_start(name, kind, srcs, lands=None):
    if lands is None:
        lands = [lax.empty((N_DEV, *s.shape) if kind == "gather_chips" else s.shape, s.dtype) for s in srcs]
    n = len(lands)
    ops = ([] if srcs is None else list(srcs)) + list(lands)
    m = len(ops)

    def body(*refs):
        src_refs = [None] * n if srcs is None else refs[:n]
        land_refs = refs[m - n:m]
        send_sems, recv_sems, token = refs[m:m + n], refs[m + n:m + 2 * n], refs[-1]
        for a in range(n):
            for cp in _exchange_copies(kind, src_refs[a], land_refs[a], send_sems[a], recv_sems[a], True):
                cp.start()
        token[...] = jnp.zeros_like(token)

    sems = [pltpu.SemaphoreType.DMA((N_COPIES[kind],))] * (2 * n)
    outs = pl.pallas_call(
        body, name=name,
        out_shape=sems + [pltpu.HBM(o.shape, o.dtype) for o in ops] + [jax.ShapeDtypeStruct((8, LANES), F32)],
        in_specs=[HBM] * m,
        out_specs=[SEM] * (2 * n) + [HBM] * m + [pl.BlockSpec(memory_space=pltpu.VMEM)],
        input_output_aliases={i: 2 * n + i for i in range(m)},
        compiler_params=SIDE_EFFECT,
    )(*[pltpu.with_memory_space_constraint(o, pltpu.HBM) for o in ops])
    thru = outs[2 * n:2 * n + m]
    return outs[:n], outs[n:2 * n], (None if srcs is None else thru[:n]), thru[m - n:], outs[-1]


def _exchange_wait(name, kind, send_sems, recv_sems, srcs, lands, after):
    n = len(lands)
    ops = ([] if srcs is None else list(srcs)) + list(lands)
    m = len(ops)

    def body(*refs):
        src_refs = [None] * n if srcs is None else refs[:n]
        land_refs = refs[m - n:m]
        send_refs, recv_refs = refs[m:m + n], refs[m + n:m + 2 * n]
        for a in range(n):
            for cp in _exchange_copies(kind, src_refs[a], land_refs[a], send_refs[a], recv_refs[a], False):
                cp.wait_send()
                cp.wait_recv()

    outs = pl.pallas_call(
        body, name=name,
        out_shape=[pltpu.HBM(o.shape, o.dtype) for o in ops],
        in_specs=[HBM] * m + [SEM] * (2 * n) + [pl.BlockSpec(memory_space=pl.ANY)],
        out_specs=[HBM] * m,
        input_output_aliases={i: i for i in range(m)},
        compiler_params=SIDE_EFFECT,
    )(*ops, *send_sems, *recv_sems, after)
    return (None if srcs is None else outs[:n]), outs[m - n:]


def _with_own(land, own, me):
    return lax.dynamic_update_slice_in_dim(land, own, me, 0)


def _shards_to_cols(g):
    return jnp.transpose(g, (1, 0, 2)).reshape(g.shape[1], N_DEV * g.shape[2])


def _cols_to_shards(a):
    R, Ctot = a.shape
    return jnp.transpose(a.reshape(R, N_DEV, Ctot // N_DEV), (1, 0, 2))


def kernel(x, c, ctx, c_ctx, norm_attn_w, norm_mlp_w, w_ada, b_ada, w_in, attn_sink, pool_w, pool_scale, w_out, w_mlp_up, w_mlp_down, final_norm_w, loss_target, m_c_ctx, m_norm_attn_w, m_norm_mlp_w, m_w_ada, m_b_ada, m_w_in, m_attn_sink, m_pool_w, m_pool_scale, m_w_out, m_w_mlp_up, m_w_mlp_down, m_final_norm_w, v_c_ctx, v_norm_attn_w, v_norm_mlp_w, v_w_ada, v_b_ada, v_w_in, v_attn_sink, v_pool_w, v_pool_scale, v_w_out, v_w_mlp_up, v_w_mlp_down, v_final_norm_w):
    _, L, D = x.shape
    H = attn_sink.shape[1]
    A = H * HEAD_DIM
    KV = A // GQA
    P = pool_scale.shape[1]
    MODW = N_MOD * D
    ws = MODW // N_DEV
    gd = P // POOL_GROUPS
    me = _dev_index(lax.axis_index("x"), lax.axis_index("y"), lax.axis_index("c"))

    x2d, ctx2d, tgt = x[0], ctx[0], loss_target[0]
    cctx_row = c_ctx.reshape(1, D)
    wf_row = final_norm_w.reshape(1, D)
    w_ada_l = w_ada[0]
    pool_w_l = pool_w[0].reshape(POOL_GROUPS * (gd // N_DEV), gd)

    (c_all,) = _all_gather("gather_cond", [c])
    cond = jnp.concatenate([c_all[:, 0, :], cctx_row, jnp.zeros((COND_ROWS - N_DEV - 1, D), F32)], axis=0)
    b_sh = lax.dynamic_slice_in_dim(b_ada, me * ws, ws, axis=1)
    (mods_sh,) = _mm("ada_mod", cond, w_ada_l, "nn", [F32], SMALL_TILES, a_pre=_silu, extras=[("n", b_sh)], epilogue=lambda acc, b: (acc + b,))
    (mods_g,) = _all_gather("gather_mods", [mods_sh])

    w_srcs = [w_in[0].astype(BF16), w_out[0].astype(BF16), pool_w_l.astype(BF16), w_mlp_up[0].astype(BF16), w_mlp_down[0].astype(BF16)]
    w_srcs, mods_g = lax.optimization_barrier((w_srcs, mods_g))
    gather_start = _exchange_start("gather_weights_start", "gather_chips", w_srcs)

    def weights(tag, started, lo, hi, after_chips, after_forward):
        gw_send, gw_recv, gw_src, gw_land, _ = started
        mine, lands = _exchange_wait(f"gather_{tag}_wait", "gather_chips", gw_send[lo:hi], gw_recv[lo:hi], gw_src[lo:hi],
                                     gw_land[lo:hi], after_chips)
        f_send, f_recv, _, f_land, f_token = _exchange_start(f"forward_{tag}_start", "forward", None, lands)
        _, lands = _exchange_wait(f"forward_{tag}_wait", "forward", f_send, f_recv, None, f_land,
                                  f_token if after_forward is None else after_forward)
        return [_with_own(l, s[None], me) for l, s in zip(lands, mine)]

    mods = _shards_to_cols(mods_g)
    mod_b = lax.dynamic_slice_in_dim(mods, me, 1, axis=0)
    sh_a, sc_a, g_a, sh_m, sc_m, g_m = [mod_b[:, i * D:(i + 1) * D] for i in range(N_MOD)]
    csh_a, csc_a = mods[N_DEV:N_DEV + 1, :D], mods[N_DEV:N_DEV + 1, D:2 * D]

    cos, sin = _rope_tables(L)
    h = _norm_fwd("norm_attn", x2d, norm_attn_w, sc_a, sh_a)
    hc = _norm_fwd("norm_attn_ctx", ctx2d, norm_attn_w, csc_a, csh_a)
    (win_g,) = weights("w_in", gather_start, 0, 1, h, None)
    W_in = _shards_to_cols(win_g)
    W_qk, W_kv = W_in[:, :A + KV], W_in[:, A:A + 2 * KV]
    W_uv = jnp.concatenate([W_in[:, A + 2 * KV:], W_in[:, A + KV:A + 2 * KV]], axis=1)
    qk, uv = _in_proj(h, W_qk, W_uv, cos, sin)
    (kvc,) = _mm("in_proj_ctx", hc, W_kv, "nn", [BF16], SMALL_TILES)
    attn, lse = _attn_fwd(qk, uv, kvc, attn_sink, A, KV, P)
    wout_g, pw_g = weights("w_out", gather_start, 1, 3, qk, attn)
    W_out = wout_g.reshape(A + P, D)
    PW = jnp.transpose(pw_g.reshape(N_DEV, POOL_GROUPS, gd // N_DEV, gd), (1, 0, 2, 3)).reshape(POOL_GROUPS, gd, gd)
    ap, pooled = _pool_fwd(uv, PW, pool_scale, attn)
    o, x1, hm = _mm("out_proj_norm", ap, W_out, "nn", [F32, F32, BF16], (256, D, D), chunk=128, epilogue=_out_proj_epilogue,
                    extras=[("mn", x2d), ("n", g_a), ("n", norm_mlp_w), ("n", sc_m), ("n", sh_m)])
    (W_up,) = weights("w_up", gather_start, 3, 4, attn, x1)
    up, act = _mm("mlp_up", hm, W_up, "nn", [F32, BF16], (1024, 1024, 2048), epilogue=lambda acc: (acc, _relu2(acc)), b_shards=True)
    (wdown_g,) = weights("w_down", gather_start, 4, 5, up, None)
    W_down = wdown_g.reshape(-1, D)
    d_x2, d_mlp, d_wf, d_gm, loss_row = _mm(
        "mlp_down_loss", act, W_down, "nn", [F32, BF16], (512, D, 1024), chunk=128, n_sums=3, vmem=FUSED_VMEM_LIMIT,
        epilogue=_mlp_down_epilogue, extras=[("mn", x1), ("mn", tgt), ("n", g_m), ("n", wf_row)])
    loss_p = loss_row[:, :1]

    (d_up,) = _mm("mlp_down_bwd_act", d_mlp, W_down, "nt", [BF16], (1024, 1024, 2048), extras=[("mn", up)],
                  epilogue=lambda acc, uu: (acc * (2.0 * jnp.maximum(uu, 0.0)),))
    (gW_down,) = _mm("mlp_down_bwd_w", act, d_mlp, "tn", [BF16], (1024, 2048, 2048), vmem=FUSED_VMEM_LIMIT)
    (gW_up_s,) = _mm("mlp_up_bwd_w", hm, d_up, "tn", [BF16], (2048, 1024, 2048), out_shards=True, vmem=FUSED_VMEM_LIMIT)
    g_mlp_srcs = [gW_up_s, gW_down.reshape(N_DEV, -1, D)]
    g_mlp = _exchange_start("grads_mlp_start", "all_to_all", g_mlp_srcs)
    (d_hm,) = _mm("mlp_up_bwd_act", d_up, W_up, "nt", [F32], (1024, D, 1024), b_shards=True)

    d_x1, d_o, d_attn, d_pool, s_sh_m, s_sc_m, s_w_nm, d_ga = _norm_bwd_out_proj_bwd(
        x1, d_hm, d_x2, o, norm_mlp_w, sc_m + g_mlp[4][0, 0], g_a, W_out.T, A)
    (gW_out,) = _mm("out_proj_bwd_w", ap, d_o, "tn", [BF16], (1024, 2048, 2048))
    d_pooled, gPW, d_pscale = _pool_bwd_mix(d_pool, pooled, PW, pool_scale)
    gpw_s = jnp.transpose(gPW.astype(BF16).reshape(POOL_GROUPS, N_DEV, gd // N_DEV, gd), (1, 0, 2, 3)).reshape(N_DEV, -1, gd)
    g_mix_srcs = [gW_out.reshape(N_DEV, (A + P) // N_DEV, D), gpw_s]
    g_mix = _exchange_start("grads_mix_start", "all_to_all", g_mix_srcs)
    lse = lse + g_mix[4][0, 0]
    d_u = _pool_bwd_window(d_pooled)
    d_q, rd, dsink_q, d_kvc = _attn_bwd_dq(qk, uv, kvc, attn_sink, d_attn, lse, cos, sin, A, KV, P)
    d_k, d_v = _attn_bwd_dkv(qk, uv, d_attn, lse, rd, cos, sin, A, KV, P)
    d_sink = _sum_lanes("sink_grad", dsink_q).reshape(1, H)
    d_p = jnp.concatenate([d_q, d_k, d_v, d_u], axis=1)
    d_kvc_b = d_kvc.astype(BF16)
    (gW_kv_ctx,) = _mm("in_proj_ctx_bwd_w", hc, d_kvc_b, "tn", [F32], SMALL_TILES)
    (d_hc,) = _mm("in_proj_ctx_bwd_act", d_kvc_b, W_kv, "nt", [F32], SMALL_TILES)
    gW_in_init = jnp.pad(gW_kv_ctx, ((0, 0), (A, P)))
    (gW_in,) = _mm("in_proj_bwd_w", h, d_p, "tn", [BF16], (1024, 1280, 2048), extras=[("mn", gW_in_init)], epilogue=lambda acc, init: (acc + init,))
    g_in_srcs = [_cols_to_shards(gW_in)]
    g_in = _exchange_start("grads_in_start", "all_to_all", g_in_srcs)
    grad_x, s_sh_a, s_sc_a, s_w_na = _mm(
        "in_proj_bwd_norm", d_p, W_in, "nt", [F32], (256, D, A + 2 * KV + P), chunk=128, n_sums=3,
        epilogue=lambda acc, xr, dres, w, sc: _modulated_norm_bwd(xr, acc, dres, w, sc),
        extras=[("mn", x2d), ("mn", d_x1), ("n", norm_attn_w), ("n", sc_a + g_in[4][0, 0])])
    s_csh, s_csc, s_w_na = _norm_bwd_sums("norm_attn_ctx_bwd", ctx2d, d_hc, norm_attn_w, csc_a, s_w_na)

    pad_l = lambda a: jnp.pad(a, ((0, 0), (0, LANES - a.shape[1])))
    d_mod_b = jnp.concatenate([s_sh_a, s_sc_a, d_ga, s_sh_m, s_sc_m, d_gm], axis=1)
    summed = jnp.concatenate([s_csh, s_csc, s_w_na, s_w_nm, d_wf, d_pscale, pad_l(d_sink), pad_l(loss_p)], axis=1)
    (small_g,) = _all_gather("gather_small", [jnp.concatenate([d_mod_b, summed], axis=1)])
    small_g = small_g[:, 0, :]
    tot = _sum_rows("small_sum", small_g[:, MODW:])
    off = [0]
    for wdt in (D, D, D, D, D, P, LANES, LANES):
        off.append(off[-1] + wdt)
    seg = lambda i: tot[:, off[i]:off[i + 1]]
    g_norm_attn, g_norm_mlp, g_final, g_pscale = seg(2), seg(3), seg(4), seg(5)
    g_sink, loss = seg(6)[:, :H], seg(7)[0, 0]
    d_mod_ctx = jnp.concatenate([seg(0), seg(1), jnp.zeros((1, MODW - 2 * D), F32)], axis=1)
    d_mod = jnp.concatenate([small_g[:, :MODW], d_mod_ctx, jnp.zeros((COND_ROWS - N_DEV - 1, MODW), F32)], axis=0)
    g_b_ada = _sum_rows("b_ada_grad", d_mod[:N_DEV + 1])
    d_mod_sh = lax.dynamic_slice_in_dim(d_mod, me * ws, ws, axis=1)
    (g_w_ada,) = _mm("ada_bwd_w", cond, d_mod_sh, "tn", [F32], SMALL_TILES, a_pre=_silu)
    (d_cond_p,) = _mm("ada_bwd_cond", d_mod_sh, w_ada_l, "nt", [F32], SMALL_TILES)
    (d_cctx_g,) = _all_gather("gather_cctx", [d_cond_p[N_DEV:N_DEV + 1]])
    g_c_ctx = _silu_grad_mul(cctx_row, _sum_rows("cctx_sum", d_cctx_g[:, 0, :]))

    def arrived(name, started):
        srcs, lands = _exchange_wait(name, "all_to_all", started[0], started[1], started[2], started[3], g_c_ctx)
        return [_with_own(l, lax.dynamic_index_in_dim(s, me, 0, keepdims=True), me) for l, s in zip(lands, srcs)]

    r_up, r_down = arrived("grads_mlp_wait", g_mlp)
    r_out, r_pw = arrived("grads_mix_wait", g_mix)
    (r_in,) = arrived("grads_in_wait", g_in)

    results = {
        "c_ctx": _adamw("adam_c_ctx", cctx_row, g_c_ctx, m_c_ctx.reshape(1, D), v_c_ctx.reshape(1, D)),
        "norm_attn_w": _adamw("adam_norm_attn", norm_attn_w, g_norm_attn, m_norm_attn_w, v_norm_attn_w),
        "norm_mlp_w": _adamw("adam_norm_mlp", norm_mlp_w, g_norm_mlp, m_norm_mlp_w, v_norm_mlp_w),
        "w_ada": _adamw("adam_w_ada", w_ada_l, g_w_ada, m_w_ada[0], v_w_ada[0]),
        "b_ada": _adamw("adam_b_ada", b_ada, g_b_ada, m_b_ada, v_b_ada),
        "w_in": _adamw("adam_w_in", w_in[0], r_in, m_w_in[0], v_w_in[0]),
        "attn_sink": _adamw("adam_sink", attn_sink, g_sink, m_attn_sink, v_attn_sink),
        "pool_w": _adamw("adam_pool_w", pool_w_l, r_pw, m_pool_w[0].reshape(pool_w_l.shape), v_pool_w[0].reshape(pool_w_l.shape)),
        "pool_scale": _adamw("adam_pool_scale", pool_scale, g_pscale, m_pool_scale, v_pool_scale),
        "w_out": _adamw("adam_w_out", w_out[0], r_out, m_w_out[0], v_w_out[0]),
        "w_mlp_up": _adamw("adam_w_up", w_mlp_up[0], r_up, m_w_mlp_up[0], v_w_mlp_up[0]),
        "w_mlp_down": _adamw("adam_w_down", w_mlp_down[0], r_down, m_w_mlp_down[0], v_w_mlp_down[0]),
        "final_norm_w": _adamw("adam_final_norm", wf_row, g_final, m_final_norm_w.reshape(1, D), v_final_norm_w.reshape(1, D)),
    }
    shapes = {"c_ctx": c_ctx.shape, "norm_attn_w": norm_attn_w.shape, "norm_mlp_w": norm_mlp_w.shape, "w_ada": w_ada.shape,
              "b_ada": b_ada.shape, "w_in": w_in.shape, "attn_sink": attn_sink.shape, "pool_w": pool_w.shape,
              "pool_scale": pool_scale.shape, "w_out": w_out.shape, "w_mlp_up": w_mlp_up.shape, "w_mlp_down": w_mlp_down.shape,
              "final_norm_w": final_norm_w.shape}
    outs = [loss, grad_x.reshape(x.shape)]
    for part in range(4):
        outs += [results[name][part].reshape(shape) for name, shape in shapes.items()]
    return tuple(outs)
```

```python
import jax
import jax.numpy as jnp
import numpy as np
from jax import lax
from jax.experimental import pallas as pl
from jax.experimental.pallas import tpu as pltpu

F32 = jnp.float32
BF16 = jnp.bfloat16
I32 = jnp.int32

HEAD_DIM = 64
GQA = 4
BLOCK = 128
GRID_W = 64
ROPE_BASE = 10000.0
POOL_WINDOWS = (2, 4, 8, 16)
POOL_GROUPS = len(POOL_WINDOWS)
HALO = 8
N_MOD = 6
EPS = 1e-6
NEG_INF = -1e30
ADAM_LR = 0.001
ADAM_B1 = 0.9
ADAM_B2 = 0.999
ADAM_EPS = 1e-08
ADAM_WD = 0.01
ADAM_STEP = 10
N_DEV = 8
COND_ROWS = 2 * N_DEV
LANES = 128
SUBLANES_16BIT = 16
VMEM_LIMIT = 48 * 1024 * 1024
FUSED_VMEM_LIMIT = 56 * 1024 * 1024
SMALL_TILES = (512, 1024, 512)
MESH = pl.DeviceIdType.MESH
HBM = pl.BlockSpec(memory_space=pltpu.HBM)
SEM = pl.BlockSpec(memory_space=pltpu.SEMAPHORE)
SIDE_EFFECT = pltpu.CompilerParams(has_side_effects=pltpu.SideEffectType.DATAFLOW_SIDE_EFFECTING)


def _cparams(*sem):
    return pltpu.CompilerParams(dimension_semantics=sem, vmem_limit_bytes=VMEM_LIMIT)


def _tile(n, pref, align):
    if n <= pref:
        return n
    t = (pref // align) * align
    while t >= align:
        if n % t == 0:
            return t
        t -= align
    return n


def _dot(a, b):
    return lax.dot_general(a, b, (((1,), (0,)), ((), ())), preferred_element_type=F32)


def _dot_nt(a, b):
    return lax.dot_general(a, b, (((1,), (1,)), ((), ())), preferred_element_type=F32)


def _dot_tn(a, b):
    return lax.dot_general(a, b, (((0,), (0,)), ((), ())), preferred_element_type=F32)


_DOTS = {"nn": _dot, "nt": _dot_nt, "tn": _dot_tn}


def _mm(name, a, b, mode, out_dtypes, tiles, *, epilogue=None, extras=(), a_pre=None, n_sums=0, chunk=None,
        b_shards=False, out_shards=False, vmem=VMEM_LIMIT):
    if mode == "nn":
        M, K = a.shape
        K2, N = (b.shape[1], N_DEV * b.shape[2]) if b_shards else b.shape
    elif mode == "nt":
        M, K = a.shape
        N, K2 = (b.shape[1], N_DEV * b.shape[2]) if b_shards else b.shape
    else:
        (K, M), (K2, N) = a.shape, b.shape
    assert K == K2 and not (b_shards and mode == "tn"), (name, a.shape, b.shape)
    n_span = N // N_DEV if out_shards or (b_shards and mode == "nn") else N
    k_span = K // N_DEV if b_shards and mode == "nt" else K
    tm = _tile(M, tiles[0], LANES if mode == "tn" else SUBLANES_16BIT)
    tn = _tile(n_span, tiles[1], LANES)
    tk = _tile(k_span, tiles[2], SUBLANES_16BIT if mode == "tn" else LANES)
    nk, nb, kb = K // tk, n_span // tn, k_span // tk
    rows = tm if chunk is None else min(chunk, tm)
    n_ex, n_out = len(extras), len(out_dtypes)
    use_acc = nk > 1 or rows < tm
    assert n_sums == 0 or N == tn, name

    def product(a_ref, b_ref):
        at = a_ref[...]
        if a_pre is not None:
            at = a_pre(at)
        return _DOTS[mode](at.astype(BF16), b_ref[...].astype(BF16))

    def apply(acc, ex, out_refs, sl):
        res = (acc,) if epilogue is None else epilogue(acc, *ex)
        for o_ref, o in zip(out_refs, res[:n_out]):
            o_ref[sl, :] = o.astype(o_ref.dtype)
        return tuple(res[n_out:])

    def finish(acc, ex_refs, out_refs, sum_refs):
        if rows == tm:
            acc = acc if not use_acc else acc[...]
            sums = apply(acc, [r[...] for r in ex_refs], out_refs, slice(None))
        else:
            def one(ci, sums):
                sl = pl.ds(pl.multiple_of(ci * rows, rows), rows)
                ex = [r[...] if kind == "n" else r[sl, :] for (kind, _), r in zip(extras, ex_refs)]
                return tuple(s + v for s, v in zip(sums, apply(acc[sl, :], ex, out_refs, sl)))
            sums = lax.fori_loop(0, tm // rows, one, tuple(jnp.zeros((1, tn), F32) for _ in range(n_sums)))
        first = pl.program_id(0) == 0
        for s_ref, sv in zip(sum_refs, sums):
            @pl.when(first)
            def _(s_ref=s_ref, sv=sv):
                s_ref[...] = sv

            @pl.when(jnp.logical_not(first))
            def _(s_ref=s_ref, sv=sv):
                s_ref[...] += sv

    def body(a_ref, b_ref, *rest):
        ex_refs, out_refs = rest[:n_ex], rest[n_ex:n_ex + n_out]
        sum_refs = rest[n_ex + n_out:n_ex + n_out + n_sums]
        if not use_acc:
            finish(product(a_ref, b_ref), ex_refs, out_refs, sum_refs)
            return
        acc_ref = rest[-1]
        k = pl.program_id(2)

        @pl.when(k == 0)
        def _():
            acc_ref[...] = product(a_ref, b_ref)

        @pl.when(k > 0)
        def _():
            acc_ref[...] += product(a_ref, b_ref)

        @pl.when(k == nk - 1)
        def _():
            finish(acc_ref, ex_refs, out_refs, sum_refs)

    a_spec = pl.BlockSpec((tk, tm), lambda i, j, k: (k, i)) if mode == "tn" else pl.BlockSpec((tm, tk), lambda i, j, k: (i, k))
    if not b_shards:
        b_spec = pl.BlockSpec((tn, tk), lambda i, j, k: (j, k)) if mode == "nt" else pl.BlockSpec((tk, tn), lambda i, j, k: (k, j))
    elif mode == "nn":
        b_spec = pl.BlockSpec((None, tk, tn), lambda i, j, k: (j // nb, k, j % nb))
    else:
        b_spec = pl.BlockSpec((None, tn, tk), lambda i, j, k: (k // kb, j, k % kb))
    ex_specs = []
    for kind, arr in extras:
        if kind == "mn":
            ex_specs.append(pl.BlockSpec((tm, tn), lambda i, j, k: (i, j)))
        elif kind == "n":
            ex_specs.append(pl.BlockSpec((1, tn), lambda i, j, k: (0, j)))
        else:
            ex_specs.append(pl.BlockSpec((tm, arr.shape[1]), lambda i, j, k: (i, 0)))
    if out_shards:
        out_specs = [pl.BlockSpec((None, tm, tn), lambda i, j, k: (j // nb, i, j % nb)) for _ in out_dtypes]
        out_shape = [jax.ShapeDtypeStruct((N_DEV, M, n_span), d) for d in out_dtypes]
    else:
        out_specs = [pl.BlockSpec((tm, tn), lambda i, j, k: (i, j)) for _ in out_dtypes]
        out_shape = [jax.ShapeDtypeStruct((M, N), d) for d in out_dtypes]
    out_specs += [pl.BlockSpec((1, tn), lambda i, j, k: (0, 0))] * n_sums
    out_shape += [jax.ShapeDtypeStruct((1, N), F32)] * n_sums
    return pl.pallas_call(
        body,
        name=name,
        grid=(M // tm, N // tn, nk),
        in_specs=[a_spec, b_spec] + ex_specs,
        out_specs=out_specs,
        out_shape=out_shape,
        scratch_shapes=[pltpu.VMEM((tm, tn), F32)] if use_acc else [],
        compiler_params=pltpu.CompilerParams(
            dimension_semantics=("arbitrary",) * 3 if n_sums else ("parallel", "parallel", "arbitrary"), vmem_limit_bytes=vmem),
    )(a, b, *[arr for _, arr in extras])


def _silu(v):
    return v / (1.0 + jnp.exp(-v))


def _relu2(v):
    r = jnp.maximum(v, 0.0)
    return r * r


def _rope_tables(L):
    half = HEAD_DIM // 2
    inv_freq = np.float32(ROPE_BASE) ** (-np.arange(0, half, 2, dtype=np.float32) / np.float32(half))
    t = np.arange(L)
    row, col = t // GRID_W, t % GRID_W
    ang_r = row.astype(np.float32)[:, None] * inv_freq[None, :]
    ang_c = col.astype(np.float32)[:, None] * inv_freq[None, :]
    cos = np.concatenate([np.cos(ang_r), np.cos(ang_r), np.cos(ang_c), np.cos(ang_c)], axis=1)
    sin = np.concatenate([-np.sin(ang_r), np.sin(ang_r), -np.sin(ang_c), np.sin(ang_c)], axis=1)
    reps = LANES // HEAD_DIM
    return jnp.asarray(np.tile(cos, (1, reps)), F32), jnp.asarray(np.tile(sin, (1, reps)), F32)


def _rope(xf, cos, sin):
    quarter = HEAD_DIM // 4
    lane = lax.broadcasted_iota(I32, (xf.shape[0], LANES), 1)
    first = (lane & quarter) == 0
    outs = []
    for j in range(xf.shape[1] // LANES):
        xc = xf[:, j * LANES:(j + 1) * LANES]
        partner = jnp.where(first, pltpu.roll(xc, LANES - quarter, 1), pltpu.roll(xc, quarter, 1))
        outs.append(xc * cos + partner * sin)
    return outs[0] if len(outs) == 1 else jnp.concatenate(outs, axis=1)


def _inv_rms(xf):
    return lax.rsqrt(jnp.mean(xf * xf, axis=-1, keepdims=True) + EPS)


def _modulated_norm(xf, w, sc, sh):
    return ((xf * _inv_rms(xf)) * w) * (1.0 + sc) + sh


def _modulated_norm_bwd(xf, dh, dres, w, sc):
    r = _inv_rms(xf)
    xh = xf * r
    dn = dh * (1.0 + sc)
    dxh = dn * w
    dx = dres + r * (dxh - xh * jnp.mean(dxh * xh, axis=-1, keepdims=True))
    col = lambda v: jnp.sum(v, axis=0, keepdims=True)
    return dx, col(dh), col(dh * (xh * w)), col(dn * xh)


def _in_proj(h, w_qk, w_uv, cos, sin):
    L, D = h.shape
    T = _tile(L, 512, SUBLANES_16BIT)

    def body(h_ref, wqk_ref, wuv_ref, cos_ref, sin_ref, qk_ref, uv_ref):
        hv = h_ref[...]
        qk_ref[...] = _rope(_dot(hv, wqk_ref[...]), cos_ref[...], sin_ref[...]).astype(BF16)
        uv_ref[...] = _dot(hv, wuv_ref[...])

    tile = lambda wd: pl.BlockSpec((T, wd), lambda i: (i, 0))
    whole = lambda a: pl.BlockSpec(a.shape, lambda i: (0, 0))
    return pl.pallas_call(
        body, name="in_proj", grid=(L // T,),
        in_specs=[tile(D), whole(w_qk), whole(w_uv), tile(LANES), tile(LANES)],
        out_specs=[tile(w_qk.shape[1]), tile(w_uv.shape[1])],
        out_shape=[jax.ShapeDtypeStruct((L, w_qk.shape[1]), BF16), jax.ShapeDtypeStruct((L, w_uv.shape[1]), F32)],
        compiler_params=_cparams("parallel"),
    )(h, w_qk, w_uv, cos, sin)


def _norm_fwd(name, x, w, sc, sh):
    L, D = x.shape
    T = _tile(L, 512, 8)

    def body(x_ref, w_ref, sc_ref, sh_ref, h_ref):
        h_ref[...] = _modulated_norm(x_ref[...], w_ref[...], sc_ref[...], sh_ref[...]).astype(BF16)

    row = pl.BlockSpec((1, D), lambda i: (0, 0))
    return pl.pallas_call(
        body, name=name, grid=(L // T,),
        in_specs=[pl.BlockSpec((T, D), lambda i: (i, 0)), row, row, row],
        out_specs=pl.BlockSpec((T, D), lambda i: (i, 0)),
        out_shape=jax.ShapeDtypeStruct((L, D), BF16),
        compiler_params=_cparams("parallel"),
    )(x, w, sc, sh)


def _norm_bwd_sums(name, x, dh, w, sc, w_init):
    L, D = x.shape
    T = _tile(L, 256, 8)

    def body(x_ref, dh_ref, w_ref, sc_ref, wi_ref, ssh_ref, ssc_ref, sw_ref):
        @pl.when(pl.program_id(0) == 0)
        def _():
            ssh_ref[...] = jnp.zeros_like(ssh_ref)
            ssc_ref[...] = jnp.zeros_like(ssc_ref)
            sw_ref[...] = wi_ref[...]

        dh = dh_ref[...]
        _, s_sh, s_sc, s_w = _modulated_norm_bwd(x_ref[...], dh, jnp.zeros_like(dh), w_ref[...], sc_ref[...])
        ssh_ref[...] += s_sh
        ssc_ref[...] += s_sc
        sw_ref[...] += s_w

    tile = pl.BlockSpec((T, D), lambda i: (i, 0))
    row = pl.BlockSpec((1, D), lambda i: (0, 0))
    return pl.pallas_call(
        body, name=name, grid=(L // T,), in_specs=[tile, tile, row, row, row], out_specs=[row, row, row],
        out_shape=[jax.ShapeDtypeStruct((1, D), F32)] * 3, compiler_params=_cparams("arbitrary"),
    )(x, dh, w, sc, w_init)


def _norm_bwd_out_proj_bwd(x, dh, dres, o, w, sc, g, w_out_t, A):
    L, D = x.shape
    N = w_out_t.shape[1]
    T = _tile(L, 256, SUBLANES_16BIT)
    half = T // 2

    def body(x_ref, dh_ref, dres_ref, o_ref, w_ref, sc_ref, g_ref, wt_ref, dx_ref, do_ref, dattn_ref, dpool_ref,
             ssh_ref, ssc_ref, sw_ref, sg_ref):
        @pl.when(pl.program_id(0) == 0)
        def _():
            for s_ref in (ssh_ref, ssc_ref, sw_ref, sg_ref):
                s_ref[...] = jnp.zeros_like(s_ref)

        for rows in (slice(0, half), slice(half, T)):
            dx, s_sh, s_sc, s_w = _modulated_norm_bwd(x_ref[rows, :], dh_ref[rows, :], dres_ref[rows, :], w_ref[...], sc_ref[...])
            ssh_ref[...] += s_sh
            ssc_ref[...] += s_sc
            sw_ref[...] += s_w
            sg_ref[...] += jnp.sum(dx * o_ref[rows, :], axis=0, keepdims=True)
            dx_ref[rows, :] = dx
            do_ref[rows, :] = (g_ref[...] * dx).astype(BF16)
        dap = _dot(do_ref[...], wt_ref[...])
        dattn_ref[...] = dap[:, :A].astype(BF16)
        dpool_ref[...] = dap[:, A:]

    tile = pl.BlockSpec((T, D), lambda i: (i, 0))
    row = pl.BlockSpec((1, D), lambda i: (0, 0))
    return pl.pallas_call(
        body, name="norm_mlp_bwd_out_proj_bwd", grid=(L // T,),
        in_specs=[tile, tile, tile, tile, row, row, row, pl.BlockSpec((D, N), lambda i: (0, 0))],
        out_specs=[tile, tile, pl.BlockSpec((T, A), lambda i: (i, 0)), pl.BlockSpec((T, N - A), lambda i: (i, 0)), row, row, row, row],
        out_shape=[jax.ShapeDtypeStruct((L, D), F32), jax.ShapeDtypeStruct((L, D), BF16), jax.ShapeDtypeStruct((L, A), BF16),
                   jax.ShapeDtypeStruct((L, N - A), F32)] + [jax.ShapeDtypeStruct((1, D), F32)] * 4,
        compiler_params=pltpu.CompilerParams(dimension_semantics=("arbitrary",), vmem_limit_bytes=FUSED_VMEM_LIMIT),
    )(x, dh, dres, o, w, sc, g, w_out_t)


def _out_proj_epilogue(acc, xr, g, w, sc, sh):
    x1 = xr + g * acc
    return acc, x1, _modulated_norm(x1, w, sc, sh)


def _mlp_down_epilogue(acc, x1, tgt, g, wf):
    D = acc.shape[1]
    x2 = x1 + g * acc
    r = _inv_rms(x2)
    xh = x2 * r
    err = xh * wf - tgt
    loss = 0.5 * jnp.sum(jnp.mean(err * err, axis=-1, keepdims=True), axis=0, keepdims=True)
    dy = err * (1.0 / D)
    dxh = dy * wf
    dx = r * (dxh - xh * jnp.mean(dxh * xh, axis=-1, keepdims=True))
    col = lambda v: jnp.sum(v, axis=0, keepdims=True)
    return dx, g * dx, col(dy * xh), col(dx * acc), jnp.broadcast_to(loss, (1, D))


def _heads(ref, first, n):
    return jnp.concatenate([ref[:, (first + g) * HEAD_DIM:(first + g + 1) * HEAD_DIM] for g in range(n)], axis=0)


def _edge_variants(masks):
    return jnp.asarray(np.stack([np.where(masks(first, last), 0.0, NEG_INF).astype(np.float32)
                                 for last in (False, True) for first in (False, True)]))


def _attn_bias(C):
    kj = np.arange(3 * BLOCK + C)[:, None]
    qi = (np.arange(GQA * BLOCK) % BLOCK)[None, :]

    def masks(first, last):
        window = (kj >= qi) & (kj <= qi + 2 * BLOCK) & (kj >= (BLOCK if first else 0)) & (kj < (2 * BLOCK if last else 3 * BLOCK))
        return window | (kj >= 3 * BLOCK)

    return _edge_variants(masks)


def _attn_bias_keys():
    kj = np.arange(BLOCK)[:, None]
    col = np.arange(3 * GQA * BLOCK)[None, :]
    part, qi = col // (GQA * BLOCK), col % BLOCK

    def masks(first, last):
        return ((part == 0) & (kj <= qi) & (not first)) | (part == 1) | ((part == 2) & (kj >= qi) & (not last))

    return _edge_variants(masks)


def _edge_index(n, nb):
    return (n == 0).astype(I32) + 2 * (n == nb - 1).astype(I32)


def _head_rows(ref, hk):
    return jnp.concatenate([ref[hk * GQA + g:hk * GQA + g + 1, :] for g in range(GQA)], axis=1)


def _rows_to_heads(rows_by_kv_head):
    return jnp.concatenate([r[:, g * BLOCK:(g + 1) * BLOCK] for r in rows_by_kv_head for g in range(GQA)], axis=0)


def _queries_to_rows(t):
    return jnp.concatenate([t[:, g * BLOCK:(g + 1) * BLOCK].T for g in range(GQA)], axis=1)


def _attn_specs(L, A, KV, C, vcol):
    nb = L // BLOCK
    kcol = A // KV
    prev = lambda n: jnp.maximum(n - 1, 0)
    nxt = lambda n: jnp.minimum(n + 1, nb - 1)
    q_spec = pl.BlockSpec((BLOCK, A), lambda n: (n, 0))
    k_specs = [pl.BlockSpec((BLOCK, KV), lambda n: (prev(n), kcol)), pl.BlockSpec((BLOCK, KV), lambda n: (n, kcol)),
               pl.BlockSpec((BLOCK, KV), lambda n: (nxt(n), kcol))]
    v_specs = [pl.BlockSpec((BLOCK, KV), lambda n: (prev(n), vcol)), pl.BlockSpec((BLOCK, KV), lambda n: (n, vcol)),
               pl.BlockSpec((BLOCK, KV), lambda n: (nxt(n), vcol))]
    kvc_spec = pl.BlockSpec((C, 2 * KV), lambda n: (0, 0))
    return q_spec, k_specs, v_specs, kvc_spec


def _keys_values(hk, k_refs, v_refs, kvc_ref, KV):
    sl = slice(hk * HEAD_DIM, (hk + 1) * HEAD_DIM)
    keys = jnp.concatenate([r[:, sl] for r in k_refs] + [kvc_ref[:, sl]], axis=0)
    vals = jnp.concatenate([r[:, sl].astype(BF16) for r in v_refs] + [kvc_ref[:, KV + hk * HEAD_DIM:KV + (hk + 1) * HEAD_DIM]], axis=0)
    return keys, vals


def _sink_row(sink_ref, hk):
    return jnp.concatenate([jnp.full((1, BLOCK), sink_ref[0, hk * GQA + g], F32) for g in range(GQA)], axis=1)


def _attn_fwd(qk, uv, kvc, sink, A, KV, P):
    L = qk.shape[0]
    C = kvc.shape[0]
    nkv = KV // HEAD_DIM
    H = nkv * GQA
    scale = HEAD_DIM ** -0.5

    def body(sink_ref, q_ref, kp_ref, kc_ref, kn_ref, vp_ref, vc_ref, vn_ref, kvc_ref, bias_ref, o_ref, lse_ref):
        bias = bias_ref[_edge_index(pl.program_id(0), L // BLOCK)]
        lse_rows = []
        for hk in range(nkv):
            keys, vals = _keys_values(hk, (kp_ref, kc_ref, kn_ref), (vp_ref, vc_ref, vn_ref), kvc_ref, KV)
            qs = _heads(q_ref, hk * GQA, GQA) * scale
            s = _dot_nt(keys, qs) + bias
            sk = _sink_row(sink_ref, hk)
            m = jnp.maximum(jnp.max(s, axis=0, keepdims=True), sk)
            p = jnp.exp(s - m)
            den = jnp.sum(p, axis=0, keepdims=True) + jnp.exp(sk - m)
            o = _dot_tn(vals, p.astype(BF16)) * (1.0 / den)
            lse_rows.append(m + jnp.log(den))
            o_ref[:, hk * GQA * HEAD_DIM:(hk + 1) * GQA * HEAD_DIM] = _queries_to_rows(o).astype(BF16)
        lse_ref[...] = _rows_to_heads(lse_rows)

    q_spec, k_specs, v_specs, kvc_spec = _attn_specs(L, A, KV, C, P // KV)
    bias = _attn_bias(C)
    return pl.pallas_call(
        body, name="attn_fwd", grid=(L // BLOCK,),
        in_specs=[pl.BlockSpec(memory_space=pltpu.SMEM), q_spec] + k_specs + v_specs
                 + [kvc_spec, pl.BlockSpec(bias.shape, lambda n: (0, 0, 0))],
        out_specs=[pl.BlockSpec((BLOCK, A), lambda n: (n, 0)), pl.BlockSpec((H, BLOCK), lambda n: (0, n))],
        out_shape=[jax.ShapeDtypeStruct((L, A + P), BF16), jax.ShapeDtypeStruct((H, L), F32)],
        compiler_params=_cparams("parallel"),
    )(sink, qk, qk, qk, qk, uv, uv, uv, kvc, bias)


def _attn_bwd_dq(qk, uv, kvc, sink, dap, lse_t, cos, sin, A, KV, P):
    L = qk.shape[0]
    C = kvc.shape[0]
    nkv = KV // HEAD_DIM
    H = nkv * GQA
    scale = HEAD_DIM ** -0.5
    W = 3 * BLOCK

    def body(sink_ref, q_ref, kp_ref, kc_ref, kn_ref, vp_ref, vc_ref, vn_ref, kvc_ref, do_ref, lse_ref, cos_ref, sin_ref, bias_ref,
             dq_ref, rd_ref, ds_ref, dkvc_ref):
        n = pl.program_id(0)

        @pl.when(n == 0)
        def _():
            dkvc_ref[...] = jnp.zeros_like(dkvc_ref)

        bias = bias_ref[_edge_index(n, L // BLOCK)]
        rd_rows, dsink_rows, dq_parts = [], [], []
        for hk in range(nkv):
            sl = slice(hk * HEAD_DIM, (hk + 1) * HEAD_DIM)
            keys, vals = _keys_values(hk, (kp_ref, kc_ref, kn_ref), (vp_ref, vc_ref, vn_ref), kvc_ref, KV)
            qs = _heads(q_ref, hk * GQA, GQA) * scale
            dos = _heads(do_ref, hk * GQA, GQA).astype(BF16)
            lse = _head_rows(lse_ref, hk)
            p = jnp.exp(_dot_nt(keys, qs) + bias - lse)
            dp = _dot_nt(vals, dos)
            rd = jnp.sum(p * dp, axis=0, keepdims=True)
            ds = (p * (dp - rd)).astype(BF16)
            dq_parts.append(_queries_to_rows(_dot_tn(keys, ds) * scale))
            dkvc_ref[:, sl] += _dot(ds[W:, :], qs)
            dkvc_ref[:, KV + hk * HEAD_DIM:KV + (hk + 1) * HEAD_DIM] += _dot(p[W:, :].astype(BF16), dos)
            rd_rows.append(rd)
            dsink_rows.append(-(jnp.exp(_sink_row(sink_ref, hk) - lse) * rd))
        rd_ref[...] = _rows_to_heads(rd_rows)
        ds_ref[...] = _rows_to_heads(dsink_rows)
        dq = dq_parts[0] if nkv == 1 else jnp.concatenate(dq_parts, axis=1)
        dq_ref[...] = _rope(dq, cos_ref[...], -sin_ref[...]).astype(BF16)

    q_spec, k_specs, v_specs, kvc_spec = _attn_specs(L, A, KV, C, P // KV)
    blk = lambda w: pl.BlockSpec((BLOCK, w), lambda n: (n, 0))
    per_head = pl.BlockSpec((H, BLOCK), lambda n: (0, n))
    bias = _attn_bias(C)
    return pl.pallas_call(
        body, name="attn_bwd_dq", grid=(L // BLOCK,),
        in_specs=[pl.BlockSpec(memory_space=pltpu.SMEM), q_spec] + k_specs + v_specs
                 + [kvc_spec, blk(A), per_head, blk(LANES), blk(LANES), pl.BlockSpec(bias.shape, lambda n: (0, 0, 0))],
        out_specs=[blk(A), per_head, per_head, pl.BlockSpec((C, 2 * KV), lambda n: (0, 0))],
        out_shape=[jax.ShapeDtypeStruct((L, A), BF16), jax.ShapeDtypeStruct((H, L), F32), jax.ShapeDtypeStruct((H, L), F32),
                   jax.ShapeDtypeStruct((C, 2 * KV), F32)],
        compiler_params=_cparams("arbitrary"),
    )(sink, qk, qk, qk, qk, uv, uv, uv, kvc, dap, lse_t, cos, sin, bias)


def _attn_bwd_dkv(qk, uv, dap, lse_t, rd_t, cos, sin, A, KV, P):
    L = qk.shape[0]
    nb = L // BLOCK
    nkv = KV // HEAD_DIM
    H = nkv * GQA
    scale = HEAD_DIM ** -0.5

    def body(k_ref, v_ref, qp_ref, qc_ref, qn_ref, dop_ref, doc_ref, don_ref, lsep_ref, lsec_ref, lsen_ref,
             rdp_ref, rdc_ref, rdn_ref, cos_ref, sin_ref, bias_ref, dk_ref, dv_ref):
        bias = bias_ref[_edge_index(pl.program_id(0), nb)]
        dk_parts, dv_parts = [], []
        for hk in range(nkv):
            sl = slice(hk * HEAD_DIM, (hk + 1) * HEAD_DIM)
            km = k_ref[:, sl]
            vm = v_ref[:, sl].astype(BF16)
            qs = jnp.concatenate([_heads(q, hk * GQA, GQA) for q in (qp_ref, qc_ref, qn_ref)], axis=0) * scale
            dos = jnp.concatenate([_heads(d, hk * GQA, GQA) for d in (dop_ref, doc_ref, don_ref)], axis=0).astype(BF16)
            rows = [slice(hk * GQA + g, hk * GQA + g + 1) for g in range(GQA)]
            lse = jnp.concatenate([t[r, :] for t in (lsep_ref, lsec_ref, lsen_ref) for r in rows], axis=1)
            rdv = jnp.concatenate([t[r, :] for t in (rdp_ref, rdc_ref, rdn_ref) for r in rows], axis=1)
            p = jnp.exp(_dot_nt(km, qs) + bias - lse)
            ds = (p * (_dot_nt(vm, dos) - rdv)).astype(BF16)
            dk_parts.append(_dot(ds, qs))
            dv_parts.append(_dot(p.astype(BF16), dos))
        dk = dk_parts[0] if nkv == 1 else jnp.concatenate(dk_parts, axis=1)
        dv = dv_parts[0] if nkv == 1 else jnp.concatenate(dv_parts, axis=1)
        dk_ref[...] = _rope(dk, cos_ref[...], -sin_ref[...]).astype(BF16)
        dv_ref[...] = dv.astype(BF16)

    prev = lambda m: jnp.maximum(m - 1, 0)
    nxt = lambda m: jnp.minimum(m + 1, nb - 1)
    three = lambda w: [pl.BlockSpec((BLOCK, w), lambda m: (prev(m), 0)), pl.BlockSpec((BLOCK, w), lambda m: (m, 0)),
                       pl.BlockSpec((BLOCK, w), lambda m: (nxt(m), 0))]
    three_t = [pl.BlockSpec((H, BLOCK), lambda m: (0, prev(m))), pl.BlockSpec((H, BLOCK), lambda m: (0, m)),
               pl.BlockSpec((H, BLOCK), lambda m: (0, nxt(m)))]
    blk = lambda w: pl.BlockSpec((BLOCK, w), lambda m: (m, 0))
    bias = _attn_bias_keys()
    return pl.pallas_call(
        body, name="attn_bwd_dkv", grid=(nb,),
        in_specs=[pl.BlockSpec((BLOCK, KV), lambda m: (m, A // KV)), pl.BlockSpec((BLOCK, KV), lambda m: (m, P // KV))]
                 + three(A) + three(A) + three_t + three_t + [blk(LANES), blk(LANES), pl.BlockSpec(bias.shape, lambda m: (0, 0, 0))],
        out_specs=[blk(KV), blk(KV)],
        out_shape=[jax.ShapeDtypeStruct((L, KV), BF16), jax.ShapeDtypeStruct((L, KV), BF16)],
        compiler_params=_cparams("parallel"),
    )(qk, uv, qk, qk, qk, dap, dap, dap, lse_t, lse_t, lse_t, rd_t, rd_t, rd_t, cos, sin, bias)


def _halo_specs(T, L, W, col):
    per = T // HALO
    return [pl.BlockSpec((HALO, W), lambda i: (jnp.maximum(i * per - 1, 0), col)),
            pl.BlockSpec((T, W), lambda i: (i, col)),
            pl.BlockSpec((HALO, W), lambda i: (jnp.minimum((i + 1) * per, L // HALO - 1), col))]


def _fill_halo_buf(buf, prev_ref, cur_ref, next_ref, i, nt, T):
    buf[0:HALO, :] = jnp.where(i > 0, prev_ref[...], 0.0)
    buf[HALO:HALO + T, :] = cur_ref[...]
    buf[HALO + T:2 * HALO + T, :] = jnp.where(i < nt - 1, next_ref[...], 0.0)


def _zero_margins(lv):
    rows = lv.shape[0]
    lv[0:HALO, :] = jnp.zeros((HALO, lv.shape[1]), F32)
    lv[rows - HALO:rows, :] = jnp.zeros((HALO, lv.shape[1]), F32)


def _window_sums(lv, x, w, first):
    n = x.shape[0]
    lv[HALO:HALO + n, :] = x
    cur = x + lv[pl.ds(HALO + first, n), :]
    span = 1
    while 2 * span < w:
        lv[HALO:HALO + n, :] = cur
        cur = lv[pl.ds(HALO - span, n), :] + lv[pl.ds(HALO + span, n), :]
        span *= 2
    return cur


def _counts(t, w, L):
    lo = jnp.clip(t - w // 2, 0, L)
    hi = jnp.clip(t - w // 2 + w, 0, L)
    return jnp.maximum(hi - lo, 1).astype(F32)


def _pool_fwd(u, pw, scale, mix):
    L, P = u.shape[0], scale.shape[1]
    gd = P // POOL_GROUPS
    T = _tile(L, 256, 8)
    nt = L // T
    assert (mix.shape[1] - P) % P == 0
    mix_col = mix.shape[1] // P - 1

    def body(up_ref, uc_ref, un_ref, pw_ref, sc_ref, mix_ref, out_ref, pooled_ref, buf, lv):
        i = pl.program_id(0)
        _fill_halo_buf(buf, up_ref, uc_ref, un_ref, i, nt, T)
        _zero_margins(lv)
        t = i * T + lax.broadcasted_iota(I32, (T, 1), 0)
        for g, w in enumerate(POOL_WINDOWS):
            cols = slice(g * gd, (g + 1) * gd)
            acc = _window_sums(lv, buf[:, cols], w, -1)[HALO:HALO + T]
            pooled = (acc / _counts(t, w, L) - uc_ref[:, cols]).astype(BF16)
            pooled_ref[:, cols] = pooled
            out_ref[:, cols] = (_dot(pooled, pw_ref[g]) * sc_ref[:, cols]).astype(BF16)

    return pl.pallas_call(
        body, name="pool_fwd", grid=(nt,),
        in_specs=_halo_specs(T, L, P, 0) + [pl.BlockSpec((POOL_GROUPS, gd, gd), lambda i: (0, 0, 0)), pl.BlockSpec((1, P), lambda i: (0, 0)),
                                            pl.BlockSpec(memory_space=pl.ANY)],
        out_specs=[pl.BlockSpec((T, P), lambda i: (i, mix_col)), pl.BlockSpec((T, P), lambda i: (i, 0))],
        out_shape=[jax.ShapeDtypeStruct(mix.shape, BF16), jax.ShapeDtypeStruct((L, P), BF16)],
        scratch_shapes=[pltpu.VMEM((T + 2 * HALO, P), F32), pltpu.VMEM((T + 4 * HALO, gd), F32)],
        input_output_aliases={5: 0},
        compiler_params=_cparams("parallel"),
    )(u, u, u, pw, scale, mix)


def _pool_bwd_mix(d_pool, pooled, pw, scale):
    L, P = pooled.shape
    gd = P // POOL_GROUPS
    T = _tile(L, 256, 8)

    def body(dp_ref, pooled_ref, pw_ref, sc_ref, dpooled_ref, dpw_ref, dsc_ref):
        i = pl.program_id(0)

        @pl.when(i == 0)
        def _():
            dpw_ref[...] = jnp.zeros_like(dpw_ref)
            dsc_ref[...] = jnp.zeros_like(dsc_ref)

        for g in range(POOL_GROUPS):
            cols = slice(g * gd, (g + 1) * gd)
            pb = pooled_ref[:, cols]
            dp = dp_ref[:, cols]
            dsc_ref[:, cols] += jnp.sum(dp * _dot(pb, pw_ref[g]), axis=0, keepdims=True)
            dm = (dp * sc_ref[:, cols]).astype(BF16)
            dpw_ref[g] += _dot_tn(pb, dm)
            dpooled_ref[:, cols] = _dot_nt(dm, pw_ref[g])

    return pl.pallas_call(
        body, name="pool_bwd_mix", grid=(L // T,),
        in_specs=[pl.BlockSpec((T, P), lambda i: (i, 0)), pl.BlockSpec((T, P), lambda i: (i, 0)),
                  pl.BlockSpec((POOL_GROUPS, gd, gd), lambda i: (0, 0, 0)), pl.BlockSpec((1, P), lambda i: (0, 0))],
        out_specs=[pl.BlockSpec((T, P), lambda i: (i, 0)), pl.BlockSpec((POOL_GROUPS, gd, gd), lambda i: (0, 0, 0)),
                   pl.BlockSpec((1, P), lambda i: (0, 0))],
        out_shape=[jax.ShapeDtypeStruct((L, P), F32), jax.ShapeDtypeStruct((POOL_GROUPS, gd, gd), F32), jax.ShapeDtypeStruct((1, P), F32)],
        compiler_params=_cparams("arbitrary"),
    )(d_pool, pooled, pw, scale)


def _pool_bwd_window(dpooled):
    L, P = dpooled.shape
    gd = P // POOL_GROUPS
    T = _tile(L, 256, 8)
    nt = L // T

    def body(dp_ref, dc_ref, dn_ref, du_ref, buf, lv):
        i = pl.program_id(0)
        _fill_halo_buf(buf, dp_ref, dc_ref, dn_ref, i, nt, T)
        _zero_margins(lv)
        t = i * T - HALO + lax.broadcasted_iota(I32, (T + 2 * HALO, 1), 0)
        for g, w in enumerate(POOL_WINDOWS):
            cols = slice(g * gd, (g + 1) * gd)
            acc = _window_sums(lv, buf[:, cols] / _counts(t, w, L), w, 1)[HALO:HALO + T]
            du_ref[:, cols] = (acc - dc_ref[:, cols]).astype(BF16)

    return pl.pallas_call(
        body, name="pool_bwd_window", grid=(nt,),
        in_specs=_halo_specs(T, L, P, 0),
        out_specs=pl.BlockSpec((T, P), lambda i: (i, 0)),
        out_shape=jax.ShapeDtypeStruct((L, P), BF16),
        scratch_shapes=[pltpu.VMEM((T + 2 * HALO, P), F32), pltpu.VMEM((T + 4 * HALO, gd), F32)],
        compiler_params=_cparams("parallel"),
    )(dpooled, dpooled, dpooled)


def _sum_rows(name, a):
    R, N = a.shape

    def body(a_ref, o_ref):
        if R <= 16:
            acc = a_ref[0:1, :]
            for r in range(1, R):
                acc = acc + a_ref[r:r + 1, :]
        else:
            acc = jnp.sum(a_ref[...], axis=0, keepdims=True)
        o_ref[...] = acc

    return pl.pallas_call(body, name=name, out_shape=jax.ShapeDtypeStruct((1, N), F32))(a)


def _sum_lanes(name, a):
    def body(a_ref, o_ref):
        o_ref[...] = jnp.sum(a_ref[...], axis=1, keepdims=True)

    return pl.pallas_call(body, name=name, out_shape=jax.ShapeDtypeStruct((a.shape[0], 1), F32))(a)


def _silu_grad_mul(cv, g):
    def body(c_ref, g_ref, o_ref):
        cvv = c_ref[...]
        s = 1.0 / (1.0 + jnp.exp(-cvv))
        o_ref[...] = g_ref[...] * (s * (1.0 + cvv * (1.0 - s)))

    return pl.pallas_call(body, name="silu_grad_mul", out_shape=jax.ShapeDtypeStruct(cv.shape, F32))(cv, g)


def _adamw(name, w, g, m, v):
    R, C = w.shape
    parts = g.ndim == 3
    n_parts = g.shape[0] if parts else 1
    T = _tile(R, max(8, 262144 // C), 8)

    def body(w_ref, g_ref, m_ref, v_ref, go_ref, d_ref, mo_ref, vo_ref):
        if parts:
            gv = g_ref[0].astype(F32)
            for p in range(1, n_parts):
                gv = gv + g_ref[p].astype(F32)
        else:
            gv = g_ref[...]
        mn = ADAM_B1 * m_ref[...] + (1.0 - ADAM_B1) * gv
        vn = ADAM_B2 * v_ref[...] + (1.0 - ADAM_B2) * (gv * gv)
        m_hat = mn / (1.0 - ADAM_B1 ** ADAM_STEP)
        v_hat = vn / (1.0 - ADAM_B2 ** ADAM_STEP)
        go_ref[...] = gv
        d_ref[...] = -ADAM_LR * (m_hat / (jnp.sqrt(v_hat) + ADAM_EPS) + ADAM_WD * w_ref[...])
        mo_ref[...] = mn
        vo_ref[...] = vn

    tile = pl.BlockSpec((T, C), lambda i: (i, 0))
    g_spec = pl.BlockSpec((n_parts, T, C), lambda i: (0, i, 0)) if parts else tile
    return pl.pallas_call(
        body, name=name, grid=(R // T,),
        in_specs=[tile, g_spec, tile, tile], out_specs=[tile] * 4,
        out_shape=[jax.ShapeDtypeStruct((R, C), F32)] * 4,
        compiler_params=_cparams("parallel"),
    )(w, g, m, v)


def _dev_index(px, py, pc):
    return 4 * px + 2 * py + pc


def _all_gather(name, arrs):
    n = len(arrs)

    def body(*refs):
        ins, outs = refs[:n], refs[n:2 * n]
        send_sems, recv_sems, local_sems = refs[2 * n:]
        x, y, c = lax.axis_index("x"), lax.axis_index("y"), lax.axis_index("c")
        me, sibling = (x, y, c), (x, y, 1 - c)
        chips = [(1 - x, y), (x, 1 - y), (1 - x, 1 - y)]

        def copy(a, k, block, to, src=None):
            slot = outs[a].at[_dev_index(*block)]
            return pltpu.make_async_remote_copy(
                src_ref=slot if src is None else src, dst_ref=slot, send_sem=send_sems.at[a, k], recv_sem=recv_sems.at[a, k],
                device_id=to, device_id_type=MESH)

        mine = [pltpu.make_async_copy(ins[a], outs[a].at[_dev_index(*me)], local_sems.at[a]) for a in range(n)]
        for cp in mine:
            cp.start()
        first = []
        for a in range(n):
            first.append(copy(a, 0, me, sibling, src=ins[a]))
            first += [copy(a, 1 + j, me, (*chip, c), src=ins[a]) for j, chip in enumerate(chips)]
        for cp in first:
            cp.start()
        passed = []
        for j, chip in enumerate(chips):
            for a in range(n):
                copy(a, 1 + j, (*chip, c), me).wait_recv()
                fwd = copy(a, 4 + j, (*chip, c), sibling)
                fwd.start()
                passed.append(fwd)
        for a in range(n):
            copy(a, 0, sibling, me).wait_recv()
            for j, chip in enumerate(chips):
                copy(a, 4 + j, (*chip, 1 - c), me).wait_recv()
        for cp in first + passed:
            cp.wait_send()
        for cp in mine:
            cp.wait()

    return pl.pallas_call(
        body, name=name,
        in_specs=[HBM] * n, out_specs=[HBM] * n,
        out_shape=[jax.ShapeDtypeStruct((N_DEV, *a.shape), a.dtype) for a in arrs],
        scratch_shapes=[pltpu.SemaphoreType.DMA((n, N_DEV - 1)), pltpu.SemaphoreType.DMA((n, N_DEV - 1)), pltpu.SemaphoreType.DMA((n,))],
    )(*arrs)


N_COPIES = {"all_to_all": N_DEV - 1, "gather_all": N_DEV - 1, "gather_chips": 4, "forward": 3}


def _exchange_copies(kind, src_ref, land_ref, send_sems, recv_sems, sending):
    x, y, c = lax.axis_index("x"), lax.axis_index("y"), lax.axis_index("c")
    me = _dev_index(x, y, c)
    others = [(1 - x, y), (x, 1 - y), (1 - x, 1 - y)]
    if kind in ("all_to_all", "gather_all"):
        flips = [(dx, dy, dc) for dx in (0, 1) for dy in (0, 1) for dc in (0, 1)][1:]
        peers = [(1 - x if dx else x, 1 - y if dy else y, 1 - c if dc else c) for dx, dy, dc in flips]
        plan = [(p, src_ref if kind == "gather_all" else src_ref.at[_dev_index(*p)], me if sending else _dev_index(*p)) for p in peers]
    elif kind == "gather_chips":
        peers = [(x, y, 1 - c)] + [(*o, c) for o in others]
        plan = [(p, src_ref, me if sending else _dev_index(*p)) for p in peers]
    else:
        plan = [((x, y, 1 - c), land_ref.at[_dev_index(*o, c)], _dev_index(*o, c if sending else 1 - c)) for o in others]
    return [pltpu.make_async_remote_copy(src_ref=src, dst_ref=land_ref.at[slot], send_sem=send_sems.at[k], recv_sem=recv_sems.at[k],
                                         device_id=peer, device_id_type=MESH)
            for k, (peer, src, slot) in enumerate(plan)]


def _exchange_start(name, kind, srcs, lands=None):
    if lands is None:
        lands = [lax.empty(s.shape if kind == "all_to_all" else (N_DEV, *s.shape), s.dtype) for s in srcs]
    n = len(lands)
    ops = ([] if srcs is None else list(srcs)) + list(lands)
    m = len(ops)

    def body(*refs):
        src_refs = [None] * n if srcs is None else refs[:n]
        land_refs = refs[m - n:m]
        send_sems, recv_sems, token = refs[m:m + n], refs[m + n:m + 2 * n], refs[-1]
        for a in range(n):
            for cp in _exchange_copies(kind, src_refs[a], land_refs[a], send_sems[a], recv_sems[a], True):
                cp.start()
        token[...] = jnp.zeros_like(token)

    sems = [pltpu.SemaphoreType.DMA((N_COPIES[kind],))] * (2 * n)
    outs = pl.pallas_call(
        body, name=name,
        out_shape=sems + [pltpu.HBM(o.shape, o.dtype) for o in ops] + [jax.ShapeDtypeStruct((8, LANES), F32)],
        in_specs=[HBM] * m,
        out_specs=[SEM] * (2 * n) + [HBM] * m + [pl.BlockSpec(memory_space=pltpu.VMEM)],
        input_output_aliases={i: 2 * n + i for i in range(m)},
        compiler_params=SIDE_EFFECT,
    )(*[pltpu.with_memory_space_constraint(o, pltpu.HBM) for o in ops])
    thru = outs[2 * n:2 * n + m]
    return outs[:n], outs[n:2 * n], (None if srcs is None else thru[:n]), thru[m - n:], outs[-1]


def _exchange_wait(name, kind, send_sems, recv_sems, srcs, lands, after):
    n = len(lands)
    ops = ([] if srcs is None else list(srcs)) + list(lands)
    m = len(ops)

    def body(*refs):
        src_refs = [None] * n if srcs is None else refs[:n]
        land_refs = refs[m - n:m]
        send_refs, recv_refs = refs[m:m + n], refs[m + n:m + 2 * n]
        for a in range(n):
            for cp in _exchange_copies(kind, src_refs[a], land_refs[a], send_refs[a], recv_refs[a], False):
                cp.wait_send()
                cp.wait_recv()

    outs = pl.pallas_call(
        body, name=name,
        out_shape=[pltpu.HBM(o.shape, o.dtype) for o in ops],
        in_specs=[HBM] * m + [SEM] * (2 * n) + [pl.BlockSpec(memory_space=pl.ANY)],
        out_specs=[HBM] * m,
        input_output_aliases={i: i for i in range(m)},
        compiler_params=SIDE_EFFECT,
    )(*ops, *send_sems, *recv_sems, after)
    return (None if srcs is None else outs[:n]), outs[m - n:]


def _with_own(land, own, me):
    return lax.dynamic_update_slice_in_dim(land, own, me, 0)


def _shards_to_cols(g):
    return jnp.transpose(g, (1, 0, 2)).reshape(g.shape[1], N_DEV * g.shape[2])


def _cols_to_shards(a):
    R, Ctot = a.shape
    return jnp.transpose(a.reshape(R, N_DEV, Ctot // N_DEV), (1, 0, 2))


def kernel(x, c, ctx, c_ctx, norm_attn_w, norm_mlp_w, w_ada, b_ada, w_in, attn_sink, pool_w, pool_scale, w_out, w_mlp_up, w_mlp_down, final_norm_w, loss_target, m_c_ctx, m_norm_attn_w, m_norm_mlp_w, m_w_ada, m_b_ada, m_w_in, m_attn_sink, m_pool_w, m_pool_scale, m_w_out, m_w_mlp_up, m_w_mlp_down, m_final_norm_w, v_c_ctx, v_norm_attn_w, v_norm_mlp_w, v_w_ada, v_b_ada, v_w_in, v_attn_sink, v_pool_w, v_pool_scale, v_w_out, v_w_mlp_up, v_w_mlp_down, v_final_norm_w):
    _, L, D = x.shape
    H = attn_sink.shape[1]
    A = H * HEAD_DIM
    KV = A // GQA
    P = pool_scale.shape[1]
    MODW = N_MOD * D
    ws = MODW // N_DEV
    gd = P // POOL_GROUPS
    me = _dev_index(lax.axis_index("x"), lax.axis_index("y"), lax.axis_index("c"))

    x2d, ctx2d, tgt = x[0], ctx[0], loss_target[0]
    cctx_row = c_ctx.reshape(1, D)
    wf_row = final_norm_w.reshape(1, D)
    w_ada_l = w_ada[0]
    pool_w_l = pool_w[0].reshape(POOL_GROUPS * (gd // N_DEV), gd)

    (c_all,) = _all_gather("gather_cond", [c])
    cond = jnp.concatenate([c_all[:, 0, :], cctx_row, jnp.zeros((COND_ROWS - N_DEV - 1, D), F32)], axis=0)
    b_sh = lax.dynamic_slice_in_dim(b_ada, me * ws, ws, axis=1)
    (mods_sh,) = _mm("ada_mod", cond, w_ada_l, "nn", [F32], SMALL_TILES, a_pre=_silu, extras=[("n", b_sh)], epilogue=lambda acc, b: (acc + b,))
    (mods_g,) = _all_gather("gather_mods", [mods_sh])

    w_srcs = [w_in[0].astype(BF16), w_out[0].astype(BF16), pool_w_l.astype(BF16), w_mlp_up[0].astype(BF16), w_mlp_down[0].astype(BF16)]
    w_srcs, mods_g = lax.optimization_barrier((w_srcs, mods_g))
    gather_start = _exchange_start("gather_weights_start", "gather_chips", w_srcs[:4])
    w_down_src, _ = lax.optimization_barrier((w_srcs[4], gather_start[4]))
    down_start = _exchange_start("gather_w_down_start", "gather_all", [w_down_src])

    def weights(tag, started, lo, hi, after_chips, after_forward):
        gw_send, gw_recv, gw_src, gw_land, _ = started
        mine, lands = _exchange_wait(f"gather_{tag}_wait", "gather_chips", gw_send[lo:hi], gw_recv[lo:hi], gw_src[lo:hi],
                                     gw_land[lo:hi], after_chips)
        f_send, f_recv, _, f_land, f_token = _exchange_start(f"forward_{tag}_start", "forward", None, lands)
        _, lands = _exchange_wait(f"forward_{tag}_wait", "forward", f_send, f_recv, None, f_land,
                                  f_token if after_forward is None else after_forward)
        return [_with_own(l, s[None], me) for l, s in zip(lands, mine)]

    mods = _shards_to_cols(mods_g)
    mod_b = lax.dynamic_slice_in_dim(mods, me, 1, axis=0)
    sh_a, sc_a, g_a, sh_m, sc_m, g_m = [mod_b[:, i * D:(i + 1) * D] for i in range(N_MOD)]
    csh_a, csc_a = mods[N_DEV:N_DEV + 1, :D], mods[N_DEV:N_DEV + 1, D:2 * D]

    cos, sin = _rope_tables(L)
    h = _norm_fwd("norm_attn", x2d, norm_attn_w, sc_a, sh_a)
    hc = _norm_fwd("norm_attn_ctx", ctx2d, norm_attn_w, csc_a, csh_a)
    (win_g,) = weights("w_in", gather_start, 0, 1, h, None)
    W_in = _shards_to_cols(win_g)
    W_qk, W_kv = W_in[:, :A + KV], W_in[:, A:A + 2 * KV]
    W_uv = jnp.concatenate([W_in[:, A + 2 * KV:], W_in[:, A + KV:A + 2 * KV]], axis=1)
    qk, uv = _in_proj(h, W_qk, W_uv, cos, sin)
    (kvc,) = _mm("in_proj_ctx", hc, W_kv, "nn", [BF16], SMALL_TILES)
    attn, lse = _attn_fwd(qk, uv, kvc, attn_sink, A, KV, P)
    wout_g, pw_g = weights("w_out", gather_start, 1, 3, qk, attn)
    W_out = wout_g.reshape(A + P, D)
    PW = jnp.transpose(pw_g.reshape(N_DEV, POOL_GROUPS, gd // N_DEV, gd), (1, 0, 2, 3)).reshape(POOL_GROUPS, gd, gd)
    ap, pooled = _pool_fwd(uv, PW, pool_scale, attn)
    o, x1, hm = _mm("out_proj_norm", ap, W_out, "nn", [F32, F32, BF16], (256, D, D), chunk=128, epilogue=_out_proj_epilogue,
                    extras=[("mn", x2d), ("n", g_a), ("n", norm_mlp_w), ("n", sc_m), ("n", sh_m)])
    (W_up,) = weights("w_up", gather_start, 3, 4, attn, x1)
    up, act = _mm("mlp_up", hm, W_up, "nn", [F32, BF16], (1024, 1024, 2048), epilogue=lambda acc: (acc, _relu2(acc)), b_shards=True)
    down_src, down_land = _exchange_wait("gather_w_down_wait", "gather_all", down_start[0], down_start[1], down_start[2], down_start[3], up)
    wdown_g = _with_own(down_land[0], down_src[0][None], me)
    W_down = wdown_g.reshape(-1, D)
    d_x2, d_mlp, d_wf, d_gm, loss_row = _mm(
        "mlp_down_loss", act, W_down, "nn", [F32, BF16], (512, D, 1024), chunk=128, n_sums=3, vmem=FUSED_VMEM_LIMIT,
        epilogue=_mlp_down_epilogue, extras=[("mn", x1), ("mn", tgt), ("n", g_m), ("n", wf_row)])
    loss_p = loss_row[:, :1]

    (d_up,) = _mm("mlp_down_bwd_act", d_mlp, W_down, "nt", [BF16], (1024, 1024, 2048), extras=[("mn", up)],
                  epilogue=lambda acc, uu: (acc * (2.0 * jnp.maximum(uu, 0.0)),))
    (gW_down,) = _mm("mlp_down_bwd_w", act, d_mlp, "tn", [BF16], (1024, 2048, 2048), vmem=FUSED_VMEM_LIMIT)
    (gW_up_s,) = _mm("mlp_up_bwd_w", hm, d_up, "tn", [BF16], (2048, 1024, 2048), out_shards=True, vmem=FUSED_VMEM_LIMIT)
    g_mlp_srcs = [gW_up_s, gW_down.reshape(N_DEV, -1, D)]
    g_mlp = _exchange_start("grads_mlp_start", "all_to_all", g_mlp_srcs)
    (d_hm,) = _mm("mlp_up_bwd_act", d_up, W_up, "nt", [F32], (1024, D, 1024), b_shards=True)

    d_x1, d_o, d_attn, d_pool, s_sh_m, s_sc_m, s_w_nm, d_ga = _norm_bwd_out_proj_bwd(
        x1, d_hm, d_x2, o, norm_mlp_w, sc_m + g_mlp[4][0, 0], g_a, W_out.T, A)
    (gW_out,) = _mm("out_proj_bwd_w", ap, d_o, "tn", [BF16], (1024, 2048, 2048))
    d_pooled, gPW, d_pscale = _pool_bwd_mix(d_pool, pooled, PW, pool_scale)
    gpw_s = jnp.transpose(gPW.astype(BF16).reshape(POOL_GROUPS, N_DEV, gd // N_DEV, gd), (1, 0, 2, 3)).reshape(N_DEV, -1, gd)
    g_mix_srcs = [gW_out.reshape(N_DEV, (A + P) // N_DEV, D), gpw_s]
    g_mix = _exchange_start("grads_mix_start", "all_to_all", g_mix_srcs)
    lse = lse + g_mix[4][0, 0]
    d_u = _pool_bwd_window(d_pooled)
    d_q, rd, dsink_q, d_kvc = _attn_bwd_dq(qk, uv, kvc, attn_sink, d_attn, lse, cos, sin, A, KV, P)
    d_k, d_v = _attn_bwd_dkv(qk, uv, d_attn, lse, rd, cos, sin, A, KV, P)
    d_sink = _sum_lanes("sink_grad", dsink_q).reshape(1, H)
    d_p = jnp.concatenate([d_q, d_k, d_v, d_u], axis=1)
    d_kvc_b = d_kvc.astype(BF16)
    (gW_kv_ctx,) = _mm("in_proj_ctx_bwd_w", hc, d_kvc_b, "tn", [F32], SMALL_TILES)
    (d_hc,) = _mm("in_proj_ctx_bwd_act", d_kvc_b, W_kv, "nt", [F32], SMALL_TILES)
    gW_in_init = jnp.pad(gW_kv_ctx, ((0, 0), (A, P)))
    (gW_in,) = _mm("in_proj_bwd_w", h, d_p, "tn", [BF16], (1024, 1280, 2048), extras=[("mn", gW_in_init)], epilogue=lambda acc, init: (acc + init,))
    g_in_srcs = [_cols_to_shards(gW_in)]
    g_in = _exchange_start("grads_in_start", "all_to_all", g_in_srcs)
    grad_x, s_sh_a, s_sc_a, s_w_na = _mm(
        "in_proj_bwd_norm", d_p, W_in, "nt", [F32], (256, D, A + 2 * KV + P), chunk=128, n_sums=3,
        epilogue=lambda acc, xr, dres, w, sc: _modulated_norm_bwd(xr, acc, dres, w, sc),
        extras=[("mn", x2d), ("mn", d_x1), ("n", norm_attn_w), ("n", sc_a + g_in[4][0, 0])])
    s_csh, s_csc, s_w_na = _norm_bwd_sums("norm_attn_ctx_bwd", ctx2d, d_hc, norm_attn_w, csc_a, s_w_na)

    pad_l = lambda a: jnp.pad(a, ((0, 0), (0, LANES - a.shape[1])))
    d_mod_b = jnp.concatenate([s_sh_a, s_sc_a, d_ga, s_sh_m, s_sc_m, d_gm], axis=1)
    summed = jnp.concatenate([s_csh, s_csc, s_w_na, s_w_nm, d_wf, d_pscale, pad_l(d_sink), pad_l(loss_p)], axis=1)
    (small_g,) = _all_gather("gather_small", [jnp.concatenate([d_mod_b, summed], axis=1)])
    small_g = small_g[:, 0, :]
    tot = _sum_rows("small_sum", small_g[:, MODW:])
    off = [0]
    for wdt in (D, D, D, D, D, P, LANES, LANES):
        off.append(off[-1] + wdt)
    seg = lambda i: tot[:, off[i]:off[i + 1]]
    g_norm_attn, g_norm_mlp, g_final, g_pscale = seg(2), seg(3), seg(4), seg(5)
    g_sink, loss = seg(6)[:, :H], seg(7)[0, 0]
    d_mod_ctx = jnp.concatenate([seg(0), seg(1), jnp.zeros((1, MODW - 2 * D), F32)], axis=1)
    d_mod = jnp.concatenate([small_g[:, :MODW], d_mod_ctx, jnp.zeros((COND_ROWS - N_DEV - 1, MODW), F32)], axis=0)
    g_b_ada = _sum_rows("b_ada_grad", d_mod[:N_DEV + 1])
    d_mod_sh = lax.dynamic_slice_in_dim(d_mod, me * ws, ws, axis=1)
    (g_w_ada,) = _mm("ada_bwd_w", cond, d_mod_sh, "tn", [F32], SMALL_TILES, a_pre=_silu)
    (d_cond_p,) = _mm("ada_bwd_cond", d_mod_sh, w_ada_l, "nt", [F32], SMALL_TILES)
    (d_cctx_g,) = _all_gather("gather_cctx", [d_cond_p[N_DEV:N_DEV + 1]])
    g_c_ctx = _silu_grad_mul(cctx_row, _sum_rows("cctx_sum", d_cctx_g[:, 0, :]))

    def arrived(name, started):
        srcs, lands = _exchange_wait(name, "all_to_all", started[0], started[1], started[2], started[3], g_c_ctx)
        return [_with_own(l, lax.dynamic_index_in_dim(s, me, 0, keepdims=True), me) for l, s in zip(lands, srcs)]

    r_up, r_down = arrived("grads_mlp_wait", g_mlp)
    r_out, r_pw = arrived("grads_mix_wait", g_mix)
    (r_in,) = arrived("grads_in_wait", g_in)

    results = {
        "c_ctx": _adamw("adam_c_ctx", cctx_row, g_c_ctx, m_c_ctx.reshape(1, D), v_c_ctx.reshape(1, D)),
        "norm_attn_w": _adamw("adam_norm_attn", norm_attn_w, g_norm_attn, m_norm_attn_w, v_norm_attn_w),
        "norm_mlp_w": _adamw("adam_norm_mlp", norm_mlp_w, g_norm_mlp, m_norm_mlp_w, v_norm_mlp_w),
        "w_ada": _adamw("adam_w_ada", w_ada_l, g_w_ada, m_w_ada[0], v_w_ada[0]),
        "b_ada": _adamw("adam_b_ada", b_ada, g_b_ada, m_b_ada, v_b_ada),
        "w_in": _adamw("adam_w_in", w_in[0], r_in, m_w_in[0], v_w_in[0]),
        "attn_sink": _adamw("adam_sink", attn_sink, g_sink, m_attn_sink, v_attn_sink),
        "pool_w": _adamw("adam_pool_w", pool_w_l, r_pw, m_pool_w[0].reshape(pool_w_l.shape), v_pool_w[0].reshape(pool_w_l.shape)),
        "pool_scale": _adamw("adam_pool_scale", pool_scale, g_pscale, m_pool_scale, v_pool_scale),
        "w_out": _adamw("adam_w_out", w_out[0], r_out, m_w_out[0], v_w_out[0]),
        "w_mlp_up": _adamw("adam_w_up", w_mlp_up[0], r_up, m_w_mlp_up[0], v_w_mlp_up[0]),
        "w_mlp_down": _adamw("adam_w_down", w_mlp_down[0], r_down, m_w_mlp_down[0], v_w_mlp_down[0]),
        "final_norm_w": _adamw("adam_final_norm", wf_row, g_final, m_final_norm_w.reshape(1, D), v_final_norm_w.reshape(1, D)),
    }
    shapes = {"c_ctx": c_ctx.shape, "norm_attn_w": norm_attn_w.shape, "norm_mlp_w": norm_mlp_w.shape, "w_ada": w_ada.shape,
              "b_ada": b_ada.shape, "w_in": w_in.shape, "attn_sink": attn_sink.shape, "pool_w": pool_w.shape,
              "pool_scale": pool_scale.shape, "w_out": w_out.shape, "w_mlp_up": w_mlp_up.shape, "w_mlp_down": w_mlp_down.shape,
              "final_norm_w": final_norm_w.shape}
    outs = [loss, grad_x.reshape(x.shape)]
    for part in range(4):
        outs += [results[name][part].reshape(shape) for name, shape in shapes.items()]
    return tuple(outs)
```

```python
import jax
import jax.numpy as jnp
import numpy as np
from jax import lax
from jax.experimental import pallas as pl
from jax.experimental.pallas import tpu as pltpu

F32 = jnp.float32
BF16 = jnp.bfloat16
I32 = jnp.int32

HEAD_DIM = 64
GQA = 4
BLOCK = 128
GRID_W = 64
ROPE_BASE = 10000.0
POOL_WINDOWS = (2, 4, 8, 16)
POOL_GROUPS = len(POOL_WINDOWS)
HALO = 8
N_MOD = 6
EPS = 1e-6
NEG_INF = -1e30
ADAM_LR = 0.001
ADAM_B1 = 0.9
ADAM_B2 = 0.999
ADAM_EPS = 1e-08
ADAM_WD = 0.01
ADAM_STEP = 10
N_DEV = 8
COND_ROWS = 2 * N_DEV
LANES = 128
SUBLANES_16BIT = 16
VMEM_LIMIT = 48 * 1024 * 1024
FUSED_VMEM_LIMIT = 56 * 1024 * 1024
SMALL_TILES = (512, 1024, 512)
MESH = pl.DeviceIdType.MESH
HBM = pl.BlockSpec(memory_space=pltpu.HBM)
SEM = pl.BlockSpec(memory_space=pltpu.SEMAPHORE)
SIDE_EFFECT = pltpu.CompilerParams(has_side_effects=pltpu.SideEffectType.DATAFLOW_SIDE_EFFECTING)


def _cparams(*sem):
    return pltpu.CompilerParams(dimension_semantics=sem, vmem_limit_bytes=VMEM_LIMIT)


def _tile(n, pref, align):
    if n <= pref:
        return n
    t = (pref // align) * align
    while t >= align:
        if n % t == 0:
            return t
        t -= align
    return n


def _dot(a, b):
    return lax.dot_general(a, b, (((1,), (0,)), ((), ())), preferred_element_type=F32)


def _dot_nt(a, b):
    return lax.dot_general(a, b, (((1,), (1,)), ((), ())), preferred_element_type=F32)


def _dot_tn(a, b):
    return lax.dot_general(a, b, (((0,), (0,)), ((), ())), preferred_element_type=F32)


_DOTS = {"nn": _dot, "nt": _dot_nt, "tn": _dot_tn}


def _mm(name, a, b, mode, out_dtypes, tiles, *, epilogue=None, extras=(), a_pre=None, n_sums=0, chunk=None,
        b_shards=False, out_shards=False, vmem=VMEM_LIMIT):
    if mode == "nn":
        M, K = a.shape
        K2, N = (b.shape[1], N_DEV * b.shape[2]) if b_shards else b.shape
    elif mode == "nt":
        M, K = a.shape
        N, K2 = (b.shape[1], N_DEV * b.shape[2]) if b_shards else b.shape
    else:
        (K, M), (K2, N) = a.shape, b.shape
    assert K == K2 and not (b_shards and mode == "tn"), (name, a.shape, b.shape)
    n_span = N // N_DEV if out_shards or (b_shards and mode == "nn") else N
    k_span = K // N_DEV if b_shards and mode == "nt" else K
    tm = _tile(M, tiles[0], LANES if mode == "tn" else SUBLANES_16BIT)
    tn = _tile(n_span, tiles[1], LANES)
    tk = _tile(k_span, tiles[2], SUBLANES_16BIT if mode == "tn" else LANES)
    nk, nb, kb = K // tk, n_span // tn, k_span // tk
    rows = tm if chunk is None else min(chunk, tm)
    n_ex, n_out = len(extras), len(out_dtypes)
    use_acc = nk > 1 or rows < tm
    assert n_sums == 0 or N == tn, name

    def product(a_ref, b_ref):
        at = a_ref[...]
        if a_pre is not None:
            at = a_pre(at)
        return _DOTS[mode](at.astype(BF16), b_ref[...].astype(BF16))

    def apply(acc, ex, out_refs, sl):
        res = (acc,) if epilogue is None else epilogue(acc, *ex)
        for o_ref, o in zip(out_refs, res[:n_out]):
            o_ref[sl, :] = o.astype(o_ref.dtype)
        return tuple(res[n_out:])

    def finish(acc, ex_refs, out_refs, sum_refs):
        if rows == tm:
            acc = acc if not use_acc else acc[...]
            sums = apply(acc, [r[...] for r in ex_refs], out_refs, slice(None))
        else:
            def one(ci, sums):
                sl = pl.ds(pl.multiple_of(ci * rows, rows), rows)
                ex = [r[...] if kind == "n" else r[sl, :] for (kind, _), r in zip(extras, ex_refs)]
                return tuple(s + v for s, v in zip(sums, apply(acc[sl, :], ex, out_refs, sl)))
            sums = lax.fori_loop(0, tm // rows, one, tuple(jnp.zeros((1, tn), F32) for _ in range(n_sums)))
        first = pl.program_id(0) == 0
        for s_ref, sv in zip(sum_refs, sums):
            @pl.when(first)
            def _(s_ref=s_ref, sv=sv):
                s_ref[...] = sv

            @pl.when(jnp.logical_not(first))
            def _(s_ref=s_ref, sv=sv):
                s_ref[...] += sv

    def body(a_ref, b_ref, *rest):
        ex_refs, out_refs = rest[:n_ex], rest[n_ex:n_ex + n_out]
        sum_refs = rest[n_ex + n_out:n_ex + n_out + n_sums]
        if not use_acc:
            finish(product(a_ref, b_ref), ex_refs, out_refs, sum_refs)
            return
        acc_ref = rest[-1]
        k = pl.program_id(2)

        @pl.when(k == 0)
        def _():
            acc_ref[...] = product(a_ref, b_ref)

        @pl.when(k > 0)
        def _():
            acc_ref[...] += product(a_ref, b_ref)

        @pl.when(k == nk - 1)
        def _():
            finish(acc_ref, ex_refs, out_refs, sum_refs)

    a_spec = pl.BlockSpec((tk, tm), lambda i, j, k: (k, i)) if mode == "tn" else pl.BlockSpec((tm, tk), lambda i, j, k: (i, k))
    if not b_shards:
        b_spec = pl.BlockSpec((tn, tk), lambda i, j, k: (j, k)) if mode == "nt" else pl.BlockSpec((tk, tn), lambda i, j, k: (k, j))
    elif mode == "nn":
        b_spec = pl.BlockSpec((None, tk, tn), lambda i, j, k: (j // nb, k, j % nb))
    else:
        b_spec = pl.BlockSpec((None, tn, tk), lambda i, j, k: (k // kb, j, k % kb))
    ex_specs = []
    for kind, arr in extras:
        if kind == "mn":
            ex_specs.append(pl.BlockSpec((tm, tn), lambda i, j, k: (i, j)))
        elif kind == "n":
            ex_specs.append(pl.BlockSpec((1, tn), lambda i, j, k: (0, j)))
        else:
            ex_specs.append(pl.BlockSpec((tm, arr.shape[1]), lambda i, j, k: (i, 0)))
    if out_shards:
        out_specs = [pl.BlockSpec((None, tm, tn), lambda i, j, k: (j // nb, i, j % nb)) for _ in out_dtypes]
        out_shape = [jax.ShapeDtypeStruct((N_DEV, M, n_span), d) for d in out_dtypes]
    else:
        out_specs = [pl.BlockSpec((tm, tn), lambda i, j, k: (i, j)) for _ in out_dtypes]
        out_shape = [jax.ShapeDtypeStruct((M, N), d) for d in out_dtypes]
    out_specs += [pl.BlockSpec((1, tn), lambda i, j, k: (0, 0))] * n_sums
    out_shape += [jax.ShapeDtypeStruct((1, N), F32)] * n_sums
    return pl.pallas_call(
        body,
        name=name,
        grid=(M // tm, N // tn, nk),
        in_specs=[a_spec, b_spec] + ex_specs,
        out_specs=out_specs,
        out_shape=out_shape,
        scratch_shapes=[pltpu.VMEM((tm, tn), F32)] if use_acc else [],
        compiler_params=pltpu.CompilerParams(
            dimension_semantics=("arbitrary",) * 3 if n_sums else ("parallel", "parallel", "arbitrary"), vmem_limit_bytes=vmem),
    )(a, b, *[arr for _, arr in extras])


def _silu(v):
    return v / (1.0 + jnp.exp(-v))


def _relu2(v):
    r = jnp.maximum(v, 0.0)
    return r * r


def _rope_tables(L):
    half = HEAD_DIM // 2
    inv_freq = np.float32(ROPE_BASE) ** (-np.arange(0, half, 2, dtype=np.float32) / np.float32(half))
    t = np.arange(L)
    row, col = t // GRID_W, t % GRID_W
    ang_r = row.astype(np.float32)[:, None] * inv_freq[None, :]
    ang_c = col.astype(np.float32)[:, None] * inv_freq[None, :]
    cos = np.concatenate([np.cos(ang_r), np.cos(ang_r), np.cos(ang_c), np.cos(ang_c)], axis=1)
    sin = np.concatenate([-np.sin(ang_r), np.sin(ang_r), -np.sin(ang_c), np.sin(ang_c)], axis=1)
    reps = LANES // HEAD_DIM
    return jnp.asarray(np.tile(cos, (1, reps)), F32), jnp.asarray(np.tile(sin, (1, reps)), F32)


def _rope(xf, cos, sin):
    quarter = HEAD_DIM // 4
    lane = lax.broadcasted_iota(I32, (xf.shape[0], LANES), 1)
    first = (lane & quarter) == 0
    outs = []
    for j in range(xf.shape[1] // LANES):
        xc = xf[:, j * LANES:(j + 1) * LANES]
        partner = jnp.where(first, pltpu.roll(xc, LANES - quarter, 1), pltpu.roll(xc, quarter, 1))
        outs.append(xc * cos + partner * sin)
    return outs[0] if len(outs) == 1 else jnp.concatenate(outs, axis=1)


def _inv_rms(xf):
    return lax.rsqrt(jnp.mean(xf * xf, axis=-1, keepdims=True) + EPS)


def _modulated_norm(xf, w, sc, sh):
    return ((xf * _inv_rms(xf)) * w) * (1.0 + sc) + sh


def _modulated_norm_bwd(xf, dh, dres, w, sc):
    r = _inv_rms(xf)
    xh = xf * r
    dn = dh * (1.0 + sc)
    dxh = dn * w
    dx = dres + r * (dxh - xh * jnp.mean(dxh * xh, axis=-1, keepdims=True))
    col = lambda v: jnp.sum(v, axis=0, keepdims=True)
    return dx, col(dh), col(dh * (xh * w)), col(dn * xh)


def _in_proj(h, w_qk, w_uv, cos, sin):
    L, D = h.shape
    T = _tile(L, 512, SUBLANES_16BIT)

    def body(h_ref, wqk_ref, wuv_ref, cos_ref, sin_ref, qk_ref, uv_ref):
        hv = h_ref[...]
        qk_ref[...] = _rope(_dot(hv, wqk_ref[...]), cos_ref[...], sin_ref[...]).astype(BF16)
        uv_ref[...] = _dot(hv, wuv_ref[...])

    tile = lambda wd: pl.BlockSpec((T, wd), lambda i: (i, 0))
    whole = lambda a: pl.BlockSpec(a.shape, lambda i: (0, 0))
    return pl.pallas_call(
        body, name="in_proj", grid=(L // T,),
        in_specs=[tile(D), whole(w_qk), whole(w_uv), tile(LANES), tile(LANES)],
        out_specs=[tile(w_qk.shape[1]), tile(w_uv.shape[1])],
        out_shape=[jax.ShapeDtypeStruct((L, w_qk.shape[1]), BF16), jax.ShapeDtypeStruct((L, w_uv.shape[1]), F32)],
        compiler_params=_cparams("parallel"),
    )(h, w_qk, w_uv, cos, sin)


def _norm_fwd(name, x, w, sc, sh):
    L, D = x.shape
    T = _tile(L, 512, 8)

    def body(x_ref, w_ref, sc_ref, sh_ref, h_ref):
        h_ref[...] = _modulated_norm(x_ref[...], w_ref[...], sc_ref[...], sh_ref[...]).astype(BF16)

    row = pl.BlockSpec((1, D), lambda i: (0, 0))
    return pl.pallas_call(
        body, name=name, grid=(L // T,),
        in_specs=[pl.BlockSpec((T, D), lambda i: (i, 0)), row, row, row],
        out_specs=pl.BlockSpec((T, D), lambda i: (i, 0)),
        out_shape=jax.ShapeDtypeStruct((L, D), BF16),
        compiler_params=_cparams("parallel"),
    )(x, w, sc, sh)


def _norm_bwd_sums(name, x, dh, w, sc, w_init):
    L, D = x.shape
    T = _tile(L, 256, 8)

    def body(x_ref, dh_ref, w_ref, sc_ref, wi_ref, ssh_ref, ssc_ref, sw_ref):
        @pl.when(pl.program_id(0) == 0)
        def _():
            ssh_ref[...] = jnp.zeros_like(ssh_ref)
            ssc_ref[...] = jnp.zeros_like(ssc_ref)
            sw_ref[...] = wi_ref[...]

        dh = dh_ref[...]
        _, s_sh, s_sc, s_w = _modulated_norm_bwd(x_ref[...], dh, jnp.zeros_like(dh), w_ref[...], sc_ref[...])
        ssh_ref[...] += s_sh
        ssc_ref[...] += s_sc
        sw_ref[...] += s_w

    tile = pl.BlockSpec((T, D), lambda i: (i, 0))
    row = pl.BlockSpec((1, D), lambda i: (0, 0))
    return pl.pallas_call(
        body, name=name, grid=(L // T,), in_specs=[tile, tile, row, row, row], out_specs=[row, row, row],
        out_shape=[jax.ShapeDtypeStruct((1, D), F32)] * 3, compiler_params=_cparams("arbitrary"),
    )(x, dh, w, sc, w_init)


def _norm_bwd_out_proj_bwd(x, dh, dres, o, w, sc, g, w_out_t, A):
    L, D = x.shape
    N = w_out_t.shape[1]
    T = _tile(L, 256, SUBLANES_16BIT)
    half = T // 2

    def body(x_ref, dh_ref, dres_ref, o_ref, w_ref, sc_ref, g_ref, wt_ref, dx_ref, do_ref, dattn_ref, dpool_ref,
             ssh_ref, ssc_ref, sw_ref, sg_ref):
        @pl.when(pl.program_id(0) == 0)
        def _():
            for s_ref in (ssh_ref, ssc_ref, sw_ref, sg_ref):
                s_ref[...] = jnp.zeros_like(s_ref)

        for rows in (slice(0, half), slice(half, T)):
            dx, s_sh, s_sc, s_w = _modulated_norm_bwd(x_ref[rows, :], dh_ref[rows, :], dres_ref[rows, :], w_ref[...], sc_ref[...])
            ssh_ref[...] += s_sh
            ssc_ref[...] += s_sc
            sw_ref[...] += s_w
            sg_ref[...] += jnp.sum(dx * o_ref[rows, :], axis=0, keepdims=True)
            dx_ref[rows, :] = dx
            do_ref[rows, :] = (g_ref[...] * dx).astype(BF16)
        dap = _dot(do_ref[...], wt_ref[...])
        dattn_ref[...] = dap[:, :A].astype(BF16)
        dpool_ref[...] = dap[:, A:]

    tile = pl.BlockSpec((T, D), lambda i: (i, 0))
    row = pl.BlockSpec((1, D), lambda i: (0, 0))
    return pl.pallas_call(
        body, name="norm_mlp_bwd_out_proj_bwd", grid=(L // T,),
        in_specs=[tile, tile, tile, tile, row, row, row, pl.BlockSpec((D, N), lambda i: (0, 0))],
        out_specs=[tile, tile, pl.BlockSpec((T, A), lambda i: (i, 0)), pl.BlockSpec((T, N - A), lambda i: (i, 0)), row, row, row, row],
        out_shape=[jax.ShapeDtypeStruct((L, D), F32), jax.ShapeDtypeStruct((L, D), BF16), jax.ShapeDtypeStruct((L, A), BF16),
                   jax.ShapeDtypeStruct((L, N - A), F32)] + [jax.ShapeDtypeStruct((1, D), F32)] * 4,
        compiler_params=pltpu.CompilerParams(dimension_semantics=("arbitrary",), vmem_limit_bytes=FUSED_VMEM_LIMIT),
    )(x, dh, dres, o, w, sc, g, w_out_t)


def _out_proj_epilogue(acc, xr, g, w, sc, sh):
    x1 = xr + g * acc
    return acc, x1, _modulated_norm(x1, w, sc, sh)


def _mlp_down_epilogue(acc, x1, tgt, g, wf):
    D = acc.shape[1]
    x2 = x1 + g * acc
    r = _inv_rms(x2)
    xh = x2 * r
    err = xh * wf - tgt
    loss = 0.5 * jnp.sum(jnp.mean(err * err, axis=-1, keepdims=True), axis=0, keepdims=True)
    dy = err * (1.0 / D)
    dxh = dy * wf
    dx = r * (dxh - xh * jnp.mean(dxh * xh, axis=-1, keepdims=True))
    col = lambda v: jnp.sum(v, axis=0, keepdims=True)
    return dx, g * dx, col(dy * xh), col(dx * acc), jnp.broadcast_to(loss, (1, D))


def _heads(ref, first, n):
    return jnp.concatenate([ref[:, (first + g) * HEAD_DIM:(first + g + 1) * HEAD_DIM] for g in range(n)], axis=0)


def _edge_variants(masks):
    return jnp.asarray(np.stack([np.where(masks(first, last), 0.0, NEG_INF).astype(np.float32)
                                 for last in (False, True) for first in (False, True)]))


def _attn_bias(C):
    kj = np.arange(3 * BLOCK + C)[:, None]
    qi = (np.arange(GQA * BLOCK) % BLOCK)[None, :]

    def masks(first, last):
        window = (kj >= qi) & (kj <= qi + 2 * BLOCK) & (kj >= (BLOCK if first else 0)) & (kj < (2 * BLOCK if last else 3 * BLOCK))
        return window | (kj >= 3 * BLOCK)

    return _edge_variants(masks)


def _attn_bias_keys():
    kj = np.arange(BLOCK)[:, None]
    col = np.arange(3 * GQA * BLOCK)[None, :]
    part, qi = col // (GQA * BLOCK), col % BLOCK

    def masks(first, last):
        return ((part == 0) & (kj <= qi) & (not first)) | (part == 1) | ((part == 2) & (kj >= qi) & (not last))

    return _edge_variants(masks)


def _edge_index(n, nb):
    return (n == 0).astype(I32) + 2 * (n == nb - 1).astype(I32)


def _head_rows(ref, hk):
    return jnp.concatenate([ref[hk * GQA + g:hk * GQA + g + 1, :] for g in range(GQA)], axis=1)


def _rows_to_heads(rows_by_kv_head):
    return jnp.concatenate([r[:, g * BLOCK:(g + 1) * BLOCK] for r in rows_by_kv_head for g in range(GQA)], axis=0)


def _queries_to_rows(t):
    return jnp.concatenate([t[:, g * BLOCK:(g + 1) * BLOCK].T for g in range(GQA)], axis=1)


def _attn_specs(L, A, KV, C, vcol):
    nb = L // BLOCK
    kcol = A // KV
    prev = lambda n: jnp.maximum(n - 1, 0)
    nxt = lambda n: jnp.minimum(n + 1, nb - 1)
    q_spec = pl.BlockSpec((BLOCK, A), lambda n: (n, 0))
    k_specs = [pl.BlockSpec((BLOCK, KV), lambda n: (prev(n), kcol)), pl.BlockSpec((BLOCK, KV), lambda n: (n, kcol)),
               pl.BlockSpec((BLOCK, KV), lambda n: (nxt(n), kcol))]
    v_specs = [pl.BlockSpec((BLOCK, KV), lambda n: (prev(n), vcol)), pl.BlockSpec((BLOCK, KV), lambda n: (n, vcol)),
               pl.BlockSpec((BLOCK, KV), lambda n: (nxt(n), vcol))]
    kvc_spec = pl.BlockSpec((C, 2 * KV), lambda n: (0, 0))
    return q_spec, k_specs, v_specs, kvc_spec


def _keys_values(hk, k_refs, v_refs, kvc_ref, KV):
    sl = slice(hk * HEAD_DIM, (hk + 1) * HEAD_DIM)
    keys = jnp.concatenate([r[:, sl] for r in k_refs] + [kvc_ref[:, sl]], axis=0)
    vals = jnp.concatenate([r[:, sl].astype(BF16) for r in v_refs] + [kvc_ref[:, KV + hk * HEAD_DIM:KV + (hk + 1) * HEAD_DIM]], axis=0)
    return keys, vals


def _sink_row(sink_ref, hk):
    return jnp.concatenate([jnp.full((1, BLOCK), sink_ref[0, hk * GQA + g], F32) for g in range(GQA)], axis=1)


def _attn_fwd(qk, uv, kvc, sink, A, KV, P):
    L = qk.shape[0]
    C = kvc.shape[0]
    nkv = KV // HEAD_DIM
    H = nkv * GQA
    scale = HEAD_DIM ** -0.5

    def body(sink_ref, q_ref, kp_ref, kc_ref, kn_ref, vp_ref, vc_ref, vn_ref, kvc_ref, bias_ref, o_ref, lse_ref):
        bias = bias_ref[_edge_index(pl.program_id(0), L // BLOCK)]
        lse_rows = []
        for hk in range(nkv):
            keys, vals = _keys_values(hk, (kp_ref, kc_ref, kn_ref), (vp_ref, vc_ref, vn_ref), kvc_ref, KV)
            qs = _heads(q_ref, hk * GQA, GQA) * scale
            s = _dot_nt(keys, qs) + bias
            sk = _sink_row(sink_ref, hk)
            m = jnp.maximum(jnp.max(s, axis=0, keepdims=True), sk)
            p = jnp.exp(s - m)
            den = jnp.sum(p, axis=0, keepdims=True) + jnp.exp(sk - m)
            o = _dot_tn(vals, p.astype(BF16)) * (1.0 / den)
            lse_rows.append(m + jnp.log(den))
            o_ref[:, hk * GQA * HEAD_DIM:(hk + 1) * GQA * HEAD_DIM] = _queries_to_rows(o).astype(BF16)
        lse_ref[...] = _rows_to_heads(lse_rows)

    q_spec, k_specs, v_specs, kvc_spec = _attn_specs(L, A, KV, C, P // KV)
    bias = _attn_bias(C)
    return pl.pallas_call(
        body, name="attn_fwd", grid=(L // BLOCK,),
        in_specs=[pl.BlockSpec(memory_space=pltpu.SMEM), q_spec] + k_specs + v_specs
                 + [kvc_spec, pl.BlockSpec(bias.shape, lambda n: (0, 0, 0))],
        out_specs=[pl.BlockSpec((BLOCK, A), lambda n: (n, 0)), pl.BlockSpec((H, BLOCK), lambda n: (0, n))],
        out_shape=[jax.ShapeDtypeStruct((L, A + P), BF16), jax.ShapeDtypeStruct((H, L), F32)],
        compiler_params=_cparams("parallel"),
    )(sink, qk, qk, qk, qk, uv, uv, uv, kvc, bias)


def _attn_bwd_dq(qk, uv, kvc, sink, dap, lse_t, cos, sin, A, KV, P):
    L = qk.shape[0]
    C = kvc.shape[0]
    nkv = KV // HEAD_DIM
    H = nkv * GQA
    scale = HEAD_DIM ** -0.5
    W = 3 * BLOCK

    def body(sink_ref, q_ref, kp_ref, kc_ref, kn_ref, vp_ref, vc_ref, vn_ref, kvc_ref, do_ref, lse_ref, cos_ref, sin_ref, bias_ref,
             dq_ref, rd_ref, ds_ref, dkvc_ref):
        n = pl.program_id(0)

        @pl.when(n == 0)
        def _():
            dkvc_ref[...] = jnp.zeros_like(dkvc_ref)

        bias = bias_ref[_edge_index(n, L // BLOCK)]
        rd_rows, dsink_rows, dq_parts = [], [], []
        for hk in range(nkv):
            sl = slice(hk * HEAD_DIM, (hk + 1) * HEAD_DIM)
            keys, vals = _keys_values(hk, (kp_ref, kc_ref, kn_ref), (vp_ref, vc_ref, vn_ref), kvc_ref, KV)
            qs = _heads(q_ref, hk * GQA, GQA) * scale
            dos = _heads(do_ref, hk * GQA, GQA).astype(BF16)
            lse = _head_rows(lse_ref, hk)
            p = jnp.exp(_dot_nt(keys, qs) + bias - lse)
            dp = _dot_nt(vals, dos)
            rd = jnp.sum(p * dp, axis=0, keepdims=True)
            ds = (p * (dp - rd)).astype(BF16)
            dq_parts.append(_queries_to_rows(_dot_tn(keys, ds) * scale))
            dkvc_ref[:, sl] += _dot(ds[W:, :], qs)
            dkvc_ref[:, KV + hk * HEAD_DIM:KV + (hk + 1) * HEAD_DIM] += _dot(p[W:, :].astype(BF16), dos)
            rd_rows.append(rd)
            dsink_rows.append(-(jnp.exp(_sink_row(sink_ref, hk) - lse) * rd))
        rd_ref[...] = _rows_to_heads(rd_rows)
        ds_ref[...] = _rows_to_heads(dsink_rows)
        dq = dq_parts[0] if nkv == 1 else jnp.concatenate(dq_parts, axis=1)
        dq_ref[...] = _rope(dq, cos_ref[...], -sin_ref[...]).astype(BF16)

    q_spec, k_specs, v_specs, kvc_spec = _attn_specs(L, A, KV, C, P // KV)
    blk = lambda w: pl.BlockSpec((BLOCK, w), lambda n: (n, 0))
    per_head = pl.BlockSpec((H, BLOCK), lambda n: (0, n))
    bias = _attn_bias(C)
    return pl.pallas_call(
        body, name="attn_bwd_dq", grid=(L // BLOCK,),
        in_specs=[pl.BlockSpec(memory_space=pltpu.SMEM), q_spec] + k_specs + v_specs
                 + [kvc_spec, blk(A), per_head, blk(LANES), blk(LANES), pl.BlockSpec(bias.shape, lambda n: (0, 0, 0))],
        out_specs=[blk(A), per_head, per_head, pl.BlockSpec((C, 2 * KV), lambda n: (0, 0))],
        out_shape=[jax.ShapeDtypeStruct((L, A), BF16), jax.ShapeDtypeStruct((H, L), F32), jax.ShapeDtypeStruct((H, L), F32),
                   jax.ShapeDtypeStruct((C, 2 * KV), F32)],
        compiler_params=_cparams("arbitrary"),
    )(sink, qk, qk, qk, qk, uv, uv, uv, kvc, dap, lse_t, cos, sin, bias)


def _attn_bwd_dkv(qk, uv, dap, lse_t, rd_t, cos, sin, A, KV, P):
    L = qk.shape[0]
    nb = L // BLOCK
    nkv = KV // HEAD_DIM
    H = nkv * GQA
    scale = HEAD_DIM ** -0.5

    def body(k_ref, v_ref, qp_ref, qc_ref, qn_ref, dop_ref, doc_ref, don_ref, lsep_ref, lsec_ref, lsen_ref,
             rdp_ref, rdc_ref, rdn_ref, cos_ref, sin_ref, bias_ref, dk_ref, dv_ref):
        bias = bias_ref[_edge_index(pl.program_id(0), nb)]
        dk_parts, dv_parts = [], []
        for hk in range(nkv):
            sl = slice(hk * HEAD_DIM, (hk + 1) * HEAD_DIM)
            km = k_ref[:, sl]
            vm = v_ref[:, sl].astype(BF16)
            qs = jnp.concatenate([_heads(q, hk * GQA, GQA) for q in (qp_ref, qc_ref, qn_ref)], axis=0) * scale
            dos = jnp.concatenate([_heads(d, hk * GQA, GQA) for d in (dop_ref, doc_ref, don_ref)], axis=0).astype(BF16)
            rows = [slice(hk * GQA + g, hk * GQA + g + 1) for g in range(GQA)]
            lse = jnp.concatenate([t[r, :] for t in (lsep_ref, lsec_ref, lsen_ref) for r in rows], axis=1)
            rdv = jnp.concatenate([t[r, :] for t in (rdp_ref, rdc_ref, rdn_ref) for r in rows], axis=1)
            p = jnp.exp(_dot_nt(km, qs) + bias - lse)
            ds = (p * (_dot_nt(vm, dos) - rdv)).astype(BF16)
            dk_parts.append(_dot(ds, qs))
            dv_parts.append(_dot(p.astype(BF16), dos))
        dk = dk_parts[0] if nkv == 1 else jnp.concatenate(dk_parts, axis=1)
        dv = dv_parts[0] if nkv == 1 else jnp.concatenate(dv_parts, axis=1)
        dk_ref[...] = _rope(dk, cos_ref[...], -sin_ref[...]).astype(BF16)
        dv_ref[...] = dv.astype(BF16)

    prev = lambda m: jnp.maximum(m - 1, 0)
    nxt = lambda m: jnp.minimum(m + 1, nb - 1)
    three = lambda w: [pl.BlockSpec((BLOCK, w), lambda m: (prev(m), 0)), pl.BlockSpec((BLOCK, w), lambda m: (m, 0)),
                       pl.BlockSpec((BLOCK, w), lambda m: (nxt(m), 0))]
    three_t = [pl.BlockSpec((H, BLOCK), lambda m: (0, prev(m))), pl.BlockSpec((H, BLOCK), lambda m: (0, m)),
               pl.BlockSpec((H, BLOCK), lambda m: (0, nxt(m)))]
    blk = lambda w: pl.BlockSpec((BLOCK, w), lambda m: (m, 0))
    bias = _attn_bias_keys()
    return pl.pallas_call(
        body, name="attn_bwd_dkv", grid=(nb,),
        in_specs=[pl.BlockSpec((BLOCK, KV), lambda m: (m, A // KV)), pl.BlockSpec((BLOCK, KV), lambda m: (m, P // KV))]
                 + three(A) + three(A) + three_t + three_t + [blk(LANES), blk(LANES), pl.BlockSpec(bias.shape, lambda m: (0, 0, 0))],
        out_specs=[blk(KV), blk(KV)],
        out_shape=[jax.ShapeDtypeStruct((L, KV), BF16), jax.ShapeDtypeStruct((L, KV), BF16)],
        compiler_params=_cparams("parallel"),
    )(qk, uv, qk, qk, qk, dap, dap, dap, lse_t, lse_t, lse_t, rd_t, rd_t, rd_t, cos, sin, bias)


def _halo_specs(T, L, W, col):
    per = T // HALO
    return [pl.BlockSpec((HALO, W), lambda i: (jnp.maximum(i * per - 1, 0), col)),
            pl.BlockSpec((T, W), lambda i: (i, col)),
            pl.BlockSpec((HALO, W), lambda i: (jnp.minimum((i + 1) * per, L // HALO - 1), col))]


def _fill_halo_buf(buf, prev_ref, cur_ref, next_ref, i, nt, T):
    buf[0:HALO, :] = jnp.where(i > 0, prev_ref[...], 0.0)
    buf[HALO:HALO + T, :] = cur_ref[...]
    buf[HALO + T:2 * HALO + T, :] = jnp.where(i < nt - 1, next_ref[...], 0.0)


def _zero_margins(lv):
    rows = lv.shape[0]
    lv[0:HALO, :] = jnp.zeros((HALO, lv.shape[1]), F32)
    lv[rows - HALO:rows, :] = jnp.zeros((HALO, lv.shape[1]), F32)


def _window_sums(lv, x, w, first):
    n = x.shape[0]
    lv[HALO:HALO + n, :] = x
    cur = x + lv[pl.ds(HALO + first, n), :]
    span = 1
    while 2 * span < w:
        lv[HALO:HALO + n, :] = cur
        cur = lv[pl.ds(HALO - span, n), :] + lv[pl.ds(HALO + span, n), :]
        span *= 2
    return cur


def _counts(t, w, L):
    lo = jnp.clip(t - w // 2, 0, L)
    hi = jnp.clip(t - w // 2 + w, 0, L)
    return jnp.maximum(hi - lo, 1).astype(F32)


def _pool_fwd(u, pw, scale, mix):
    L, P = u.shape[0], scale.shape[1]
    gd = P // POOL_GROUPS
    T = _tile(L, 256, 8)
    nt = L // T
    assert (mix.shape[1] - P) % P == 0
    mix_col = mix.shape[1] // P - 1

    def body(up_ref, uc_ref, un_ref, pw_ref, sc_ref, mix_ref, out_ref, pooled_ref, buf, lv):
        i = pl.program_id(0)
        _fill_halo_buf(buf, up_ref, uc_ref, un_ref, i, nt, T)
        _zero_margins(lv)
        t = i * T + lax.broadcasted_iota(I32, (T, 1), 0)
        for g, w in enumerate(POOL_WINDOWS):
            cols = slice(g * gd, (g + 1) * gd)
            acc = _window_sums(lv, buf[:, cols], w, -1)[HALO:HALO + T]
            pooled = (acc / _counts(t, w, L) - uc_ref[:, cols]).astype(BF16)
            pooled_ref[:, cols] = pooled
            out_ref[:, cols] = (_dot(pooled, pw_ref[g]) * sc_ref[:, cols]).astype(BF16)

    return pl.pallas_call(
        body, name="pool_fwd", grid=(nt,),
        in_specs=_halo_specs(T, L, P, 0) + [pl.BlockSpec((POOL_GROUPS, gd, gd), lambda i: (0, 0, 0)), pl.BlockSpec((1, P), lambda i: (0, 0)),
                                            pl.BlockSpec(memory_space=pl.ANY)],
        out_specs=[pl.BlockSpec((T, P), lambda i: (i, mix_col)), pl.BlockSpec((T, P), lambda i: (i, 0))],
        out_shape=[jax.ShapeDtypeStruct(mix.shape, BF16), jax.ShapeDtypeStruct((L, P), BF16)],
        scratch_shapes=[pltpu.VMEM((T + 2 * HALO, P), F32), pltpu.VMEM((T + 4 * HALO, gd), F32)],
        input_output_aliases={5: 0},
        compiler_params=_cparams("parallel"),
    )(u, u, u, pw, scale, mix)


def _pool_bwd_mix(d_pool, pooled, pw, scale):
    L, P = pooled.shape
    gd = P // POOL_GROUPS
    T = _tile(L, 256, 8)

    def body(dp_ref, pooled_ref, pw_ref, sc_ref, dpooled_ref, dpw_ref, dsc_ref):
        i = pl.program_id(0)

        @pl.when(i == 0)
        def _():
            dpw_ref[...] = jnp.zeros_like(dpw_ref)
            dsc_ref[...] = jnp.zeros_like(dsc_ref)

        for g in range(POOL_GROUPS):
            cols = slice(g * gd, (g + 1) * gd)
            pb = pooled_ref[:, cols]
            dp = dp_ref[:, cols]
            dsc_ref[:, cols] += jnp.sum(dp * _dot(pb, pw_ref[g]), axis=0, keepdims=True)
            dm = (dp * sc_ref[:, cols]).astype(BF16)
            dpw_ref[g] += _dot_tn(pb, dm)
            dpooled_ref[:, cols] = _dot_nt(dm, pw_ref[g])

    return pl.pallas_call(
        body, name="pool_bwd_mix", grid=(L // T,),
        in_specs=[pl.BlockSpec((T, P), lambda i: (i, 0)), pl.BlockSpec((T, P), lambda i: (i, 0)),
                  pl.BlockSpec((POOL_GROUPS, gd, gd), lambda i: (0, 0, 0)), pl.BlockSpec((1, P), lambda i: (0, 0))],
        out_specs=[pl.BlockSpec((T, P), lambda i: (i, 0)), pl.BlockSpec((POOL_GROUPS, gd, gd), lambda i: (0, 0, 0)),
                   pl.BlockSpec((1, P), lambda i: (0, 0))],
        out_shape=[jax.ShapeDtypeStruct((L, P), F32), jax.ShapeDtypeStruct((POOL_GROUPS, gd, gd), F32), jax.ShapeDtypeStruct((1, P), F32)],
        compiler_params=_cparams("arbitrary"),
    )(d_pool, pooled, pw, scale)


def _pool_bwd_window(dpooled):
    L, P = dpooled.shape
    gd = P // POOL_GROUPS
    T = _tile(L, 256, 8)
    nt = L // T

    def body(dp_ref, dc_ref, dn_ref, du_ref, buf, lv):
        i = pl.program_id(0)
        _fill_halo_buf(buf, dp_ref, dc_ref, dn_ref, i, nt, T)
        _zero_margins(lv)
        t = i * T - HALO + lax.broadcasted_iota(I32, (T + 2 * HALO, 1), 0)
        for g, w in enumerate(POOL_WINDOWS):
            cols = slice(g * gd, (g + 1) * gd)
            acc = _window_sums(lv, buf[:, cols] / _counts(t, w, L), w, 1)[HALO:HALO + T]
            du_ref[:, cols] = (acc - dc_ref[:, cols]).astype(BF16)

    return pl.pallas_call(
        body, name="pool_bwd_window", grid=(nt,),
        in_specs=_halo_specs(T, L, P, 0),
        out_specs=pl.BlockSpec((T, P), lambda i: (i, 0)),
        out_shape=jax.ShapeDtypeStruct((L, P), BF16),
        scratch_shapes=[pltpu.VMEM((T + 2 * HALO, P), F32), pltpu.VMEM((T + 4 * HALO, gd), F32)],
        compiler_params=_cparams("parallel"),
    )(dpooled, dpooled, dpooled)


def _sum_rows(name, a):
    R, N = a.shape

    def body(a_ref, o_ref):
        if R <= 16:
            acc = a_ref[0:1, :]
            for r in range(1, R):
                acc = acc + a_ref[r:r + 1, :]
        else:
            acc = jnp.sum(a_ref[...], axis=0, keepdims=True)
        o_ref[...] = acc

    return pl.pallas_call(body, name=name, out_shape=jax.ShapeDtypeStruct((1, N), F32))(a)


def _sum_lanes(name, a):
    def body(a_ref, o_ref):
        o_ref[...] = jnp.sum(a_ref[...], axis=1, keepdims=True)

    return pl.pallas_call(body, name=name, out_shape=jax.ShapeDtypeStruct((a.shape[0], 1), F32))(a)


def _silu_grad_mul(cv, g):
    def body(c_ref, g_ref, o_ref):
        cvv = c_ref[...]
        s = 1.0 / (1.0 + jnp.exp(-cvv))
        o_ref[...] = g_ref[...] * (s * (1.0 + cvv * (1.0 - s)))

    return pl.pallas_call(body, name="silu_grad_mul", out_shape=jax.ShapeDtypeStruct(cv.shape, F32))(cv, g)


def _adamw(name, w, g, m, v):
    R, C = w.shape
    parts = g.ndim == 3
    n_parts = g.shape[0] if parts else 1
    T = _tile(R, max(8, 262144 // C), 8)

    def body(w_ref, g_ref, m_ref, v_ref, go_ref, d_ref, mo_ref, vo_ref):
        if parts:
            gv = g_ref[0].astype(F32)
            for p in range(1, n_parts):
                gv = gv + g_ref[p].astype(F32)
        else:
            gv = g_ref[...]
        mn = ADAM_B1 * m_ref[...] + (1.0 - ADAM_B1) * gv
        vn = ADAM_B2 * v_ref[...] + (1.0 - ADAM_B2) * (gv * gv)
        m_hat = mn / (1.0 - ADAM_B1 ** ADAM_STEP)
        v_hat = vn / (1.0 - ADAM_B2 ** ADAM_STEP)
        go_ref[...] = gv
        d_ref[...] = -ADAM_LR * (m_hat / (jnp.sqrt(v_hat) + ADAM_EPS) + ADAM_WD * w_ref[...])
        mo_ref[...] = mn
        vo_ref[...] = vn

    tile = pl.BlockSpec((T, C), lambda i: (i, 0))
    g_spec = pl.BlockSpec((n_parts, T, C), lambda i: (0, i, 0)) if parts else tile
    return pl.pallas_call(
        body, name=name, grid=(R // T,),
        in_specs=[tile, g_spec, tile, tile], out_specs=[tile] * 4,
        out_shape=[jax.ShapeDtypeStruct((R, C), F32)] * 4,
        compiler_params=_cparams("parallel"),
    )(w, g, m, v)


def _dev_index(px, py, pc):
    return 4 * px + 2 * py + pc


def _all_gather(name, arrs):
    n = len(arrs)

    def body(*refs):
        ins, outs = refs[:n], refs[n:2 * n]
        send_sems, recv_sems, local_sems = refs[2 * n:]
        x, y, c = lax.axis_index("x"), lax.axis_index("y"), lax.axis_index("c")
        me, sibling = (x, y, c), (x, y, 1 - c)
        chips = [(1 - x, y), (x, 1 - y), (1 - x, 1 - y)]

        def copy(a, k, block, to, src=None):
            slot = outs[a].at[_dev_index(*block)]
            return pltpu.make_async_remote_copy(
                src_ref=slot if src is None else src, dst_ref=slot, send_sem=send_sems.at[a, k], recv_sem=recv_sems.at[a, k],
                device_id=to, device_id_type=MESH)

        mine = [pltpu.make_async_copy(ins[a], outs[a].at[_dev_index(*me)], local_sems.at[a]) for a in range(n)]
        for cp in mine:
            cp.start()
        first = []
        for a in range(n):
            first.append(copy(a, 0, me, sibling, src=ins[a]))
            first += [copy(a, 1 + j, me, (*chip, c), src=ins[a]) for j, chip in enumerate(chips)]
        for cp in first:
            cp.start()
        passed = []
        for j, chip in enumerate(chips):
            for a in range(n):
                copy(a, 1 + j, (*chip, c), me).wait_recv()
                fwd = copy(a, 4 + j, (*chip, c), sibling)
                fwd.start()
                passed.append(fwd)
        for a in range(n):
            copy(a, 0, sibling, me).wait_recv()
            for j, chip in enumerate(chips):
                copy(a, 4 + j, (*chip, 1 - c), me).wait_recv()
        for cp in first + passed:
            cp.wait_send()
        for cp in mine:
            cp.wait()

    return pl.pallas_call(
        body, name=name,
        in_specs=[HBM] * n, out_specs=[HBM] * n,
        out_shape=[jax.ShapeDtypeStruct((N_DEV, *a.shape), a.dtype) for a in arrs],
        scratch_shapes=[pltpu.SemaphoreType.DMA((n, N_DEV - 1)), pltpu.SemaphoreType.DMA((n, N_DEV - 1)), pltpu.SemaphoreType.DMA((n,))],
    )(*arrs)


N_COPIES = {"all_to_all": N_DEV - 1, "gather_all": N_DEV - 1, "gather_chips": 4, "forward": 3}


def _exchange_copies(kind, src_ref, land_ref, send_sems, recv_sems, sending):
    x, y, c = lax.axis_index("x"), lax.axis_index("y"), lax.axis_index("c")
    me = _dev_index(x, y, c)
    others = [(1 - x, y), (x, 1 - y), (1 - x, 1 - y)]
    if kind in ("all_to_all", "gather_all"):
        flips = [(dx, dy, dc) for dx in (0, 1) for dy in (0, 1) for dc in (0, 1)][1:]
        peers = [(1 - x if dx else x, 1 - y if dy else y, 1 - c if dc else c) for dx, dy, dc in flips]
        plan = [(p, src_ref if kind == "gather_all" else src_ref.at[_dev_index(*p)], me if sending else _dev_index(*p)) for p in peers]
    elif kind == "gather_chips":
        peers = [(*o, c) for o in others] + [(x, y, 1 - c)]
        plan = [(p, src_ref, me if sending else _dev_index(*p)) for p in peers]
    else:
        plan = [((x, y, 1 - c), land_ref.at[_dev_index(*o, c)], _dev_index(*o, c if sending else 1 - c)) for o in others]
    return [pltpu.make_async_remote_copy(src_ref=src, dst_ref=land_ref.at[slot], send_sem=send_sems.at[k], recv_sem=recv_sems.at[k],
                                         device_id=peer, device_id_type=MESH)
            for k, (peer, src, slot) in enumerate(plan)]


def _exchange_start(name, kind, srcs, lands=None):
    if lands is None:
        lands = [lax.empty(s.shape if kind == "all_to_all" else (N_DEV, *s.shape), s.dtype) for s in srcs]
    n = len(lands)
    ops = ([] if srcs is None else list(srcs)) + list(lands)
    m = len(ops)

    def body(*refs):
        src_refs = [None] * n if srcs is None else refs[:n]
        land_refs = refs[m - n:m]
        send_sems, recv_sems, token = refs[m:m + n], refs[m + n:m + 2 * n], refs[-1]
        for a in range(n):
            for cp in _exchange_copies(kind, src_refs[a], land_refs[a], send_sems[a], recv_sems[a], True):
                cp.start()
        token[...] = jnp.zeros_like(token)

    sems = [pltpu.SemaphoreType.DMA((N_COPIES[kind],))] * (2 * n)
    outs = pl.pallas_call(
        body, name=name,
        out_shape=sems + [pltpu.HBM(o.shape, o.dtype) for o in ops] + [jax.ShapeDtypeStruct((8, LANES), F32)],
        in_specs=[HBM] * m,
        out_specs=[SEM] * (2 * n) + [HBM] * m + [pl.BlockSpec(memory_space=pltpu.VMEM)],
        input_output_aliases={i: 2 * n + i for i in range(m)},
        compiler_params=SIDE_EFFECT,
    )(*[pltpu.with_memory_space_constraint(o, pltpu.HBM) for o in ops])
    thru = outs[2 * n:2 * n + m]
    return outs[:n], outs[n:2 * n], (None if srcs is None else thru[:n]), thru[m - n:], outs[-1]


def _exchange_wait(name, kind, send_sems, recv_sems, srcs, lands, after):
    n = len(lands)
    ops = ([] if srcs is None else list(srcs)) + list(lands)
    m = len(ops)

    def body(*refs):
        src_refs = [None] * n if srcs is None else refs[:n]
        land_refs = refs[m - n:m]
        send_refs, recv_refs = refs[m:m + n], refs[m + n:m + 2 * n]
        for a in range(n):
            for cp in _exchange_copies(kind, src_refs[a], land_refs[a], send_refs[a], recv_refs[a], False):
                cp.wait_send()
                cp.wait_recv()

    outs = pl.pallas_call(
        body, name=name,
        out_shape=[pltpu.HBM(o.shape, o.dtype) for o in ops],
        in_specs=[HBM] * m + [SEM] * (2 * n) + [pl.BlockSpec(memory_space=pl.ANY)],
        out_specs=[HBM] * m,
        input_output_aliases={i: i for i in range(m)},
        compiler_params=SIDE_EFFECT,
    )(*ops, *send_sems, *recv_sems, after)
    return (None if srcs is None else outs[:n]), outs[m - n:]


def _with_own(land, own, me):
    return lax.dynamic_update_slice_in_dim(land, own, me, 0)


def _shards_to_cols(g):
    return jnp.transpose(g, (1, 0, 2)).reshape(g.shape[1], N_DEV * g.shape[2])


def _cols_to_shards(a):
    R, Ctot = a.shape
    return jnp.transpose(a.reshape(R, N_DEV, Ctot // N_DEV), (1, 0, 2))


def kernel(x, c, ctx, c_ctx, norm_attn_w, norm_mlp_w, w_ada, b_ada, w_in, attn_sink, pool_w, pool_scale, w_out, w_mlp_up, w_mlp_down, final_norm_w, loss_target, m_c_ctx, m_norm_attn_w, m_norm_mlp_w, m_w_ada, m_b_ada, m_w_in, m_attn_sink, m_pool_w, m_pool_scale, m_w_out, m_w_mlp_up, m_w_mlp_down, m_final_norm_w, v_c_ctx, v_norm_attn_w, v_norm_mlp_w, v_w_ada, v_b_ada, v_w_in, v_attn_sink, v_pool_w, v_pool_scale, v_w_out, v_w_mlp_up, v_w_mlp_down, v_final_norm_w):
    _, L, D = x.shape
    H = attn_sink.shape[1]
    A = H * HEAD_DIM
    KV = A // GQA
    P = pool_scale.shape[1]
    MODW = N_MOD * D
    ws = MODW // N_DEV
    gd = P // POOL_GROUPS
    me = _dev_index(lax.axis_index("x"), lax.axis_index("y"), lax.axis_index("c"))

    x2d, ctx2d, tgt = x[0], ctx[0], loss_target[0]
    cctx_row = c_ctx.reshape(1, D)
    wf_row = final_norm_w.reshape(1, D)
    w_ada_l = w_ada[0]
    pool_w_l = pool_w[0].reshape(POOL_GROUPS * (gd // N_DEV), gd)

    (c_all,) = _all_gather("gather_cond", [c])
    cond = jnp.concatenate([c_all[:, 0, :], cctx_row, jnp.zeros((COND_ROWS - N_DEV - 1, D), F32)], axis=0)
    b_sh = lax.dynamic_slice_in_dim(b_ada, me * ws, ws, axis=1)
    (mods_sh,) = _mm("ada_mod", cond, w_ada_l, "nn", [F32], SMALL_TILES, a_pre=_silu, extras=[("n", b_sh)], epilogue=lambda acc, b: (acc + b,))
    (mods_g,) = _all_gather("gather_mods", [mods_sh])

    w_srcs = [w_in[0].astype(BF16), w_out[0].astype(BF16), pool_w_l.astype(BF16), w_mlp_up[0].astype(BF16), w_mlp_down[0].astype(BF16)]
    w_srcs, mods_g = lax.optimization_barrier((w_srcs, mods_g))
    gather_start = _exchange_start("gather_weights_start", "gather_chips", w_srcs[:4])
    w_down_src, _ = lax.optimization_barrier((w_srcs[4], gather_start[4]))
    down_start = _exchange_start("gather_w_down_start", "gather_all", [w_down_src])

    def weights(tag, started, lo, hi, after_chips, after_forward):
        gw_send, gw_recv, gw_src, gw_land, _ = started
        mine, lands = _exchange_wait(f"gather_{tag}_wait", "gather_chips", gw_send[lo:hi], gw_recv[lo:hi], gw_src[lo:hi],
                                     gw_land[lo:hi], after_chips)
        f_send, f_recv, _, f_land, f_token = _exchange_start(f"forward_{tag}_start", "forward", None, lands)
        _, lands = _exchange_wait(f"forward_{tag}_wait", "forward", f_send, f_recv, None, f_land,
                                  f_token if after_forward is None else after_forward)
        return [_with_own(l, s[None], me) for l, s in zip(lands, mine)]

    mods = _shards_to_cols(mods_g)
    mod_b = lax.dynamic_slice_in_dim(mods, me, 1, axis=0)
    sh_a, sc_a, g_a, sh_m, sc_m, g_m = [mod_b[:, i * D:(i + 1) * D] for i in range(N_MOD)]
    csh_a, csc_a = mods[N_DEV:N_DEV + 1, :D], mods[N_DEV:N_DEV + 1, D:2 * D]

    cos, sin = _rope_tables(L)
    h = _norm_fwd("norm_attn", x2d, norm_attn_w, sc_a, sh_a)
    hc = _norm_fwd("norm_attn_ctx", ctx2d, norm_attn_w, csc_a, csh_a)
    (win_g,) = weights("w_in", gather_start, 0, 1, h, None)
    W_in = _shards_to_cols(win_g)
    W_qk, W_kv = W_in[:, :A + KV], W_in[:, A:A + 2 * KV]
    W_uv = jnp.concatenate([W_in[:, A + 2 * KV:], W_in[:, A + KV:A + 2 * KV]], axis=1)
    qk, uv = _in_proj(h, W_qk, W_uv, cos, sin)
    (kvc,) = _mm("in_proj_ctx", hc, W_kv, "nn", [BF16], SMALL_TILES)
    attn, lse = _attn_fwd(qk, uv, kvc, attn_sink, A, KV, P)
    wout_g, pw_g = weights("w_out", gather_start, 1, 3, qk, attn)
    W_out = wout_g.reshape(A + P, D)
    PW = jnp.transpose(pw_g.reshape(N_DEV, POOL_GROUPS, gd // N_DEV, gd), (1, 0, 2, 3)).reshape(POOL_GROUPS, gd, gd)
    ap, pooled = _pool_fwd(uv, PW, pool_scale, attn)
    o, x1, hm = _mm("out_proj_norm", ap, W_out, "nn", [F32, F32, BF16], (256, D, D), chunk=128, epilogue=_out_proj_epilogue,
                    extras=[("mn", x2d), ("n", g_a), ("n", norm_mlp_w), ("n", sc_m), ("n", sh_m)])
    (W_up,) = weights("w_up", gather_start, 3, 4, attn, x1)
    up, act = _mm("mlp_up", hm, W_up, "nn", [F32, BF16], (1024, 1024, 2048), epilogue=lambda acc: (acc, _relu2(acc)), b_shards=True)
    down_src, down_land = _exchange_wait("gather_w_down_wait", "gather_all", down_start[0], down_start[1], down_start[2], down_start[3], up)
    wdown_g = _with_own(down_land[0], down_src[0][None], me)
    W_down = wdown_g.reshape(-1, D)
    d_x2, d_mlp, d_wf, d_gm, loss_row = _mm(
        "mlp_down_loss", act, W_down, "nn", [F32, BF16], (512, D, 1024), chunk=128, n_sums=3, vmem=FUSED_VMEM_LIMIT,
        epilogue=_mlp_down_epilogue, extras=[("mn", x1), ("mn", tgt), ("n", g_m), ("n", wf_row)])
    loss_p = loss_row[:, :1]

    (d_up,) = _mm("mlp_down_bwd_act", d_mlp, W_down, "nt", [BF16], (1024, 1024, 2048), extras=[("mn", up)],
                  epilogue=lambda acc, uu: (acc * (2.0 * jnp.maximum(uu, 0.0)),))
    (gW_down,) = _mm("mlp_down_bwd_w", act, d_mlp, "tn", [BF16], (1024, 2048, 2048), vmem=FUSED_VMEM_LIMIT)
    (gW_up_s,) = _mm("mlp_up_bwd_w", hm, d_up, "tn", [BF16], (2048, 1024, 2048), out_shards=True, vmem=FUSED_VMEM_LIMIT)
    g_mlp_srcs = [gW_up_s, gW_down.reshape(N_DEV, -1, D)]
    g_mlp = _exchange_start("grads_mlp_start", "all_to_all", g_mlp_srcs)
    (d_hm,) = _mm("mlp_up_bwd_act", d_up, W_up, "nt", [F32], (1024, D, 1024), b_shards=True)

    d_x1, d_o, d_attn, d_pool, s_sh_m, s_sc_m, s_w_nm, d_ga = _norm_bwd_out_proj_bwd(
        x1, d_hm, d_x2, o, norm_mlp_w, sc_m + g_mlp[4][0, 0], g_a, W_out.T, A)
    (gW_out,) = _mm("out_proj_bwd_w", ap, d_o, "tn", [BF16], (1024, 2048, 2048))
    d_pooled, gPW, d_pscale = _pool_bwd_mix(d_pool, pooled, PW, pool_scale)
    gpw_s = jnp.transpose(gPW.astype(BF16).reshape(POOL_GROUPS, N_DEV, gd // N_DEV, gd), (1, 0, 2, 3)).reshape(N_DEV, -1, gd)
    g_mix_srcs = [gW_out.reshape(N_DEV, (A + P) // N_DEV, D), gpw_s]
    g_mix = _exchange_start("grads_mix_start", "all_to_all", g_mix_srcs)
    lse = lse + g_mix[4][0, 0]
    d_u = _pool_bwd_window(d_pooled)
    d_q, rd, dsink_q, d_kvc = _attn_bwd_dq(qk, uv, kvc, attn_sink, d_attn, lse, cos, sin, A, KV, P)
    d_k, d_v = _attn_bwd_dkv(qk, uv, d_attn, lse, rd, cos, sin, A, KV, P)
    d_sink = _sum_lanes("sink_grad", dsink_q).reshape(1, H)
    d_p = jnp.concatenate([d_q, d_k, d_v, d_u], axis=1)
    d_kvc_b = d_kvc.astype(BF16)
    (gW_kv_ctx,) = _mm("in_proj_ctx_bwd_w", hc, d_kvc_b, "tn", [F32], SMALL_TILES)
    (d_hc,) = _mm("in_proj_ctx_bwd_act", d_kvc_b, W_kv, "nt", [F32], SMALL_TILES)
    gW_in_init = jnp.pad(gW_kv_ctx, ((0, 0), (A, P)))
    (gW_in,) = _mm("in_proj_bwd_w", h, d_p, "tn", [BF16], (1024, 1280, 2048), extras=[("mn", gW_in_init)], epilogue=lambda acc, init: (acc + init,))
    g_in_srcs = [_cols_to_shards(gW_in)]
    g_in = _exchange_start("grads_in_start", "all_to_all", g_in_srcs)
    grad_x, s_sh_a, s_sc_a, s_w_na = _mm(
        "in_proj_bwd_norm", d_p, W_in, "nt", [F32], (256, D, A + 2 * KV + P), chunk=128, n_sums=3,
        epilogue=lambda acc, xr, dres, w, sc: _modulated_norm_bwd(xr, acc, dres, w, sc),
        extras=[("mn", x2d), ("mn", d_x1), ("n", norm_attn_w), ("n", sc_a + g_in[4][0, 0])])
    s_csh, s_csc, s_w_na = _norm_bwd_sums("norm_attn_ctx_bwd", ctx2d, d_hc, norm_attn_w, csc_a, s_w_na)

    pad_l = lambda a: jnp.pad(a, ((0, 0), (0, LANES - a.shape[1])))
    d_mod_b = jnp.concatenate([s_sh_a, s_sc_a, d_ga, s_sh_m, s_sc_m, d_gm], axis=1)
    summed = jnp.concatenate([s_csh, s_csc, s_w_na, s_w_nm, d_wf, d_pscale, pad_l(d_sink), pad_l(loss_p)], axis=1)
    (small_g,) = _all_gather("gather_small", [jnp.concatenate([d_mod_b, summed], axis=1)])
    small_g = small_g[:, 0, :]
    tot = _sum_rows("small_sum", small_g[:, MODW:])
    off = [0]
    for wdt in (D, D, D, D, D, P, LANES, LANES):
        off.append(off[-1] + wdt)
    seg = lambda i: tot[:, off[i]:off[i + 1]]
    g_norm_attn, g_norm_mlp, g_final, g_pscale = seg(2), seg(3), seg(4), seg(5)
    g_sink, loss = seg(6)[:, :H], seg(7)[0, 0]
    d_mod_ctx = jnp.concatenate([seg(0), seg(1), jnp.zeros((1, MODW - 2 * D), F32)], axis=1)
    d_mod = jnp.concatenate([small_g[:, :MODW], d_mod_ctx, jnp.zeros((COND_ROWS - N_DEV - 1, MODW), F32)], axis=0)
    g_b_ada = _sum_rows("b_ada_grad", d_mod[:N_DEV + 1])
    d_mod_sh = lax.dynamic_slice_in_dim(d_mod, me * ws, ws, axis=1)
    (g_w_ada,) = _mm("ada_bwd_w", cond, d_mod_sh, "tn", [F32], SMALL_TILES, a_pre=_silu)
    (d_cond_p,) = _mm("ada_bwd_cond", d_mod_sh, w_ada_l, "nt", [F32], SMALL_TILES)
    (d_cctx_g,) = _all_gather("gather_cctx", [d_cond_p[N_DEV:N_DEV + 1]])
    g_c_ctx = _silu_grad_mul(cctx_row, _sum_rows("cctx_sum", d_cctx_g[:, 0, :]))

    def arrived(name, started):
        srcs, lands = _exchange_wait(name, "all_to_all", started[0], started[1], started[2], started[3], g_c_ctx)
        return [_with_own(l, lax.dynamic_index_in_dim(s, me, 0, keepdims=True), me) for l, s in zip(lands, srcs)]

    r_up, r_down = arrived("grads_mlp_wait", g_mlp)
    r_out, r_pw = arrived("grads_mix_wait", g_mix)
    (r_in,) = arrived("grads_in_wait", g_in)

    results = {
        "c_ctx": _adamw("adam_c_ctx", cctx_row, g_c_ctx, m_c_ctx.reshape(1, D), v_c_ctx.reshape(1, D)),
        "norm_attn_w": _adamw("adam_norm_attn", norm_attn_w, g_norm_attn, m_norm_attn_w, v_norm_attn_w),
        "norm_mlp_w": _adamw("adam_norm_mlp", norm_mlp_w, g_norm_mlp, m_norm_mlp_w, v_norm_mlp_w),
        "w_ada": _adamw("adam_w_ada", w_ada_l, g_w_ada, m_w_ada[0], v_w_ada[0]),
        "b_ada": _adamw("adam_b_ada", b_ada, g_b_ada, m_b_ada, v_b_ada),
        "w_in": _adamw("adam_w_in", w_in[0], r_in, m_w_in[0], v_w_in[0]),
        "attn_sink": _adamw("adam_sink", attn_sink, g_sink, m_attn_sink, v_attn_sink),
        "pool_w": _adamw("adam_pool_w", pool_w_l, r_pw, m_pool_w[0].reshape(pool_w_l.shape), v_pool_w[0].reshape(pool_w_l.shape)),
        "pool_scale": _adamw("adam_pool_scale", pool_scale, g_pscale, m_pool_scale, v_pool_scale),
        "w_out": _adamw("adam_w_out", w_out[0], r_out, m_w_out[0], v_w_out[0]),
        "w_mlp_up": _adamw("adam_w_up", w_mlp_up[0], r_up, m_w_mlp_up[0], v_w_mlp_up[0]),
        "w_mlp_down": _adamw("adam_w_down", w_mlp_down[0], r_down, m_w_mlp_down[0], v_w_mlp_down[0]),
        "final_norm_w": _adamw("adam_final_norm", wf_row, g_final, m_final_norm_w.reshape(1, D), v_final_norm_w.reshape(1, D)),
    }
    shapes = {"c_ctx": c_ctx.shape, "norm_attn_w": norm_attn_w.shape, "norm_mlp_w": norm_mlp_w.shape, "w_ada": w_ada.shape,
              "b_ada": b_ada.shape, "w_in": w_in.shape, "attn_sink": attn_sink.shape, "pool_w": pool_w.shape,
              "pool_scale": pool_scale.shape, "w_out": w_out.shape, "w_mlp_up": w_mlp_up.shape, "w_mlp_down": w_mlp_down.shape,
              "final_norm_w": final_norm_w.shape}
    outs = [loss, grad_x.reshape(x.shape)]
    for part in range(4):
        outs += [results[name][part].reshape(shape) for name, shape in shapes.items()]
    return tuple(outs)
```
